```python
import jax, jax.numpy as jnp
from jax import lax
import numpy as np

D_MODEL = 1024
BATCH = 8
SEQ = 2048
DEPTH = 4

N_MIXERS = 2
N_RG = (DEPTH + 1) // 2
N_HG = DEPTH // 2
LRU_WIDTH = D_MODEL
RG_HEADS = 8
RG_BLOCK = LRU_WIDTH // RG_HEADS
CONV_WIDTH = 4
LRU_C = 8.0
HG_HEADS = 8
HG_DK = D_MODEL // HG_HEADS
HG_DV = D_MODEL // HG_HEADS
HG_CHUNK = 64
D_FF = 4 * D_MODEL
NORM_EPS = 1e-6
GNORM_EPS = 1e-5

kernel_name = "hybrid_rglru_hgrn2_adaln_trunk"


def rms_norm(x, gain, eps=NORM_EPS):
    xf = x.astype(jnp.float32)
    y = xf * lax.rsqrt(jnp.mean(xf * xf, axis=-1, keepdims=True) + eps)
    return (y * gain.astype(jnp.float32)).astype(x.dtype)


def causal_depthwise_conv(x, w, b):
    T = x.shape[1]
    xp = jnp.pad(x, ((0, 0), (CONV_WIDTH - 1, 0), (0, 0)))
    y = b
    for tap in range(CONV_WIDTH):
        y = y + xp[:, tap:tap + T, :] * w[tap]
    return y


def rg_lru(x, w_a, b_a, w_x, b_x, lam):
    B, T, _ = x.shape
    xh = x.reshape(B, T, RG_HEADS, RG_BLOCK)
    r = jax.nn.sigmoid(jnp.einsum('bthi,hij->bthj', xh, w_a).reshape(B, T, LRU_WIDTH) + b_a)
    i = jax.nn.sigmoid(jnp.einsum('bthi,hij->bthj', xh, w_x).reshape(B, T, LRU_WIDTH) + b_x)
    log_a = -LRU_C * r.astype(jnp.float32) * jax.nn.softplus(-lam.astype(jnp.float32))
    a = jnp.exp(log_a)
    mult = jnp.sqrt(-jnp.expm1(2.0 * log_a))
    mult = jnp.where(jnp.arange(T)[None, :, None] == 0, 1.0, mult)
    u = mult * (i * x).astype(jnp.float32)

    def combine(left, right):
        a1, b1 = left
        a2, b2 = right
        return a1 * a2, a2 * b1 + b2

    _, h = lax.associative_scan(combine, (a, u), axis=1)
    return h.astype(x.dtype)


def rglru_block(h, w_in, conv_w, conv_b, w_a, b_a, w_x, b_x, lam, w_out):
    xb, yb = jnp.split(h @ w_in, 2, axis=-1)
    gate = jax.nn.gelu(yb, approximate=True)
    xb = causal_depthwise_conv(xb, conv_w, conv_b)
    hr = rg_lru(xb, w_a, b_a, w_x, b_x, lam)
    return (hr * gate) @ w_out


def hgrn2_chunk_scan(q, k, v, log_f):
    B, T, H, _ = q.shape
    N = T // HG_CHUNK

    def to_chunks(t):
        return t.astype(jnp.float32).reshape(B, N, HG_CHUNK, H, -1).transpose(1, 0, 3, 2, 4)

    qc, kc, vc, lc = to_chunks(q), to_chunks(k), to_chunks(v), to_chunks(log_f)
    bc = jnp.cumsum(lc, axis=3)
    causal = jnp.tril(jnp.ones((HG_CHUNK, HG_CHUNK), dtype=bool))

    def step(S, inp):
        q_, k_, v_, b_ = inp
        diff = b_[:, :, :, None, :] - b_[:, :, None, :, :]
        decay = jnp.exp(jnp.where(causal[:, :, None], diff, -jnp.inf))
        A = jnp.einsum('bhtsk,bhsk->bhts', q_[:, :, :, None, :] * decay, k_)
        o = A @ v_ + jnp.einsum('bhtk,bhkv->bhtv', q_ * jnp.exp(b_), S)
        b_last = b_[:, :, -1:, :]
        S = jnp.exp(b_last)[:, :, 0, :, None] * S + jnp.einsum(
            'bhsk,bhsv->bhkv', k_ * jnp.exp(b_last - b_), v_)
        return S, o

    S0 = jnp.zeros((B, H, HG_DK, HG_DV), jnp.float32)
    _, o = lax.scan(step, S0, (qc, kc, vc, bc))
    return o.transpose(1, 0, 3, 2, 4).reshape(B, T, H, HG_DV)


def hgrn2_block(h, w_in, lb, gnorm_w, w_out):
    B, T, _ = h.shape
    q, f, i, g = jnp.split(h @ w_in, 4, axis=-1)
    q = jax.nn.silu(q)
    fg = lb + (1.0 - lb) * jax.nn.sigmoid(f.astype(jnp.float32))
    log_f = jnp.log(fg)
    k = 1.0 - fg

    def heads(t):
        return t.reshape(B, T, HG_HEADS, -1)

    o = hgrn2_chunk_scan(heads(q), heads(k), heads(i), heads(log_f))
    o = rms_norm(o, gnorm_w, GNORM_EPS) * jax.nn.silu(heads(g).astype(jnp.float32))
    return (o.reshape(B, T, D_MODEL) @ w_out).astype(h.dtype)


def _fwd_setup_inputs(seed: int = 0) -> dict:
    key = jax.random.key(seed)
    ks = jax.random.split(key, 24)
    f32 = jnp.float32
    nrm = lambda k, shape, s: jax.random.normal(k, shape, f32) * s
    u = jax.random.uniform(ks[11], (N_RG, LRU_WIDTH), f32, 0.9, 0.999)
    s = u ** (1.0 / LRU_C)
    lam = jnp.log(s) - jnp.log1p(-s)
    return {
        "x": nrm(ks[0], (BATCH, SEQ, D_MODEL), 1.0),
        "c": nrm(ks[1], (BATCH, D_MODEL), 1.0),
        "mod_w": nrm(ks[2], (DEPTH, D_MODEL, 6 * D_MODEL), D_MODEL ** -0.5),
        "mod_b": nrm(ks[3], (DEPTH, 6 * D_MODEL), 0.02),
        "norm_mix": 1.0 + nrm(ks[4], (DEPTH, D_MODEL), 0.02),
        "norm_mlp": 1.0 + nrm(ks[5], (DEPTH, D_MODEL), 0.02),
        "norm_final": 1.0 + nrm(ks[6], (D_MODEL,), 0.02),
        "rg_w_in": nrm(ks[7], (N_RG, D_MODEL, 2 * LRU_WIDTH), D_MODEL ** -0.5),
        "rg_conv_w": nrm(ks[8], (N_RG, CONV_WIDTH, LRU_WIDTH), CONV_WIDTH ** -0.5),
        "rg_conv_b": nrm(ks[9], (N_RG, LRU_WIDTH), 0.02),
        "rg_w_a": nrm(ks[10], (N_RG, RG_HEADS, RG_BLOCK, RG_BLOCK), RG_BLOCK ** -0.5),
        "rg_b_a": nrm(ks[12], (N_RG, LRU_WIDTH), 0.1),
        "rg_w_x": nrm(ks[13], (N_RG, RG_HEADS, RG_BLOCK, RG_BLOCK), RG_BLOCK ** -0.5),
        "rg_b_x": nrm(ks[14], (N_RG, LRU_WIDTH), 0.1),
        "rg_lambda": lam,
        "rg_w_out": nrm(ks[15], (N_RG, LRU_WIDTH, D_MODEL), LRU_WIDTH ** -0.5),
        "hg_w_in": nrm(ks[16], (N_HG, D_MODEL, 4 * D_MODEL), D_MODEL ** -0.5),
        "hg_lower_bounds": nrm(ks[17], (DEPTH, D_MODEL), 0.1),
        "hg_gnorm": 1.0 + nrm(ks[18], (N_HG, HG_DV), 0.02),
        "hg_w_out": nrm(ks[19], (N_HG, D_MODEL, D_MODEL), D_MODEL ** -0.5),
        "mlp_w1": nrm(ks[20], (DEPTH, D_MODEL, D_FF), D_MODEL ** -0.5),
        "mlp_w2": nrm(ks[21], (DEPTH, D_FF, D_MODEL), D_FF ** -0.5),
    }


def _fwd_reference(x, c, mod_w, mod_b, norm_mix, norm_mlp, norm_final,
              rg_w_in, rg_conv_w, rg_conv_b, rg_w_a, rg_b_a, rg_w_x, rg_b_x, rg_lambda, rg_w_out,
              hg_w_in, hg_lower_bounds, hg_gnorm, hg_w_out, mlp_w1, mlp_w2):
    cs = jax.nn.silu(c)
    lb_all = jnp.cumsum(jax.nn.softmax(hg_lower_bounds.astype(jnp.float32), axis=0), axis=0)
    lb_all = lb_all - lb_all[0]
    for layer in range(DEPTH):
        mod = cs @ mod_w[layer] + mod_b[layer]
        sh_t, sc_t, g_t, sh_c, sc_c, g_c = [m[:, None, :] for m in jnp.split(mod, 6, axis=-1)]
        h = rms_norm(x, norm_mix[layer]) * (1.0 + sc_t) + sh_t
        j = layer // N_MIXERS
        if layer % N_MIXERS == 0:
            y = rglru_block(h, rg_w_in[j], rg_conv_w[j], rg_conv_b[j], rg_w_a[j], rg_b_a[j],
                            rg_w_x[j], rg_b_x[j], rg_lambda[j], rg_w_out[j])
        else:
            y = hgrn2_block(h, hg_w_in[j], lb_all[layer], hg_gnorm[j], hg_w_out[j])
        x = (x + g_t * y).astype(x.dtype)
        h = rms_norm(x, norm_mlp[layer]) * (1.0 + sc_c) + sh_c
        ff = jnp.square(jax.nn.relu(h @ mlp_w1[layer])) @ mlp_w2[layer]
        x = (x + g_c * ff).astype(x.dtype)
    return rms_norm(x, norm_final)


import jax as _jax
import jax.numpy as _jnp

TWIN_FORMAT = 'train_step'
FWD_PARAMS = ['x', 'c', 'mod_w', 'mod_b', 'norm_mix', 'norm_mlp', 'norm_final', 'rg_w_in', 'rg_conv_w', 'rg_conv_b', 'rg_w_a', 'rg_b_a', 'rg_w_x', 'rg_b_x', 'rg_lambda', 'rg_w_out', 'hg_w_in', 'hg_lower_bounds', 'hg_gnorm', 'hg_w_out', 'mlp_w1', 'mlp_w2']
TWIN_WEIGHTS = ['mod_w', 'mod_b', 'norm_mix', 'norm_mlp', 'norm_final', 'rg_w_in', 'rg_conv_w', 'rg_conv_b', 'rg_w_a', 'rg_b_a', 'rg_w_x', 'rg_b_x', 'rg_lambda', 'rg_w_out', 'hg_w_in', 'hg_lower_bounds', 'hg_gnorm', 'hg_w_out', 'mlp_w1', 'mlp_w2']
TWIN_DIFF_INPUT = 'x'
TWIN_INPUTS = ['x', 'c', 'mod_w', 'mod_b', 'norm_mix', 'norm_mlp', 'norm_final', 'rg_w_in', 'rg_conv_w', 'rg_conv_b', 'rg_w_a', 'rg_b_a', 'rg_w_x', 'rg_b_x', 'rg_lambda', 'rg_w_out', 'hg_w_in', 'hg_lower_bounds', 'hg_gnorm', 'hg_w_out', 'mlp_w1', 'mlp_w2', 'loss_target', 'm_mod_w', 'm_mod_b', 'm_norm_mix', 'm_norm_mlp', 'm_norm_final', 'm_rg_w_in', 'm_rg_conv_w', 'm_rg_conv_b', 'm_rg_w_a', 'm_rg_b_a', 'm_rg_w_x', 'm_rg_b_x', 'm_rg_lambda', 'm_rg_w_out', 'm_hg_w_in', 'm_hg_lower_bounds', 'm_hg_gnorm', 'm_hg_w_out', 'm_mlp_w1', 'm_mlp_w2', 'v_mod_w', 'v_mod_b', 'v_norm_mix', 'v_norm_mlp', 'v_norm_final', 'v_rg_w_in', 'v_rg_conv_w', 'v_rg_conv_b', 'v_rg_w_a', 'v_rg_b_a', 'v_rg_w_x', 'v_rg_b_x', 'v_rg_lambda', 'v_rg_w_out', 'v_hg_w_in', 'v_hg_lower_bounds', 'v_hg_gnorm', 'v_hg_w_out', 'v_mlp_w1', 'v_mlp_w2']
TWIN_OUTPUTS = ['loss', 'grad_x', 'grad_mod_w', 'grad_mod_b', 'grad_norm_mix', 'grad_norm_mlp', 'grad_norm_final', 'grad_rg_w_in', 'grad_rg_conv_w', 'grad_rg_conv_b', 'grad_rg_w_a', 'grad_rg_b_a', 'grad_rg_w_x', 'grad_rg_b_x', 'grad_rg_lambda', 'grad_rg_w_out', 'grad_hg_w_in', 'grad_hg_lower_bounds', 'grad_hg_gnorm', 'grad_hg_w_out', 'grad_mlp_w1', 'grad_mlp_w2', 'delta_mod_w', 'delta_mod_b', 'delta_norm_mix', 'delta_norm_mlp', 'delta_norm_final', 'delta_rg_w_in', 'delta_rg_conv_w', 'delta_rg_conv_b', 'delta_rg_w_a', 'delta_rg_b_a', 'delta_rg_w_x', 'delta_rg_b_x', 'delta_rg_lambda', 'delta_rg_w_out', 'delta_hg_w_in', 'delta_hg_lower_bounds', 'delta_hg_gnorm', 'delta_hg_w_out', 'delta_mlp_w1', 'delta_mlp_w2', 'new_m_mod_w', 'new_m_mod_b', 'new_m_norm_mix', 'new_m_norm_mlp', 'new_m_norm_final', 'new_m_rg_w_in', 'new_m_rg_conv_w', 'new_m_rg_conv_b', 'new_m_rg_w_a', 'new_m_rg_b_a', 'new_m_rg_w_x', 'new_m_rg_b_x', 'new_m_rg_lambda', 'new_m_rg_w_out', 'new_m_hg_w_in', 'new_m_hg_lower_bounds', 'new_m_hg_gnorm', 'new_m_hg_w_out', 'new_m_mlp_w1', 'new_m_mlp_w2', 'new_v_mod_w', 'new_v_mod_b', 'new_v_norm_mix', 'new_v_norm_mlp', 'new_v_norm_final', 'new_v_rg_w_in', 'new_v_rg_conv_w', 'new_v_rg_conv_b', 'new_v_rg_w_a', 'new_v_rg_b_a', 'new_v_rg_w_x', 'new_v_rg_b_x', 'new_v_rg_lambda', 'new_v_rg_w_out', 'new_v_hg_w_in', 'new_v_hg_lower_bounds', 'new_v_hg_gnorm', 'new_v_hg_w_out', 'new_v_mlp_w1', 'new_v_mlp_w2']
TWIN_LEAF_KINDS = {'loss': 'loss', 'grad_x': 'grad_x', 'grad_mod_w': 'grad_w', 'grad_mod_b': 'grad_w', 'grad_norm_mix': 'grad_w', 'grad_norm_mlp': 'grad_w', 'grad_norm_final': 'grad_w', 'grad_rg_w_in': 'grad_w', 'grad_rg_conv_w': 'grad_w', 'grad_rg_conv_b': 'grad_w', 'grad_rg_w_a': 'grad_w', 'grad_rg_b_a': 'grad_w', 'grad_rg_w_x': 'grad_w', 'grad_rg_b_x': 'grad_w', 'grad_rg_lambda': 'grad_w', 'grad_rg_w_out': 'grad_w', 'grad_hg_w_in': 'grad_w', 'grad_hg_lower_bounds': 'grad_w', 'grad_hg_gnorm': 'grad_w', 'grad_hg_w_out': 'grad_w', 'grad_mlp_w1': 'grad_w', 'grad_mlp_w2': 'grad_w', 'delta_mod_w': 'delta_w', 'delta_mod_b': 'delta_w', 'delta_norm_mix': 'delta_w', 'delta_norm_mlp': 'delta_w', 'delta_norm_final': 'delta_w', 'delta_rg_w_in': 'delta_w', 'delta_rg_conv_w': 'delta_w', 'delta_rg_conv_b': 'delta_w', 'delta_rg_w_a': 'delta_w', 'delta_rg_b_a': 'delta_w', 'delta_rg_w_x': 'delta_w', 'delta_rg_b_x': 'delta_w', 'delta_rg_lambda': 'delta_w', 'delta_rg_w_out': 'delta_w', 'delta_hg_w_in': 'delta_w', 'delta_hg_lower_bounds': 'delta_w', 'delta_hg_gnorm': 'delta_w', 'delta_hg_w_out': 'delta_w', 'delta_mlp_w1': 'delta_w', 'delta_mlp_w2': 'delta_w', 'new_m_mod_w': 'new_m', 'new_m_mod_b': 'new_m', 'new_m_norm_mix': 'new_m', 'new_m_norm_mlp': 'new_m', 'new_m_norm_final': 'new_m', 'new_m_rg_w_in': 'new_m', 'new_m_rg_conv_w': 'new_m', 'new_m_rg_conv_b': 'new_m', 'new_m_rg_w_a': 'new_m', 'new_m_rg_b_a': 'new_m', 'new_m_rg_w_x': 'new_m', 'new_m_rg_b_x': 'new_m', 'new_m_rg_lambda': 'new_m', 'new_m_rg_w_out': 'new_m', 'new_m_hg_w_in': 'new_m', 'new_m_hg_lower_bounds': 'new_m', 'new_m_hg_gnorm': 'new_m', 'new_m_hg_w_out': 'new_m', 'new_m_mlp_w1': 'new_m', 'new_m_mlp_w2': 'new_m', 'new_v_mod_w': 'new_v', 'new_v_mod_b': 'new_v', 'new_v_norm_mix': 'new_v', 'new_v_norm_mlp': 'new_v', 'new_v_norm_final': 'new_v', 'new_v_rg_w_in': 'new_v', 'new_v_rg_conv_w': 'new_v', 'new_v_rg_conv_b': 'new_v', 'new_v_rg_w_a': 'new_v', 'new_v_rg_b_a': 'new_v', 'new_v_rg_w_x': 'new_v', 'new_v_rg_b_x': 'new_v', 'new_v_rg_lambda': 'new_v', 'new_v_rg_w_out': 'new_v', 'new_v_hg_w_in': 'new_v', 'new_v_hg_lower_bounds': 'new_v', 'new_v_hg_gnorm': 'new_v', 'new_v_hg_w_out': 'new_v', 'new_v_mlp_w1': 'new_v', 'new_v_mlp_w2': 'new_v'}


def _forward(args):
    return _fwd_reference(*[args[k] for k in FWD_PARAMS])


def _output_shape():
    out = _jax.eval_shape(lambda: _forward(_fwd_setup_inputs(0)))
    return out.shape, out.dtype

N_MICROBATCH = 1
ADAM_LR = 0.001
ADAM_B1 = 0.9
ADAM_B2 = 0.999
ADAM_EPS = 1e-08
ADAM_WD = 0.01
ADAM_STEP = 10
PER_EXAMPLE_BATCH_AXIS = {'x': 0, 'c': 0, 'loss_target': 0}
SHARED_INPUTS = []
_WEIGHT_DTYPES = {'mod_w': _jnp.float32, 'mod_b': _jnp.float32, 'norm_mix': _jnp.float32, 'norm_mlp': _jnp.float32, 'norm_final': _jnp.float32, 'rg_w_in': _jnp.float32, 'rg_conv_w': _jnp.float32, 'rg_conv_b': _jnp.float32, 'rg_w_a': _jnp.float32, 'rg_b_a': _jnp.float32, 'rg_w_x': _jnp.float32, 'rg_b_x': _jnp.float32, 'rg_lambda': _jnp.float32, 'rg_w_out': _jnp.float32, 'hg_w_in': _jnp.float32, 'hg_lower_bounds': _jnp.float32, 'hg_gnorm': _jnp.float32, 'hg_w_out': _jnp.float32, 'mlp_w1': _jnp.float32, 'mlp_w2': _jnp.float32}
MOMENT_SCALE = {'mod_w': 8.042801e-02, 'mod_b': 1.337976e-01, 'norm_mix': 1.001791e-01, 'norm_mlp': 6.395344e-02, 'norm_final': 1.741294e+01, 'rg_w_in': 1.539685e-01, 'rg_conv_w': 1.928110e-01, 'rg_conv_b': 2.562860e-01, 'rg_w_a': 2.482360e-02, 'rg_b_a': 3.377811e-02, 'rg_w_x': 5.241383e-02, 'rg_b_x': 6.598082e-02, 'rg_lambda': 7.909848e-02, 'rg_w_out': 1.563740e-01, 'hg_w_in': 1.915529e-02, 'hg_lower_bounds': 6.089765e-04, 'hg_gnorm': 8.140266e-02, 'hg_w_out': 2.692619e-02, 'mlp_w1': 3.705157e-02, 'mlp_w2': 7.001128e-02}


def _to_microbatches(a, axis):
    t = _jnp.moveaxis(a, axis, 0)
    t = t.reshape((N_MICROBATCH, t.shape[0] // N_MICROBATCH) + t.shape[1:])
    return _jnp.moveaxis(t, 1, axis + 1)


def setup_inputs(seed: int = 0) -> dict:
    inp = _fwd_setup_inputs(seed)
    key = _jax.random.fold_in(_jax.random.key(seed), 7919)
    shape, _ = _output_shape()
    out = dict(inp)
    out["loss_target"] = _jax.random.normal(_jax.random.fold_in(key, 0), shape, _jnp.float32)
    for i, name in enumerate(TWIN_WEIGHTS):
        w = inp[name].astype(_jnp.float32)
        if MOMENT_SCALE is None:
            s = _jnp.sqrt(_jnp.mean(_jnp.square(w)) + 1e-30)
        else:
            s = MOMENT_SCALE[name]
        km, kv = _jax.random.split(_jax.random.fold_in(key, i + 1))
        out[name] = w
        out["m_" + name] = s * _jax.random.normal(km, w.shape, _jnp.float32)
        out["v_" + name] = (s * s) * _jax.random.uniform(kv, w.shape, _jnp.float32, 0.5, 1.5)
    if N_MICROBATCH > 1:
        for name, axis in PER_EXAMPLE_BATCH_AXIS.items():
            out[name] = _to_microbatches(out[name], axis)
    return {'x': out['x'], 'c': out['c'], 'mod_w': out['mod_w'], 'mod_b': out['mod_b'], 'norm_mix': out['norm_mix'], 'norm_mlp': out['norm_mlp'], 'norm_final': out['norm_final'], 'rg_w_in': out['rg_w_in'], 'rg_conv_w': out['rg_conv_w'], 'rg_conv_b': out['rg_conv_b'], 'rg_w_a': out['rg_w_a'], 'rg_b_a': out['rg_b_a'], 'rg_w_x': out['rg_w_x'], 'rg_b_x': out['rg_b_x'], 'rg_lambda': out['rg_lambda'], 'rg_w_out': out['rg_w_out'], 'hg_w_in': out['hg_w_in'], 'hg_lower_bounds': out['hg_lower_bounds'], 'hg_gnorm': out['hg_gnorm'], 'hg_w_out': out['hg_w_out'], 'mlp_w1': out['mlp_w1'], 'mlp_w2': out['mlp_w2'], 'loss_target': out['loss_target'], 'm_mod_w': out['m_mod_w'], 'm_mod_b': out['m_mod_b'], 'm_norm_mix': out['m_norm_mix'], 'm_norm_mlp': out['m_norm_mlp'], 'm_norm_final': out['m_norm_final'], 'm_rg_w_in': out['m_rg_w_in'], 'm_rg_conv_w': out['m_rg_conv_w'], 'm_rg_conv_b': out['m_rg_conv_b'], 'm_rg_w_a': out['m_rg_w_a'], 'm_rg_b_a': out['m_rg_b_a'], 'm_rg_w_x': out['m_rg_w_x'], 'm_rg_b_x': out['m_rg_b_x'], 'm_rg_lambda': out['m_rg_lambda'], 'm_rg_w_out': out['m_rg_w_out'], 'm_hg_w_in': out['m_hg_w_in'], 'm_hg_lower_bounds': out['m_hg_lower_bounds'], 'm_hg_gnorm': out['m_hg_gnorm'], 'm_hg_w_out': out['m_hg_w_out'], 'm_mlp_w1': out['m_mlp_w1'], 'm_mlp_w2': out['m_mlp_w2'], 'v_mod_w': out['v_mod_w'], 'v_mod_b': out['v_mod_b'], 'v_norm_mix': out['v_norm_mix'], 'v_norm_mlp': out['v_norm_mlp'], 'v_norm_final': out['v_norm_final'], 'v_rg_w_in': out['v_rg_w_in'], 'v_rg_conv_w': out['v_rg_conv_w'], 'v_rg_conv_b': out['v_rg_conv_b'], 'v_rg_w_a': out['v_rg_w_a'], 'v_rg_b_a': out['v_rg_b_a'], 'v_rg_w_x': out['v_rg_w_x'], 'v_rg_b_x': out['v_rg_b_x'], 'v_rg_lambda': out['v_rg_lambda'], 'v_rg_w_out': out['v_rg_w_out'], 'v_hg_w_in': out['v_hg_w_in'], 'v_hg_lower_bounds': out['v_hg_lower_bounds'], 'v_hg_gnorm': out['v_hg_gnorm'], 'v_hg_w_out': out['v_hg_w_out'], 'v_mlp_w1': out['v_mlp_w1'], 'v_mlp_w2': out['v_mlp_w2']}


def _loss(weights, diff, rest, loss_target):
    with _jax.named_scope("forward"):
        args = {**rest, TWIN_DIFF_INPUT: diff, **{k: w.astype(_WEIGHT_DTYPES[k]) for k, w in weights.items()}}
        y = _forward(args)
    with _jax.named_scope("loss_head"):
        err = _jnp.square(y.astype(_jnp.float32) - loss_target)
        return 0.5 * _jnp.sum(_jnp.mean(err, axis=-1)) if err.ndim else 0.5 * err


def _adamw(w, g, m, v):
    m = ADAM_B1 * m + (1.0 - ADAM_B1) * g
    v = ADAM_B2 * v + (1.0 - ADAM_B2) * _jnp.square(g)
    m_hat = m / (1.0 - ADAM_B1 ** ADAM_STEP)
    v_hat = v / (1.0 - ADAM_B2 ** ADAM_STEP)
    delta = -ADAM_LR * (m_hat / (_jnp.sqrt(v_hat) + ADAM_EPS) + ADAM_WD * w)
    return delta, m, v


def reference(x, c, mod_w, mod_b, norm_mix, norm_mlp, norm_final, rg_w_in, rg_conv_w, rg_conv_b, rg_w_a, rg_b_a, rg_w_x, rg_b_x, rg_lambda, rg_w_out, hg_w_in, hg_lower_bounds, hg_gnorm, hg_w_out, mlp_w1, mlp_w2, loss_target, m_mod_w, m_mod_b, m_norm_mix, m_norm_mlp, m_norm_final, m_rg_w_in, m_rg_conv_w, m_rg_conv_b, m_rg_w_a, m_rg_b_a, m_rg_w_x, m_rg_b_x, m_rg_lambda, m_rg_w_out, m_hg_w_in, m_hg_lower_bounds, m_hg_gnorm, m_hg_w_out, m_mlp_w1, m_mlp_w2, v_mod_w, v_mod_b, v_norm_mix, v_norm_mlp, v_norm_final, v_rg_w_in, v_rg_conv_w, v_rg_conv_b, v_rg_w_a, v_rg_b_a, v_rg_w_x, v_rg_b_x, v_rg_lambda, v_rg_w_out, v_hg_w_in, v_hg_lower_bounds, v_hg_gnorm, v_hg_w_out, v_mlp_w1, v_mlp_w2):
    given = dict(x=x, c=c, mod_w=mod_w, mod_b=mod_b, norm_mix=norm_mix, norm_mlp=norm_mlp, norm_final=norm_final, rg_w_in=rg_w_in, rg_conv_w=rg_conv_w, rg_conv_b=rg_conv_b, rg_w_a=rg_w_a, rg_b_a=rg_b_a, rg_w_x=rg_w_x, rg_b_x=rg_b_x, rg_lambda=rg_lambda, rg_w_out=rg_w_out, hg_w_in=hg_w_in, hg_lower_bounds=hg_lower_bounds, hg_gnorm=hg_gnorm, hg_w_out=hg_w_out, mlp_w1=mlp_w1, mlp_w2=mlp_w2, loss_target=loss_target, m_mod_w=m_mod_w, m_mod_b=m_mod_b, m_norm_mix=m_norm_mix, m_norm_mlp=m_norm_mlp, m_norm_final=m_norm_final, m_rg_w_in=m_rg_w_in, m_rg_conv_w=m_rg_conv_w, m_rg_conv_b=m_rg_conv_b, m_rg_w_a=m_rg_w_a, m_rg_b_a=m_rg_b_a, m_rg_w_x=m_rg_w_x, m_rg_b_x=m_rg_b_x, m_rg_lambda=m_rg_lambda, m_rg_w_out=m_rg_w_out, m_hg_w_in=m_hg_w_in, m_hg_lower_bounds=m_hg_lower_bounds, m_hg_gnorm=m_hg_gnorm, m_hg_w_out=m_hg_w_out, m_mlp_w1=m_mlp_w1, m_mlp_w2=m_mlp_w2, v_mod_w=v_mod_w, v_mod_b=v_mod_b, v_norm_mix=v_norm_mix, v_norm_mlp=v_norm_mlp, v_norm_final=v_norm_final, v_rg_w_in=v_rg_w_in, v_rg_conv_w=v_rg_conv_w, v_rg_conv_b=v_rg_conv_b, v_rg_w_a=v_rg_w_a, v_rg_b_a=v_rg_b_a, v_rg_w_x=v_rg_w_x, v_rg_b_x=v_rg_b_x, v_rg_lambda=v_rg_lambda, v_rg_w_out=v_rg_w_out, v_hg_w_in=v_hg_w_in, v_hg_lower_bounds=v_hg_lower_bounds, v_hg_gnorm=v_hg_gnorm, v_hg_w_out=v_hg_w_out, v_mlp_w1=v_mlp_w1, v_mlp_w2=v_mlp_w2)
    weights = {n: given[n] for n in TWIN_WEIGHTS}
    shared = {n: given[n] for n in SHARED_INPUTS}
    per_example = {n: given[n] for n in ['x', 'c']}
    grad_fn = _jax.value_and_grad(_loss, argnums=(0, 1))

    def one_microbatch(ex, loss_target):
        ex = dict(ex)
        diff = ex.pop(TWIN_DIFF_INPUT)
        return grad_fn(weights, diff, {**shared, **ex}, loss_target)

    if N_MICROBATCH == 1:
        loss, (grad_w, grad_x) = one_microbatch(per_example, given["loss_target"])
    else:
        def body(carry, xs):
            loss_sum, grad_sum = carry
            l_k, (gw_k, gx_k) = one_microbatch(xs[0], xs[1])
            with _jax.named_scope("update"):
                return (loss_sum + l_k, _jax.tree.map(_jnp.add, grad_sum, gw_k)), gx_k

        init = (_jnp.zeros((), _jnp.float32), _jax.tree.map(_jnp.zeros_like, weights))
        (loss, grad_w), grad_x = _jax.lax.scan(body, init, (per_example, given["loss_target"]))
    with _jax.named_scope("update"):
        delta_w, new_m, new_v = {}, {}, {}
        for n in TWIN_WEIGHTS:
            delta_w[n], new_m[n], new_v[n] = _adamw(weights[n], grad_w[n], given["m_" + n], given["v_" + n])
    return (loss, grad_x, *[grad_w[n] for n in TWIN_WEIGHTS], *[delta_w[n] for n in TWIN_WEIGHTS],
            *[new_m[n] for n in TWIN_WEIGHTS], *[new_v[n] for n in TWIN_WEIGHTS])
```

```python
import functools

import jax
import jax.numpy as jnp
from jax import lax
from jax.experimental import pallas as pl
from jax.experimental.pallas import tpu as pltpu

F32 = jnp.float32
MXU = jnp.bfloat16

N_DEV = 8
D = 1024
DEPTH = 4
HEADS = 8
HEAD_DIM = 128
CONV_WIDTH = 4
LRU_C = 8.0
HG_CHUNK = 64
NORM_EPS = 1e-6
GNORM_EPS = 1e-5
ADAM_LR = 0.001
ADAM_B1 = 0.9
ADAM_B2 = 0.999
ADAM_EPS = 1e-08
ADAM_WD = 0.01
ADAM_STEP = 10
GELU_C = 0.7978845608028654
GELU_K = 0.044715
EXP_CLAMP = 80.0
SUBLANES = 8
VMEM_LIMIT = 48 * 1024 * 1024
ROW_TILE = 256
MM_TILE = 1024
ADAMW_STEP_BYTES = 8 * 1024 * 1024

MESH = pl.DeviceIdType.MESH
ANY = pl.BlockSpec(memory_space=pl.ANY)


def _params():
    return pltpu.CompilerParams(vmem_limit_bytes=VMEM_LIMIT)


def _tile(n, target):
    if n <= target:
        return n
    t = target // SUBLANES * SUBLANES
    while n % t:
        t -= SUBLANES
    return t


def _sigmoid(x):
    return 1.0 / (1.0 + jnp.exp(-x))


def _silu(x):
    return x * _sigmoid(x)


def _silu_grad(x):
    s = _sigmoid(x)
    return s * (1.0 + x * (1.0 - s))


def _gelu(y):
    return 0.5 * y * (1.0 + jnp.tanh(GELU_C * (y + GELU_K * y * y * y)))


def _gelu_grad(y):
    th = jnp.tanh(GELU_C * (y + GELU_K * y * y * y))
    return 0.5 * (1.0 + th) + 0.5 * y * (1.0 - th * th) * GELU_C * (1.0 + 3.0 * GELU_K * y * y)


def _dot(a, b):
    return lax.dot_general(a.astype(MXU), b.astype(MXU), (((1,), (0,)), ((), ())), preferred_element_type=F32)


def _dot_nt(a, b):
    return lax.dot_general(a.astype(MXU), b.astype(MXU), (((1,), (1,)), ((), ())), preferred_element_type=F32)


def _dot_tn(a, b):
    return lax.dot_general(a.astype(MXU), b.astype(MXU), (((0,), (0,)), ((), ())), preferred_element_type=F32)


def _dot_exact(tri, x):
    t = tri.astype(MXU)
    hi = x.astype(MXU)
    r1 = x - hi.astype(F32)
    mid = r1.astype(MXU)
    lo = (r1 - mid.astype(F32)).astype(MXU)
    dn = (((1,), (0,)), ((), ()))
    return (lax.dot_general(t, hi, dn, preferred_element_type=F32) + lax.dot_general(t, mid, dn, preferred_element_type=F32)
            + lax.dot_general(t, lo, dn, preferred_element_type=F32))


def _row_sum(v):
    return jnp.sum(v, axis=0, keepdims=True)


def _all_gather(arrs, name):
    n = len(arrs)

    def body(*refs):
        ins, outs = refs[:n], refs[n:2 * n]
        send_sems, recv_sems, local_sems = refs[2 * n:]
        x, y, c = lax.axis_index("x"), lax.axis_index("y"), lax.axis_index("c")
        me, sibling = (x, y, c), (x, y, 1 - c)
        chips = [(1 - x, y), (x, 1 - y), (1 - x, 1 - y)]

        def slot(a, p):
            return outs[a].at[4 * p[0] + 2 * p[1] + p[2]]

        def copy(a, k, block, to, src=None):
            return pltpu.make_async_remote_copy(
                src_ref=slot(a, block) if src is None else src, dst_ref=slot(a, block),
                send_sem=send_sems.at[a, k], recv_sem=recv_sems.at[a, k], device_id=to, device_id_type=MESH)

        mine = [pltpu.make_async_copy(ins[a], slot(a, me), local_sems.at[a]) for a in range(n)]
        for cp in mine:
            cp.start()
        first = []
        for a in range(n):
            first.append(copy(a, 0, me, sibling, src=ins[a]))
            first += [copy(a, 1 + j, me, (*chip, c), src=ins[a]) for j, chip in enumerate(chips)]
        for cp in first:
            cp.start()
        passed = []
        for j, chip in enumerate(chips):
            for a in range(n):
                copy(a, 1 + j, (*chip, c), me).wait_recv()
                cp = copy(a, 4 + j, (*chip, c), sibling)
                cp.start()
                passed.append(cp)
        for a in range(n):
            copy(a, 0, sibling, me).wait_recv()
            for j, chip in enumerate(chips):
                copy(a, 4 + j, (*chip, 1 - c), me).wait_recv()
        for cp in first + passed:
            cp.wait_send()
        for cp in mine:
            cp.wait()

    return pl.pallas_call(
        body, name=name, in_specs=[ANY] * n, out_specs=[ANY] * n,
        out_shape=[jax.ShapeDtypeStruct((N_DEV,) + a.shape, a.dtype) for a in arrs],
        scratch_shapes=[pltpu.SemaphoreType.DMA((n, 7)), pltpu.SemaphoreType.DMA((n, 7)), pltpu.SemaphoreType.DMA((n,))],
    )(*arrs)


def _all_to_all(arrs, name):
    n = len(arrs)
    flips = [(fx, fy, fc) for fx in (0, 1) for fy in (0, 1) for fc in (0, 1)][1:]

    def body(*refs):
        ins, outs = refs[:n], refs[n:2 * n]
        send_sems, recv_sems, local_sems = refs[2 * n:]
        x, y, c = lax.axis_index("x"), lax.axis_index("y"), lax.axis_index("c")
        me = 4 * x + 2 * y + c

        def flip(v, f):
            return 1 - v if f else v

        peers = [(flip(x, fx), flip(y, fy), flip(c, fc)) for fx, fy, fc in flips]
        pids = [4 * p[0] + 2 * p[1] + p[2] for p in peers]

        def copy(a, k, landing):
            return pltpu.make_async_remote_copy(
                src_ref=ins[a].at[pids[k]], dst_ref=outs[a].at[landing], send_sem=send_sems.at[a, k],
                recv_sem=recv_sems.at[a, k], device_id=peers[k], device_id_type=MESH)

        mine = [pltpu.make_async_copy(ins[a].at[me], outs[a].at[me], local_sems.at[a]) for a in range(n)]
        for cp in mine:
            cp.start()
        sent = [copy(a, k, me) for a in range(n) for k in range(7)]
        for cp in sent:
            cp.start()
        for a in range(n):
            for k in range(7):
                copy(a, k, pids[k]).wait_recv()
        for cp in sent:
            cp.wait_send()
        for cp in mine:
            cp.wait()

    return pl.pallas_call(
        body, name=name, in_specs=[ANY] * n, out_specs=[ANY] * n,
        out_shape=[jax.ShapeDtypeStruct(a.shape, a.dtype) for a in arrs],
        scratch_shapes=[pltpu.SemaphoreType.DMA((n, 7)), pltpu.SemaphoreType.DMA((n, 7)), pltpu.SemaphoreType.DMA((n,))],
    )(*arrs)


NN = (((1,), (0,)), ((), ()))
NT = (((1,), (1,)), ((), ()))
TN = (((0,), (0,)), ((), ()))


def _matmul(name, a, b, dims, grid, a_spec, b_spec, outs, epilogue, extras=(), acc_shape=None):
    n_in = 2 + len(extras)
    n_out = len(outs)
    nk = grid[2]

    def body(*refs):
        a_ref, b_ref = refs[0], refs[1]
        ex, out_refs = refs[2:n_in], refs[n_in:n_in + n_out]

        def part():
            return lax.dot_general(a_ref[...], b_ref[...], dims, preferred_element_type=F32)

        if nk == 1:
            epilogue(part(), ex, out_refs)
            return
        acc = refs[n_in + n_out]
        k = pl.program_id(2)

        @pl.when(k == 0)
        def _():
            acc[...] = part()

        @pl.when(k > 0)
        def _():
            acc[...] += part()

        @pl.when(k == nk - 1)
        def _():
            epilogue(acc[...], ex, out_refs)

    return pl.pallas_call(
        body, name=name, grid=grid, in_specs=[a_spec, b_spec] + [s for _, s in extras], out_specs=[s for _, _, s in outs],
        out_shape=[jax.ShapeDtypeStruct(sh, dt) for sh, dt, _ in outs],
        scratch_shapes=[] if nk == 1 else [pltpu.VMEM(acc_shape, F32)], compiler_params=_params(),
    )(a, b, *[e for e, _ in extras])


def _ep_store(acc, ex, outs):
    outs[0][...] = acc.astype(outs[0].dtype)


def _ep_residual(acc, ex, outs):
    outs[0][...] = acc
    outs[1][...] = ex[0][...] + ex[1][0:1, :] * acc


def _ep_relu2(acc, ex, outs):
    outs[0][...] = acc.astype(outs[0].dtype)
    r = jnp.maximum(acc, 0.0)
    outs[1][...] = (r * r).astype(outs[1].dtype)


def _ep_relu2_grad(acc, ex, outs):
    outs[0][...] = (acc * (2.0 * jnp.maximum(ex[0][...].astype(F32), 0.0))).astype(outs[0].dtype)


def _mm_cols(name, a, wg, epilogue=_ep_store, out_dtypes=(F32,)):
    m, k = a.shape
    sw = wg.shape[2]
    tm = _tile(m, MM_TILE)
    spec = pl.BlockSpec((tm, sw), lambda i, j, kk: (i, j))
    return _matmul(name, a, wg, NN, (m // tm, N_DEV, 1), pl.BlockSpec((tm, k), lambda i, j, kk: (i, 0)),
                   pl.BlockSpec((None, k, sw), lambda i, j, kk: (j, 0, 0)),
                   [((m, N_DEV * sw), dt, spec) for dt in out_dtypes], epilogue)


def _mm_cols_t(name, a, wg, out_dtype=F32):
    m = a.shape[0]
    _, r, sw = wg.shape
    tm, tn = _tile(m, MM_TILE), _tile(r, MM_TILE)
    return _matmul(name, a, wg, NT, (m // tm, r // tn, N_DEV), pl.BlockSpec((tm, sw), lambda i, j, kk: (i, kk)),
                   pl.BlockSpec((None, tn, sw), lambda i, j, kk: (kk, j, 0)),
                   [((m, r), out_dtype, pl.BlockSpec((tm, tn), lambda i, j, kk: (i, j)))], _ep_store, acc_shape=(tm, tn))[0]


def _mm_rows(name, a, w, epilogue, outs_dtypes, extras=()):
    m, k = a.shape
    n = w.shape[1]
    tm, tn, tk = _tile(m, MM_TILE), _tile(n, MM_TILE // 2), _tile(k, MM_TILE)
    spec = pl.BlockSpec((tm, tn), lambda i, j, kk: (i, j))
    return _matmul(name, a, w, NN, (m // tm, n // tn, k // tk), pl.BlockSpec((tm, tk), lambda i, j, kk: (i, kk)),
                   pl.BlockSpec((tk, tn), lambda i, j, kk: (kk, j)), [((m, n), dt, spec) for dt in outs_dtypes], epilogue,
                   extras=[(e, spec if e.shape[0] == m else pl.BlockSpec((SUBLANES, tn), lambda i, j, kk: (0, j))) for e in extras],
                   acc_shape=(tm, tn))


def _mm_rows_t(name, a, w, epilogue=_ep_store, out_dtype=F32, extras=()):
    m, c = a.shape
    r = w.shape[0]
    tm, tn, tk = _tile(m, MM_TILE), _tile(r, MM_TILE // 2), _tile(c, MM_TILE)
    spec = pl.BlockSpec((tm, tn), lambda i, j, kk: (i, j))
    return _matmul(name, a, w, NT, (m // tm, r // tn, c // tk), pl.BlockSpec((tm, tk), lambda i, j, kk: (i, kk)),
                   pl.BlockSpec((tn, tk), lambda i, j, kk: (j, kk)), [((m, r), out_dtype, spec)], epilogue,
                   extras=[(e, spec) for e in extras], acc_shape=(tm, tn))[0]


def _mm_grad(name, a, b, shard_cols):
    t, m = a.shape
    n = b.shape[1]
    tk, tm = _tile(t, MM_TILE), _tile(m, MM_TILE)
    if shard_cols:
        tn = n // N_DEV
        out = ((N_DEV, m, tn), MXU, pl.BlockSpec((None, tm, tn), lambda i, j, kk: (j, i, 0)))
    else:
        tn = _tile(n, MM_TILE)
        out = ((m, n), MXU, pl.BlockSpec((tm, tn), lambda i, j, kk: (i, j)))
    return _matmul(name, a, b, TN, (m // tm, n // tn, t // tk), pl.BlockSpec((tk, tm), lambda i, j, kk: (kk, i)),
                   pl.BlockSpec((tk, tn), lambda i, j, kk: (kk, j)), [out], _ep_store, acc_shape=(tm, tn))[0]


def _rows_spec(tt, width=D):
    return pl.BlockSpec((tt, width), lambda i: (i, 0))


def _vec_spec(rows=SUBLANES, width=D):
    return pl.BlockSpec((rows, width), lambda i: (0, 0))


def _norm_mod(name, x, vec):
    t = x.shape[0]
    tt = _tile(t, ROW_TILE)

    def body(x_ref, v_ref, h_ref):
        xv = x_ref[...]
        r = lax.rsqrt(jnp.mean(xv * xv, axis=-1, keepdims=True) + NORM_EPS)
        h = (xv * r) * v_ref[0:1, :]
        h_ref[...] = (h * (1.0 + v_ref[1:2, :]) + v_ref[2:3, :]).astype(h_ref.dtype)

    return pl.pallas_call(body, name=name, grid=(t // tt,), in_specs=[_rows_spec(tt), _vec_spec()], out_specs=_rows_spec(tt),
                          out_shape=jax.ShapeDtypeStruct((t, D), MXU))(x, vec)


def _norm_mod_grad(name, dh, x, vec, dres):
    t = x.shape[0]
    tt = _tile(t, ROW_TILE)
    nt = t // tt

    def body(dh_ref, x_ref, v_ref, dres_ref, dx_ref, acc_ref):
        i = pl.program_id(0)

        @pl.when(i == 0)
        def _():
            acc_ref[...] = jnp.zeros_like(acc_ref)

        xv, dhv = x_ref[...], dh_ref[...]
        r = lax.rsqrt(jnp.mean(xv * xv, axis=-1, keepdims=True) + NORM_EPS)
        xn = xv * r
        w = v_ref[0:1, :] * (1.0 + v_ref[1:2, :])
        acc_ref[0:1, :] += _row_sum(dhv * xn)
        acc_ref[2:3, :] += _row_sum(dhv)
        dxn = dhv * w
        dx_ref[...] = dres_ref[...] + r * (dxn - xn * jnp.mean(dxn * xn, axis=-1, keepdims=True))

        @pl.when(i == nt - 1)
        def _():
            dw = acc_ref[0:1, :]
            acc_ref[1:2, :] = dw * v_ref[0:1, :]
            acc_ref[0:1, :] = dw * (1.0 + v_ref[1:2, :])

    return pl.pallas_call(
        body, name=name, grid=(nt,), in_specs=[_rows_spec(tt), _rows_spec(tt), _vec_spec(), _rows_spec(tt)],
        out_specs=[_rows_spec(tt), _vec_spec()],
        out_shape=[jax.ShapeDtypeStruct((t, D), F32), jax.ShapeDtypeStruct((SUBLANES, D), F32)])(dh, x, vec, dres)


def _residual_grad(name, dxo, y, vec):
    t = y.shape[0]
    tt = _tile(t, ROW_TILE)

    def body(dx_ref, y_ref, v_ref, dy_ref, acc_ref):
        @pl.when(pl.program_id(0) == 0)
        def _():
            acc_ref[...] = jnp.zeros_like(acc_ref)

        dxv = dx_ref[...]
        dy_ref[...] = (dxv * v_ref[0:1, :]).astype(dy_ref.dtype)
        acc_ref[0:1, :] += _row_sum(dxv * y_ref[...])

    return pl.pallas_call(
        body, name=name, grid=(t // tt,), in_specs=[_rows_spec(tt), _rows_spec(tt), _vec_spec()],
        out_specs=[_rows_spec(tt), _vec_spec()],
        out_shape=[jax.ShapeDtypeStruct((t, D), MXU), jax.ShapeDtypeStruct((SUBLANES, D), F32)])(dxo, y, vec)


def _loss_head(name, x, target, vec):
    t = x.shape[0]
    tt = _tile(t, ROW_TILE)

    def body(x_ref, t_ref, v_ref, dx_ref, acc_ref):
        @pl.when(pl.program_id(0) == 0)
        def _():
            acc_ref[...] = jnp.zeros_like(acc_ref)

        xv = x_ref[...]
        r = lax.rsqrt(jnp.mean(xv * xv, axis=-1, keepdims=True) + NORM_EPS)
        xn = xv * r
        gain = v_ref[0:1, :]
        err = xn * gain - t_ref[...]
        acc_ref[1:2, :] += _row_sum(err * err) * (0.5 / D)
        dout = err * (1.0 / D)
        acc_ref[0:1, :] += _row_sum(dout * xn)
        dxn = dout * gain
        dx_ref[...] = r * (dxn - xn * jnp.mean(dxn * xn, axis=-1, keepdims=True))

    return pl.pallas_call(
        body, name=name, grid=(t // tt,), in_specs=[_rows_spec(tt), _rows_spec(tt), _vec_spec()],
        out_specs=[_rows_spec(tt), _vec_spec()],
        out_shape=[jax.ShapeDtypeStruct((t, D), F32), jax.ShapeDtypeStruct((SUBLANES, D), F32)])(x, target, vec)


def _shift_down(x, halo, k):
    y = pltpu.roll(x, k, 0)
    top = jnp.where(lax.broadcasted_iota(jnp.int32, halo.shape, 0) < k, pltpu.roll(halo, k, 0), y[0:SUBLANES, :])
    return jnp.concatenate([top, y[SUBLANES:, :]], axis=0)


def _shift_up(x, halo, k):
    n = x.shape[0]
    y = pltpu.roll(x, n - k, 0)
    bottom = jnp.where(lax.broadcasted_iota(jnp.int32, halo.shape, 0) >= SUBLANES - k, pltpu.roll(halo, SUBLANES - k, 0),
                       y[n - SUBLANES:, :])
    return jnp.concatenate([y[:n - SUBLANES, :], bottom], axis=0)


def _rg_gates(xb, halo, cw_ref, vec_ref, wa_ref, wx_ref, at_start):
    shifted = [xb] + [_shift_down(xb, halo, k) for k in range(1, CONV_WIDTH)]
    xc = vec_ref[0:1, :] + shifted[0] * cw_ref[CONV_WIDTH - 1:CONV_WIDTH, :]
    for k in range(1, CONV_WIDTH):
        xc = xc + shifted[k] * cw_ref[CONV_WIDTH - 1 - k:CONV_WIDTH - k, :]
    heads = [slice(h * HEAD_DIM, (h + 1) * HEAD_DIM) for h in range(HEADS)]
    pa = jnp.concatenate([_dot(xc[:, s], wa_ref[h]) for h, s in enumerate(heads)], axis=1) + vec_ref[1:2, :]
    px = jnp.concatenate([_dot(xc[:, s], wx_ref[h]) for h, s in enumerate(heads)], axis=1) + vec_ref[2:3, :]
    ra, ia = _sigmoid(pa), _sigmoid(px)
    nl = -vec_ref[3:4, :]
    sp = jnp.maximum(nl, 0.0) + jnp.log(1.0 + jnp.exp(-jnp.abs(nl)))
    log_a = (-LRU_C) * ra * sp
    a = jnp.exp(log_a)
    th = jnp.tanh(log_a)
    is_t0 = jnp.logical_and(lax.broadcasted_iota(jnp.int32, xb.shape, 0) == 0, at_start)
    mult = jnp.where(is_t0, 1.0, jnp.sqrt(-2.0 * th / (1.0 - th)))
    return dict(shifted=shifted, xc=xc, ra=ra, ia=ia, sp=sp, a=a, mult=mult, is_t0=is_t0, heads=heads)


def _rg_specs(tt, nt, order):
    blk = tt // SUBLANES
    return dict(
        x=pl.BlockSpec((tt, D), lambda i: (order(i), 0)), y=pl.BlockSpec((tt, D), lambda i: (order(i), 1)),
        halo=pl.BlockSpec((SUBLANES, D), lambda i: (jnp.maximum(order(i) * blk - 1, 0), 0)),
        cw=_vec_spec(CONV_WIDTH), vec=_vec_spec(), w=pl.BlockSpec((HEADS, HEAD_DIM, HEAD_DIM), lambda i: (0, 0, 0)))


def _rg_forward(name, z, cw, vec, wa, wx):
    t = z.shape[0]
    tt = _tile(t, ROW_TILE)
    nt = t // tt
    sp = _rg_specs(tt, nt, lambda i: i)

    def body(zx_ref, zy_ref, halo_ref, cw_ref, vec_ref, wa_ref, wx_ref, p_ref, h_ref, a_s, u_s, carry):
        i = pl.program_id(0)

        @pl.when(i == 0)
        def _():
            carry[...] = jnp.zeros_like(carry)

        halo = jnp.where(i > 0, halo_ref[...], 0.0)
        g = _rg_gates(zx_ref[...], halo, cw_ref, vec_ref, wa_ref, wx_ref, i == 0)
        a_s[...] = g["a"]
        u_s[...] = g["mult"] * (g["ia"] * g["xc"])

        def group(gi, h):
            rows = pl.ds(pl.multiple_of(gi * SUBLANES, SUBLANES), SUBLANES)
            a8, u8 = a_s[rows, :], u_s[rows, :]
            out = []
            for j in range(SUBLANES):
                h = a8[j:j + 1, :] * h + u8[j:j + 1, :]
                out.append(h)
            h_ref[rows, :] = jnp.concatenate(out, axis=0)
            return h

        carry[0:1, :] = lax.fori_loop(0, tt // SUBLANES, group, carry[0:1, :])
        p_ref[...] = (h_ref[...] * _gelu(zy_ref[...])).astype(p_ref.dtype)

    return pl.pallas_call(
        body, name=name, grid=(nt,), in_specs=[sp["x"], sp["y"], sp["halo"], sp["cw"], sp["vec"], sp["w"], sp["w"]],
        out_specs=[_rows_spec(tt), _rows_spec(tt)],
        out_shape=[jax.ShapeDtypeStruct((t, D), MXU), jax.ShapeDtypeStruct((t, D), F32)],
        scratch_shapes=[pltpu.VMEM((tt, D), F32), pltpu.VMEM((tt, D), F32), pltpu.VMEM((SUBLANES, D), F32)],
        compiler_params=_params())(z, z, z, cw, vec, wa, wx)


def _rg_backward(name, dp, z, h, cw, vec, wa, wx):
    t = z.shape[0]
    tt = _tile(t, ROW_TILE)
    nt = t // tt
    rev = lambda i: nt - 1 - i
    sp = _rg_specs(tt, nt, rev)
    rows_rev = pl.BlockSpec((tt, D), lambda i: (rev(i), 0))

    def body(dp_ref, zx_ref, zy_ref, halo_ref, h_ref, hhalo_ref, cw_ref, vec_ref, wa_ref, wx_ref,
             dz_ref, dvec_ref, dcw_ref, dwa_ref, dwx_ref, a_s, d_s, carry, nxt):
        i = pl.program_id(0)
        j = rev(i)

        @pl.when(i == 0)
        def _():
            carry[...] = jnp.zeros_like(carry)
            nxt[...] = jnp.zeros_like(nxt)
            dvec_ref[...] = jnp.zeros_like(dvec_ref)
            dcw_ref[...] = jnp.zeros_like(dcw_ref)
            dwa_ref[...] = jnp.zeros_like(dwa_ref)
            dwx_ref[...] = jnp.zeros_like(dwx_ref)

        halo = jnp.where(j > 0, halo_ref[...], 0.0)
        g = _rg_gates(zx_ref[...], halo, cw_ref, vec_ref, wa_ref, wx_ref, j == 0)
        xc, ra, ia, a, mult = g["xc"], g["ra"], g["ia"], g["a"], g["mult"]
        hv, zy, dpv = h_ref[...], zy_ref[...], dp_ref[...]
        dyb = dpv * hv * _gelu_grad(zy)
        a_s[...] = a
        d_s[...] = dpv * _gelu(zy)

        def group(gi, c):
            rows = pl.ds(pl.multiple_of((tt // SUBLANES - 1 - gi) * SUBLANES, SUBLANES), SUBLANES)
            a8, d8 = a_s[rows, :], d_s[rows, :]
            out = [None] * SUBLANES
            for r in reversed(range(SUBLANES)):
                dht = d8[r:r + 1, :] + c
                out[r] = dht
                c = a8[r:r + 1, :] * dht
            d_s[rows, :] = jnp.concatenate(out, axis=0)
            return c

        carry[0:1, :] = lax.fori_loop(0, tt // SUBLANES, group, carry[0:1, :])
        dht = d_s[...]
        hprev = _shift_down(hv, jnp.where(j > 0, hhalo_ref[...], 0.0), 1)
        ixc = ia * xc
        dlog_a = dht * hprev * a + jnp.where(g["is_t0"], 0.0, dht * ixc * (-(a * a) / mult))
        dia = dht * mult * xc
        dxc = dht * mult * ia
        dra = dlog_a * ((-LRU_C) * g["sp"])
        nl = -vec_ref[3:4, :]
        dvec_ref[3:4, :] += _row_sum(dlog_a * ((-LRU_C) * ra)) * (-_sigmoid(nl))
        dpa = dra * ra * (1.0 - ra)
        dpx = dia * ia * (1.0 - ia)
        dvec_ref[1:2, :] += _row_sum(dpa)
        dvec_ref[2:3, :] += _row_sum(dpx)
        back = []
        for hd, s in enumerate(g["heads"]):
            dwa_ref[hd] += _dot_tn(xc[:, s], dpa[:, s])
            dwx_ref[hd] += _dot_tn(xc[:, s], dpx[:, s])
            back.append(_dot_nt(dpa[:, s], wa_ref[hd]) + _dot_nt(dpx[:, s], wx_ref[hd]))
        dxc = dxc + jnp.concatenate(back, axis=1)
        dvec_ref[0:1, :] += _row_sum(dxc)
        dxb = dxc * cw_ref[CONV_WIDTH - 1:CONV_WIDTH, :]
        for k in range(CONV_WIDTH):
            row = CONV_WIDTH - 1 - k
            dcw_ref[row:row + 1, :] += _row_sum(dxc * g["shifted"][k])
            if k:
                dxb = dxb + _shift_up(dxc, nxt[...], k) * cw_ref[row:row + 1, :]
        nxt[...] = dxc[0:SUBLANES, :]
        dz_ref[:, 0:D] = dxb.astype(dz_ref.dtype)
        dz_ref[:, D:2 * D] = dyb.astype(dz_ref.dtype)

    hhalo = pl.BlockSpec((SUBLANES, D), lambda i: (jnp.maximum(rev(i) * (tt // SUBLANES) - 1, 0), 0))
    wacc = pl.BlockSpec((HEADS, HEAD_DIM, HEAD_DIM), lambda i: (0, 0, 0))
    return pl.pallas_call(
        body, name=name, grid=(nt,),
        in_specs=[rows_rev, sp["x"], sp["y"], sp["halo"], rows_rev, hhalo, sp["cw"], sp["vec"], sp["w"], sp["w"]],
        out_specs=[pl.BlockSpec((tt, 2 * D), lambda i: (rev(i), 0)), _vec_spec(), _vec_spec(), wacc, wacc],
        out_shape=[jax.ShapeDtypeStruct((t, 2 * D), MXU), jax.ShapeDtypeStruct((SUBLANES, D), F32),
                   jax.ShapeDtypeStruct((SUBLANES, D), F32), jax.ShapeDtypeStruct((HEADS, HEAD_DIM, HEAD_DIM), F32),
                   jax.ShapeDtypeStruct((HEADS, HEAD_DIM, HEAD_DIM), F32)],
        scratch_shapes=[pltpu.VMEM((tt, D), F32), pltpu.VMEM((tt, D), F32), pltpu.VMEM((SUBLANES, D), F32),
                        pltpu.VMEM((SUBLANES, D), F32)],
        compiler_params=_params())(dp, z, z, z, h, h, cw, vec, wa, wx)


def _hg_chunk(zq, zf, zi, lb):
    c = HG_CHUNK
    q = _silu(zq)
    sg = _sigmoid(zf)
    fg = lb + (1.0 - lb) * sg
    k = 1.0 - fg
    row, col = lax.broadcasted_iota(jnp.int32, (c, c), 0), lax.broadcasted_iota(jnp.int32, (c, c), 1)
    tri, tri_t = (row >= col).astype(F32), (row <= col).astype(F32)
    b = _dot_exact(tri, jnp.log(fg))
    mid, last = b[c // 2 - 1:c // 2, :], b[c - 1:c, :]
    eq = jnp.exp(jnp.minimum(b - mid, EXP_CLAMP))
    ek = jnp.exp(jnp.minimum(mid - b, EXP_CLAMP))
    eb = jnp.exp(b)
    ed = jnp.exp(last - b)
    return dict(q=q, sg=sg, fg=fg, k=k, v=zi, eq=eq, ek=ek, eb=eb, ed=ed, elast=jnp.exp(last), tri=tri, tri_t=tri_t,
                qe=q * eq, ke=k * ek, qb=q * eb, kd=k * ed)


def _hg_specs(tt, order):
    return [pl.BlockSpec((tt, D), lambda i, col=col: (order(i), col)) for col in range(4)]


def _hg_forward(name, z, vec):
    t = z.shape[0]
    tt = _tile(t, ROW_TILE)
    nt, nc = t // tt, tt // HG_CHUNK
    heads = [slice(h * HEAD_DIM, (h + 1) * HEAD_DIM) for h in range(HEADS)]

    def body(zq_ref, zf_ref, zi_ref, zg_ref, vec_ref, p_ref, o_ref, ss_ref, state):
        @pl.when(pl.program_id(0) == 0)
        def _():
            state[...] = jnp.zeros_like(state)

        lb, gain = vec_ref[0:1, :], vec_ref[1:2, :]

        def chunk(ci, carry):
            rows = pl.ds(pl.multiple_of(ci * HG_CHUNK, HG_CHUNK), HG_CHUNK)
            g = _hg_chunk(zq_ref[rows, :], zf_ref[rows, :], zi_ref[rows, :], lb)
            zg = zg_ref[rows, :]
            causal = g["tri"] > 0.0
            o_parts, p_parts = [], []
            for hd, s in enumerate(heads):
                st = state[hd]
                ss_ref[ci, hd] = st
                att = jnp.where(causal, _dot_nt(g["qe"][:, s], g["ke"][:, s]), 0.0)
                o = _dot(att, g["v"][:, s]) + _dot_nt(g["qb"][:, s], st)
                state[hd] = st * g["elast"][:, s] + _dot_tn(g["v"][:, s], g["kd"][:, s])
                r = lax.rsqrt(jnp.mean(o * o, axis=-1, keepdims=True) + GNORM_EPS)
                o_parts.append(o)
                p_parts.append((o * r) * gain[:, s])
            o_ref[rows, :] = jnp.concatenate(o_parts, axis=1)
            p_ref[rows, :] = (jnp.concatenate(p_parts, axis=1) * _silu(zg)).astype(p_ref.dtype)
            return carry

        lax.fori_loop(0, nc, chunk, 0)

    return pl.pallas_call(
        body, name=name, grid=(nt,), in_specs=_hg_specs(tt, lambda i: i) + [_vec_spec()],
        out_specs=[_rows_spec(tt), _rows_spec(tt), pl.BlockSpec((nc, HEADS, HEAD_DIM, HEAD_DIM), lambda i: (i, 0, 0, 0))],
        out_shape=[jax.ShapeDtypeStruct((t, D), MXU), jax.ShapeDtypeStruct((t, D), F32),
                   jax.ShapeDtypeStruct((t // HG_CHUNK, HEADS, HEAD_DIM, HEAD_DIM), F32)],
        scratch_shapes=[pltpu.VMEM((HEADS, HEAD_DIM, HEAD_DIM), F32)], compiler_params=_params())(z, z, z, z, vec)


def _hg_backward(name, dp, z, o, ss, vec):
    t = z.shape[0]
    tt = _tile(t, ROW_TILE)
    nt, nc = t // tt, tt // HG_CHUNK
    rev = lambda i: nt - 1 - i
    heads = [slice(h * HEAD_DIM, (h + 1) * HEAD_DIM) for h in range(HEADS)]
    rows_rev = pl.BlockSpec((tt, D), lambda i: (rev(i), 0))

    def body(dp_ref, zq_ref, zf_ref, zi_ref, zg_ref, o_ref, ss_ref, vec_ref, dz_ref, acc_ref,
             dstate, dqe_s, dke_s, dqb_s, dkd_s, dv_s, dzg_s, dlast_s):
        @pl.when(pl.program_id(0) == 0)
        def _():
            dstate[...] = jnp.zeros_like(dstate)
            acc_ref[...] = jnp.zeros_like(acc_ref)

        lb, gain = vec_ref[0:1, :], vec_ref[1:2, :]

        def chunk(cr, carry):
            ci = nc - 1 - cr
            rows = pl.ds(pl.multiple_of(ci * HG_CHUNK, HG_CHUNK), HG_CHUNK)
            zq, zg = zq_ref[rows, :], zg_ref[rows, :]
            g = _hg_chunk(zq, zf_ref[rows, :], zi_ref[rows, :], lb)
            ov, dpv = o_ref[rows, :], dp_ref[rows, :]
            causal = g["tri"] > 0.0
            don = dpv * _silu(zg)
            dgate = dpv * _silu_grad(zg)
            for hd, s in enumerate(heads):
                oh = ov[:, s]
                r = lax.rsqrt(jnp.mean(oh * oh, axis=-1, keepdims=True) + GNORM_EPS)
                on = oh * r
                dzg_s[:, s] = dgate[:, s] * (on * gain[:, s])
                acc_ref[1:2, s] += _row_sum(don[:, s] * on)
                dtmp = don[:, s] * gain[:, s]
                do = r * (dtmp - on * jnp.mean(dtmp * on, axis=-1, keepdims=True))
                st, dst = ss_ref[ci, hd], dstate[hd]
                qe, ke, qb, kd, v = g["qe"][:, s], g["ke"][:, s], g["qb"][:, s], g["kd"][:, s], g["v"][:, s]
                att = jnp.where(causal, _dot_nt(qe, ke), 0.0)
                datt = jnp.where(causal, _dot_nt(do, v), 0.0)
                dv_s[:, s] = _dot_tn(att, do) + _dot_nt(kd, dst)
                dqe_s[:, s] = _dot(datt, ke)
                dke_s[:, s] = _dot_tn(datt, qe)
                dqb_s[:, s] = _dot(do, st)
                dkd_s[:, s] = _dot(v, dst)
                dlast_s[0:1, s] = g["elast"][:, s] * _row_sum(dst * st)
                dstate[hd] = dst * g["elast"][:, s] + _dot_tn(do, qb)
            dqe, dke, dqb, dkd = dqe_s[...], dke_s[...], dqb_s[...], dkd_s[...]
            dq = dqe * g["eq"] + dqb * g["eb"]
            dk = dke * g["ek"] + dkd * g["ed"]
            dkdk = dkd * g["kd"]
            db = dqe * g["qe"] - dke * g["ke"] + dqb * g["qb"] - dkdk
            dlogf = _dot_exact(g["tri_t"], db) + (dlast_s[0:1, :] + _row_sum(dkdk))
            dfg = dlogf / g["fg"] - dk
            sg = g["sg"]
            acc_ref[0:1, :] += _row_sum(dfg * (1.0 - sg))
            dz_ref[rows, 0:D] = (dq * _silu_grad(zq)).astype(dz_ref.dtype)
            dz_ref[rows, D:2 * D] = (dfg * (1.0 - lb) * sg * (1.0 - sg)).astype(dz_ref.dtype)
            dz_ref[rows, 2 * D:3 * D] = dv_s[...].astype(dz_ref.dtype)
            dz_ref[rows, 3 * D:4 * D] = dzg_s[...].astype(dz_ref.dtype)
            return carry

        lax.fori_loop(0, nc, chunk, 0)

    chunk_buf = pltpu.VMEM((HG_CHUNK, D), F32)
    return pl.pallas_call(
        body, name=name, grid=(nt,),
        in_specs=[rows_rev] + _hg_specs(tt, rev) + [rows_rev, pl.BlockSpec((nc, HEADS, HEAD_DIM, HEAD_DIM), lambda i: (rev(i), 0, 0, 0)),
                                                  _vec_spec()],
        out_specs=[pl.BlockSpec((tt, 4 * D), lambda i: (rev(i), 0)), _vec_spec()],
        out_shape=[jax.ShapeDtypeStruct((t, 4 * D), MXU), jax.ShapeDtypeStruct((SUBLANES, D), F32)],
        scratch_shapes=[pltpu.VMEM((HEADS, HEAD_DIM, HEAD_DIM), F32)] + [chunk_buf] * 6 + [pltpu.VMEM((SUBLANES, D), F32)],
        compiler_params=_params())(dp, z, z, z, z, o, ss, vec)


def _mod_forward(name, c_all, mod_w, lower):
    depth, _, sw = mod_w.shape

    def body(c_ref, w_ref, lo_ref, cs_ref, mod_ref, lb_ref):
        cs = _silu(c_ref[...])
        mod_ref[...] = _dot(cs, w_ref[...])

        @pl.when(pl.program_id(0) == 0)
        def _():
            cs_ref[...] = cs
            lo = lo_ref[...]
            e = jnp.exp(lo - jnp.max(lo, axis=0, keepdims=True))
            sm = e / jnp.sum(e, axis=0, keepdims=True)
            lb_ref[0:1, :] = jnp.zeros((1, D), F32)
            for l in range(1, depth):
                lb_ref[l:l + 1, :] = lb_ref[l - 1:l, :] + sm[l:l + 1, :]

    return pl.pallas_call(
        body, name=name, grid=(depth,),
        in_specs=[pl.BlockSpec((N_DEV, D), lambda l: (0, 0)), pl.BlockSpec((None, D, sw), lambda l: (l, 0, 0)),
                  pl.BlockSpec((depth, D), lambda l: (0, 0))],
        out_specs=[pl.BlockSpec((N_DEV, D), lambda l: (0, 0)), pl.BlockSpec((None, N_DEV, sw), lambda l: (l, 0, 0)),
                   pl.BlockSpec((depth, D), lambda l: (0, 0))],
        out_shape=[jax.ShapeDtypeStruct((N_DEV, D), F32), jax.ShapeDtypeStruct((depth, N_DEV, sw), F32),
                   jax.ShapeDtypeStruct((depth, D), F32)], compiler_params=_params())(c_all, mod_w, lower)


def _mod_weight_grad(name, cs_t, dmod):
    depth, pad, sw = dmod.shape

    def body(c_ref, d_ref, g_ref):
        g_ref[...] = _dot(c_ref[...], d_ref[...])

    return pl.pallas_call(
        body, name=name, grid=(depth,),
        in_specs=[pl.BlockSpec((D, pad), lambda l: (0, 0)), pl.BlockSpec((None, pad, sw), lambda l: (l, 0, 0))],
        out_specs=pl.BlockSpec((None, D, sw), lambda l: (l, 0, 0)),
        out_shape=jax.ShapeDtypeStruct((depth, D, sw), F32), compiler_params=_params())(cs_t, dmod)


def _lower_bound_grad(name, lower, dlb):
    depth = lower.shape[0]

    def body(lo_ref, d_ref, out_ref):
        lo = lo_ref[...]
        e = jnp.exp(lo - jnp.max(lo, axis=0, keepdims=True))
        sm = e / jnp.sum(e, axis=0, keepdims=True)
        out_ref[...] = jnp.zeros_like(out_ref)
        dsm = [jnp.zeros((1, D), F32)]
        for l in range(1, depth):
            tail = d_ref[l:l + 1, :]
            for m in range(l + 1, depth):
                tail = tail + d_ref[m:m + 1, :]
            dsm.append(tail)
        inner = sm[1:2, :] * dsm[1]
        for l in range(2, depth):
            inner = inner + sm[l:l + 1, :] * dsm[l]
        for l in range(depth):
            out_ref[l:l + 1, :] = sm[l:l + 1, :] * (dsm[l] - inner)
        for j in range(SUBLANES - depth):
            row = d_ref[depth + j:depth + j + 1, :]
            tot = row[:, 0:HEAD_DIM]
            for hd in range(1, HEADS):
                tot = tot + row[:, hd * HEAD_DIM:(hd + 1) * HEAD_DIM]
            out_ref[depth + j:depth + j + 1, 0:HEAD_DIM] = tot

    return pl.pallas_call(body, name=name, out_shape=jax.ShapeDtypeStruct((SUBLANES, D), F32))(lower, dlb)


def _adamw(name, parts, w, m, v, layer=None, prev=None):
    p, r, c = parts.shape
    tr = _tile(r, max(SUBLANES, ADAMW_STEP_BYTES // (4 * c * (p + 7))))
    stacked = layer is not None
    n_prev = 4 if prev is not None else 0

    def body(*refs):
        parts_ref, w_ref, m_ref, v_ref = refs[:4]
        g_ref, d_ref, m_out, v_out = refs[4 + n_prev:]
        g = parts_ref[0].astype(F32)
        for q in range(1, p):
            g = g + parts_ref[q].astype(F32)
        m2 = ADAM_B1 * m_ref[...] + (1.0 - ADAM_B1) * g
        v2 = ADAM_B2 * v_ref[...] + (1.0 - ADAM_B2) * (g * g)
        m_hat = m2 / (1.0 - ADAM_B1 ** ADAM_STEP)
        v_hat = v2 / (1.0 - ADAM_B2 ** ADAM_STEP)
        g_ref[...] = g
        d_ref[...] = -ADAM_LR * (m_hat / (jnp.sqrt(v_hat) + ADAM_EPS) + ADAM_WD * w_ref[...])
        m_out[...] = m2
        v_out[...] = v2

    if stacked:
        spec = pl.BlockSpec((None, tr, c), lambda i: (layer, i, 0))
    else:
        spec = pl.BlockSpec((tr, c), lambda i: (i, 0))
    return pl.pallas_call(
        body, name=name, grid=(r // tr,),
        in_specs=[pl.BlockSpec((p, tr, c), lambda i: (0, i, 0)), spec, spec, spec] + [ANY] * n_prev, out_specs=[spec] * 4,
        out_shape=[jax.ShapeDtypeStruct(w.shape, F32)] * 4,
        input_output_aliases={4 + q: q for q in range(n_prev)}, compiler_params=_params(),
    )(parts, w, m, v, *(prev or []))


def _vec(*rows):
    rows = [r.reshape(1, D).astype(F32) for r in rows]
    return jnp.concatenate(rows + [jnp.zeros((SUBLANES - len(rows), D), F32)], axis=0)


def _pad_rows(a, rows=SUBLANES):
    a = a.reshape(-1, a.shape[-1])
    return jnp.concatenate([a, jnp.zeros((rows - a.shape[0], a.shape[1]), a.dtype)], axis=0) if a.shape[0] < rows else a


def kernel(x, c, mod_w, mod_b, norm_mix, norm_mlp, norm_final, rg_w_in, rg_conv_w, rg_conv_b, rg_w_a, rg_b_a, rg_w_x, rg_b_x, rg_lambda, rg_w_out, hg_w_in, hg_lower_bounds, hg_gnorm, hg_w_out, mlp_w1, mlp_w2, loss_target, m_mod_w, m_mod_b, m_norm_mix, m_norm_mlp, m_norm_final, m_rg_w_in, m_rg_conv_w, m_rg_conv_b, m_rg_w_a, m_rg_b_a, m_rg_w_x, m_rg_b_x, m_rg_lambda, m_rg_w_out, m_hg_w_in, m_hg_lower_bounds, m_hg_gnorm, m_hg_w_out, m_mlp_w1, m_mlp_w2, v_mod_w, v_mod_b, v_norm_mix, v_norm_mlp, v_norm_final, v_rg_w_in, v_rg_conv_w, v_rg_conv_b, v_rg_w_a, v_rg_b_a, v_rg_w_x, v_rg_b_x, v_rg_lambda, v_rg_w_out, v_hg_w_in, v_hg_lower_bounds, v_hg_gnorm, v_hg_w_out, v_mlp_w1, v_mlp_w2):
    me = 4 * lax.axis_index("x") + 2 * lax.axis_index("y") + lax.axis_index("c")
    x0 = x[0]
    target = loss_target[0]
    n_rg, n_hg = rg_w_in.shape[0], hg_w_in.shape[0]
    sw_mod = mod_w.shape[2]

    c_all, cw_all = _all_gather([_pad_rows(c), rg_conv_w.reshape(n_rg * CONV_WIDTH, -1)], "gather_cond")
    c_all = c_all[:, 0, :]
    conv_w = cw_all.transpose(1, 0, 2).reshape(n_rg, CONV_WIDTH, D)
    cs_all, mod_part, lb_all = _mod_forward("mod_forward", c_all, mod_w, hg_lower_bounds)
    (mod_gathered,) = _all_gather([mod_part.reshape(DEPTH * N_DEV, sw_mod)], "gather_mod")
    mod_mine = lax.dynamic_index_in_dim(mod_gathered.reshape(N_DEV, DEPTH, N_DEV, sw_mod), me, axis=2, keepdims=False)
    mod = mod_mine.transpose(1, 0, 2).reshape(DEPTH, 6, D) + mod_b.reshape(DEPTH, 6, D)

    weights = []
    for layer in range(DEPTH):
        j = layer // 2
        w_in, w_out = (rg_w_in[j], rg_w_out[j]) if layer % 2 == 0 else (hg_w_in[j], hg_w_out[j])
        shards = [w_in.astype(MXU), w_out.astype(MXU), mlp_w1[layer].astype(MXU), mlp_w2[layer].astype(MXU)]
        g_in, g_out, g_w1, g_w2 = _all_gather(shards, f"gather_weights_{layer % 2}")
        weights.append(dict(w_in=g_in, w_out=g_out.reshape(D, D), w1=g_w1, w2=g_w2.reshape(-1, D)))

    saved = []
    xl = x0
    for layer in range(DEPTH):
        j, wt = layer // 2, weights[layer]
        is_rg = layer % 2 == 0
        s = dict(x=xl)
        s["vec_mix"] = _vec(norm_mix[layer], mod[layer, 1], mod[layer, 0])
        s["vec_mlp"] = _vec(norm_mlp[layer], mod[layer, 4], mod[layer, 3])
        s["gate_mix"], s["gate_mlp"] = _vec(mod[layer, 2]), _vec(mod[layer, 5])
        s["h"] = _norm_mod("norm_mod", xl, s["vec_mix"])
        if is_rg:
            (s["z"],) = _mm_cols("rg_in", s["h"], wt["w_in"])
            s["cw"] = conv_w[j]
            s["vec"] = _vec(rg_conv_b[j], rg_b_a[j], rg_b_x[j], rg_lambda[j])
            s["wa"], s["wx"] = rg_w_a[j].astype(MXU), rg_w_x[j].astype(MXU)
            s["p"], s["hr"] = _rg_forward("rg_forward", s["z"], s["cw"], s["vec"], s["wa"], s["wx"])
        else:
            (s["z"],) = _mm_cols("hg_in", s["h"], wt["w_in"])
            s["vec"] = _vec(lb_all[layer], jnp.tile(hg_gnorm[j], HEADS))
            s["p"], s["o"], s["ss"] = _hg_forward("hg_forward", s["z"], s["vec"])
        s["y"], s["x1"] = _mm_rows("mix_out", s["p"], wt["w_out"], _ep_residual, (F32, F32), extras=(xl, s["gate_mix"]))
        s["h2"] = _norm_mod("norm_mod", s["x1"], s["vec_mlp"])
        s["a"], s["s"] = _mm_cols("mlp_in", s["h2"], wt["w1"], _ep_relu2, (MXU, MXU))
        s["ff"], xl = _mm_rows("mlp_out", s["s"], wt["w2"], _ep_residual, (F32, F32), extras=(s["x1"], s["gate_mlp"]))
        saved.append(s)

    dx, head = _loss_head("loss_head", xl, target, _vec(norm_final))

    landed = [None] * DEPTH
    small = [None] * DEPTH
    for layer in reversed(range(DEPTH)):
        j, wt, s = layer // 2, weights[layer], saved[layer]
        is_rg = layer % 2 == 0
        dff, g_mlp = _residual_grad("residual_grad", dx, s["ff"], s["gate_mlp"])
        da = _mm_rows_t("mlp_out_t", dff, wt["w2"], _ep_relu2_grad, MXU, extras=(s["a"],))
        dw2 = _mm_grad("mlp_out_grad", s["s"], dff, False).reshape(N_DEV, -1, D)
        dw1 = _mm_grad("mlp_in_grad", s["h2"], da, True)
        dh2 = _mm_cols_t("mlp_in_t", da, wt["w1"])
        dx1, n_mlp = _norm_mod_grad("norm_mod_grad", dh2, s["x1"], s["vec_mlp"], dx)
        dyb, g_mix = _residual_grad("residual_grad", dx1, s["y"], s["gate_mix"])
        dp = _mm_rows_t("mix_out_t", dyb, wt["w_out"])
        dw_out = _mm_grad("mix_out_grad", s["p"], dyb, False).reshape(N_DEV, -1, D)
        if is_rg:
            dz, dvec, dcw, dwa, dwx = _rg_backward("rg_backward", dp, s["z"], s["hr"], s["cw"], s["vec"], s["wa"], s["wx"])
            mixer_small = dict(dvec=dvec, dcw=dcw, dwa=dwa, dwx=dwx)
            dw_in = _mm_grad("rg_in_grad", s["h"], dz, True)
            dh = _mm_cols_t("rg_in_t", dz, wt["w_in"])
        else:
            dz, dvec = _hg_backward("hg_backward", dp, s["z"], s["o"], s["ss"], s["vec"])
            mixer_small = dict(dvec=dvec)
            dw_in = _mm_grad("hg_in_grad", s["h"], dz, True)
            dh = _mm_cols_t("hg_in_t", dz, wt["w_in"])
        dx, n_mix = _norm_mod_grad("norm_mod_grad", dh, s["x"], s["vec_mix"], dx1)
        landed[layer] = _all_to_all([dw_in, dw_out, dw1, dw2], f"exchange_grads_{layer % 2}")
        small[layer] = dict(g_mlp=g_mlp, n_mlp=n_mlp, g_mix=g_mix, n_mix=n_mix, **mixer_small)

    dlb_rows = [jnp.zeros((1, D), F32) if l % 2 == 0 else small[l]["dvec"][0:1] for l in range(DEPTH)]
    dgn_rows = [small[2 * j + 1]["dvec"][1:2] for j in range(n_hg)]
    lb_grad = _lower_bound_grad("lower_bound_grad", hg_lower_bounds, _pad_rows(jnp.concatenate(dlb_rows + dgn_rows, axis=0)))
    dmod = jnp.stack([jnp.concatenate([small[l]["n_mix"][2], small[l]["n_mix"][1], small[l]["g_mix"][0],
                                       small[l]["n_mlp"][2], small[l]["n_mlp"][1], small[l]["g_mlp"][0]]) for l in range(DEPTH)])
    groups = [
        dmod.reshape(DEPTH * 6, D),
        jnp.stack([small[l]["n_mix"][0] for l in range(DEPTH)]),
        jnp.stack([small[l]["n_mlp"][0] for l in range(DEPTH)]),
        head[0:1],
        jnp.stack([small[2 * j]["dvec"][0] for j in range(n_rg)]),
        jnp.stack([small[2 * j]["dvec"][1] for j in range(n_rg)]),
        jnp.stack([small[2 * j]["dvec"][2] for j in range(n_rg)]),
        jnp.stack([small[2 * j]["dvec"][3] for j in range(n_rg)]),
        lb_grad[0:DEPTH],
        jnp.concatenate([lb_grad[DEPTH + j:DEPTH + j + 1, 0:HEAD_DIM] for j in range(n_hg)]
                        + [jnp.zeros((1, D - n_hg * HEAD_DIM), F32)], axis=1),
        jnp.concatenate([small[2 * j]["dcw"][0:CONV_WIDTH] for j in range(n_rg)], axis=0),
        head[1:2],
    ]
    params = [mod_b, norm_mix, norm_mlp, norm_final, rg_conv_b, rg_b_a, rg_b_x, rg_lambda, hg_lower_bounds, hg_gnorm]
    moms = [m_mod_b, m_norm_mix, m_norm_mlp, m_norm_final, m_rg_conv_b, m_rg_b_a, m_rg_b_x, m_rg_lambda, m_hg_lower_bounds, m_hg_gnorm]
    vars_ = [v_mod_b, v_norm_mix, v_norm_mlp, v_norm_final, v_rg_conv_b, v_rg_b_a, v_rg_b_x, v_rg_lambda, v_hg_lower_bounds, v_hg_gnorm]
    offsets, rows_of, at = [], [], 0
    for g in groups:
        offsets.append(at)
        rows_of.append(g.shape[0])
        at += -(-g.shape[0] // SUBLANES) * SUBLANES
    packed = jnp.concatenate([_pad_rows(g, -(-g.shape[0] // SUBLANES) * SUBLANES) for g in groups], axis=0)
    dwa_all = jnp.stack([small[2 * j]["dwa"] for j in range(n_rg)]).reshape(-1, HEAD_DIM)
    dwx_all = jnp.stack([small[2 * j]["dwx"] for j in range(n_rg)]).reshape(-1, HEAD_DIM)
    small_parts, wa_parts, wx_parts = _all_gather([packed, dwa_all, dwx_all], "gather_small_grads")

    def pack_like(arrs):
        out = []
        for g_rows, off, a in zip(rows_of, offsets, arrs):
            flat = a.reshape(-1)
            flat = jnp.concatenate([flat, jnp.zeros((g_rows * D - flat.shape[0],), F32)])
            out.append(_pad_rows(flat.reshape(g_rows, D), -(-g_rows // SUBLANES) * SUBLANES))
        rest = packed.shape[0] - sum(o.shape[0] for o in out)
        return jnp.concatenate(out + [jnp.zeros((rest, D), F32)], axis=0)

    small_out = _adamw("adamw_small", small_parts, pack_like(params), pack_like(moms), pack_like(vars_))

    def unpack(q, idx, like):
        rows = small_out[q][offsets[idx]:offsets[idx] + rows_of[idx]]
        return rows.reshape(-1)[:like.size].reshape(like.shape)

    loss = jnp.sum(small_out[0][offsets[11]])
    results = {}
    names = ["mod_b", "norm_mix", "norm_mlp", "norm_final", "rg_conv_b", "rg_b_a", "rg_b_x", "rg_lambda", "hg_lower_bounds", "hg_gnorm"]
    for idx, (nm, like) in enumerate(zip(names, params)):
        results[nm] = [unpack(q, idx, like) for q in range(4)]

    cw_parts = lax.dynamic_slice_in_dim(small_parts[:, offsets[10]:offsets[10] + n_rg * CONV_WIDTH, :], me * (D // N_DEV), D // N_DEV, axis=2)
    shp = rg_conv_w.shape
    results["rg_conv_w"] = [o.reshape(shp) for o in _adamw(
        "adamw_conv", cw_parts, rg_conv_w.reshape(-1, shp[-1]), m_rg_conv_w.reshape(-1, shp[-1]), v_rg_conv_w.reshape(-1, shp[-1]))]
    shp = rg_w_a.shape
    results["rg_w_a"] = [o.reshape(shp) for o in _adamw(
        "adamw_gate", wa_parts, rg_w_a.reshape(-1, HEAD_DIM), m_rg_w_a.reshape(-1, HEAD_DIM), v_rg_w_a.reshape(-1, HEAD_DIM))]
    results["rg_w_x"] = [o.reshape(shp) for o in _adamw(
        "adamw_gate", wx_parts, rg_w_x.reshape(-1, HEAD_DIM), m_rg_w_x.reshape(-1, HEAD_DIM), v_rg_w_x.reshape(-1, HEAD_DIM))]

    dmod_all = small_parts[:, 0:DEPTH * 6, :].reshape(N_DEV, DEPTH, 6 * D)
    dmod_cols = lax.dynamic_slice_in_dim(dmod_all, me * sw_mod, sw_mod, axis=2).transpose(1, 0, 2)
    pad = HEAD_DIM - N_DEV
    dmod_pad = jnp.concatenate([dmod_cols, jnp.zeros((DEPTH, pad, sw_mod), F32)], axis=1).astype(MXU)
    cs_t = jnp.concatenate([cs_all.T, jnp.zeros((D, pad), F32)], axis=1).astype(MXU)
    g_mod_w = _mod_weight_grad("mod_weight_grad", cs_t, dmod_pad)
    results["mod_w"] = [o.reshape(mod_w.shape) for o in _adamw(
        "adamw_mod", g_mod_w.reshape(1, -1, sw_mod), mod_w.reshape(-1, sw_mod), m_mod_w.reshape(-1, sw_mod), v_mod_w.reshape(-1, sw_mod))]

    big = dict(rg_w_in=(rg_w_in, m_rg_w_in, v_rg_w_in), rg_w_out=(rg_w_out, m_rg_w_out, v_rg_w_out),
               hg_w_in=(hg_w_in, m_hg_w_in, v_hg_w_in), hg_w_out=(hg_w_out, m_hg_w_out, v_hg_w_out),
               mlp_w1=(mlp_w1, m_mlp_w1, v_mlp_w1), mlp_w2=(mlp_w2, m_mlp_w2, v_mlp_w2))
    for layer in reversed(range(DEPTH)):
        mixer = "rg" if layer % 2 == 0 else "hg"
        l_in, l_out, l_w1, l_w2 = landed[layer]
        for nm, parts, idx in ((f"{mixer}_w_in", l_in, layer // 2), (f"{mixer}_w_out", l_out, layer // 2),
                               ("mlp_w1", l_w1, layer), ("mlp_w2", l_w2, layer)):
            w, m, v = big[nm]
            results[nm] = _adamw(f"adamw_{nm}", parts, w, m, v, layer=idx, prev=results.get(nm))

    order = ["mod_w", "mod_b", "norm_mix", "norm_mlp", "norm_final", "rg_w_in", "rg_conv_w", "rg_conv_b", "rg_w_a", "rg_b_a", "rg_w_x",
             "rg_b_x", "rg_lambda", "rg_w_out", "hg_w_in", "hg_lower_bounds", "hg_gnorm", "hg_w_out", "mlp_w1", "mlp_w2"]
    return (loss, dx[None], *[results[n][0] for n in order], *[results[n][1] for n in order],
            *[results[n][2] for n in order], *[results[n][3] for n in order])
```

```python
import functools

import jax
import jax.numpy as jnp
from jax import lax
from jax.experimental import pallas as pl
from jax.experimental.pallas import tpu as pltpu
from jax.experimental.pallas import tpu_sc as plsc

F32 = jnp.float32
MXU = jnp.bfloat16

N_DEV = 8
D = 1024
DEPTH = 4
HEADS = 8
HEAD_DIM = 128
CONV_WIDTH = 4
LRU_C = 8.0
HG_CHUNK = 64
NORM_EPS = 1e-6
GNORM_EPS = 1e-5
ADAM_LR = 0.001
ADAM_B1 = 0.9
ADAM_B2 = 0.999
ADAM_EPS = 1e-08
ADAM_WD = 0.01
ADAM_STEP = 10
GELU_C = 0.7978845608028654
GELU_K = 0.044715
EXP_CLAMP = 80.0
SUBLANES = 8
VMEM_LIMIT = 48 * 1024 * 1024
ROW_TILE = 256
MM_TILE = 1024
ADAMW_STEP_BYTES = 8 * 1024 * 1024

MESH = pl.DeviceIdType.MESH
ANY = pl.BlockSpec(memory_space=pl.ANY)


def _params():
    return pltpu.CompilerParams(vmem_limit_bytes=VMEM_LIMIT)


def _tile(n, target):
    if n <= target:
        return n
    t = target // SUBLANES * SUBLANES
    while n % t:
        t -= SUBLANES
    return t


def _sigmoid(x):
    return 1.0 / (1.0 + jnp.exp(-x))


def _silu(x):
    return x * _sigmoid(x)


def _silu_grad(x):
    s = _sigmoid(x)
    return s * (1.0 + x * (1.0 - s))


def _gelu(y):
    return 0.5 * y * (1.0 + jnp.tanh(GELU_C * (y + GELU_K * y * y * y)))


def _gelu_grad(y):
    th = jnp.tanh(GELU_C * (y + GELU_K * y * y * y))
    return 0.5 * (1.0 + th) + 0.5 * y * (1.0 - th * th) * GELU_C * (1.0 + 3.0 * GELU_K * y * y)


def _dot(a, b):
    return lax.dot_general(a.astype(MXU), b.astype(MXU), (((1,), (0,)), ((), ())), preferred_element_type=F32)


def _dot_nt(a, b):
    return lax.dot_general(a.astype(MXU), b.astype(MXU), (((1,), (1,)), ((), ())), preferred_element_type=F32)


def _dot_tn(a, b):
    return lax.dot_general(a.astype(MXU), b.astype(MXU), (((0,), (0,)), ((), ())), preferred_element_type=F32)


def _dot_exact(tri, x):
    t = tri.astype(MXU)
    hi = x.astype(MXU)
    r1 = x - hi.astype(F32)
    mid = r1.astype(MXU)
    lo = (r1 - mid.astype(F32)).astype(MXU)
    dn = (((1,), (0,)), ((), ()))
    return (lax.dot_general(t, hi, dn, preferred_element_type=F32) + lax.dot_general(t, mid, dn, preferred_element_type=F32)
            + lax.dot_general(t, lo, dn, preferred_element_type=F32))


def _row_sum(v):
    return jnp.sum(v, axis=0, keepdims=True)


def _handshake(partners):
    barrier = pltpu.get_barrier_semaphore()
    for p in partners:
        pl.semaphore_signal(barrier, inc=1, device_id=p, device_id_type=MESH)
    pl.semaphore_wait(barrier, len(partners))


def _gather_body(n, per_array_sems, handshake):
    def body(*refs):
        ins, outs = refs[:n], refs[n:2 * n]
        send_sems, recv_sems, local_sems = refs[2 * n:]
        x, y, c = lax.axis_index("x"), lax.axis_index("y"), lax.axis_index("c")
        me, sibling = (x, y, c), (x, y, 1 - c)
        chips = [(1 - x, y), (x, 1 - y), (1 - x, 1 - y)]
        if handshake:
            _handshake([sibling] + [(*chip, c) for chip in chips])

        def sem(sems, a, k):
            return sems.at[a, k] if per_array_sems else sems.at[k]

        def slot(a, p):
            return outs[a].at[4 * p[0] + 2 * p[1] + p[2]]

        def copy(a, k, block, to, src=None):
            return pltpu.make_async_remote_copy(
                src_ref=slot(a, block) if src is None else src, dst_ref=slot(a, block),
                send_sem=sem(send_sems, a, k), recv_sem=sem(recv_sems, a, k), device_id=to, device_id_type=MESH)

        mine = [pltpu.make_async_copy(ins[a], slot(a, me), local_sems.at[a if per_array_sems else 0]) for a in range(n)]
        for cp in mine:
            cp.start()
        first = []
        for a in range(n):
            first.append(copy(a, 0, me, sibling, src=ins[a]))
            first += [copy(a, 1 + j, me, (*chip, c), src=ins[a]) for j, chip in enumerate(chips)]
        for cp in first:
            cp.start()
        passed = []
        for j, chip in enumerate(chips):
            for a in range(n):
                copy(a, 1 + j, (*chip, c), me).wait_recv()
            for a in range(n):
                cp = copy(a, 4 + j, (*chip, c), sibling)
                cp.start()
                passed.append(cp)
        for a in range(n):
            copy(a, 0, sibling, me).wait_recv()
        for j, chip in enumerate(chips):
            for a in range(n):
                copy(a, 4 + j, (*chip, 1 - c), me).wait_recv()
        for cp in first + passed:
            cp.wait_send()
        for cp in mine:
            cp.wait()

    return body


def _exchange_body(n, per_array_sems, handshake):
    flips = [(fx, fy, fc) for fx in (0, 1) for fy in (0, 1) for fc in (0, 1)][1:]

    def body(*refs):
        ins, outs = refs[:n], refs[n:2 * n]
        send_sems, recv_sems, local_sems = refs[2 * n:]
        x, y, c = lax.axis_index("x"), lax.axis_index("y"), lax.axis_index("c")
        me = 4 * x + 2 * y + c

        def flip(v, f):
            return 1 - v if f else v

        peers = [(flip(x, fx), flip(y, fy), flip(c, fc)) for fx, fy, fc in flips]
        pids = [4 * p[0] + 2 * p[1] + p[2] for p in peers]
        if handshake:
            _handshake(peers)

        def sem(sems, a, k):
            return sems.at[a, k] if per_array_sems else sems.at[k]

        def copy(a, k, landing):
            return pltpu.make_async_remote_copy(
                src_ref=ins[a].at[pids[k]], dst_ref=outs[a].at[landing], send_sem=sem(send_sems, a, k),
                recv_sem=sem(recv_sems, a, k), device_id=peers[k], device_id_type=MESH)

        mine = [pltpu.make_async_copy(ins[a].at[me], outs[a].at[me], local_sems.at[a if per_array_sems else 0])
                for a in range(n)]
        for cp in mine:
            cp.start()
        sent = [copy(a, k, me) for a in range(n) for k in range(7)]
        for cp in sent:
            cp.start()
        for a in range(n):
            for k in range(7):
                copy(a, k, pids[k]).wait_recv()
        for cp in sent:
            cp.wait_send()
        for cp in mine:
            cp.wait()

    return body


def _all_gather(arrs, name):
    n = len(arrs)
    return pl.pallas_call(
        _gather_body(n, True, False), name=name, in_specs=[ANY] * n, out_specs=[ANY] * n,
        out_shape=[jax.ShapeDtypeStruct((N_DEV,) + a.shape, a.dtype) for a in arrs],
        scratch_shapes=[pltpu.SemaphoreType.DMA((n, 7)), pltpu.SemaphoreType.DMA((n, 7)), pltpu.SemaphoreType.DMA((n,))],
    )(*arrs)


def _on_sequencer(body, arrs, out_type, name, collective_id):
    return pl.kernel(
        body, name=name, out_type=out_type, mesh=plsc.ScalarSubcoreMesh(axis_name="sequencer", num_cores=1),
        scratch_types=[pltpu.SemaphoreType.DMA((7,)), pltpu.SemaphoreType.DMA((7,)), pltpu.SemaphoreType.DMA((1,))],
        compiler_params=pltpu.CompilerParams(collective_id=collective_id))(*arrs)


def _all_gather_async(arrs, name, collective_id):
    out_type = [jax.ShapeDtypeStruct((N_DEV,) + a.shape, a.dtype) for a in arrs]
    return _on_sequencer(_gather_body(len(arrs), False, True), arrs, out_type, name, collective_id)


def _exchange_async(arrs, name, collective_id):
    out_type = [jax.ShapeDtypeStruct(a.shape, a.dtype) for a in arrs]
    return _on_sequencer(_exchange_body(len(arrs), False, True), arrs, out_type, name, collective_id)


NN = (((1,), (0,)), ((), ()))
NT = (((1,), (1,)), ((), ()))
TN = (((0,), (0,)), ((), ()))


def _matmul(name, a, b, dims, grid, a_spec, b_spec, outs, epilogue, extras=(), acc_shape=None):
    n_in = 2 + len(extras)
    n_out = len(outs)
    nk = grid[2]

    def body(*refs):
        a_ref, b_ref = refs[0], refs[1]
        ex, out_refs = refs[2:n_in], refs[n_in:n_in + n_out]

        def part():
            return lax.dot_general(a_ref[...], b_ref[...], dims, preferred_element_type=F32)

        if nk == 1:
            epilogue(part(), ex, out_refs)
            return
        acc = refs[n_in + n_out]
        k = pl.program_id(2)

        @pl.when(k == 0)
        def _():
            acc[...] = part()

        @pl.when(k > 0)
        def _():
            acc[...] += part()

        @pl.when(k == nk - 1)
        def _():
            epilogue(acc[...], ex, out_refs)

    return pl.pallas_call(
        body, name=name, grid=grid, in_specs=[a_spec, b_spec] + [s for _, s in extras], out_specs=[s for _, _, s in outs],
        out_shape=[jax.ShapeDtypeStruct(sh, dt) for sh, dt, _ in outs],
        scratch_shapes=[] if nk == 1 else [pltpu.VMEM(acc_shape, F32)], compiler_params=_params(),
    )(a, b, *[e for e, _ in extras])


def _ep_store(acc, ex, outs):
    outs[0][...] = acc.astype(outs[0].dtype)


def _ep_residual(acc, ex, outs):
    outs[0][...] = acc
    outs[1][...] = ex[0][...] + ex[1][0:1, :] * acc


def _ep_relu2(acc, ex, outs):
    outs[0][...] = acc.astype(outs[0].dtype)
    r = jnp.maximum(acc, 0.0)
    outs[1][...] = (r * r).astype(outs[1].dtype)


def _ep_relu2_grad(acc, ex, outs):
    outs[0][...] = (acc * (2.0 * jnp.maximum(ex[0][...].astype(F32), 0.0))).astype(outs[0].dtype)


def _mm_cols(name, a, wg, epilogue=_ep_store, out_dtypes=(F32,)):
    m, k = a.shape
    sw = wg.shape[2]
    tm = _tile(m, MM_TILE)
    spec = pl.BlockSpec((tm, sw), lambda i, j, kk: (i, j))
    return _matmul(name, a, wg, NN, (m // tm, N_DEV, 1), pl.BlockSpec((tm, k), lambda i, j, kk: (i, 0)),
                   pl.BlockSpec((None, k, sw), lambda i, j, kk: (j, 0, 0)),
                   [((m, N_DEV * sw), dt, spec) for dt in out_dtypes], epilogue)


def _mm_cols_t(name, a, wg, out_dtype=F32):
    m = a.shape[0]
    _, r, sw = wg.shape
    tm, tn = _tile(m, MM_TILE), _tile(r, MM_TILE)
    return _matmul(name, a, wg, NT, (m // tm, r // tn, N_DEV), pl.BlockSpec((tm, sw), lambda i, j, kk: (i, kk)),
                   pl.BlockSpec((None, tn, sw), lambda i, j, kk: (kk, j, 0)),
                   [((m, r), out_dtype, pl.BlockSpec((tm, tn), lambda i, j, kk: (i, j)))], _ep_store, acc_shape=(tm, tn))[0]


def _mm_rows(name, a, w, epilogue, outs_dtypes, extras=()):
    m, k = a.shape
    n = w.shape[1]
    tm, tn, tk = _tile(m, MM_TILE), _tile(n, MM_TILE // 2), _tile(k, MM_TILE)
    spec = pl.BlockSpec((tm, tn), lambda i, j, kk: (i, j))
    return _matmul(name, a, w, NN, (m // tm, n // tn, k // tk), pl.BlockSpec((tm, tk), lambda i, j, kk: (i, kk)),
                   pl.BlockSpec((tk, tn), lambda i, j, kk: (kk, j)), [((m, n), dt, spec) for dt in outs_dtypes], epilogue,
                   extras=[(e, spec if e.shape[0] == m else pl.BlockSpec((SUBLANES, tn), lambda i, j, kk: (0, j))) for e in extras],
                   acc_shape=(tm, tn))


def _mm_rows_t(name, a, w, epilogue=_ep_store, out_dtype=F32, extras=()):
    m, c = a.shape
    r = w.shape[0]
    tm, tn, tk = _tile(m, MM_TILE), _tile(r, MM_TILE // 2), _tile(c, MM_TILE)
    spec = pl.BlockSpec((tm, tn), lambda i, j, kk: (i, j))
    return _matmul(name, a, w, NT, (m // tm, r // tn, c // tk), pl.BlockSpec((tm, tk), lambda i, j, kk: (i, kk)),
                   pl.BlockSpec((tn, tk), lambda i, j, kk: (j, kk)), [((m, r), out_dtype, spec)], epilogue,
                   extras=[(e, spec) for e in extras], acc_shape=(tm, tn))[0]


def _mm_grad(name, a, b, shard_cols):
    t, m = a.shape
    n = b.shape[1]
    tk, tm = _tile(t, MM_TILE), _tile(m, MM_TILE)
    if shard_cols:
        tn = n // N_DEV
        out = ((N_DEV, m, tn), MXU, pl.BlockSpec((None, tm, tn), lambda i, j, kk: (j, i, 0)))
    else:
        tn = _tile(n, MM_TILE)
        out = ((m, n), MXU, pl.BlockSpec((tm, tn), lambda i, j, kk: (i, j)))
    return _matmul(name, a, b, TN, (m // tm, n // tn, t // tk), pl.BlockSpec((tk, tm), lambda i, j, kk: (kk, i)),
                   pl.BlockSpec((tk, tn), lambda i, j, kk: (kk, j)), [out], _ep_store, acc_shape=(tm, tn))[0]


def _rows_spec(tt, width=D):
    return pl.BlockSpec((tt, width), lambda i: (i, 0))


def _vec_spec(rows=SUBLANES, width=D):
    return pl.BlockSpec((rows, width), lambda i: (0, 0))


def _norm_mod(name, x, vec):
    t = x.shape[0]
    tt = _tile(t, ROW_TILE)

    def body(x_ref, v_ref, h_ref):
        xv = x_ref[...]
        r = lax.rsqrt(jnp.mean(xv * xv, axis=-1, keepdims=True) + NORM_EPS)
        h = (xv * r) * v_ref[0:1, :]
        h_ref[...] = (h * (1.0 + v_ref[1:2, :]) + v_ref[2:3, :]).astype(h_ref.dtype)

    return pl.pallas_call(body, name=name, grid=(t // tt,), in_specs=[_rows_spec(tt), _vec_spec()], out_specs=_rows_spec(tt),
                          out_shape=jax.ShapeDtypeStruct((t, D), MXU))(x, vec)


def _norm_mod_grad(name, dh, x, vec, dres):
    t = x.shape[0]
    tt = _tile(t, ROW_TILE)
    nt = t // tt

    def body(dh_ref, x_ref, v_ref, dres_ref, dx_ref, acc_ref):
        i = pl.program_id(0)

        @pl.when(i == 0)
        def _():
            acc_ref[...] = jnp.zeros_like(acc_ref)

        xv, dhv = x_ref[...], dh_ref[...]
        r = lax.rsqrt(jnp.mean(xv * xv, axis=-1, keepdims=True) + NORM_EPS)
        xn = xv * r
        w = v_ref[0:1, :] * (1.0 + v_ref[1:2, :])
        acc_ref[0:1, :] += _row_sum(dhv * xn)
        acc_ref[2:3, :] += _row_sum(dhv)
        dxn = dhv * w
        dx_ref[...] = dres_ref[...] + r * (dxn - xn * jnp.mean(dxn * xn, axis=-1, keepdims=True))

        @pl.when(i == nt - 1)
        def _():
            dw = acc_ref[0:1, :]
            acc_ref[1:2, :] = dw * v_ref[0:1, :]
            acc_ref[0:1, :] = dw * (1.0 + v_ref[1:2, :])

    return pl.pallas_call(
        body, name=name, grid=(nt,), in_specs=[_rows_spec(tt), _rows_spec(tt), _vec_spec(), _rows_spec(tt)],
        out_specs=[_rows_spec(tt), _vec_spec()],
        out_shape=[jax.ShapeDtypeStruct((t, D), F32), jax.ShapeDtypeStruct((SUBLANES, D), F32)])(dh, x, vec, dres)


def _residual_grad(name, dxo, y, vec):
    t = y.shape[0]
    tt = _tile(t, ROW_TILE)

    def body(dx_ref, y_ref, v_ref, dy_ref, acc_ref):
        @pl.when(pl.program_id(0) == 0)
        def _():
            acc_ref[...] = jnp.zeros_like(acc_ref)

        dxv = dx_ref[...]
        dy_ref[...] = (dxv * v_ref[0:1, :]).astype(dy_ref.dtype)
        acc_ref[0:1, :] += _row_sum(dxv * y_ref[...])

    return pl.pallas_call(
        body, name=name, grid=(t // tt,), in_specs=[_rows_spec(tt), _rows_spec(tt), _vec_spec()],
        out_specs=[_rows_spec(tt), _vec_spec()],
        out_shape=[jax.ShapeDtypeStruct((t, D), MXU), jax.ShapeDtypeStruct((SUBLANES, D), F32)])(dxo, y, vec)


def _loss_head(name, x, target, vec):
    t = x.shape[0]
    tt = _tile(t, ROW_TILE)

    def body(x_ref, t_ref, v_ref, dx_ref, acc_ref):
        @pl.when(pl.program_id(0) == 0)
        def _():
            acc_ref[...] = jnp.zeros_like(acc_ref)

        xv = x_ref[...]
        r = lax.rsqrt(jnp.mean(xv * xv, axis=-1, keepdims=True) + NORM_EPS)
        xn = xv * r
        gain = v_ref[0:1, :]
        err = xn * gain - t_ref[...]
        acc_ref[1:2, :] += _row_sum(err * err) * (0.5 / D)
        dout = err * (1.0 / D)
        acc_ref[0:1, :] += _row_sum(dout * xn)
        dxn = dout * gain
        dx_ref[...] = r * (dxn - xn * jnp.mean(dxn * xn, axis=-1, keepdims=True))

    return pl.pallas_call(
        body, name=name, grid=(t // tt,), in_specs=[_rows_spec(tt), _rows_spec(tt), _vec_spec()],
        out_specs=[_rows_spec(tt), _vec_spec()],
        out_shape=[jax.ShapeDtypeStruct((t, D), F32), jax.ShapeDtypeStruct((SUBLANES, D), F32)])(x, target, vec)


def _shift_down(x, halo, k):
    y = pltpu.roll(x, k, 0)
    top = jnp.where(lax.broadcasted_iota(jnp.int32, halo.shape, 0) < k, pltpu.roll(halo, k, 0), y[0:SUBLANES, :])
    return jnp.concatenate([top, y[SUBLANES:, :]], axis=0)


def _shift_up(x, halo, k):
    n = x.shape[0]
    y = pltpu.roll(x, n - k, 0)
    bottom = jnp.where(lax.broadcasted_iota(jnp.int32, halo.shape, 0) >= SUBLANES - k, pltpu.roll(halo, SUBLANES - k, 0),
                       y[n - SUBLANES:, :])
    return jnp.concatenate([y[:n - SUBLANES, :], bottom], axis=0)


def _rg_gates(xb, halo, cw_ref, vec_ref, wa_ref, wx_ref, at_start):
    shifted = [xb] + [_shift_down(xb, halo, k) for k in range(1, CONV_WIDTH)]
    xc = vec_ref[0:1, :] + shifted[0] * cw_ref[CONV_WIDTH - 1:CONV_WIDTH, :]
    for k in range(1, CONV_WIDTH):
        xc = xc + shifted[k] * cw_ref[CONV_WIDTH - 1 - k:CONV_WIDTH - k, :]
    heads = [slice(h * HEAD_DIM, (h + 1) * HEAD_DIM) for h in range(HEADS)]
    pa = jnp.concatenate([_dot(xc[:, s], wa_ref[h]) for h, s in enumerate(heads)], axis=1) + vec_ref[1:2, :]
    px = jnp.concatenate([_dot(xc[:, s], wx_ref[h]) for h, s in enumerate(heads)], axis=1) + vec_ref[2:3, :]
    ra, ia = _sigmoid(pa), _sigmoid(px)
    nl = -vec_ref[3:4, :]
    sp = jnp.maximum(nl, 0.0) + jnp.log(1.0 + jnp.exp(-jnp.abs(nl)))
    log_a = (-LRU_C) * ra * sp
    a = jnp.exp(log_a)
    th = jnp.tanh(log_a)
    is_t0 = jnp.logical_and(lax.broadcasted_iota(jnp.int32, xb.shape, 0) == 0, at_start)
    mult = jnp.where(is_t0, 1.0, jnp.sqrt(-2.0 * th / (1.0 - th)))
    return dict(shifted=shifted, xc=xc, ra=ra, ia=ia, sp=sp, a=a, mult=mult, is_t0=is_t0, heads=heads)


def _rg_specs(tt, nt, order):
    blk = tt // SUBLANES
    return dict(
        x=pl.BlockSpec((tt, D), lambda i: (order(i), 0)), y=pl.BlockSpec((tt, D), lambda i: (order(i), 1)),
        halo=pl.BlockSpec((SUBLANES, D), lambda i: (jnp.maximum(order(i) * blk - 1, 0), 0)),
        cw=_vec_spec(CONV_WIDTH), vec=_vec_spec(), w=pl.BlockSpec((HEADS, HEAD_DIM, HEAD_DIM), lambda i: (0, 0, 0)))


def _rg_forward(name, z, cw, vec, wa, wx):
    t = z.shape[0]
    tt = _tile(t, ROW_TILE)
    nt = t // tt
    sp = _rg_specs(tt, nt, lambda i: i)

    def body(zx_ref, zy_ref, halo_ref, cw_ref, vec_ref, wa_ref, wx_ref, p_ref, h_ref, a_s, u_s, carry):
        i = pl.program_id(0)

        @pl.when(i == 0)
        def _():
            carry[...] = jnp.zeros_like(carry)

        halo = jnp.where(i > 0, halo_ref[...], 0.0)
        g = _rg_gates(zx_ref[...], halo, cw_ref, vec_ref, wa_ref, wx_ref, i == 0)
        a_s[...] = g["a"]
        u_s[...] = g["mult"] * (g["ia"] * g["xc"])

        def group(gi, h):
            rows = pl.ds(pl.multiple_of(gi * SUBLANES, SUBLANES), SUBLANES)
            a8, u8 = a_s[rows, :], u_s[rows, :]
            out = []
            for j in range(SUBLANES):
                h = a8[j:j + 1, :] * h + u8[j:j + 1, :]
                out.append(h)
            h_ref[rows, :] = jnp.concatenate(out, axis=0)
            return h

        carry[0:1, :] = lax.fori_loop(0, tt // SUBLANES, group, carry[0:1, :])
        p_ref[...] = (h_ref[...] * _gelu(zy_ref[...])).astype(p_ref.dtype)

    return pl.pallas_call(
        body, name=name, grid=(nt,), in_specs=[sp["x"], sp["y"], sp["halo"], sp["cw"], sp["vec"], sp["w"], sp["w"]],
        out_specs=[_rows_spec(tt), _rows_spec(tt)],
        out_shape=[jax.ShapeDtypeStruct((t, D), MXU), jax.ShapeDtypeStruct((t, D), F32)],
        scratch_shapes=[pltpu.VMEM((tt, D), F32), pltpu.VMEM((tt, D), F32), pltpu.VMEM((SUBLANES, D), F32)],
        compiler_params=_params())(z, z, z, cw, vec, wa, wx)


def _rg_backward(name, dp, z, h, cw, vec, wa, wx):
    t = z.shape[0]
    tt = _tile(t, ROW_TILE)
    nt = t // tt
    rev = lambda i: nt - 1 - i
    sp = _rg_specs(tt, nt, rev)
    rows_rev = pl.BlockSpec((tt, D), lambda i: (rev(i), 0))

    def body(dp_ref, zx_ref, zy_ref, halo_ref, h_ref, hhalo_ref, cw_ref, vec_ref, wa_ref, wx_ref,
             dz_ref, dvec_ref, dcw_ref, dwa_ref, dwx_ref, a_s, d_s, carry, nxt):
        i = pl.program_id(0)
        j = rev(i)

        @pl.when(i == 0)
        def _():
            carry[...] = jnp.zeros_like(carry)
            nxt[...] = jnp.zeros_like(nxt)
            dvec_ref[...] = jnp.zeros_like(dvec_ref)
            dcw_ref[...] = jnp.zeros_like(dcw_ref)
            dwa_ref[...] = jnp.zeros_like(dwa_ref)
            dwx_ref[...] = jnp.zeros_like(dwx_ref)

        halo = jnp.where(j > 0, halo_ref[...], 0.0)
        g = _rg_gates(zx_ref[...], halo, cw_ref, vec_ref, wa_ref, wx_ref, j == 0)
        xc, ra, ia, a, mult = g["xc"], g["ra"], g["ia"], g["a"], g["mult"]
        hv, zy, dpv = h_ref[...], zy_ref[...], dp_ref[...]
        dyb = dpv * hv * _gelu_grad(zy)
        a_s[...] = a
        d_s[...] = dpv * _gelu(zy)

        def group(gi, c):
            rows = pl.ds(pl.multiple_of((tt // SUBLANES - 1 - gi) * SUBLANES, SUBLANES), SUBLANES)
            a8, d8 = a_s[rows, :], d_s[rows, :]
            out = [None] * SUBLANES
            for r in reversed(range(SUBLANES)):
                dht = d8[r:r + 1, :] + c
                out[r] = dht
                c = a8[r:r + 1, :] * dht
            d_s[rows, :] = jnp.concatenate(out, axis=0)
            return c

        carry[0:1, :] = lax.fori_loop(0, tt // SUBLANES, group, carry[0:1, :])
        dht = d_s[...]
        hprev = _shift_down(hv, jnp.where(j > 0, hhalo_ref[...], 0.0), 1)
        ixc = ia * xc
        dlog_a = dht * hprev * a + jnp.where(g["is_t0"], 0.0, dht * ixc * (-(a * a) / mult))
        dia = dht * mult * xc
        dxc = dht * mult * ia
        dra = dlog_a * ((-LRU_C) * g["sp"])
        nl = -vec_ref[3:4, :]
        dvec_ref[3:4, :] += _row_sum(dlog_a * ((-LRU_C) * ra)) * (-_sigmoid(nl))
        dpa = dra * ra * (1.0 - ra)
        dpx = dia * ia * (1.0 - ia)
        dvec_ref[1:2, :] += _row_sum(dpa)
        dvec_ref[2:3, :] += _row_sum(dpx)
        back = []
        for hd, s in enumerate(g["heads"]):
            dwa_ref[hd] += _dot_tn(xc[:, s], dpa[:, s])
            dwx_ref[hd] += _dot_tn(xc[:, s], dpx[:, s])
            back.append(_dot_nt(dpa[:, s], wa_ref[hd]) + _dot_nt(dpx[:, s], wx_ref[hd]))
        dxc = dxc + jnp.concatenate(back, axis=1)
        dvec_ref[0:1, :] += _row_sum(dxc)
        dxb = dxc * cw_ref[CONV_WIDTH - 1:CONV_WIDTH, :]
        for k in range(CONV_WIDTH):
            row = CONV_WIDTH - 1 - k
            dcw_ref[row:row + 1, :] += _row_sum(dxc * g["shifted"][k])
            if k:
                dxb = dxb + _shift_up(dxc, nxt[...], k) * cw_ref[row:row + 1, :]
        nxt[...] = dxc[0:SUBLANES, :]
        dz_ref[:, 0:D] = dxb.astype(dz_ref.dtype)
        dz_ref[:, D:2 * D] = dyb.astype(dz_ref.dtype)

    hhalo = pl.BlockSpec((SUBLANES, D), lambda i: (jnp.maximum(rev(i) * (tt // SUBLANES) - 1, 0), 0))
    wacc = pl.BlockSpec((HEADS, HEAD_DIM, HEAD_DIM), lambda i: (0, 0, 0))
    return pl.pallas_call(
        body, name=name, grid=(nt,),
        in_specs=[rows_rev, sp["x"], sp["y"], sp["halo"], rows_rev, hhalo, sp["cw"], sp["vec"], sp["w"], sp["w"]],
        out_specs=[pl.BlockSpec((tt, 2 * D), lambda i: (rev(i), 0)), _vec_spec(), _vec_spec(), wacc, wacc],
        out_shape=[jax.ShapeDtypeStruct((t, 2 * D), MXU), jax.ShapeDtypeStruct((SUBLANES, D), F32),
                   jax.ShapeDtypeStruct((SUBLANES, D), F32), jax.ShapeDtypeStruct((HEADS, HEAD_DIM, HEAD_DIM), F32),
                   jax.ShapeDtypeStruct((HEADS, HEAD_DIM, HEAD_DIM), F32)],
        scratch_shapes=[pltpu.VMEM((tt, D), F32), pltpu.VMEM((tt, D), F32), pltpu.VMEM((SUBLANES, D), F32),
                        pltpu.VMEM((SUBLANES, D), F32)],
        compiler_params=_params())(dp, z, z, z, h, h, cw, vec, wa, wx)


def _hg_chunk(zq, zf, zi, lb):
    c = HG_CHUNK
    q = _silu(zq)
    sg = _sigmoid(zf)
    fg = lb + (1.0 - lb) * sg
    k = 1.0 - fg
    row, col = lax.broadcasted_iota(jnp.int32, (c, c), 0), lax.broadcasted_iota(jnp.int32, (c, c), 1)
    tri, tri_t = (row >= col).astype(F32), (row <= col).astype(F32)
    b = _dot_exact(tri, jnp.log(fg))
    mid, last = b[c // 2 - 1:c // 2, :], b[c - 1:c, :]
    eq = jnp.exp(jnp.minimum(b - mid, EXP_CLAMP))
    ek = jnp.exp(jnp.minimum(mid - b, EXP_CLAMP))
    eb = jnp.exp(b)
    ed = jnp.exp(last - b)
    return dict(q=q, sg=sg, fg=fg, k=k, v=zi, eq=eq, ek=ek, eb=eb, ed=ed, elast=jnp.exp(last), tri=tri, tri_t=tri_t,
                qe=q * eq, ke=k * ek, qb=q * eb, kd=k * ed)


def _hg_specs(tt, order):
    return [pl.BlockSpec((tt, D), lambda i, col=col: (order(i), col)) for col in range(4)]


def _hg_forward(name, z, vec):
    t = z.shape[0]
    tt = _tile(t, ROW_TILE)
    nt, nc = t // tt, tt // HG_CHUNK
    heads = [slice(h * HEAD_DIM, (h + 1) * HEAD_DIM) for h in range(HEADS)]

    def body(zq_ref, zf_ref, zi_ref, zg_ref, vec_ref, p_ref, o_ref, ss_ref, state):
        @pl.when(pl.program_id(0) == 0)
        def _():
            state[...] = jnp.zeros_like(state)

        lb, gain = vec_ref[0:1, :], vec_ref[1:2, :]

        def chunk(ci, carry):
            rows = pl.ds(pl.multiple_of(ci * HG_CHUNK, HG_CHUNK), HG_CHUNK)
            g = _hg_chunk(zq_ref[rows, :], zf_ref[rows, :], zi_ref[rows, :], lb)
            zg = zg_ref[rows, :]
            causal = g["tri"] > 0.0
            o_parts, p_parts = [], []
            for hd, s in enumerate(heads):
                st = state[hd]
                ss_ref[ci, hd] = st
                att = jnp.where(causal, _dot_nt(g["qe"][:, s], g["ke"][:, s]), 0.0)
                o = _dot(att, g["v"][:, s]) + _dot_nt(g["qb"][:, s], st)
                state[hd] = st * g["elast"][:, s] + _dot_tn(g["v"][:, s], g["kd"][:, s])
                r = lax.rsqrt(jnp.mean(o * o, axis=-1, keepdims=True) + GNORM_EPS)
                o_parts.append(o)
                p_parts.append((o * r) * gain[:, s])
            o_ref[rows, :] = jnp.concatenate(o_parts, axis=1)
            p_ref[rows, :] = (jnp.concatenate(p_parts, axis=1) * _silu(zg)).astype(p_ref.dtype)
            return carry

        lax.fori_loop(0, nc, chunk, 0)

    return pl.pallas_call(
        body, name=name, grid=(nt,), in_specs=_hg_specs(tt, lambda i: i) + [_vec_spec()],
        out_specs=[_rows_spec(tt), _rows_spec(tt), pl.BlockSpec((nc, HEADS, HEAD_DIM, HEAD_DIM), lambda i: (i, 0, 0, 0))],
        out_shape=[jax.ShapeDtypeStruct((t, D), MXU), jax.ShapeDtypeStruct((t, D), F32),
                   jax.ShapeDtypeStruct((t // HG_CHUNK, HEADS, HEAD_DIM, HEAD_DIM), F32)],
        scratch_shapes=[pltpu.VMEM((HEADS, HEAD_DIM, HEAD_DIM), F32)], compiler_params=_params())(z, z, z, z, vec)


def _hg_backward(name, dp, z, o, ss, vec):
    t = z.shape[0]
    tt = _tile(t, ROW_TILE)
    nt, nc = t // tt, tt // HG_CHUNK
    rev = lambda i: nt - 1 - i
    heads = [slice(h * HEAD_DIM, (h + 1) * HEAD_DIM) for h in range(HEADS)]
    rows_rev = pl.BlockSpec((tt, D), lambda i: (rev(i), 0))

    def body(dp_ref, zq_ref, zf_ref, zi_ref, zg_ref, o_ref, ss_ref, vec_ref, dz_ref, acc_ref,
             dstate, dqe_s, dke_s, dqb_s, dkd_s, dv_s, dzg_s, dlast_s):
        @pl.when(pl.program_id(0) == 0)
        def _():
            dstate[...] = jnp.zeros_like(dstate)
            acc_ref[...] = jnp.zeros_like(acc_ref)

        lb, gain = vec_ref[0:1, :], vec_ref[1:2, :]

        def chunk(cr, carry):
            ci = nc - 1 - cr
            rows = pl.ds(pl.multiple_of(ci * HG_CHUNK, HG_CHUNK), HG_CHUNK)
            zq, zg = zq_ref[rows, :], zg_ref[rows, :]
            g = _hg_chunk(zq, zf_ref[rows, :], zi_ref[rows, :], lb)
            ov, dpv = o_ref[rows, :], dp_ref[rows, :]
            causal = g["tri"] > 0.0
            don = dpv * _silu(zg)
            dgate = dpv * _silu_grad(zg)
            for hd, s in enumerate(heads):
                oh = ov[:, s]
                r = lax.rsqrt(jnp.mean(oh * oh, axis=-1, keepdims=True) + GNORM_EPS)
                on = oh * r
                dzg_s[:, s] = dgate[:, s] * (on * gain[:, s])
                acc_ref[1:2, s] += _row_sum(don[:, s] * on)
                dtmp = don[:, s] * gain[:, s]
                do = r * (dtmp - on * jnp.mean(dtmp * on, axis=-1, keepdims=True))
                st, dst = ss_ref[ci, hd], dstate[hd]
                qe, ke, qb, kd, v = g["qe"][:, s], g["ke"][:, s], g["qb"][:, s], g["kd"][:, s], g["v"][:, s]
                att = jnp.where(causal, _dot_nt(qe, ke), 0.0)
                datt = jnp.where(causal, _dot_nt(do, v), 0.0)
                dv_s[:, s] = _dot_tn(att, do) + _dot_nt(kd, dst)
                dqe_s[:, s] = _dot(datt, ke)
                dke_s[:, s] = _dot_tn(datt, qe)
                dqb_s[:, s] = _dot(do, st)
                dkd_s[:, s] = _dot(v, dst)
                dlast_s[0:1, s] = g["elast"][:, s] * _row_sum(dst * st)
                dstate[hd] = dst * g["elast"][:, s] + _dot_tn(do, qb)
            dqe, dke, dqb, dkd = dqe_s[...], dke_s[...], dqb_s[...], dkd_s[...]
            dq = dqe * g["eq"] + dqb * g["eb"]
            dk = dke * g["ek"] + dkd * g["ed"]
            dkdk = dkd * g["kd"]
            db = dqe * g["qe"].astype(MXU).astype(F32) - dke * g["ke"].astype(MXU).astype(F32) + dqb * g["qb"] - dkdk
            dlogf = _dot_exact(g["tri_t"], db) + (dlast_s[0:1, :] + _row_sum(dkdk))
            dfg = dlogf / g["fg"] - dk
            sg = g["sg"]
            acc_ref[0:1, :] += _row_sum(dfg * (1.0 - sg))
            dz_ref[rows, 0:D] = (dq * _silu_grad(zq)).astype(dz_ref.dtype)
            dz_ref[rows, D:2 * D] = (dfg * (1.0 - lb) * sg * (1.0 - sg)).astype(dz_ref.dtype)
            dz_ref[rows, 2 * D:3 * D] = dv_s[...].astype(dz_ref.dtype)
            dz_ref[rows, 3 * D:4 * D] = dzg_s[...].astype(dz_ref.dtype)
            return carry

        lax.fori_loop(0, nc, chunk, 0)

    chunk_buf = pltpu.VMEM((HG_CHUNK, D), F32)
    return pl.pallas_call(
        body, name=name, grid=(nt,),
        in_specs=[rows_rev] + _hg_specs(tt, rev) + [rows_rev, pl.BlockSpec((nc, HEADS, HEAD_DIM, HEAD_DIM), lambda i: (rev(i), 0, 0, 0)),
                                                  _vec_spec()],
        out_specs=[pl.BlockSpec((tt, 4 * D), lambda i: (rev(i), 0)), _vec_spec()],
        out_shape=[jax.ShapeDtypeStruct((t, 4 * D), MXU), jax.ShapeDtypeStruct((SUBLANES, D), F32)],
        scratch_shapes=[pltpu.VMEM((HEADS, HEAD_DIM, HEAD_DIM), F32)] + [chunk_buf] * 6 + [pltpu.VMEM((SUBLANES, D), F32)],
        compiler_params=_params())(dp, z, z, z, z, o, ss, vec)


def _mod_forward(name, c_all, mod_w, lower):
    depth, _, sw = mod_w.shape

    def body(c_ref, w_ref, lo_ref, cs_ref, mod_ref, lb_ref):
        cs = _silu(c_ref[...])
        mod_ref[...] = _dot(cs, w_ref[...])

        @pl.when(pl.program_id(0) == 0)
        def _():
            cs_ref[...] = cs
            lo = lo_ref[...]
            e = jnp.exp(lo - jnp.max(lo, axis=0, keepdims=True))
            sm = e / jnp.sum(e, axis=0, keepdims=True)
            lb_ref[0:1, :] = jnp.zeros((1, D), F32)
            for l in range(1, depth):
                lb_ref[l:l + 1, :] = lb_ref[l - 1:l, :] + sm[l:l + 1, :]

    return pl.pallas_call(
        body, name=name, grid=(depth,),
        in_specs=[pl.BlockSpec((N_DEV, D), lambda l: (0, 0)), pl.BlockSpec((None, D, sw), lambda l: (l, 0, 0)),
                  pl.BlockSpec((depth, D), lambda l: (0, 0))],
        out_specs=[pl.BlockSpec((N_DEV, D), lambda l: (0, 0)), pl.BlockSpec((None, N_DEV, sw), lambda l: (l, 0, 0)),
                   pl.BlockSpec((depth, D), lambda l: (0, 0))],
        out_shape=[jax.ShapeDtypeStruct((N_DEV, D), F32), jax.ShapeDtypeStruct((depth, N_DEV, sw), F32),
                   jax.ShapeDtypeStruct((depth, D), F32)], compiler_params=_params())(c_all, mod_w, lower)


def _mod_weight_grad(name, cs_t, dmod):
    depth, pad, sw = dmod.shape

    def body(c_ref, d_ref, g_ref):
        g_ref[...] = _dot(c_ref[...], d_ref[...])

    return pl.pallas_call(
        body, name=name, grid=(depth,),
        in_specs=[pl.BlockSpec((D, pad), lambda l: (0, 0)), pl.BlockSpec((None, pad, sw), lambda l: (l, 0, 0))],
        out_specs=pl.BlockSpec((None, D, sw), lambda l: (l, 0, 0)),
        out_shape=jax.ShapeDtypeStruct((depth, D, sw), F32), compiler_params=_params())(cs_t, dmod)


def _lower_bound_grad(name, lower, dlb):
    depth = lower.shape[0]

    def body(lo_ref, d_ref, out_ref):
        lo = lo_ref[...]
        e = jnp.exp(lo - jnp.max(lo, axis=0, keepdims=True))
        sm = e / jnp.sum(e, axis=0, keepdims=True)
        out_ref[...] = jnp.zeros_like(out_ref)
        dsm = [jnp.zeros((1, D), F32)]
        for l in range(1, depth):
            tail = d_ref[l:l + 1, :]
            for m in range(l + 1, depth):
                tail = tail + d_ref[m:m + 1, :]
            dsm.append(tail)
        inner = sm[1:2, :] * dsm[1]
        for l in range(2, depth):
            inner = inner + sm[l:l + 1, :] * dsm[l]
        for l in range(depth):
            out_ref[l:l + 1, :] = sm[l:l + 1, :] * (dsm[l] - inner)
        for j in range(SUBLANES - depth):
            row = d_ref[depth + j:depth + j + 1, :]
            tot = row[:, 0:HEAD_DIM]
            for hd in range(1, HEADS):
                tot = tot + row[:, hd * HEAD_DIM:(hd + 1) * HEAD_DIM]
            out_ref[depth + j:depth + j + 1, 0:HEAD_DIM] = tot

    return pl.pallas_call(body, name=name, out_shape=jax.ShapeDtypeStruct((SUBLANES, D), F32))(lower, dlb)


def _adamw(name, parts, w, m, v, layer=None, prev=None):
    p, r, c = parts.shape
    tr = _tile(r, max(SUBLANES, ADAMW_STEP_BYTES // (4 * c * (p + 7))))
    stacked = layer is not None
    n_prev = 4 if prev is not None else 0

    def body(*refs):
        parts_ref, w_ref, m_ref, v_ref = refs[:4]
        g_ref, d_ref, m_out, v_out = refs[4 + n_prev:]
        g = parts_ref[0].astype(F32)
        for q in range(1, p):
            g = g + parts_ref[q].astype(F32)
        m2 = ADAM_B1 * m_ref[...] + (1.0 - ADAM_B1) * g
        v2 = ADAM_B2 * v_ref[...] + (1.0 - ADAM_B2) * (g * g)
        m_hat = m2 / (1.0 - ADAM_B1 ** ADAM_STEP)
        v_hat = v2 / (1.0 - ADAM_B2 ** ADAM_STEP)
        g_ref[...] = g
        d_ref[...] = -ADAM_LR * (m_hat / (jnp.sqrt(v_hat) + ADAM_EPS) + ADAM_WD * w_ref[...])
        m_out[...] = m2
        v_out[...] = v2

    if stacked:
        spec = pl.BlockSpec((None, tr, c), lambda i: (layer, i, 0))
    else:
        spec = pl.BlockSpec((tr, c), lambda i: (i, 0))
    return pl.pallas_call(
        body, name=name, grid=(r // tr,),
        in_specs=[pl.BlockSpec((p, tr, c), lambda i: (0, i, 0)), spec, spec, spec] + [ANY] * n_prev, out_specs=[spec] * 4,
        out_shape=[jax.ShapeDtypeStruct(w.shape, F32)] * 4,
        input_output_aliases={4 + q: q for q in range(n_prev)}, compiler_params=_params(),
    )(parts, w, m, v, *(prev or []))


def _vec(*rows):
    rows = [r.reshape(1, D).astype(F32) for r in rows]
    return jnp.concatenate(rows + [jnp.zeros((SUBLANES - len(rows), D), F32)], axis=0)


def _pad_rows(a, rows=SUBLANES):
    a = a.reshape(-1, a.shape[-1])
    return jnp.concatenate([a, jnp.zeros((rows - a.shape[0], a.shape[1]), a.dtype)], axis=0) if a.shape[0] < rows else a


def kernel(x, c, mod_w, mod_b, norm_mix, norm_mlp, norm_final, rg_w_in, rg_conv_w, rg_conv_b, rg_w_a, rg_b_a, rg_w_x, rg_b_x, rg_lambda, rg_w_out, hg_w_in, hg_lower_bounds, hg_gnorm, hg_w_out, mlp_w1, mlp_w2, loss_target, m_mod_w, m_mod_b, m_norm_mix, m_norm_mlp, m_norm_final, m_rg_w_in, m_rg_conv_w, m_rg_conv_b, m_rg_w_a, m_rg_b_a, m_rg_w_x, m_rg_b_x, m_rg_lambda, m_rg_w_out, m_hg_w_in, m_hg_lower_bounds, m_hg_gnorm, m_hg_w_out, m_mlp_w1, m_mlp_w2, v_mod_w, v_mod_b, v_norm_mix, v_norm_mlp, v_norm_final, v_rg_w_in, v_rg_conv_w, v_rg_conv_b, v_rg_w_a, v_rg_b_a, v_rg_w_x, v_rg_b_x, v_rg_lambda, v_rg_w_out, v_hg_w_in, v_hg_lower_bounds, v_hg_gnorm, v_hg_w_out, v_mlp_w1, v_mlp_w2):
    me = 4 * lax.axis_index("x") + 2 * lax.axis_index("y") + lax.axis_index("c")
    x0 = x[0]
    target = loss_target[0]
    n_rg, n_hg = rg_w_in.shape[0], hg_w_in.shape[0]
    sw_mod = mod_w.shape[2]

    c_all, cw_all = _all_gather([_pad_rows(c), rg_conv_w.reshape(n_rg * CONV_WIDTH, -1)], "gather_cond")
    c_all = c_all[:, 0, :]
    conv_w = cw_all.transpose(1, 0, 2).reshape(n_rg, CONV_WIDTH, D)
    cs_all, mod_part, lb_all = _mod_forward("mod_forward", c_all, mod_w, hg_lower_bounds)
    (mod_gathered,) = _all_gather([mod_part.reshape(DEPTH * N_DEV, sw_mod)], "gather_mod")
    mod_mine = lax.dynamic_index_in_dim(mod_gathered.reshape(N_DEV, DEPTH, N_DEV, sw_mod), me, axis=2, keepdims=False)
    mod = mod_mine.transpose(1, 0, 2).reshape(DEPTH, 6, D) + mod_b.reshape(DEPTH, 6, D)

    weights = []
    for layer in range(DEPTH):
        j = layer // 2
        w_in, w_out = (rg_w_in[j], rg_w_out[j]) if layer % 2 == 0 else (hg_w_in[j], hg_w_out[j])
        shards = [w_in.astype(MXU), w_out.astype(MXU), mlp_w1[layer].astype(MXU), mlp_w2[layer].astype(MXU)]
        g_in, g_out, g_w1, g_w2 = _all_gather_async(shards, f"gather_weights_{layer}", layer)
        weights.append(dict(w_in=g_in, w_out=g_out.reshape(D, D), w1=g_w1, w2=g_w2.reshape(-1, D)))

    saved = []
    xl = x0
    for layer in range(DEPTH):
        j, wt = layer // 2, weights[layer]
        is_rg = layer % 2 == 0
        s = dict(x=xl)
        s["vec_mix"] = _vec(norm_mix[layer], mod[layer, 1], mod[layer, 0])
        s["vec_mlp"] = _vec(norm_mlp[layer], mod[layer, 4], mod[layer, 3])
        s["gate_mix"], s["gate_mlp"] = _vec(mod[layer, 2]), _vec(mod[layer, 5])
        s["h"] = _norm_mod("norm_mod", xl, s["vec_mix"])
        if is_rg:
            (s["z"],) = _mm_cols("rg_in", s["h"], wt["w_in"])
            s["cw"] = conv_w[j]
            s["vec"] = _vec(rg_conv_b[j], rg_b_a[j], rg_b_x[j], rg_lambda[j])
            s["wa"], s["wx"] = rg_w_a[j].astype(MXU), rg_w_x[j].astype(MXU)
            s["p"], s["hr"] = _rg_forward("rg_forward", s["z"], s["cw"], s["vec"], s["wa"], s["wx"])
        else:
            (s["z"],) = _mm_cols("hg_in", s["h"], wt["w_in"])
            s["vec"] = _vec(lb_all[layer], jnp.tile(hg_gnorm[j], HEADS))
            s["p"], s["o"], s["ss"] = _hg_forward("hg_forward", s["z"], s["vec"])
        s["y"], s["x1"] = _mm_rows("mix_out", s["p"], wt["w_out"], _ep_residual, (F32, F32), extras=(xl, s["gate_mix"]))
        s["h2"] = _norm_mod("norm_mod", s["x1"], s["vec_mlp"])
        s["a"], s["s"] = _mm_cols("mlp_in", s["h2"], wt["w1"], _ep_relu2, (MXU, MXU))
        s["ff"], xl = _mm_rows("mlp_out", s["s"], wt["w2"], _ep_residual, (F32, F32), extras=(s["x1"], s["gate_mlp"]))
        saved.append(s)

    dx, head = _loss_head("loss_head", xl, target, _vec(norm_final))

    results = {}
    big = dict(rg_w_in=(rg_w_in, m_rg_w_in, v_rg_w_in), rg_w_out=(rg_w_out, m_rg_w_out, v_rg_w_out),
               hg_w_in=(hg_w_in, m_hg_w_in, v_hg_w_in), hg_w_out=(hg_w_out, m_hg_w_out, v_hg_w_out),
               mlp_w1=(mlp_w1, m_mlp_w1, v_mlp_w1), mlp_w2=(mlp_w2, m_mlp_w2, v_mlp_w2))

    def update_layer(layer, landed):
        mixer = "rg" if layer % 2 == 0 else "hg"
        l_in, l_out, l_w1, l_w2 = landed
        for nm, parts, idx in ((f"{mixer}_w_in", l_in, layer // 2), (f"{mixer}_w_out", l_out, layer // 2),
                               ("mlp_w1", l_w1, layer), ("mlp_w2", l_w2, layer)):
            w, m, v = big[nm]
            results[nm] = _adamw(f"adamw_{nm}", parts, w, m, v, layer=idx, prev=results.get(nm))

    landed = [None] * DEPTH
    small = [None] * DEPTH
    for layer in reversed(range(DEPTH)):
        j, wt, s = layer // 2, weights[layer], saved[layer]
        is_rg = layer % 2 == 0
        dff, g_mlp = _residual_grad("residual_grad", dx, s["ff"], s["gate_mlp"])
        da = _mm_rows_t("mlp_out_t", dff, wt["w2"], _ep_relu2_grad, MXU, extras=(s["a"],))
        dw2 = _mm_grad("mlp_out_grad", s["s"], dff, False).reshape(N_DEV, -1, D)
        dw1 = _mm_grad("mlp_in_grad", s["h2"], da, True)
        l_w1, l_w2 = _exchange_async([dw1, dw2], f"exchange_mlp_grads_{layer}", DEPTH + 2 * layer)
        dh2 = _mm_cols_t("mlp_in_t", da, wt["w1"])
        dx1, n_mlp = _norm_mod_grad("norm_mod_grad", dh2, s["x1"], s["vec_mlp"], dx)
        dyb, g_mix = _residual_grad("residual_grad", dx1, s["y"], s["gate_mix"])
        dp = _mm_rows_t("mix_out_t", dyb, wt["w_out"])
        dw_out = _mm_grad("mix_out_grad", s["p"], dyb, False).reshape(N_DEV, -1, D)
        if is_rg:
            dz, dvec, dcw, dwa, dwx = _rg_backward("rg_backward", dp, s["z"], s["hr"], s["cw"], s["vec"], s["wa"], s["wx"])
            mixer_small = dict(dvec=dvec, dcw=dcw, dwa=dwa, dwx=dwx)
            dw_in = _mm_grad("rg_in_grad", s["h"], dz, True)
            dh = _mm_cols_t("rg_in_t", dz, wt["w_in"])
        else:
            dz, dvec = _hg_backward("hg_backward", dp, s["z"], s["o"], s["ss"], s["vec"])
            mixer_small = dict(dvec=dvec)
            dw_in = _mm_grad("hg_in_grad", s["h"], dz, True)
            dh = _mm_cols_t("hg_in_t", dz, wt["w_in"])
        l_in, l_out = _exchange_async([dw_in, dw_out], f"exchange_mixer_grads_{layer}", DEPTH + 2 * layer + 1)
        dx, n_mix = _norm_mod_grad("norm_mod_grad", dh, s["x"], s["vec_mix"], dx1)
        landed[layer] = (l_in, l_out, l_w1, l_w2)
        small[layer] = dict(g_mlp=g_mlp, n_mlp=n_mlp, g_mix=g_mix, n_mix=n_mix, **mixer_small)
        if layer + 1 < DEPTH:
            update_layer(layer + 1, landed[layer + 1])

    dlb_rows = [jnp.zeros((1, D), F32) if l % 2 == 0 else small[l]["dvec"][0:1] for l in range(DEPTH)]
    dgn_rows = [small[2 * j + 1]["dvec"][1:2] for j in range(n_hg)]
    lb_grad = _lower_bound_grad("lower_bound_grad", hg_lower_bounds, _pad_rows(jnp.concatenate(dlb_rows + dgn_rows, axis=0)))
    dmod = jnp.stack([jnp.concatenate([small[l]["n_mix"][2], small[l]["n_mix"][1], small[l]["g_mix"][0],
                                       small[l]["n_mlp"][2], small[l]["n_mlp"][1], small[l]["g_mlp"][0]]) for l in range(DEPTH)])
    groups = [
        dmod.reshape(DEPTH * 6, D),
        jnp.stack([small[l]["n_mix"][0] for l in range(DEPTH)]),
        jnp.stack([small[l]["n_mlp"][0] for l in range(DEPTH)]),
        head[0:1],
        jnp.stack([small[2 * j]["dvec"][0] for j in range(n_rg)]),
        jnp.stack([small[2 * j]["dvec"][1] for j in range(n_rg)]),
        jnp.stack([small[2 * j]["dvec"][2] for j in range(n_rg)]),
        jnp.stack([small[2 * j]["dvec"][3] for j in range(n_rg)]),
        lb_grad[0:DEPTH],
        jnp.concatenate([lb_grad[DEPTH + j:DEPTH + j + 1, 0:HEAD_DIM] for j in range(n_hg)]
                        + [jnp.zeros((1, D - n_hg * HEAD_DIM), F32)], axis=1),
        jnp.concatenate([small[2 * j]["dcw"][0:CONV_WIDTH] for j in range(n_rg)], axis=0),
        head[1:2],
    ]
    params = [mod_b, norm_mix, norm_mlp, norm_final, rg_conv_b, rg_b_a, rg_b_x, rg_lambda, hg_lower_bounds, hg_gnorm]
    moms = [m_mod_b, m_norm_mix, m_norm_mlp, m_norm_final, m_rg_conv_b, m_rg_b_a, m_rg_b_x, m_rg_lambda, m_hg_lower_bounds, m_hg_gnorm]
    vars_ = [v_mod_b, v_norm_mix, v_norm_mlp, v_norm_final, v_rg_conv_b, v_rg_b_a, v_rg_b_x, v_rg_lambda, v_hg_lower_bounds, v_hg_gnorm]
    offsets, rows_of, at = [], [], 0
    for g in groups:
        offsets.append(at)
        rows_of.append(g.shape[0])
        at += -(-g.shape[0] // SUBLANES) * SUBLANES
    packed = jnp.concatenate([_pad_rows(g, -(-g.shape[0] // SUBLANES) * SUBLANES) for g in groups], axis=0)
    dwa_all = jnp.stack([small[2 * j]["dwa"] for j in range(n_rg)]).reshape(-1, HEAD_DIM)
    dwx_all = jnp.stack([small[2 * j]["dwx"] for j in range(n_rg)]).reshape(-1, HEAD_DIM)
    small_parts, wa_parts, wx_parts = _all_gather([packed, dwa_all, dwx_all], "gather_small_grads")

    def pack_like(arrs):
        out = []
        for g_rows, off, a in zip(rows_of, offsets, arrs):
            flat = a.reshape(-1)
            flat = jnp.concatenate([flat, jnp.zeros((g_rows * D - flat.shape[0],), F32)])
            out.append(_pad_rows(flat.reshape(g_rows, D), -(-g_rows // SUBLANES) * SUBLANES))
        rest = packed.shape[0] - sum(o.shape[0] for o in out)
        return jnp.concatenate(out + [jnp.zeros((rest, D), F32)], axis=0)

    small_out = _adamw("adamw_small", small_parts, pack_like(params), pack_like(moms), pack_like(vars_))

    def unpack(q, idx, like):
        rows = small_out[q][offsets[idx]:offsets[idx] + rows_of[idx]]
        return rows.reshape(-1)[:like.size].reshape(like.shape)

    loss = jnp.sum(small_out[0][offsets[11]])
    names =["mod_b", "norm_mix", "norm_mlp", "norm_final", "rg_conv_b", "rg_b_a", "rg_b_x", "rg_lambda", "hg_lower_bounds", "hg_gnorm"]
    for idx, (nm, like) in enumerate(zip(names, params)):
        results[nm] = [unpack(q, idx, like) for q in range(4)]

    cw_parts = lax.dynamic_slice_in_dim(small_parts[:, offsets[10]:offsets[10] + n_rg * CONV_WIDTH, :], me * (D // N_DEV), D // N_DEV, axis=2)
    shp = rg_conv_w.shape
    results["rg_conv_w"] = [o.reshape(shp) for o in _adamw(
        "adamw_conv", cw_parts, rg_conv_w.reshape(-1, shp[-1]), m_rg_conv_w.reshape(-1, shp[-1]), v_rg_conv_w.reshape(-1, shp[-1]))]
    shp = rg_w_a.shape
    results["rg_w_a"] = [o.reshape(shp) for o in _adamw(
        "adamw_gate", wa_parts, rg_w_a.reshape(-1, HEAD_DIM), m_rg_w_a.reshape(-1, HEAD_DIM), v_rg_w_a.reshape(-1, HEAD_DIM))]
    results["rg_w_x"] = [o.reshape(shp) for o in _adamw(
        "adamw_gate", wx_parts, rg_w_x.reshape(-1, HEAD_DIM), m_rg_w_x.reshape(-1, HEAD_DIM), v_rg_w_x.reshape(-1, HEAD_DIM))]

    dmod_all = small_parts[:, 0:DEPTH * 6, :].reshape(N_DEV, DEPTH, 6 * D)
    dmod_cols = lax.dynamic_slice_in_dim(dmod_all, me * sw_mod, sw_mod, axis=2).transpose(1, 0, 2)
    pad = HEAD_DIM - N_DEV
    dmod_pad = jnp.concatenate([dmod_cols, jnp.zeros((DEPTH, pad, sw_mod), F32)], axis=1).astype(MXU)
    cs_t = jnp.concatenate([cs_all.T, jnp.zeros((D, pad), F32)], axis=1).astype(MXU)
    g_mod_w = _mod_weight_grad("mod_weight_grad", cs_t, dmod_pad)
    results["mod_w"] = [o.reshape(mod_w.shape) for o in _adamw(
        "adamw_mod", g_mod_w.reshape(1, -1, sw_mod), mod_w.reshape(-1, sw_mod), m_mod_w.reshape(-1, sw_mod), v_mod_w.reshape(-1, sw_mod))]

    update_layer(0, landed[0])

    order = ["mod_w", "mod_b", "norm_mix", "norm_mlp", "norm_final", "rg_w_in", "rg_conv_w", "rg_conv_b", "rg_w_a", "rg_b_a", "rg_w_x",
             "rg_b_x", "rg_lambda", "rg_w_out", "hg_w_in", "hg_lower_bounds", "hg_gnorm", "hg_w_out", "mlp_w1", "mlp_w2"]
    return (loss, dx[None], *[results[n][0] for n in order], *[results[n][1] for n in order],
            *[results[n][2] for n in order], *[results[n][3] for n in order])
```

```python
import functools

import jax
import jax.numpy as jnp
from jax import lax
from jax.experimental import pallas as pl
from jax.experimental.pallas import tpu as pltpu
from jax.experimental.pallas import tpu_sc as plsc

F32 = jnp.float32
MXU = jnp.bfloat16

N_DEV = 8
D = 1024
DEPTH = 4
HEADS = 8
HEAD_DIM = 128
CONV_WIDTH = 4
LRU_C = 8.0
HG_CHUNK = 64
NORM_EPS = 1e-6
GNORM_EPS = 1e-5
ADAM_LR = 0.001
ADAM_B1 = 0.9
ADAM_B2 = 0.999
ADAM_EPS = 1e-08
ADAM_WD = 0.01
ADAM_STEP = 10
GELU_C = 0.7978845608028654
GELU_K = 0.044715
EXP_CLAMP = 80.0
SUBLANES = 8
VMEM_LIMIT = 48 * 1024 * 1024
ROW_TILE = 256
MM_TILE = 1024
ADAMW_STEP_BYTES = 8 * 1024 * 1024

MESH = pl.DeviceIdType.MESH
ANY = pl.BlockSpec(memory_space=pl.ANY)


def _params():
    return pltpu.CompilerParams(vmem_limit_bytes=VMEM_LIMIT)


def _tile(n, target):
    if n <= target:
        return n
    t = target // SUBLANES * SUBLANES
    while n % t:
        t -= SUBLANES
    return t


def _sigmoid(x):
    return 1.0 / (1.0 + jnp.exp(-x))


def _silu(x):
    return x * _sigmoid(x)


def _silu_grad(x):
    s = _sigmoid(x)
    return s * (1.0 + x * (1.0 - s))


def _gelu(y):
    return 0.5 * y * (1.0 + jnp.tanh(GELU_C * (y + GELU_K * y * y * y)))


def _gelu_grad(y):
    th = jnp.tanh(GELU_C * (y + GELU_K * y * y * y))
    return 0.5 * (1.0 + th) + 0.5 * y * (1.0 - th * th) * GELU_C * (1.0 + 3.0 * GELU_K * y * y)


def _dot(a, b):
    return lax.dot_general(a.astype(MXU), b.astype(MXU), (((1,), (0,)), ((), ())), preferred_element_type=F32)


def _dot_nt(a, b):
    return lax.dot_general(a.astype(MXU), b.astype(MXU), (((1,), (1,)), ((), ())), preferred_element_type=F32)


def _dot_tn(a, b):
    return lax.dot_general(a.astype(MXU), b.astype(MXU), (((0,), (0,)), ((), ())), preferred_element_type=F32)


def _dot_exact(tri, x):
    t = tri.astype(MXU)
    hi = x.astype(MXU)
    r1 = x - hi.astype(F32)
    mid = r1.astype(MXU)
    lo = (r1 - mid.astype(F32)).astype(MXU)
    dn = (((1,), (0,)), ((), ()))
    return (lax.dot_general(t, hi, dn, preferred_element_type=F32) + lax.dot_general(t, mid, dn, preferred_element_type=F32)
            + lax.dot_general(t, lo, dn, preferred_element_type=F32))


def _row_sum(v):
    return jnp.sum(v, axis=0, keepdims=True)


def _handshake(partners):
    barrier = pltpu.get_barrier_semaphore()
    for p in partners:
        pl.semaphore_signal(barrier, inc=1, device_id=p, device_id_type=MESH)
    pl.semaphore_wait(barrier, len(partners))


def _gather_body(n, per_array_sems, handshake):
    def body(*refs):
        ins, outs = refs[:n], refs[n:2 * n]
        send_sems, recv_sems, local_sems = refs[2 * n:]
        x, y, c = lax.axis_index("x"), lax.axis_index("y"), lax.axis_index("c")
        me, sibling = (x, y, c), (x, y, 1 - c)
        chips = [(1 - x, y), (x, 1 - y), (1 - x, 1 - y)]
        if handshake:
            _handshake([sibling] + [(*chip, c) for chip in chips])

        def sem(sems, a, k):
            return sems.at[a, k] if per_array_sems else sems.at[k]

        def slot(a, p):
            return outs[a].at[4 * p[0] + 2 * p[1] + p[2]]

        def copy(a, k, block, to, src=None):
            return pltpu.make_async_remote_copy(
                src_ref=slot(a, block) if src is None else src, dst_ref=slot(a, block),
                send_sem=sem(send_sems, a, k), recv_sem=sem(recv_sems, a, k), device_id=to, device_id_type=MESH)

        mine = [pltpu.make_async_copy(ins[a], slot(a, me), local_sems.at[a if per_array_sems else 0]) for a in range(n)]
        for cp in mine:
            cp.start()
        first = []
        for a in range(n):
            first.append(copy(a, 0, me, sibling, src=ins[a]))
            first += [copy(a, 1 + j, me, (*chip, c), src=ins[a]) for j, chip in enumerate(chips)]
        for cp in first:
            cp.start()
        passed = []
        for j, chip in enumerate(chips):
            for a in range(n):
                copy(a, 1 + j, (*chip, c), me).wait_recv()
            for a in range(n):
                cp = copy(a, 4 + j, (*chip, c), sibling)
                cp.start()
                passed.append(cp)
        for a in range(n):
            copy(a, 0, sibling, me).wait_recv()
        for j, chip in enumerate(chips):
            for a in range(n):
                copy(a, 4 + j, (*chip, 1 - c), me).wait_recv()
        for cp in first + passed:
            cp.wait_send()
        for cp in mine:
            cp.wait()

    return body


def _sibling_send_body(n, per_array_sems, handshake):
    def body(*refs):
        ins, outs = refs[:n], refs[n:2 * n]
        send_sems, recv_sems, _ = refs[2 * n:]
        x, y, c = lax.axis_index("x"), lax.axis_index("y"), lax.axis_index("c")
        sibling = (x, y, 1 - c)
        if handshake:
            _handshake([sibling])

        def sem(sems, a):
            return sems.at[a, 0] if per_array_sems else sems.at[0]

        copies = [pltpu.make_async_remote_copy(
            src_ref=ins[a].at[2 * q + 1 - c], dst_ref=outs[a].at[q], send_sem=sem(send_sems, a), recv_sem=sem(recv_sems, a),
            device_id=sibling, device_id_type=MESH) for a in range(n) for q in range(4)]
        for cp in copies:
            cp.start()
        for cp in copies:
            cp.wait_recv()
        for cp in copies:
            cp.wait_send()

    return body


def _chip_exchange_body(n, per_array_sems, handshake):
    def body(*refs):
        ins, outs = refs[:n], refs[n:2 * n]
        send_sems, recv_sems, local_sems = refs[2 * n:]
        x, y, c = lax.axis_index("x"), lax.axis_index("y"), lax.axis_index("c")
        my_chip = 2 * x + y
        chips = [(1 - x, y), (x, 1 - y), (1 - x, 1 - y)]
        if handshake:
            _handshake([(*chip, c) for chip in chips])

        def sem(sems, a, k):
            return sems.at[a, k] if per_array_sems else sems.at[k]

        def copy(a, k, landing):
            px, py = chips[k]
            return pltpu.make_async_remote_copy(
                src_ref=ins[a].at[2 * px + py], dst_ref=outs[a].at[landing], send_sem=sem(send_sems, a, k),
                recv_sem=sem(recv_sems, a, k), device_id=(px, py, c), device_id_type=MESH)

        mine = [pltpu.make_async_copy(ins[a].at[my_chip], outs[a].at[my_chip], local_sems.at[a if per_array_sems else 0])
                for a in range(n)]
        for cp in mine:
            cp.start()
        sent = [copy(a, k, my_chip) for a in range(n) for k in range(3)]
        for cp in sent:
            cp.start()
        for a in range(n):
            for k, (px, py) in enumerate(chips):
                copy(a, k, 2 * px + py).wait_recv()
        for cp in sent:
            cp.wait_send()
        for cp in mine:
            cp.wait()

    return body


def _all_gather(arrs, name):
    n = len(arrs)
    return pl.pallas_call(
        _gather_body(n, True, False), name=name, in_specs=[ANY] * n, out_specs=[ANY] * n,
        out_shape=[jax.ShapeDtypeStruct((N_DEV,) + a.shape, a.dtype) for a in arrs],
        scratch_shapes=[pltpu.SemaphoreType.DMA((n, 7)), pltpu.SemaphoreType.DMA((n, 7)), pltpu.SemaphoreType.DMA((n,))],
    )(*arrs)


def _on_sequencer(body, arrs, out_type, name, collective_id):
    return pl.kernel(
        body, name=name, out_type=out_type, mesh=plsc.ScalarSubcoreMesh(axis_name="sequencer", num_cores=1),
        scratch_types=[pltpu.SemaphoreType.DMA((7,)), pltpu.SemaphoreType.DMA((7,)), pltpu.SemaphoreType.DMA((1,))],
        compiler_params=pltpu.CompilerParams(collective_id=collective_id))(*arrs)


def _all_gather_async(arrs, name, collective_id):
    out_type = [jax.ShapeDtypeStruct((N_DEV,) + a.shape, a.dtype) for a in arrs]
    return _on_sequencer(_gather_body(len(arrs), False, True), arrs, out_type, name, collective_id)


def _sibling_send_async(arrs, name, collective_id):
    out_type = [jax.ShapeDtypeStruct((N_DEV // 2,) + a.shape[1:], a.dtype) for a in arrs]
    return _on_sequencer(_sibling_send_body(len(arrs), False, True), arrs, out_type, name, collective_id)


def _chip_exchange_async(arrs, name, collective_id):
    out_type = [jax.ShapeDtypeStruct(a.shape, a.dtype) for a in arrs]
    return _on_sequencer(_chip_exchange_body(len(arrs), False, True), arrs, out_type, name, collective_id)


def _pair_sum(name, parity, mine, theirs):
    n = len(mine)

    def body(par_ref, *refs):
        for a in range(n):
            refs[2 * n + a][...] = (refs[a][...].astype(F32) + refs[n + a][...].astype(F32)).astype(refs[2 * n + a].dtype)

    def block(a):
        return (None,) + a.shape[1:]

    grid_spec = pltpu.PrefetchScalarGridSpec(
        num_scalar_prefetch=1, grid=(N_DEV // 2,),
        in_specs=[pl.BlockSpec(block(a), lambda q, par: (2 * q + par[0], 0, 0)) for a in mine]
        + [pl.BlockSpec(block(a), lambda q, par: (q, 0, 0)) for a in theirs],
        out_specs=[pl.BlockSpec(block(a), lambda q, par: (q, 0, 0)) for a in theirs])
    return pl.pallas_call(body, name=name, grid_spec=grid_spec, out_shape=[jax.ShapeDtypeStruct(a.shape, a.dtype) for a in theirs],
                          compiler_params=_params())(parity, *mine, *theirs)


NN = (((1,), (0,)), ((), ()))
NT = (((1,), (1,)), ((), ()))
TN = (((0,), (0,)), ((), ()))


def _matmul(name, a, b, dims, grid, a_spec, b_spec, outs, epilogue, extras=(), acc_shape=None):
    n_in = 2 + len(extras)
    n_out = len(outs)
    nk = grid[2]

    def body(*refs):
        a_ref, b_ref = refs[0], refs[1]
        ex, out_refs = refs[2:n_in], refs[n_in:n_in + n_out]

        def part():
            return lax.dot_general(a_ref[...], b_ref[...], dims, preferred_element_type=F32)

        if nk == 1:
            epilogue(part(), ex, out_refs)
            return
        acc = refs[n_in + n_out]
        k = pl.program_id(2)

        @pl.when(k == 0)
        def _():
            acc[...] = part()

        @pl.when(k > 0)
        def _():
            acc[...] += part()

        @pl.when(k == nk - 1)
        def _():
            epilogue(acc[...], ex, out_refs)

    return pl.pallas_call(
        body, name=name, grid=grid, in_specs=[a_spec, b_spec] + [s for _, s in extras], out_specs=[s for _, _, s in outs],
        out_shape=[jax.ShapeDtypeStruct(sh, dt) for sh, dt, _ in outs],
        scratch_shapes=[] if nk == 1 else [pltpu.VMEM(acc_shape, F32)], compiler_params=_params(),
    )(a, b, *[e for e, _ in extras])


def _ep_store(acc, ex, outs):
    outs[0][...] = acc.astype(outs[0].dtype)


def _ep_residual(acc, ex, outs):
    outs[0][...] = acc
    outs[1][...] = ex[0][...] + ex[1][0:1, :] * acc


def _ep_relu2(acc, ex, outs):
    outs[0][...] = acc.astype(outs[0].dtype)
    r = jnp.maximum(acc, 0.0)
    outs[1][...] = (r * r).astype(outs[1].dtype)


def _ep_relu2_grad(acc, ex, outs):
    outs[0][...] = (acc * (2.0 * jnp.maximum(ex[0][...].astype(F32), 0.0))).astype(outs[0].dtype)


def _mm_cols(name, a, wg, epilogue=_ep_store, out_dtypes=(F32,)):
    m, k = a.shape
    sw = wg.shape[2]
    tm = _tile(m, MM_TILE)
    spec = pl.BlockSpec((tm, sw), lambda i, j, kk: (i, j))
    return _matmul(name, a, wg, NN, (m // tm, N_DEV, 1), pl.BlockSpec((tm, k), lambda i, j, kk: (i, 0)),
                   pl.BlockSpec((None, k, sw), lambda i, j, kk: (j, 0, 0)),
                   [((m, N_DEV * sw), dt, spec) for dt in out_dtypes], epilogue)


def _mm_cols_t(name, a, wg, out_dtype=F32):
    m = a.shape[0]
    _, r, sw = wg.shape
    tm, tn = _tile(m, MM_TILE), _tile(r, MM_TILE)
    return _matmul(name, a, wg, NT, (m // tm, r // tn, N_DEV), pl.BlockSpec((tm, sw), lambda i, j, kk: (i, kk)),
                   pl.BlockSpec((None, tn, sw), lambda i, j, kk: (kk, j, 0)),
                   [((m, r), out_dtype, pl.BlockSpec((tm, tn), lambda i, j, kk: (i, j)))], _ep_store, acc_shape=(tm, tn))[0]


def _mm_rows(name, a, w, epilogue, outs_dtypes, extras=()):
    m, k = a.shape
    n = w.shape[1]
    tm, tn, tk = _tile(m, MM_TILE), _tile(n, MM_TILE // 2), _tile(k, MM_TILE)
    spec = pl.BlockSpec((tm, tn), lambda i, j, kk: (i, j))
    return _matmul(name, a, w, NN, (m // tm, n // tn, k // tk), pl.BlockSpec((tm, tk), lambda i, j, kk: (i, kk)),
                   pl.BlockSpec((tk, tn), lambda i, j, kk: (kk, j)), [((m, n), dt, spec) for dt in outs_dtypes], epilogue,
                   extras=[(e, spec if e.shape[0] == m else pl.BlockSpec((SUBLANES, tn), lambda i, j, kk: (0, j))) for e in extras],
                   acc_shape=(tm, tn))


def _mm_rows_t(name, a, w, epilogue=_ep_store, out_dtype=F32, extras=()):
    m, c = a.shape
    r = w.shape[0]
    tm, tn, tk = _tile(m, MM_TILE), _tile(r, MM_TILE // 2), _tile(c, MM_TILE)
    spec = pl.BlockSpec((tm, tn), lambda i, j, kk: (i, j))
    return _matmul(name, a, w, NT, (m // tm, r // tn, c // tk), pl.BlockSpec((tm, tk), lambda i, j, kk: (i, kk)),
                   pl.BlockSpec((tn, tk), lambda i, j, kk: (j, kk)), [((m, r), out_dtype, spec)], epilogue,
                   extras=[(e, spec) for e in extras], acc_shape=(tm, tn))[0]


def _mm_grad(name, a, b, shard_cols):
    t, m = a.shape
    n = b.shape[1]
    tk, tm = _tile(t, MM_TILE), _tile(m, MM_TILE)
    if shard_cols:
        tn = n // N_DEV
        out = ((N_DEV, m, tn), MXU, pl.BlockSpec((None, tm, tn), lambda i, j, kk: (j, i, 0)))
    else:
        tn = _tile(n, MM_TILE)
        out = ((m, n), MXU, pl.BlockSpec((tm, tn), lambda i, j, kk: (i, j)))
    return _matmul(name, a, b, TN, (m // tm, n // tn, t // tk), pl.BlockSpec((tk, tm), lambda i, j, kk: (kk, i)),
                   pl.BlockSpec((tk, tn), lambda i, j, kk: (kk, j)), [out], _ep_store, acc_shape=(tm, tn))[0]


def _rows_spec(tt, width=D):
    return pl.BlockSpec((tt, width), lambda i: (i, 0))


def _vec_spec(rows=SUBLANES, width=D):
    return pl.BlockSpec((rows, width), lambda i: (0, 0))


def _norm_mod(name, x, vec):
    t = x.shape[0]
    tt = _tile(t, ROW_TILE)

    def body(x_ref, v_ref, h_ref):
        xv = x_ref[...]
        r = lax.rsqrt(jnp.mean(xv * xv, axis=-1, keepdims=True) + NORM_EPS)
        h = (xv * r) * v_ref[0:1, :]
        h_ref[...] = (h * (1.0 + v_ref[1:2, :]) + v_ref[2:3, :]).astype(h_ref.dtype)

    return pl.pallas_call(body, name=name, grid=(t // tt,), in_specs=[_rows_spec(tt), _vec_spec()], out_specs=_rows_spec(tt),
                          out_shape=jax.ShapeDtypeStruct((t, D), MXU))(x, vec)


def _norm_mod_grad(name, dh, x, vec, dres):
    t = x.shape[0]
    tt = _tile(t, ROW_TILE)
    nt = t // tt

    def body(dh_ref, x_ref, v_ref, dres_ref, dx_ref, acc_ref):
        i = pl.program_id(0)

        @pl.when(i == 0)
        def _():
            acc_ref[...] = jnp.zeros_like(acc_ref)

        xv, dhv = x_ref[...], dh_ref[...]
        r = lax.rsqrt(jnp.mean(xv * xv, axis=-1, keepdims=True) + NORM_EPS)
        xn = xv * r
        w = v_ref[0:1, :] * (1.0 + v_ref[1:2, :])
        acc_ref[0:1, :] += _row_sum(dhv * xn)
        acc_ref[2:3, :] += _row_sum(dhv)
        dxn = dhv * w
        dx_ref[...] = dres_ref[...] + r * (dxn - xn * jnp.mean(dxn * xn, axis=-1, keepdims=True))

        @pl.when(i == nt - 1)
        def _():
            dw = acc_ref[0:1, :]
            acc_ref[1:2, :] = dw * v_ref[0:1, :]
            acc_ref[0:1, :] = dw * (1.0 + v_ref[1:2, :])

    return pl.pallas_call(
        body, name=name, grid=(nt,), in_specs=[_rows_spec(tt), _rows_spec(tt), _vec_spec(), _rows_spec(tt)],
        out_specs=[_rows_spec(tt), _vec_spec()],
        out_shape=[jax.ShapeDtypeStruct((t, D), F32), jax.ShapeDtypeStruct((SUBLANES, D), F32)])(dh, x, vec, dres)


def _residual_grad(name, dxo, y, vec):
    t = y.shape[0]
    tt = _tile(t, ROW_TILE)

    def body(dx_ref, y_ref, v_ref, dy_ref, acc_ref):
        @pl.when(pl.program_id(0) == 0)
        def _():
            acc_ref[...] = jnp.zeros_like(acc_ref)

        dxv = dx_ref[...]
        dy_ref[...] = (dxv * v_ref[0:1, :]).astype(dy_ref.dtype)
        acc_ref[0:1, :] += _row_sum(dxv * y_ref[...])

    return pl.pallas_call(
        body, name=name, grid=(t // tt,), in_specs=[_rows_spec(tt), _rows_spec(tt), _vec_spec()],
        out_specs=[_rows_spec(tt), _vec_spec()],
        out_shape=[jax.ShapeDtypeStruct((t, D), MXU), jax.ShapeDtypeStruct((SUBLANES, D), F32)])(dxo, y, vec)


def _loss_head(name, x, target, vec):
    t = x.shape[0]
    tt = _tile(t, ROW_TILE)

    def body(x_ref, t_ref, v_ref, dx_ref, acc_ref):
        @pl.when(pl.program_id(0) == 0)
        def _():
            acc_ref[...] = jnp.zeros_like(acc_ref)

        xv = x_ref[...]
        r = lax.rsqrt(jnp.mean(xv * xv, axis=-1, keepdims=True) + NORM_EPS)
        xn = xv * r
        gain = v_ref[0:1, :]
        err = xn * gain - t_ref[...]
        acc_ref[1:2, :] += _row_sum(err * err) * (0.5 / D)
        dout = err * (1.0 / D)
        acc_ref[0:1, :] += _row_sum(dout * xn)
        dxn = dout * gain
        dx_ref[...] = r * (dxn - xn * jnp.mean(dxn * xn, axis=-1, keepdims=True))

    return pl.pallas_call(
        body, name=name, grid=(t // tt,), in_specs=[_rows_spec(tt), _rows_spec(tt), _vec_spec()],
        out_specs=[_rows_spec(tt), _vec_spec()],
        out_shape=[jax.ShapeDtypeStruct((t, D), F32), jax.ShapeDtypeStruct((SUBLANES, D), F32)])(x, target, vec)


def _shift_down(x, halo, k):
    y = pltpu.roll(x, k, 0)
    top = jnp.where(lax.broadcasted_iota(jnp.int32, halo.shape, 0) < k, pltpu.roll(halo, k, 0), y[0:SUBLANES, :])
    return jnp.concatenate([top, y[SUBLANES:, :]], axis=0)


def _shift_up(x, halo, k):
    n = x.shape[0]
    y = pltpu.roll(x, n - k, 0)
    bottom = jnp.where(lax.broadcasted_iota(jnp.int32, halo.shape, 0) >= SUBLANES - k, pltpu.roll(halo, SUBLANES - k, 0),
                       y[n - SUBLANES:, :])
    return jnp.concatenate([y[:n - SUBLANES, :], bottom], axis=0)


def _rg_gates(xb, halo, cw_ref, vec_ref, wa_ref, wx_ref, at_start):
    shifted = [xb] + [_shift_down(xb, halo, k) for k in range(1, CONV_WIDTH)]
    xc = vec_ref[0:1, :] + shifted[0] * cw_ref[CONV_WIDTH - 1:CONV_WIDTH, :]
    for k in range(1, CONV_WIDTH):
        xc = xc + shifted[k] * cw_ref[CONV_WIDTH - 1 - k:CONV_WIDTH - k, :]
    heads = [slice(h * HEAD_DIM, (h + 1) * HEAD_DIM) for h in range(HEADS)]
    pa = jnp.concatenate([_dot(xc[:, s], wa_ref[h]) for h, s in enumerate(heads)], axis=1) + vec_ref[1:2, :]
    px = jnp.concatenate([_dot(xc[:, s], wx_ref[h]) for h, s in enumerate(heads)], axis=1) + vec_ref[2:3, :]
    ra, ia = _sigmoid(pa), _sigmoid(px)
    nl = -vec_ref[3:4, :]
    sp = jnp.maximum(nl, 0.0) + jnp.log(1.0 + jnp.exp(-jnp.abs(nl)))
    log_a = (-LRU_C) * ra * sp
    a = jnp.exp(log_a)
    th = jnp.tanh(log_a)
    is_t0 = jnp.logical_and(lax.broadcasted_iota(jnp.int32, xb.shape, 0) == 0, at_start)
    mult = jnp.where(is_t0, 1.0, jnp.sqrt(-2.0 * th / (1.0 - th)))
    return dict(shifted=shifted, xc=xc, ra=ra, ia=ia, sp=sp, a=a, mult=mult, is_t0=is_t0, heads=heads)


def _rg_specs(tt, nt, order):
    blk = tt // SUBLANES
    return dict(
        x=pl.BlockSpec((tt, D), lambda i: (order(i), 0)), y=pl.BlockSpec((tt, D), lambda i: (order(i), 1)),
        halo=pl.BlockSpec((SUBLANES, D), lambda i: (jnp.maximum(order(i) * blk - 1, 0), 0)),
        cw=_vec_spec(CONV_WIDTH), vec=_vec_spec(), w=pl.BlockSpec((HEADS, HEAD_DIM, HEAD_DIM), lambda i: (0, 0, 0)))


def _rg_forward(name, z, cw, vec, wa, wx):
    t = z.shape[0]
    tt = _tile(t, ROW_TILE)
    nt = t // tt
    sp = _rg_specs(tt, nt, lambda i: i)

    def body(zx_ref, zy_ref, halo_ref, cw_ref, vec_ref, wa_ref, wx_ref, p_ref, h_ref, a_s, u_s, carry):
        i = pl.program_id(0)

        @pl.when(i == 0)
        def _():
            carry[...] = jnp.zeros_like(carry)

        halo = jnp.where(i > 0, halo_ref[...], 0.0)
        g = _rg_gates(zx_ref[...], halo, cw_ref, vec_ref, wa_ref, wx_ref, i == 0)
        a_s[...] = g["a"]
        u_s[...] = g["mult"] * (g["ia"] * g["xc"])

        def group(gi, h):
            rows = pl.ds(pl.multiple_of(gi * SUBLANES, SUBLANES), SUBLANES)
            a8, u8 = a_s[rows, :], u_s[rows, :]
            out = []
            for j in range(SUBLANES):
                h = a8[j:j + 1, :] * h + u8[j:j + 1, :]
                out.append(h)
            h_ref[rows, :] = jnp.concatenate(out, axis=0)
            return h

        carry[0:1, :] = lax.fori_loop(0, tt // SUBLANES, group, carry[0:1, :])
        p_ref[...] = (h_ref[...] * _gelu(zy_ref[...])).astype(p_ref.dtype)

    return pl.pallas_call(
        body, name=name, grid=(nt,), in_specs=[sp["x"], sp["y"], sp["halo"], sp["cw"], sp["vec"], sp["w"], sp["w"]],
        out_specs=[_rows_spec(tt), _rows_spec(tt)],
        out_shape=[jax.ShapeDtypeStruct((t, D), MXU), jax.ShapeDtypeStruct((t, D), F32)],
        scratch_shapes=[pltpu.VMEM((tt, D), F32), pltpu.VMEM((tt, D), F32), pltpu.VMEM((SUBLANES, D), F32)],
        compiler_params=_params())(z, z, z, cw, vec, wa, wx)


def _rg_backward(name, dp, z, h, cw, vec, wa, wx):
    t = z.shape[0]
    tt = _tile(t, ROW_TILE)
    nt = t // tt
    rev = lambda i: nt - 1 - i
    sp = _rg_specs(tt, nt, rev)
    rows_rev = pl.BlockSpec((tt, D), lambda i: (rev(i), 0))

    def body(dp_ref, zx_ref, zy_ref, halo_ref, h_ref, hhalo_ref, cw_ref, vec_ref, wa_ref, wx_ref,
             dz_ref, dvec_ref, dcw_ref, dwa_ref, dwx_ref, a_s, d_s, carry, nxt):
        i = pl.program_id(0)
        j = rev(i)

        @pl.when(i == 0)
        def _():
            carry[...] = jnp.zeros_like(carry)
            nxt[...] = jnp.zeros_like(nxt)
            dvec_ref[...] = jnp.zeros_like(dvec_ref)
            dcw_ref[...] = jnp.zeros_like(dcw_ref)
            dwa_ref[...] = jnp.zeros_like(dwa_ref)
            dwx_ref[...] = jnp.zeros_like(dwx_ref)

        halo = jnp.where(j > 0, halo_ref[...], 0.0)
        g = _rg_gates(zx_ref[...], halo, cw_ref, vec_ref, wa_ref, wx_ref, j == 0)
        xc, ra, ia, a, mult = g["xc"], g["ra"], g["ia"], g["a"], g["mult"]
        hv, zy, dpv = h_ref[...], zy_ref[...], dp_ref[...]
        dyb = dpv * hv * _gelu_grad(zy)
        a_s[...] = a
        d_s[...] = dpv * _gelu(zy)

        def group(gi, c):
            rows = pl.ds(pl.multiple_of((tt // SUBLANES - 1 - gi) * SUBLANES, SUBLANES), SUBLANES)
            a8, d8 = a_s[rows, :], d_s[rows, :]
            out = [None] * SUBLANES
            for r in reversed(range(SUBLANES)):
                dht = d8[r:r + 1, :] + c
                out[r] = dht
                c = a8[r:r + 1, :] * dht
            d_s[rows, :] = jnp.concatenate(out, axis=0)
            return c

        carry[0:1, :] = lax.fori_loop(0, tt // SUBLANES, group, carry[0:1, :])
        dht = d_s[...]
        hprev = _shift_down(hv, jnp.where(j > 0, hhalo_ref[...], 0.0), 1)
        ixc = ia * xc
        dlog_a = dht * hprev * a + jnp.where(g["is_t0"], 0.0, dht * ixc * (-(a * a) / mult))
        dia = dht * mult * xc
        dxc = dht * mult * ia
        dra = dlog_a * ((-LRU_C) * g["sp"])
        nl = -vec_ref[3:4, :]
        dvec_ref[3:4, :] += _row_sum(dlog_a * ((-LRU_C) * ra)) * (-_sigmoid(nl))
        dpa = dra * ra * (1.0 - ra)
        dpx = dia * ia * (1.0 - ia)
        dvec_ref[1:2, :] += _row_sum(dpa)
        dvec_ref[2:3, :] += _row_sum(dpx)
        back = []
        for hd, s in enumerate(g["heads"]):
            dwa_ref[hd] += _dot_tn(xc[:, s], dpa[:, s])
            dwx_ref[hd] += _dot_tn(xc[:, s], dpx[:, s])
            back.append(_dot_nt(dpa[:, s], wa_ref[hd]) + _dot_nt(dpx[:, s], wx_ref[hd]))
        dxc = dxc + jnp.concatenate(back, axis=1)
        dvec_ref[0:1, :] += _row_sum(dxc)
        dxb = dxc * cw_ref[CONV_WIDTH - 1:CONV_WIDTH, :]
        for k in range(CONV_WIDTH):
            row = CONV_WIDTH - 1 - k
            dcw_ref[row:row + 1, :] += _row_sum(dxc * g["shifted"][k])
            if k:
                dxb = dxb + _shift_up(dxc, nxt[...], k) * cw_ref[row:row + 1, :]
        nxt[...] = dxc[0:SUBLANES, :]
        dz_ref[:, 0:D] = dxb.astype(dz_ref.dtype)
        dz_ref[:, D:2 * D] = dyb.astype(dz_ref.dtype)

    hhalo = pl.BlockSpec((SUBLANES, D), lambda i: (jnp.maximum(rev(i) * (tt // SUBLANES) - 1, 0), 0))
    wacc = pl.BlockSpec((HEADS, HEAD_DIM, HEAD_DIM), lambda i: (0, 0, 0))
    return pl.pallas_call(
        body, name=name, grid=(nt,),
        in_specs=[rows_rev, sp["x"], sp["y"], sp["halo"], rows_rev, hhalo, sp["cw"], sp["vec"], sp["w"], sp["w"]],
        out_specs=[pl.BlockSpec((tt, 2 * D), lambda i: (rev(i), 0)), _vec_spec(), _vec_spec(), wacc, wacc],
        out_shape=[jax.ShapeDtypeStruct((t, 2 * D), MXU), jax.ShapeDtypeStruct((SUBLANES, D), F32),
                   jax.ShapeDtypeStruct((SUBLANES, D), F32), jax.ShapeDtypeStruct((HEADS, HEAD_DIM, HEAD_DIM), F32),
                   jax.ShapeDtypeStruct((HEADS, HEAD_DIM, HEAD_DIM), F32)],
        scratch_shapes=[pltpu.VMEM((tt, D), F32), pltpu.VMEM((tt, D), F32), pltpu.VMEM((SUBLANES, D), F32),
                        pltpu.VMEM((SUBLANES, D), F32)],
        compiler_params=_params())(dp, z, z, z, h, h, cw, vec, wa, wx)


def _hg_chunk(zq, zf, zi, lb):
    c = HG_CHUNK
    q = _silu(zq)
    sg = _sigmoid(zf)
    fg = lb + (1.0 - lb) * sg
    k = 1.0 - fg
    row, col = lax.broadcasted_iota(jnp.int32, (c, c), 0), lax.broadcasted_iota(jnp.int32, (c, c), 1)
    tri, tri_t = (row >= col).astype(F32), (row <= col).astype(F32)
    b = _dot_exact(tri, jnp.log(fg))
    mid, last = b[c // 2 - 1:c // 2, :], b[c - 1:c, :]
    eq = jnp.exp(jnp.minimum(b - mid, EXP_CLAMP))
    ek = jnp.exp(jnp.minimum(mid - b, EXP_CLAMP))
    eb = jnp.exp(b)
    ed = jnp.exp(last - b)
    return dict(q=q, sg=sg, fg=fg, k=k, v=zi, eq=eq, ek=ek, eb=eb, ed=ed, elast=jnp.exp(last), tri=tri, tri_t=tri_t,
                qe=q * eq, ke=k * ek, qb=q * eb, kd=k * ed)


def _hg_specs(tt, order):
    return [pl.BlockSpec((tt, D), lambda i, col=col: (order(i), col)) for col in range(4)]


def _hg_forward(name, z, vec):
    t = z.shape[0]
    tt = _tile(t, ROW_TILE)
    nt, nc = t // tt, tt // HG_CHUNK
    heads = [slice(h * HEAD_DIM, (h + 1) * HEAD_DIM) for h in range(HEADS)]

    def body(zq_ref, zf_ref, zi_ref, zg_ref, vec_ref, p_ref, o_ref, ss_ref, state):
        @pl.when(pl.program_id(0) == 0)
        def _():
            state[...] = jnp.zeros_like(state)

        lb, gain = vec_ref[0:1, :], vec_ref[1:2, :]

        def chunk(ci, carry):
            rows = pl.ds(pl.multiple_of(ci * HG_CHUNK, HG_CHUNK), HG_CHUNK)
            g = _hg_chunk(zq_ref[rows, :], zf_ref[rows, :], zi_ref[rows, :], lb)
            zg = zg_ref[rows, :]
            causal = g["tri"] > 0.0
            o_parts, p_parts = [], []
            for hd, s in enumerate(heads):
                st = state[hd]
                ss_ref[ci, hd] = st
                att = jnp.where(causal, _dot_nt(g["qe"][:, s], g["ke"][:, s]), 0.0)
                o = _dot(att, g["v"][:, s]) + _dot_nt(g["qb"][:, s], st)
                state[hd] = st * g["elast"][:, s] + _dot_tn(g["v"][:, s], g["kd"][:, s])
                r = lax.rsqrt(jnp.mean(o * o, axis=-1, keepdims=True) + GNORM_EPS)
                o_parts.append(o)
                p_parts.append((o * r) * gain[:, s])
            o_ref[rows, :] = jnp.concatenate(o_parts, axis=1)
            p_ref[rows, :] = (jnp.concatenate(p_parts, axis=1) * _silu(zg)).astype(p_ref.dtype)
            return carry

        lax.fori_loop(0, nc, chunk, 0)

    return pl.pallas_call(
        body, name=name, grid=(nt,), in_specs=_hg_specs(tt, lambda i: i) + [_vec_spec()],
        out_specs=[_rows_spec(tt), _rows_spec(tt), pl.BlockSpec((nc, HEADS, HEAD_DIM, HEAD_DIM), lambda i: (i, 0, 0, 0))],
        out_shape=[jax.ShapeDtypeStruct((t, D), MXU), jax.ShapeDtypeStruct((t, D), F32),
                   jax.ShapeDtypeStruct((t // HG_CHUNK, HEADS, HEAD_DIM, HEAD_DIM), F32)],
        scratch_shapes=[pltpu.VMEM((HEADS, HEAD_DIM, HEAD_DIM), F32)], compiler_params=_params())(z, z, z, z, vec)


def _hg_backward(name, dp, z, o, ss, vec):
    t = z.shape[0]
    tt = _tile(t, ROW_TILE)
    nt, nc = t // tt, tt // HG_CHUNK
    rev = lambda i: nt - 1 - i
    heads = [slice(h * HEAD_DIM, (h + 1) * HEAD_DIM) for h in range(HEADS)]
    rows_rev = pl.BlockSpec((tt, D), lambda i: (rev(i), 0))

    def body(dp_ref, zq_ref, zf_ref, zi_ref, zg_ref, o_ref, ss_ref, vec_ref, dz_ref, acc_ref,
             dstate, dqe_s, dke_s, dqb_s, dkd_s, dv_s, dzg_s, dlast_s):
        @pl.when(pl.program_id(0) == 0)
        def _():
            dstate[...] = jnp.zeros_like(dstate)
            acc_ref[...] = jnp.zeros_like(acc_ref)

        lb, gain = vec_ref[0:1, :], vec_ref[1:2, :]

        def chunk(cr, carry):
            ci = nc - 1 - cr
            rows = pl.ds(pl.multiple_of(ci * HG_CHUNK, HG_CHUNK), HG_CHUNK)
            zq, zg = zq_ref[rows, :], zg_ref[rows, :]
            g = _hg_chunk(zq, zf_ref[rows, :], zi_ref[rows, :], lb)
            ov, dpv = o_ref[rows, :], dp_ref[rows, :]
            causal = g["tri"] > 0.0
            don = dpv * _silu(zg)
            dgate = dpv * _silu_grad(zg)
            for hd, s in enumerate(heads):
                oh = ov[:, s]
                r = lax.rsqrt(jnp.mean(oh * oh, axis=-1, keepdims=True) + GNORM_EPS)
                on = oh * r
                dzg_s[:, s] = dgate[:, s] * (on * gain[:, s])
                acc_ref[1:2, s] += _row_sum(don[:, s] * on)
                dtmp = don[:, s] * gain[:, s]
                do = r * (dtmp - on * jnp.mean(dtmp * on, axis=-1, keepdims=True))
                st, dst = ss_ref[ci, hd], dstate[hd]
                qe, ke, qb, kd, v = g["qe"][:, s], g["ke"][:, s], g["qb"][:, s], g["kd"][:, s], g["v"][:, s]
                att = jnp.where(causal, _dot_nt(qe, ke), 0.0)
                datt = jnp.where(causal, _dot_nt(do, v), 0.0)
                dv_s[:, s] = _dot_tn(att, do) + _dot_nt(kd, dst)
                dqe_s[:, s] = _dot(datt, ke)
                dke_s[:, s] = _dot_tn(datt, qe)
                dqb_s[:, s] = _dot(do, st)
                dkd_s[:, s] = _dot(v, dst)
                dlast_s[0:1, s] = g["elast"][:, s] * _row_sum(dst * st)
                dstate[hd] = dst * g["elast"][:, s] + _dot_tn(do, qb)
            dqe, dke, dqb, dkd = dqe_s[...], dke_s[...], dqb_s[...], dkd_s[...]
            dq = dqe * g["eq"] + dqb * g["eb"]
            dk = dke * g["ek"] + dkd * g["ed"]
            dkdk = dkd * g["kd"]
            db = dqe * g["qe"].astype(MXU).astype(F32) - dke * g["ke"].astype(MXU).astype(F32) + dqb * g["qb"] - dkdk
            dlogf = _dot_exact(g["tri_t"], db) + (dlast_s[0:1, :] + _row_sum(dkdk))
            dfg = dlogf / g["fg"] - dk
            sg = g["sg"]
            acc_ref[0:1, :] += _row_sum(dfg * (1.0 - sg))
            dz_ref[rows, 0:D] = (dq * _silu_grad(zq)).astype(dz_ref.dtype)
            dz_ref[rows, D:2 * D] = (dfg * (1.0 - lb) * sg * (1.0 - sg)).astype(dz_ref.dtype)
            dz_ref[rows, 2 * D:3 * D] = dv_s[...].astype(dz_ref.dtype)
            dz_ref[rows, 3 * D:4 * D] = dzg_s[...].astype(dz_ref.dtype)
            return carry

        lax.fori_loop(0, nc, chunk, 0)

    chunk_buf = pltpu.VMEM((HG_CHUNK, D), F32)
    return pl.pallas_call(
        body, name=name, grid=(nt,),
        in_specs=[rows_rev] + _hg_specs(tt, rev) + [rows_rev, pl.BlockSpec((nc, HEADS, HEAD_DIM, HEAD_DIM), lambda i: (rev(i), 0, 0, 0)),
                                                  _vec_spec()],
        out_specs=[pl.BlockSpec((tt, 4 * D), lambda i: (rev(i), 0)), _vec_spec()],
        out_shape=[jax.ShapeDtypeStruct((t, 4 * D), MXU), jax.ShapeDtypeStruct((SUBLANES, D), F32)],
        scratch_shapes=[pltpu.VMEM((HEADS, HEAD_DIM, HEAD_DIM), F32)] + [chunk_buf] * 6 + [pltpu.VMEM((SUBLANES, D), F32)],
        compiler_params=_params())(dp, z, z, z, z, o, ss, vec)


def _mod_forward(name, c_all, mod_w, lower):
    depth, _, sw = mod_w.shape

    def body(c_ref, w_ref, lo_ref, cs_ref, mod_ref, lb_ref):
        cs = _silu(c_ref[...])
        mod_ref[...] = _dot(cs, w_ref[...])

        @pl.when(pl.program_id(0) == 0)
        def _():
            cs_ref[...] = cs
            lo = lo_ref[...]
            e = jnp.exp(lo - jnp.max(lo, axis=0, keepdims=True))
            sm = e / jnp.sum(e, axis=0, keepdims=True)
            lb_ref[0:1, :] = jnp.zeros((1, D), F32)
            for l in range(1, depth):
                lb_ref[l:l + 1, :] = lb_ref[l - 1:l, :] + sm[l:l + 1, :]

    return pl.pallas_call(
        body, name=name, grid=(depth,),
        in_specs=[pl.BlockSpec((N_DEV, D), lambda l: (0, 0)), pl.BlockSpec((None, D, sw), lambda l: (l, 0, 0)),
                  pl.BlockSpec((depth, D), lambda l: (0, 0))],
        out_specs=[pl.BlockSpec((N_DEV, D), lambda l: (0, 0)), pl.BlockSpec((None, N_DEV, sw), lambda l: (l, 0, 0)),
                   pl.BlockSpec((depth, D), lambda l: (0, 0))],
        out_shape=[jax.ShapeDtypeStruct((N_DEV, D), F32), jax.ShapeDtypeStruct((depth, N_DEV, sw), F32),
                   jax.ShapeDtypeStruct((depth, D), F32)], compiler_params=_params())(c_all, mod_w, lower)


def _mod_weight_grad(name, cs_t, dmod):
    depth, pad, sw = dmod.shape

    def body(c_ref, d_ref, g_ref):
        g_ref[...] = _dot(c_ref[...], d_ref[...])

    return pl.pallas_call(
        body, name=name, grid=(depth,),
        in_specs=[pl.BlockSpec((D, pad), lambda l: (0, 0)), pl.BlockSpec((None, pad, sw), lambda l: (l, 0, 0))],
        out_specs=pl.BlockSpec((None, D, sw), lambda l: (l, 0, 0)),
        out_shape=jax.ShapeDtypeStruct((depth, D, sw), F32), compiler_params=_params())(cs_t, dmod)


def _lower_bound_grad(name, lower, dlb):
    depth = lower.shape[0]

    def body(lo_ref, d_ref, out_ref):
        lo = lo_ref[...]
        e = jnp.exp(lo - jnp.max(lo, axis=0, keepdims=True))
        sm = e / jnp.sum(e, axis=0, keepdims=True)
        out_ref[...] = jnp.zeros_like(out_ref)
        dsm = [jnp.zeros((1, D), F32)]
        for l in range(1, depth):
            tail = d_ref[l:l + 1, :]
            for m in range(l + 1, depth):
                tail = tail + d_ref[m:m + 1, :]
            dsm.append(tail)
        inner = sm[1:2, :] * dsm[1]
        for l in range(2, depth):
            inner = inner + sm[l:l + 1, :] * dsm[l]
        for l in range(depth):
            out_ref[l:l + 1, :] = sm[l:l + 1, :] * (dsm[l] - inner)
        for j in range(SUBLANES - depth):
            row = d_ref[depth + j:depth + j + 1, :]
            tot = row[:, 0:HEAD_DIM]
            for hd in range(1, HEADS):
                tot = tot + row[:, hd * HEAD_DIM:(hd + 1) * HEAD_DIM]
            out_ref[depth + j:depth + j + 1, 0:HEAD_DIM] = tot

    return pl.pallas_call(body, name=name, out_shape=jax.ShapeDtypeStruct((SUBLANES, D), F32))(lower, dlb)


def _adamw(name, parts, w, m, v, layer=None, prev=None):
    p, r, c = parts.shape
    tr = _tile(r, max(SUBLANES, ADAMW_STEP_BYTES // (4 * c * (p + 7))))
    stacked = layer is not None
    n_prev = 4 if prev is not None else 0

    def body(*refs):
        parts_ref, w_ref, m_ref, v_ref = refs[:4]
        g_ref, d_ref, m_out, v_out = refs[4 + n_prev:]
        g = parts_ref[0].astype(F32)
        for q in range(1, p):
            g = g + parts_ref[q].astype(F32)
        m2 = ADAM_B1 * m_ref[...] + (1.0 - ADAM_B1) * g
        v2 = ADAM_B2 * v_ref[...] + (1.0 - ADAM_B2) * (g * g)
        m_hat = m2 / (1.0 - ADAM_B1 ** ADAM_STEP)
        v_hat = v2 / (1.0 - ADAM_B2 ** ADAM_STEP)
        g_ref[...] = g
        d_ref[...] = -ADAM_LR * (m_hat / (jnp.sqrt(v_hat) + ADAM_EPS) + ADAM_WD * w_ref[...])
        m_out[...] = m2
        v_out[...] = v2

    if stacked:
        spec = pl.BlockSpec((None, tr, c), lambda i: (layer, i, 0))
    else:
        spec = pl.BlockSpec((tr, c), lambda i: (i, 0))
    return pl.pallas_call(
        body, name=name, grid=(r // tr,),
        in_specs=[pl.BlockSpec((p, tr, c), lambda i: (0, i, 0)), spec, spec, spec] + [ANY] * n_prev, out_specs=[spec] * 4,
        out_shape=[jax.ShapeDtypeStruct(w.shape, F32)] * 4,
        input_output_aliases={4 + q: q for q in range(n_prev)}, compiler_params=_params(),
    )(parts, w, m, v, *(prev or []))


def _vec(*rows):
    rows = [r.reshape(1, D).astype(F32) for r in rows]
    return jnp.concatenate(rows + [jnp.zeros((SUBLANES - len(rows), D), F32)], axis=0)


def _pad_rows(a, rows=SUBLANES):
    a = a.reshape(-1, a.shape[-1])
    return jnp.concatenate([a, jnp.zeros((rows - a.shape[0], a.shape[1]), a.dtype)], axis=0) if a.shape[0] < rows else a


def kernel(x, c, mod_w, mod_b, norm_mix, norm_mlp, norm_final, rg_w_in, rg_conv_w, rg_conv_b, rg_w_a, rg_b_a, rg_w_x, rg_b_x, rg_lambda, rg_w_out, hg_w_in, hg_lower_bounds, hg_gnorm, hg_w_out, mlp_w1, mlp_w2, loss_target, m_mod_w, m_mod_b, m_norm_mix, m_norm_mlp, m_norm_final, m_rg_w_in, m_rg_conv_w, m_rg_conv_b, m_rg_w_a, m_rg_b_a, m_rg_w_x, m_rg_b_x, m_rg_lambda, m_rg_w_out, m_hg_w_in, m_hg_lower_bounds, m_hg_gnorm, m_hg_w_out, m_mlp_w1, m_mlp_w2, v_mod_w, v_mod_b, v_norm_mix, v_norm_mlp, v_norm_final, v_rg_w_in, v_rg_conv_w, v_rg_conv_b, v_rg_w_a, v_rg_b_a, v_rg_w_x, v_rg_b_x, v_rg_lambda, v_rg_w_out, v_hg_w_in, v_hg_lower_bounds, v_hg_gnorm, v_hg_w_out, v_mlp_w1, v_mlp_w2):
    me = 4 * lax.axis_index("x") + 2 * lax.axis_index("y") + lax.axis_index("c")
    x0 = x[0]
    target = loss_target[0]
    n_rg, n_hg = rg_w_in.shape[0], hg_w_in.shape[0]
    sw_mod = mod_w.shape[2]

    c_all, cw_all = _all_gather([_pad_rows(c), rg_conv_w.reshape(n_rg * CONV_WIDTH, -1)], "gather_cond")
    c_all = c_all[:, 0, :]
    conv_w = cw_all.transpose(1, 0, 2).reshape(n_rg, CONV_WIDTH, D)
    cs_all, mod_part, lb_all = _mod_forward("mod_forward", c_all, mod_w, hg_lower_bounds)
    (mod_gathered,) = _all_gather([mod_part.reshape(DEPTH * N_DEV, sw_mod)], "gather_mod")

    ids = iter(range(4 * DEPTH))
    shards = []
    for layer in range(DEPTH):
        j = layer // 2
        w_in, w_out = (rg_w_in[j], rg_w_out[j]) if layer % 2 == 0 else (hg_w_in[j], hg_w_out[j])
        shards.append([w_in.astype(MXU), w_out.astype(MXU), mlp_w1[layer].astype(MXU), mlp_w2[layer].astype(MXU)])
    shards, mod_gathered = lax.optimization_barrier((shards, mod_gathered))
    weights = []
    for layer in range(DEPTH):
        if layer == 0:
            g_in, g_out = _all_gather_async(shards[0][:2], "gather_mixer_weights_0", next(ids))
            g_w1, g_w2 = _all_gather_async(shards[0][2:], "gather_mlp_weights_0", next(ids))
        else:
            g_in, g_out, g_w1, g_w2 = _all_gather_async(shards[layer], f"gather_weights_{layer}", next(ids))
        weights.append(dict(w_in=g_in, w_out=g_out.reshape(D, D), w1=g_w1, w2=g_w2.reshape(-1, D)))

    mod_mine = lax.dynamic_index_in_dim(mod_gathered.reshape(N_DEV, DEPTH, N_DEV, sw_mod), me, axis=2, keepdims=False)
    mod = mod_mine.transpose(1, 0, 2).reshape(DEPTH, 6, D) + mod_b.reshape(DEPTH, 6, D)

    saved = []
    xl = x0
    for layer in range(DEPTH):
        j, wt = layer // 2, weights[layer]
        is_rg = layer % 2 == 0
        s = dict(x=xl)
        s["vec_mix"] = _vec(norm_mix[layer], mod[layer, 1], mod[layer, 0])
        s["vec_mlp"] = _vec(norm_mlp[layer], mod[layer, 4], mod[layer, 3])
        s["gate_mix"], s["gate_mlp"] = _vec(mod[layer, 2]), _vec(mod[layer, 5])
        s["h"] = _norm_mod("norm_mod", xl, s["vec_mix"])
        if is_rg:
            (s["z"],) = _mm_cols("rg_in", s["h"], wt["w_in"])
            s["cw"] = conv_w[j]
            s["vec"] = _vec(rg_conv_b[j], rg_b_a[j], rg_b_x[j], rg_lambda[j])
            s["wa"], s["wx"] = rg_w_a[j].astype(MXU), rg_w_x[j].astype(MXU)
            s["p"], s["hr"] = _rg_forward("rg_forward", s["z"], s["cw"], s["vec"], s["wa"], s["wx"])
        else:
            (s["z"],) = _mm_cols("hg_in", s["h"], wt["w_in"])
            s["vec"] = _vec(lb_all[layer], jnp.tile(hg_gnorm[j], HEADS))
            s["p"], s["o"], s["ss"] = _hg_forward("hg_forward", s["z"], s["vec"])
        s["y"], s["x1"] = _mm_rows("mix_out", s["p"], wt["w_out"], _ep_residual, (F32, F32), extras=(xl, s["gate_mix"]))
        s["h2"] = _norm_mod("norm_mod", s["x1"], s["vec_mlp"])
        s["a"], s["s"] = _mm_cols("mlp_in", s["h2"], wt["w1"], _ep_relu2, (MXU, MXU))
        s["ff"], xl = _mm_rows("mlp_out", s["s"], wt["w2"], _ep_residual, (F32, F32), extras=(s["x1"], s["gate_mlp"]))
        saved.append(s)

    dx, head = _loss_head("loss_head", xl, target, _vec(norm_final))

    results = {}
    big = dict(rg_w_in=(rg_w_in, m_rg_w_in, v_rg_w_in), rg_w_out=(rg_w_out, m_rg_w_out, v_rg_w_out),
               hg_w_in=(hg_w_in, m_hg_w_in, v_hg_w_in), hg_w_out=(hg_w_out, m_hg_w_out, v_hg_w_out),
               mlp_w1=(mlp_w1, m_mlp_w1, v_mlp_w1), mlp_w2=(mlp_w2, m_mlp_w2, v_mlp_w2))

    def update_layer(layer, landed):
        mixer = "rg" if layer % 2 == 0 else "hg"
        l_in, l_out, l_w1, l_w2 = landed
        for nm, parts, idx in ((f"{mixer}_w_in", l_in, layer // 2), (f"{mixer}_w_out", l_out, layer // 2),
                               ("mlp_w1", l_w1, layer), ("mlp_w2", l_w2, layer)):
            w, m, v = big[nm]
            results[nm] = _adamw(f"adamw_{nm}", parts, w, m, v, layer=idx, prev=results.get(nm))

    landed = [None] * DEPTH
    small = [None] * DEPTH
    parity = lax.axis_index("c").astype(jnp.int32).reshape(1)

    def send_chip_sums(layer, grads, from_sibling):
        sums = _pair_sum("pair_sum", parity, grads, from_sibling)
        landed[layer] = _chip_exchange_async(sums, f"exchange_grads_{layer}", next(ids))

    pending = None
    for layer in reversed(range(DEPTH)):
        j, wt, s = layer // 2, weights[layer], saved[layer]
        is_rg = layer % 2 == 0
        dff, g_mlp = _residual_grad("residual_grad", dx, s["ff"], s["gate_mlp"])
        da = _mm_rows_t("mlp_out_t", dff, wt["w2"], _ep_relu2_grad, MXU, extras=(s["a"],))
        dw2 = _mm_grad("mlp_out_grad", s["s"], dff, False).reshape(N_DEV, -1, D)
        dw1 = _mm_grad("mlp_in_grad", s["h2"], da, True)
        dh2 = _mm_cols_t("mlp_in_t", da, wt["w1"])
        if pending is not None:
            send_chip_sums(*pending)
        dx1, n_mlp = _norm_mod_grad("norm_mod_grad", dh2, s["x1"], s["vec_mlp"], dx)
        dyb, g_mix = _residual_grad("residual_grad", dx1, s["y"], s["gate_mix"])
        dp = _mm_rows_t("mix_out_t", dyb, wt["w_out"])
        dw_out = _mm_grad("mix_out_grad", s["p"], dyb, False).reshape(N_DEV, -1, D)
        if is_rg:
            dz, dvec, dcw, dwa, dwx = _rg_backward("rg_backward", dp, s["z"], s["hr"], s["cw"], s["vec"], s["wa"], s["wx"])
            gate_parts = _all_gather_async([dwa.reshape(-1, HEAD_DIM).astype(MXU), dwx.reshape(-1, HEAD_DIM).astype(MXU)],
                                           f"gather_gate_grads_{j}", next(ids))
            mixer_small = dict(dvec=dvec, dcw=dcw, gate_parts=gate_parts)
            dw_in = _mm_grad("rg_in_grad", s["h"], dz, True)
            dh = _mm_cols_t("rg_in_t", dz, wt["w_in"])
        else:
            dz, dvec = _hg_backward("hg_backward", dp, s["z"], s["o"], s["ss"], s["vec"])
            mixer_small = dict(dvec=dvec)
            dw_in = _mm_grad("hg_in_grad", s["h"], dz, True)
            dh = _mm_cols_t("hg_in_t", dz, wt["w_in"])
        grads = [dw_in, dw_out, dw1, dw2]
        pending = (layer, grads, _sibling_send_async(grads, f"pair_grads_{layer}", next(ids)))
        dx, n_mix = _norm_mod_grad("norm_mod_grad", dh, s["x"], s["vec_mix"], dx1)
        small[layer] = dict(g_mlp=g_mlp, n_mlp=n_mlp, g_mix=g_mix, n_mix=n_mix, **mixer_small)
        if layer + 1 < DEPTH:
            update_layer(layer + 1, landed[layer + 1])

    send_chip_sums(*pending)

    dlb_rows = [jnp.zeros((1, D), F32) if l % 2 == 0 else small[l]["dvec"][0:1] for l in range(DEPTH)]
    dgn_rows = [small[2 * j + 1]["dvec"][1:2] for j in range(n_hg)]
    lb_grad = _lower_bound_grad("lower_bound_grad", hg_lower_bounds, _pad_rows(jnp.concatenate(dlb_rows + dgn_rows, axis=0)))
    dmod = jnp.stack([jnp.concatenate([small[l]["n_mix"][2], small[l]["n_mix"][1], small[l]["g_mix"][0],
                                       small[l]["n_mlp"][2], small[l]["n_mlp"][1], small[l]["g_mlp"][0]]) for l in range(DEPTH)])
    groups = [
        dmod.reshape(DEPTH * 6, D),
        jnp.stack([small[l]["n_mix"][0] for l in range(DEPTH)]),
        jnp.stack([small[l]["n_mlp"][0] for l in range(DEPTH)]),
        head[0:1],
        jnp.stack([small[2 * j]["dvec"][0] for j in range(n_rg)]),
        jnp.stack([small[2 * j]["dvec"][1] for j in range(n_rg)]),
        jnp.stack([small[2 * j]["dvec"][2] for j in range(n_rg)]),
        jnp.stack([small[2 * j]["dvec"][3] for j in range(n_rg)]),
        lb_grad[0:DEPTH],
        jnp.concatenate([lb_grad[DEPTH + j:DEPTH + j + 1, 0:HEAD_DIM] for j in range(n_hg)]
                        + [jnp.zeros((1, D - n_hg * HEAD_DIM), F32)], axis=1),
        jnp.concatenate([small[2 * j]["dcw"][0:CONV_WIDTH] for j in range(n_rg)], axis=0),
        head[1:2],
    ]
    params = [mod_b, norm_mix, norm_mlp, norm_final, rg_conv_b, rg_b_a, rg_b_x, rg_lambda, hg_lower_bounds, hg_gnorm]
    moms = [m_mod_b, m_norm_mix, m_norm_mlp, m_norm_final, m_rg_conv_b, m_rg_b_a, m_rg_b_x, m_rg_lambda, m_hg_lower_bounds, m_hg_gnorm]
    vars_ = [v_mod_b, v_norm_mix, v_norm_mlp, v_norm_final, v_rg_conv_b, v_rg_b_a, v_rg_b_x, v_rg_lambda, v_hg_lower_bounds, v_hg_gnorm]
    offsets, rows_of, at = [], [], 0
    for g in groups:
        offsets.append(at)
        rows_of.append(g.shape[0])
        at += -(-g.shape[0] // SUBLANES) * SUBLANES
    packed = jnp.concatenate([_pad_rows(g, -(-g.shape[0] // SUBLANES) * SUBLANES) for g in groups], axis=0)
    (small_parts,) = _all_gather([packed], "gather_small_grads")

    def pack_like(arrs):
        out = []
        for g_rows, off, a in zip(rows_of, offsets, arrs):
            flat = a.reshape(-1)
            flat = jnp.concatenate([flat, jnp.zeros((g_rows * D - flat.shape[0],), F32)])
            out.append(_pad_rows(flat.reshape(g_rows, D), -(-g_rows // SUBLANES) * SUBLANES))
        rest = packed.shape[0] - sum(o.shape[0] for o in out)
        return jnp.concatenate(out + [jnp.zeros((rest, D), F32)], axis=0)

    small_out = _adamw("adamw_small", small_parts, pack_like(params), pack_like(moms), pack_like(vars_))

    def unpack(q, idx, like):
        rows = small_out[q][offsets[idx]:offsets[idx] + rows_of[idx]]
        return rows.reshape(-1)[:like.size].reshape(like.shape)

    loss = jnp.sum(small_out[0][offsets[11]])
    names =["mod_b", "norm_mix", "norm_mlp", "norm_final", "rg_conv_b", "rg_b_a", "rg_b_x", "rg_lambda", "hg_lower_bounds", "hg_gnorm"]
    for idx, (nm, like) in enumerate(zip(names, params)):
        results[nm] = [unpack(q, idx, like) for q in range(4)]

    cw_parts = lax.dynamic_slice_in_dim(small_parts[:, offsets[10]:offsets[10] + n_rg * CONV_WIDTH, :], me * (D // N_DEV), D // N_DEV, axis=2)
    shp = rg_conv_w.shape
    results["rg_conv_w"] = [o.reshape(shp) for o in _adamw(
        "adamw_conv", cw_parts, rg_conv_w.reshape(-1, shp[-1]), m_rg_conv_w.reshape(-1, shp[-1]), v_rg_conv_w.reshape(-1, shp[-1]))]
    shp = rg_w_a.shape
    stacked = (n_rg, HEADS * HEAD_DIM, HEAD_DIM)
    for nm, which, (w, m, v) in (("rg_w_a", 0, (rg_w_a, m_rg_w_a, v_rg_w_a)), ("rg_w_x", 1, (rg_w_x, m_rg_w_x, v_rg_w_x))):
        out = None
        for j in reversed(range(n_rg)):
            out = _adamw("adamw_gate", small[2 * j]["gate_parts"][which], w.reshape(stacked), m.reshape(stacked),
                         v.reshape(stacked), layer=j, prev=out)
        results[nm] = [o.reshape(shp) for o in out]

    dmod_all = small_parts[:, 0:DEPTH * 6, :].reshape(N_DEV, DEPTH, 6 * D)
    dmod_cols = lax.dynamic_slice_in_dim(dmod_all, me * sw_mod, sw_mod, axis=2).transpose(1, 0, 2)
    pad = HEAD_DIM - N_DEV
    dmod_pad = jnp.concatenate([dmod_cols, jnp.zeros((DEPTH, pad, sw_mod), F32)], axis=1).astype(MXU)
    cs_t = jnp.concatenate([cs_all.T, jnp.zeros((D, pad), F32)], axis=1).astype(MXU)
    g_mod_w = _mod_weight_grad("mod_weight_grad", cs_t, dmod_pad)
    results["mod_w"] = [o.reshape(mod_w.shape) for o in _adamw(
        "adamw_mod", g_mod_w.reshape(1, -1, sw_mod), mod_w.reshape(-1, sw_mod), m_mod_w.reshape(-1, sw_mod), v_mod_w.reshape(-1, sw_mod))]

    update_layer(0, landed[0])

    order = ["mod_w", "mod_b", "norm_mix", "norm_mlp", "norm_final", "rg_w_in", "rg_conv_w", "rg_conv_b", "rg_w_a", "rg_b_a", "rg_w_x",
             "rg_b_x", "rg_lambda", "rg_w_out", "hg_w_in", "hg_lower_bounds", "hg_gnorm", "hg_w_out", "mlp_w1", "mlp_w2"]
    return (loss, dx[None], *[results[n][0] for n in order], *[results[n][1] for n in order],
            *[results[n][2] for n in order], *[results[n][3] for n in order])
```

```python
import functools

import jax
import jax.numpy as jnp
from jax import lax
from jax.experimental import pallas as pl
from jax.experimental.pallas import tpu as pltpu
from jax.experimental.pallas import tpu_sc as plsc

F32 = jnp.float32
MXU = jnp.bfloat16

N_DEV = 8
D = 1024
DEPTH = 4
HEADS = 8
HEAD_DIM = 128
CONV_WIDTH = 4
LRU_C = 8.0
HG_CHUNK = 64
NORM_EPS = 1e-6
GNORM_EPS = 1e-5
ADAM_LR = 0.001
ADAM_B1 = 0.9
ADAM_B2 = 0.999
ADAM_EPS = 1e-08
ADAM_WD = 0.01
ADAM_STEP = 10
GELU_C = 0.7978845608028654
GELU_K = 0.044715
EXP_CLAMP = 80.0
SUBLANES = 8
VMEM_LIMIT = 48 * 1024 * 1024
ROW_TILE = 256
MM_TILE = 1024
ADAMW_STEP_BYTES = 8 * 1024 * 1024

MESH = pl.DeviceIdType.MESH
ANY = pl.BlockSpec(memory_space=pl.ANY)


def _params():
    return pltpu.CompilerParams(vmem_limit_bytes=VMEM_LIMIT)


def _tile(n, target):
    if n <= target:
        return n
    t = target // SUBLANES * SUBLANES
    while n % t:
        t -= SUBLANES
    return t


def _sigmoid(x):
    return 1.0 / (1.0 + jnp.exp(-x))


def _silu(x):
    return x * _sigmoid(x)


def _silu_grad(x):
    s = _sigmoid(x)
    return s * (1.0 + x * (1.0 - s))


def _gelu(y):
    return 0.5 * y * (1.0 + jnp.tanh(GELU_C * (y + GELU_K * y * y * y)))


def _gelu_grad(y):
    th = jnp.tanh(GELU_C * (y + GELU_K * y * y * y))
    return 0.5 * (1.0 + th) + 0.5 * y * (1.0 - th * th) * GELU_C * (1.0 + 3.0 * GELU_K * y * y)


def _dot(a, b):
    return lax.dot_general(a.astype(MXU), b.astype(MXU), (((1,), (0,)), ((), ())), preferred_element_type=F32)


def _dot_nt(a, b):
    return lax.dot_general(a.astype(MXU), b.astype(MXU), (((1,), (1,)), ((), ())), preferred_element_type=F32)


def _dot_tn(a, b):
    return lax.dot_general(a.astype(MXU), b.astype(MXU), (((0,), (0,)), ((), ())), preferred_element_type=F32)


def _dot_exact(tri, x):
    t = tri.astype(MXU)
    hi = x.astype(MXU)
    r1 = x - hi.astype(F32)
    mid = r1.astype(MXU)
    lo = (r1 - mid.astype(F32)).astype(MXU)
    dn = (((1,), (0,)), ((), ()))
    return (lax.dot_general(t, hi, dn, preferred_element_type=F32) + lax.dot_general(t, mid, dn, preferred_element_type=F32)
            + lax.dot_general(t, lo, dn, preferred_element_type=F32))


def _row_sum(v):
    return jnp.sum(v, axis=0, keepdims=True)


def _handshake(partners):
    barrier = pltpu.get_barrier_semaphore()
    for p in partners:
        pl.semaphore_signal(barrier, inc=1, device_id=p, device_id_type=MESH)
    pl.semaphore_wait(barrier, len(partners))


def _gather_body(n, per_array_sems, handshake):
    def body(*refs):
        ins, outs = refs[:n], refs[n:2 * n]
        send_sems, recv_sems, local_sems = refs[2 * n:]
        x, y, c = lax.axis_index("x"), lax.axis_index("y"), lax.axis_index("c")
        me, sibling = (x, y, c), (x, y, 1 - c)
        chips = [(1 - x, y), (x, 1 - y), (1 - x, 1 - y)]
        if handshake:
            _handshake([sibling] + [(*chip, c) for chip in chips])

        def sem(sems, a, k):
            return sems.at[a, k] if per_array_sems else sems.at[k]

        def slot(a, p):
            return outs[a].at[4 * p[0] + 2 * p[1] + p[2]]

        def copy(a, k, block, to, src=None):
            return pltpu.make_async_remote_copy(
                src_ref=slot(a, block) if src is None else src, dst_ref=slot(a, block),
                send_sem=sem(send_sems, a, k), recv_sem=sem(recv_sems, a, k), device_id=to, device_id_type=MESH)

        mine = [pltpu.make_async_copy(ins[a], slot(a, me), local_sems.at[a if per_array_sems else 0]) for a in range(n)]
        for cp in mine:
            cp.start()
        first = []
        for a in range(n):
            first.append(copy(a, 0, me, sibling, src=ins[a]))
            first += [copy(a, 1 + j, me, (*chip, c), src=ins[a]) for j, chip in enumerate(chips)]
        for cp in first:
            cp.start()
        passed = []
        for j, chip in enumerate(chips):
            for a in range(n):
                copy(a, 1 + j, (*chip, c), me).wait_recv()
            for a in range(n):
                cp = copy(a, 4 + j, (*chip, c), sibling)
                cp.start()
                passed.append(cp)
        for a in range(n):
            copy(a, 0, sibling, me).wait_recv()
        for j, chip in enumerate(chips):
            for a in range(n):
                copy(a, 4 + j, (*chip, 1 - c), me).wait_recv()
        for cp in first + passed:
            cp.wait_send()
        for cp in mine:
            cp.wait()

    return body


def _sibling_send_body(n, per_array_sems, handshake):
    def body(*refs):
        ins, outs = refs[:n], refs[n:2 * n]
        send_sems, recv_sems, _ = refs[2 * n:]
        x, y, c = lax.axis_index("x"), lax.axis_index("y"), lax.axis_index("c")
        sibling = (x, y, 1 - c)
        if handshake:
            _handshake([sibling])

        def sem(sems, a):
            return sems.at[a, 0] if per_array_sems else sems.at[0]

        copies = [pltpu.make_async_remote_copy(
            src_ref=ins[a].at[2 * q + 1 - c], dst_ref=outs[a].at[q], send_sem=sem(send_sems, a), recv_sem=sem(recv_sems, a),
            device_id=sibling, device_id_type=MESH) for a in range(n) for q in range(4)]
        for cp in copies:
            cp.start()
        for cp in copies:
            cp.wait_recv()
        for cp in copies:
            cp.wait_send()

    return body


def _chip_exchange_body(n, per_array_sems, handshake):
    def body(*refs):
        ins, outs = refs[:n], refs[n:2 * n]
        send_sems, recv_sems, local_sems = refs[2 * n:]
        x, y, c = lax.axis_index("x"), lax.axis_index("y"), lax.axis_index("c")
        my_chip = 2 * x + y
        chips = [(1 - x, y), (x, 1 - y), (1 - x, 1 - y)]
        if handshake:
            _handshake([(*chip, c) for chip in chips])

        def sem(sems, a, k):
            return sems.at[a, k] if per_array_sems else sems.at[k]

        def copy(a, k, landing):
            px, py = chips[k]
            return pltpu.make_async_remote_copy(
                src_ref=ins[a].at[2 * px + py], dst_ref=outs[a].at[landing], send_sem=sem(send_sems, a, k),
                recv_sem=sem(recv_sems, a, k), device_id=(px, py, c), device_id_type=MESH)

        mine = [pltpu.make_async_copy(ins[a].at[my_chip], outs[a].at[my_chip], local_sems.at[a if per_array_sems else 0])
                for a in range(n)]
        for cp in mine:
            cp.start()
        sent = [copy(a, k, my_chip) for a in range(n) for k in range(3)]
        for cp in sent:
            cp.start()
        for a in range(n):
            for k, (px, py) in enumerate(chips):
                copy(a, k, 2 * px + py).wait_recv()
        for cp in sent:
            cp.wait_send()
        for cp in mine:
            cp.wait()

    return body


def _all_gather(arrs, name):
    n = len(arrs)
    return pl.pallas_call(
        _gather_body(n, True, False), name=name, in_specs=[ANY] * n, out_specs=[ANY] * n,
        out_shape=[jax.ShapeDtypeStruct((N_DEV,) + a.shape, a.dtype) for a in arrs],
        scratch_shapes=[pltpu.SemaphoreType.DMA((n, 7)), pltpu.SemaphoreType.DMA((n, 7)), pltpu.SemaphoreType.DMA((n,))],
    )(*arrs)


def _on_sequencer(body, arrs, out_type, name, collective_id):
    return pl.kernel(
        body, name=name, out_type=out_type, mesh=plsc.ScalarSubcoreMesh(axis_name="sequencer", num_cores=1),
        scratch_types=[pltpu.SemaphoreType.DMA((7,)), pltpu.SemaphoreType.DMA((7,)), pltpu.SemaphoreType.DMA((1,))],
        compiler_params=pltpu.CompilerParams(collective_id=collective_id))(*arrs)


def _all_gather_async(arrs, name, collective_id):
    out_type = [jax.ShapeDtypeStruct((N_DEV,) + a.shape, a.dtype) for a in arrs]
    return _on_sequencer(_gather_body(len(arrs), False, True), arrs, out_type, name, collective_id)


def _sibling_send_async(arrs, name, collective_id):
    out_type = [jax.ShapeDtypeStruct((N_DEV // 2,) + a.shape[1:], a.dtype) for a in arrs]
    return _on_sequencer(_sibling_send_body(len(arrs), False, True), arrs, out_type, name, collective_id)


def _chip_exchange_async(arrs, name, collective_id):
    out_type = [jax.ShapeDtypeStruct(a.shape, a.dtype) for a in arrs]
    return _on_sequencer(_chip_exchange_body(len(arrs), False, True), arrs, out_type, name, collective_id)


def _pair_sum(name, parity, mine, theirs):
    n = len(mine)

    def body(par_ref, *refs):
        for a in range(n):
            refs[2 * n + a][...] = (refs[a][...].astype(F32) + refs[n + a][...].astype(F32)).astype(refs[2 * n + a].dtype)

    def block(a):
        return (None,) + a.shape[1:]

    grid_spec = pltpu.PrefetchScalarGridSpec(
        num_scalar_prefetch=1, grid=(N_DEV // 2,),
        in_specs=[pl.BlockSpec(block(a), lambda q, par: (2 * q + par[0], 0, 0)) for a in mine]
        + [pl.BlockSpec(block(a), lambda q, par: (q, 0, 0)) for a in theirs],
        out_specs=[pl.BlockSpec(block(a), lambda q, par: (q, 0, 0)) for a in theirs])
    return pl.pallas_call(body, name=name, grid_spec=grid_spec, out_shape=[jax.ShapeDtypeStruct(a.shape, a.dtype) for a in theirs],
                          compiler_params=_params())(parity, *mine, *theirs)


NN = (((1,), (0,)), ((), ()))
NT = (((1,), (1,)), ((), ()))
TN = (((0,), (0,)), ((), ()))


def _matmul(name, a, b, dims, grid, a_spec, b_spec, outs, epilogue, extras=(), acc_shape=None):
    n_in = 2 + len(extras)
    n_out = len(outs)
    nk = grid[2]

    def body(*refs):
        a_ref, b_ref = refs[0], refs[1]
        ex, out_refs = refs[2:n_in], refs[n_in:n_in + n_out]

        def part():
            return lax.dot_general(a_ref[...], b_ref[...], dims, preferred_element_type=F32)

        if nk == 1:
            epilogue(part(), ex, out_refs)
            return
        acc = refs[n_in + n_out]
        k = pl.program_id(2)

        @pl.when(k == 0)
        def _():
            acc[...] = part()

        @pl.when(k > 0)
        def _():
            acc[...] += part()

        @pl.when(k == nk - 1)
        def _():
            epilogue(acc[...], ex, out_refs)

    return pl.pallas_call(
        body, name=name, grid=grid, in_specs=[a_spec, b_spec] + [s for _, s in extras], out_specs=[s for _, _, s in outs],
        out_shape=[jax.ShapeDtypeStruct(sh, dt) for sh, dt, _ in outs],
        scratch_shapes=[] if nk == 1 else [pltpu.VMEM(acc_shape, F32)], compiler_params=_params(),
    )(a, b, *[e for e, _ in extras])


def _ep_store(acc, ex, outs):
    outs[0][...] = acc.astype(outs[0].dtype)


def _ep_residual(acc, ex, outs):
    outs[0][...] = acc
    outs[1][...] = ex[0][...] + ex[1][0:1, :] * acc


def _ep_relu2(acc, ex, outs):
    outs[0][...] = acc.astype(outs[0].dtype)
    r = jnp.maximum(acc, 0.0)
    outs[1][...] = (r * r).astype(outs[1].dtype)


def _ep_relu2_grad(acc, ex, outs):
    outs[0][...] = (acc * (2.0 * jnp.maximum(ex[0][...].astype(F32), 0.0))).astype(outs[0].dtype)


def _mm_cols(name, a, wg, epilogue=_ep_store, out_dtypes=(F32,)):
    m, k = a.shape
    sw = wg.shape[2]
    tm = _tile(m, MM_TILE)
    spec = pl.BlockSpec((tm, sw), lambda i, j, kk: (i, j))
    return _matmul(name, a, wg, NN, (m // tm, N_DEV, 1), pl.BlockSpec((tm, k), lambda i, j, kk: (i, 0)),
                   pl.BlockSpec((None, k, sw), lambda i, j, kk: (j, 0, 0)),
                   [((m, N_DEV * sw), dt, spec) for dt in out_dtypes], epilogue)


def _mm_cols_t(name, a, wg, out_dtype=F32):
    m = a.shape[0]
    _, r, sw = wg.shape
    tm, tn = _tile(m, MM_TILE), _tile(r, MM_TILE)
    return _matmul(name, a, wg, NT, (m // tm, r // tn, N_DEV), pl.BlockSpec((tm, sw), lambda i, j, kk: (i, kk)),
                   pl.BlockSpec((None, tn, sw), lambda i, j, kk: (kk, j, 0)),
                   [((m, r), out_dtype, pl.BlockSpec((tm, tn), lambda i, j, kk: (i, j)))], _ep_store, acc_shape=(tm, tn))[0]


def _mm_rows(name, a, w, epilogue, outs_dtypes, extras=()):
    m, k = a.shape
    n = w.shape[1]
    tm, tn, tk = _tile(m, MM_TILE), _tile(n, MM_TILE // 2), _tile(k, MM_TILE)
    spec = pl.BlockSpec((tm, tn), lambda i, j, kk: (i, j))
    return _matmul(name, a, w, NN, (m // tm, n // tn, k // tk), pl.BlockSpec((tm, tk), lambda i, j, kk: (i, kk)),
                   pl.BlockSpec((tk, tn), lambda i, j, kk: (kk, j)), [((m, n), dt, spec) for dt in outs_dtypes], epilogue,
                   extras=[(e, spec if e.shape[0] == m else pl.BlockSpec((SUBLANES, tn), lambda i, j, kk: (0, j))) for e in extras],
                   acc_shape=(tm, tn))


def _mm_rows_t(name, a, w, epilogue=_ep_store, out_dtype=F32, extras=()):
    m, c = a.shape
    r = w.shape[0]
    tm, tn, tk = _tile(m, MM_TILE), _tile(r, MM_TILE // 2), _tile(c, MM_TILE)
    spec = pl.BlockSpec((tm, tn), lambda i, j, kk: (i, j))
    return _matmul(name, a, w, NT, (m // tm, r // tn, c // tk), pl.BlockSpec((tm, tk), lambda i, j, kk: (i, kk)),
                   pl.BlockSpec((tn, tk), lambda i, j, kk: (j, kk)), [((m, r), out_dtype, spec)], epilogue,
                   extras=[(e, spec) for e in extras], acc_shape=(tm, tn))[0]


def _mm_grad(name, a, b, shard_cols):
    t, m = a.shape
    n = b.shape[1]
    tk, tm = _tile(t, MM_TILE), _tile(m, MM_TILE)
    if shard_cols:
        tn = n // N_DEV
        out = ((N_DEV, m, tn), MXU, pl.BlockSpec((None, tm, tn), lambda i, j, kk: (j, i, 0)))
    else:
        tn = _tile(n, MM_TILE)
        out = ((m, n), MXU, pl.BlockSpec((tm, tn), lambda i, j, kk: (i, j)))
    return _matmul(name, a, b, TN, (m // tm, n // tn, t // tk), pl.BlockSpec((tk, tm), lambda i, j, kk: (kk, i)),
                   pl.BlockSpec((tk, tn), lambda i, j, kk: (kk, j)), [out], _ep_store, acc_shape=(tm, tn))[0]


def _rows_spec(tt, width=D):
    return pl.BlockSpec((tt, width), lambda i: (i, 0))


def _vec_spec(rows=SUBLANES, width=D):
    return pl.BlockSpec((rows, width), lambda i: (0, 0))


def _norm_mod(name, x, vec):
    t = x.shape[0]
    tt = _tile(t, ROW_TILE)

    def body(x_ref, v_ref, h_ref):
        xv = x_ref[...]
        r = lax.rsqrt(jnp.mean(xv * xv, axis=-1, keepdims=True) + NORM_EPS)
        h = (xv * r) * v_ref[0:1, :]
        h_ref[...] = (h * (1.0 + v_ref[1:2, :]) + v_ref[2:3, :]).astype(h_ref.dtype)

    return pl.pallas_call(body, name=name, grid=(t // tt,), in_specs=[_rows_spec(tt), _vec_spec()], out_specs=_rows_spec(tt),
                          out_shape=jax.ShapeDtypeStruct((t, D), MXU))(x, vec)


def _norm_mod_grad(name, dh, x, vec, dres):
    t = x.shape[0]
    tt = _tile(t, ROW_TILE)
    nt = t // tt

    def body(dh_ref, x_ref, v_ref, dres_ref, dx_ref, acc_ref):
        i = pl.program_id(0)

        @pl.when(i == 0)
        def _():
            acc_ref[...] = jnp.zeros_like(acc_ref)

        xv, dhv = x_ref[...], dh_ref[...]
        r = lax.rsqrt(jnp.mean(xv * xv, axis=-1, keepdims=True) + NORM_EPS)
        xn = xv * r
        w = v_ref[0:1, :] * (1.0 + v_ref[1:2, :])
        acc_ref[0:1, :] += _row_sum(dhv * xn)
        acc_ref[2:3, :] += _row_sum(dhv)
        dxn = dhv * w
        dx_ref[...] = dres_ref[...] + r * (dxn - xn * jnp.mean(dxn * xn, axis=-1, keepdims=True))

        @pl.when(i == nt - 1)
        def _():
            dw = acc_ref[0:1, :]
            acc_ref[1:2, :] = dw * v_ref[0:1, :]
            acc_ref[0:1, :] = dw * (1.0 + v_ref[1:2, :])

    return pl.pallas_call(
        body, name=name, grid=(nt,), in_specs=[_rows_spec(tt), _rows_spec(tt), _vec_spec(), _rows_spec(tt)],
        out_specs=[_rows_spec(tt), _vec_spec()],
        out_shape=[jax.ShapeDtypeStruct((t, D), F32), jax.ShapeDtypeStruct((SUBLANES, D), F32)])(dh, x, vec, dres)


def _residual_grad(name, dxo, y, vec):
    t = y.shape[0]
    tt = _tile(t, ROW_TILE)

    def body(dx_ref, y_ref, v_ref, dy_ref, acc_ref):
        @pl.when(pl.program_id(0) == 0)
        def _():
            acc_ref[...] = jnp.zeros_like(acc_ref)

        dxv = dx_ref[...]
        dy_ref[...] = (dxv * v_ref[0:1, :]).astype(dy_ref.dtype)
        acc_ref[0:1, :] += _row_sum(dxv * y_ref[...])

    return pl.pallas_call(
        body, name=name, grid=(t // tt,), in_specs=[_rows_spec(tt), _rows_spec(tt), _vec_spec()],
        out_specs=[_rows_spec(tt), _vec_spec()],
        out_shape=[jax.ShapeDtypeStruct((t, D), MXU), jax.ShapeDtypeStruct((SUBLANES, D), F32)])(dxo, y, vec)


def _loss_head(name, x, target, vec):
    t = x.shape[0]
    tt = _tile(t, ROW_TILE)

    def body(x_ref, t_ref, v_ref, dx_ref, acc_ref):
        @pl.when(pl.program_id(0) == 0)
        def _():
            acc_ref[...] = jnp.zeros_like(acc_ref)

        xv = x_ref[...]
        r = lax.rsqrt(jnp.mean(xv * xv, axis=-1, keepdims=True) + NORM_EPS)
        xn = xv * r
        gain = v_ref[0:1, :]
        err = xn * gain - t_ref[...]
        acc_ref[1:2, :] += _row_sum(err * err) * (0.5 / D)
        dout = err * (1.0 / D)
        acc_ref[0:1, :] += _row_sum(dout * xn)
        dxn = dout * gain
        dx_ref[...] = r * (dxn - xn * jnp.mean(dxn * xn, axis=-1, keepdims=True))

    return pl.pallas_call(
        body, name=name, grid=(t // tt,), in_specs=[_rows_spec(tt), _rows_spec(tt), _vec_spec()],
        out_specs=[_rows_spec(tt), _vec_spec()],
        out_shape=[jax.ShapeDtypeStruct((t, D), F32), jax.ShapeDtypeStruct((SUBLANES, D), F32)])(x, target, vec)


def _shift_down(x, halo, k):
    y = pltpu.roll(x, k, 0)
    top = jnp.where(lax.broadcasted_iota(jnp.int32, halo.shape, 0) < k, pltpu.roll(halo, k, 0), y[0:SUBLANES, :])
    return jnp.concatenate([top, y[SUBLANES:, :]], axis=0)


def _shift_up(x, halo, k):
    n = x.shape[0]
    y = pltpu.roll(x, n - k, 0)
    bottom = jnp.where(lax.broadcasted_iota(jnp.int32, halo.shape, 0) >= SUBLANES - k, pltpu.roll(halo, SUBLANES - k, 0),
                       y[n - SUBLANES:, :])
    return jnp.concatenate([y[:n - SUBLANES, :], bottom], axis=0)


def _rg_gates(xb, halo, cw_ref, vec_ref, wa_ref, wx_ref, at_start):
    shifted = [xb] + [_shift_down(xb, halo, k) for k in range(1, CONV_WIDTH)]
    xc = vec_ref[0:1, :] + shifted[0] * cw_ref[CONV_WIDTH - 1:CONV_WIDTH, :]
    for k in range(1, CONV_WIDTH):
        xc = xc + shifted[k] * cw_ref[CONV_WIDTH - 1 - k:CONV_WIDTH - k, :]
    heads = [slice(h * HEAD_DIM, (h + 1) * HEAD_DIM) for h in range(HEADS)]
    pa = jnp.concatenate([_dot(xc[:, s], wa_ref[h]) for h, s in enumerate(heads)], axis=1) + vec_ref[1:2, :]
    px = jnp.concatenate([_dot(xc[:, s], wx_ref[h]) for h, s in enumerate(heads)], axis=1) + vec_ref[2:3, :]
    ra, ia = _sigmoid(pa), _sigmoid(px)
    nl = -vec_ref[3:4, :]
    sp = jnp.maximum(nl, 0.0) + jnp.log(1.0 + jnp.exp(-jnp.abs(nl)))
    log_a = (-LRU_C) * ra * sp
    a = jnp.exp(log_a)
    th = jnp.tanh(log_a)
    is_t0 = jnp.logical_and(lax.broadcasted_iota(jnp.int32, xb.shape, 0) == 0, at_start)
    mult = jnp.where(is_t0, 1.0, jnp.sqrt(-2.0 * th / (1.0 - th)))
    return dict(shifted=shifted, xc=xc, ra=ra, ia=ia, sp=sp, a=a, mult=mult, is_t0=is_t0, heads=heads)


def _rg_specs(tt, nt, order):
    blk = tt // SUBLANES
    return dict(
        x=pl.BlockSpec((tt, D), lambda i: (order(i), 0)), y=pl.BlockSpec((tt, D), lambda i: (order(i), 1)),
        halo=pl.BlockSpec((SUBLANES, D), lambda i: (jnp.maximum(order(i) * blk - 1, 0), 0)),
        cw=_vec_spec(CONV_WIDTH), vec=_vec_spec(), w=pl.BlockSpec((HEADS, HEAD_DIM, HEAD_DIM), lambda i: (0, 0, 0)))


def _rg_forward(name, z, cw, vec, wa, wx):
    t = z.shape[0]
    tt = _tile(t, ROW_TILE)
    nt = t // tt
    sp = _rg_specs(tt, nt, lambda i: i)

    def body(zx_ref, zy_ref, halo_ref, cw_ref, vec_ref, wa_ref, wx_ref, p_ref, h_ref, a_s, u_s, carry):
        i = pl.program_id(0)

        @pl.when(i == 0)
        def _():
            carry[...] = jnp.zeros_like(carry)

        halo = jnp.where(i > 0, halo_ref[...], 0.0)
        g = _rg_gates(zx_ref[...], halo, cw_ref, vec_ref, wa_ref, wx_ref, i == 0)
        a_s[...] = g["a"]
        u_s[...] = g["mult"] * (g["ia"] * g["xc"])

        def group(gi, h):
            rows = pl.ds(pl.multiple_of(gi * SUBLANES, SUBLANES), SUBLANES)
            a8, u8 = a_s[rows, :], u_s[rows, :]
            out = []
            for j in range(SUBLANES):
                h = a8[j:j + 1, :] * h + u8[j:j + 1, :]
                out.append(h)
            h_ref[rows, :] = jnp.concatenate(out, axis=0)
            return h

        carry[0:1, :] = lax.fori_loop(0, tt // SUBLANES, group, carry[0:1, :])
        p_ref[...] = (h_ref[...] * _gelu(zy_ref[...])).astype(p_ref.dtype)

    return pl.pallas_call(
        body, name=name, grid=(nt,), in_specs=[sp["x"], sp["y"], sp["halo"], sp["cw"], sp["vec"], sp["w"], sp["w"]],
        out_specs=[_rows_spec(tt), _rows_spec(tt)],
        out_shape=[jax.ShapeDtypeStruct((t, D), MXU), jax.ShapeDtypeStruct((t, D), F32)],
        scratch_shapes=[pltpu.VMEM((tt, D), F32), pltpu.VMEM((tt, D), F32), pltpu.VMEM((SUBLANES, D), F32)],
        compiler_params=_params())(z, z, z, cw, vec, wa, wx)


def _rg_backward(name, dp, z, h, cw, vec, wa, wx):
    t = z.shape[0]
    tt = _tile(t, ROW_TILE)
    nt = t // tt
    rev = lambda i: nt - 1 - i
    sp = _rg_specs(tt, nt, rev)
    rows_rev = pl.BlockSpec((tt, D), lambda i: (rev(i), 0))

    def body(dp_ref, zx_ref, zy_ref, halo_ref, h_ref, hhalo_ref, cw_ref, vec_ref, wa_ref, wx_ref,
             dz_ref, dvec_ref, dcw_ref, dwa_ref, dwx_ref, a_s, d_s, carry, nxt):
        i = pl.program_id(0)
        j = rev(i)

        @pl.when(i == 0)
        def _():
            carry[...] = jnp.zeros_like(carry)
            nxt[...] = jnp.zeros_like(nxt)
            dvec_ref[...] = jnp.zeros_like(dvec_ref)
            dcw_ref[...] = jnp.zeros_like(dcw_ref)
            dwa_ref[...] = jnp.zeros_like(dwa_ref)
            dwx_ref[...] = jnp.zeros_like(dwx_ref)

        halo = jnp.where(j > 0, halo_ref[...], 0.0)
        g = _rg_gates(zx_ref[...], halo, cw_ref, vec_ref, wa_ref, wx_ref, j == 0)
        xc, ra, ia, a, mult = g["xc"], g["ra"], g["ia"], g["a"], g["mult"]
        hv, zy, dpv = h_ref[...], zy_ref[...], dp_ref[...]
        dyb = dpv * hv * _gelu_grad(zy)
        a_s[...] = a
        d_s[...] = dpv * _gelu(zy)

        def group(gi, c):
            rows = pl.ds(pl.multiple_of((tt // SUBLANES - 1 - gi) * SUBLANES, SUBLANES), SUBLANES)
            a8, d8 = a_s[rows, :], d_s[rows, :]
            out = [None] * SUBLANES
            for r in reversed(range(SUBLANES)):
                dht = d8[r:r + 1, :] + c
                out[r] = dht
                c = a8[r:r + 1, :] * dht
            d_s[rows, :] = jnp.concatenate(out, axis=0)
            return c

        carry[0:1, :] = lax.fori_loop(0, tt // SUBLANES, group, carry[0:1, :])
        dht = d_s[...]
        hprev = _shift_down(hv, jnp.where(j > 0, hhalo_ref[...], 0.0), 1)
        ixc = ia * xc
        dlog_a = dht * hprev * a + jnp.where(g["is_t0"], 0.0, dht * ixc * (-(a * a) / mult))
        dia = dht * mult * xc
        dxc = dht * mult * ia
        dra = dlog_a * ((-LRU_C) * g["sp"])
        nl = -vec_ref[3:4, :]
        dvec_ref[3:4, :] += _row_sum(dlog_a * ((-LRU_C) * ra)) * (-_sigmoid(nl))
        dpa = dra * ra * (1.0 - ra)
        dpx = dia * ia * (1.0 - ia)
        dvec_ref[1:2, :] += _row_sum(dpa)
        dvec_ref[2:3, :] += _row_sum(dpx)
        back = []
        for hd, s in enumerate(g["heads"]):
            dwa_ref[hd] += _dot_tn(xc[:, s], dpa[:, s])
            dwx_ref[hd] += _dot_tn(xc[:, s], dpx[:, s])
            back.append(_dot_nt(dpa[:, s], wa_ref[hd]) + _dot_nt(dpx[:, s], wx_ref[hd]))
        dxc = dxc + jnp.concatenate(back, axis=1)
        dvec_ref[0:1, :] += _row_sum(dxc)
        dxb = dxc * cw_ref[CONV_WIDTH - 1:CONV_WIDTH, :]
        for k in range(CONV_WIDTH):
            row = CONV_WIDTH - 1 - k
            dcw_ref[row:row + 1, :] += _row_sum(dxc * g["shifted"][k])
            if k:
                dxb = dxb + _shift_up(dxc, nxt[...], k) * cw_ref[row:row + 1, :]
        nxt[...] = dxc[0:SUBLANES, :]
        dz_ref[:, 0:D] = dxb.astype(dz_ref.dtype)
        dz_ref[:, D:2 * D] = dyb.astype(dz_ref.dtype)

    hhalo = pl.BlockSpec((SUBLANES, D), lambda i: (jnp.maximum(rev(i) * (tt // SUBLANES) - 1, 0), 0))
    wacc = pl.BlockSpec((HEADS, HEAD_DIM, HEAD_DIM), lambda i: (0, 0, 0))
    return pl.pallas_call(
        body, name=name, grid=(nt,),
        in_specs=[rows_rev, sp["x"], sp["y"], sp["halo"], rows_rev, hhalo, sp["cw"], sp["vec"], sp["w"], sp["w"]],
        out_specs=[pl.BlockSpec((tt, 2 * D), lambda i: (rev(i), 0)), _vec_spec(), _vec_spec(), wacc, wacc],
        out_shape=[jax.ShapeDtypeStruct((t, 2 * D), MXU), jax.ShapeDtypeStruct((SUBLANES, D), F32),
                   jax.ShapeDtypeStruct((SUBLANES, D), F32), jax.ShapeDtypeStruct((HEADS, HEAD_DIM, HEAD_DIM), F32),
                   jax.ShapeDtypeStruct((HEADS, HEAD_DIM, HEAD_DIM), F32)],
        scratch_shapes=[pltpu.VMEM((tt, D), F32), pltpu.VMEM((tt, D), F32), pltpu.VMEM((SUBLANES, D), F32),
                        pltpu.VMEM((SUBLANES, D), F32)],
        compiler_params=_params())(dp, z, z, z, h, h, cw, vec, wa, wx)


def _hg_chunk(zq, zf, zi, lb):
    c = HG_CHUNK
    q = _silu(zq)
    sg = _sigmoid(zf)
    fg = lb + (1.0 - lb) * sg
    k = 1.0 - fg
    row, col = lax.broadcasted_iota(jnp.int32, (c, c), 0), lax.broadcasted_iota(jnp.int32, (c, c), 1)
    tri, tri_t = (row >= col).astype(F32), (row <= col).astype(F32)
    b = _dot_exact(tri, jnp.log(fg))
    mid, last = b[c // 2 - 1:c // 2, :], b[c - 1:c, :]
    eq = jnp.exp(jnp.minimum(b - mid, EXP_CLAMP))
    ek = jnp.exp(jnp.minimum(mid - b, EXP_CLAMP))
    eb = jnp.exp(b)
    ed = jnp.exp(last - b)
    return dict(q=q, sg=sg, fg=fg, k=k, v=zi, eq=eq, ek=ek, eb=eb, ed=ed, elast=jnp.exp(last), tri=tri, tri_t=tri_t,
                qe=q * eq, ke=k * ek, qb=q * eb, kd=k * ed)


def _hg_specs(tt, order):
    return [pl.BlockSpec((tt, D), lambda i, col=col: (order(i), col)) for col in range(4)]


def _hg_forward(name, z, vec):
    t = z.shape[0]
    tt = _tile(t, ROW_TILE)
    nt, nc = t // tt, tt // HG_CHUNK
    heads = [slice(h * HEAD_DIM, (h + 1) * HEAD_DIM) for h in range(HEADS)]

    def body(zq_ref, zf_ref, zi_ref, zg_ref, vec_ref, p_ref, o_ref, ss_ref, state):
        @pl.when(pl.program_id(0) == 0)
        def _():
            state[...] = jnp.zeros_like(state)

        lb, gain = vec_ref[0:1, :], vec_ref[1:2, :]

        def chunk(ci, carry):
            rows = pl.ds(pl.multiple_of(ci * HG_CHUNK, HG_CHUNK), HG_CHUNK)
            g = _hg_chunk(zq_ref[rows, :], zf_ref[rows, :], zi_ref[rows, :], lb)
            zg = zg_ref[rows, :]
            causal = g["tri"] > 0.0
            o_parts, p_parts = [], []
            for hd, s in enumerate(heads):
                st = state[hd]
                ss_ref[ci, hd] = st
                att = jnp.where(causal, _dot_nt(g["qe"][:, s], g["ke"][:, s]), 0.0)
                o = _dot(att, g["v"][:, s]) + _dot_nt(g["qb"][:, s], st)
                state[hd] = st * g["elast"][:, s] + _dot_tn(g["v"][:, s], g["kd"][:, s])
                r = lax.rsqrt(jnp.mean(o * o, axis=-1, keepdims=True) + GNORM_EPS)
                o_parts.append(o)
                p_parts.append((o * r) * gain[:, s])
            o_ref[rows, :] = jnp.concatenate(o_parts, axis=1)
            p_ref[rows, :] = (jnp.concatenate(p_parts, axis=1) * _silu(zg)).astype(p_ref.dtype)
            return carry

        lax.fori_loop(0, nc, chunk, 0)

    return pl.pallas_call(
        body, name=name, grid=(nt,), in_specs=_hg_specs(tt, lambda i: i) + [_vec_spec()],
        out_specs=[_rows_spec(tt), _rows_spec(tt), pl.BlockSpec((nc, HEADS, HEAD_DIM, HEAD_DIM), lambda i: (i, 0, 0, 0))],
        out_shape=[jax.ShapeDtypeStruct((t, D), MXU), jax.ShapeDtypeStruct((t, D), F32),
                   jax.ShapeDtypeStruct((t // HG_CHUNK, HEADS, HEAD_DIM, HEAD_DIM), F32)],
        scratch_shapes=[pltpu.VMEM((HEADS, HEAD_DIM, HEAD_DIM), F32)], compiler_params=_params())(z, z, z, z, vec)


def _hg_backward(name, dp, z, o, ss, vec):
    t = z.shape[0]
    tt = _tile(t, ROW_TILE)
    nt, nc = t // tt, tt // HG_CHUNK
    rev = lambda i: nt - 1 - i
    heads = [slice(h * HEAD_DIM, (h + 1) * HEAD_DIM) for h in range(HEADS)]
    rows_rev = pl.BlockSpec((tt, D), lambda i: (rev(i), 0))

    def body(dp_ref, zq_ref, zf_ref, zi_ref, zg_ref, o_ref, ss_ref, vec_ref, dz_ref, acc_ref,
             dstate, dqe_s, dke_s, dqb_s, dkd_s, dv_s, dzg_s, dlast_s):
        @pl.when(pl.program_id(0) == 0)
        def _():
            dstate[...] = jnp.zeros_like(dstate)
            acc_ref[...] = jnp.zeros_like(acc_ref)

        lb, gain = vec_ref[0:1, :], vec_ref[1:2, :]

        def chunk(cr, carry):
            ci = nc - 1 - cr
            rows = pl.ds(pl.multiple_of(ci * HG_CHUNK, HG_CHUNK), HG_CHUNK)
            zq, zg = zq_ref[rows, :], zg_ref[rows, :]
            g = _hg_chunk(zq, zf_ref[rows, :], zi_ref[rows, :], lb)
            ov, dpv = o_ref[rows, :], dp_ref[rows, :]
            causal = g["tri"] > 0.0
            don = dpv * _silu(zg)
            dgate = dpv * _silu_grad(zg)
            for hd, s in enumerate(heads):
                oh = ov[:, s]
                r = lax.rsqrt(jnp.mean(oh * oh, axis=-1, keepdims=True) + GNORM_EPS)
                on = oh * r
                dzg_s[:, s] = dgate[:, s] * (on * gain[:, s])
                acc_ref[1:2, s] += _row_sum(don[:, s] * on)
                dtmp = don[:, s] * gain[:, s]
                do = r * (dtmp - on * jnp.mean(dtmp * on, axis=-1, keepdims=True))
                st, dst = ss_ref[ci, hd], dstate[hd]
                qe, ke, qb, kd, v = g["qe"][:, s], g["ke"][:, s], g["qb"][:, s], g["kd"][:, s], g["v"][:, s]
                att = jnp.where(causal, _dot_nt(qe, ke), 0.0)
                datt = jnp.where(causal, _dot_nt(do, v), 0.0)
                dv_s[:, s] = _dot_tn(att, do) + _dot_nt(kd, dst)
                dqe_s[:, s] = _dot(datt, ke)
                dke_s[:, s] = _dot_tn(datt, qe)
                dqb_s[:, s] = _dot(do, st)
                dkd_s[:, s] = _dot(v, dst)
                dlast_s[0:1, s] = g["elast"][:, s] * _row_sum(dst * st)
                dstate[hd] = dst * g["elast"][:, s] + _dot_tn(do, qb)
            dqe, dke, dqb, dkd = dqe_s[...], dke_s[...], dqb_s[...], dkd_s[...]
            dq = dqe * g["eq"] + dqb * g["eb"]
            dk = dke * g["ek"] + dkd * g["ed"]
            dkdk = dkd * g["kd"]
            db = dqe * g["qe"].astype(MXU).astype(F32) - dke * g["ke"].astype(MXU).astype(F32) + dqb * g["qb"] - dkdk
            dlogf = _dot_exact(g["tri_t"], db) + (dlast_s[0:1, :] + _row_sum(dkdk))
            dfg = dlogf / g["fg"] - dk
            sg = g["sg"]
            acc_ref[0:1, :] += _row_sum(dfg * (1.0 - sg))
            dz_ref[rows, 0:D] = (dq * _silu_grad(zq)).astype(dz_ref.dtype)
            dz_ref[rows, D:2 * D] = (dfg * (1.0 - lb) * sg * (1.0 - sg)).astype(dz_ref.dtype)
            dz_ref[rows, 2 * D:3 * D] = dv_s[...].astype(dz_ref.dtype)
            dz_ref[rows, 3 * D:4 * D] = dzg_s[...].astype(dz_ref.dtype)
            return carry

        lax.fori_loop(0, nc, chunk, 0)

    chunk_buf = pltpu.VMEM((HG_CHUNK, D), F32)
    return pl.pallas_call(
        body, name=name, grid=(nt,),
        in_specs=[rows_rev] + _hg_specs(tt, rev) + [rows_rev, pl.BlockSpec((nc, HEADS, HEAD_DIM, HEAD_DIM), lambda i: (rev(i), 0, 0, 0)),
                                                  _vec_spec()],
        out_specs=[pl.BlockSpec((tt, 4 * D), lambda i: (rev(i), 0)), _vec_spec()],
        out_shape=[jax.ShapeDtypeStruct((t, 4 * D), MXU), jax.ShapeDtypeStruct((SUBLANES, D), F32)],
        scratch_shapes=[pltpu.VMEM((HEADS, HEAD_DIM, HEAD_DIM), F32)] + [chunk_buf] * 6 + [pltpu.VMEM((SUBLANES, D), F32)],
        compiler_params=_params())(dp, z, z, z, z, o, ss, vec)


def _mod_forward(name, c_all, mod_w, lower):
    depth, _, sw = mod_w.shape

    def body(c_ref, w_ref, lo_ref, cs_ref, mod_ref, lb_ref):
        cs = _silu(c_ref[...])
        mod_ref[...] = _dot(cs, w_ref[...])

        @pl.when(pl.program_id(0) == 0)
        def _():
            cs_ref[...] = cs
            lo = lo_ref[...]
            e = jnp.exp(lo - jnp.max(lo, axis=0, keepdims=True))
            sm = e / jnp.sum(e, axis=0, keepdims=True)
            lb_ref[0:1, :] = jnp.zeros((1, D), F32)
            for l in range(1, depth):
                lb_ref[l:l + 1, :] = lb_ref[l - 1:l, :] + sm[l:l + 1, :]

    return pl.pallas_call(
        body, name=name, grid=(depth,),
        in_specs=[pl.BlockSpec((N_DEV, D), lambda l: (0, 0)), pl.BlockSpec((None, D, sw), lambda l: (l, 0, 0)),
                  pl.BlockSpec((depth, D), lambda l: (0, 0))],
        out_specs=[pl.BlockSpec((N_DEV, D), lambda l: (0, 0)), pl.BlockSpec((None, N_DEV, sw), lambda l: (l, 0, 0)),
                   pl.BlockSpec((depth, D), lambda l: (0, 0))],
        out_shape=[jax.ShapeDtypeStruct((N_DEV, D), F32), jax.ShapeDtypeStruct((depth, N_DEV, sw), F32),
                   jax.ShapeDtypeStruct((depth, D), F32)], compiler_params=_params())(c_all, mod_w, lower)


def _mod_weight_grad(name, cs_t, dmod):
    depth, pad, sw = dmod.shape

    def body(c_ref, d_ref, g_ref):
        g_ref[...] = _dot(c_ref[...], d_ref[...])

    return pl.pallas_call(
        body, name=name, grid=(depth,),
        in_specs=[pl.BlockSpec((D, pad), lambda l: (0, 0)), pl.BlockSpec((None, pad, sw), lambda l: (l, 0, 0))],
        out_specs=pl.BlockSpec((None, D, sw), lambda l: (l, 0, 0)),
        out_shape=jax.ShapeDtypeStruct((depth, D, sw), F32), compiler_params=_params())(cs_t, dmod)


def _lower_bound_grad(name, lower, dlb):
    depth = lower.shape[0]

    def body(lo_ref, d_ref, out_ref):
        lo = lo_ref[...]
        e = jnp.exp(lo - jnp.max(lo, axis=0, keepdims=True))
        sm = e / jnp.sum(e, axis=0, keepdims=True)
        out_ref[...] = jnp.zeros_like(out_ref)
        dsm = [jnp.zeros((1, D), F32)]
        for l in range(1, depth):
            tail = d_ref[l:l + 1, :]
            for m in range(l + 1, depth):
                tail = tail + d_ref[m:m + 1, :]
            dsm.append(tail)
        inner = sm[1:2, :] * dsm[1]
        for l in range(2, depth):
            inner = inner + sm[l:l + 1, :] * dsm[l]
        for l in range(depth):
            out_ref[l:l + 1, :] = sm[l:l + 1, :] * (dsm[l] - inner)
        for j in range(SUBLANES - depth):
            row = d_ref[depth + j:depth + j + 1, :]
            tot = row[:, 0:HEAD_DIM]
            for hd in range(1, HEADS):
                tot = tot + row[:, hd * HEAD_DIM:(hd + 1) * HEAD_DIM]
            out_ref[depth + j:depth + j + 1, 0:HEAD_DIM] = tot

    return pl.pallas_call(body, name=name, out_shape=jax.ShapeDtypeStruct((SUBLANES, D), F32))(lower, dlb)


def _adamw(name, parts, w, m, v, layer=None, prev=None):
    p, r, c = parts.shape
    tr = _tile(r, max(SUBLANES, ADAMW_STEP_BYTES // (4 * c * (p + 7))))
    stacked = layer is not None
    n_prev = 4 if prev is not None else 0

    def body(*refs):
        parts_ref, w_ref, m_ref, v_ref = refs[:4]
        g_ref, d_ref, m_out, v_out = refs[4 + n_prev:]
        g = parts_ref[0].astype(F32)
        for q in range(1, p):
            g = g + parts_ref[q].astype(F32)
        m2 = ADAM_B1 * m_ref[...] + (1.0 - ADAM_B1) * g
        v2 = ADAM_B2 * v_ref[...] + (1.0 - ADAM_B2) * (g * g)
        m_hat = m2 / (1.0 - ADAM_B1 ** ADAM_STEP)
        v_hat = v2 / (1.0 - ADAM_B2 ** ADAM_STEP)
        g_ref[...] = g
        d_ref[...] = -ADAM_LR * (m_hat / (jnp.sqrt(v_hat) + ADAM_EPS) + ADAM_WD * w_ref[...])
        m_out[...] = m2
        v_out[...] = v2

    if stacked:
        spec = pl.BlockSpec((None, tr, c), lambda i: (layer, i, 0))
    else:
        spec = pl.BlockSpec((tr, c), lambda i: (i, 0))
    return pl.pallas_call(
        body, name=name, grid=(r // tr,),
        in_specs=[pl.BlockSpec((p, tr, c), lambda i: (0, i, 0)), spec, spec, spec] + [ANY] * n_prev, out_specs=[spec] * 4,
        out_shape=[jax.ShapeDtypeStruct(w.shape, F32)] * 4,
        input_output_aliases={4 + q: q for q in range(n_prev)}, compiler_params=_params(),
    )(parts, w, m, v, *(prev or []))


def _vec(*rows):
    rows = [r.reshape(1, D).astype(F32) for r in rows]
    return jnp.concatenate(rows + [jnp.zeros((SUBLANES - len(rows), D), F32)], axis=0)


def _pad_rows(a, rows=SUBLANES):
    a = a.reshape(-1, a.shape[-1])
    return jnp.concatenate([a, jnp.zeros((rows - a.shape[0], a.shape[1]), a.dtype)], axis=0) if a.shape[0] < rows else a


def kernel(x, c, mod_w, mod_b, norm_mix, norm_mlp, norm_final, rg_w_in, rg_conv_w, rg_conv_b, rg_w_a, rg_b_a, rg_w_x, rg_b_x, rg_lambda, rg_w_out, hg_w_in, hg_lower_bounds, hg_gnorm, hg_w_out, mlp_w1, mlp_w2, loss_target, m_mod_w, m_mod_b, m_norm_mix, m_norm_mlp, m_norm_final, m_rg_w_in, m_rg_conv_w, m_rg_conv_b, m_rg_w_a, m_rg_b_a, m_rg_w_x, m_rg_b_x, m_rg_lambda, m_rg_w_out, m_hg_w_in, m_hg_lower_bounds, m_hg_gnorm, m_hg_w_out, m_mlp_w1, m_mlp_w2, v_mod_w, v_mod_b, v_norm_mix, v_norm_mlp, v_norm_final, v_rg_w_in, v_rg_conv_w, v_rg_conv_b, v_rg_w_a, v_rg_b_a, v_rg_w_x, v_rg_b_x, v_rg_lambda, v_rg_w_out, v_hg_w_in, v_hg_lower_bounds, v_hg_gnorm, v_hg_w_out, v_mlp_w1, v_mlp_w2):
    me = 4 * lax.axis_index("x") + 2 * lax.axis_index("y") + lax.axis_index("c")
    x0 = x[0]
    target = loss_target[0]
    n_rg, n_hg = rg_w_in.shape[0], hg_w_in.shape[0]
    sw_mod = mod_w.shape[2]

    c_all, cw_all = _all_gather([_pad_rows(c), rg_conv_w.reshape(n_rg * CONV_WIDTH, -1)], "gather_cond")
    c_all = c_all[:, 0, :]
    conv_w = cw_all.transpose(1, 0, 2).reshape(n_rg, CONV_WIDTH, D)
    cs_all, mod_part, lb_all = _mod_forward("mod_forward", c_all, mod_w, hg_lower_bounds)
    (mod_gathered,) = _all_gather([mod_part.reshape(DEPTH * N_DEV, sw_mod)], "gather_mod")

    ids = iter(range(4 * DEPTH))
    shards = []
    for layer in range(DEPTH):
        j = layer // 2
        w_in, w_out = (rg_w_in[j], rg_w_out[j]) if layer % 2 == 0 else (hg_w_in[j], hg_w_out[j])
        shards.append([w_in.astype(MXU), w_out.astype(MXU), mlp_w1[layer].astype(MXU), mlp_w2[layer].astype(MXU)])
    shards, mod_gathered = lax.optimization_barrier((shards, mod_gathered))
    weights = []
    for layer in range(DEPTH):
        if layer == 0:
            g_in, g_out = _all_gather_async(shards[0][:2], "gather_mixer_weights_0", next(ids))
            g_w1, g_w2 = _all_gather_async(shards[0][2:], "gather_mlp_weights_0", next(ids))
        else:
            g_in, g_out, g_w1, g_w2 = _all_gather_async(shards[layer], f"gather_weights_{layer}", next(ids))
        weights.append(dict(w_in=g_in, w_out=g_out.reshape(D, D), w1=g_w1, w2=g_w2.reshape(-1, D)))

    mod_mine = lax.dynamic_index_in_dim(mod_gathered.reshape(N_DEV, DEPTH, N_DEV, sw_mod), me, axis=2, keepdims=False)
    mod = mod_mine.transpose(1, 0, 2).reshape(DEPTH, 6, D) + mod_b.reshape(DEPTH, 6, D)

    saved = []
    xl = x0
    for layer in range(DEPTH):
        j, wt = layer // 2, weights[layer]
        is_rg = layer % 2 == 0
        s = dict(x=xl)
        s["vec_mix"] = _vec(norm_mix[layer], mod[layer, 1], mod[layer, 0])
        s["vec_mlp"] = _vec(norm_mlp[layer], mod[layer, 4], mod[layer, 3])
        s["gate_mix"], s["gate_mlp"] = _vec(mod[layer, 2]), _vec(mod[layer, 5])
        s["h"] = _norm_mod("norm_mod", xl, s["vec_mix"])
        if is_rg:
            (s["z"],) = _mm_cols("rg_in", s["h"], wt["w_in"])
            s["cw"] = conv_w[j]
            s["vec"] = _vec(rg_conv_b[j], rg_b_a[j], rg_b_x[j], rg_lambda[j])
            s["wa"], s["wx"] = rg_w_a[j].astype(MXU), rg_w_x[j].astype(MXU)
            s["p"], s["hr"] = _rg_forward("rg_forward", s["z"], s["cw"], s["vec"], s["wa"], s["wx"])
        else:
            (s["z"],) = _mm_cols("hg_in", s["h"], wt["w_in"])
            s["vec"] = _vec(lb_all[layer], jnp.tile(hg_gnorm[j], HEADS))
            s["p"], s["o"], s["ss"] = _hg_forward("hg_forward", s["z"], s["vec"])
        s["y"], s["x1"] = _mm_rows("mix_out", s["p"], wt["w_out"], _ep_residual, (F32, F32), extras=(xl, s["gate_mix"]))
        s["h2"] = _norm_mod("norm_mod", s["x1"], s["vec_mlp"])
        s["a"], s["s"] = _mm_cols("mlp_in", s["h2"], wt["w1"], _ep_relu2, (MXU, MXU))
        s["ff"], xl = _mm_rows("mlp_out", s["s"], wt["w2"], _ep_residual, (F32, F32), extras=(s["x1"], s["gate_mlp"]))
        saved.append(s)

    dx, head = _loss_head("loss_head", xl, target, _vec(norm_final))

    results = {}
    big = dict(rg_w_in=(rg_w_in, m_rg_w_in, v_rg_w_in), rg_w_out=(rg_w_out, m_rg_w_out, v_rg_w_out),
               hg_w_in=(hg_w_in, m_hg_w_in, v_hg_w_in), hg_w_out=(hg_w_out, m_hg_w_out, v_hg_w_out),
               mlp_w1=(mlp_w1, m_mlp_w1, v_mlp_w1), mlp_w2=(mlp_w2, m_mlp_w2, v_mlp_w2))

    def update_layer(layer, landed):
        mixer = "rg" if layer % 2 == 0 else "hg"
        l_in, l_out, l_w1, l_w2 = landed
        for nm, parts, idx in ((f"{mixer}_w_in", l_in, layer // 2), (f"{mixer}_w_out", l_out, layer // 2),
                               ("mlp_w1", l_w1, layer), ("mlp_w2", l_w2, layer)):
            w, m, v = big[nm]
            results[nm] = _adamw(f"adamw_{nm}", parts, w, m, v, layer=idx, prev=results.get(nm))

    landed = [None] * DEPTH
    small = [None] * DEPTH
    parity = lax.axis_index("c").astype(jnp.int32).reshape(1)

    def send_chip_sums(layer, grads, from_sibling, anchor):
        sums = _pair_sum("pair_sum", parity, grads, from_sibling)
        sums, anchor = lax.optimization_barrier((sums, anchor))
        landed[layer] = _chip_exchange_async(sums, f"exchange_grads_{layer}", next(ids))
        return anchor

    pending = None
    for layer in reversed(range(DEPTH)):
        j, wt, s = layer // 2, weights[layer], saved[layer]
        is_rg = layer % 2 == 0
        dff, g_mlp = _residual_grad("residual_grad", dx, s["ff"], s["gate_mlp"])
        da = _mm_rows_t("mlp_out_t", dff, wt["w2"], _ep_relu2_grad, MXU, extras=(s["a"],))
        dw2 = _mm_grad("mlp_out_grad", s["s"], dff, False).reshape(N_DEV, -1, D)
        dw1 = _mm_grad("mlp_in_grad", s["h2"], da, True)
        if pending is not None:
            da = send_chip_sums(*pending, da)
        dh2 = _mm_cols_t("mlp_in_t", da, wt["w1"])
        dx1, n_mlp = _norm_mod_grad("norm_mod_grad", dh2, s["x1"], s["vec_mlp"], dx)
        dyb, g_mix = _residual_grad("residual_grad", dx1, s["y"], s["gate_mix"])
        dp = _mm_rows_t("mix_out_t", dyb, wt["w_out"])
        dw_out = _mm_grad("mix_out_grad", s["p"], dyb, False).reshape(N_DEV, -1, D)
        if is_rg:
            dz, dvec, dcw, dwa, dwx = _rg_backward("rg_backward", dp, s["z"], s["hr"], s["cw"], s["vec"], s["wa"], s["wx"])
            gate_grads, dz = lax.optimization_barrier(
                ([dwa.reshape(-1, HEAD_DIM).astype(MXU), dwx.reshape(-1, HEAD_DIM).astype(MXU)], dz))
            gate_parts = _all_gather_async(gate_grads, f"gather_gate_grads_{j}", next(ids))
            mixer_small = dict(dvec=dvec, dcw=dcw, gate_parts=gate_parts)
            dw_in = _mm_grad("rg_in_grad", s["h"], dz, True)
            dh = _mm_cols_t("rg_in_t", dz, wt["w_in"])
        else:
            dz, dvec = _hg_backward("hg_backward", dp, s["z"], s["o"], s["ss"], s["vec"])
            mixer_small = dict(dvec=dvec)
            dw_in = _mm_grad("hg_in_grad", s["h"], dz, True)
            dh = _mm_cols_t("hg_in_t", dz, wt["w_in"])
        grads = [dw_in, dw_out, dw1, dw2]
        pending = (layer, grads, _sibling_send_async(grads, f"pair_grads_{layer}", next(ids)))
        dx, n_mix = _norm_mod_grad("norm_mod_grad", dh, s["x"], s["vec_mix"], dx1)
        small[layer] = dict(g_mlp=g_mlp, n_mlp=n_mlp, g_mix=g_mix, n_mix=n_mix, **mixer_small)
        if layer + 1 < DEPTH:
            landed[layer + 1], dx = lax.optimization_barrier((landed[layer + 1], dx))
            if "gate_parts" in small[layer + 1]:
                small[layer + 1]["gate_parts"], dx = lax.optimization_barrier((small[layer + 1]["gate_parts"], dx))
            update_layer(layer + 1, landed[layer + 1])

    dlb_rows = [jnp.zeros((1, D), F32) if l % 2 == 0 else small[l]["dvec"][0:1] for l in range(DEPTH)]
    dgn_rows = [small[2 * j + 1]["dvec"][1:2] for j in range(n_hg)]
    lb_grad = _lower_bound_grad("lower_bound_grad", hg_lower_bounds, _pad_rows(jnp.concatenate(dlb_rows + dgn_rows, axis=0)))
    dmod = jnp.stack([jnp.concatenate([small[l]["n_mix"][2], small[l]["n_mix"][1], small[l]["g_mix"][0],
                                       small[l]["n_mlp"][2], small[l]["n_mlp"][1], small[l]["g_mlp"][0]]) for l in range(DEPTH)])
    groups = [
        dmod.reshape(DEPTH * 6, D),
        jnp.stack([small[l]["n_mix"][0] for l in range(DEPTH)]),
        jnp.stack([small[l]["n_mlp"][0] for l in range(DEPTH)]),
        head[0:1],
        jnp.stack([small[2 * j]["dvec"][0] for j in range(n_rg)]),
        jnp.stack([small[2 * j]["dvec"][1] for j in range(n_rg)]),
        jnp.stack([small[2 * j]["dvec"][2] for j in range(n_rg)]),
        jnp.stack([small[2 * j]["dvec"][3] for j in range(n_rg)]),
        lb_grad[0:DEPTH],
        jnp.concatenate([lb_grad[DEPTH + j:DEPTH + j + 1, 0:HEAD_DIM] for j in range(n_hg)]
                        + [jnp.zeros((1, D - n_hg * HEAD_DIM), F32)], axis=1),
        jnp.concatenate([small[2 * j]["dcw"][0:CONV_WIDTH] for j in range(n_rg)], axis=0),
        head[1:2],
    ]
    params = [mod_b, norm_mix, norm_mlp, norm_final, rg_conv_b, rg_b_a, rg_b_x, rg_lambda, hg_lower_bounds, hg_gnorm]
    moms = [m_mod_b, m_norm_mix, m_norm_mlp, m_norm_final, m_rg_conv_b, m_rg_b_a, m_rg_b_x, m_rg_lambda, m_hg_lower_bounds, m_hg_gnorm]
    vars_ = [v_mod_b, v_norm_mix, v_norm_mlp, v_norm_final, v_rg_conv_b, v_rg_b_a, v_rg_b_x, v_rg_lambda, v_hg_lower_bounds, v_hg_gnorm]
    offsets, rows_of, at = [], [], 0
    for g in groups:
        offsets.append(at)
        rows_of.append(g.shape[0])
        at += -(-g.shape[0] // SUBLANES) * SUBLANES
    packed = jnp.concatenate([_pad_rows(g, -(-g.shape[0] // SUBLANES) * SUBLANES) for g in groups], axis=0)
    (small_parts,) = _all_gather([packed], "gather_small_grads")
    last_layer, last_grads, from_sibling = pending
    from_sibling, small_parts = lax.optimization_barrier((from_sibling, small_parts))
    send_chip_sums(last_layer, last_grads, from_sibling, dx)

    def pack_like(arrs):
        out = []
        for g_rows, off, a in zip(rows_of, offsets, arrs):
            flat = a.reshape(-1)
            flat = jnp.concatenate([flat, jnp.zeros((g_rows * D - flat.shape[0],), F32)])
            out.append(_pad_rows(flat.reshape(g_rows, D), -(-g_rows // SUBLANES) * SUBLANES))
        rest = packed.shape[0] - sum(o.shape[0] for o in out)
        return jnp.concatenate(out + [jnp.zeros((rest, D), F32)], axis=0)

    small_out = _adamw("adamw_small", small_parts, pack_like(params), pack_like(moms), pack_like(vars_))

    def unpack(q, idx, like):
        rows = small_out[q][offsets[idx]:offsets[idx] + rows_of[idx]]
        return rows.reshape(-1)[:like.size].reshape(like.shape)

    loss = jnp.sum(small_out[0][offsets[11]])
    names =["mod_b", "norm_mix", "norm_mlp", "norm_final", "rg_conv_b", "rg_b_a", "rg_b_x", "rg_lambda", "hg_lower_bounds", "hg_gnorm"]
    for idx, (nm, like) in enumerate(zip(names, params)):
        results[nm] = [unpack(q, idx, like) for q in range(4)]

    cw_parts = lax.dynamic_slice_in_dim(small_parts[:, offsets[10]:offsets[10] + n_rg * CONV_WIDTH, :], me * (D // N_DEV), D // N_DEV, axis=2)
    shp = rg_conv_w.shape
    results["rg_conv_w"] = [o.reshape(shp) for o in _adamw(
        "adamw_conv", cw_parts, rg_conv_w.reshape(-1, shp[-1]), m_rg_conv_w.reshape(-1, shp[-1]), v_rg_conv_w.reshape(-1, shp[-1]))]
    shp = rg_w_a.shape
    stacked = (n_rg, HEADS * HEAD_DIM, HEAD_DIM)
    for nm, which, (w, m, v) in (("rg_w_a", 0, (rg_w_a, m_rg_w_a, v_rg_w_a)), ("rg_w_x", 1, (rg_w_x, m_rg_w_x, v_rg_w_x))):
        out = None
        for j in reversed(range(n_rg)):
            out = _adamw("adamw_gate", small[2 * j]["gate_parts"][which], w.reshape(stacked), m.reshape(stacked),
                         v.reshape(stacked), layer=j, prev=out)
        results[nm] = [o.reshape(shp) for o in out]

    dmod_all = small_parts[:, 0:DEPTH * 6, :].reshape(N_DEV, DEPTH, 6 * D)
    dmod_cols = lax.dynamic_slice_in_dim(dmod_all, me * sw_mod, sw_mod, axis=2).transpose(1, 0, 2)
    pad = HEAD_DIM - N_DEV
    dmod_pad = jnp.concatenate([dmod_cols, jnp.zeros((DEPTH, pad, sw_mod), F32)], axis=1).astype(MXU)
    cs_t = jnp.concatenate([cs_all.T, jnp.zeros((D, pad), F32)], axis=1).astype(MXU)
    g_mod_w = _mod_weight_grad("mod_weight_grad", cs_t, dmod_pad)
    results["mod_w"] = [o.reshape(mod_w.shape) for o in _adamw(
        "adamw_mod", g_mod_w.reshape(1, -1, sw_mod), mod_w.reshape(-1, sw_mod), m_mod_w.reshape(-1, sw_mod), v_mod_w.reshape(-1, sw_mod))]

    update_layer(0, landed[0])

    order = ["mod_w", "mod_b", "norm_mix", "norm_mlp", "norm_final", "rg_w_in", "rg_conv_w", "rg_conv_b", "rg_w_a", "rg_b_a", "rg_w_x",
             "rg_b_x", "rg_lambda", "rg_w_out", "hg_w_in", "hg_lower_bounds", "hg_gnorm", "hg_w_out", "mlp_w1", "mlp_w2"]
    return (loss, dx[None], *[results[n][0] for n in order], *[results[n][1] for n in order],
            *[results[n][2] for n in order], *[results[n][3] for n in order])
```

```python
import functools

import jax
import jax.numpy as jnp
from jax import lax
from jax.experimental import pallas as pl
from jax.experimental.pallas import tpu as pltpu
from jax.experimental.pallas import tpu_sc as plsc

F32 = jnp.float32
MXU = jnp.bfloat16

N_DEV = 8
D = 1024
DEPTH = 4
HEADS = 8
HEAD_DIM = 128
CONV_WIDTH = 4
LRU_C = 8.0
HG_CHUNK = 64
NORM_EPS = 1e-6
GNORM_EPS = 1e-5
ADAM_LR = 0.001
ADAM_B1 = 0.9
ADAM_B2 = 0.999
ADAM_EPS = 1e-08
ADAM_WD = 0.01
ADAM_STEP = 10
GELU_C = 0.7978845608028654
GELU_K = 0.044715
EXP_CLAMP = 80.0
SUBLANES = 8
VMEM_LIMIT = 48 * 1024 * 1024
ROW_TILE = 256
MM_TILE = 1024
MM_ROWS = 2048
ADAMW_STEP_BYTES = 8 * 1024 * 1024

MESH = pl.DeviceIdType.MESH
ANY = pl.BlockSpec(memory_space=pl.ANY)


def _params():
    return pltpu.CompilerParams(vmem_limit_bytes=VMEM_LIMIT)


def _tile(n, target):
    if n <= target:
        return n
    t = target // SUBLANES * SUBLANES
    while n % t:
        t -= SUBLANES
    return t


def _sigmoid(x):
    return 1.0 / (1.0 + jnp.exp(-x))


def _silu(x):
    return x * _sigmoid(x)


def _silu_grad(x):
    s = _sigmoid(x)
    return s * (1.0 + x * (1.0 - s))


def _gelu(y):
    return 0.5 * y * (1.0 + jnp.tanh(GELU_C * (y + GELU_K * y * y * y)))


def _gelu_grad(y):
    th = jnp.tanh(GELU_C * (y + GELU_K * y * y * y))
    return 0.5 * (1.0 + th) + 0.5 * y * (1.0 - th * th) * GELU_C * (1.0 + 3.0 * GELU_K * y * y)


def _dot(a, b):
    return lax.dot_general(a.astype(MXU), b.astype(MXU), (((1,), (0,)), ((), ())), preferred_element_type=F32)


def _dot_nt(a, b):
    return lax.dot_general(a.astype(MXU), b.astype(MXU), (((1,), (1,)), ((), ())), preferred_element_type=F32)


def _dot_tn(a, b):
    return lax.dot_general(a.astype(MXU), b.astype(MXU), (((0,), (0,)), ((), ())), preferred_element_type=F32)


def _dot_exact(tri, x):
    t = tri.astype(MXU)
    hi = x.astype(MXU)
    r1 = x - hi.astype(F32)
    mid = r1.astype(MXU)
    lo = (r1 - mid.astype(F32)).astype(MXU)
    dn = (((1,), (0,)), ((), ()))
    return (lax.dot_general(t, hi, dn, preferred_element_type=F32) + lax.dot_general(t, mid, dn, preferred_element_type=F32)
            + lax.dot_general(t, lo, dn, preferred_element_type=F32))


def _row_sum(v):
    return jnp.sum(v, axis=0, keepdims=True)


def _handshake(partners):
    barrier = pltpu.get_barrier_semaphore()
    for p in partners:
        pl.semaphore_signal(barrier, inc=1, device_id=p, device_id_type=MESH)
    pl.semaphore_wait(barrier, len(partners))


def _gather_body(n, per_array_sems, handshake):
    def body(*refs):
        ins, outs = refs[:n], refs[n:2 * n]
        send_sems, recv_sems, local_sems = refs[2 * n:]
        x, y, c = lax.axis_index("x"), lax.axis_index("y"), lax.axis_index("c")
        me, sibling = (x, y, c), (x, y, 1 - c)
        chips = [(1 - x, y), (x, 1 - y), (1 - x, 1 - y)]
        if handshake:
            _handshake([sibling] + [(*chip, c) for chip in chips])

        def sem(sems, a, k):
            return sems.at[a, k] if per_array_sems else sems.at[k]

        def slot(a, p):
            return outs[a].at[4 * p[0] + 2 * p[1] + p[2]]

        def copy(a, k, block, to, src=None):
            return pltpu.make_async_remote_copy(
                src_ref=slot(a, block) if src is None else src, dst_ref=slot(a, block),
                send_sem=sem(send_sems, a, k), recv_sem=sem(recv_sems, a, k), device_id=to, device_id_type=MESH)

        mine = [pltpu.make_async_copy(ins[a], slot(a, me), local_sems.at[a if per_array_sems else 0]) for a in range(n)]
        for cp in mine:
            cp.start()
        first = []
        for a in range(n):
            first.append(copy(a, 0, me, sibling, src=ins[a]))
            first += [copy(a, 1 + j, me, (*chip, c), src=ins[a]) for j, chip in enumerate(chips)]
        for cp in first:
            cp.start()
        passed = []
        for j, chip in enumerate(chips):
            for a in range(n):
                copy(a, 1 + j, (*chip, c), me).wait_recv()
            for a in range(n):
                cp = copy(a, 4 + j, (*chip, c), sibling)
                cp.start()
                passed.append(cp)
        for a in range(n):
            copy(a, 0, sibling, me).wait_recv()
        for j, chip in enumerate(chips):
            for a in range(n):
                copy(a, 4 + j, (*chip, 1 - c), me).wait_recv()
        for cp in first + passed:
            cp.wait_send()
        for cp in mine:
            cp.wait()

    return body


def _sibling_send_body(n, per_array_sems, handshake):
    def body(*refs):
        ins, outs = refs[:n], refs[n:2 * n]
        send_sems, recv_sems, _ = refs[2 * n:]
        x, y, c = lax.axis_index("x"), lax.axis_index("y"), lax.axis_index("c")
        sibling = (x, y, 1 - c)
        if handshake:
            _handshake([sibling])

        def sem(sems, a):
            return sems.at[a, 0] if per_array_sems else sems.at[0]

        copies = [pltpu.make_async_remote_copy(
            src_ref=ins[a].at[2 * q + 1 - c], dst_ref=outs[a].at[q], send_sem=sem(send_sems, a), recv_sem=sem(recv_sems, a),
            device_id=sibling, device_id_type=MESH) for a in range(n) for q in range(4)]
        for cp in copies:
            cp.start()
        for cp in copies:
            cp.wait_recv()
        for cp in copies:
            cp.wait_send()

    return body


def _chip_exchange_body(n, per_array_sems, handshake):
    def body(*refs):
        ins, outs = refs[:n], refs[n:2 * n]
        send_sems, recv_sems, local_sems = refs[2 * n:]
        x, y, c = lax.axis_index("x"), lax.axis_index("y"), lax.axis_index("c")
        my_chip = 2 * x + y
        chips = [(1 - x, y), (x, 1 - y), (1 - x, 1 - y)]
        if handshake:
            _handshake([(*chip, c) for chip in chips])

        def sem(sems, a, k):
            return sems.at[a, k] if per_array_sems else sems.at[k]

        def copy(a, k, landing):
            px, py = chips[k]
            return pltpu.make_async_remote_copy(
                src_ref=ins[a].at[2 * px + py], dst_ref=outs[a].at[landing], send_sem=sem(send_sems, a, k),
                recv_sem=sem(recv_sems, a, k), device_id=(px, py, c), device_id_type=MESH)

        mine = [pltpu.make_async_copy(ins[a].at[my_chip], outs[a].at[my_chip], local_sems.at[a if per_array_sems else 0])
                for a in range(n)]
        for cp in mine:
            cp.start()
        sent = [copy(a, k, my_chip) for a in range(n) for k in range(3)]
        for cp in sent:
            cp.start()
        for a in range(n):
            for k, (px, py) in enumerate(chips):
                copy(a, k, 2 * px + py).wait_recv()
        for cp in sent:
            cp.wait_send()
        for cp in mine:
            cp.wait()

    return body


def _all_gather(arrs, name):
    n = len(arrs)
    return pl.pallas_call(
        _gather_body(n, True, False), name=name, in_specs=[ANY] * n, out_specs=[ANY] * n,
        out_shape=[jax.ShapeDtypeStruct((N_DEV,) + a.shape, a.dtype) for a in arrs],
        scratch_shapes=[pltpu.SemaphoreType.DMA((n, 7)), pltpu.SemaphoreType.DMA((n, 7)), pltpu.SemaphoreType.DMA((n,))],
    )(*arrs)


def _on_sequencer(body, arrs, out_type, name, collective_id):
    return pl.kernel(
        body, name=name, out_type=out_type, mesh=plsc.ScalarSubcoreMesh(axis_name="sequencer", num_cores=1),
        scratch_types=[pltpu.SemaphoreType.DMA((7,)), pltpu.SemaphoreType.DMA((7,)), pltpu.SemaphoreType.DMA((1,))],
        compiler_params=pltpu.CompilerParams(collective_id=collective_id))(*arrs)


def _all_gather_async(arrs, name, collective_id):
    out_type = [jax.ShapeDtypeStruct((N_DEV,) + a.shape, a.dtype) for a in arrs]
    return _on_sequencer(_gather_body(len(arrs), False, True), arrs, out_type, name, collective_id)


def _sibling_send_async(arrs, name, collective_id):
    out_type = [jax.ShapeDtypeStruct((N_DEV // 2,) + a.shape[1:], a.dtype) for a in arrs]
    return _on_sequencer(_sibling_send_body(len(arrs), False, True), arrs, out_type, name, collective_id)


def _chip_exchange_async(arrs, name, collective_id):
    out_type = [jax.ShapeDtypeStruct(a.shape, a.dtype) for a in arrs]
    return _on_sequencer(_chip_exchange_body(len(arrs), False, True), arrs, out_type, name, collective_id)


def _pair_sum(name, parity, mine, theirs):
    n = len(mine)

    def body(par_ref, *refs):
        for a in range(n):
            refs[2 * n + a][...] = (refs[a][...].astype(F32) + refs[n + a][...].astype(F32)).astype(refs[2 * n + a].dtype)

    def block(a):
        return (None,) + a.shape[1:]

    grid_spec = pltpu.PrefetchScalarGridSpec(
        num_scalar_prefetch=1, grid=(N_DEV // 2,),
        in_specs=[pl.BlockSpec(block(a), lambda q, par: (2 * q + par[0], 0, 0)) for a in mine]
        + [pl.BlockSpec(block(a), lambda q, par: (q, 0, 0)) for a in theirs],
        out_specs=[pl.BlockSpec(block(a), lambda q, par: (q, 0, 0)) for a in theirs])
    return pl.pallas_call(body, name=name, grid_spec=grid_spec, out_shape=[jax.ShapeDtypeStruct(a.shape, a.dtype) for a in theirs],
                          compiler_params=_params())(parity, *mine, *theirs)


NN = (((1,), (0,)), ((), ()))
NT = (((1,), (1,)), ((), ()))
TN = (((0,), (0,)), ((), ()))


def _matmul(name, a, b, dims, grid, a_spec, b_spec, outs, epilogue, extras=(), acc_shape=None):
    n_in = 2 + len(extras)
    n_out = len(outs)
    nk = grid[2]

    def body(*refs):
        a_ref, b_ref = refs[0], refs[1]
        ex, out_refs = refs[2:n_in], refs[n_in:n_in + n_out]

        def part():
            return lax.dot_general(a_ref[...], b_ref[...], dims, preferred_element_type=F32)

        if nk == 1:
            epilogue(part(), ex, out_refs)
            return
        acc = refs[n_in + n_out]
        k = pl.program_id(2)

        @pl.when(k == 0)
        def _():
            acc[...] = part()

        @pl.when(k > 0)
        def _():
            acc[...] += part()

        @pl.when(k == nk - 1)
        def _():
            epilogue(acc[...], ex, out_refs)

    return pl.pallas_call(
        body, name=name, grid=grid, in_specs=[a_spec, b_spec] + [s for _, s in extras], out_specs=[s for _, _, s in outs],
        out_shape=[jax.ShapeDtypeStruct(sh, dt) for sh, dt, _ in outs],
        scratch_shapes=[] if nk == 1 else [pltpu.VMEM(acc_shape, F32)], compiler_params=_params(),
    )(a, b, *[e for e, _ in extras])


def _ep_store(acc, ex, outs):
    outs[0][...] = acc.astype(outs[0].dtype)


def _ep_residual(acc, ex, outs):
    outs[0][...] = acc
    outs[1][...] = ex[0][...] + ex[1][0:1, :] * acc


def _ep_relu2(acc, ex, outs):
    r = jnp.maximum(acc, 0.0)
    outs[0][...] = (r * r).astype(outs[0].dtype)


def _ep_relu2_grad(acc, ex, outs):
    outs[0][...] = (acc * (2.0 * jnp.sqrt(ex[0][...].astype(F32)))).astype(outs[0].dtype)


def _mm_cols(name, a, wg, epilogue=_ep_store, out_dtypes=(F32,)):
    m, k = a.shape
    sw = wg.shape[2]
    tm = _tile(m, MM_ROWS)
    spec = pl.BlockSpec((tm, sw), lambda i, j, kk: (i, j))
    return _matmul(name, a, wg, NN, (m // tm, N_DEV, 1), pl.BlockSpec((tm, k), lambda i, j, kk: (i, 0)),
                   pl.BlockSpec((None, k, sw), lambda i, j, kk: (j, 0, 0)),
                   [((m, N_DEV * sw), dt, spec) for dt in out_dtypes], epilogue)


def _mm_cols_t(name, a, wg, out_dtype=F32):
    m = a.shape[0]
    _, r, sw = wg.shape
    tm, tn = _tile(m, MM_ROWS), _tile(r, MM_TILE)
    return _matmul(name, a, wg, NT, (m // tm, r // tn, N_DEV), pl.BlockSpec((tm, sw), lambda i, j, kk: (i, kk)),
                   pl.BlockSpec((None, tn, sw), lambda i, j, kk: (kk, j, 0)),
                   [((m, r), out_dtype, pl.BlockSpec((tm, tn), lambda i, j, kk: (i, j)))], _ep_store, acc_shape=(tm, tn))[0]


def _mm_rows(name, a, w, epilogue, outs_dtypes, extras=(), rows=MM_ROWS, cols=MM_TILE // 2):
    m, k = a.shape
    n = w.shape[1]
    tm, tn, tk = _tile(m, rows), _tile(n, cols), _tile(k, MM_TILE)
    spec = pl.BlockSpec((tm, tn), lambda i, j, kk: (i, j))
    return _matmul(name, a, w, NN, (m // tm, n // tn, k // tk), pl.BlockSpec((tm, tk), lambda i, j, kk: (i, kk)),
                   pl.BlockSpec((tk, tn), lambda i, j, kk: (kk, j)), [((m, n), dt, spec) for dt in outs_dtypes], epilogue,
                   extras=[(e, spec if e.shape[0] == m else pl.BlockSpec((SUBLANES, tn), lambda i, j, kk: (0, j))) for e in extras],
                   acc_shape=(tm, tn))


def _mm_rows_t(name, a, w, epilogue=_ep_store, out_dtype=F32, extras=()):
    m, c = a.shape
    r = w.shape[0]
    tm, tn, tk = _tile(m, MM_ROWS), _tile(r, MM_TILE // 2), _tile(c, MM_TILE)
    spec = pl.BlockSpec((tm, tn), lambda i, j, kk: (i, j))
    return _matmul(name, a, w, NT, (m // tm, r // tn, c // tk), pl.BlockSpec((tm, tk), lambda i, j, kk: (i, kk)),
                   pl.BlockSpec((tn, tk), lambda i, j, kk: (j, kk)), [((m, r), out_dtype, spec)], epilogue,
                   extras=[(e, spec) for e in extras], acc_shape=(tm, tn))[0]


def _mm_grad(name, a, b, shard_cols):
    t, m = a.shape
    n = b.shape[1]
    tk, tm = _tile(t, MM_ROWS), _tile(m, MM_TILE)
    if shard_cols:
        tn = n // N_DEV
        out = ((N_DEV, m, tn), MXU, pl.BlockSpec((None, tm, tn), lambda i, j, kk: (j, i, 0)))
    else:
        tn = _tile(n, MM_TILE)
        out = ((m, n), MXU, pl.BlockSpec((tm, tn), lambda i, j, kk: (i, j)))
    return _matmul(name, a, b, TN, (m // tm, n // tn, t // tk), pl.BlockSpec((tk, tm), lambda i, j, kk: (kk, i)),
                   pl.BlockSpec((tk, tn), lambda i, j, kk: (kk, j)), [out], _ep_store, acc_shape=(tm, tn))[0]


def _rows_spec(tt, width=D):
    return pl.BlockSpec((tt, width), lambda i: (i, 0))


def _vec_spec(rows=SUBLANES, width=D):
    return pl.BlockSpec((rows, width), lambda i: (0, 0))


def _norm_mod(name, x, vec):
    t = x.shape[0]
    tt = _tile(t, ROW_TILE)

    def body(x_ref, v_ref, h_ref):
        xv = x_ref[...]
        r = lax.rsqrt(jnp.mean(xv * xv, axis=-1, keepdims=True) + NORM_EPS)
        h = (xv * r) * v_ref[0:1, :]
        h_ref[...] = (h * (1.0 + v_ref[1:2, :]) + v_ref[2:3, :]).astype(h_ref.dtype)

    return pl.pallas_call(body, name=name, grid=(t // tt,), in_specs=[_rows_spec(tt), _vec_spec()], out_specs=_rows_spec(tt),
                          out_shape=jax.ShapeDtypeStruct((t, D), MXU))(x, vec)


def _norm_mod_grad(name, dh, x, vec, dres):
    t = x.shape[0]
    tt = _tile(t, ROW_TILE)
    nt = t // tt

    def body(dh_ref, x_ref, v_ref, dres_ref, dx_ref, acc_ref):
        i = pl.program_id(0)

        @pl.when(i == 0)
        def _():
            acc_ref[...] = jnp.zeros_like(acc_ref)

        xv, dhv = x_ref[...], dh_ref[...]
        r = lax.rsqrt(jnp.mean(xv * xv, axis=-1, keepdims=True) + NORM_EPS)
        xn = xv * r
        w = v_ref[0:1, :] * (1.0 + v_ref[1:2, :])
        acc_ref[0:1, :] += _row_sum(dhv * xn)
        acc_ref[2:3, :] += _row_sum(dhv)
        dxn = dhv * w
        dx_ref[...] = dres_ref[...] + r * (dxn - xn * jnp.mean(dxn * xn, axis=-1, keepdims=True))

        @pl.when(i == nt - 1)
        def _():
            dw = acc_ref[0:1, :]
            acc_ref[1:2, :] = dw * v_ref[0:1, :]
            acc_ref[0:1, :] = dw * (1.0 + v_ref[1:2, :])

    return pl.pallas_call(
        body, name=name, grid=(nt,), in_specs=[_rows_spec(tt), _rows_spec(tt), _vec_spec(), _rows_spec(tt)],
        out_specs=[_rows_spec(tt), _vec_spec()],
        out_shape=[jax.ShapeDtypeStruct((t, D), F32), jax.ShapeDtypeStruct((SUBLANES, D), F32)])(dh, x, vec, dres)


def _residual_grad(name, dxo, y, vec):
    t = y.shape[0]
    tt = _tile(t, ROW_TILE)

    def body(dx_ref, y_ref, v_ref, dy_ref, acc_ref):
        @pl.when(pl.program_id(0) == 0)
        def _():
            acc_ref[...] = jnp.zeros_like(acc_ref)

        dxv = dx_ref[...]
        dy_ref[...] = (dxv * v_ref[0:1, :]).astype(dy_ref.dtype)
        acc_ref[0:1, :] += _row_sum(dxv * y_ref[...])

    return pl.pallas_call(
        body, name=name, grid=(t // tt,), in_specs=[_rows_spec(tt), _rows_spec(tt), _vec_spec()],
        out_specs=[_rows_spec(tt), _vec_spec()],
        out_shape=[jax.ShapeDtypeStruct((t, D), MXU), jax.ShapeDtypeStruct((SUBLANES, D), F32)])(dxo, y, vec)


def _loss_head(name, x, target, vec):
    t = x.shape[0]
    tt = _tile(t, ROW_TILE)

    def body(x_ref, t_ref, v_ref, dx_ref, acc_ref):
        @pl.when(pl.program_id(0) == 0)
        def _():
            acc_ref[...] = jnp.zeros_like(acc_ref)

        xv = x_ref[...]
        r = lax.rsqrt(jnp.mean(xv * xv, axis=-1, keepdims=True) + NORM_EPS)
        xn = xv * r
        gain = v_ref[0:1, :]
        err = xn * gain - t_ref[...]
        acc_ref[1:2, :] += _row_sum(err * err) * (0.5 / D)
        dout = err * (1.0 / D)
        acc_ref[0:1, :] += _row_sum(dout * xn)
        dxn = dout * gain
        dx_ref[...] = r * (dxn - xn * jnp.mean(dxn * xn, axis=-1, keepdims=True))

    return pl.pallas_call(
        body, name=name, grid=(t // tt,), in_specs=[_rows_spec(tt), _rows_spec(tt), _vec_spec()],
        out_specs=[_rows_spec(tt), _vec_spec()],
        out_shape=[jax.ShapeDtypeStruct((t, D), F32), jax.ShapeDtypeStruct((SUBLANES, D), F32)])(x, target, vec)


def _shift_down(x, halo, k):
    y = pltpu.roll(x, k, 0)
    top = jnp.where(lax.broadcasted_iota(jnp.int32, halo.shape, 0) < k, pltpu.roll(halo, k, 0), y[0:SUBLANES, :])
    return jnp.concatenate([top, y[SUBLANES:, :]], axis=0)


def _shift_up(x, halo, k):
    n = x.shape[0]
    y = pltpu.roll(x, n - k, 0)
    bottom = jnp.where(lax.broadcasted_iota(jnp.int32, halo.shape, 0) >= SUBLANES - k, pltpu.roll(halo, SUBLANES - k, 0),
                       y[n - SUBLANES:, :])
    return jnp.concatenate([y[:n - SUBLANES, :], bottom], axis=0)


def _rg_gates(xb, halo, cw_ref, vec_ref, wa_ref, wx_ref, at_start):
    shifted = [xb] + [_shift_down(xb, halo, k) for k in range(1, CONV_WIDTH)]
    xc = vec_ref[0:1, :] + shifted[0] * cw_ref[CONV_WIDTH - 1:CONV_WIDTH, :]
    for k in range(1, CONV_WIDTH):
        xc = xc + shifted[k] * cw_ref[CONV_WIDTH - 1 - k:CONV_WIDTH - k, :]
    heads = [slice(h * HEAD_DIM, (h + 1) * HEAD_DIM) for h in range(HEADS)]
    pa = jnp.concatenate([_dot(xc[:, s], wa_ref[h]) for h, s in enumerate(heads)], axis=1) + vec_ref[1:2, :]
    px = jnp.concatenate([_dot(xc[:, s], wx_ref[h]) for h, s in enumerate(heads)], axis=1) + vec_ref[2:3, :]
    ra, ia = _sigmoid(pa), _sigmoid(px)
    nl = -vec_ref[3:4, :]
    sp = jnp.maximum(nl, 0.0) + jnp.log(1.0 + jnp.exp(-jnp.abs(nl)))
    log_a = (-LRU_C) * ra * sp
    a = jnp.exp(log_a)
    th = jnp.tanh(log_a)
    is_t0 = jnp.logical_and(lax.broadcasted_iota(jnp.int32, xb.shape, 0) == 0, at_start)
    mult = jnp.where(is_t0, 1.0, jnp.sqrt(-2.0 * th / (1.0 - th)))
    return dict(shifted=shifted, xc=xc, ra=ra, ia=ia, sp=sp, a=a, mult=mult, is_t0=is_t0, heads=heads)


def _rg_specs(tt, nt, order):
    blk = tt // SUBLANES
    return dict(
        x=pl.BlockSpec((tt, D), lambda i: (order(i), 0)), y=pl.BlockSpec((tt, D), lambda i: (order(i), 1)),
        halo=pl.BlockSpec((SUBLANES, D), lambda i: (jnp.maximum(order(i) * blk - 1, 0), 0)),
        cw=_vec_spec(CONV_WIDTH), vec=_vec_spec(), w=pl.BlockSpec((HEADS, HEAD_DIM, HEAD_DIM), lambda i: (0, 0, 0)))


def _rg_forward(name, z, cw, vec, wa, wx):
    t = z.shape[0]
    tt = _tile(t, ROW_TILE)
    nt = t // tt
    sp = _rg_specs(tt, nt, lambda i: i)

    def body(zx_ref, zy_ref, halo_ref, cw_ref, vec_ref, wa_ref, wx_ref, p_ref, h_ref, a_s, u_s, carry):
        i = pl.program_id(0)

        @pl.when(i == 0)
        def _():
            carry[...] = jnp.zeros_like(carry)

        halo = jnp.where(i > 0, halo_ref[...], 0.0)
        g = _rg_gates(zx_ref[...], halo, cw_ref, vec_ref, wa_ref, wx_ref, i == 0)
        a_s[...] = g["a"]
        u_s[...] = g["mult"] * (g["ia"] * g["xc"])

        def group(gi, h):
            rows = pl.ds(pl.multiple_of(gi * SUBLANES, SUBLANES), SUBLANES)
            a8, u8 = a_s[rows, :], u_s[rows, :]
            out = []
            for j in range(SUBLANES):
                h = a8[j:j + 1, :] * h + u8[j:j + 1, :]
                out.append(h)
            h_ref[rows, :] = jnp.concatenate(out, axis=0)
            return h

        carry[0:1, :] = lax.fori_loop(0, tt // SUBLANES, group, carry[0:1, :])
        p_ref[...] = (h_ref[...] * _gelu(zy_ref[...])).astype(p_ref.dtype)

    return pl.pallas_call(
        body, name=name, grid=(nt,), in_specs=[sp["x"], sp["y"], sp["halo"], sp["cw"], sp["vec"], sp["w"], sp["w"]],
        out_specs=[_rows_spec(tt), _rows_spec(tt)],
        out_shape=[jax.ShapeDtypeStruct((t, D), MXU), jax.ShapeDtypeStruct((t, D), F32)],
        scratch_shapes=[pltpu.VMEM((tt, D), F32), pltpu.VMEM((tt, D), F32), pltpu.VMEM((SUBLANES, D), F32)],
        compiler_params=_params())(z, z, z, cw, vec, wa, wx)


def _rg_backward(name, dp, z, h, cw, vec, wa, wx):
    t = z.shape[0]
    tt = _tile(t, ROW_TILE)
    nt = t // tt
    rev = lambda i: nt - 1 - i
    sp = _rg_specs(tt, nt, rev)
    rows_rev = pl.BlockSpec((tt, D), lambda i: (rev(i), 0))

    def body(dp_ref, zx_ref, zy_ref, halo_ref, h_ref, hhalo_ref, cw_ref, vec_ref, wa_ref, wx_ref,
             dz_ref, dvec_ref, dcw_ref, dwa_ref, dwx_ref, a_s, d_s, carry, nxt):
        i = pl.program_id(0)
        j = rev(i)

        @pl.when(i == 0)
        def _():
            carry[...] = jnp.zeros_like(carry)
            nxt[...] = jnp.zeros_like(nxt)
            dvec_ref[...] = jnp.zeros_like(dvec_ref)
            dcw_ref[...] = jnp.zeros_like(dcw_ref)
            dwa_ref[...] = jnp.zeros_like(dwa_ref)
            dwx_ref[...] = jnp.zeros_like(dwx_ref)

        halo = jnp.where(j > 0, halo_ref[...], 0.0)
        g = _rg_gates(zx_ref[...], halo, cw_ref, vec_ref, wa_ref, wx_ref, j == 0)
        xc, ra, ia, a, mult = g["xc"], g["ra"], g["ia"], g["a"], g["mult"]
        hv, zy, dpv = h_ref[...], zy_ref[...], dp_ref[...]
        dyb = dpv * hv * _gelu_grad(zy)
        a_s[...] = a
        d_s[...] = dpv * _gelu(zy)

        def group(gi, c):
            rows = pl.ds(pl.multiple_of((tt // SUBLANES - 1 - gi) * SUBLANES, SUBLANES), SUBLANES)
            a8, d8 = a_s[rows, :], d_s[rows, :]
            out = [None] * SUBLANES
            for r in reversed(range(SUBLANES)):
                dht = d8[r:r + 1, :] + c
                out[r] = dht
                c = a8[r:r + 1, :] * dht
            d_s[rows, :] = jnp.concatenate(out, axis=0)
            return c

        carry[0:1, :] = lax.fori_loop(0, tt // SUBLANES, group, carry[0:1, :])
        dht = d_s[...]
        hprev = _shift_down(hv, jnp.where(j > 0, hhalo_ref[...], 0.0), 1)
        ixc = ia * xc
        dlog_a = dht * hprev * a + jnp.where(g["is_t0"], 0.0, dht * ixc * (-(a * a) / mult))
        dia = dht * mult * xc
        dxc = dht * mult * ia
        dra = dlog_a * ((-LRU_C) * g["sp"])
        nl = -vec_ref[3:4, :]
        dvec_ref[3:4, :] += _row_sum(dlog_a * ((-LRU_C) * ra)) * (-_sigmoid(nl))
        dpa = dra * ra * (1.0 - ra)
        dpx = dia * ia * (1.0 - ia)
        dvec_ref[1:2, :] += _row_sum(dpa)
        dvec_ref[2:3, :] += _row_sum(dpx)
        back = []
        for hd, s in enumerate(g["heads"]):
            dwa_ref[hd] += _dot_tn(xc[:, s], dpa[:, s])
            dwx_ref[hd] += _dot_tn(xc[:, s], dpx[:, s])
            back.append(_dot_nt(dpa[:, s], wa_ref[hd]) + _dot_nt(dpx[:, s], wx_ref[hd]))
        dxc = dxc + jnp.concatenate(back, axis=1)
        dvec_ref[0:1, :] += _row_sum(dxc)
        dxb = dxc * cw_ref[CONV_WIDTH - 1:CONV_WIDTH, :]
        for k in range(CONV_WIDTH):
            row = CONV_WIDTH - 1 - k
            dcw_ref[row:row + 1, :] += _row_sum(dxc * g["shifted"][k])
            if k:
                dxb = dxb + _shift_up(dxc, nxt[...], k) * cw_ref[row:row + 1, :]
        nxt[...] = dxc[0:SUBLANES, :]
        dz_ref[:, 0:D] = dxb.astype(dz_ref.dtype)
        dz_ref[:, D:2 * D] = dyb.astype(dz_ref.dtype)

    hhalo = pl.BlockSpec((SUBLANES, D), lambda i: (jnp.maximum(rev(i) * (tt // SUBLANES) - 1, 0), 0))
    wacc = pl.BlockSpec((HEADS, HEAD_DIM, HEAD_DIM), lambda i: (0, 0, 0))
    return pl.pallas_call(
        body, name=name, grid=(nt,),
        in_specs=[rows_rev, sp["x"], sp["y"], sp["halo"], rows_rev, hhalo, sp["cw"], sp["vec"], sp["w"], sp["w"]],
        out_specs=[pl.BlockSpec((tt, 2 * D), lambda i: (rev(i), 0)), _vec_spec(), _vec_spec(), wacc, wacc],
        out_shape=[jax.ShapeDtypeStruct((t, 2 * D), MXU), jax.ShapeDtypeStruct((SUBLANES, D), F32),
                   jax.ShapeDtypeStruct((SUBLANES, D), F32), jax.ShapeDtypeStruct((HEADS, HEAD_DIM, HEAD_DIM), F32),
                   jax.ShapeDtypeStruct((HEADS, HEAD_DIM, HEAD_DIM), F32)],
        scratch_shapes=[pltpu.VMEM((tt, D), F32), pltpu.VMEM((tt, D), F32), pltpu.VMEM((SUBLANES, D), F32),
                        pltpu.VMEM((SUBLANES, D), F32)],
        compiler_params=_params())(dp, z, z, z, h, h, cw, vec, wa, wx)


def _hg_chunk(zq, zf, zi, lb):
    c = HG_CHUNK
    q = _silu(zq)
    sg = _sigmoid(zf)
    fg = lb + (1.0 - lb) * sg
    k = 1.0 - fg
    row, col = lax.broadcasted_iota(jnp.int32, (c, c), 0), lax.broadcasted_iota(jnp.int32, (c, c), 1)
    tri, tri_t = (row >= col).astype(F32), (row <= col).astype(F32)
    b = _dot_exact(tri, jnp.log(fg))
    mid, last = b[c // 2 - 1:c // 2, :], b[c - 1:c, :]
    eq = jnp.exp(jnp.minimum(b - mid, EXP_CLAMP))
    ek = jnp.exp(jnp.minimum(mid - b, EXP_CLAMP))
    eb = jnp.exp(b)
    ed = jnp.exp(last - b)
    return dict(q=q, sg=sg, fg=fg, k=k, v=zi, eq=eq, ek=ek, eb=eb, ed=ed, elast=jnp.exp(last), tri=tri, tri_t=tri_t,
                qe=q * eq, ke=k * ek, qb=q * eb, kd=k * ed)


def _hg_specs(tt, order):
    return [pl.BlockSpec((tt, D), lambda i, col=col: (order(i), col)) for col in range(4)]


def _hg_forward(name, z, vec):
    t = z.shape[0]
    tt = _tile(t, ROW_TILE)
    nt, nc = t // tt, tt // HG_CHUNK
    heads = [slice(h * HEAD_DIM, (h + 1) * HEAD_DIM) for h in range(HEADS)]

    def body(zq_ref, zf_ref, zi_ref, zg_ref, vec_ref, p_ref, o_ref, ss_ref, state):
        @pl.when(pl.program_id(0) == 0)
        def _():
            state[...] = jnp.zeros_like(state)

        lb, gain = vec_ref[0:1, :], vec_ref[1:2, :]

        def chunk(ci, carry):
            rows = pl.ds(pl.multiple_of(ci * HG_CHUNK, HG_CHUNK), HG_CHUNK)
            g = _hg_chunk(zq_ref[rows, :], zf_ref[rows, :], zi_ref[rows, :], lb)
            zg = zg_ref[rows, :]
            causal = g["tri"] > 0.0
            o_parts, p_parts = [], []
            for hd, s in enumerate(heads):
                st = state[hd]
                ss_ref[ci, hd] = st
                att = jnp.where(causal, _dot_nt(g["qe"][:, s], g["ke"][:, s]), 0.0)
                o = _dot(att, g["v"][:, s]) + _dot_nt(g["qb"][:, s], st)
                state[hd] = st * g["elast"][:, s] + _dot_tn(g["v"][:, s], g["kd"][:, s])
                r = lax.rsqrt(jnp.mean(o * o, axis=-1, keepdims=True) + GNORM_EPS)
                o_parts.append(o)
                p_parts.append((o * r) * gain[:, s])
            o_ref[rows, :] = jnp.concatenate(o_parts, axis=1)
            p_ref[rows, :] = (jnp.concatenate(p_parts, axis=1) * _silu(zg)).astype(p_ref.dtype)
            return carry

        lax.fori_loop(0, nc, chunk, 0)

    return pl.pallas_call(
        body, name=name, grid=(nt,), in_specs=_hg_specs(tt, lambda i: i) + [_vec_spec()],
        out_specs=[_rows_spec(tt), _rows_spec(tt), pl.BlockSpec((nc, HEADS, HEAD_DIM, HEAD_DIM), lambda i: (i, 0, 0, 0))],
        out_shape=[jax.ShapeDtypeStruct((t, D), MXU), jax.ShapeDtypeStruct((t, D), F32),
                   jax.ShapeDtypeStruct((t // HG_CHUNK, HEADS, HEAD_DIM, HEAD_DIM), F32)],
        scratch_shapes=[pltpu.VMEM((HEADS, HEAD_DIM, HEAD_DIM), F32)], compiler_params=_params())(z, z, z, z, vec)


def _hg_backward(name, dp, z, o, ss, vec):
    t = z.shape[0]
    tt = _tile(t, ROW_TILE)
    nt, nc = t // tt, tt // HG_CHUNK
    rev = lambda i: nt - 1 - i
    heads = [slice(h * HEAD_DIM, (h + 1) * HEAD_DIM) for h in range(HEADS)]
    rows_rev = pl.BlockSpec((tt, D), lambda i: (rev(i), 0))

    def body(dp_ref, zq_ref, zf_ref, zi_ref, zg_ref, o_ref, ss_ref, vec_ref, dz_ref, acc_ref,
             dstate, dqe_s, dke_s, dqb_s, dkd_s, dv_s, dzg_s, dlast_s):
        @pl.when(pl.program_id(0) == 0)
        def _():
            dstate[...] = jnp.zeros_like(dstate)
            acc_ref[...] = jnp.zeros_like(acc_ref)

        lb, gain = vec_ref[0:1, :], vec_ref[1:2, :]

        def chunk(cr, carry):
            ci = nc - 1 - cr
            rows = pl.ds(pl.multiple_of(ci * HG_CHUNK, HG_CHUNK), HG_CHUNK)
            zq, zg = zq_ref[rows, :], zg_ref[rows, :]
            g = _hg_chunk(zq, zf_ref[rows, :], zi_ref[rows, :], lb)
            ov, dpv = o_ref[rows, :], dp_ref[rows, :]
            causal = g["tri"] > 0.0
            don = dpv * _silu(zg)
            dgate = dpv * _silu_grad(zg)
            for hd, s in enumerate(heads):
                oh = ov[:, s]
                r = lax.rsqrt(jnp.mean(oh * oh, axis=-1, keepdims=True) + GNORM_EPS)
                on = oh * r
                dzg_s[:, s] = dgate[:, s] * (on * gain[:, s])
                acc_ref[1:2, s] += _row_sum(don[:, s] * on)
                dtmp = don[:, s] * gain[:, s]
                do = r * (dtmp - on * jnp.mean(dtmp * on, axis=-1, keepdims=True))
                st, dst = ss_ref[ci, hd], dstate[hd]
                qe, ke, qb, kd, v = g["qe"][:, s], g["ke"][:, s], g["qb"][:, s], g["kd"][:, s], g["v"][:, s]
                att = jnp.where(causal, _dot_nt(qe, ke), 0.0)
                datt = jnp.where(causal, _dot_nt(do, v), 0.0)
                dv_s[:, s] = _dot_tn(att, do) + _dot_nt(kd, dst)
                dqe_s[:, s] = _dot(datt, ke)
                dke_s[:, s] = _dot_tn(datt, qe)
                dqb_s[:, s] = _dot(do, st)
                dkd_s[:, s] = _dot(v, dst)
                dlast_s[0:1, s] = g["elast"][:, s] * _row_sum(dst * st)
                dstate[hd] = dst * g["elast"][:, s] + _dot_tn(do, qb)
            dqe, dke, dqb, dkd = dqe_s[...], dke_s[...], dqb_s[...], dkd_s[...]
            dq = dqe * g["eq"] + dqb * g["eb"]
            dk = dke * g["ek"] + dkd * g["ed"]
            dkdk = dkd * g["kd"]
            db = dqe * g["qe"].astype(MXU).astype(F32) - dke * g["ke"].astype(MXU).astype(F32) + dqb * g["qb"] - dkdk
            dlogf = _dot_exact(g["tri_t"], db) + (dlast_s[0:1, :] + _row_sum(dkdk))
            dfg = dlogf / g["fg"] - dk
            sg = g["sg"]
            acc_ref[0:1, :] += _row_sum(dfg * (1.0 - sg))
            dz_ref[rows, 0:D] = (dq * _silu_grad(zq)).astype(dz_ref.dtype)
            dz_ref[rows, D:2 * D] = (dfg * (1.0 - lb) * sg * (1.0 - sg)).astype(dz_ref.dtype)
            dz_ref[rows, 2 * D:3 * D] = dv_s[...].astype(dz_ref.dtype)
            dz_ref[rows, 3 * D:4 * D] = dzg_s[...].astype(dz_ref.dtype)
            return carry

        lax.fori_loop(0, nc, chunk, 0)

    chunk_buf = pltpu.VMEM((HG_CHUNK, D), F32)
    return pl.pallas_call(
        body, name=name, grid=(nt,),
        in_specs=[rows_rev] + _hg_specs(tt, rev) + [rows_rev, pl.BlockSpec((nc, HEADS, HEAD_DIM, HEAD_DIM), lambda i: (rev(i), 0, 0, 0)),
                                                  _vec_spec()],
        out_specs=[pl.BlockSpec((tt, 4 * D), lambda i: (rev(i), 0)), _vec_spec()],
        out_shape=[jax.ShapeDtypeStruct((t, 4 * D), MXU), jax.ShapeDtypeStruct((SUBLANES, D), F32)],
        scratch_shapes=[pltpu.VMEM((HEADS, HEAD_DIM, HEAD_DIM), F32)] + [chunk_buf] * 6 + [pltpu.VMEM((SUBLANES, D), F32)],
        compiler_params=_params())(dp, z, z, z, z, o, ss, vec)


def _mod_forward(name, c_all, mod_w, lower):
    depth, _, sw = mod_w.shape

    def body(c_ref, w_ref, lo_ref, cs_ref, mod_ref, lb_ref):
        cs = _silu(c_ref[...])
        mod_ref[...] = _dot(cs, w_ref[...])

        @pl.when(pl.program_id(0) == 0)
        def _():
            cs_ref[...] = cs
            lo = lo_ref[...]
            e = jnp.exp(lo - jnp.max(lo, axis=0, keepdims=True))
            sm = e / jnp.sum(e, axis=0, keepdims=True)
            lb_ref[0:1, :] = jnp.zeros((1, D), F32)
            for l in range(1, depth):
                lb_ref[l:l + 1, :] = lb_ref[l - 1:l, :] + sm[l:l + 1, :]

    return pl.pallas_call(
        body, name=name, grid=(depth,),
        in_specs=[pl.BlockSpec((N_DEV, D), lambda l: (0, 0)), pl.BlockSpec((None, D, sw), lambda l: (l, 0, 0)),
                  pl.BlockSpec((depth, D), lambda l: (0, 0))],
        out_specs=[pl.BlockSpec((N_DEV, D), lambda l: (0, 0)), pl.BlockSpec((None, N_DEV, sw), lambda l: (l, 0, 0)),
                   pl.BlockSpec((depth, D), lambda l: (0, 0))],
        out_shape=[jax.ShapeDtypeStruct((N_DEV, D), F32), jax.ShapeDtypeStruct((depth, N_DEV, sw), F32),
                   jax.ShapeDtypeStruct((depth, D), F32)], compiler_params=_params())(c_all, mod_w, lower)


def _mod_weight_grad(name, cs_t, dmod):
    depth, pad, sw = dmod.shape

    def body(c_ref, d_ref, g_ref):
        g_ref[...] = _dot(c_ref[...], d_ref[...])

    return pl.pallas_call(
        body, name=name, grid=(depth,),
        in_specs=[pl.BlockSpec((D, pad), lambda l: (0, 0)), pl.BlockSpec((None, pad, sw), lambda l: (l, 0, 0))],
        out_specs=pl.BlockSpec((None, D, sw), lambda l: (l, 0, 0)),
        out_shape=jax.ShapeDtypeStruct((depth, D, sw), F32), compiler_params=_params())(cs_t, dmod)


def _lower_bound_grad(name, lower, dlb):
    depth = lower.shape[0]

    def body(lo_ref, d_ref, out_ref):
        lo = lo_ref[...]
        e = jnp.exp(lo - jnp.max(lo, axis=0, keepdims=True))
        sm = e / jnp.sum(e, axis=0, keepdims=True)
        out_ref[...] = jnp.zeros_like(out_ref)
        dsm = [jnp.zeros((1, D), F32)]
        for l in range(1, depth):
            tail = d_ref[l:l + 1, :]
            for m in range(l + 1, depth):
                tail = tail + d_ref[m:m + 1, :]
            dsm.append(tail)
        inner = sm[1:2, :] * dsm[1]
        for l in range(2, depth):
            inner = inner + sm[l:l + 1, :] * dsm[l]
        for l in range(depth):
            out_ref[l:l + 1, :] = sm[l:l + 1, :] * (dsm[l] - inner)
        for j in range(SUBLANES - depth):
            row = d_ref[depth + j:depth + j + 1, :]
            tot = row[:, 0:HEAD_DIM]
            for hd in range(1, HEADS):
                tot = tot + row[:, hd * HEAD_DIM:(hd + 1) * HEAD_DIM]
            out_ref[depth + j:depth + j + 1, 0:HEAD_DIM] = tot

    return pl.pallas_call(body, name=name, out_shape=jax.ShapeDtypeStruct((SUBLANES, D), F32))(lower, dlb)


def _adamw(name, parts, w, m, v, layer=None, prev=None):
    p, r, c = parts.shape
    tr = _tile(r, max(SUBLANES, ADAMW_STEP_BYTES // (4 * c * (p + 7))))
    stacked = layer is not None
    n_prev = 4 if prev is not None else 0

    def body(*refs):
        parts_ref, w_ref, m_ref, v_ref = refs[:4]
        g_ref, d_ref, m_out, v_out = refs[4 + n_prev:]
        g = parts_ref[0].astype(F32)
        for q in range(1, p):
            g = g + parts_ref[q].astype(F32)
        m2 = ADAM_B1 * m_ref[...] + (1.0 - ADAM_B1) * g
        v2 = ADAM_B2 * v_ref[...] + (1.0 - ADAM_B2) * (g * g)
        m_hat = m2 / (1.0 - ADAM_B1 ** ADAM_STEP)
        v_hat = v2 / (1.0 - ADAM_B2 ** ADAM_STEP)
        g_ref[...] = g
        d_ref[...] = -ADAM_LR * (m_hat / (jnp.sqrt(v_hat) + ADAM_EPS) + ADAM_WD * w_ref[...])
        m_out[...] = m2
        v_out[...] = v2

    if stacked:
        spec = pl.BlockSpec((None, tr, c), lambda i: (layer, i, 0))
    else:
        spec = pl.BlockSpec((tr, c), lambda i: (i, 0))
    return pl.pallas_call(
        body, name=name, grid=(r // tr,),
        in_specs=[pl.BlockSpec((p, tr, c), lambda i: (0, i, 0)), spec, spec, spec] + [ANY] * n_prev, out_specs=[spec] * 4,
        out_shape=[jax.ShapeDtypeStruct(w.shape, F32)] * 4,
        input_output_aliases={4 + q: q for q in range(n_prev)}, compiler_params=_params(),
    )(parts, w, m, v, *(prev or []))


def _hbm(a):
    return pltpu.with_memory_space_constraint(a, pltpu.HBM)


def _vec(*rows):
    rows = [r.reshape(1, D).astype(F32) for r in rows]
    packed = jnp.concatenate(rows + [jnp.zeros((SUBLANES - len(rows), D), F32)], axis=0)
    return _hbm(packed)


def _pad_rows(a, rows=SUBLANES):
    a = a.reshape(-1, a.shape[-1])
    return jnp.concatenate([a, jnp.zeros((rows - a.shape[0], a.shape[1]), a.dtype)], axis=0) if a.shape[0] < rows else a


def kernel(x, c, mod_w, mod_b, norm_mix, norm_mlp, norm_final, rg_w_in, rg_conv_w, rg_conv_b, rg_w_a, rg_b_a, rg_w_x, rg_b_x, rg_lambda, rg_w_out, hg_w_in, hg_lower_bounds, hg_gnorm, hg_w_out, mlp_w1, mlp_w2, loss_target, m_mod_w, m_mod_b, m_norm_mix, m_norm_mlp, m_norm_final, m_rg_w_in, m_rg_conv_w, m_rg_conv_b, m_rg_w_a, m_rg_b_a, m_rg_w_x, m_rg_b_x, m_rg_lambda, m_rg_w_out, m_hg_w_in, m_hg_lower_bounds, m_hg_gnorm, m_hg_w_out, m_mlp_w1, m_mlp_w2, v_mod_w, v_mod_b, v_norm_mix, v_norm_mlp, v_norm_final, v_rg_w_in, v_rg_conv_w, v_rg_conv_b, v_rg_w_a, v_rg_b_a, v_rg_w_x, v_rg_b_x, v_rg_lambda, v_rg_w_out, v_hg_w_in, v_hg_lower_bounds, v_hg_gnorm, v_hg_w_out, v_mlp_w1, v_mlp_w2):
    me = 4 * lax.axis_index("x") + 2 * lax.axis_index("y") + lax.axis_index("c")
    x0 = x[0]
    target = loss_target[0]
    n_rg, n_hg = rg_w_in.shape[0], hg_w_in.shape[0]
    sw_mod = mod_w.shape[2]

    c_all, cw_all = _all_gather([_pad_rows(c), rg_conv_w.reshape(n_rg * CONV_WIDTH, -1)], "gather_cond")
    c_all = c_all[:, 0, :]
    conv_w = cw_all.transpose(1, 0, 2).reshape(n_rg, CONV_WIDTH, D)
    cs_all, mod_part, lb_all = _mod_forward("mod_forward", c_all, mod_w, hg_lower_bounds)
    (mod_gathered,) = _all_gather([mod_part.reshape(DEPTH * N_DEV, sw_mod)], "gather_mod")

    ids = iter(range(4 * DEPTH))
    shards = []
    for layer in range(DEPTH):
        j = layer // 2
        w_in, w_out = (rg_w_in[j], rg_w_out[j]) if layer % 2 == 0 else (hg_w_in[j], hg_w_out[j])
        shards.append([w_in.astype(MXU), w_out.astype(MXU), mlp_w1[layer].astype(MXU), mlp_w2[layer].astype(MXU)])
    shards, mod_gathered = lax.optimization_barrier((shards, mod_gathered))
    weights = []
    for layer in range(DEPTH):
        if layer == 0:
            g_in, g_out = _all_gather_async(shards[0][:2], "gather_mixer_weights_0", next(ids))
            g_w1, g_w2 = _all_gather_async(shards[0][2:], "gather_mlp_weights_0", next(ids))
        else:
            g_in, g_out, g_w1, g_w2 = _all_gather_async(shards[layer], f"gather_weights_{layer}", next(ids))
        weights.append(dict(w_in=g_in, w_out=g_out.reshape(D, D), w1=g_w1, w2=g_w2.reshape(-1, D)))

    mod_mine = lax.dynamic_index_in_dim(mod_gathered.reshape(N_DEV, DEPTH, N_DEV, sw_mod), me, axis=2, keepdims=False)
    mod = mod_mine.transpose(1, 0, 2).reshape(DEPTH, 6, D) + mod_b.reshape(DEPTH, 6, D)

    saved = []
    xl = x0
    for layer in range(DEPTH):
        j, wt = layer // 2, weights[layer]
        is_rg = layer % 2 == 0
        s = dict(x=xl)
        s["vec_mix"] = _vec(norm_mix[layer], mod[layer, 1], mod[layer, 0])
        s["vec_mlp"] = _vec(norm_mlp[layer], mod[layer, 4], mod[layer, 3])
        s["gate_mix"], s["gate_mlp"] = _vec(mod[layer, 2]), _vec(mod[layer, 5])
        s["h"] = _hbm(_norm_mod("norm_mod", xl, s["vec_mix"]))
        if is_rg:
            (s["z"],) = _mm_cols("rg_in", s["h"], wt["w_in"])
            s["cw"] = conv_w[j]
            s["vec"] = _vec(rg_conv_b[j], rg_b_a[j], rg_b_x[j], rg_lambda[j])
            s["wa"], s["wx"] = rg_w_a[j].astype(MXU), rg_w_x[j].astype(MXU)
            s["p"], s["hr"] = _rg_forward("rg_forward", s["z"], s["cw"], s["vec"], s["wa"], s["wx"])
        else:
            (s["z"],) = _mm_cols("hg_in", s["h"], wt["w_in"])
            s["vec"] = _vec(lb_all[layer], jnp.tile(hg_gnorm[j], HEADS))
            s["p"], s["o"], s["ss"] = _hg_forward("hg_forward", s["z"], s["vec"])
        s["y"], s["x1"] = _mm_rows("mix_out", s["p"], wt["w_out"], _ep_residual, (F32, F32), extras=(xl, s["gate_mix"]))
        s["p"] = _hbm(s["p"])
        s["h2"] = _hbm(_norm_mod("norm_mod", s["x1"], s["vec_mlp"]))
        (s["s"],) = _mm_cols("mlp_in", s["h2"], wt["w1"], _ep_relu2, (MXU,))
        s["ff"], xl = _mm_rows("mlp_out", s["s"], wt["w2"], _ep_residual, (F32, F32), extras=(s["x1"], s["gate_mlp"]),
                               rows=MM_TILE, cols=MM_TILE)
        saved.append(s)

    dx, head = _loss_head("loss_head", xl, target, _vec(norm_final))

    results = {}
    big = dict(rg_w_in=(rg_w_in, m_rg_w_in, v_rg_w_in), rg_w_out=(rg_w_out, m_rg_w_out, v_rg_w_out),
               hg_w_in=(hg_w_in, m_hg_w_in, v_hg_w_in), hg_w_out=(hg_w_out, m_hg_w_out, v_hg_w_out),
               mlp_w1=(mlp_w1, m_mlp_w1, v_mlp_w1), mlp_w2=(mlp_w2, m_mlp_w2, v_mlp_w2))

    def update_layer(layer, landed):
        mixer = "rg" if layer % 2 == 0 else "hg"
        l_in, l_out, l_w1, l_w2 = landed
        for nm, parts, idx in ((f"{mixer}_w_in", l_in, layer // 2), (f"{mixer}_w_out", l_out, layer // 2),
                               ("mlp_w1", l_w1, layer), ("mlp_w2", l_w2, layer)):
            w, m, v = big[nm]
            results[nm] = _adamw(f"adamw_{nm}", parts, w, m, v, layer=idx, prev=results.get(nm))

    landed = [None] * DEPTH
    small = [None] * DEPTH
    parity = lax.axis_index("c").astype(jnp.int32).reshape(1)

    def send_chip_sums(layer, grads, from_sibling, anchor):
        sums = _pair_sum("pair_sum", parity, grads, from_sibling)
        sums, anchor = lax.optimization_barrier((sums, anchor))
        landed[layer] = _chip_exchange_async(sums, f"exchange_grads_{layer}", next(ids))
        return anchor

    pending = None
    for layer in reversed(range(DEPTH)):
        j, wt, s = layer // 2, weights[layer], saved[layer]
        is_rg = layer % 2 == 0
        dff, g_mlp = _residual_grad("residual_grad", dx, s["ff"], s["gate_mlp"])
        dff = _hbm(dff)
        da = _mm_rows_t("mlp_out_t", dff, wt["w2"], _ep_relu2_grad, MXU, extras=(s["s"],))
        dw2 = _mm_grad("mlp_out_grad", s["s"], dff, False).reshape(N_DEV, -1, D)
        dw1 = _mm_grad("mlp_in_grad", s["h2"], da, True)
        if pending is not None:
            da = send_chip_sums(*pending, da)
        dh2 = _mm_cols_t("mlp_in_t", da, wt["w1"])
        dx1, n_mlp = _norm_mod_grad("norm_mod_grad", dh2, s["x1"], s["vec_mlp"], dx)
        dyb, g_mix = _residual_grad("residual_grad", dx1, s["y"], s["gate_mix"])
        dyb = _hbm(dyb)
        dp = _mm_rows_t("mix_out_t", dyb, wt["w_out"])
        dw_out = _hbm(_mm_grad("mix_out_grad", s["p"], dyb, False).reshape(N_DEV, -1, D))
        if is_rg:
            dz, dvec, dcw, dwa, dwx = _rg_backward("rg_backward", dp, s["z"], s["hr"], s["cw"], s["vec"], s["wa"], s["wx"])
            gate_grads, dz = lax.optimization_barrier(
                ([dwa.reshape(-1, HEAD_DIM).astype(MXU), dwx.reshape(-1, HEAD_DIM).astype(MXU)], dz))
            gate_parts = _all_gather_async(gate_grads, f"gather_gate_grads_{j}", next(ids))
            mixer_small = dict(dvec=dvec, dcw=dcw, gate_parts=gate_parts)
            dw_in = _mm_grad("rg_in_grad", s["h"], dz, True)
            dh = _mm_cols_t("rg_in_t", dz, wt["w_in"])
        else:
            dz, dvec = _hg_backward("hg_backward", dp, s["z"], s["o"], s["ss"], s["vec"])
            mixer_small = dict(dvec=dvec)
            dw_in = _mm_grad("hg_in_grad", s["h"], dz, True)
            dh = _mm_cols_t("hg_in_t", dz, wt["w_in"])
        grads = [dw_in, dw_out, dw1, dw2]
        pending = (layer, grads, _sibling_send_async(grads, f"pair_grads_{layer}", next(ids)))
        dx, n_mix = _norm_mod_grad("norm_mod_grad", dh, s["x"], s["vec_mix"], dx1)
        small[layer] = dict(g_mlp=g_mlp, n_mlp=n_mlp, g_mix=g_mix, n_mix=n_mix, **mixer_small)
        if layer + 1 < DEPTH:
            landed[layer + 1], dx = lax.optimization_barrier((landed[layer + 1], dx))
            if "gate_parts" in small[layer + 1]:
                small[layer + 1]["gate_parts"], dx = lax.optimization_barrier((small[layer + 1]["gate_parts"], dx))
            update_layer(layer + 1, landed[layer + 1])

    dlb_rows = [jnp.zeros((1, D), F32) if l % 2 == 0 else small[l]["dvec"][0:1] for l in range(DEPTH)]
    dgn_rows = [small[2 * j + 1]["dvec"][1:2] for j in range(n_hg)]
    lb_grad = _lower_bound_grad("lower_bound_grad", hg_lower_bounds, _pad_rows(jnp.concatenate(dlb_rows + dgn_rows, axis=0)))
    dmod = jnp.stack([jnp.concatenate([small[l]["n_mix"][2], small[l]["n_mix"][1], small[l]["g_mix"][0],
                                       small[l]["n_mlp"][2], small[l]["n_mlp"][1], small[l]["g_mlp"][0]]) for l in range(DEPTH)])
    groups = [
        dmod.reshape(DEPTH * 6, D),
        jnp.stack([small[l]["n_mix"][0] for l in range(DEPTH)]),
        jnp.stack([small[l]["n_mlp"][0] for l in range(DEPTH)]),
        head[0:1],
        jnp.stack([small[2 * j]["dvec"][0] for j in range(n_rg)]),
        jnp.stack([small[2 * j]["dvec"][1] for j in range(n_rg)]),
        jnp.stack([small[2 * j]["dvec"][2] for j in range(n_rg)]),
        jnp.stack([small[2 * j]["dvec"][3] for j in range(n_rg)]),
        lb_grad[0:DEPTH],
        jnp.concatenate([lb_grad[DEPTH + j:DEPTH + j + 1, 0:HEAD_DIM] for j in range(n_hg)]
                        + [jnp.zeros((1, D - n_hg * HEAD_DIM), F32)], axis=1),
        jnp.concatenate([small[2 * j]["dcw"][0:CONV_WIDTH] for j in range(n_rg)], axis=0),
        head[1:2],
    ]
    params = [mod_b, norm_mix, norm_mlp, norm_final, rg_conv_b, rg_b_a, rg_b_x, rg_lambda, hg_lower_bounds, hg_gnorm]
    moms = [m_mod_b, m_norm_mix, m_norm_mlp, m_norm_final, m_rg_conv_b, m_rg_b_a, m_rg_b_x, m_rg_lambda, m_hg_lower_bounds, m_hg_gnorm]
    vars_ = [v_mod_b, v_norm_mix, v_norm_mlp, v_norm_final, v_rg_conv_b, v_rg_b_a, v_rg_b_x, v_rg_lambda, v_hg_lower_bounds, v_hg_gnorm]
    offsets, rows_of, at = [], [], 0
    for g in groups:
        offsets.append(at)
        rows_of.append(g.shape[0])
        at += -(-g.shape[0] // SUBLANES) * SUBLANES
    packed = jnp.concatenate([_pad_rows(g, -(-g.shape[0] // SUBLANES) * SUBLANES) for g in groups], axis=0)
    (small_parts,) = _all_gather([packed], "gather_small_grads")
    last_layer, last_grads, from_sibling = pending
    from_sibling, small_parts = lax.optimization_barrier((from_sibling, small_parts))
    send_chip_sums(last_layer, last_grads, from_sibling, dx)

    def pack_like(arrs):
        out = []
        for g_rows, off, a in zip(rows_of, offsets, arrs):
            flat = a.reshape(-1)
            flat = jnp.concatenate([flat, jnp.zeros((g_rows * D - flat.shape[0],), F32)])
            out.append(_pad_rows(flat.reshape(g_rows, D), -(-g_rows // SUBLANES) * SUBLANES))
        rest = packed.shape[0] - sum(o.shape[0] for o in out)
        return jnp.concatenate(out + [jnp.zeros((rest, D), F32)], axis=0)

    small_out = _adamw("adamw_small", small_parts, pack_like(params), pack_like(moms), pack_like(vars_))

    def unpack(q, idx, like):
        rows = small_out[q][offsets[idx]:offsets[idx] + rows_of[idx]]
        return rows.reshape(-1)[:like.size].reshape(like.shape)

    loss = jnp.sum(small_out[0][offsets[11]])
    names =["mod_b", "norm_mix", "norm_mlp", "norm_final", "rg_conv_b", "rg_b_a", "rg_b_x", "rg_lambda", "hg_lower_bounds", "hg_gnorm"]
    for idx, (nm, like) in enumerate(zip(names, params)):
        results[nm] = [unpack(q, idx, like) for q in range(4)]

    cw_parts = lax.dynamic_slice_in_dim(small_parts[:, offsets[10]:offsets[10] + n_rg * CONV_WIDTH, :], me * (D // N_DEV), D // N_DEV, axis=2)
    shp = rg_conv_w.shape
    results["rg_conv_w"] = [o.reshape(shp) for o in _adamw(
        "adamw_conv", cw_parts, rg_conv_w.reshape(-1, shp[-1]), m_rg_conv_w.reshape(-1, shp[-1]), v_rg_conv_w.reshape(-1, shp[-1]))]
    shp = rg_w_a.shape
    stacked = (n_rg, HEADS * HEAD_DIM, HEAD_DIM)
    for nm, which, (w, m, v) in (("rg_w_a", 0, (rg_w_a, m_rg_w_a, v_rg_w_a)), ("rg_w_x", 1, (rg_w_x, m_rg_w_x, v_rg_w_x))):
        out = None
        for j in reversed(range(n_rg)):
            out = _adamw("adamw_gate", small[2 * j]["gate_parts"][which], w.reshape(stacked), m.reshape(stacked),
                         v.reshape(stacked), layer=j, prev=out)
        results[nm] = [o.reshape(shp) for o in out]

    dmod_all = small_parts[:, 0:DEPTH * 6, :].reshape(N_DEV, DEPTH, 6 * D)
    dmod_cols = lax.dynamic_slice_in_dim(dmod_all, me * sw_mod, sw_mod, axis=2).transpose(1, 0, 2)
    pad = HEAD_DIM - N_DEV
    dmod_pad = jnp.concatenate([dmod_cols, jnp.zeros((DEPTH, pad, sw_mod), F32)], axis=1).astype(MXU)
    cs_t = jnp.concatenate([cs_all.T, jnp.zeros((D, pad), F32)], axis=1).astype(MXU)
    g_mod_w = _mod_weight_grad("mod_weight_grad", cs_t, dmod_pad)
    results["mod_w"] = [o.reshape(mod_w.shape) for o in _adamw(
        "adamw_mod", g_mod_w.reshape(1, -1, sw_mod), mod_w.reshape(-1, sw_mod), m_mod_w.reshape(-1, sw_mod), v_mod_w.reshape(-1, sw_mod))]

    update_layer(0, landed[0])

    order = ["mod_w", "mod_b", "norm_mix", "norm_mlp", "norm_final", "rg_w_in", "rg_conv_w", "rg_conv_b", "rg_w_a", "rg_b_a", "rg_w_x",
             "rg_b_x", "rg_lambda", "rg_w_out", "hg_w_in", "hg_lower_bounds", "hg_gnorm", "hg_w_out", "mlp_w1", "mlp_w2"]
    return (loss, dx[None], *[results[n][0] for n in order], *[results[n][1] for n in order],
            *[results[n][2] for n in order], *[results[n][3] for n in order])
```

```python
import functools

import jax
import jax.numpy as jnp
from jax import lax
from jax.experimental import pallas as pl
from jax.experimental.pallas import tpu as pltpu
from jax.experimental.pallas import tpu_sc as plsc

F32 = jnp.float32
MXU = jnp.bfloat16

N_DEV = 8
D = 1024
DEPTH = 4
HEADS = 8
HEAD_DIM = 128
CONV_WIDTH = 4
LRU_C = 8.0
HG_CHUNK = 64
NORM_EPS = 1e-6
GNORM_EPS = 1e-5
ADAM_LR = 0.001
ADAM_B1 = 0.9
ADAM_B2 = 0.999
ADAM_EPS = 1e-08
ADAM_WD = 0.01
ADAM_STEP = 10
GELU_C = 0.7978845608028654
GELU_K = 0.044715
EXP_CLAMP = 80.0
SUBLANES = 8
VMEM_LIMIT = 48 * 1024 * 1024
ROW_TILE = 256
MM_TILE = 1024
MM_TOKENS = 512
ADAMW_STEP_BYTES = 8 * 1024 * 1024

MESH = pl.DeviceIdType.MESH
ANY = pl.BlockSpec(memory_space=pl.ANY)


def _params():
    return pltpu.CompilerParams(vmem_limit_bytes=VMEM_LIMIT)


def _tile(n, target):
    if n <= target:
        return n
    t = target // SUBLANES * SUBLANES
    while n % t:
        t -= SUBLANES
    return t


def _sigmoid(x):
    return 1.0 / (1.0 + jnp.exp(-x))


def _silu(x):
    return x * _sigmoid(x)


def _silu_grad(x):
    s = _sigmoid(x)
    return s * (1.0 + x * (1.0 - s))


def _gelu(y):
    return 0.5 * y * (1.0 + jnp.tanh(GELU_C * (y + GELU_K * y * y * y)))


def _gelu_grad(y):
    th = jnp.tanh(GELU_C * (y + GELU_K * y * y * y))
    return 0.5 * (1.0 + th) + 0.5 * y * (1.0 - th * th) * GELU_C * (1.0 + 3.0 * GELU_K * y * y)


def _dot(a, b):
    return lax.dot_general(a.astype(MXU), b.astype(MXU), (((1,), (0,)), ((), ())), preferred_element_type=F32)


def _dot_nt(a, b):
    return lax.dot_general(a.astype(MXU), b.astype(MXU), (((1,), (1,)), ((), ())), preferred_element_type=F32)


def _dot_tn(a, b):
    return lax.dot_general(a.astype(MXU), b.astype(MXU), (((0,), (0,)), ((), ())), preferred_element_type=F32)


def _dot_exact(tri, x):
    t = tri.astype(MXU)
    hi = x.astype(MXU)
    r1 = x - hi.astype(F32)
    mid = r1.astype(MXU)
    lo = (r1 - mid.astype(F32)).astype(MXU)
    dn = (((1,), (0,)), ((), ()))
    return (lax.dot_general(t, hi, dn, preferred_element_type=F32) + lax.dot_general(t, mid, dn, preferred_element_type=F32)
            + lax.dot_general(t, lo, dn, preferred_element_type=F32))


def _row_sum(v):
    return jnp.sum(v, axis=0, keepdims=True)


def _handshake(partners):
    barrier = pltpu.get_barrier_semaphore()
    for p in partners:
        pl.semaphore_signal(barrier, inc=1, device_id=p, device_id_type=MESH)
    pl.semaphore_wait(barrier, len(partners))


def _gather_body(n, per_array_sems, handshake, layouts):
    def body(*refs):
        ins, outs = refs[:n], refs[n:2 * n]
        send_sems, recv_sems, local_sems = refs[2 * n:]
        x, y, c = lax.axis_index("x"), lax.axis_index("y"), lax.axis_index("c")
        me, sibling = (x, y, c), (x, y, 1 - c)
        chips = [(1 - x, y), (x, 1 - y), (1 - x, 1 - y)]
        if handshake:
            _handshake([sibling] + [(*chip, c) for chip in chips])

        def sem(sems, a, k):
            return sems.at[a, k] if per_array_sems else sems.at[k]

        def slot(a, p):
            block = 4 * p[0] + 2 * p[1] + p[2]
            r, c_ = ins[a].shape
            if layouts[a] == "rows":
                return outs[a].at[pl.ds(block * r, r), :]
            if layouts[a] == "cols":
                return outs[a].at[:, pl.ds(block * c_, c_)]
            return outs[a].at[block]

        def copy(a, k, block, to, src=None):
            return pltpu.make_async_remote_copy(
                src_ref=slot(a, block) if src is None else src, dst_ref=slot(a, block),
                send_sem=sem(send_sems, a, k), recv_sem=sem(recv_sems, a, k), device_id=to, device_id_type=MESH)

        mine = [pltpu.make_async_copy(ins[a], slot(a, me), local_sems.at[a if per_array_sems else 0]) for a in range(n)]
        for cp in mine:
            cp.start()
        first = []
        for a in range(n):
            first.append(copy(a, 0, me, sibling, src=ins[a]))
            first += [copy(a, 1 + j, me, (*chip, c), src=ins[a]) for j, chip in enumerate(chips)]
        for cp in first:
            cp.start()
        passed = []
        for j, chip in enumerate(chips):
            for a in range(n):
                copy(a, 1 + j, (*chip, c), me).wait_recv()
            for a in range(n):
                cp = copy(a, 4 + j, (*chip, c), sibling)
                cp.start()
                passed.append(cp)
        for a in range(n):
            copy(a, 0, sibling, me).wait_recv()
        for j, chip in enumerate(chips):
            for a in range(n):
                copy(a, 4 + j, (*chip, 1 - c), me).wait_recv()
        for cp in first + passed:
            cp.wait_send()
        for cp in mine:
            cp.wait()

    return body


def _sibling_send_body(n, per_array_sems, handshake):
    def body(*refs):
        ins, outs = refs[:n], refs[n:2 * n]
        send_sems, recv_sems, _ = refs[2 * n:]
        x, y, c = lax.axis_index("x"), lax.axis_index("y"), lax.axis_index("c")
        sibling = (x, y, 1 - c)
        if handshake:
            _handshake([sibling])

        def sem(sems, a):
            return sems.at[a, 0] if per_array_sems else sems.at[0]

        copies = [pltpu.make_async_remote_copy(
            src_ref=ins[a].at[2 * q + 1 - c], dst_ref=outs[a].at[q], send_sem=sem(send_sems, a), recv_sem=sem(recv_sems, a),
            device_id=sibling, device_id_type=MESH) for a in range(n) for q in range(4)]
        for cp in copies:
            cp.start()
        for cp in copies:
            cp.wait_recv()
        for cp in copies:
            cp.wait_send()

    return body


def _chip_exchange_body(n, per_array_sems, handshake):
    def body(*refs):
        ins, outs = refs[:n], refs[n:2 * n]
        send_sems, recv_sems, local_sems = refs[2 * n:]
        x, y, c = lax.axis_index("x"), lax.axis_index("y"), lax.axis_index("c")
        my_chip = 2 * x + y
        chips = [(1 - x, y), (x, 1 - y), (1 - x, 1 - y)]
        if handshake:
            _handshake([(*chip, c) for chip in chips])

        def sem(sems, a, k):
            return sems.at[a, k] if per_array_sems else sems.at[k]

        def copy(a, k, landing):
            px, py = chips[k]
            return pltpu.make_async_remote_copy(
                src_ref=ins[a].at[2 * px + py], dst_ref=outs[a].at[landing], send_sem=sem(send_sems, a, k),
                recv_sem=sem(recv_sems, a, k), device_id=(px, py, c), device_id_type=MESH)

        mine = [pltpu.make_async_copy(ins[a].at[my_chip], outs[a].at[my_chip], local_sems.at[a if per_array_sems else 0])
                for a in range(n)]
        for cp in mine:
            cp.start()
        sent = [copy(a, k, my_chip) for a in range(n) for k in range(3)]
        for cp in sent:
            cp.start()
        for a in range(n):
            for k, (px, py) in enumerate(chips):
                copy(a, k, 2 * px + py).wait_recv()
        for cp in sent:
            cp.wait_send()
        for cp in mine:
            cp.wait()

    return body


def _all_gather(arrs, name):
    n = len(arrs)
    return pl.pallas_call(
        _gather_body(n, True, False, ["stack"] * n), name=name, in_specs=[ANY] * n, out_specs=[ANY] * n,
        out_shape=[jax.ShapeDtypeStruct((N_DEV,) + a.shape, a.dtype) for a in arrs],
        scratch_shapes=[pltpu.SemaphoreType.DMA((n, 7)), pltpu.SemaphoreType.DMA((n, 7)), pltpu.SemaphoreType.DMA((n,))],
    )(*arrs)


def _on_sequencer(body, arrs, out_type, name, collective_id):
    return pl.kernel(
        body, name=name, out_type=out_type, mesh=plsc.ScalarSubcoreMesh(axis_name="sequencer", num_cores=1),
        scratch_types=[pltpu.SemaphoreType.DMA((7,)), pltpu.SemaphoreType.DMA((7,)), pltpu.SemaphoreType.DMA((1,))],
        compiler_params=pltpu.CompilerParams(collective_id=collective_id))(*arrs)


def _all_gather_async(arrs, name, collective_id, layouts):
    shape = dict(stack=lambda r, c: (N_DEV, r, c), rows=lambda r, c: (N_DEV * r, c), cols=lambda r, c: (r, N_DEV * c))
    out_type = [jax.ShapeDtypeStruct(shape[lay](*a.shape), a.dtype) for a, lay in zip(arrs, layouts)]
    return _on_sequencer(_gather_body(len(arrs), False, True, layouts), arrs, out_type, name, collective_id)


def _sibling_send_async(arrs, name, collective_id):
    out_type = [jax.ShapeDtypeStruct((N_DEV // 2,) + a.shape[1:], a.dtype) for a in arrs]
    return _on_sequencer(_sibling_send_body(len(arrs), False, True), arrs, out_type, name, collective_id)


def _chip_exchange_async(arrs, name, collective_id):
    out_type = [jax.ShapeDtypeStruct(a.shape, a.dtype) for a in arrs]
    return _on_sequencer(_chip_exchange_body(len(arrs), False, True), arrs, out_type, name, collective_id)


def _pair_sum(name, parity, mine, theirs):
    n = len(mine)

    def body(par_ref, *refs):
        for a in range(n):
            refs[2 * n + a][...] = (refs[a][...].astype(F32) + refs[n + a][...].astype(F32)).astype(refs[2 * n + a].dtype)

    def block(a):
        return (None,) + a.shape[1:]

    grid_spec = pltpu.PrefetchScalarGridSpec(
        num_scalar_prefetch=1, grid=(N_DEV // 2,),
        in_specs=[pl.BlockSpec(block(a), lambda q, par: (2 * q + par[0], 0, 0)) for a in mine]
        + [pl.BlockSpec(block(a), lambda q, par: (q, 0, 0)) for a in theirs],
        out_specs=[pl.BlockSpec(block(a), lambda q, par: (q, 0, 0)) for a in theirs])
    return pl.pallas_call(body, name=name, grid_spec=grid_spec, out_shape=[jax.ShapeDtypeStruct(a.shape, a.dtype) for a in theirs],
                          compiler_params=_params())(parity, *mine, *theirs)


NN = (((1,), (0,)), ((), ()))
NT = (((1,), (1,)), ((), ()))
TN = (((0,), (0,)), ((), ()))


def _matmul(name, a, b, dims, grid, a_spec, b_spec, outs, epilogue, extras=(), prologue=None):
    n_in = 2 + len(extras)
    n_out = len(outs)

    def body(*refs):
        a_ref, b_ref = refs[0], refs[1]
        ex, out_refs = refs[2:n_in], refs[n_in:n_in + n_out]
        a_tile = a_ref[...] if prologue is None else prologue(a_ref[...])
        epilogue(lax.dot_general(a_tile, b_ref[...], dims, preferred_element_type=F32), ex, out_refs)

    return pl.pallas_call(
        body, name=name, grid=grid, in_specs=[a_spec, b_spec] + [s for _, s in extras], out_specs=[s for _, _, s in outs],
        out_shape=[jax.ShapeDtypeStruct(sh, dt) for sh, dt, _ in outs], compiler_params=_params(),
    )(a, b, *[e for e, _ in extras])


def _square(tile):
    return tile * tile


def _ep_store(acc, ex, outs):
    outs[0][...] = acc.astype(outs[0].dtype)


def _ep_residual(acc, ex, outs):
    outs[0][...] = acc
    outs[1][...] = ex[0][...] + ex[1][0:1, :] * acc


def _ep_relu(acc, ex, outs):
    outs[0][...] = jnp.maximum(acc, 0.0).astype(outs[0].dtype)


def _ep_relu2_grad(acc, ex, outs):
    outs[0][...] = (acc * (2.0 * ex[0][...].astype(F32))).astype(outs[0].dtype)


def _mm_tokens(name, a, w, transposed, out_dtypes, epilogue=_ep_store, extras=(), prologue=None, rows=MM_TOKENS):
    m, n = a.shape[0], w.shape[0 if transposed else 1]
    tm = _tile(m, rows)
    rows_spec = lambda width: pl.BlockSpec((tm, width), lambda i, j, kk: (i, 0))
    whole = lambda arr: pl.BlockSpec(arr.shape, lambda i, j, kk: (0, 0))
    return _matmul(name, a, w, NT if transposed else NN, (m // tm, 1, 1), rows_spec(a.shape[1]), whole(w),
                   [((m, n), dt, rows_spec(n)) for dt in out_dtypes], epilogue,
                   extras=[(e, rows_spec(n) if e.shape[0] == m else whole(e)) for e in extras], prologue=prologue)


def _mm_grad(name, a, b, shard_cols, prologue=None):
    t, m = a.shape
    n = b.shape[1]
    tm = _tile(m, MM_TILE)
    if shard_cols:
        tn = n // N_DEV
        out = ((N_DEV, m, tn), MXU, pl.BlockSpec((None, tm, tn), lambda i, j, kk: (j, i, 0)))
    else:
        tn = _tile(n, MM_TILE)
        out = ((m, n), MXU, pl.BlockSpec((tm, tn), lambda i, j, kk: (i, j)))
    return _matmul(name, a, b, TN, (m // tm, n // tn, 1), pl.BlockSpec((t, tm), lambda i, j, kk: (0, i)),
                   pl.BlockSpec((t, tn), lambda i, j, kk: (0, j)), [out], _ep_store, prologue=prologue)[0]


def _rows_spec(tt, width=D):
    return pl.BlockSpec((tt, width), lambda i: (i, 0))


def _vec_spec(rows=SUBLANES, width=D):
    return pl.BlockSpec((rows, width), lambda i: (0, 0))


def _norm_mod(name, x, vec):
    t = x.shape[0]
    tt = _tile(t, ROW_TILE)

    def body(x_ref, v_ref, h_ref):
        xv = x_ref[...]
        r = lax.rsqrt(jnp.mean(xv * xv, axis=-1, keepdims=True) + NORM_EPS)
        h = (xv * r) * v_ref[0:1, :]
        h_ref[...] = (h * (1.0 + v_ref[1:2, :]) + v_ref[2:3, :]).astype(h_ref.dtype)

    return pl.pallas_call(body, name=name, grid=(t // tt,), in_specs=[_rows_spec(tt), _vec_spec()], out_specs=_rows_spec(tt),
                          out_shape=jax.ShapeDtypeStruct((t, D), MXU))(x, vec)


def _norm_mod_grad(name, dh, x, vec, dres):
    t = x.shape[0]
    tt = _tile(t, ROW_TILE)
    nt = t // tt

    def body(dh_ref, x_ref, v_ref, dres_ref, dx_ref, acc_ref):
        i = pl.program_id(0)

        @pl.when(i == 0)
        def _():
            acc_ref[...] = jnp.zeros_like(acc_ref)

        xv, dhv = x_ref[...], dh_ref[...]
        r = lax.rsqrt(jnp.mean(xv * xv, axis=-1, keepdims=True) + NORM_EPS)
        xn = xv * r
        w = v_ref[0:1, :] * (1.0 + v_ref[1:2, :])
        acc_ref[0:1, :] += _row_sum(dhv * xn)
        acc_ref[2:3, :] += _row_sum(dhv)
        dxn = dhv * w
        dx_ref[...] = dres_ref[...] + r * (dxn - xn * jnp.mean(dxn * xn, axis=-1, keepdims=True))

        @pl.when(i == nt - 1)
        def _():
            dw = acc_ref[0:1, :]
            acc_ref[1:2, :] = dw * v_ref[0:1, :]
            acc_ref[0:1, :] = dw * (1.0 + v_ref[1:2, :])

    return pl.pallas_call(
        body, name=name, grid=(nt,), in_specs=[_rows_spec(tt), _rows_spec(tt), _vec_spec(), _rows_spec(tt)],
        out_specs=[_rows_spec(tt), _vec_spec()],
        out_shape=[jax.ShapeDtypeStruct((t, D), F32), jax.ShapeDtypeStruct((SUBLANES, D), F32)])(dh, x, vec, dres)


def _residual_grad(name, dxo, y, vec):
    t = y.shape[0]
    tt = _tile(t, ROW_TILE)

    def body(dx_ref, y_ref, v_ref, dy_ref, acc_ref):
        @pl.when(pl.program_id(0) == 0)
        def _():
            acc_ref[...] = jnp.zeros_like(acc_ref)

        dxv = dx_ref[...]
        dy_ref[...] = (dxv * v_ref[0:1, :]).astype(dy_ref.dtype)
        acc_ref[0:1, :] += _row_sum(dxv * y_ref[...])

    return pl.pallas_call(
        body, name=name, grid=(t // tt,), in_specs=[_rows_spec(tt), _rows_spec(tt), _vec_spec()],
        out_specs=[_rows_spec(tt), _vec_spec()],
        out_shape=[jax.ShapeDtypeStruct((t, D), MXU), jax.ShapeDtypeStruct((SUBLANES, D), F32)])(dxo, y, vec)


def _loss_head(name, x, target, vec):
    t = x.shape[0]
    tt = _tile(t, ROW_TILE)

    def body(x_ref, t_ref, v_ref, dx_ref, acc_ref):
        @pl.when(pl.program_id(0) == 0)
        def _():
            acc_ref[...] = jnp.zeros_like(acc_ref)

        xv = x_ref[...]
        r = lax.rsqrt(jnp.mean(xv * xv, axis=-1, keepdims=True) + NORM_EPS)
        xn = xv * r
        gain = v_ref[0:1, :]
        err = xn * gain - t_ref[...]
        acc_ref[1:2, :] += _row_sum(err * err) * (0.5 / D)
        dout = err * (1.0 / D)
        acc_ref[0:1, :] += _row_sum(dout * xn)
        dxn = dout * gain
        dx_ref[...] = r * (dxn - xn * jnp.mean(dxn * xn, axis=-1, keepdims=True))

    return pl.pallas_call(
        body, name=name, grid=(t // tt,), in_specs=[_rows_spec(tt), _rows_spec(tt), _vec_spec()],
        out_specs=[_rows_spec(tt), _vec_spec()],
        out_shape=[jax.ShapeDtypeStruct((t, D), F32), jax.ShapeDtypeStruct((SUBLANES, D), F32)])(x, target, vec)


def _shift_down(x, halo, k):
    y = pltpu.roll(x, k, 0)
    top = jnp.where(lax.broadcasted_iota(jnp.int32, halo.shape, 0) < k, pltpu.roll(halo, k, 0), y[0:SUBLANES, :])
    return jnp.concatenate([top, y[SUBLANES:, :]], axis=0)


def _shift_up(x, halo, k):
    n = x.shape[0]
    y = pltpu.roll(x, n - k, 0)
    bottom = jnp.where(lax.broadcasted_iota(jnp.int32, halo.shape, 0) >= SUBLANES - k, pltpu.roll(halo, SUBLANES - k, 0),
                       y[n - SUBLANES:, :])
    return jnp.concatenate([y[:n - SUBLANES, :], bottom], axis=0)


def _rg_gates(xb, halo, cw_ref, vec_ref, wa_ref, wx_ref, at_start):
    shifted = [xb] + [_shift_down(xb, halo, k) for k in range(1, CONV_WIDTH)]
    xc = vec_ref[0:1, :] + shifted[0] * cw_ref[CONV_WIDTH - 1:CONV_WIDTH, :]
    for k in range(1, CONV_WIDTH):
        xc = xc + shifted[k] * cw_ref[CONV_WIDTH - 1 - k:CONV_WIDTH - k, :]
    heads = [slice(h * HEAD_DIM, (h + 1) * HEAD_DIM) for h in range(HEADS)]
    pa = jnp.concatenate([_dot(xc[:, s], wa_ref[h]) for h, s in enumerate(heads)], axis=1) + vec_ref[1:2, :]
    px = jnp.concatenate([_dot(xc[:, s], wx_ref[h]) for h, s in enumerate(heads)], axis=1) + vec_ref[2:3, :]
    ra, ia = _sigmoid(pa), _sigmoid(px)
    nl = -vec_ref[3:4, :]
    sp = jnp.maximum(nl, 0.0) + jnp.log(1.0 + jnp.exp(-jnp.abs(nl)))
    log_a = (-LRU_C) * ra * sp
    a = jnp.exp(log_a)
    th = jnp.tanh(log_a)
    is_t0 = jnp.logical_and(lax.broadcasted_iota(jnp.int32, xb.shape, 0) == 0, at_start)
    mult = jnp.where(is_t0, 1.0, jnp.sqrt(-2.0 * th / (1.0 - th)))
    return dict(shifted=shifted, xc=xc, ra=ra, ia=ia, sp=sp, a=a, mult=mult, is_t0=is_t0, heads=heads)


def _rg_specs(tt, nt, order):
    blk = tt // SUBLANES
    return dict(
        x=pl.BlockSpec((tt, D), lambda i: (order(i), 0)), y=pl.BlockSpec((tt, D), lambda i: (order(i), 1)),
        halo=pl.BlockSpec((SUBLANES, D), lambda i: (jnp.maximum(order(i) * blk - 1, 0), 0)),
        cw=_vec_spec(CONV_WIDTH), vec=_vec_spec(), w=pl.BlockSpec((HEADS, HEAD_DIM, HEAD_DIM), lambda i: (0, 0, 0)))


def _rg_forward(name, z, cw, vec, wa, wx):
    t = z.shape[0]
    tt = _tile(t, ROW_TILE)
    nt = t // tt
    sp = _rg_specs(tt, nt, lambda i: i)

    def body(zx_ref, zy_ref, halo_ref, cw_ref, vec_ref, wa_ref, wx_ref, p_ref, h_ref, a_s, u_s, carry):
        i = pl.program_id(0)

        @pl.when(i == 0)
        def _():
            carry[...] = jnp.zeros_like(carry)

        halo = jnp.where(i > 0, halo_ref[...], 0.0)
        g = _rg_gates(zx_ref[...], halo, cw_ref, vec_ref, wa_ref, wx_ref, i == 0)
        a_s[...] = g["a"]
        u_s[...] = g["mult"] * (g["ia"] * g["xc"])

        def group(gi, h):
            rows = pl.ds(pl.multiple_of(gi * SUBLANES, SUBLANES), SUBLANES)
            a8, u8 = a_s[rows, :], u_s[rows, :]
            out = []
            for j in range(SUBLANES):
                h = a8[j:j + 1, :] * h + u8[j:j + 1, :]
                out.append(h)
            h_ref[rows, :] = jnp.concatenate(out, axis=0)
            return h

        carry[0:1, :] = lax.fori_loop(0, tt // SUBLANES, group, carry[0:1, :])
        p_ref[...] = (h_ref[...] * _gelu(zy_ref[...])).astype(p_ref.dtype)

    return pl.pallas_call(
        body, name=name, grid=(nt,), in_specs=[sp["x"], sp["y"], sp["halo"], sp["cw"], sp["vec"], sp["w"], sp["w"]],
        out_specs=[_rows_spec(tt), _rows_spec(tt)],
        out_shape=[jax.ShapeDtypeStruct((t, D), MXU), jax.ShapeDtypeStruct((t, D), F32)],
        scratch_shapes=[pltpu.VMEM((tt, D), F32), pltpu.VMEM((tt, D), F32), pltpu.VMEM((SUBLANES, D), F32)],
        compiler_params=_params())(z, z, z, cw, vec, wa, wx)


def _rg_backward(name, dp, z, h, cw, vec, wa, wx):
    t = z.shape[0]
    tt = _tile(t, ROW_TILE)
    nt = t // tt
    rev = lambda i: nt - 1 - i
    sp = _rg_specs(tt, nt, rev)
    rows_rev = pl.BlockSpec((tt, D), lambda i: (rev(i), 0))

    def body(dp_ref, zx_ref, zy_ref, halo_ref, h_ref, hhalo_ref, cw_ref, vec_ref, wa_ref, wx_ref,
             dz_ref, dvec_ref, dcw_ref, dwa_ref, dwx_ref, a_s, d_s, carry, nxt):
        i = pl.program_id(0)
        j = rev(i)

        @pl.when(i == 0)
        def _():
            carry[...] = jnp.zeros_like(carry)
            nxt[...] = jnp.zeros_like(nxt)
            dvec_ref[...] = jnp.zeros_like(dvec_ref)
            dcw_ref[...] = jnp.zeros_like(dcw_ref)
            dwa_ref[...] = jnp.zeros_like(dwa_ref)
            dwx_ref[...] = jnp.zeros_like(dwx_ref)

        halo = jnp.where(j > 0, halo_ref[...], 0.0)
        g = _rg_gates(zx_ref[...], halo, cw_ref, vec_ref, wa_ref, wx_ref, j == 0)
        xc, ra, ia, a, mult = g["xc"], g["ra"], g["ia"], g["a"], g["mult"]
        hv, zy, dpv = h_ref[...], zy_ref[...], dp_ref[...]
        dyb = dpv * hv * _gelu_grad(zy)
        a_s[...] = a
        d_s[...] = dpv * _gelu(zy)

        def group(gi, c):
            rows = pl.ds(pl.multiple_of((tt // SUBLANES - 1 - gi) * SUBLANES, SUBLANES), SUBLANES)
            a8, d8 = a_s[rows, :], d_s[rows, :]
            out = [None] * SUBLANES
            for r in reversed(range(SUBLANES)):
                dht = d8[r:r + 1, :] + c
                out[r] = dht
                c = a8[r:r + 1, :] * dht
            d_s[rows, :] = jnp.concatenate(out, axis=0)
            return c

        carry[0:1, :] = lax.fori_loop(0, tt // SUBLANES, group, carry[0:1, :])
        dht = d_s[...]
        hprev = _shift_down(hv, jnp.where(j > 0, hhalo_ref[...], 0.0), 1)
        ixc = ia * xc
        dlog_a = dht * hprev * a + jnp.where(g["is_t0"], 0.0, dht * ixc * (-(a * a) / mult))
        dia = dht * mult * xc
        dxc = dht * mult * ia
        dra = dlog_a * ((-LRU_C) * g["sp"])
        nl = -vec_ref[3:4, :]
        dvec_ref[3:4, :] += _row_sum(dlog_a * ((-LRU_C) * ra)) * (-_sigmoid(nl))
        dpa = dra * ra * (1.0 - ra)
        dpx = dia * ia * (1.0 - ia)
        dvec_ref[1:2, :] += _row_sum(dpa)
        dvec_ref[2:3, :] += _row_sum(dpx)
        back = []
        for hd, s in enumerate(g["heads"]):
            dwa_ref[hd] += _dot_tn(xc[:, s], dpa[:, s])
            dwx_ref[hd] += _dot_tn(xc[:, s], dpx[:, s])
            back.append(_dot_nt(dpa[:, s], wa_ref[hd]) + _dot_nt(dpx[:, s], wx_ref[hd]))
        dxc = dxc + jnp.concatenate(back, axis=1)
        dvec_ref[0:1, :] += _row_sum(dxc)
        dxb = dxc * cw_ref[CONV_WIDTH - 1:CONV_WIDTH, :]
        for k in range(CONV_WIDTH):
            row = CONV_WIDTH - 1 - k
            dcw_ref[row:row + 1, :] += _row_sum(dxc * g["shifted"][k])
            if k:
                dxb = dxb + _shift_up(dxc, nxt[...], k) * cw_ref[row:row + 1, :]
        nxt[...] = dxc[0:SUBLANES, :]
        dz_ref[:, 0:D] = dxb.astype(dz_ref.dtype)
        dz_ref[:, D:2 * D] = dyb.astype(dz_ref.dtype)

    hhalo = pl.BlockSpec((SUBLANES, D), lambda i: (jnp.maximum(rev(i) * (tt // SUBLANES) - 1, 0), 0))
    wacc = pl.BlockSpec((HEADS, HEAD_DIM, HEAD_DIM), lambda i: (0, 0, 0))
    return pl.pallas_call(
        body, name=name, grid=(nt,),
        in_specs=[rows_rev, sp["x"], sp["y"], sp["halo"], rows_rev, hhalo, sp["cw"], sp["vec"], sp["w"], sp["w"]],
        out_specs=[pl.BlockSpec((tt, 2 * D), lambda i: (rev(i), 0)), _vec_spec(), _vec_spec(), wacc, wacc],
        out_shape=[jax.ShapeDtypeStruct((t, 2 * D), MXU), jax.ShapeDtypeStruct((SUBLANES, D), F32),
                   jax.ShapeDtypeStruct((SUBLANES, D), F32), jax.ShapeDtypeStruct((HEADS, HEAD_DIM, HEAD_DIM), F32),
                   jax.ShapeDtypeStruct((HEADS, HEAD_DIM, HEAD_DIM), F32)],
        scratch_shapes=[pltpu.VMEM((tt, D), F32), pltpu.VMEM((tt, D), F32), pltpu.VMEM((SUBLANES, D), F32),
                        pltpu.VMEM((SUBLANES, D), F32)],
        compiler_params=_params())(dp, z, z, z, h, h, cw, vec, wa, wx)


def _hg_chunk(zq, zf, zi, lb):
    c = HG_CHUNK
    q = _silu(zq)
    sg = _sigmoid(zf)
    fg = lb + (1.0 - lb) * sg
    k = 1.0 - fg
    row, col = lax.broadcasted_iota(jnp.int32, (c, c), 0), lax.broadcasted_iota(jnp.int32, (c, c), 1)
    tri, tri_t = (row >= col).astype(F32), (row <= col).astype(F32)
    b = _dot_exact(tri, jnp.log(fg))
    mid, last = b[c // 2 - 1:c // 2, :], b[c - 1:c, :]
    eq = jnp.exp(jnp.minimum(b - mid, EXP_CLAMP))
    ek = jnp.exp(jnp.minimum(mid - b, EXP_CLAMP))
    eb = jnp.exp(b)
    ed = jnp.exp(last - b)
    return dict(q=q, sg=sg, fg=fg, k=k, v=zi, eq=eq, ek=ek, eb=eb, ed=ed, elast=jnp.exp(last), tri=tri, tri_t=tri_t,
                qe=q * eq, ke=k * ek, qb=q * eb, kd=k * ed)


def _hg_specs(tt, order):
    return [pl.BlockSpec((tt, D), lambda i, col=col: (order(i), col)) for col in range(4)]


def _hg_forward(name, z, vec):
    t = z.shape[0]
    tt = _tile(t, ROW_TILE)
    nt, nc = t // tt, tt // HG_CHUNK
    heads = [slice(h * HEAD_DIM, (h + 1) * HEAD_DIM) for h in range(HEADS)]

    def body(zq_ref, zf_ref, zi_ref, zg_ref, vec_ref, p_ref, o_ref, ss_ref, state):
        @pl.when(pl.program_id(0) == 0)
        def _():
            state[...] = jnp.zeros_like(state)

        lb, gain = vec_ref[0:1, :], vec_ref[1:2, :]

        def chunk(ci, carry):
            rows = pl.ds(pl.multiple_of(ci * HG_CHUNK, HG_CHUNK), HG_CHUNK)
            g = _hg_chunk(zq_ref[rows, :], zf_ref[rows, :], zi_ref[rows, :], lb)
            zg = zg_ref[rows, :]
            causal = g["tri"] > 0.0
            o_parts, p_parts = [], []
            for hd, s in enumerate(heads):
                st = state[hd]
                ss_ref[ci, hd] = st
                att = jnp.where(causal, _dot_nt(g["qe"][:, s], g["ke"][:, s]), 0.0)
                o = _dot(att, g["v"][:, s]) + _dot_nt(g["qb"][:, s], st)
                state[hd] = st * g["elast"][:, s] + _dot_tn(g["v"][:, s], g["kd"][:, s])
                r = lax.rsqrt(jnp.mean(o * o, axis=-1, keepdims=True) + GNORM_EPS)
                o_parts.append(o)
                p_parts.append((o * r) * gain[:, s])
            o_ref[rows, :] = jnp.concatenate(o_parts, axis=1)
            p_ref[rows, :] = (jnp.concatenate(p_parts, axis=1) * _silu(zg)).astype(p_ref.dtype)
            return carry

        lax.fori_loop(0, nc, chunk, 0)

    return pl.pallas_call(
        body, name=name, grid=(nt,), in_specs=_hg_specs(tt, lambda i: i) + [_vec_spec()],
        out_specs=[_rows_spec(tt), _rows_spec(tt), pl.BlockSpec((nc, HEADS, HEAD_DIM, HEAD_DIM), lambda i: (i, 0, 0, 0))],
        out_shape=[jax.ShapeDtypeStruct((t, D), MXU), jax.ShapeDtypeStruct((t, D), F32),
                   jax.ShapeDtypeStruct((t // HG_CHUNK, HEADS, HEAD_DIM, HEAD_DIM), F32)],
        scratch_shapes=[pltpu.VMEM((HEADS, HEAD_DIM, HEAD_DIM), F32)], compiler_params=_params())(z, z, z, z, vec)


def _hg_backward(name, dp, z, o, ss, vec):
    t = z.shape[0]
    tt = _tile(t, ROW_TILE)
    nt, nc = t // tt, tt // HG_CHUNK
    rev = lambda i: nt - 1 - i
    heads = [slice(h * HEAD_DIM, (h + 1) * HEAD_DIM) for h in range(HEADS)]
    rows_rev = pl.BlockSpec((tt, D), lambda i: (rev(i), 0))

    def body(dp_ref, zq_ref, zf_ref, zi_ref, zg_ref, o_ref, ss_ref, vec_ref, dz_ref, acc_ref,
             dstate, dqe_s, dke_s, dqb_s, dkd_s, dv_s, dzg_s, dlast_s):
        @pl.when(pl.program_id(0) == 0)
        def _():
            dstate[...] = jnp.zeros_like(dstate)
            acc_ref[...] = jnp.zeros_like(acc_ref)

        lb, gain = vec_ref[0:1, :], vec_ref[1:2, :]

        def chunk(cr, carry):
            ci = nc - 1 - cr
            rows = pl.ds(pl.multiple_of(ci * HG_CHUNK, HG_CHUNK), HG_CHUNK)
            zq, zg = zq_ref[rows, :], zg_ref[rows, :]
            g = _hg_chunk(zq, zf_ref[rows, :], zi_ref[rows, :], lb)
            ov, dpv = o_ref[rows, :], dp_ref[rows, :]
            causal = g["tri"] > 0.0
            don = dpv * _silu(zg)
            dgate = dpv * _silu_grad(zg)
            for hd, s in enumerate(heads):
                oh = ov[:, s]
                r = lax.rsqrt(jnp.mean(oh * oh, axis=-1, keepdims=True) + GNORM_EPS)
                on = oh * r
                dzg_s[:, s] = dgate[:, s] * (on * gain[:, s])
                acc_ref[1:2, s] += _row_sum(don[:, s] * on)
                dtmp = don[:, s] * gain[:, s]
                do = r * (dtmp - on * jnp.mean(dtmp * on, axis=-1, keepdims=True))
                st, dst = ss_ref[ci, hd], dstate[hd]
                qe, ke, qb, kd, v = g["qe"][:, s], g["ke"][:, s], g["qb"][:, s], g["kd"][:, s], g["v"][:, s]
                att = jnp.where(causal, _dot_nt(qe, ke), 0.0)
                datt = jnp.where(causal, _dot_nt(do, v), 0.0)
                dv_s[:, s] = _dot_tn(att, do) + _dot_nt(kd, dst)
                dqe_s[:, s] = _dot(datt, ke)
                dke_s[:, s] = _dot_tn(datt, qe)
                dqb_s[:, s] = _dot(do, st)
                dkd_s[:, s] = _dot(v, dst)
                dlast_s[0:1, s] = g["elast"][:, s] * _row_sum(dst * st)
                dstate[hd] = dst * g["elast"][:, s] + _dot_tn(do, qb)
            dqe, dke, dqb, dkd = dqe_s[...], dke_s[...], dqb_s[...], dkd_s[...]
            dq = dqe * g["eq"] + dqb * g["eb"]
            dk = dke * g["ek"] + dkd * g["ed"]
            dkdk = dkd * g["kd"]
            db = dqe * g["qe"].astype(MXU).astype(F32) - dke * g["ke"].astype(MXU).astype(F32) + dqb * g["qb"] - dkdk
            dlogf = _dot_exact(g["tri_t"], db) + (dlast_s[0:1, :] + _row_sum(dkdk))
            dfg = dlogf / g["fg"] - dk
            sg = g["sg"]
            acc_ref[0:1, :] += _row_sum(dfg * (1.0 - sg))
            dz_ref[rows, 0:D] = (dq * _silu_grad(zq)).astype(dz_ref.dtype)
            dz_ref[rows, D:2 * D] = (dfg * (1.0 - lb) * sg * (1.0 - sg)).astype(dz_ref.dtype)
            dz_ref[rows, 2 * D:3 * D] = dv_s[...].astype(dz_ref.dtype)
            dz_ref[rows, 3 * D:4 * D] = dzg_s[...].astype(dz_ref.dtype)
            return carry

        lax.fori_loop(0, nc, chunk, 0)

    chunk_buf = pltpu.VMEM((HG_CHUNK, D), F32)
    return pl.pallas_call(
        body, name=name, grid=(nt,),
        in_specs=[rows_rev] + _hg_specs(tt, rev) + [rows_rev, pl.BlockSpec((nc, HEADS, HEAD_DIM, HEAD_DIM), lambda i: (rev(i), 0, 0, 0)),
                                                  _vec_spec()],
        out_specs=[pl.BlockSpec((tt, 4 * D), lambda i: (rev(i), 0)), _vec_spec()],
        out_shape=[jax.ShapeDtypeStruct((t, 4 * D), MXU), jax.ShapeDtypeStruct((SUBLANES, D), F32)],
        scratch_shapes=[pltpu.VMEM((HEADS, HEAD_DIM, HEAD_DIM), F32)] + [chunk_buf] * 6 + [pltpu.VMEM((SUBLANES, D), F32)],
        compiler_params=_params())(dp, z, z, z, z, o, ss, vec)


def _mod_forward(name, c_all, mod_w, lower):
    depth, _, sw = mod_w.shape

    def body(c_ref, w_ref, lo_ref, cs_ref, mod_ref, lb_ref):
        cs = _silu(c_ref[...])
        mod_ref[...] = _dot(cs, w_ref[...])

        @pl.when(pl.program_id(0) == 0)
        def _():
            cs_ref[...] = cs
            lo = lo_ref[...]
            e = jnp.exp(lo - jnp.max(lo, axis=0, keepdims=True))
            sm = e / jnp.sum(e, axis=0, keepdims=True)
            lb_ref[0:1, :] = jnp.zeros((1, D), F32)
            for l in range(1, depth):
                lb_ref[l:l + 1, :] = lb_ref[l - 1:l, :] + sm[l:l + 1, :]

    return pl.pallas_call(
        body, name=name, grid=(depth,),
        in_specs=[pl.BlockSpec((N_DEV, D), lambda l: (0, 0)), pl.BlockSpec((None, D, sw), lambda l: (l, 0, 0)),
                  pl.BlockSpec((depth, D), lambda l: (0, 0))],
        out_specs=[pl.BlockSpec((N_DEV, D), lambda l: (0, 0)), pl.BlockSpec((None, N_DEV, sw), lambda l: (l, 0, 0)),
                   pl.BlockSpec((depth, D), lambda l: (0, 0))],
        out_shape=[jax.ShapeDtypeStruct((N_DEV, D), F32), jax.ShapeDtypeStruct((depth, N_DEV, sw), F32),
                   jax.ShapeDtypeStruct((depth, D), F32)], compiler_params=_params())(c_all, mod_w, lower)


def _mod_weight_grad(name, cs_t, dmod):
    depth, pad, sw = dmod.shape

    def body(c_ref, d_ref, g_ref):
        g_ref[...] = _dot(c_ref[...], d_ref[...])

    return pl.pallas_call(
        body, name=name, grid=(depth,),
        in_specs=[pl.BlockSpec((D, pad), lambda l: (0, 0)), pl.BlockSpec((None, pad, sw), lambda l: (l, 0, 0))],
        out_specs=pl.BlockSpec((None, D, sw), lambda l: (l, 0, 0)),
        out_shape=jax.ShapeDtypeStruct((depth, D, sw), F32), compiler_params=_params())(cs_t, dmod)


def _lower_bound_grad(name, lower, dlb):
    depth = lower.shape[0]

    def body(lo_ref, d_ref, out_ref):
        lo = lo_ref[...]
        e = jnp.exp(lo - jnp.max(lo, axis=0, keepdims=True))
        sm = e / jnp.sum(e, axis=0, keepdims=True)
        out_ref[...] = jnp.zeros_like(out_ref)
        dsm = [jnp.zeros((1, D), F32)]
        for l in range(1, depth):
            tail = d_ref[l:l + 1, :]
            for m in range(l + 1, depth):
                tail = tail + d_ref[m:m + 1, :]
            dsm.append(tail)
        inner = sm[1:2, :] * dsm[1]
        for l in range(2, depth):
            inner = inner + sm[l:l + 1, :] * dsm[l]
        for l in range(depth):
            out_ref[l:l + 1, :] = sm[l:l + 1, :] * (dsm[l] - inner)
        for j in range(SUBLANES - depth):
            row = d_ref[depth + j:depth + j + 1, :]
            tot = row[:, 0:HEAD_DIM]
            for hd in range(1, HEADS):
                tot = tot + row[:, hd * HEAD_DIM:(hd + 1) * HEAD_DIM]
            out_ref[depth + j:depth + j + 1, 0:HEAD_DIM] = tot

    return pl.pallas_call(body, name=name, out_shape=jax.ShapeDtypeStruct((SUBLANES, D), F32))(lower, dlb)


def _adamw(name, parts, w, m, v, layer=None, prev=None):
    p, r, c = parts.shape
    tr = _tile(r, max(SUBLANES, ADAMW_STEP_BYTES // (4 * c * (p + 7))))
    stacked = layer is not None
    n_prev = 4 if prev is not None else 0

    def body(*refs):
        parts_ref, w_ref, m_ref, v_ref = refs[:4]
        g_ref, d_ref, m_out, v_out = refs[4 + n_prev:]
        g = parts_ref[0].astype(F32)
        for q in range(1, p):
            g = g + parts_ref[q].astype(F32)
        m2 = ADAM_B1 * m_ref[...] + (1.0 - ADAM_B1) * g
        v2 = ADAM_B2 * v_ref[...] + (1.0 - ADAM_B2) * (g * g)
        m_hat = m2 / (1.0 - ADAM_B1 ** ADAM_STEP)
        v_hat = v2 / (1.0 - ADAM_B2 ** ADAM_STEP)
        g_ref[...] = g
        d_ref[...] = -ADAM_LR * (m_hat / (jnp.sqrt(v_hat) + ADAM_EPS) + ADAM_WD * w_ref[...])
        m_out[...] = m2
        v_out[...] = v2

    if stacked:
        spec = pl.BlockSpec((None, tr, c), lambda i: (layer, i, 0))
    else:
        spec = pl.BlockSpec((tr, c), lambda i: (i, 0))
    return pl.pallas_call(
        body, name=name, grid=(r // tr,),
        in_specs=[pl.BlockSpec((p, tr, c), lambda i: (0, i, 0)), spec, spec, spec] + [ANY] * n_prev, out_specs=[spec] * 4,
        out_shape=[jax.ShapeDtypeStruct(w.shape, F32)] * 4,
        input_output_aliases={4 + q: q for q in range(n_prev)}, compiler_params=_params(),
    )(parts, w, m, v, *(prev or []))


def _hbm(a):
    return pltpu.with_memory_space_constraint(a, pltpu.HBM)


def _vec(*rows):
    rows = [r.reshape(1, D).astype(F32) for r in rows]
    packed = jnp.concatenate(rows + [jnp.zeros((SUBLANES - len(rows), D), F32)], axis=0)
    return _hbm(packed)


def _pad_rows(a, rows=SUBLANES):
    a = a.reshape(-1, a.shape[-1])
    return jnp.concatenate([a, jnp.zeros((rows - a.shape[0], a.shape[1]), a.dtype)], axis=0) if a.shape[0] < rows else a


def kernel(x, c, mod_w, mod_b, norm_mix, norm_mlp, norm_final, rg_w_in, rg_conv_w, rg_conv_b, rg_w_a, rg_b_a, rg_w_x, rg_b_x, rg_lambda, rg_w_out, hg_w_in, hg_lower_bounds, hg_gnorm, hg_w_out, mlp_w1, mlp_w2, loss_target, m_mod_w, m_mod_b, m_norm_mix, m_norm_mlp, m_norm_final, m_rg_w_in, m_rg_conv_w, m_rg_conv_b, m_rg_w_a, m_rg_b_a, m_rg_w_x, m_rg_b_x, m_rg_lambda, m_rg_w_out, m_hg_w_in, m_hg_lower_bounds, m_hg_gnorm, m_hg_w_out, m_mlp_w1, m_mlp_w2, v_mod_w, v_mod_b, v_norm_mix, v_norm_mlp, v_norm_final, v_rg_w_in, v_rg_conv_w, v_rg_conv_b, v_rg_w_a, v_rg_b_a, v_rg_w_x, v_rg_b_x, v_rg_lambda, v_rg_w_out, v_hg_w_in, v_hg_lower_bounds, v_hg_gnorm, v_hg_w_out, v_mlp_w1, v_mlp_w2):
    me = 4 * lax.axis_index("x") + 2 * lax.axis_index("y") + lax.axis_index("c")
    x0 = x[0]
    target = loss_target[0]
    n_rg, n_hg = rg_w_in.shape[0], hg_w_in.shape[0]
    sw_mod = mod_w.shape[2]

    c_all, cw_all = _all_gather([_pad_rows(c), rg_conv_w.reshape(n_rg * CONV_WIDTH, -1)], "gather_cond")
    c_all = c_all[:, 0, :]
    conv_w = cw_all.transpose(1, 0, 2).reshape(n_rg, CONV_WIDTH, D)
    cs_all, mod_part, lb_all = _mod_forward("mod_forward", c_all, mod_w, hg_lower_bounds)
    (mod_gathered,) = _all_gather([mod_part.reshape(DEPTH * N_DEV, sw_mod)], "gather_mod")

    ids = iter(range(4 * DEPTH))
    shards = []
    for layer in range(DEPTH):
        j = layer // 2
        w_in, w_out = (rg_w_in[j], rg_w_out[j]) if layer % 2 == 0 else (hg_w_in[j], hg_w_out[j])
        shards.append([w_in.astype(MXU), w_out.astype(MXU), mlp_w1[layer].astype(MXU), mlp_w2[layer].astype(MXU)])
    shards, mod_gathered = lax.optimization_barrier((shards, mod_gathered))
    weights = []
    for layer in range(DEPTH):
        if layer == 0:
            g_in, g_out = _all_gather_async(shards[0][:2], "gather_mixer_weights_0", next(ids), ["cols", "rows"])
            g_w1, g_w2 = _all_gather_async(shards[0][2:], "gather_mlp_weights_0", next(ids), ["cols", "rows"])
        else:
            g_in, g_out, g_w1, g_w2 = _all_gather_async(shards[layer], f"gather_weights_{layer}", next(ids),
                                                        ["cols", "rows", "cols", "rows"])
        weights.append(dict(w_in=g_in, w_out=g_out, w1=g_w1, w2=g_w2))

    mod_mine = lax.dynamic_index_in_dim(mod_gathered.reshape(N_DEV, DEPTH, N_DEV, sw_mod), me, axis=2, keepdims=False)
    mod = mod_mine.transpose(1, 0, 2).reshape(DEPTH, 6, D) + mod_b.reshape(DEPTH, 6, D)

    saved = []
    xl = x0
    for layer in range(DEPTH):
        j, wt = layer // 2, weights[layer]
        is_rg = layer % 2 == 0
        s = dict(x=xl)
        s["vec_mix"] = _vec(norm_mix[layer], mod[layer, 1], mod[layer, 0])
        s["vec_mlp"] = _vec(norm_mlp[layer], mod[layer, 4], mod[layer, 3])
        s["gate_mix"], s["gate_mlp"] = _vec(mod[layer, 2]), _vec(mod[layer, 5])
        s["h"] = _hbm(_norm_mod("norm_mod", xl, s["vec_mix"]))
        if is_rg:
            (s["z"],) = _mm_tokens("rg_in", s["h"], wt["w_in"], False, (F32,))
            s["cw"] = conv_w[j]
            s["vec"] = _vec(rg_conv_b[j], rg_b_a[j], rg_b_x[j], rg_lambda[j])
            s["wa"], s["wx"] = rg_w_a[j].astype(MXU), rg_w_x[j].astype(MXU)
            s["p"], s["hr"] = _rg_forward("rg_forward", s["z"], s["cw"], s["vec"], s["wa"], s["wx"])
        else:
            (s["z"],) = _mm_tokens("hg_in", s["h"], wt["w_in"], False, (F32,))
            s["vec"] = _vec(lb_all[layer], jnp.tile(hg_gnorm[j], HEADS))
            s["p"], s["o"], s["ss"] = _hg_forward("hg_forward", s["z"], s["vec"])
        s["y"], s["x1"] = _mm_tokens("mix_out", s["p"], wt["w_out"], False, (F32, F32), _ep_residual, extras=(xl, s["gate_mix"]))
        s["p"] = _hbm(s["p"])
        s["h2"] = _hbm(_norm_mod("norm_mod", s["x1"], s["vec_mlp"]))
        (s["r"],) = _mm_tokens("mlp_in", s["h2"], wt["w1"], False, (MXU,), _ep_relu)
        s["ff"], xl = _mm_tokens("mlp_out", s["r"], wt["w2"], False, (F32, F32), _ep_residual,
                                 extras=(s["x1"], s["gate_mlp"]), prologue=_square)
        saved.append(s)

    dx, head = _loss_head("loss_head", xl, target, _vec(norm_final))

    results = {}
    big = dict(rg_w_in=(rg_w_in, m_rg_w_in, v_rg_w_in), rg_w_out=(rg_w_out, m_rg_w_out, v_rg_w_out),
               hg_w_in=(hg_w_in, m_hg_w_in, v_hg_w_in), hg_w_out=(hg_w_out, m_hg_w_out, v_hg_w_out),
               mlp_w1=(mlp_w1, m_mlp_w1, v_mlp_w1), mlp_w2=(mlp_w2, m_mlp_w2, v_mlp_w2))

    def update_layer(layer, landed):
        mixer = "rg" if layer % 2 == 0 else "hg"
        l_in, l_out, l_w1, l_w2 = landed
        for nm, parts, idx in ((f"{mixer}_w_in", l_in, layer // 2), (f"{mixer}_w_out", l_out, layer // 2),
                               ("mlp_w1", l_w1, layer), ("mlp_w2", l_w2, layer)):
            w, m, v = big[nm]
            results[nm] = _adamw(f"adamw_{nm}", parts, w, m, v, layer=idx, prev=results.get(nm))

    landed = [None] * DEPTH
    small = [None] * DEPTH
    parity = lax.axis_index("c").astype(jnp.int32).reshape(1)

    def send_chip_sums(layer, grads, from_sibling, anchor):
        sums = _pair_sum("pair_sum", parity, grads, from_sibling)
        sums, anchor = lax.optimization_barrier((sums, anchor))
        landed[layer] = _chip_exchange_async(sums, f"exchange_grads_{layer}", next(ids))
        return anchor

    pending = None
    for layer in reversed(range(DEPTH)):
        j, wt, s = layer // 2, weights[layer], saved[layer]
        is_rg = layer % 2 == 0
        dff, g_mlp = _residual_grad("residual_grad", dx, s["ff"], s["gate_mlp"])
        dff = _hbm(dff)
        (da,) = _mm_tokens("mlp_out_t", dff, wt["w2"], True, (MXU,), _ep_relu2_grad, extras=(s["r"],))
        dw2 = _mm_grad("mlp_out_grad", s["r"], dff, False, prologue=_square).reshape(N_DEV, -1, D)
        dw1 = _mm_grad("mlp_in_grad", s["h2"], da, True)
        if pending is not None:
            da = send_chip_sums(*pending, da)
        (dh2,) = _mm_tokens("mlp_in_t", da, wt["w1"], True, (F32,))
        dx1, n_mlp = _norm_mod_grad("norm_mod_grad", dh2, s["x1"], s["vec_mlp"], dx)
        dyb, g_mix = _residual_grad("residual_grad", dx1, s["y"], s["gate_mix"])
        dyb = _hbm(dyb)
        (dp,) = _mm_tokens("mix_out_t", dyb, wt["w_out"], True, (F32,))
        dw_out = _hbm(_mm_grad("mix_out_grad", s["p"], dyb, False).reshape(N_DEV, -1, D))
        if is_rg:
            dz, dvec, dcw, dwa, dwx = _rg_backward("rg_backward", dp, s["z"], s["hr"], s["cw"], s["vec"], s["wa"], s["wx"])
            gate_grads, dz = lax.optimization_barrier(
                ([dwa.reshape(-1, HEAD_DIM).astype(MXU), dwx.reshape(-1, HEAD_DIM).astype(MXU)], dz))
            gate_parts = _all_gather_async(gate_grads, f"gather_gate_grads_{j}", next(ids), ["stack", "stack"])
            mixer_small = dict(dvec=dvec, dcw=dcw, gate_parts=gate_parts)
            dw_in = _mm_grad("rg_in_grad", s["h"], dz, True)
            (dh,) = _mm_tokens("rg_in_t", dz, wt["w_in"], True, (F32,))
        else:
            dz, dvec = _hg_backward("hg_backward", dp, s["z"], s["o"], s["ss"], s["vec"])
            mixer_small = dict(dvec=dvec)
            dw_in = _mm_grad("hg_in_grad", s["h"], dz, True)
            (dh,) = _mm_tokens("hg_in_t", dz, wt["w_in"], True, (F32,))
        grads = [dw_in, dw_out, dw1, dw2]
        pending = (layer, grads, _sibling_send_async(grads, f"pair_grads_{layer}", next(ids)))
        dx, n_mix = _norm_mod_grad("norm_mod_grad", dh, s["x"], s["vec_mix"], dx1)
        small[layer] = dict(g_mlp=g_mlp, n_mlp=n_mlp, g_mix=g_mix, n_mix=n_mix, **mixer_small)
        if layer + 1 < DEPTH:
            landed[layer + 1], dx = lax.optimization_barrier((landed[layer + 1], dx))
            if "gate_parts" in small[layer + 1]:
                small[layer + 1]["gate_parts"], dx = lax.optimization_barrier((small[layer + 1]["gate_parts"], dx))
            update_layer(layer + 1, landed[layer + 1])

    dlb_rows = [jnp.zeros((1, D), F32) if l % 2 == 0 else small[l]["dvec"][0:1] for l in range(DEPTH)]
    dgn_rows = [small[2 * j + 1]["dvec"][1:2] for j in range(n_hg)]
    lb_grad = _lower_bound_grad("lower_bound_grad", hg_lower_bounds, _pad_rows(jnp.concatenate(dlb_rows + dgn_rows, axis=0)))
    dmod = jnp.stack([jnp.concatenate([small[l]["n_mix"][2], small[l]["n_mix"][1], small[l]["g_mix"][0],
                                       small[l]["n_mlp"][2], small[l]["n_mlp"][1], small[l]["g_mlp"][0]]) for l in range(DEPTH)])
    groups = [
        dmod.reshape(DEPTH * 6, D),
        jnp.stack([small[l]["n_mix"][0] for l in range(DEPTH)]),
        jnp.stack([small[l]["n_mlp"][0] for l in range(DEPTH)]),
        head[0:1],
        jnp.stack([small[2 * j]["dvec"][0] for j in range(n_rg)]),
        jnp.stack([small[2 * j]["dvec"][1] for j in range(n_rg)]),
        jnp.stack([small[2 * j]["dvec"][2] for j in range(n_rg)]),
        jnp.stack([small[2 * j]["dvec"][3] for j in range(n_rg)]),
        lb_grad[0:DEPTH],
        jnp.concatenate([lb_grad[DEPTH + j:DEPTH + j + 1, 0:HEAD_DIM] for j in range(n_hg)]
                        + [jnp.zeros((1, D - n_hg * HEAD_DIM), F32)], axis=1),
        jnp.concatenate([small[2 * j]["dcw"][0:CONV_WIDTH] for j in range(n_rg)], axis=0),
        head[1:2],
    ]
    params = [mod_b, norm_mix, norm_mlp, norm_final, rg_conv_b, rg_b_a, rg_b_x, rg_lambda, hg_lower_bounds, hg_gnorm]
    moms = [m_mod_b, m_norm_mix, m_norm_mlp, m_norm_final, m_rg_conv_b, m_rg_b_a, m_rg_b_x, m_rg_lambda, m_hg_lower_bounds, m_hg_gnorm]
    vars_ = [v_mod_b, v_norm_mix, v_norm_mlp, v_norm_final, v_rg_conv_b, v_rg_b_a, v_rg_b_x, v_rg_lambda, v_hg_lower_bounds, v_hg_gnorm]
    offsets, rows_of, at = [], [], 0
    for g in groups:
        offsets.append(at)
        rows_of.append(g.shape[0])
        at += -(-g.shape[0] // SUBLANES) * SUBLANES
    packed = jnp.concatenate([_pad_rows(g, -(-g.shape[0] // SUBLANES) * SUBLANES) for g in groups], axis=0)
    (small_parts,) = _all_gather([packed], "gather_small_grads")
    last_layer, last_grads, from_sibling = pending
    from_sibling, small_parts = lax.optimization_barrier((from_sibling, small_parts))
    send_chip_sums(last_layer, last_grads, from_sibling, dx)

    def pack_like(arrs):
        out = []
        for g_rows, off, a in zip(rows_of, offsets, arrs):
            flat = a.reshape(-1)
            flat = jnp.concatenate([flat, jnp.zeros((g_rows * D - flat.shape[0],), F32)])
            out.append(_pad_rows(flat.reshape(g_rows, D), -(-g_rows // SUBLANES) * SUBLANES))
        rest = packed.shape[0] - sum(o.shape[0] for o in out)
        return jnp.concatenate(out + [jnp.zeros((rest, D), F32)], axis=0)

    small_out = _adamw("adamw_small", small_parts, pack_like(params), pack_like(moms), pack_like(vars_))

    def unpack(q, idx, like):
        rows = small_out[q][offsets[idx]:offsets[idx] + rows_of[idx]]
        return rows.reshape(-1)[:like.size].reshape(like.shape)

    loss = jnp.sum(small_out[0][offsets[11]])
    names =["mod_b", "norm_mix", "norm_mlp", "norm_final", "rg_conv_b", "rg_b_a", "rg_b_x", "rg_lambda", "hg_lower_bounds", "hg_gnorm"]
    for idx, (nm, like) in enumerate(zip(names, params)):
        results[nm] = [unpack(q, idx, like) for q in range(4)]

    cw_parts = lax.dynamic_slice_in_dim(small_parts[:, offsets[10]:offsets[10] + n_rg * CONV_WIDTH, :], me * (D // N_DEV), D // N_DEV, axis=2)
    shp = rg_conv_w.shape
    results["rg_conv_w"] = [o.reshape(shp) for o in _adamw(
        "adamw_conv", cw_parts, rg_conv_w.reshape(-1, shp[-1]), m_rg_conv_w.reshape(-1, shp[-1]), v_rg_conv_w.reshape(-1, shp[-1]))]
    shp = rg_w_a.shape
    stacked = (n_rg, HEADS * HEAD_DIM, HEAD_DIM)
    for nm, which, (w, m, v) in (("rg_w_a", 0, (rg_w_a, m_rg_w_a, v_rg_w_a)), ("rg_w_x", 1, (rg_w_x, m_rg_w_x, v_rg_w_x))):
        out = None
        for j in reversed(range(n_rg)):
            out = _adamw("adamw_gate", small[2 * j]["gate_parts"][which], w.reshape(stacked), m.reshape(stacked),
                         v.reshape(stacked), layer=j, prev=out)
        results[nm] = [o.reshape(shp) for o in out]

    dmod_all = small_parts[:, 0:DEPTH * 6, :].reshape(N_DEV, DEPTH, 6 * D)
    dmod_cols = lax.dynamic_slice_in_dim(dmod_all, me * sw_mod, sw_mod, axis=2).transpose(1, 0, 2)
    pad = HEAD_DIM - N_DEV
    dmod_pad = jnp.concatenate([dmod_cols, jnp.zeros((DEPTH, pad, sw_mod), F32)], axis=1).astype(MXU)
    cs_t = jnp.concatenate([cs_all.T, jnp.zeros((D, pad), F32)], axis=1).astype(MXU)
    g_mod_w = _mod_weight_grad("mod_weight_grad", cs_t, dmod_pad)
    results["mod_w"] = [o.reshape(mod_w.shape) for o in _adamw(
        "adamw_mod", g_mod_w.reshape(1, -1, sw_mod), mod_w.reshape(-1, sw_mod), m_mod_w.reshape(-1, sw_mod), v_mod_w.reshape(-1, sw_mod))]

    update_layer(0, landed[0])

    order = ["mod_w", "mod_b", "norm_mix", "norm_mlp", "norm_final", "rg_w_in", "rg_conv_w", "rg_conv_b", "rg_w_a", "rg_b_a", "rg_w_x",
             "rg_b_x", "rg_lambda", "rg_w_out", "hg_w_in", "hg_lower_bounds", "hg_gnorm", "hg_w_out", "mlp_w1", "mlp_w2"]
    return (loss, dx[None], *[results[n][0] for n in order], *[results[n][1] for n in order],
            *[results[n][2] for n in order], *[results[n][3] for n in order])
```

```python
import functools

import jax
import jax.numpy as jnp
from jax import lax
from jax.experimental import pallas as pl
from jax.experimental.pallas import tpu as pltpu
from jax.experimental.pallas import tpu_sc as plsc

F32 = jnp.float32
MXU = jnp.bfloat16

N_DEV = 8
D = 1024
DEPTH = 4
HEADS = 8
HEAD_DIM = 128
CONV_WIDTH = 4
LRU_C = 8.0
HG_CHUNK = 64
NORM_EPS = 1e-6
GNORM_EPS = 1e-5
ADAM_LR = 0.001
ADAM_B1 = 0.9
ADAM_B2 = 0.999
ADAM_EPS = 1e-08
ADAM_WD = 0.01
ADAM_STEP = 10
GELU_C = 0.7978845608028654
GELU_K = 0.044715
EXP_CLAMP = 80.0
SUBLANES = 8
VMEM_LIMIT = 48 * 1024 * 1024
ROW_TILE = 256
MM_TILE = 1024
MM_TOKENS = 512
ADAMW_STEP_BYTES = 8 * 1024 * 1024

MESH = pl.DeviceIdType.MESH
ANY = pl.BlockSpec(memory_space=pl.ANY)


def _params():
    return pltpu.CompilerParams(vmem_limit_bytes=VMEM_LIMIT)


def _tile(n, target):
    if n <= target:
        return n
    t = target // SUBLANES * SUBLANES
    while n % t:
        t -= SUBLANES
    return t


def _sigmoid(x):
    return 1.0 / (1.0 + jnp.exp(-x))


def _silu(x):
    return x * _sigmoid(x)


def _silu_grad(x):
    s = _sigmoid(x)
    return s * (1.0 + x * (1.0 - s))


def _gelu(y):
    return 0.5 * y * (1.0 + jnp.tanh(GELU_C * (y + GELU_K * y * y * y)))


def _gelu_grad(y):
    th = jnp.tanh(GELU_C * (y + GELU_K * y * y * y))
    return 0.5 * (1.0 + th) + 0.5 * y * (1.0 - th * th) * GELU_C * (1.0 + 3.0 * GELU_K * y * y)


def _dot(a, b):
    return lax.dot_general(a.astype(MXU), b.astype(MXU), (((1,), (0,)), ((), ())), preferred_element_type=F32)


def _dot_nt(a, b):
    return lax.dot_general(a.astype(MXU), b.astype(MXU), (((1,), (1,)), ((), ())), preferred_element_type=F32)


def _dot_tn(a, b):
    return lax.dot_general(a.astype(MXU), b.astype(MXU), (((0,), (0,)), ((), ())), preferred_element_type=F32)


def _dot_exact(tri, x):
    t = tri.astype(MXU)
    hi = x.astype(MXU)
    r1 = x - hi.astype(F32)
    mid = r1.astype(MXU)
    lo = (r1 - mid.astype(F32)).astype(MXU)
    dn = (((1,), (0,)), ((), ()))
    return (lax.dot_general(t, hi, dn, preferred_element_type=F32) + lax.dot_general(t, mid, dn, preferred_element_type=F32)
            + lax.dot_general(t, lo, dn, preferred_element_type=F32))


def _row_sum(v):
    return jnp.sum(v, axis=0, keepdims=True)


def _handshake(partners):
    barrier = pltpu.get_barrier_semaphore()
    for p in partners:
        pl.semaphore_signal(barrier, inc=1, device_id=p, device_id_type=MESH)
    pl.semaphore_wait(barrier, len(partners))


def _gather_body(n, per_array_sems, handshake, layouts):
    def body(*refs):
        ins, outs = refs[:n], refs[n:2 * n]
        send_sems, recv_sems, local_sems = refs[2 * n:]
        x, y, c = lax.axis_index("x"), lax.axis_index("y"), lax.axis_index("c")
        me, sibling = (x, y, c), (x, y, 1 - c)
        chips = [(1 - x, y), (x, 1 - y), (1 - x, 1 - y)]
        if handshake:
            _handshake([sibling] + [(*chip, c) for chip in chips])

        def sem(sems, a, k):
            return sems.at[a, k] if per_array_sems else sems.at[k]

        def slot(a, p):
            block = 4 * p[0] + 2 * p[1] + p[2]
            r, c_ = ins[a].shape
            if layouts[a] == "rows":
                return outs[a].at[pl.ds(block * r, r), :]
            if layouts[a] == "cols":
                return outs[a].at[:, pl.ds(block * c_, c_)]
            return outs[a].at[block]

        def copy(a, k, block, to, src=None):
            return pltpu.make_async_remote_copy(
                src_ref=slot(a, block) if src is None else src, dst_ref=slot(a, block),
                send_sem=sem(send_sems, a, k), recv_sem=sem(recv_sems, a, k), device_id=to, device_id_type=MESH)

        mine = [pltpu.make_async_copy(ins[a], slot(a, me), local_sems.at[a if per_array_sems else 0]) for a in range(n)]
        for cp in mine:
            cp.start()
        first = []
        for a in range(n):
            first.append(copy(a, 0, me, sibling, src=ins[a]))
            first += [copy(a, 1 + j, me, (*chip, c), src=ins[a]) for j, chip in enumerate(chips)]
        for cp in first:
            cp.start()
        passed = []
        for j, chip in enumerate(chips):
            for a in range(n):
                copy(a, 1 + j, (*chip, c), me).wait_recv()
            for a in range(n):
                cp = copy(a, 4 + j, (*chip, c), sibling)
                cp.start()
                passed.append(cp)
        for a in range(n):
            copy(a, 0, sibling, me).wait_recv()
        for j, chip in enumerate(chips):
            for a in range(n):
                copy(a, 4 + j, (*chip, 1 - c), me).wait_recv()
        for cp in first + passed:
            cp.wait_send()
        for cp in mine:
            cp.wait()

    return body


def _sibling_send_body(n, per_array_sems, handshake):
    def body(*refs):
        ins, outs = refs[:n], refs[n:2 * n]
        send_sems, recv_sems, _ = refs[2 * n:]
        x, y, c = lax.axis_index("x"), lax.axis_index("y"), lax.axis_index("c")
        sibling = (x, y, 1 - c)
        if handshake:
            _handshake([sibling])

        def sem(sems, a):
            return sems.at[a, 0] if per_array_sems else sems.at[0]

        copies = [pltpu.make_async_remote_copy(
            src_ref=ins[a].at[2 * q + 1 - c], dst_ref=outs[a].at[q], send_sem=sem(send_sems, a), recv_sem=sem(recv_sems, a),
            device_id=sibling, device_id_type=MESH) for a in range(n) for q in range(4)]
        for cp in copies:
            cp.start()
        for cp in copies:
            cp.wait_recv()
        for cp in copies:
            cp.wait_send()

    return body


def _chip_exchange_body(n, per_array_sems, handshake):
    def body(*refs):
        ins, outs = refs[:n], refs[n:2 * n]
        send_sems, recv_sems, local_sems = refs[2 * n:]
        x, y, c = lax.axis_index("x"), lax.axis_index("y"), lax.axis_index("c")
        my_chip = 2 * x + y
        chips = [(1 - x, y), (x, 1 - y), (1 - x, 1 - y)]
        if handshake:
            _handshake([(*chip, c) for chip in chips])

        def sem(sems, a, k):
            return sems.at[a, k] if per_array_sems else sems.at[k]

        def copy(a, k, landing):
            px, py = chips[k]
            return pltpu.make_async_remote_copy(
                src_ref=ins[a].at[2 * px + py], dst_ref=outs[a].at[landing], send_sem=sem(send_sems, a, k),
                recv_sem=sem(recv_sems, a, k), device_id=(px, py, c), device_id_type=MESH)

        mine = [pltpu.make_async_copy(ins[a].at[my_chip], outs[a].at[my_chip], local_sems.at[a if per_array_sems else 0])
                for a in range(n)]
        for cp in mine:
            cp.start()
        sent = [copy(a, k, my_chip) for a in range(n) for k in range(3)]
        for cp in sent:
            cp.start()
        for a in range(n):
            for k, (px, py) in enumerate(chips):
                copy(a, k, 2 * px + py).wait_recv()
        for cp in sent:
            cp.wait_send()
        for cp in mine:
            cp.wait()

    return body


def _all_gather(arrs, name):
    n = len(arrs)
    return pl.pallas_call(
        _gather_body(n, True, False, ["stack"] * n), name=name, in_specs=[ANY] * n, out_specs=[ANY] * n,
        out_shape=[jax.ShapeDtypeStruct((N_DEV,) + a.shape, a.dtype) for a in arrs],
        scratch_shapes=[pltpu.SemaphoreType.DMA((n, 7)), pltpu.SemaphoreType.DMA((n, 7)), pltpu.SemaphoreType.DMA((n,))],
    )(*arrs)


def _on_sequencer(body, arrs, out_type, name, collective_id):
    return pl.kernel(
        body, name=name, out_type=out_type, mesh=plsc.ScalarSubcoreMesh(axis_name="sequencer", num_cores=1),
        scratch_types=[pltpu.SemaphoreType.DMA((7,)), pltpu.SemaphoreType.DMA((7,)), pltpu.SemaphoreType.DMA((1,))],
        compiler_params=pltpu.CompilerParams(collective_id=collective_id))(*arrs)


def _all_gather_async(arrs, name, collective_id, layouts):
    shape = dict(stack=lambda r, c: (N_DEV, r, c), rows=lambda r, c: (N_DEV * r, c), cols=lambda r, c: (r, N_DEV * c))
    out_type = [jax.ShapeDtypeStruct(shape[lay](*a.shape), a.dtype) for a, lay in zip(arrs, layouts)]
    return _on_sequencer(_gather_body(len(arrs), False, True, layouts), arrs, out_type, name, collective_id)


def _sibling_send_async(arrs, name, collective_id):
    out_type = [jax.ShapeDtypeStruct((N_DEV // 2,) + a.shape[1:], a.dtype) for a in arrs]
    return _on_sequencer(_sibling_send_body(len(arrs), False, True), arrs, out_type, name, collective_id)


def _chip_exchange_async(arrs, name, collective_id):
    out_type = [jax.ShapeDtypeStruct(a.shape, a.dtype) for a in arrs]
    return _on_sequencer(_chip_exchange_body(len(arrs), False, True), arrs, out_type, name, collective_id)


def _pair_sum(name, parity, mine, theirs):
    n = len(mine)

    def body(par_ref, *refs):
        for a in range(n):
            refs[2 * n + a][...] = (refs[a][...].astype(F32) + refs[n + a][...].astype(F32)).astype(refs[2 * n + a].dtype)

    def block(a):
        return (None,) + a.shape[1:]

    grid_spec = pltpu.PrefetchScalarGridSpec(
        num_scalar_prefetch=1, grid=(N_DEV // 2,),
        in_specs=[pl.BlockSpec(block(a), lambda q, par: (2 * q + par[0], 0, 0)) for a in mine]
        + [pl.BlockSpec(block(a), lambda q, par: (q, 0, 0)) for a in theirs],
        out_specs=[pl.BlockSpec(block(a), lambda q, par: (q, 0, 0)) for a in theirs])
    return pl.pallas_call(body, name=name, grid_spec=grid_spec, out_shape=[jax.ShapeDtypeStruct(a.shape, a.dtype) for a in theirs],
                          compiler_params=_params())(parity, *mine, *theirs)


NN = (((1,), (0,)), ((), ()))
NT = (((1,), (1,)), ((), ()))
TN = (((0,), (0,)), ((), ()))


def _matmul(name, a, b, dims, grid, a_spec, b_spec, outs, epilogue, extras=(), prologue=None):
    n_in = 2 + len(extras)
    n_out = len(outs)

    def body(*refs):
        a_ref, b_ref = refs[0], refs[1]
        ex, out_refs = refs[2:n_in], refs[n_in:n_in + n_out]
        a_tile = a_ref[...] if prologue is None else prologue(a_ref[...])
        epilogue(lax.dot_general(a_tile, b_ref[...], dims, preferred_element_type=F32), ex, out_refs)

    return pl.pallas_call(
        body, name=name, grid=grid, in_specs=[a_spec, b_spec] + [s for _, s in extras], out_specs=[s for _, _, s in outs],
        out_shape=[jax.ShapeDtypeStruct(sh, dt) for sh, dt, _ in outs], compiler_params=_params(),
    )(a, b, *[e for e, _ in extras])


def _square(tile):
    return tile * tile


def _ep_store(acc, ex, outs):
    outs[0][...] = acc.astype(outs[0].dtype)


def _ep_residual(acc, ex, outs):
    outs[0][...] = acc
    outs[1][...] = ex[0][...] + ex[1][0:1, :] * acc


def _ep_relu(acc, ex, outs):
    outs[0][...] = jnp.maximum(acc, 0.0).astype(outs[0].dtype)


def _ep_relu2_grad(acc, ex, outs):
    outs[0][...] = (acc * (2.0 * ex[0][...].astype(F32))).astype(outs[0].dtype)


def _mm_tokens(name, a, w, transposed, out_dtypes, epilogue=_ep_store, extras=(), prologue=None, rows=MM_TOKENS):
    m, n = a.shape[0], w.shape[0 if transposed else 1]
    tm = _tile(m, rows)
    rows_spec = lambda width: pl.BlockSpec((tm, width), lambda i, j, kk: (i, 0))
    whole = lambda arr: pl.BlockSpec(arr.shape, lambda i, j, kk: (0, 0))
    return _matmul(name, a, w, NT if transposed else NN, (m // tm, 1, 1), rows_spec(a.shape[1]), whole(w),
                   [((m, n), dt, rows_spec(n)) for dt in out_dtypes], epilogue,
                   extras=[(e, rows_spec(n) if e.shape[0] == m else whole(e)) for e in extras], prologue=prologue)


def _mm_grad(name, a, b, shard_cols, prologue=None):
    t, m = a.shape
    n = b.shape[1]
    tm = _tile(m, MM_TILE)
    if shard_cols:
        tn = n // N_DEV
        out = ((N_DEV, m, tn), MXU, pl.BlockSpec((None, tm, tn), lambda i, j, kk: (j, i, 0)))
    else:
        tn = _tile(n, MM_TILE)
        out = ((m, n), MXU, pl.BlockSpec((tm, tn), lambda i, j, kk: (i, j)))
    return _matmul(name, a, b, TN, (m // tm, n // tn, 1), pl.BlockSpec((t, tm), lambda i, j, kk: (0, i)),
                   pl.BlockSpec((t, tn), lambda i, j, kk: (0, j)), [out], _ep_store, prologue=prologue)[0]


def _rows_spec(tt, width=D):
    return pl.BlockSpec((tt, width), lambda i: (i, 0))


def _vec_spec(rows=SUBLANES, width=D):
    return pl.BlockSpec((rows, width), lambda i: (0, 0))


def _norm_mod(name, x, vec):
    t = x.shape[0]
    tt = _tile(t, ROW_TILE)

    def body(x_ref, v_ref, h_ref):
        xv = x_ref[...]
        r = lax.rsqrt(jnp.mean(xv * xv, axis=-1, keepdims=True) + NORM_EPS)
        h = (xv * r) * v_ref[0:1, :]
        h_ref[...] = (h * (1.0 + v_ref[1:2, :]) + v_ref[2:3, :]).astype(h_ref.dtype)

    return pl.pallas_call(body, name=name, grid=(t // tt,), in_specs=[_rows_spec(tt), _vec_spec()], out_specs=_rows_spec(tt),
                          out_shape=jax.ShapeDtypeStruct((t, D), MXU))(x, vec)


GATE_ROW = 3


def _branch_grad(dx, y_ref, g_ref, dy_ref, acc_ref):
    dy_ref[...] = (dx * g_ref[0:1, :]).astype(dy_ref.dtype)
    acc_ref[GATE_ROW:GATE_ROW + 1, :] += _row_sum(dx * y_ref[...])


def _norm_mod_grad(name, dh, x, vec, dres, y=None, gate=None):
    t = x.shape[0]
    tt = _tile(t, ROW_TILE)
    nt = t // tt
    branch = y is not None

    def body(dh_ref, x_ref, v_ref, dres_ref, *rest):
        dx_ref, acc_ref = rest[2 * branch], rest[2 * branch + 1]
        i = pl.program_id(0)

        @pl.when(i == 0)
        def _():
            acc_ref[...] = jnp.zeros_like(acc_ref)

        xv, dhv = x_ref[...], dh_ref[...]
        r = lax.rsqrt(jnp.mean(xv * xv, axis=-1, keepdims=True) + NORM_EPS)
        xn = xv * r
        w = v_ref[0:1, :] * (1.0 + v_ref[1:2, :])
        acc_ref[0:1, :] += _row_sum(dhv * xn)
        acc_ref[2:3, :] += _row_sum(dhv)
        dxn = dhv * w
        dx = dres_ref[...] + r * (dxn - xn * jnp.mean(dxn * xn, axis=-1, keepdims=True))
        dx_ref[...] = dx
        if branch:
            _branch_grad(dx, rest[0], rest[1], rest[4], acc_ref)

        @pl.when(i == nt - 1)
        def _():
            dw = acc_ref[0:1, :]
            acc_ref[1:2, :] = dw * v_ref[0:1, :]
            acc_ref[0:1, :] = dw * (1.0 + v_ref[1:2, :])

    extra_in = [_rows_spec(tt), _vec_spec()] if branch else []
    extra_out = [_rows_spec(tt)] if branch else []
    extra_shape = [jax.ShapeDtypeStruct((t, D), MXU)] if branch else []
    return pl.pallas_call(
        body, name=name, grid=(nt,), in_specs=[_rows_spec(tt), _rows_spec(tt), _vec_spec(), _rows_spec(tt)] + extra_in,
        out_specs=[_rows_spec(tt), _vec_spec()] + extra_out,
        out_shape=[jax.ShapeDtypeStruct((t, D), F32), jax.ShapeDtypeStruct((SUBLANES, D), F32)] + extra_shape,
    )(dh, x, vec, dres, *([y, gate] if branch else []))


def _loss_head(name, x, target, vec, y, gate):
    t = x.shape[0]
    tt = _tile(t, ROW_TILE)

    def body(x_ref, t_ref, v_ref, y_ref, g_ref, dx_ref, acc_ref, dy_ref):
        @pl.when(pl.program_id(0) == 0)
        def _():
            acc_ref[...] = jnp.zeros_like(acc_ref)

        xv = x_ref[...]
        r = lax.rsqrt(jnp.mean(xv * xv, axis=-1, keepdims=True) + NORM_EPS)
        xn = xv * r
        gain = v_ref[0:1, :]
        err = xn * gain - t_ref[...]
        acc_ref[1:2, :] += _row_sum(err * err) * (0.5 / D)
        dout = err * (1.0 / D)
        acc_ref[0:1, :] += _row_sum(dout * xn)
        dxn = dout * gain
        dx = r * (dxn - xn * jnp.mean(dxn * xn, axis=-1, keepdims=True))
        dx_ref[...] = dx
        _branch_grad(dx, y_ref, g_ref, dy_ref, acc_ref)

    return pl.pallas_call(
        body, name=name, grid=(t // tt,), in_specs=[_rows_spec(tt), _rows_spec(tt), _vec_spec(), _rows_spec(tt), _vec_spec()],
        out_specs=[_rows_spec(tt), _vec_spec(), _rows_spec(tt)],
        out_shape=[jax.ShapeDtypeStruct((t, D), F32), jax.ShapeDtypeStruct((SUBLANES, D), F32),
                   jax.ShapeDtypeStruct((t, D), MXU)])(x, target, vec, y, gate)


def _shift_down(x, halo, k):
    y = pltpu.roll(x, k, 0)
    top = jnp.where(lax.broadcasted_iota(jnp.int32, halo.shape, 0) < k, pltpu.roll(halo, k, 0), y[0:SUBLANES, :])
    return jnp.concatenate([top, y[SUBLANES:, :]], axis=0)


def _shift_up(x, halo, k):
    n = x.shape[0]
    y = pltpu.roll(x, n - k, 0)
    bottom = jnp.where(lax.broadcasted_iota(jnp.int32, halo.shape, 0) >= SUBLANES - k, pltpu.roll(halo, SUBLANES - k, 0),
                       y[n - SUBLANES:, :])
    return jnp.concatenate([y[:n - SUBLANES, :], bottom], axis=0)


def _rg_gates(xb, halo, cw_ref, vec_ref, wa_ref, wx_ref, at_start):
    shifted = [xb] + [_shift_down(xb, halo, k) for k in range(1, CONV_WIDTH)]
    xc = vec_ref[0:1, :] + shifted[0] * cw_ref[CONV_WIDTH - 1:CONV_WIDTH, :]
    for k in range(1, CONV_WIDTH):
        xc = xc + shifted[k] * cw_ref[CONV_WIDTH - 1 - k:CONV_WIDTH - k, :]
    heads = [slice(h * HEAD_DIM, (h + 1) * HEAD_DIM) for h in range(HEADS)]
    pa = jnp.concatenate([_dot(xc[:, s], wa_ref[h]) for h, s in enumerate(heads)], axis=1) + vec_ref[1:2, :]
    px = jnp.concatenate([_dot(xc[:, s], wx_ref[h]) for h, s in enumerate(heads)], axis=1) + vec_ref[2:3, :]
    ra, ia = _sigmoid(pa), _sigmoid(px)
    nl = -vec_ref[3:4, :]
    sp = jnp.maximum(nl, 0.0) + jnp.log(1.0 + jnp.exp(-jnp.abs(nl)))
    log_a = (-LRU_C) * ra * sp
    a = jnp.exp(log_a)
    th = jnp.tanh(log_a)
    is_t0 = jnp.logical_and(lax.broadcasted_iota(jnp.int32, xb.shape, 0) == 0, at_start)
    mult = jnp.where(is_t0, 1.0, jnp.sqrt(-2.0 * th / (1.0 - th)))
    return dict(shifted=shifted, xc=xc, ra=ra, ia=ia, sp=sp, a=a, mult=mult, is_t0=is_t0, heads=heads)


def _rg_specs(tt, nt, order):
    blk = tt // SUBLANES
    return dict(
        x=pl.BlockSpec((tt, D), lambda i: (order(i), 0)), y=pl.BlockSpec((tt, D), lambda i: (order(i), 1)),
        halo=pl.BlockSpec((SUBLANES, D), lambda i: (jnp.maximum(order(i) * blk - 1, 0), 0)),
        cw=_vec_spec(CONV_WIDTH), vec=_vec_spec(), w=pl.BlockSpec((HEADS, HEAD_DIM, HEAD_DIM), lambda i: (0, 0, 0)))


def _rg_forward(name, z, cw, vec, wa, wx):
    t = z.shape[0]
    tt = _tile(t, ROW_TILE)
    nt = t // tt
    sp = _rg_specs(tt, nt, lambda i: i)

    def body(zx_ref, zy_ref, halo_ref, cw_ref, vec_ref, wa_ref, wx_ref, p_ref, h_ref, a_s, u_s, carry):
        i = pl.program_id(0)

        @pl.when(i == 0)
        def _():
            carry[...] = jnp.zeros_like(carry)

        halo = jnp.where(i > 0, halo_ref[...], 0.0)
        g = _rg_gates(zx_ref[...], halo, cw_ref, vec_ref, wa_ref, wx_ref, i == 0)
        a_s[...] = g["a"]
        u_s[...] = g["mult"] * (g["ia"] * g["xc"])

        def group(gi, h):
            rows = pl.ds(pl.multiple_of(gi * SUBLANES, SUBLANES), SUBLANES)
            a8, u8 = a_s[rows, :], u_s[rows, :]
            out = []
            for j in range(SUBLANES):
                h = a8[j:j + 1, :] * h + u8[j:j + 1, :]
                out.append(h)
            h_ref[rows, :] = jnp.concatenate(out, axis=0)
            return h

        carry[0:1, :] = lax.fori_loop(0, tt // SUBLANES, group, carry[0:1, :])
        p_ref[...] = (h_ref[...] * _gelu(zy_ref[...])).astype(p_ref.dtype)

    return pl.pallas_call(
        body, name=name, grid=(nt,), in_specs=[sp["x"], sp["y"], sp["halo"], sp["cw"], sp["vec"], sp["w"], sp["w"]],
        out_specs=[_rows_spec(tt), _rows_spec(tt)],
        out_shape=[jax.ShapeDtypeStruct((t, D), MXU), jax.ShapeDtypeStruct((t, D), F32)],
        scratch_shapes=[pltpu.VMEM((tt, D), F32), pltpu.VMEM((tt, D), F32), pltpu.VMEM((SUBLANES, D), F32)],
        compiler_params=_params())(z, z, z, cw, vec, wa, wx)


def _rg_backward(name, dp, z, h, cw, vec, wa, wx):
    t = z.shape[0]
    tt = _tile(t, ROW_TILE)
    nt = t // tt
    rev = lambda i: nt - 1 - i
    sp = _rg_specs(tt, nt, rev)
    rows_rev = pl.BlockSpec((tt, D), lambda i: (rev(i), 0))

    def body(dp_ref, zx_ref, zy_ref, halo_ref, h_ref, hhalo_ref, cw_ref, vec_ref, wa_ref, wx_ref,
             dz_ref, dvec_ref, dcw_ref, dwa_ref, dwx_ref, a_s, d_s, carry, nxt):
        i = pl.program_id(0)
        j = rev(i)

        @pl.when(i == 0)
        def _():
            carry[...] = jnp.zeros_like(carry)
            nxt[...] = jnp.zeros_like(nxt)
            dvec_ref[...] = jnp.zeros_like(dvec_ref)
            dcw_ref[...] = jnp.zeros_like(dcw_ref)
            dwa_ref[...] = jnp.zeros_like(dwa_ref)
            dwx_ref[...] = jnp.zeros_like(dwx_ref)

        halo = jnp.where(j > 0, halo_ref[...], 0.0)
        g = _rg_gates(zx_ref[...], halo, cw_ref, vec_ref, wa_ref, wx_ref, j == 0)
        xc, ra, ia, a, mult = g["xc"], g["ra"], g["ia"], g["a"], g["mult"]
        hv, zy, dpv = h_ref[...], zy_ref[...], dp_ref[...]
        dyb = dpv * hv * _gelu_grad(zy)
        a_s[...] = a
        d_s[...] = dpv * _gelu(zy)

        def group(gi, c):
            rows = pl.ds(pl.multiple_of((tt // SUBLANES - 1 - gi) * SUBLANES, SUBLANES), SUBLANES)
            a8, d8 = a_s[rows, :], d_s[rows, :]
            out = [None] * SUBLANES
            for r in reversed(range(SUBLANES)):
                dht = d8[r:r + 1, :] + c
                out[r] = dht
                c = a8[r:r + 1, :] * dht
            d_s[rows, :] = jnp.concatenate(out, axis=0)
            return c

        carry[0:1, :] = lax.fori_loop(0, tt // SUBLANES, group, carry[0:1, :])
        dht = d_s[...]
        hprev = _shift_down(hv, jnp.where(j > 0, hhalo_ref[...], 0.0), 1)
        ixc = ia * xc
        dlog_a = dht * hprev * a + jnp.where(g["is_t0"], 0.0, dht * ixc * (-(a * a) / mult))
        dia = dht * mult * xc
        dxc = dht * mult * ia
        dra = dlog_a * ((-LRU_C) * g["sp"])
        nl = -vec_ref[3:4, :]
        dvec_ref[3:4, :] += _row_sum(dlog_a * ((-LRU_C) * ra)) * (-_sigmoid(nl))
        dpa = dra * ra * (1.0 - ra)
        dpx = dia * ia * (1.0 - ia)
        dvec_ref[1:2, :] += _row_sum(dpa)
        dvec_ref[2:3, :] += _row_sum(dpx)
        back = []
        for hd, s in enumerate(g["heads"]):
            dwa_ref[hd] += _dot_tn(xc[:, s], dpa[:, s])
            dwx_ref[hd] += _dot_tn(xc[:, s], dpx[:, s])
            back.append(_dot_nt(dpa[:, s], wa_ref[hd]) + _dot_nt(dpx[:, s], wx_ref[hd]))
        dxc = dxc + jnp.concatenate(back, axis=1)
        dvec_ref[0:1, :] += _row_sum(dxc)
        dxb = dxc * cw_ref[CONV_WIDTH - 1:CONV_WIDTH, :]
        for k in range(CONV_WIDTH):
            row = CONV_WIDTH - 1 - k
            dcw_ref[row:row + 1, :] += _row_sum(dxc * g["shifted"][k])
            if k:
                dxb = dxb + _shift_up(dxc, nxt[...], k) * cw_ref[row:row + 1, :]
        nxt[...] = dxc[0:SUBLANES, :]
        dz_ref[:, 0:D] = dxb.astype(dz_ref.dtype)
        dz_ref[:, D:2 * D] = dyb.astype(dz_ref.dtype)

    hhalo = pl.BlockSpec((SUBLANES, D), lambda i: (jnp.maximum(rev(i) * (tt // SUBLANES) - 1, 0), 0))
    wacc = pl.BlockSpec((HEADS, HEAD_DIM, HEAD_DIM), lambda i: (0, 0, 0))
    return pl.pallas_call(
        body, name=name, grid=(nt,),
        in_specs=[rows_rev, sp["x"], sp["y"], sp["halo"], rows_rev, hhalo, sp["cw"], sp["vec"], sp["w"], sp["w"]],
        out_specs=[pl.BlockSpec((tt, 2 * D), lambda i: (rev(i), 0)), _vec_spec(), _vec_spec(), wacc, wacc],
        out_shape=[jax.ShapeDtypeStruct((t, 2 * D), MXU), jax.ShapeDtypeStruct((SUBLANES, D), F32),
                   jax.ShapeDtypeStruct((SUBLANES, D), F32), jax.ShapeDtypeStruct((HEADS, HEAD_DIM, HEAD_DIM), F32),
                   jax.ShapeDtypeStruct((HEADS, HEAD_DIM, HEAD_DIM), F32)],
        scratch_shapes=[pltpu.VMEM((tt, D), F32), pltpu.VMEM((tt, D), F32), pltpu.VMEM((SUBLANES, D), F32),
                        pltpu.VMEM((SUBLANES, D), F32)],
        compiler_params=_params())(dp, z, z, z, h, h, cw, vec, wa, wx)


def _hg_chunk(zq, zf, zi, lb):
    c = HG_CHUNK
    q = _silu(zq)
    sg = _sigmoid(zf)
    fg = lb + (1.0 - lb) * sg
    k = 1.0 - fg
    row, col = lax.broadcasted_iota(jnp.int32, (c, c), 0), lax.broadcasted_iota(jnp.int32, (c, c), 1)
    tri, tri_t = (row >= col).astype(F32), (row <= col).astype(F32)
    b = _dot_exact(tri, jnp.log(fg))
    mid, last = b[c // 2 - 1:c // 2, :], b[c - 1:c, :]
    eq = jnp.exp(jnp.minimum(b - mid, EXP_CLAMP))
    ek = jnp.exp(jnp.minimum(mid - b, EXP_CLAMP))
    eb = jnp.exp(b)
    ed = jnp.exp(last - b)
    return dict(q=q, sg=sg, fg=fg, k=k, v=zi, eq=eq, ek=ek, eb=eb, ed=ed, elast=jnp.exp(last), tri=tri, tri_t=tri_t,
                qe=q * eq, ke=k * ek, qb=q * eb, kd=k * ed)


def _hg_specs(tt, order):
    return [pl.BlockSpec((tt, D), lambda i, col=col: (order(i), col)) for col in range(4)]


def _hg_forward(name, z, vec):
    t = z.shape[0]
    tt = _tile(t, ROW_TILE)
    nt, nc = t // tt, tt // HG_CHUNK
    heads = [slice(h * HEAD_DIM, (h + 1) * HEAD_DIM) for h in range(HEADS)]

    def body(zq_ref, zf_ref, zi_ref, zg_ref, vec_ref, p_ref, o_ref, ss_ref, state):
        @pl.when(pl.program_id(0) == 0)
        def _():
            state[...] = jnp.zeros_like(state)

        lb, gain = vec_ref[0:1, :], vec_ref[1:2, :]

        def chunk(ci, carry):
            rows = pl.ds(pl.multiple_of(ci * HG_CHUNK, HG_CHUNK), HG_CHUNK)
            g = _hg_chunk(zq_ref[rows, :], zf_ref[rows, :], zi_ref[rows, :], lb)
            zg = zg_ref[rows, :]
            causal = g["tri"] > 0.0
            o_parts, p_parts = [], []
            for hd, s in enumerate(heads):
                st = state[hd]
                ss_ref[ci, hd] = st
                att = jnp.where(causal, _dot_nt(g["qe"][:, s], g["ke"][:, s]), 0.0)
                o = _dot(att, g["v"][:, s]) + _dot_nt(g["qb"][:, s], st)
                state[hd] = st * g["elast"][:, s] + _dot_tn(g["v"][:, s], g["kd"][:, s])
                r = lax.rsqrt(jnp.mean(o * o, axis=-1, keepdims=True) + GNORM_EPS)
                o_parts.append(o)
                p_parts.append((o * r) * gain[:, s])
            o_ref[rows, :] = jnp.concatenate(o_parts, axis=1)
            p_ref[rows, :] = (jnp.concatenate(p_parts, axis=1) * _silu(zg)).astype(p_ref.dtype)
            return carry

        lax.fori_loop(0, nc, chunk, 0)

    return pl.pallas_call(
        body, name=name, grid=(nt,), in_specs=_hg_specs(tt, lambda i: i) + [_vec_spec()],
        out_specs=[_rows_spec(tt), _rows_spec(tt), pl.BlockSpec((nc, HEADS, HEAD_DIM, HEAD_DIM), lambda i: (i, 0, 0, 0))],
        out_shape=[jax.ShapeDtypeStruct((t, D), MXU), jax.ShapeDtypeStruct((t, D), F32),
                   jax.ShapeDtypeStruct((t // HG_CHUNK, HEADS, HEAD_DIM, HEAD_DIM), F32)],
        scratch_shapes=[pltpu.VMEM((HEADS, HEAD_DIM, HEAD_DIM), F32)], compiler_params=_params())(z, z, z, z, vec)


def _hg_backward(name, dp, z, o, ss, vec):
    t = z.shape[0]
    tt = _tile(t, ROW_TILE)
    nt, nc = t // tt, tt // HG_CHUNK
    rev = lambda i: nt - 1 - i
    heads = [slice(h * HEAD_DIM, (h + 1) * HEAD_DIM) for h in range(HEADS)]
    rows_rev = pl.BlockSpec((tt, D), lambda i: (rev(i), 0))

    def body(dp_ref, zq_ref, zf_ref, zi_ref, zg_ref, o_ref, ss_ref, vec_ref, dz_ref, acc_ref,
             dstate, dqe_s, dke_s, dqb_s, dkd_s, dv_s, dzg_s, dlast_s):
        @pl.when(pl.program_id(0) == 0)
        def _():
            dstate[...] = jnp.zeros_like(dstate)
            acc_ref[...] = jnp.zeros_like(acc_ref)

        lb, gain = vec_ref[0:1, :], vec_ref[1:2, :]

        def chunk(cr, carry):
            ci = nc - 1 - cr
            rows = pl.ds(pl.multiple_of(ci * HG_CHUNK, HG_CHUNK), HG_CHUNK)
            zq, zg = zq_ref[rows, :], zg_ref[rows, :]
            g = _hg_chunk(zq, zf_ref[rows, :], zi_ref[rows, :], lb)
            ov, dpv = o_ref[rows, :], dp_ref[rows, :]
            causal = g["tri"] > 0.0
            don = dpv * _silu(zg)
            dgate = dpv * _silu_grad(zg)
            for hd, s in enumerate(heads):
                oh = ov[:, s]
                r = lax.rsqrt(jnp.mean(oh * oh, axis=-1, keepdims=True) + GNORM_EPS)
                on = oh * r
                dzg_s[:, s] = dgate[:, s] * (on * gain[:, s])
                acc_ref[1:2, s] += _row_sum(don[:, s] * on)
                dtmp = don[:, s] * gain[:, s]
                do = r * (dtmp - on * jnp.mean(dtmp * on, axis=-1, keepdims=True))
                st, dst = ss_ref[ci, hd], dstate[hd]
                qe, ke, qb, kd, v = g["qe"][:, s], g["ke"][:, s], g["qb"][:, s], g["kd"][:, s], g["v"][:, s]
                att = jnp.where(causal, _dot_nt(qe, ke), 0.0)
                datt = jnp.where(causal, _dot_nt(do, v), 0.0)
                dv_s[:, s] = _dot_tn(att, do) + _dot_nt(kd, dst)
                dqe_s[:, s] = _dot(datt, ke)
                dke_s[:, s] = _dot_tn(datt, qe)
                dqb_s[:, s] = _dot(do, st)
                dkd_s[:, s] = _dot(v, dst)
                dlast_s[0:1, s] = g["elast"][:, s] * _row_sum(dst * st)
                dstate[hd] = dst * g["elast"][:, s] + _dot_tn(do, qb)
            dqe, dke, dqb, dkd = dqe_s[...], dke_s[...], dqb_s[...], dkd_s[...]
            dq = dqe * g["eq"] + dqb * g["eb"]
            dk = dke * g["ek"] + dkd * g["ed"]
            dkdk = dkd * g["kd"]
            db = dqe * g["qe"].astype(MXU).astype(F32) - dke * g["ke"].astype(MXU).astype(F32) + dqb * g["qb"] - dkdk
            dlogf = _dot_exact(g["tri_t"], db) + (dlast_s[0:1, :] + _row_sum(dkdk))
            dfg = dlogf / g["fg"] - dk
            sg = g["sg"]
            acc_ref[0:1, :] += _row_sum(dfg * (1.0 - sg))
            dz_ref[rows, 0:D] = (dq * _silu_grad(zq)).astype(dz_ref.dtype)
            dz_ref[rows, D:2 * D] = (dfg * (1.0 - lb) * sg * (1.0 - sg)).astype(dz_ref.dtype)
            dz_ref[rows, 2 * D:3 * D] = dv_s[...].astype(dz_ref.dtype)
            dz_ref[rows, 3 * D:4 * D] = dzg_s[...].astype(dz_ref.dtype)
            return carry

        lax.fori_loop(0, nc, chunk, 0)

    chunk_buf = pltpu.VMEM((HG_CHUNK, D), F32)
    return pl.pallas_call(
        body, name=name, grid=(nt,),
        in_specs=[rows_rev] + _hg_specs(tt, rev) + [rows_rev, pl.BlockSpec((nc, HEADS, HEAD_DIM, HEAD_DIM), lambda i: (rev(i), 0, 0, 0)),
                                                  _vec_spec()],
        out_specs=[pl.BlockSpec((tt, 4 * D), lambda i: (rev(i), 0)), _vec_spec()],
        out_shape=[jax.ShapeDtypeStruct((t, 4 * D), MXU), jax.ShapeDtypeStruct((SUBLANES, D), F32)],
        scratch_shapes=[pltpu.VMEM((HEADS, HEAD_DIM, HEAD_DIM), F32)] + [chunk_buf] * 6 + [pltpu.VMEM((SUBLANES, D), F32)],
        compiler_params=_params())(dp, z, z, z, z, o, ss, vec)


def _mod_forward(name, c_all, mod_w, lower):
    depth, _, sw = mod_w.shape

    def body(c_ref, w_ref, lo_ref, cs_ref, mod_ref, lb_ref):
        cs = _silu(c_ref[...])
        mod_ref[...] = _dot(cs, w_ref[...])

        @pl.when(pl.program_id(0) == 0)
        def _():
            cs_ref[...] = cs
            lo = lo_ref[...]
            e = jnp.exp(lo - jnp.max(lo, axis=0, keepdims=True))
            sm = e / jnp.sum(e, axis=0, keepdims=True)
            lb_ref[0:1, :] = jnp.zeros((1, D), F32)
            for l in range(1, depth):
                lb_ref[l:l + 1, :] = lb_ref[l - 1:l, :] + sm[l:l + 1, :]

    return pl.pallas_call(
        body, name=name, grid=(depth,),
        in_specs=[pl.BlockSpec((N_DEV, D), lambda l: (0, 0)), pl.BlockSpec((None, D, sw), lambda l: (l, 0, 0)),
                  pl.BlockSpec((depth, D), lambda l: (0, 0))],
        out_specs=[pl.BlockSpec((N_DEV, D), lambda l: (0, 0)), pl.BlockSpec((None, N_DEV, sw), lambda l: (l, 0, 0)),
                   pl.BlockSpec((depth, D), lambda l: (0, 0))],
        out_shape=[jax.ShapeDtypeStruct((N_DEV, D), F32), jax.ShapeDtypeStruct((depth, N_DEV, sw), F32),
                   jax.ShapeDtypeStruct((depth, D), F32)], compiler_params=_params())(c_all, mod_w, lower)


def _mod_weight_grad(name, cs_t, dmod):
    depth, pad, sw = dmod.shape

    def body(c_ref, d_ref, g_ref):
        g_ref[...] = _dot(c_ref[...], d_ref[...])

    return pl.pallas_call(
        body, name=name, grid=(depth,),
        in_specs=[pl.BlockSpec((D, pad), lambda l: (0, 0)), pl.BlockSpec((None, pad, sw), lambda l: (l, 0, 0))],
        out_specs=pl.BlockSpec((None, D, sw), lambda l: (l, 0, 0)),
        out_shape=jax.ShapeDtypeStruct((depth, D, sw), F32), compiler_params=_params())(cs_t, dmod)


def _lower_bound_grad(name, lower, dlb):
    depth = lower.shape[0]

    def body(lo_ref, d_ref, out_ref):
        lo = lo_ref[...]
        e = jnp.exp(lo - jnp.max(lo, axis=0, keepdims=True))
        sm = e / jnp.sum(e, axis=0, keepdims=True)
        out_ref[...] = jnp.zeros_like(out_ref)
        dsm = [jnp.zeros((1, D), F32)]
        for l in range(1, depth):
            tail = d_ref[l:l + 1, :]
            for m in range(l + 1, depth):
                tail = tail + d_ref[m:m + 1, :]
            dsm.append(tail)
        inner = sm[1:2, :] * dsm[1]
        for l in range(2, depth):
            inner = inner + sm[l:l + 1, :] * dsm[l]
        for l in range(depth):
            out_ref[l:l + 1, :] = sm[l:l + 1, :] * (dsm[l] - inner)
        for j in range(SUBLANES - depth):
            row = d_ref[depth + j:depth + j + 1, :]
            tot = row[:, 0:HEAD_DIM]
            for hd in range(1, HEADS):
                tot = tot + row[:, hd * HEAD_DIM:(hd + 1) * HEAD_DIM]
            out_ref[depth + j:depth + j + 1, 0:HEAD_DIM] = tot

    return pl.pallas_call(body, name=name, out_shape=jax.ShapeDtypeStruct((SUBLANES, D), F32))(lower, dlb)


def _adamw(name, parts, w, m, v, layer=None, prev=None):
    p, r, c = parts.shape
    tr = _tile(r, max(SUBLANES, ADAMW_STEP_BYTES // (4 * c * (p + 7))))
    stacked = layer is not None
    n_prev = 4 if prev is not None else 0

    def body(*refs):
        parts_ref, w_ref, m_ref, v_ref = refs[:4]
        g_ref, d_ref, m_out, v_out = refs[4 + n_prev:]
        g = parts_ref[0].astype(F32)
        for q in range(1, p):
            g = g + parts_ref[q].astype(F32)
        m2 = ADAM_B1 * m_ref[...] + (1.0 - ADAM_B1) * g
        v2 = ADAM_B2 * v_ref[...] + (1.0 - ADAM_B2) * (g * g)
        m_hat = m2 / (1.0 - ADAM_B1 ** ADAM_STEP)
        v_hat = v2 / (1.0 - ADAM_B2 ** ADAM_STEP)
        g_ref[...] = g
        d_ref[...] = -ADAM_LR * (m_hat / (jnp.sqrt(v_hat) + ADAM_EPS) + ADAM_WD * w_ref[...])
        m_out[...] = m2
        v_out[...] = v2

    if stacked:
        spec = pl.BlockSpec((None, tr, c), lambda i: (layer, i, 0))
    else:
        spec = pl.BlockSpec((tr, c), lambda i: (i, 0))
    return pl.pallas_call(
        body, name=name, grid=(r // tr,),
        in_specs=[pl.BlockSpec((p, tr, c), lambda i: (0, i, 0)), spec, spec, spec] + [ANY] * n_prev, out_specs=[spec] * 4,
        out_shape=[jax.ShapeDtypeStruct(w.shape, F32)] * 4,
        input_output_aliases={4 + q: q for q in range(n_prev)}, compiler_params=_params(),
    )(parts, w, m, v, *(prev or []))


def _hbm(a):
    return pltpu.with_memory_space_constraint(a, pltpu.HBM)


def _vec(*rows):
    rows = [r.reshape(1, D).astype(F32) for r in rows]
    packed = jnp.concatenate(rows + [jnp.zeros((SUBLANES - len(rows), D), F32)], axis=0)
    return _hbm(packed)


def _pad_rows(a, rows=SUBLANES):
    a = a.reshape(-1, a.shape[-1])
    return jnp.concatenate([a, jnp.zeros((rows - a.shape[0], a.shape[1]), a.dtype)], axis=0) if a.shape[0] < rows else a


def kernel(x, c, mod_w, mod_b, norm_mix, norm_mlp, norm_final, rg_w_in, rg_conv_w, rg_conv_b, rg_w_a, rg_b_a, rg_w_x, rg_b_x, rg_lambda, rg_w_out, hg_w_in, hg_lower_bounds, hg_gnorm, hg_w_out, mlp_w1, mlp_w2, loss_target, m_mod_w, m_mod_b, m_norm_mix, m_norm_mlp, m_norm_final, m_rg_w_in, m_rg_conv_w, m_rg_conv_b, m_rg_w_a, m_rg_b_a, m_rg_w_x, m_rg_b_x, m_rg_lambda, m_rg_w_out, m_hg_w_in, m_hg_lower_bounds, m_hg_gnorm, m_hg_w_out, m_mlp_w1, m_mlp_w2, v_mod_w, v_mod_b, v_norm_mix, v_norm_mlp, v_norm_final, v_rg_w_in, v_rg_conv_w, v_rg_conv_b, v_rg_w_a, v_rg_b_a, v_rg_w_x, v_rg_b_x, v_rg_lambda, v_rg_w_out, v_hg_w_in, v_hg_lower_bounds, v_hg_gnorm, v_hg_w_out, v_mlp_w1, v_mlp_w2):
    me = 4 * lax.axis_index("x") + 2 * lax.axis_index("y") + lax.axis_index("c")
    x0 = x[0]
    target = loss_target[0]
    n_rg, n_hg = rg_w_in.shape[0], hg_w_in.shape[0]
    sw_mod = mod_w.shape[2]

    c_all, cw_all = _all_gather([_pad_rows(c), rg_conv_w.reshape(n_rg * CONV_WIDTH, -1)], "gather_cond")
    c_all = c_all[:, 0, :]
    conv_w = cw_all.transpose(1, 0, 2).reshape(n_rg, CONV_WIDTH, D)
    cs_all, mod_part, lb_all = _mod_forward("mod_forward", c_all, mod_w, hg_lower_bounds)
    (mod_gathered,) = _all_gather([mod_part.reshape(DEPTH * N_DEV, sw_mod)], "gather_mod")

    ids = iter(range(5 * DEPTH))
    shards = []
    for layer in range(DEPTH):
        j = layer // 2
        w_in, w_out = (rg_w_in[j], rg_w_out[j]) if layer % 2 == 0 else (hg_w_in[j], hg_w_out[j])
        shards.append([w_in.astype(MXU), w_out.astype(MXU), mlp_w1[layer].astype(MXU), mlp_w2[layer].astype(MXU)])
    shards, mod_gathered = lax.optimization_barrier((shards, mod_gathered))
    weights = []
    for layer in range(DEPTH):
        g_in, g_out = _all_gather_async(shards[layer][:2], f"gather_mixer_weights_{layer}", next(ids), ["cols", "rows"])
        g_w1, g_w2 = _all_gather_async(shards[layer][2:], f"gather_mlp_weights_{layer}", next(ids), ["cols", "rows"])
        weights.append(dict(w_in=g_in, w_out=g_out, w1=g_w1, w2=g_w2))

    mod_mine = lax.dynamic_index_in_dim(mod_gathered.reshape(N_DEV, DEPTH, N_DEV, sw_mod), me, axis=2, keepdims=False)
    mod = mod_mine.transpose(1, 0, 2).reshape(DEPTH, 6, D) + mod_b.reshape(DEPTH, 6, D)

    saved = []
    xl = x0
    for layer in range(DEPTH):
        j, wt = layer // 2, weights[layer]
        is_rg = layer % 2 == 0
        s = dict(x=xl)
        s["vec_mix"] = _vec(norm_mix[layer], mod[layer, 1], mod[layer, 0])
        s["vec_mlp"] = _vec(norm_mlp[layer], mod[layer, 4], mod[layer, 3])
        s["gate_mix"], s["gate_mlp"] = _vec(mod[layer, 2]), _vec(mod[layer, 5])
        s["h"] = _hbm(_norm_mod("norm_mod", xl, s["vec_mix"]))
        if is_rg:
            (s["z"],) = _mm_tokens("rg_in", s["h"], wt["w_in"], False, (F32,))
            s["cw"] = conv_w[j]
            s["vec"] = _vec(rg_conv_b[j], rg_b_a[j], rg_b_x[j], rg_lambda[j])
            s["wa"], s["wx"] = rg_w_a[j].astype(MXU), rg_w_x[j].astype(MXU)
            s["p"], s["hr"] = _rg_forward("rg_forward", s["z"], s["cw"], s["vec"], s["wa"], s["wx"])
        else:
            (s["z"],) = _mm_tokens("hg_in", s["h"], wt["w_in"], False, (F32,))
            s["vec"] = _vec(lb_all[layer], jnp.tile(hg_gnorm[j], HEADS))
            s["p"], s["o"], s["ss"] = _hg_forward("hg_forward", s["z"], s["vec"])
        s["y"], s["x1"] = _mm_tokens("mix_out", s["p"], wt["w_out"], False, (F32, F32), _ep_residual, extras=(xl, s["gate_mix"]))
        s["p"] = _hbm(s["p"])
        s["h2"] = _hbm(_norm_mod("norm_mod", s["x1"], s["vec_mlp"]))
        (s["r"],) = _mm_tokens("mlp_in", s["h2"], wt["w1"], False, (MXU,), _ep_relu)
        s["ff"], xl = _mm_tokens("mlp_out", s["r"], wt["w2"], False, (F32, F32), _ep_residual,
                                 extras=(s["x1"], s["gate_mlp"]), prologue=_square)
        saved.append(s)

    dx, head, dff = _loss_head("loss_head", xl, target, _vec(norm_final), saved[-1]["ff"], saved[-1]["gate_mlp"])
    g_mlp = head[GATE_ROW:GATE_ROW + 1]

    results = {}
    big = dict(rg_w_in=(rg_w_in, m_rg_w_in, v_rg_w_in), rg_w_out=(rg_w_out, m_rg_w_out, v_rg_w_out),
               hg_w_in=(hg_w_in, m_hg_w_in, v_hg_w_in), hg_w_out=(hg_w_out, m_hg_w_out, v_hg_w_out),
               mlp_w1=(mlp_w1, m_mlp_w1, v_mlp_w1), mlp_w2=(mlp_w2, m_mlp_w2, v_mlp_w2))

    def update_layer(layer, landed):
        mixer = "rg" if layer % 2 == 0 else "hg"
        l_in, l_out, l_w1, l_w2 = landed
        for nm, parts, idx in ((f"{mixer}_w_in", l_in, layer // 2), (f"{mixer}_w_out", l_out, layer // 2),
                               ("mlp_w1", l_w1, layer), ("mlp_w2", l_w2, layer)):
            w, m, v = big[nm]
            results[nm] = _adamw(f"adamw_{nm}", parts, w, m, v, layer=idx, prev=results.get(nm))

    landed = [None] * DEPTH
    small = [None] * DEPTH
    parity = lax.axis_index("c").astype(jnp.int32).reshape(1)

    def send_chip_sums(layer, grads, from_sibling, anchor):
        sums = _pair_sum("pair_sum", parity, grads, from_sibling)
        sums, anchor = lax.optimization_barrier((sums, anchor))
        landed[layer] = _chip_exchange_async(sums, f"exchange_grads_{layer}", next(ids))
        return anchor

    pending = None
    for layer in reversed(range(DEPTH)):
        j, wt, s = layer // 2, weights[layer], saved[layer]
        is_rg = layer % 2 == 0
        dff = _hbm(dff)
        (da,) = _mm_tokens("mlp_out_t", dff, wt["w2"], True, (MXU,), _ep_relu2_grad, extras=(s["r"],))
        dw2 = _mm_grad("mlp_out_grad", s["r"], dff, False, prologue=_square).reshape(N_DEV, -1, D)
        dw1 = _mm_grad("mlp_in_grad", s["h2"], da, True)
        if pending is not None:
            da = send_chip_sums(*pending, da)
        (dh2,) = _mm_tokens("mlp_in_t", da, wt["w1"], True, (F32,))
        dx1, n_mlp, dyb = _norm_mod_grad("norm_mod_grad", dh2, s["x1"], s["vec_mlp"], dx, s["y"], s["gate_mix"])
        dyb = _hbm(dyb)
        (dp,) = _mm_tokens("mix_out_t", dyb, wt["w_out"], True, (F32,))
        dw_out = _hbm(_mm_grad("mix_out_grad", s["p"], dyb, False).reshape(N_DEV, -1, D))
        if is_rg:
            dz, dvec, dcw, dwa, dwx = _rg_backward("rg_backward", dp, s["z"], s["hr"], s["cw"], s["vec"], s["wa"], s["wx"])
            gate_grads, dz = lax.optimization_barrier(
                ([dwa.reshape(-1, HEAD_DIM).astype(MXU), dwx.reshape(-1, HEAD_DIM).astype(MXU)], dz))
            gate_parts = _all_gather_async(gate_grads, f"gather_gate_grads_{j}", next(ids), ["stack", "stack"])
            mixer_small = dict(dvec=dvec, dcw=dcw, gate_parts=gate_parts)
            dw_in = _mm_grad("rg_in_grad", s["h"], dz, True)
            (dh,) = _mm_tokens("rg_in_t", dz, wt["w_in"], True, (F32,))
        else:
            dz, dvec = _hg_backward("hg_backward", dp, s["z"], s["o"], s["ss"], s["vec"])
            mixer_small = dict(dvec=dvec)
            dw_in = _mm_grad("hg_in_grad", s["h"], dz, True)
            (dh,) = _mm_tokens("hg_in_t", dz, wt["w_in"], True, (F32,))
        grads = [dw_in, dw_out, dw1, dw2]
        pending = (layer, grads, _sibling_send_async(grads, f"pair_grads_{layer}", next(ids)))
        small[layer] = dict(g_mlp=g_mlp, n_mlp=n_mlp, g_mix=n_mlp[GATE_ROW:GATE_ROW + 1], **mixer_small)
        if layer:
            below = saved[layer - 1]
            dx, n_mix, dff = _norm_mod_grad("norm_mod_grad", dh, s["x"], s["vec_mix"], dx1, below["ff"], below["gate_mlp"])
            g_mlp = n_mix[GATE_ROW:GATE_ROW + 1]
        else:
            dx, n_mix = _norm_mod_grad("norm_mod_grad_first", dh, s["x"], s["vec_mix"], dx1)
        small[layer]["n_mix"] = n_mix
        if layer + 1 < DEPTH:
            stream = (dx, dff) if layer else (dx,)
            landed[layer + 1], stream = lax.optimization_barrier((landed[layer + 1], stream))
            if "gate_parts" in small[layer + 1]:
                small[layer + 1]["gate_parts"], stream = lax.optimization_barrier((small[layer + 1]["gate_parts"], stream))
            dx, dff = stream if layer else (stream[0], None)
            update_layer(layer + 1, landed[layer + 1])

    dlb_rows = [jnp.zeros((1, D), F32) if l % 2 == 0 else small[l]["dvec"][0:1] for l in range(DEPTH)]
    dgn_rows = [small[2 * j + 1]["dvec"][1:2] for j in range(n_hg)]
    lb_grad = _lower_bound_grad("lower_bound_grad", hg_lower_bounds, _pad_rows(jnp.concatenate(dlb_rows + dgn_rows, axis=0)))
    dmod = jnp.stack([jnp.concatenate([small[l]["n_mix"][2], small[l]["n_mix"][1], small[l]["g_mix"][0],
                                       small[l]["n_mlp"][2], small[l]["n_mlp"][1], small[l]["g_mlp"][0]]) for l in range(DEPTH)])
    groups = [
        dmod.reshape(DEPTH * 6, D),
        jnp.stack([small[l]["n_mix"][0] for l in range(DEPTH)]),
        jnp.stack([small[l]["n_mlp"][0] for l in range(DEPTH)]),
        head[0:1],
        jnp.stack([small[2 * j]["dvec"][0] for j in range(n_rg)]),
        jnp.stack([small[2 * j]["dvec"][1] for j in range(n_rg)]),
        jnp.stack([small[2 * j]["dvec"][2] for j in range(n_rg)]),
        jnp.stack([small[2 * j]["dvec"][3] for j in range(n_rg)]),
        lb_grad[0:DEPTH],
        jnp.concatenate([lb_grad[DEPTH + j:DEPTH + j + 1, 0:HEAD_DIM] for j in range(n_hg)]
                        + [jnp.zeros((1, D - n_hg * HEAD_DIM), F32)], axis=1),
        jnp.concatenate([small[2 * j]["dcw"][0:CONV_WIDTH] for j in range(n_rg)], axis=0),
        head[1:2],
    ]
    params = [mod_b, norm_mix, norm_mlp, norm_final, rg_conv_b, rg_b_a, rg_b_x, rg_lambda, hg_lower_bounds, hg_gnorm]
    moms = [m_mod_b, m_norm_mix, m_norm_mlp, m_norm_final, m_rg_conv_b, m_rg_b_a, m_rg_b_x, m_rg_lambda, m_hg_lower_bounds, m_hg_gnorm]
    vars_ = [v_mod_b, v_norm_mix, v_norm_mlp, v_norm_final, v_rg_conv_b, v_rg_b_a, v_rg_b_x, v_rg_lambda, v_hg_lower_bounds, v_hg_gnorm]
    offsets, rows_of, at = [], [], 0
    for g in groups:
        offsets.append(at)
        rows_of.append(g.shape[0])
        at += -(-g.shape[0] // SUBLANES) * SUBLANES
    packed = jnp.concatenate([_pad_rows(g, -(-g.shape[0] // SUBLANES) * SUBLANES) for g in groups], axis=0)
    (small_parts,) = _all_gather([packed], "gather_small_grads")
    last_layer, last_grads, from_sibling = pending
    from_sibling, small_parts = lax.optimization_barrier((from_sibling, small_parts))
    send_chip_sums(last_layer, last_grads, from_sibling, dx)

    def pack_like(arrs):
        out = []
        for g_rows, off, a in zip(rows_of, offsets, arrs):
            flat = a.reshape(-1)
            flat = jnp.concatenate([flat, jnp.zeros((g_rows * D - flat.shape[0],), F32)])
            out.append(_pad_rows(flat.reshape(g_rows, D), -(-g_rows // SUBLANES) * SUBLANES))
        rest = packed.shape[0] - sum(o.shape[0] for o in out)
        return jnp.concatenate(out + [jnp.zeros((rest, D), F32)], axis=0)

    small_out = _adamw("adamw_small", small_parts, pack_like(params), pack_like(moms), pack_like(vars_))

    def unpack(q, idx, like):
        rows = small_out[q][offsets[idx]:offsets[idx] + rows_of[idx]]
        return rows.reshape(-1)[:like.size].reshape(like.shape)

    loss = jnp.sum(small_out[0][offsets[11]])
    names =["mod_b", "norm_mix", "norm_mlp", "norm_final", "rg_conv_b", "rg_b_a", "rg_b_x", "rg_lambda", "hg_lower_bounds", "hg_gnorm"]
    for idx, (nm, like) in enumerate(zip(names, params)):
        results[nm] = [unpack(q, idx, like) for q in range(4)]

    cw_parts = lax.dynamic_slice_in_dim(small_parts[:, offsets[10]:offsets[10] + n_rg * CONV_WIDTH, :], me * (D // N_DEV), D // N_DEV, axis=2)
    shp = rg_conv_w.shape
    results["rg_conv_w"] = [o.reshape(shp) for o in _adamw(
        "adamw_conv", cw_parts, rg_conv_w.reshape(-1, shp[-1]), m_rg_conv_w.reshape(-1, shp[-1]), v_rg_conv_w.reshape(-1, shp[-1]))]
    shp = rg_w_a.shape
    stacked = (n_rg, HEADS * HEAD_DIM, HEAD_DIM)
    for nm, which, (w, m, v) in (("rg_w_a", 0, (rg_w_a, m_rg_w_a, v_rg_w_a)), ("rg_w_x", 1, (rg_w_x, m_rg_w_x, v_rg_w_x))):
        out = None
        for j in reversed(range(n_rg)):
            out = _adamw("adamw_gate", small[2 * j]["gate_parts"][which], w.reshape(stacked), m.reshape(stacked),
                         v.reshape(stacked), layer=j, prev=out)
        results[nm] = [o.reshape(shp) for o in out]

    dmod_all = small_parts[:, 0:DEPTH * 6, :].reshape(N_DEV, DEPTH, 6 * D)
    dmod_cols = lax.dynamic_slice_in_dim(dmod_all, me * sw_mod, sw_mod, axis=2).transpose(1, 0, 2)
    pad = HEAD_DIM - N_DEV
    dmod_pad = jnp.concatenate([dmod_cols, jnp.zeros((DEPTH, pad, sw_mod), F32)], axis=1).astype(MXU)
    cs_t = jnp.concatenate([cs_all.T, jnp.zeros((D, pad), F32)], axis=1).astype(MXU)
    g_mod_w = _mod_weight_grad("mod_weight_grad", cs_t, dmod_pad)
    results["mod_w"] = [o.reshape(mod_w.shape) for o in _adamw(
        "adamw_mod", g_mod_w.reshape(1, -1, sw_mod), mod_w.reshape(-1, sw_mod), m_mod_w.reshape(-1, sw_mod), v_mod_w.reshape(-1, sw_mod))]

    update_layer(0, landed[0])

    order = ["mod_w", "mod_b", "norm_mix", "norm_mlp", "norm_final", "rg_w_in", "rg_conv_w", "rg_conv_b", "rg_w_a", "rg_b_a", "rg_w_x",
             "rg_b_x", "rg_lambda", "rg_w_out", "hg_w_in", "hg_lower_bounds", "hg_gnorm", "hg_w_out", "mlp_w1", "mlp_w2"]
    return (loss, dx[None], *[results[n][0] for n in order], *[results[n][1] for n in order],
            *[results[n][2] for n in order], *[results[n][3] for n in order])
```

```python
import functools

import jax
import jax.numpy as jnp
from jax import lax
from jax.experimental import pallas as pl
from jax.experimental.pallas import tpu as pltpu
from jax.experimental.pallas import tpu_sc as plsc

F32 = jnp.float32
MXU = jnp.bfloat16

N_DEV = 8
D = 1024
DEPTH = 4
HEADS = 8
HEAD_DIM = 128
CONV_WIDTH = 4
LRU_C = 8.0
HG_CHUNK = 64
NORM_EPS = 1e-6
GNORM_EPS = 1e-5
ADAM_LR = 0.001
ADAM_B1 = 0.9
ADAM_B2 = 0.999
ADAM_EPS = 1e-08
ADAM_WD = 0.01
ADAM_STEP = 10
GELU_C = 0.7978845608028654
GELU_K = 0.044715
EXP_CLAMP = 80.0
SUBLANES = 8
VMEM_LIMIT = 48 * 1024 * 1024
ROW_TILE = 256
MM_TILE = 1024
MM_TOKENS = 512
MLP_TOKENS = 256
ADAMW_STEP_BYTES = 8 * 1024 * 1024

MESH = pl.DeviceIdType.MESH
ANY = pl.BlockSpec(memory_space=pl.ANY)


def _params():
    return pltpu.CompilerParams(vmem_limit_bytes=VMEM_LIMIT)


def _tile(n, target):
    if n <= target:
        return n
    t = target // SUBLANES * SUBLANES
    while n % t:
        t -= SUBLANES
    return t


def _sigmoid(x):
    return 1.0 / (1.0 + jnp.exp(-x))


def _silu(x):
    return x * _sigmoid(x)


def _silu_and_grad(x):
    s = _sigmoid(x)
    return x * s, s * (1.0 + x * (1.0 - s))


def _gelu(y):
    return 0.5 * y * (1.0 + jnp.tanh(GELU_C * (y + GELU_K * y * y * y)))


def _gelu_and_grad(y):
    y2 = y * y
    th = jnp.tanh(GELU_C * (y + GELU_K * y2 * y))
    half = 0.5 * (1.0 + th)
    return y * half, half + 0.5 * y * (1.0 - th * th) * GELU_C * (1.0 + 3.0 * GELU_K * y2)


def _dot(a, b):
    return lax.dot_general(a.astype(MXU), b.astype(MXU), (((1,), (0,)), ((), ())), preferred_element_type=F32)


def _dot_nt(a, b):
    return lax.dot_general(a.astype(MXU), b.astype(MXU), (((1,), (1,)), ((), ())), preferred_element_type=F32)


def _dot_tn(a, b):
    return lax.dot_general(a.astype(MXU), b.astype(MXU), (((0,), (0,)), ((), ())), preferred_element_type=F32)


def _dot_exact(tri, x):
    t = tri.astype(MXU)
    hi = x.astype(MXU)
    r1 = x - hi.astype(F32)
    mid = r1.astype(MXU)
    lo = (r1 - mid.astype(F32)).astype(MXU)
    dn = (((1,), (0,)), ((), ()))
    return (lax.dot_general(t, hi, dn, preferred_element_type=F32) + lax.dot_general(t, mid, dn, preferred_element_type=F32)
            + lax.dot_general(t, lo, dn, preferred_element_type=F32))


def _row_sum(v):
    return jnp.sum(v, axis=0, keepdims=True)


def _handshake(partners):
    barrier = pltpu.get_barrier_semaphore()
    for p in partners:
        pl.semaphore_signal(barrier, inc=1, device_id=p, device_id_type=MESH)
    pl.semaphore_wait(barrier, len(partners))


def _gather_body(n, per_array_sems, handshake, layouts):
    def body(*refs):
        ins, outs = refs[:n], refs[n:2 * n]
        send_sems, recv_sems, local_sems = refs[2 * n:]
        x, y, c = lax.axis_index("x"), lax.axis_index("y"), lax.axis_index("c")
        me, sibling = (x, y, c), (x, y, 1 - c)
        chips = [(1 - x, y), (x, 1 - y), (1 - x, 1 - y)]
        if handshake:
            _handshake([sibling] + [(*chip, c) for chip in chips])

        def sem(sems, a, k):
            return sems.at[a, k] if per_array_sems else sems.at[k]

        def slot(a, p):
            block = 4 * p[0] + 2 * p[1] + p[2]
            r, c_ = ins[a].shape
            if layouts[a] == "rows":
                return outs[a].at[pl.ds(block * r, r), :]
            if layouts[a] == "cols":
                return outs[a].at[:, pl.ds(block * c_, c_)]
            return outs[a].at[block]

        def copy(a, k, block, to, src=None):
            return pltpu.make_async_remote_copy(
                src_ref=slot(a, block) if src is None else src, dst_ref=slot(a, block),
                send_sem=sem(send_sems, a, k), recv_sem=sem(recv_sems, a, k), device_id=to, device_id_type=MESH)

        mine = [pltpu.make_async_copy(ins[a], slot(a, me), local_sems.at[a if per_array_sems else 0]) for a in range(n)]
        for cp in mine:
            cp.start()
        first = []
        for a in range(n):
            first.append(copy(a, 0, me, sibling, src=ins[a]))
            first += [copy(a, 1 + j, me, (*chip, c), src=ins[a]) for j, chip in enumerate(chips)]
        for cp in first:
            cp.start()
        passed = []
        for j, chip in enumerate(chips):
            for a in range(n):
                copy(a, 1 + j, (*chip, c), me).wait_recv()
            for a in range(n):
                cp = copy(a, 4 + j, (*chip, c), sibling)
                cp.start()
                passed.append(cp)
        for a in range(n):
            copy(a, 0, sibling, me).wait_recv()
        for j, chip in enumerate(chips):
            for a in range(n):
                copy(a, 4 + j, (*chip, 1 - c), me).wait_recv()
        for cp in first + passed:
            cp.wait_send()
        for cp in mine:
            cp.wait()

    return body


def _sibling_send_body(n, per_array_sems, handshake):
    def body(*refs):
        ins, outs = refs[:n], refs[n:2 * n]
        send_sems, recv_sems, _ = refs[2 * n:]
        x, y, c = lax.axis_index("x"), lax.axis_index("y"), lax.axis_index("c")
        sibling = (x, y, 1 - c)
        if handshake:
            _handshake([sibling])

        def sem(sems, a):
            return sems.at[a, 0] if per_array_sems else sems.at[0]

        copies = [pltpu.make_async_remote_copy(
            src_ref=ins[a].at[2 * q + 1 - c], dst_ref=outs[a].at[q], send_sem=sem(send_sems, a), recv_sem=sem(recv_sems, a),
            device_id=sibling, device_id_type=MESH) for a in range(n) for q in range(4)]
        for cp in copies:
            cp.start()
        for cp in copies:
            cp.wait_recv()
        for cp in copies:
            cp.wait_send()

    return body


def _chip_exchange_body(n, per_array_sems, handshake):
    def body(*refs):
        ins, outs = refs[:n], refs[n:2 * n]
        send_sems, recv_sems, local_sems = refs[2 * n:]
        x, y, c = lax.axis_index("x"), lax.axis_index("y"), lax.axis_index("c")
        my_chip = 2 * x + y
        chips = [(1 - x, y), (x, 1 - y), (1 - x, 1 - y)]
        if handshake:
            _handshake([(*chip, c) for chip in chips])

        def sem(sems, a, k):
            return sems.at[a, k] if per_array_sems else sems.at[k]

        def copy(a, k, landing):
            px, py = chips[k]
            return pltpu.make_async_remote_copy(
                src_ref=ins[a].at[2 * px + py], dst_ref=outs[a].at[landing], send_sem=sem(send_sems, a, k),
                recv_sem=sem(recv_sems, a, k), device_id=(px, py, c), device_id_type=MESH)

        mine = [pltpu.make_async_copy(ins[a].at[my_chip], outs[a].at[my_chip], local_sems.at[a if per_array_sems else 0])
                for a in range(n)]
        for cp in mine:
            cp.start()
        sent = [copy(a, k, my_chip) for a in range(n) for k in range(3)]
        for cp in sent:
            cp.start()
        for a in range(n):
            for k, (px, py) in enumerate(chips):
                copy(a, k, 2 * px + py).wait_recv()
        for cp in sent:
            cp.wait_send()
        for cp in mine:
            cp.wait()

    return body


def _all_gather(arrs, name):
    n = len(arrs)
    return pl.pallas_call(
        _gather_body(n, True, False, ["stack"] * n), name=name, in_specs=[ANY] * n, out_specs=[ANY] * n,
        out_shape=[jax.ShapeDtypeStruct((N_DEV,) + a.shape, a.dtype) for a in arrs],
        scratch_shapes=[pltpu.SemaphoreType.DMA((n, 7)), pltpu.SemaphoreType.DMA((n, 7)), pltpu.SemaphoreType.DMA((n,))],
    )(*arrs)


def _on_sequencer(body, arrs, out_type, name, collective_id):
    return pl.kernel(
        body, name=name, out_type=out_type, mesh=plsc.ScalarSubcoreMesh(axis_name="sequencer", num_cores=1),
        scratch_types=[pltpu.SemaphoreType.DMA((7,)), pltpu.SemaphoreType.DMA((7,)), pltpu.SemaphoreType.DMA((1,))],
        compiler_params=pltpu.CompilerParams(collective_id=collective_id))(*arrs)


def _all_gather_async(arrs, name, collective_id, layouts):
    shape = dict(stack=lambda r, c: (N_DEV, r, c), rows=lambda r, c: (N_DEV * r, c), cols=lambda r, c: (r, N_DEV * c))
    out_type = [jax.ShapeDtypeStruct(shape[lay](*a.shape), a.dtype) for a, lay in zip(arrs, layouts)]
    return _on_sequencer(_gather_body(len(arrs), False, True, layouts), arrs, out_type, name, collective_id)


def _sibling_send_async(arrs, name, collective_id):
    out_type = [jax.ShapeDtypeStruct((N_DEV // 2,) + a.shape[1:], a.dtype) for a in arrs]
    return _on_sequencer(_sibling_send_body(len(arrs), False, True), arrs, out_type, name, collective_id)


def _chip_exchange_async(arrs, name, collective_id):
    out_type = [jax.ShapeDtypeStruct(a.shape, a.dtype) for a in arrs]
    return _on_sequencer(_chip_exchange_body(len(arrs), False, True), arrs, out_type, name, collective_id)


def _pair_sum(name, parity, mine, theirs):
    n = len(mine)

    def body(par_ref, *refs):
        for a in range(n):
            refs[2 * n + a][...] = (refs[a][...].astype(F32) + refs[n + a][...].astype(F32)).astype(refs[2 * n + a].dtype)

    def block(a):
        return (None,) + a.shape[1:]

    grid_spec = pltpu.PrefetchScalarGridSpec(
        num_scalar_prefetch=1, grid=(N_DEV // 2,),
        in_specs=[pl.BlockSpec(block(a), lambda q, par: (2 * q + par[0], 0, 0)) for a in mine]
        + [pl.BlockSpec(block(a), lambda q, par: (q, 0, 0)) for a in theirs],
        out_specs=[pl.BlockSpec(block(a), lambda q, par: (q, 0, 0)) for a in theirs])
    return pl.pallas_call(body, name=name, grid_spec=grid_spec, out_shape=[jax.ShapeDtypeStruct(a.shape, a.dtype) for a in theirs],
                          compiler_params=_params())(parity, *mine, *theirs)


NN = (((1,), (0,)), ((), ()))
NT = (((1,), (1,)), ((), ()))
TN = (((0,), (0,)), ((), ()))


def _matmul(name, a, b, dims, grid, a_spec, b_spec, outs, epilogue, extras=(), prologue=None):
    n_in = 2 + len(extras)
    n_out = len(outs)

    def body(*refs):
        a_ref, b_ref = refs[0], refs[1]
        ex, out_refs = refs[2:n_in], refs[n_in:n_in + n_out]
        a_tile = a_ref[...] if prologue is None else prologue(a_ref[...])
        epilogue(lax.dot_general(a_tile, b_ref[...], dims, preferred_element_type=F32), ex, out_refs)

    return pl.pallas_call(
        body, name=name, grid=grid, in_specs=[a_spec, b_spec] + [s for _, s in extras], out_specs=[s for _, _, s in outs],
        out_shape=[jax.ShapeDtypeStruct(sh, dt) for sh, dt, _ in outs], compiler_params=_params(),
    )(a, b, *[e for e, _ in extras])


def _square(tile):
    return tile * tile


def _ep_store(acc, ex, outs):
    outs[0][...] = acc.astype(outs[0].dtype)


def _ep_residual(acc, ex, outs):
    outs[0][...] = acc
    outs[1][...] = ex[0][...] + ex[1][0:1, :] * acc


def _mm_tokens(name, a, w, transposed, out_dtypes, epilogue=_ep_store, extras=(), prologue=None, rows=MM_TOKENS):
    m, n = a.shape[0], w.shape[0 if transposed else 1]
    tm = _tile(m, rows)
    rows_spec = lambda width: pl.BlockSpec((tm, width), lambda i, j, kk: (i, 0))
    whole = lambda arr: pl.BlockSpec(arr.shape, lambda i, j, kk: (0, 0))
    return _matmul(name, a, w, NT if transposed else NN, (m // tm, 1, 1), rows_spec(a.shape[1]), whole(w),
                   [((m, n), dt, rows_spec(n)) for dt in out_dtypes], epilogue,
                   extras=[(e, rows_spec(n) if e.shape[0] == m else whole(e)) for e in extras], prologue=prologue)


def _resident(w):
    return pl.BlockSpec(w.shape, lambda i: (0, 0), pipeline_mode=pl.Buffered(1))


def _mlp_forward(name, h, w1, w2, x, gate):
    m, d = x.shape
    f = w1.shape[1]
    tm = _tile(m, MLP_TOKENS)
    rows = lambda width: pl.BlockSpec((tm, width), lambda i: (i, 0))

    def body(h_ref, w1_ref, w2_ref, x_ref, g_ref, r_ref, ff_ref, out_ref):
        r = jnp.maximum(lax.dot_general(h_ref[...], w1_ref[...], NN, preferred_element_type=F32), 0.0).astype(r_ref.dtype)
        r_ref[...] = r
        ff = lax.dot_general(r * r, w2_ref[...], NN, preferred_element_type=F32)
        ff_ref[...] = ff
        out_ref[...] = x_ref[...] + g_ref[0:1, :] * ff

    return pl.pallas_call(
        body, name=name, grid=(m // tm,), in_specs=[rows(d), _resident(w1), _resident(w2), rows(d), _vec_spec()],
        out_specs=[rows(f), rows(d), rows(d)],
        out_shape=[jax.ShapeDtypeStruct((m, f), MXU), jax.ShapeDtypeStruct((m, d), F32), jax.ShapeDtypeStruct((m, d), F32)],
        compiler_params=_params())(h, w1, w2, x, gate)


def _mlp_backward(name, dff, r, w1, w2):
    m, d = dff.shape
    f = r.shape[1]
    tm = _tile(m, MLP_TOKENS)
    rows = lambda width: pl.BlockSpec((tm, width), lambda i: (i, 0))

    def body(dff_ref, r_ref, w1_ref, w2_ref, da_ref, dh_ref):
        ds = lax.dot_general(dff_ref[...], w2_ref[...], NT, preferred_element_type=F32)
        da = (ds * (2.0 * r_ref[...].astype(F32))).astype(da_ref.dtype)
        da_ref[...] = da
        dh_ref[...] = lax.dot_general(da, w1_ref[...], NT, preferred_element_type=F32)

    return pl.pallas_call(
        body, name=name, grid=(m // tm,), in_specs=[rows(d), rows(f), _resident(w1), _resident(w2)],
        out_specs=[rows(f), rows(d)],
        out_shape=[jax.ShapeDtypeStruct((m, f), MXU), jax.ShapeDtypeStruct((m, d), F32)],
        compiler_params=_params())(dff, r, w1, w2)


def _mm_grad(name, a, b, shard_cols, prologue=None):
    t, m = a.shape
    n = b.shape[1]
    tm = _tile(m, MM_TILE)
    if shard_cols:
        tn = n // N_DEV
        out = ((N_DEV, m, tn), MXU, pl.BlockSpec((None, tm, tn), lambda i, j, kk: (j, i, 0)))
    else:
        tn = _tile(n, MM_TILE)
        out = ((m, n), MXU, pl.BlockSpec((tm, tn), lambda i, j, kk: (i, j)))
    return _matmul(name, a, b, TN, (m // tm, n // tn, 1), pl.BlockSpec((t, tm), lambda i, j, kk: (0, i)),
                   pl.BlockSpec((t, tn), lambda i, j, kk: (0, j)), [out], _ep_store, prologue=prologue)[0]


def _rows_spec(tt, width=D):
    return pl.BlockSpec((tt, width), lambda i: (i, 0))


def _vec_spec(rows=SUBLANES, width=D):
    return pl.BlockSpec((rows, width), lambda i: (0, 0))


def _norm_mod(name, x, vec):
    t = x.shape[0]
    tt = _tile(t, ROW_TILE)

    def body(x_ref, v_ref, h_ref):
        xv = x_ref[...]
        r = lax.rsqrt(jnp.mean(xv * xv, axis=-1, keepdims=True) + NORM_EPS)
        h = (xv * r) * v_ref[0:1, :]
        h_ref[...] = (h * (1.0 + v_ref[1:2, :]) + v_ref[2:3, :]).astype(h_ref.dtype)

    return pl.pallas_call(body, name=name, grid=(t // tt,), in_specs=[_rows_spec(tt), _vec_spec()], out_specs=_rows_spec(tt),
                          out_shape=jax.ShapeDtypeStruct((t, D), MXU))(x, vec)


GATE_ROW = 3


def _branch_grad(dx, y_ref, g_ref, dy_ref, acc_ref):
    dy_ref[...] = (dx * g_ref[0:1, :]).astype(dy_ref.dtype)
    acc_ref[GATE_ROW:GATE_ROW + 1, :] += _row_sum(dx * y_ref[...])


def _norm_mod_grad(name, dh, x, vec, dres, y=None, gate=None):
    t = x.shape[0]
    tt = _tile(t, ROW_TILE)
    nt = t // tt
    branch = y is not None

    def body(dh_ref, x_ref, v_ref, dres_ref, *rest):
        dx_ref, acc_ref = rest[2 * branch], rest[2 * branch + 1]
        i = pl.program_id(0)

        @pl.when(i == 0)
        def _():
            acc_ref[...] = jnp.zeros_like(acc_ref)

        xv, dhv = x_ref[...], dh_ref[...]
        r = lax.rsqrt(jnp.mean(xv * xv, axis=-1, keepdims=True) + NORM_EPS)
        xn = xv * r
        w = v_ref[0:1, :] * (1.0 + v_ref[1:2, :])
        acc_ref[0:1, :] += _row_sum(dhv * xn)
        acc_ref[2:3, :] += _row_sum(dhv)
        dxn = dhv * w
        dx = dres_ref[...] + r * (dxn - xn * jnp.mean(dxn * xn, axis=-1, keepdims=True))
        dx_ref[...] = dx
        if branch:
            _branch_grad(dx, rest[0], rest[1], rest[4], acc_ref)

        @pl.when(i == nt - 1)
        def _():
            dw = acc_ref[0:1, :]
            acc_ref[1:2, :] = dw * v_ref[0:1, :]
            acc_ref[0:1, :] = dw * (1.0 + v_ref[1:2, :])

    extra_in = [_rows_spec(tt), _vec_spec()] if branch else []
    extra_out = [_rows_spec(tt)] if branch else []
    extra_shape = [jax.ShapeDtypeStruct((t, D), MXU)] if branch else []
    return pl.pallas_call(
        body, name=name, grid=(nt,), in_specs=[_rows_spec(tt), _rows_spec(tt), _vec_spec(), _rows_spec(tt)] + extra_in,
        out_specs=[_rows_spec(tt), _vec_spec()] + extra_out,
        out_shape=[jax.ShapeDtypeStruct((t, D), F32), jax.ShapeDtypeStruct((SUBLANES, D), F32)] + extra_shape,
    )(dh, x, vec, dres, *([y, gate] if branch else []))


def _loss_head(name, x, target, vec, y, gate):
    t = x.shape[0]
    tt = _tile(t, ROW_TILE)

    def body(x_ref, t_ref, v_ref, y_ref, g_ref, dx_ref, acc_ref, dy_ref):
        @pl.when(pl.program_id(0) == 0)
        def _():
            acc_ref[...] = jnp.zeros_like(acc_ref)

        xv = x_ref[...]
        r = lax.rsqrt(jnp.mean(xv * xv, axis=-1, keepdims=True) + NORM_EPS)
        xn = xv * r
        gain = v_ref[0:1, :]
        err = xn * gain - t_ref[...]
        acc_ref[1:2, :] += _row_sum(err * err) * (0.5 / D)
        dout = err * (1.0 / D)
        acc_ref[0:1, :] += _row_sum(dout * xn)
        dxn = dout * gain
        dx = r * (dxn - xn * jnp.mean(dxn * xn, axis=-1, keepdims=True))
        dx_ref[...] = dx
        _branch_grad(dx, y_ref, g_ref, dy_ref, acc_ref)

    return pl.pallas_call(
        body, name=name, grid=(t // tt,), in_specs=[_rows_spec(tt), _rows_spec(tt), _vec_spec(), _rows_spec(tt), _vec_spec()],
        out_specs=[_rows_spec(tt), _vec_spec(), _rows_spec(tt)],
        out_shape=[jax.ShapeDtypeStruct((t, D), F32), jax.ShapeDtypeStruct((SUBLANES, D), F32),
                   jax.ShapeDtypeStruct((t, D), MXU)])(x, target, vec, y, gate)


def _shift_down(x, halo, k):
    y = pltpu.roll(x, k, 0)
    top = jnp.where(lax.broadcasted_iota(jnp.int32, halo.shape, 0) < k, pltpu.roll(halo, k, 0), y[0:SUBLANES, :])
    return jnp.concatenate([top, y[SUBLANES:, :]], axis=0)


def _shift_up(x, halo, k):
    n = x.shape[0]
    y = pltpu.roll(x, n - k, 0)
    bottom = jnp.where(lax.broadcasted_iota(jnp.int32, halo.shape, 0) >= SUBLANES - k, pltpu.roll(halo, SUBLANES - k, 0),
                       y[n - SUBLANES:, :])
    return jnp.concatenate([y[:n - SUBLANES, :], bottom], axis=0)


def _rg_gates(xb, halo, cw_ref, vec_ref, wa_ref, wx_ref, at_start):
    shifted = [xb] + [_shift_down(xb, halo, k) for k in range(1, CONV_WIDTH)]
    xc = vec_ref[0:1, :] + shifted[0] * cw_ref[CONV_WIDTH - 1:CONV_WIDTH, :]
    for k in range(1, CONV_WIDTH):
        xc = xc + shifted[k] * cw_ref[CONV_WIDTH - 1 - k:CONV_WIDTH - k, :]
    heads = [slice(h * HEAD_DIM, (h + 1) * HEAD_DIM) for h in range(HEADS)]
    pa = jnp.concatenate([_dot(xc[:, s], wa_ref[h]) for h, s in enumerate(heads)], axis=1) + vec_ref[1:2, :]
    px = jnp.concatenate([_dot(xc[:, s], wx_ref[h]) for h, s in enumerate(heads)], axis=1) + vec_ref[2:3, :]
    ra, ia = _sigmoid(pa), _sigmoid(px)
    nl = -vec_ref[3:4, :]
    sp = jnp.maximum(nl, 0.0) + jnp.log(1.0 + jnp.exp(-jnp.abs(nl)))
    log_a = (-LRU_C) * ra * sp
    a = jnp.exp(log_a)
    th = jnp.tanh(log_a)
    is_t0 = jnp.logical_and(lax.broadcasted_iota(jnp.int32, xb.shape, 0) == 0, at_start)
    mult = jnp.where(is_t0, 1.0, jnp.sqrt(-2.0 * th / (1.0 - th)))
    return dict(shifted=shifted, xc=xc, ra=ra, ia=ia, sp=sp, a=a, mult=mult, is_t0=is_t0, heads=heads)


def _rg_specs(tt, nt, order):
    blk = tt // SUBLANES
    return dict(
        x=pl.BlockSpec((tt, D), lambda i: (order(i), 0)), y=pl.BlockSpec((tt, D), lambda i: (order(i), 1)),
        halo=pl.BlockSpec((SUBLANES, D), lambda i: (jnp.maximum(order(i) * blk - 1, 0), 0)),
        cw=_vec_spec(CONV_WIDTH), vec=_vec_spec(), w=pl.BlockSpec((HEADS, HEAD_DIM, HEAD_DIM), lambda i: (0, 0, 0)))


def _rg_forward(name, z, cw, vec, wa, wx):
    t = z.shape[0]
    tt = _tile(t, ROW_TILE)
    nt = t // tt
    sp = _rg_specs(tt, nt, lambda i: i)

    def body(zx_ref, zy_ref, halo_ref, cw_ref, vec_ref, wa_ref, wx_ref, p_ref, h_ref, a_s, u_s, carry):
        i = pl.program_id(0)

        @pl.when(i == 0)
        def _():
            carry[...] = jnp.zeros_like(carry)

        halo = jnp.where(i > 0, halo_ref[...], 0.0)
        g = _rg_gates(zx_ref[...], halo, cw_ref, vec_ref, wa_ref, wx_ref, i == 0)
        a_s[...] = g["a"]
        u_s[...] = g["mult"] * (g["ia"] * g["xc"])

        def group(gi, h):
            rows = pl.ds(pl.multiple_of(gi * SUBLANES, SUBLANES), SUBLANES)
            a8, u8 = a_s[rows, :], u_s[rows, :]
            out = []
            for j in range(SUBLANES):
                h = a8[j:j + 1, :] * h + u8[j:j + 1, :]
                out.append(h)
            h_ref[rows, :] = jnp.concatenate(out, axis=0)
            return h

        carry[0:1, :] = lax.fori_loop(0, tt // SUBLANES, group, carry[0:1, :])
        p_ref[...] = (h_ref[...] * _gelu(zy_ref[...])).astype(p_ref.dtype)

    return pl.pallas_call(
        body, name=name, grid=(nt,), in_specs=[sp["x"], sp["y"], sp["halo"], sp["cw"], sp["vec"], sp["w"], sp["w"]],
        out_specs=[_rows_spec(tt), _rows_spec(tt)],
        out_shape=[jax.ShapeDtypeStruct((t, D), MXU), jax.ShapeDtypeStruct((t, D), F32)],
        scratch_shapes=[pltpu.VMEM((tt, D), F32), pltpu.VMEM((tt, D), F32), pltpu.VMEM((SUBLANES, D), F32)],
        compiler_params=_params())(z, z, z, cw, vec, wa, wx)


def _rg_backward(name, dp, z, h, cw, vec, wa, wx):
    t = z.shape[0]
    tt = _tile(t, ROW_TILE)
    nt = t // tt
    rev = lambda i: nt - 1 - i
    sp = _rg_specs(tt, nt, rev)
    rows_rev = pl.BlockSpec((tt, D), lambda i: (rev(i), 0))

    def body(dp_ref, zx_ref, zy_ref, halo_ref, h_ref, hhalo_ref, cw_ref, vec_ref, wa_ref, wx_ref,
             dz_ref, dvec_ref, dcw_ref, dwa_ref, dwx_ref, a_s, d_s, carry, nxt):
        i = pl.program_id(0)
        j = rev(i)

        @pl.when(i == 0)
        def _():
            carry[...] = jnp.zeros_like(carry)
            nxt[...] = jnp.zeros_like(nxt)
            dvec_ref[...] = jnp.zeros_like(dvec_ref)
            dcw_ref[...] = jnp.zeros_like(dcw_ref)
            dwa_ref[...] = jnp.zeros_like(dwa_ref)
            dwx_ref[...] = jnp.zeros_like(dwx_ref)

        halo = jnp.where(j > 0, halo_ref[...], 0.0)
        g = _rg_gates(zx_ref[...], halo, cw_ref, vec_ref, wa_ref, wx_ref, j == 0)
        xc, ra, ia, a, mult = g["xc"], g["ra"], g["ia"], g["a"], g["mult"]
        hv, zy, dpv = h_ref[...], zy_ref[...], dp_ref[...]
        gelu, gelu_grad = _gelu_and_grad(zy)
        dyb = dpv * hv * gelu_grad
        a_s[...] = a
        d_s[...] = dpv * gelu

        def group(gi, c):
            rows = pl.ds(pl.multiple_of((tt // SUBLANES - 1 - gi) * SUBLANES, SUBLANES), SUBLANES)
            a8, d8 = a_s[rows, :], d_s[rows, :]
            out = [None] * SUBLANES
            for r in reversed(range(SUBLANES)):
                dht = d8[r:r + 1, :] + c
                out[r] = dht
                c = a8[r:r + 1, :] * dht
            d_s[rows, :] = jnp.concatenate(out, axis=0)
            return c

        carry[0:1, :] = lax.fori_loop(0, tt // SUBLANES, group, carry[0:1, :])
        dht = d_s[...]
        hprev = _shift_down(hv, jnp.where(j > 0, hhalo_ref[...], 0.0), 1)
        ixc = ia * xc
        dlog_a = dht * hprev * a + jnp.where(g["is_t0"], 0.0, dht * ixc * (-(a * a) / mult))
        dia = dht * mult * xc
        dxc = dht * mult * ia
        dra = dlog_a * ((-LRU_C) * g["sp"])
        nl = -vec_ref[3:4, :]
        dvec_ref[3:4, :] += _row_sum(dlog_a * ((-LRU_C) * ra)) * (-_sigmoid(nl))
        dpa = dra * ra * (1.0 - ra)
        dpx = dia * ia * (1.0 - ia)
        dvec_ref[1:2, :] += _row_sum(dpa)
        dvec_ref[2:3, :] += _row_sum(dpx)
        back = []
        for hd, s in enumerate(g["heads"]):
            dwa_ref[hd] += _dot_tn(xc[:, s], dpa[:, s])
            dwx_ref[hd] += _dot_tn(xc[:, s], dpx[:, s])
            back.append(_dot_nt(dpa[:, s], wa_ref[hd]) + _dot_nt(dpx[:, s], wx_ref[hd]))
        dxc = dxc + jnp.concatenate(back, axis=1)
        dvec_ref[0:1, :] += _row_sum(dxc)
        dxb = dxc * cw_ref[CONV_WIDTH - 1:CONV_WIDTH, :]
        for k in range(CONV_WIDTH):
            row = CONV_WIDTH - 1 - k
            dcw_ref[row:row + 1, :] += _row_sum(dxc * g["shifted"][k])
            if k:
                dxb = dxb + _shift_up(dxc, nxt[...], k) * cw_ref[row:row + 1, :]
        nxt[...] = dxc[0:SUBLANES, :]
        dz_ref[:, 0:D] = dxb.astype(dz_ref.dtype)
        dz_ref[:, D:2 * D] = dyb.astype(dz_ref.dtype)

    hhalo = pl.BlockSpec((SUBLANES, D), lambda i: (jnp.maximum(rev(i) * (tt // SUBLANES) - 1, 0), 0))
    wacc = pl.BlockSpec((HEADS, HEAD_DIM, HEAD_DIM), lambda i: (0, 0, 0))
    return pl.pallas_call(
        body, name=name, grid=(nt,),
        in_specs=[rows_rev, sp["x"], sp["y"], sp["halo"], rows_rev, hhalo, sp["cw"], sp["vec"], sp["w"], sp["w"]],
        out_specs=[pl.BlockSpec((tt, 2 * D), lambda i: (rev(i), 0)), _vec_spec(), _vec_spec(), wacc, wacc],
        out_shape=[jax.ShapeDtypeStruct((t, 2 * D), MXU), jax.ShapeDtypeStruct((SUBLANES, D), F32),
                   jax.ShapeDtypeStruct((SUBLANES, D), F32), jax.ShapeDtypeStruct((HEADS, HEAD_DIM, HEAD_DIM), F32),
                   jax.ShapeDtypeStruct((HEADS, HEAD_DIM, HEAD_DIM), F32)],
        scratch_shapes=[pltpu.VMEM((tt, D), F32), pltpu.VMEM((tt, D), F32), pltpu.VMEM((SUBLANES, D), F32),
                        pltpu.VMEM((SUBLANES, D), F32)],
        compiler_params=_params())(dp, z, z, z, h, h, cw, vec, wa, wx)


def _hg_chunk(zq, zf, zi, lb):
    c = HG_CHUNK
    q, q_grad = _silu_and_grad(zq)
    sg = _sigmoid(zf)
    fg = lb + (1.0 - lb) * sg
    k = 1.0 - fg
    row, col = lax.broadcasted_iota(jnp.int32, (c, c), 0), lax.broadcasted_iota(jnp.int32, (c, c), 1)
    tri, tri_t = (row >= col).astype(F32), (row <= col).astype(F32)
    b = _dot_exact(tri, jnp.log(fg))
    mid, last = b[c // 2 - 1:c // 2, :], b[c - 1:c, :]
    eq = jnp.exp(jnp.minimum(b - mid, EXP_CLAMP))
    ek = jnp.exp(jnp.minimum(mid - b, EXP_CLAMP))
    eb = jnp.exp(b)
    ed = jnp.exp(last - b)
    return dict(q=q, q_grad=q_grad, sg=sg, fg=fg, k=k, v=zi, eq=eq, ek=ek, eb=eb, ed=ed, elast=jnp.exp(last), tri=tri, tri_t=tri_t,
                qe=q * eq, ke=k * ek, qb=q * eb, kd=k * ed)


def _hg_specs(tt, order):
    return [pl.BlockSpec((tt, D), lambda i, col=col: (order(i), col)) for col in range(4)]


def _hg_forward(name, z, vec):
    t = z.shape[0]
    tt = _tile(t, ROW_TILE)
    nt, nc = t // tt, tt // HG_CHUNK
    heads = [slice(h * HEAD_DIM, (h + 1) * HEAD_DIM) for h in range(HEADS)]

    def body(zq_ref, zf_ref, zi_ref, zg_ref, vec_ref, p_ref, o_ref, ss_ref, state):
        @pl.when(pl.program_id(0) == 0)
        def _():
            state[...] = jnp.zeros_like(state)

        lb, gain = vec_ref[0:1, :], vec_ref[1:2, :]

        def chunk(ci, carry):
            rows = pl.ds(pl.multiple_of(ci * HG_CHUNK, HG_CHUNK), HG_CHUNK)
            g = _hg_chunk(zq_ref[rows, :], zf_ref[rows, :], zi_ref[rows, :], lb)
            zg = zg_ref[rows, :]
            causal = g["tri"] > 0.0
            o_parts, p_parts = [], []
            for hd, s in enumerate(heads):
                st = state[hd]
                ss_ref[ci, hd] = st
                att = jnp.where(causal, _dot_nt(g["qe"][:, s], g["ke"][:, s]), 0.0)
                o = _dot(att, g["v"][:, s]) + _dot_nt(g["qb"][:, s], st)
                state[hd] = st * g["elast"][:, s] + _dot_tn(g["v"][:, s], g["kd"][:, s])
                r = lax.rsqrt(jnp.mean(o * o, axis=-1, keepdims=True) + GNORM_EPS)
                o_parts.append(o)
                p_parts.append((o * r) * gain[:, s])
            o_ref[rows, :] = jnp.concatenate(o_parts, axis=1)
            p_ref[rows, :] = (jnp.concatenate(p_parts, axis=1) * _silu(zg)).astype(p_ref.dtype)
            return carry

        lax.fori_loop(0, nc, chunk, 0)

    return pl.pallas_call(
        body, name=name, grid=(nt,), in_specs=_hg_specs(tt, lambda i: i) + [_vec_spec()],
        out_specs=[_rows_spec(tt), _rows_spec(tt), pl.BlockSpec((nc, HEADS, HEAD_DIM, HEAD_DIM), lambda i: (i, 0, 0, 0))],
        out_shape=[jax.ShapeDtypeStruct((t, D), MXU), jax.ShapeDtypeStruct((t, D), F32),
                   jax.ShapeDtypeStruct((t // HG_CHUNK, HEADS, HEAD_DIM, HEAD_DIM), F32)],
        scratch_shapes=[pltpu.VMEM((HEADS, HEAD_DIM, HEAD_DIM), F32)], compiler_params=_params())(z, z, z, z, vec)


def _hg_backward(name, dp, z, o, ss, vec):
    t = z.shape[0]
    tt = _tile(t, ROW_TILE)
    nt, nc = t // tt, tt // HG_CHUNK
    rev = lambda i: nt - 1 - i
    heads = [slice(h * HEAD_DIM, (h + 1) * HEAD_DIM) for h in range(HEADS)]
    rows_rev = pl.BlockSpec((tt, D), lambda i: (rev(i), 0))

    def body(dp_ref, zq_ref, zf_ref, zi_ref, zg_ref, o_ref, ss_ref, vec_ref, dz_ref, acc_ref,
             dstate, dqe_s, dke_s, dqb_s, dkd_s, dv_s, dzg_s, dlast_s):
        @pl.when(pl.program_id(0) == 0)
        def _():
            dstate[...] = jnp.zeros_like(dstate)
            acc_ref[...] = jnp.zeros_like(acc_ref)

        lb, gain = vec_ref[0:1, :], vec_ref[1:2, :]

        def chunk(cr, carry):
            ci = nc - 1 - cr
            rows = pl.ds(pl.multiple_of(ci * HG_CHUNK, HG_CHUNK), HG_CHUNK)
            zq, zg = zq_ref[rows, :], zg_ref[rows, :]
            g = _hg_chunk(zq, zf_ref[rows, :], zi_ref[rows, :], lb)
            ov, dpv = o_ref[rows, :], dp_ref[rows, :]
            causal = g["tri"] > 0.0
            silu_g, silu_g_grad = _silu_and_grad(zg)
            don = dpv * silu_g
            dgate = dpv * silu_g_grad
            for hd, s in enumerate(heads):
                oh = ov[:, s]
                r = lax.rsqrt(jnp.mean(oh * oh, axis=-1, keepdims=True) + GNORM_EPS)
                on = oh * r
                dzg_s[:, s] = dgate[:, s] * (on * gain[:, s])
                acc_ref[1:2, s] += _row_sum(don[:, s] * on)
                dtmp = don[:, s] * gain[:, s]
                do = r * (dtmp - on * jnp.mean(dtmp * on, axis=-1, keepdims=True))
                st, dst = ss_ref[ci, hd], dstate[hd]
                qe, ke, qb, kd, v = g["qe"][:, s], g["ke"][:, s], g["qb"][:, s], g["kd"][:, s], g["v"][:, s]
                att = jnp.where(causal, _dot_nt(qe, ke), 0.0)
                datt = jnp.where(causal, _dot_nt(do, v), 0.0)
                dv_s[:, s] = _dot_tn(att, do) + _dot_nt(kd, dst)
                dqe_s[:, s] = _dot(datt, ke)
                dke_s[:, s] = _dot_tn(datt, qe)
                dqb_s[:, s] = _dot(do, st)
                dkd_s[:, s] = _dot(v, dst)
                dlast_s[0:1, s] = g["elast"][:, s] * _row_sum(dst * st)
                dstate[hd] = dst * g["elast"][:, s] + _dot_tn(do, qb)
            dqe, dke, dqb, dkd = dqe_s[...], dke_s[...], dqb_s[...], dkd_s[...]
            dq = dqe * g["eq"] + dqb * g["eb"]
            dk = dke * g["ek"] + dkd * g["ed"]
            dkdk = dkd * g["kd"]
            db = dqe * g["qe"].astype(MXU).astype(F32) - dke * g["ke"].astype(MXU).astype(F32) + dqb * g["qb"] - dkdk
            dlogf = _dot_exact(g["tri_t"], db) + (dlast_s[0:1, :] + _row_sum(dkdk))
            dfg = dlogf / g["fg"] - dk
            sg = g["sg"]
            acc_ref[0:1, :] += _row_sum(dfg * (1.0 - sg))
            dz_ref[rows, 0:D] = (dq * g["q_grad"]).astype(dz_ref.dtype)
            dz_ref[rows, D:2 * D] = (dfg * (1.0 - lb) * sg * (1.0 - sg)).astype(dz_ref.dtype)
            dz_ref[rows, 2 * D:3 * D] = dv_s[...].astype(dz_ref.dtype)
            dz_ref[rows, 3 * D:4 * D] = dzg_s[...].astype(dz_ref.dtype)
            return carry

        lax.fori_loop(0, nc, chunk, 0)

    chunk_buf = pltpu.VMEM((HG_CHUNK, D), F32)
    return pl.pallas_call(
        body, name=name, grid=(nt,),
        in_specs=[rows_rev] + _hg_specs(tt, rev) + [rows_rev, pl.BlockSpec((nc, HEADS, HEAD_DIM, HEAD_DIM), lambda i: (rev(i), 0, 0, 0)),
                                                  _vec_spec()],
        out_specs=[pl.BlockSpec((tt, 4 * D), lambda i: (rev(i), 0)), _vec_spec()],
        out_shape=[jax.ShapeDtypeStruct((t, 4 * D), MXU), jax.ShapeDtypeStruct((SUBLANES, D), F32)],
        scratch_shapes=[pltpu.VMEM((HEADS, HEAD_DIM, HEAD_DIM), F32)] + [chunk_buf] * 6 + [pltpu.VMEM((SUBLANES, D), F32)],
        compiler_params=_params())(dp, z, z, z, z, o, ss, vec)


def _mod_forward(name, c_all, mod_w, lower):
    depth, _, sw = mod_w.shape

    def body(c_ref, w_ref, lo_ref, cs_ref, mod_ref, lb_ref):
        cs = _silu(c_ref[...])
        mod_ref[...] = _dot(cs, w_ref[...])

        @pl.when(pl.program_id(0) == 0)
        def _():
            cs_ref[...] = cs
            lo = lo_ref[...]
            e = jnp.exp(lo - jnp.max(lo, axis=0, keepdims=True))
            sm = e / jnp.sum(e, axis=0, keepdims=True)
            lb_ref[0:1, :] = jnp.zeros((1, D), F32)
            for l in range(1, depth):
                lb_ref[l:l + 1, :] = lb_ref[l - 1:l, :] + sm[l:l + 1, :]

    return pl.pallas_call(
        body, name=name, grid=(depth,),
        in_specs=[pl.BlockSpec((N_DEV, D), lambda l: (0, 0)), pl.BlockSpec((None, D, sw), lambda l: (l, 0, 0)),
                  pl.BlockSpec((depth, D), lambda l: (0, 0))],
        out_specs=[pl.BlockSpec((N_DEV, D), lambda l: (0, 0)), pl.BlockSpec((None, N_DEV, sw), lambda l: (l, 0, 0)),
                   pl.BlockSpec((depth, D), lambda l: (0, 0))],
        out_shape=[jax.ShapeDtypeStruct((N_DEV, D), F32), jax.ShapeDtypeStruct((depth, N_DEV, sw), F32),
                   jax.ShapeDtypeStruct((depth, D), F32)], compiler_params=_params())(c_all, mod_w, lower)


def _mod_weight_grad(name, cs_t, dmod):
    depth, pad, sw = dmod.shape

    def body(c_ref, d_ref, g_ref):
        g_ref[...] = _dot(c_ref[...], d_ref[...])

    return pl.pallas_call(
        body, name=name, grid=(depth,),
        in_specs=[pl.BlockSpec((D, pad), lambda l: (0, 0)), pl.BlockSpec((None, pad, sw), lambda l: (l, 0, 0))],
        out_specs=pl.BlockSpec((None, D, sw), lambda l: (l, 0, 0)),
        out_shape=jax.ShapeDtypeStruct((depth, D, sw), F32), compiler_params=_params())(cs_t, dmod)


def _lower_bound_grad(name, lower, dlb):
    depth = lower.shape[0]

    def body(lo_ref, d_ref, out_ref):
        lo = lo_ref[...]
        e = jnp.exp(lo - jnp.max(lo, axis=0, keepdims=True))
        sm = e / jnp.sum(e, axis=0, keepdims=True)
        out_ref[...] = jnp.zeros_like(out_ref)
        dsm = [jnp.zeros((1, D), F32)]
        for l in range(1, depth):
            tail = d_ref[l:l + 1, :]
            for m in range(l + 1, depth):
                tail = tail + d_ref[m:m + 1, :]
            dsm.append(tail)
        inner = sm[1:2, :] * dsm[1]
        for l in range(2, depth):
            inner = inner + sm[l:l + 1, :] * dsm[l]
        for l in range(depth):
            out_ref[l:l + 1, :] = sm[l:l + 1, :] * (dsm[l] - inner)
        for j in range(SUBLANES - depth):
            row = d_ref[depth + j:depth + j + 1, :]
            tot = row[:, 0:HEAD_DIM]
            for hd in range(1, HEADS):
                tot = tot + row[:, hd * HEAD_DIM:(hd + 1) * HEAD_DIM]
            out_ref[depth + j:depth + j + 1, 0:HEAD_DIM] = tot

    return pl.pallas_call(body, name=name, out_shape=jax.ShapeDtypeStruct((SUBLANES, D), F32))(lower, dlb)


def _adamw(name, parts, w, m, v, layer=None, prev=None):
    p, r, c = parts.shape
    tr = _tile(r, max(SUBLANES, ADAMW_STEP_BYTES // (4 * c * (p + 7))))
    stacked = layer is not None
    n_prev = 4 if prev is not None else 0

    def body(*refs):
        parts_ref, w_ref, m_ref, v_ref = refs[:4]
        g_ref, d_ref, m_out, v_out = refs[4 + n_prev:]
        g = parts_ref[0].astype(F32)
        for q in range(1, p):
            g = g + parts_ref[q].astype(F32)
        m2 = ADAM_B1 * m_ref[...] + (1.0 - ADAM_B1) * g
        v2 = ADAM_B2 * v_ref[...] + (1.0 - ADAM_B2) * (g * g)
        m_hat = m2 / (1.0 - ADAM_B1 ** ADAM_STEP)
        v_hat = v2 / (1.0 - ADAM_B2 ** ADAM_STEP)
        g_ref[...] = g
        d_ref[...] = -ADAM_LR * (m_hat / (jnp.sqrt(v_hat) + ADAM_EPS) + ADAM_WD * w_ref[...])
        m_out[...] = m2
        v_out[...] = v2

    if stacked:
        spec = pl.BlockSpec((None, tr, c), lambda i: (layer, i, 0))
    else:
        spec = pl.BlockSpec((tr, c), lambda i: (i, 0))
    return pl.pallas_call(
        body, name=name, grid=(r // tr,),
        in_specs=[pl.BlockSpec((p, tr, c), lambda i: (0, i, 0)), spec, spec, spec] + [ANY] * n_prev, out_specs=[spec] * 4,
        out_shape=[jax.ShapeDtypeStruct(w.shape, F32)] * 4,
        input_output_aliases={4 + q: q for q in range(n_prev)}, compiler_params=_params(),
    )(parts, w, m, v, *(prev or []))


def _hbm(a):
    return pltpu.with_memory_space_constraint(a, pltpu.HBM)


def _vec(*rows):
    rows = [r.reshape(1, D).astype(F32) for r in rows]
    packed = jnp.concatenate(rows + [jnp.zeros((SUBLANES - len(rows), D), F32)], axis=0)
    return _hbm(packed)


def _pad_rows(a, rows=SUBLANES):
    a = a.reshape(-1, a.shape[-1])
    return jnp.concatenate([a, jnp.zeros((rows - a.shape[0], a.shape[1]), a.dtype)], axis=0) if a.shape[0] < rows else a


def kernel(x, c, mod_w, mod_b, norm_mix, norm_mlp, norm_final, rg_w_in, rg_conv_w, rg_conv_b, rg_w_a, rg_b_a, rg_w_x, rg_b_x, rg_lambda, rg_w_out, hg_w_in, hg_lower_bounds, hg_gnorm, hg_w_out, mlp_w1, mlp_w2, loss_target, m_mod_w, m_mod_b, m_norm_mix, m_norm_mlp, m_norm_final, m_rg_w_in, m_rg_conv_w, m_rg_conv_b, m_rg_w_a, m_rg_b_a, m_rg_w_x, m_rg_b_x, m_rg_lambda, m_rg_w_out, m_hg_w_in, m_hg_lower_bounds, m_hg_gnorm, m_hg_w_out, m_mlp_w1, m_mlp_w2, v_mod_w, v_mod_b, v_norm_mix, v_norm_mlp, v_norm_final, v_rg_w_in, v_rg_conv_w, v_rg_conv_b, v_rg_w_a, v_rg_b_a, v_rg_w_x, v_rg_b_x, v_rg_lambda, v_rg_w_out, v_hg_w_in, v_hg_lower_bounds, v_hg_gnorm, v_hg_w_out, v_mlp_w1, v_mlp_w2):
    me = 4 * lax.axis_index("x") + 2 * lax.axis_index("y") + lax.axis_index("c")
    x0 = x[0]
    target = loss_target[0]
    n_rg, n_hg = rg_w_in.shape[0], hg_w_in.shape[0]
    sw_mod = mod_w.shape[2]

    c_all, cw_all = _all_gather([_pad_rows(c), rg_conv_w.reshape(n_rg * CONV_WIDTH, -1)], "gather_cond")
    c_all = c_all[:, 0, :]
    conv_w = cw_all.transpose(1, 0, 2).reshape(n_rg, CONV_WIDTH, D)
    cs_all, mod_part, lb_all = _mod_forward("mod_forward", c_all, mod_w, hg_lower_bounds)
    (mod_gathered,) = _all_gather([mod_part.reshape(DEPTH * N_DEV, sw_mod)], "gather_mod")

    ids = iter(range(5 * DEPTH))
    shards = []
    for layer in range(DEPTH):
        j = layer // 2
        w_in, w_out = (rg_w_in[j], rg_w_out[j]) if layer % 2 == 0 else (hg_w_in[j], hg_w_out[j])
        shards.append([w_in.astype(MXU), w_out.astype(MXU), mlp_w1[layer].astype(MXU), mlp_w2[layer].astype(MXU)])
    shards, mod_gathered = lax.optimization_barrier((shards, mod_gathered))
    weights = []
    for layer in range(DEPTH):
        g_in, g_out = _all_gather_async(shards[layer][:2], f"gather_mixer_weights_{layer}", next(ids), ["cols", "rows"])
        g_w1, g_w2 = _all_gather_async(shards[layer][2:], f"gather_mlp_weights_{layer}", next(ids), ["cols", "rows"])
        weights.append(dict(w_in=g_in, w_out=g_out, w1=g_w1, w2=g_w2))

    mod_mine = lax.dynamic_index_in_dim(mod_gathered.reshape(N_DEV, DEPTH, N_DEV, sw_mod), me, axis=2, keepdims=False)
    mod = mod_mine.transpose(1, 0, 2).reshape(DEPTH, 6, D) + mod_b.reshape(DEPTH, 6, D)

    saved = []
    xl = x0
    for layer in range(DEPTH):
        j, wt = layer // 2, weights[layer]
        is_rg = layer % 2 == 0
        s = dict(x=xl)
        s["vec_mix"] = _vec(norm_mix[layer], mod[layer, 1], mod[layer, 0])
        s["vec_mlp"] = _vec(norm_mlp[layer], mod[layer, 4], mod[layer, 3])
        s["gate_mix"], s["gate_mlp"] = _vec(mod[layer, 2]), _vec(mod[layer, 5])
        s["h"] = _hbm(_norm_mod("norm_mod", xl, s["vec_mix"]))
        if is_rg:
            (s["z"],) = _mm_tokens("rg_in", s["h"], wt["w_in"], False, (F32,))
            s["cw"] = conv_w[j]
            s["vec"] = _vec(rg_conv_b[j], rg_b_a[j], rg_b_x[j], rg_lambda[j])
            s["wa"], s["wx"] = rg_w_a[j].astype(MXU), rg_w_x[j].astype(MXU)
            s["p"], s["hr"] = _rg_forward("rg_forward", s["z"], s["cw"], s["vec"], s["wa"], s["wx"])
        else:
            (s["z"],) = _mm_tokens("hg_in", s["h"], wt["w_in"], False, (F32,))
            s["vec"] = _vec(lb_all[layer], jnp.tile(hg_gnorm[j], HEADS))
            s["p"], s["o"], s["ss"] = _hg_forward("hg_forward", s["z"], s["vec"])
        s["y"], s["x1"] = _mm_tokens("mix_out", s["p"], wt["w_out"], False, (F32, F32), _ep_residual, extras=(xl, s["gate_mix"]))
        s["p"] = _hbm(s["p"])
        s["h2"] = _hbm(_norm_mod("norm_mod", s["x1"], s["vec_mlp"]))
        s["r"], s["ff"], xl = _mlp_forward("mlp_forward", s["h2"], wt["w1"], wt["w2"], s["x1"], s["gate_mlp"])
        saved.append(s)

    dx, head, dff = _loss_head("loss_head", xl, target, _vec(norm_final), saved[-1]["ff"], saved[-1]["gate_mlp"])
    g_mlp = head[GATE_ROW:GATE_ROW + 1]

    results = {}
    big = dict(rg_w_in=(rg_w_in, m_rg_w_in, v_rg_w_in), rg_w_out=(rg_w_out, m_rg_w_out, v_rg_w_out),
               hg_w_in=(hg_w_in, m_hg_w_in, v_hg_w_in), hg_w_out=(hg_w_out, m_hg_w_out, v_hg_w_out),
               mlp_w1=(mlp_w1, m_mlp_w1, v_mlp_w1), mlp_w2=(mlp_w2, m_mlp_w2, v_mlp_w2))

    def update_layer(layer, landed):
        mixer = "rg" if layer % 2 == 0 else "hg"
        l_in, l_out, l_w1, l_w2 = landed
        for nm, parts, idx in ((f"{mixer}_w_in", l_in, layer // 2), (f"{mixer}_w_out", l_out, layer // 2),
                               ("mlp_w1", l_w1, layer), ("mlp_w2", l_w2, layer)):
            w, m, v = big[nm]
            results[nm] = _adamw(f"adamw_{nm}", parts, w, m, v, layer=idx, prev=results.get(nm))

    landed = [None] * DEPTH
    small = [None] * DEPTH
    parity = lax.axis_index("c").astype(jnp.int32).reshape(1)

    def send_chip_sums(layer, grads, from_sibling, anchor):
        sums = _pair_sum("pair_sum", parity, grads, from_sibling)
        sums, anchor = lax.optimization_barrier((sums, anchor))
        landed[layer] = _chip_exchange_async(sums, f"exchange_grads_{layer}", next(ids))
        return anchor

    pending = None
    for layer in reversed(range(DEPTH)):
        j, wt, s = layer // 2, weights[layer], saved[layer]
        is_rg = layer % 2 == 0
        dff = _hbm(dff)
        da, dh2 = _mlp_backward("mlp_backward", dff, s["r"], wt["w1"], wt["w2"])
        if pending is not None:
            da = send_chip_sums(*pending, da)
        dw2 = _mm_grad("mlp_out_grad", s["r"], dff, False, prologue=_square).reshape(N_DEV, -1, D)
        dw1 = _mm_grad("mlp_in_grad", s["h2"], da, True)
        dx1, n_mlp, dyb = _norm_mod_grad("norm_mod_grad", dh2, s["x1"], s["vec_mlp"], dx, s["y"], s["gate_mix"])
        dyb = _hbm(dyb)
        (dp,) = _mm_tokens("mix_out_t", dyb, wt["w_out"], True, (F32,))
        dw_out = _hbm(_mm_grad("mix_out_grad", s["p"], dyb, False).reshape(N_DEV, -1, D))
        if is_rg:
            dz, dvec, dcw, dwa, dwx = _rg_backward("rg_backward", dp, s["z"], s["hr"], s["cw"], s["vec"], s["wa"], s["wx"])
            gate_grads, dz = lax.optimization_barrier(
                ([dwa.reshape(-1, HEAD_DIM).astype(MXU), dwx.reshape(-1, HEAD_DIM).astype(MXU)], dz))
            gate_parts = _all_gather_async(gate_grads, f"gather_gate_grads_{j}", next(ids), ["stack", "stack"])
            mixer_small = dict(dvec=dvec, dcw=dcw, gate_parts=gate_parts)
            dw_in = _mm_grad("rg_in_grad", s["h"], dz, True)
            (dh,) = _mm_tokens("rg_in_t", dz, wt["w_in"], True, (F32,))
        else:
            dz, dvec = _hg_backward("hg_backward", dp, s["z"], s["o"], s["ss"], s["vec"])
            mixer_small = dict(dvec=dvec)
            dw_in = _mm_grad("hg_in_grad", s["h"], dz, True)
            (dh,) = _mm_tokens("hg_in_t", dz, wt["w_in"], True, (F32,))
        grads = [dw_in, dw_out, dw1, dw2]
        pending = (layer, grads, _sibling_send_async(grads, f"pair_grads_{layer}", next(ids)))
        small[layer] = dict(g_mlp=g_mlp, n_mlp=n_mlp, g_mix=n_mlp[GATE_ROW:GATE_ROW + 1], **mixer_small)
        if layer:
            below = saved[layer - 1]
            dx, n_mix, dff = _norm_mod_grad("norm_mod_grad", dh, s["x"], s["vec_mix"], dx1, below["ff"], below["gate_mlp"])
            g_mlp = n_mix[GATE_ROW:GATE_ROW + 1]
        else:
            dx, n_mix = _norm_mod_grad("norm_mod_grad_first", dh, s["x"], s["vec_mix"], dx1)
        small[layer]["n_mix"] = n_mix
        if layer + 1 < DEPTH:
            stream = (dx, dff) if layer else (dx,)
            landed[layer + 1], stream = lax.optimization_barrier((landed[layer + 1], stream))
            if "gate_parts" in small[layer + 1]:
                small[layer + 1]["gate_parts"], stream = lax.optimization_barrier((small[layer + 1]["gate_parts"], stream))
            dx, dff = stream if layer else (stream[0], None)
            update_layer(layer + 1, landed[layer + 1])

    dlb_rows = [jnp.zeros((1, D), F32) if l % 2 == 0 else small[l]["dvec"][0:1] for l in range(DEPTH)]
    dgn_rows = [small[2 * j + 1]["dvec"][1:2] for j in range(n_hg)]
    lb_grad = _lower_bound_grad("lower_bound_grad", hg_lower_bounds, _pad_rows(jnp.concatenate(dlb_rows + dgn_rows, axis=0)))
    dmod = jnp.stack([jnp.concatenate([small[l]["n_mix"][2], small[l]["n_mix"][1], small[l]["g_mix"][0],
                                       small[l]["n_mlp"][2], small[l]["n_mlp"][1], small[l]["g_mlp"][0]]) for l in range(DEPTH)])
    groups = [
        dmod.reshape(DEPTH * 6, D),
        jnp.stack([small[l]["n_mix"][0] for l in range(DEPTH)]),
        jnp.stack([small[l]["n_mlp"][0] for l in range(DEPTH)]),
        head[0:1],
        jnp.stack([small[2 * j]["dvec"][0] for j in range(n_rg)]),
        jnp.stack([small[2 * j]["dvec"][1] for j in range(n_rg)]),
        jnp.stack([small[2 * j]["dvec"][2] for j in range(n_rg)]),
        jnp.stack([small[2 * j]["dvec"][3] for j in range(n_rg)]),
        lb_grad[0:DEPTH],
        jnp.concatenate([lb_grad[DEPTH + j:DEPTH + j + 1, 0:HEAD_DIM] for j in range(n_hg)]
                        + [jnp.zeros((1, D - n_hg * HEAD_DIM), F32)], axis=1),
        jnp.concatenate([small[2 * j]["dcw"][0:CONV_WIDTH] for j in range(n_rg)], axis=0),
        head[1:2],
    ]
    params = [mod_b, norm_mix, norm_mlp, norm_final, rg_conv_b, rg_b_a, rg_b_x, rg_lambda, hg_lower_bounds, hg_gnorm]
    moms = [m_mod_b, m_norm_mix, m_norm_mlp, m_norm_final, m_rg_conv_b, m_rg_b_a, m_rg_b_x, m_rg_lambda, m_hg_lower_bounds, m_hg_gnorm]
    vars_ = [v_mod_b, v_norm_mix, v_norm_mlp, v_norm_final, v_rg_conv_b, v_rg_b_a, v_rg_b_x, v_rg_lambda, v_hg_lower_bounds, v_hg_gnorm]
    offsets, rows_of, at = [], [], 0
    for g in groups:
        offsets.append(at)
        rows_of.append(g.shape[0])
        at += -(-g.shape[0] // SUBLANES) * SUBLANES
    packed = jnp.concatenate([_pad_rows(g, -(-g.shape[0] // SUBLANES) * SUBLANES) for g in groups], axis=0)
    (small_parts,) = _all_gather([packed], "gather_small_grads")
    last_layer, last_grads, from_sibling = pending
    from_sibling, small_parts = lax.optimization_barrier((from_sibling, small_parts))
    send_chip_sums(last_layer, last_grads, from_sibling, dx)

    def pack_like(arrs):
        out = []
        for g_rows, off, a in zip(rows_of, offsets, arrs):
            flat = a.reshape(-1)
            flat = jnp.concatenate([flat, jnp.zeros((g_rows * D - flat.shape[0],), F32)])
            out.append(_pad_rows(flat.reshape(g_rows, D), -(-g_rows // SUBLANES) * SUBLANES))
        rest = packed.shape[0] - sum(o.shape[0] for o in out)
        return jnp.concatenate(out + [jnp.zeros((rest, D), F32)], axis=0)

    small_out = _adamw("adamw_small", small_parts, pack_like(params), pack_like(moms), pack_like(vars_))

    def unpack(q, idx, like):
        rows = small_out[q][offsets[idx]:offsets[idx] + rows_of[idx]]
        return rows.reshape(-1)[:like.size].reshape(like.shape)

    loss = jnp.sum(small_out[0][offsets[11]])
    names =["mod_b", "norm_mix", "norm_mlp", "norm_final", "rg_conv_b", "rg_b_a", "rg_b_x", "rg_lambda", "hg_lower_bounds", "hg_gnorm"]
    for idx, (nm, like) in enumerate(zip(names, params)):
        results[nm] = [unpack(q, idx, like) for q in range(4)]

    cw_parts = lax.dynamic_slice_in_dim(small_parts[:, offsets[10]:offsets[10] + n_rg * CONV_WIDTH, :], me * (D // N_DEV), D // N_DEV, axis=2)
    shp = rg_conv_w.shape
    results["rg_conv_w"] = [o.reshape(shp) for o in _adamw(
        "adamw_conv", cw_parts, rg_conv_w.reshape(-1, shp[-1]), m_rg_conv_w.reshape(-1, shp[-1]), v_rg_conv_w.reshape(-1, shp[-1]))]
    shp = rg_w_a.shape
    stacked = (n_rg, HEADS * HEAD_DIM, HEAD_DIM)
    for nm, which, (w, m, v) in (("rg_w_a", 0, (rg_w_a, m_rg_w_a, v_rg_w_a)), ("rg_w_x", 1, (rg_w_x, m_rg_w_x, v_rg_w_x))):
        out = None
        for j in reversed(range(n_rg)):
            out = _adamw("adamw_gate", small[2 * j]["gate_parts"][which], w.reshape(stacked), m.reshape(stacked),
                         v.reshape(stacked), layer=j, prev=out)
        results[nm] = [o.reshape(shp) for o in out]

    dmod_all = small_parts[:, 0:DEPTH * 6, :].reshape(N_DEV, DEPTH, 6 * D)
    dmod_cols = lax.dynamic_slice_in_dim(dmod_all, me * sw_mod, sw_mod, axis=2).transpose(1, 0, 2)
    pad = HEAD_DIM - N_DEV
    dmod_pad = jnp.concatenate([dmod_cols, jnp.zeros((DEPTH, pad, sw_mod), F32)], axis=1).astype(MXU)
    cs_t = jnp.concatenate([cs_all.T, jnp.zeros((D, pad), F32)], axis=1).astype(MXU)
    g_mod_w = _mod_weight_grad("mod_weight_grad", cs_t, dmod_pad)
    results["mod_w"] = [o.reshape(mod_w.shape) for o in _adamw(
        "adamw_mod", g_mod_w.reshape(1, -1, sw_mod), mod_w.reshape(-1, sw_mod), m_mod_w.reshape(-1, sw_mod), v_mod_w.reshape(-1, sw_mod))]

    update_layer(0, landed[0])

    order = ["mod_w", "mod_b", "norm_mix", "norm_mlp", "norm_final", "rg_w_in", "rg_conv_w", "rg_conv_b", "rg_w_a", "rg_b_a", "rg_w_x",
             "rg_b_x", "rg_lambda", "rg_w_out", "hg_w_in", "hg_lower_bounds", "hg_gnorm", "hg_w_out", "mlp_w1", "mlp_w2"]
    return (loss, dx[None], *[results[n][0] for n in order], *[results[n][1] for n in order],
            *[results[n][2] for n in order], *[results[n][3] for n in order])
```

```python
import functools

import jax
import jax.numpy as jnp
from jax import lax
from jax.experimental import pallas as pl
from jax.experimental.pallas import tpu as pltpu
from jax.experimental.pallas import tpu_sc as plsc

F32 = jnp.float32
MXU = jnp.bfloat16

N_DEV = 8
D = 1024
DEPTH = 4
HEADS = 8
HEAD_DIM = 128
CONV_WIDTH = 4
LRU_C = 8.0
HG_CHUNK = 64
NORM_EPS = 1e-6
GNORM_EPS = 1e-5
ADAM_LR = 0.001
ADAM_B1 = 0.9
ADAM_B2 = 0.999
ADAM_EPS = 1e-08
ADAM_WD = 0.01
ADAM_STEP = 10
GELU_C = 0.7978845608028654
GELU_K = 0.044715
EXP_CLAMP = 80.0
SUBLANES = 8
VMEM_LIMIT = 48 * 1024 * 1024
ROW_TILE = 256
RG_ROWS = 256
MM_TILE = 1024
MM_TOKENS = 512
MLP_TOKENS = 256
ADAMW_STEP_BYTES = 8 * 1024 * 1024

MESH = pl.DeviceIdType.MESH
ANY = pl.BlockSpec(memory_space=pl.ANY)


def _params():
    return pltpu.CompilerParams(vmem_limit_bytes=VMEM_LIMIT)


def _tile(n, target):
    if n <= target:
        return n
    t = target // SUBLANES * SUBLANES
    while n % t:
        t -= SUBLANES
    return t


def _sigmoid(x):
    return 1.0 / (1.0 + jnp.exp(-x))


def _silu(x):
    return x * _sigmoid(x)


def _silu_and_grad(x):
    s = _sigmoid(x)
    return x * s, s * (1.0 + x * (1.0 - s))


def _gelu(y):
    return 0.5 * y * (1.0 + jnp.tanh(GELU_C * (y + GELU_K * y * y * y)))


def _gelu_and_grad(y):
    y2 = y * y
    th = jnp.tanh(GELU_C * (y + GELU_K * y2 * y))
    half = 0.5 * (1.0 + th)
    return y * half, half + 0.5 * y * (1.0 - th * th) * GELU_C * (1.0 + 3.0 * GELU_K * y2)


def _dot(a, b):
    return lax.dot_general(a.astype(MXU), b.astype(MXU), (((1,), (0,)), ((), ())), preferred_element_type=F32)


def _dot_nt(a, b):
    return lax.dot_general(a.astype(MXU), b.astype(MXU), (((1,), (1,)), ((), ())), preferred_element_type=F32)


def _dot_tn(a, b):
    return lax.dot_general(a.astype(MXU), b.astype(MXU), (((0,), (0,)), ((), ())), preferred_element_type=F32)


def _dot_exact(tri, x):
    t = tri.astype(MXU)
    hi = x.astype(MXU)
    r1 = x - hi.astype(F32)
    mid = r1.astype(MXU)
    lo = (r1 - mid.astype(F32)).astype(MXU)
    dn = (((1,), (0,)), ((), ()))
    return (lax.dot_general(t, hi, dn, preferred_element_type=F32) + lax.dot_general(t, mid, dn, preferred_element_type=F32)
            + lax.dot_general(t, lo, dn, preferred_element_type=F32))


def _row_sum(v):
    return jnp.sum(v, axis=0, keepdims=True)


def _handshake(partners):
    barrier = pltpu.get_barrier_semaphore()
    for p in partners:
        pl.semaphore_signal(barrier, inc=1, device_id=p, device_id_type=MESH)
    pl.semaphore_wait(barrier, len(partners))


def _gather_body(n, per_array_sems, handshake, layouts):
    def body(*refs):
        ins, outs = refs[:n], refs[n:2 * n]
        send_sems, recv_sems, local_sems = refs[2 * n:]
        x, y, c = lax.axis_index("x"), lax.axis_index("y"), lax.axis_index("c")
        me, sibling = (x, y, c), (x, y, 1 - c)
        chips = [(1 - x, y), (x, 1 - y), (1 - x, 1 - y)]
        if handshake:
            _handshake([sibling] + [(*chip, c) for chip in chips])

        def sem(sems, a, k):
            return sems.at[a, k] if per_array_sems else sems.at[k]

        def slot(a, p):
            block = 4 * p[0] + 2 * p[1] + p[2]
            r, c_ = ins[a].shape
            if layouts[a] == "rows":
                return outs[a].at[pl.ds(block * r, r), :]
            if layouts[a] == "cols":
                return outs[a].at[:, pl.ds(block * c_, c_)]
            return outs[a].at[block]

        def copy(a, k, block, to, src=None):
            return pltpu.make_async_remote_copy(
                src_ref=slot(a, block) if src is None else src, dst_ref=slot(a, block),
                send_sem=sem(send_sems, a, k), recv_sem=sem(recv_sems, a, k), device_id=to, device_id_type=MESH)

        mine = [pltpu.make_async_copy(ins[a], slot(a, me), local_sems.at[a if per_array_sems else 0]) for a in range(n)]
        for cp in mine:
            cp.start()
        first = []
        for a in range(n):
            first.append(copy(a, 0, me, sibling, src=ins[a]))
            first += [copy(a, 1 + j, me, (*chip, c), src=ins[a]) for j, chip in enumerate(chips)]
        for cp in first:
            cp.start()
        passed = []
        for j, chip in enumerate(chips):
            for a in range(n):
                copy(a, 1 + j, (*chip, c), me).wait_recv()
            for a in range(n):
                cp = copy(a, 4 + j, (*chip, c), sibling)
                cp.start()
                passed.append(cp)
        for a in range(n):
            copy(a, 0, sibling, me).wait_recv()
        for j, chip in enumerate(chips):
            for a in range(n):
                copy(a, 4 + j, (*chip, 1 - c), me).wait_recv()
        for cp in first + passed:
            cp.wait_send()
        for cp in mine:
            cp.wait()

    return body


def _sibling_send_body(n, per_array_sems, handshake):
    def body(*refs):
        ins, outs = refs[:n], refs[n:2 * n]
        send_sems, recv_sems, _ = refs[2 * n:]
        x, y, c = lax.axis_index("x"), lax.axis_index("y"), lax.axis_index("c")
        sibling = (x, y, 1 - c)
        if handshake:
            _handshake([sibling])

        def sem(sems, a):
            return sems.at[a, 0] if per_array_sems else sems.at[0]

        copies = [pltpu.make_async_remote_copy(
            src_ref=ins[a].at[2 * q + 1 - c], dst_ref=outs[a].at[q], send_sem=sem(send_sems, a), recv_sem=sem(recv_sems, a),
            device_id=sibling, device_id_type=MESH) for a in range(n) for q in range(4)]
        for cp in copies:
            cp.start()
        for cp in copies:
            cp.wait_recv()
        for cp in copies:
            cp.wait_send()

    return body


def _chip_exchange_body(n, per_array_sems, handshake):
    def body(*refs):
        ins, outs = refs[:n], refs[n:2 * n]
        send_sems, recv_sems, local_sems = refs[2 * n:]
        x, y, c = lax.axis_index("x"), lax.axis_index("y"), lax.axis_index("c")
        my_chip = 2 * x + y
        chips = [(1 - x, y), (x, 1 - y), (1 - x, 1 - y)]
        if handshake:
            _handshake([(*chip, c) for chip in chips])

        def sem(sems, a, k):
            return sems.at[a, k] if per_array_sems else sems.at[k]

        def copy(a, k, landing):
            px, py = chips[k]
            return pltpu.make_async_remote_copy(
                src_ref=ins[a].at[2 * px + py], dst_ref=outs[a].at[landing], send_sem=sem(send_sems, a, k),
                recv_sem=sem(recv_sems, a, k), device_id=(px, py, c), device_id_type=MESH)

        mine = [pltpu.make_async_copy(ins[a].at[my_chip], outs[a].at[my_chip], local_sems.at[a if per_array_sems else 0])
                for a in range(n)]
        for cp in mine:
            cp.start()
        sent = [copy(a, k, my_chip) for a in range(n) for k in range(3)]
        for cp in sent:
            cp.start()
        for a in range(n):
            for k, (px, py) in enumerate(chips):
                copy(a, k, 2 * px + py).wait_recv()
        for cp in sent:
            cp.wait_send()
        for cp in mine:
            cp.wait()

    return body


def _all_gather(arrs, name):
    n = len(arrs)
    return pl.pallas_call(
        _gather_body(n, True, False, ["stack"] * n), name=name, in_specs=[ANY] * n, out_specs=[ANY] * n,
        out_shape=[jax.ShapeDtypeStruct((N_DEV,) + a.shape, a.dtype) for a in arrs],
        scratch_shapes=[pltpu.SemaphoreType.DMA((n, 7)), pltpu.SemaphoreType.DMA((n, 7)), pltpu.SemaphoreType.DMA((n,))],
    )(*arrs)


def _on_sequencer(body, arrs, out_type, name, collective_id):
    return pl.kernel(
        body, name=name, out_type=out_type, mesh=plsc.ScalarSubcoreMesh(axis_name="sequencer", num_cores=1),
        scratch_types=[pltpu.SemaphoreType.DMA((7,)), pltpu.SemaphoreType.DMA((7,)), pltpu.SemaphoreType.DMA((1,))],
        compiler_params=pltpu.CompilerParams(collective_id=collective_id))(*arrs)


def _all_gather_async(arrs, name, collective_id, layouts):
    shape = dict(stack=lambda r, c: (N_DEV, r, c), rows=lambda r, c: (N_DEV * r, c), cols=lambda r, c: (r, N_DEV * c))
    out_type = [jax.ShapeDtypeStruct(shape[lay](*a.shape), a.dtype) for a, lay in zip(arrs, layouts)]
    return _on_sequencer(_gather_body(len(arrs), False, True, layouts), arrs, out_type, name, collective_id)


def _sibling_send_async(arrs, name, collective_id):
    out_type = [jax.ShapeDtypeStruct((N_DEV // 2,) + a.shape[1:], a.dtype) for a in arrs]
    return _on_sequencer(_sibling_send_body(len(arrs), False, True), arrs, out_type, name, collective_id)


def _chip_exchange_async(arrs, name, collective_id):
    out_type = [jax.ShapeDtypeStruct(a.shape, a.dtype) for a in arrs]
    return _on_sequencer(_chip_exchange_body(len(arrs), False, True), arrs, out_type, name, collective_id)


def _pair_sum(name, parity, mine, theirs):
    n = len(mine)

    def body(par_ref, *refs):
        for a in range(n):
            refs[2 * n + a][...] = (refs[a][...].astype(F32) + refs[n + a][...].astype(F32)).astype(refs[2 * n + a].dtype)

    def block(a):
        return (None,) + a.shape[1:]

    grid_spec = pltpu.PrefetchScalarGridSpec(
        num_scalar_prefetch=1, grid=(N_DEV // 2,),
        in_specs=[pl.BlockSpec(block(a), lambda q, par: (2 * q + par[0], 0, 0)) for a in mine]
        + [pl.BlockSpec(block(a), lambda q, par: (q, 0, 0)) for a in theirs],
        out_specs=[pl.BlockSpec(block(a), lambda q, par: (q, 0, 0)) for a in theirs])
    return pl.pallas_call(body, name=name, grid_spec=grid_spec, out_shape=[jax.ShapeDtypeStruct(a.shape, a.dtype) for a in theirs],
                          compiler_params=_params())(parity, *mine, *theirs)


NN = (((1,), (0,)), ((), ()))
NT = (((1,), (1,)), ((), ()))
TN = (((0,), (0,)), ((), ()))


def _matmul(name, a, b, dims, grid, a_spec, b_spec, outs, epilogue, extras=(), prologue=None):
    n_in = 2 + len(extras)
    n_out = len(outs)

    def body(*refs):
        a_ref, b_ref = refs[0], refs[1]
        ex, out_refs = refs[2:n_in], refs[n_in:n_in + n_out]
        a_tile = a_ref[...] if prologue is None else prologue(a_ref[...])
        epilogue(lax.dot_general(a_tile, b_ref[...], dims, preferred_element_type=F32), ex, out_refs)

    return pl.pallas_call(
        body, name=name, grid=grid, in_specs=[a_spec, b_spec] + [s for _, s in extras], out_specs=[s for _, _, s in outs],
        out_shape=[jax.ShapeDtypeStruct(sh, dt) for sh, dt, _ in outs], compiler_params=_params(),
    )(a, b, *[e for e, _ in extras])


def _square(tile):
    return tile * tile


def _ep_store(acc, ex, outs):
    outs[0][...] = acc.astype(outs[0].dtype)


def _ep_residual(acc, ex, outs):
    outs[0][...] = acc
    outs[1][...] = ex[0][...] + ex[1][0:1, :] * acc


def _mm_tokens(name, a, w, transposed, out_dtypes, epilogue=_ep_store, extras=(), prologue=None, rows=MM_TOKENS):
    m, n = a.shape[0], w.shape[0 if transposed else 1]
    tm = _tile(m, rows)
    rows_spec = lambda width: pl.BlockSpec((tm, width), lambda i, j, kk: (i, 0))
    whole = lambda arr: pl.BlockSpec(arr.shape, lambda i, j, kk: (0, 0))
    return _matmul(name, a, w, NT if transposed else NN, (m // tm, 1, 1), rows_spec(a.shape[1]), whole(w),
                   [((m, n), dt, rows_spec(n)) for dt in out_dtypes], epilogue,
                   extras=[(e, rows_spec(n) if e.shape[0] == m else whole(e)) for e in extras], prologue=prologue)


def _resident(w):
    return pl.BlockSpec(w.shape, lambda i: (0, 0), pipeline_mode=pl.Buffered(1))


def _mlp_forward(name, h, w1, w2, x, gate):
    m, d = x.shape
    f = w1.shape[1]
    tm = _tile(m, MLP_TOKENS)
    rows = lambda width: pl.BlockSpec((tm, width), lambda i: (i, 0))

    def body(h_ref, w1_ref, w2_ref, x_ref, g_ref, r_ref, ff_ref, out_ref):
        r = jnp.maximum(lax.dot_general(h_ref[...], w1_ref[...], NN, preferred_element_type=F32), 0.0).astype(r_ref.dtype)
        r_ref[...] = r
        ff = lax.dot_general(r * r, w2_ref[...], NN, preferred_element_type=F32)
        ff_ref[...] = ff
        out_ref[...] = x_ref[...] + g_ref[0:1, :] * ff

    return pl.pallas_call(
        body, name=name, grid=(m // tm,), in_specs=[rows(d), _resident(w1), _resident(w2), rows(d), _vec_spec()],
        out_specs=[rows(f), rows(d), rows(d)],
        out_shape=[jax.ShapeDtypeStruct((m, f), MXU), jax.ShapeDtypeStruct((m, d), F32), jax.ShapeDtypeStruct((m, d), F32)],
        compiler_params=_params())(h, w1, w2, x, gate)


def _mlp_backward(name, dff, r, w1, w2):
    m, d = dff.shape
    f = r.shape[1]
    tm = _tile(m, MLP_TOKENS)
    rows = lambda width: pl.BlockSpec((tm, width), lambda i: (i, 0))

    def body(dff_ref, r_ref, w1_ref, w2_ref, da_ref, dh_ref):
        ds = lax.dot_general(dff_ref[...], w2_ref[...], NT, preferred_element_type=F32)
        da = (ds * (2.0 * r_ref[...].astype(F32))).astype(da_ref.dtype)
        da_ref[...] = da
        dh_ref[...] = lax.dot_general(da, w1_ref[...], NT, preferred_element_type=F32)

    return pl.pallas_call(
        body, name=name, grid=(m // tm,), in_specs=[rows(d), rows(f), _resident(w1), _resident(w2)],
        out_specs=[rows(f), rows(d)],
        out_shape=[jax.ShapeDtypeStruct((m, f), MXU), jax.ShapeDtypeStruct((m, d), F32)],
        compiler_params=_params())(dff, r, w1, w2)


def _mm_grad(name, a, b, shard_cols, prologue=None):
    t, m = a.shape
    n = b.shape[1]
    tm = _tile(m, MM_TILE)
    if shard_cols:
        tn = n // N_DEV
        out = ((N_DEV, m, tn), MXU, pl.BlockSpec((None, tm, tn), lambda i, j, kk: (j, i, 0)))
    else:
        tn = _tile(n, MM_TILE)
        out = ((m, n), MXU, pl.BlockSpec((tm, tn), lambda i, j, kk: (i, j)))
    return _matmul(name, a, b, TN, (m // tm, n // tn, 1), pl.BlockSpec((t, tm), lambda i, j, kk: (0, i)),
                   pl.BlockSpec((t, tn), lambda i, j, kk: (0, j)), [out], _ep_store, prologue=prologue)[0]


def _rows_spec(tt, width=D):
    return pl.BlockSpec((tt, width), lambda i: (i, 0))


def _vec_spec(rows=SUBLANES, width=D):
    return pl.BlockSpec((rows, width), lambda i: (0, 0))


def _norm_mod(name, x, vec):
    t = x.shape[0]
    tt = _tile(t, ROW_TILE)

    def body(x_ref, v_ref, h_ref):
        xv = x_ref[...]
        r = lax.rsqrt(jnp.mean(xv * xv, axis=-1, keepdims=True) + NORM_EPS)
        h = (xv * r) * v_ref[0:1, :]
        h_ref[...] = (h * (1.0 + v_ref[1:2, :]) + v_ref[2:3, :]).astype(h_ref.dtype)

    return pl.pallas_call(body, name=name, grid=(t // tt,), in_specs=[_rows_spec(tt), _vec_spec()], out_specs=_rows_spec(tt),
                          out_shape=jax.ShapeDtypeStruct((t, D), MXU))(x, vec)


GATE_ROW = 3


def _branch_grad(dx, y_ref, g_ref, dy_ref, acc_ref):
    dy_ref[...] = (dx * g_ref[0:1, :]).astype(dy_ref.dtype)
    acc_ref[GATE_ROW:GATE_ROW + 1, :] += _row_sum(dx * y_ref[...])


def _norm_mod_grad(name, dh, x, vec, dres, y=None, gate=None):
    t = x.shape[0]
    tt = _tile(t, ROW_TILE)
    nt = t // tt
    branch = y is not None

    def body(dh_ref, x_ref, v_ref, dres_ref, *rest):
        dx_ref, acc_ref = rest[2 * branch], rest[2 * branch + 1]
        i = pl.program_id(0)

        @pl.when(i == 0)
        def _():
            acc_ref[...] = jnp.zeros_like(acc_ref)

        xv, dhv = x_ref[...], dh_ref[...]
        r = lax.rsqrt(jnp.mean(xv * xv, axis=-1, keepdims=True) + NORM_EPS)
        xn = xv * r
        w = v_ref[0:1, :] * (1.0 + v_ref[1:2, :])
        acc_ref[0:1, :] += _row_sum(dhv * xn)
        acc_ref[2:3, :] += _row_sum(dhv)
        dxn = dhv * w
        dx = dres_ref[...] + r * (dxn - xn * jnp.mean(dxn * xn, axis=-1, keepdims=True))
        dx_ref[...] = dx
        if branch:
            _branch_grad(dx, rest[0], rest[1], rest[4], acc_ref)

        @pl.when(i == nt - 1)
        def _():
            dw = acc_ref[0:1, :]
            acc_ref[1:2, :] = dw * v_ref[0:1, :]
            acc_ref[0:1, :] = dw * (1.0 + v_ref[1:2, :])

    extra_in = [_rows_spec(tt), _vec_spec()] if branch else []
    extra_out = [_rows_spec(tt)] if branch else []
    extra_shape = [jax.ShapeDtypeStruct((t, D), MXU)] if branch else []
    return pl.pallas_call(
        body, name=name, grid=(nt,), in_specs=[_rows_spec(tt), _rows_spec(tt), _vec_spec(), _rows_spec(tt)] + extra_in,
        out_specs=[_rows_spec(tt), _vec_spec()] + extra_out,
        out_shape=[jax.ShapeDtypeStruct((t, D), F32), jax.ShapeDtypeStruct((SUBLANES, D), F32)] + extra_shape,
    )(dh, x, vec, dres, *([y, gate] if branch else []))


def _loss_head(name, x, target, vec, y, gate):
    t = x.shape[0]
    tt = _tile(t, ROW_TILE)

    def body(x_ref, t_ref, v_ref, y_ref, g_ref, dx_ref, acc_ref, dy_ref):
        @pl.when(pl.program_id(0) == 0)
        def _():
            acc_ref[...] = jnp.zeros_like(acc_ref)

        xv = x_ref[...]
        r = lax.rsqrt(jnp.mean(xv * xv, axis=-1, keepdims=True) + NORM_EPS)
        xn = xv * r
        gain = v_ref[0:1, :]
        err = xn * gain - t_ref[...]
        acc_ref[1:2, :] += _row_sum(err * err) * (0.5 / D)
        dout = err * (1.0 / D)
        acc_ref[0:1, :] += _row_sum(dout * xn)
        dxn = dout * gain
        dx = r * (dxn - xn * jnp.mean(dxn * xn, axis=-1, keepdims=True))
        dx_ref[...] = dx
        _branch_grad(dx, y_ref, g_ref, dy_ref, acc_ref)

    return pl.pallas_call(
        body, name=name, grid=(t // tt,), in_specs=[_rows_spec(tt), _rows_spec(tt), _vec_spec(), _rows_spec(tt), _vec_spec()],
        out_specs=[_rows_spec(tt), _vec_spec(), _rows_spec(tt)],
        out_shape=[jax.ShapeDtypeStruct((t, D), F32), jax.ShapeDtypeStruct((SUBLANES, D), F32),
                   jax.ShapeDtypeStruct((t, D), MXU)])(x, target, vec, y, gate)


def _shift_down(x, halo, k):
    y = pltpu.roll(x, k, 0)
    top = jnp.where(lax.broadcasted_iota(jnp.int32, halo.shape, 0) < k, pltpu.roll(halo, k, 0), y[0:SUBLANES, :])
    return jnp.concatenate([top, y[SUBLANES:, :]], axis=0)


def _shift_up(x, halo, k):
    n = x.shape[0]
    y = pltpu.roll(x, n - k, 0)
    bottom = jnp.where(lax.broadcasted_iota(jnp.int32, halo.shape, 0) >= SUBLANES - k, pltpu.roll(halo, SUBLANES - k, 0),
                       y[n - SUBLANES:, :])
    return jnp.concatenate([y[:n - SUBLANES, :], bottom], axis=0)


def _rg_gates(xb, halo, cw_ref, vec_ref, wa_ref, wx_ref, at_start):
    shifted = [xb] + [_shift_down(xb, halo, k) for k in range(1, CONV_WIDTH)]
    xc = vec_ref[0:1, :] + shifted[0] * cw_ref[CONV_WIDTH - 1:CONV_WIDTH, :]
    for k in range(1, CONV_WIDTH):
        xc = xc + shifted[k] * cw_ref[CONV_WIDTH - 1 - k:CONV_WIDTH - k, :]
    heads = [slice(h * HEAD_DIM, (h + 1) * HEAD_DIM) for h in range(HEADS)]
    pa = jnp.concatenate([_dot(xc[:, s], wa_ref[h]) for h, s in enumerate(heads)], axis=1) + vec_ref[1:2, :]
    px = jnp.concatenate([_dot(xc[:, s], wx_ref[h]) for h, s in enumerate(heads)], axis=1) + vec_ref[2:3, :]
    ra, ia = _sigmoid(pa), _sigmoid(px)
    nl = -vec_ref[3:4, :]
    sp = jnp.maximum(nl, 0.0) + jnp.log(1.0 + jnp.exp(-jnp.abs(nl)))
    log_a = (-LRU_C) * ra * sp
    a = jnp.exp(log_a)
    th = jnp.tanh(log_a)
    is_t0 = jnp.logical_and(lax.broadcasted_iota(jnp.int32, xb.shape, 0) == 0, at_start)
    mult = jnp.where(is_t0, 1.0, jnp.sqrt(-2.0 * th / (1.0 - th)))
    return dict(shifted=shifted, xc=xc, ra=ra, ia=ia, sp=sp, a=a, mult=mult, is_t0=is_t0, heads=heads)


def _rg_specs(tt, nt, order):
    blk = tt // SUBLANES
    return dict(
        x=pl.BlockSpec((tt, D), lambda i: (order(i), 0)), y=pl.BlockSpec((tt, D), lambda i: (order(i), 1)),
        halo=pl.BlockSpec((SUBLANES, D), lambda i: (jnp.maximum(order(i) * blk - 1, 0), 0)),
        cw=_vec_spec(CONV_WIDTH), vec=_vec_spec(), w=pl.BlockSpec((HEADS, HEAD_DIM, HEAD_DIM), lambda i: (0, 0, 0)))


def _rg_forward(name, z, cw, vec, wa, wx):
    t = z.shape[0]
    tt = _tile(t, RG_ROWS)
    nt = t // tt
    sp = _rg_specs(tt, nt, lambda i: i)

    def body(zx_ref, zy_ref, halo_ref, cw_ref, vec_ref, wa_ref, wx_ref, p_ref, h_ref, a_s, u_s, carry):
        i = pl.program_id(0)

        @pl.when(i == 0)
        def _():
            carry[...] = jnp.zeros_like(carry)

        halo = jnp.where(i > 0, halo_ref[...], 0.0)
        g = _rg_gates(zx_ref[...], halo, cw_ref, vec_ref, wa_ref, wx_ref, i == 0)
        a_s[...] = g["a"]
        u_s[...] = g["mult"] * (g["ia"] * g["xc"])

        def group(gi, h):
            rows = pl.ds(pl.multiple_of(gi * SUBLANES, SUBLANES), SUBLANES)
            a8, u8 = a_s[rows, :], u_s[rows, :]
            out = []
            for j in range(SUBLANES):
                h = a8[j:j + 1, :] * h + u8[j:j + 1, :]
                out.append(h)
            h_ref[rows, :] = jnp.concatenate(out, axis=0)
            return h

        carry[0:1, :] = lax.fori_loop(0, tt // SUBLANES, group, carry[0:1, :])
        p_ref[...] = (h_ref[...] * _gelu(zy_ref[...])).astype(p_ref.dtype)

    return pl.pallas_call(
        body, name=name, grid=(nt,), in_specs=[sp["x"], sp["y"], sp["halo"], sp["cw"], sp["vec"], sp["w"], sp["w"]],
        out_specs=[_rows_spec(tt), _rows_spec(tt)],
        out_shape=[jax.ShapeDtypeStruct((t, D), MXU), jax.ShapeDtypeStruct((t, D), F32)],
        scratch_shapes=[pltpu.VMEM((tt, D), F32), pltpu.VMEM((tt, D), F32), pltpu.VMEM((SUBLANES, D), F32)],
        compiler_params=_params())(z, z, z, cw, vec, wa, wx)


def _rg_backward(name, dp, z, h, cw, vec, wa, wx):
    t = z.shape[0]
    tt = _tile(t, RG_ROWS)
    nt = t // tt
    rev = lambda i: nt - 1 - i
    sp = _rg_specs(tt, nt, rev)
    rows_rev = pl.BlockSpec((tt, D), lambda i: (rev(i), 0))

    def body(dp_ref, zx_ref, zy_ref, halo_ref, h_ref, hhalo_ref, cw_ref, vec_ref, wa_ref, wx_ref,
             dz_ref, dvec_ref, dcw_ref, dwa_ref, dwx_ref, a_s, d_s, carry, nxt):
        i = pl.program_id(0)
        j = rev(i)

        @pl.when(i == 0)
        def _():
            carry[...] = jnp.zeros_like(carry)
            nxt[...] = jnp.zeros_like(nxt)
            dvec_ref[...] = jnp.zeros_like(dvec_ref)
            dcw_ref[...] = jnp.zeros_like(dcw_ref)
            dwa_ref[...] = jnp.zeros_like(dwa_ref)
            dwx_ref[...] = jnp.zeros_like(dwx_ref)

        halo = jnp.where(j > 0, halo_ref[...], 0.0)
        g = _rg_gates(zx_ref[...], halo, cw_ref, vec_ref, wa_ref, wx_ref, j == 0)
        xc, ra, ia, a, mult = g["xc"], g["ra"], g["ia"], g["a"], g["mult"]
        hv, zy, dpv = h_ref[...], zy_ref[...], dp_ref[...]
        gelu, gelu_grad = _gelu_and_grad(zy)
        dyb = dpv * hv * gelu_grad
        a_s[...] = a
        d_s[...] = dpv * gelu

        def group(gi, c):
            rows = pl.ds(pl.multiple_of((tt // SUBLANES - 1 - gi) * SUBLANES, SUBLANES), SUBLANES)
            a8, d8 = a_s[rows, :], d_s[rows, :]
            out = [None] * SUBLANES
            for r in reversed(range(SUBLANES)):
                dht = d8[r:r + 1, :] + c
                out[r] = dht
                c = a8[r:r + 1, :] * dht
            d_s[rows, :] = jnp.concatenate(out, axis=0)
            return c

        carry[0:1, :] = lax.fori_loop(0, tt // SUBLANES, group, carry[0:1, :])
        dht = d_s[...]
        hprev = _shift_down(hv, jnp.where(j > 0, hhalo_ref[...], 0.0), 1)
        ixc = ia * xc
        dlog_a = dht * hprev * a + jnp.where(g["is_t0"], 0.0, dht * ixc * (-(a * a) / mult))
        dia = dht * mult * xc
        dxc = dht * mult * ia
        dra = dlog_a * ((-LRU_C) * g["sp"])
        nl = -vec_ref[3:4, :]
        dvec_ref[3:4, :] += _row_sum(dlog_a * ((-LRU_C) * ra)) * (-_sigmoid(nl))
        dpa = dra * ra * (1.0 - ra)
        dpx = dia * ia * (1.0 - ia)
        dvec_ref[1:2, :] += _row_sum(dpa)
        dvec_ref[2:3, :] += _row_sum(dpx)
        back = []
        for hd, s in enumerate(g["heads"]):
            dwa_ref[hd] += _dot_tn(xc[:, s], dpa[:, s])
            dwx_ref[hd] += _dot_tn(xc[:, s], dpx[:, s])
            back.append(_dot_nt(dpa[:, s], wa_ref[hd]) + _dot_nt(dpx[:, s], wx_ref[hd]))
        dxc = dxc + jnp.concatenate(back, axis=1)
        dvec_ref[0:1, :] += _row_sum(dxc)
        dxb = dxc * cw_ref[CONV_WIDTH - 1:CONV_WIDTH, :]
        for k in range(CONV_WIDTH):
            row = CONV_WIDTH - 1 - k
            dcw_ref[row:row + 1, :] += _row_sum(dxc * g["shifted"][k])
            if k:
                dxb = dxb + _shift_up(dxc, nxt[...], k) * cw_ref[row:row + 1, :]
        nxt[...] = dxc[0:SUBLANES, :]
        dz_ref[:, 0:D] = dxb.astype(dz_ref.dtype)
        dz_ref[:, D:2 * D] = dyb.astype(dz_ref.dtype)

    hhalo = pl.BlockSpec((SUBLANES, D), lambda i: (jnp.maximum(rev(i) * (tt // SUBLANES) - 1, 0), 0))
    wacc = pl.BlockSpec((HEADS, HEAD_DIM, HEAD_DIM), lambda i: (0, 0, 0))
    return pl.pallas_call(
        body, name=name, grid=(nt,),
        in_specs=[rows_rev, sp["x"], sp["y"], sp["halo"], rows_rev, hhalo, sp["cw"], sp["vec"], sp["w"], sp["w"]],
        out_specs=[pl.BlockSpec((tt, 2 * D), lambda i: (rev(i), 0)), _vec_spec(), _vec_spec(), wacc, wacc],
        out_shape=[jax.ShapeDtypeStruct((t, 2 * D), MXU), jax.ShapeDtypeStruct((SUBLANES, D), F32),
                   jax.ShapeDtypeStruct((SUBLANES, D), F32), jax.ShapeDtypeStruct((HEADS, HEAD_DIM, HEAD_DIM), F32),
                   jax.ShapeDtypeStruct((HEADS, HEAD_DIM, HEAD_DIM), F32)],
        scratch_shapes=[pltpu.VMEM((tt, D), F32), pltpu.VMEM((tt, D), F32), pltpu.VMEM((SUBLANES, D), F32),
                        pltpu.VMEM((SUBLANES, D), F32)],
        compiler_params=_params())(dp, z, z, z, h, h, cw, vec, wa, wx)


def _hg_chunk(zq, zf, zi, lb):
    c = HG_CHUNK
    q, q_grad = _silu_and_grad(zq)
    sg = _sigmoid(zf)
    fg = lb + (1.0 - lb) * sg
    k = 1.0 - fg
    row, col = lax.broadcasted_iota(jnp.int32, (c, c), 0), lax.broadcasted_iota(jnp.int32, (c, c), 1)
    tri, tri_t = (row >= col).astype(F32), (row <= col).astype(F32)
    b = _dot_exact(tri, jnp.log(fg))
    mid, last = b[c // 2 - 1:c // 2, :], b[c - 1:c, :]
    eq = jnp.exp(jnp.minimum(b - mid, EXP_CLAMP))
    ek = jnp.exp(jnp.minimum(mid - b, EXP_CLAMP))
    eb = jnp.exp(b)
    ed = jnp.exp(last - b)
    return dict(q=q, q_grad=q_grad, sg=sg, fg=fg, k=k, v=zi, eq=eq, ek=ek, eb=eb, ed=ed, elast=jnp.exp(last), tri=tri, tri_t=tri_t,
                qe=q * eq, ke=k * ek, qb=q * eb, kd=k * ed)


def _hg_specs(tt, order):
    return [pl.BlockSpec((tt, D), lambda i, col=col: (order(i), col)) for col in range(4)]


def _hg_forward(name, z, vec):
    t = z.shape[0]
    tt = _tile(t, ROW_TILE)
    nt, nc = t // tt, tt // HG_CHUNK
    heads = [slice(h * HEAD_DIM, (h + 1) * HEAD_DIM) for h in range(HEADS)]

    def body(zq_ref, zf_ref, zi_ref, zg_ref, vec_ref, p_ref, o_ref, ss_ref, state):
        @pl.when(pl.program_id(0) == 0)
        def _():
            state[...] = jnp.zeros_like(state)

        lb, gain = vec_ref[0:1, :], vec_ref[1:2, :]

        def chunk(ci, carry):
            rows = pl.ds(pl.multiple_of(ci * HG_CHUNK, HG_CHUNK), HG_CHUNK)
            g = _hg_chunk(zq_ref[rows, :], zf_ref[rows, :], zi_ref[rows, :], lb)
            zg = zg_ref[rows, :]
            causal = g["tri"] > 0.0
            qe_m, ke_m, qb_m, kd_m, v_m = (g[n].astype(MXU) for n in ("qe", "ke", "qb", "kd", "v"))
            atts = [_dot_nt(qe_m[:, s], ke_m[:, s]) for s in heads]
            o_parts, p_parts = [], []
            for hd, s in enumerate(heads):
                st = state[hd]
                ss_ref[ci, hd] = st
                att = jnp.where(causal, atts[hd], 0.0)
                o_parts.append(_dot(att, v_m[:, s]) + _dot_nt(qb_m[:, s], st))
                state[hd] = st * g["elast"][:, s] + _dot_tn(v_m[:, s], kd_m[:, s])
            for o, s in zip(o_parts, heads):
                r = lax.rsqrt(jnp.mean(o * o, axis=-1, keepdims=True) + GNORM_EPS)
                p_parts.append((o * r) * gain[:, s])
            o_ref[rows, :] = jnp.concatenate(o_parts, axis=1)
            p_ref[rows, :] = (jnp.concatenate(p_parts, axis=1) * _silu(zg)).astype(p_ref.dtype)
            return carry

        lax.fori_loop(0, nc, chunk, 0, unroll=True)

    return pl.pallas_call(
        body, name=name, grid=(nt,), in_specs=_hg_specs(tt, lambda i: i) + [_vec_spec()],
        out_specs=[_rows_spec(tt), _rows_spec(tt), pl.BlockSpec((nc, HEADS, HEAD_DIM, HEAD_DIM), lambda i: (i, 0, 0, 0))],
        out_shape=[jax.ShapeDtypeStruct((t, D), MXU), jax.ShapeDtypeStruct((t, D), F32),
                   jax.ShapeDtypeStruct((t // HG_CHUNK, HEADS, HEAD_DIM, HEAD_DIM), F32)],
        scratch_shapes=[pltpu.VMEM((HEADS, HEAD_DIM, HEAD_DIM), F32)], compiler_params=_params())(z, z, z, z, vec)


def _hg_backward(name, dp, z, o, ss, vec):
    t = z.shape[0]
    tt = _tile(t, ROW_TILE)
    nt, nc = t // tt, tt // HG_CHUNK
    rev = lambda i: nt - 1 - i
    heads = [slice(h * HEAD_DIM, (h + 1) * HEAD_DIM) for h in range(HEADS)]
    rows_rev = pl.BlockSpec((tt, D), lambda i: (rev(i), 0))

    def body(dp_ref, zq_ref, zf_ref, zi_ref, zg_ref, o_ref, ss_ref, vec_ref, dz_ref, acc_ref, dstate):
        @pl.when(pl.program_id(0) == 0)
        def _():
            dstate[...] = jnp.zeros_like(dstate)
            acc_ref[...] = jnp.zeros_like(acc_ref)

        lb, gain = vec_ref[0:1, :], vec_ref[1:2, :]

        def chunk(cr, carry):
            ci = nc - 1 - cr
            rows = pl.ds(pl.multiple_of(ci * HG_CHUNK, HG_CHUNK), HG_CHUNK)
            zq, zg = zq_ref[rows, :], zg_ref[rows, :]
            g = _hg_chunk(zq, zf_ref[rows, :], zi_ref[rows, :], lb)
            ov, dpv = o_ref[rows, :], dp_ref[rows, :]
            causal = g["tri"] > 0.0
            silu_g, silu_g_grad = _silu_and_grad(zg)
            don = dpv * silu_g
            dgate = dpv * silu_g_grad
            qe_m, ke_m, qb_m, kd_m, v_m = (g[n].astype(MXU) for n in ("qe", "ke", "qb", "kd", "v"))
            parts = {n: [] for n in ("dzg", "dgain", "dv", "dqe", "dke", "dqb", "dkd", "dlast")}
            dos, atts, datts = [], [], []
            for hd, s in enumerate(heads):
                oh = ov[:, s]
                r = lax.rsqrt(jnp.mean(oh * oh, axis=-1, keepdims=True) + GNORM_EPS)
                on = oh * r
                parts["dzg"].append(dgate[:, s] * (on * gain[:, s]))
                parts["dgain"].append(_row_sum(don[:, s] * on))
                dtmp = don[:, s] * gain[:, s]
                dos.append((r * (dtmp - on * jnp.mean(dtmp * on, axis=-1, keepdims=True))).astype(MXU))
            for hd, s in enumerate(heads):
                atts.append(_dot_nt(qe_m[:, s], ke_m[:, s]))
                datts.append(_dot_nt(dos[hd], v_m[:, s]))
            for hd, s in enumerate(heads):
                do, att, datt = dos[hd], jnp.where(causal, atts[hd], 0.0).astype(MXU), jnp.where(causal, datts[hd], 0.0).astype(MXU)
                st, dst = ss_ref[ci, hd], dstate[hd]
                st_m, dst_m = st.astype(MXU), dst.astype(MXU)
                qe, ke, qb, kd, v = qe_m[:, s], ke_m[:, s], qb_m[:, s], kd_m[:, s], v_m[:, s]
                parts["dv"].append(_dot_tn(att, do) + _dot_nt(kd, dst_m))
                parts["dqe"].append(_dot(datt, ke))
                parts["dke"].append(_dot_tn(datt, qe))
                parts["dqb"].append(_dot(do, st_m))
                parts["dkd"].append(_dot(v, dst_m))
                parts["dlast"].append(g["elast"][:, s] * _row_sum(dst * st))
                dstate[hd] = dst * g["elast"][:, s] + _dot_tn(do, qb)
            whole = {n: jnp.concatenate(p, axis=1) for n, p in parts.items()}
            acc_ref[1:2, :] += whole["dgain"]
            dqe, dke, dqb, dkd = whole["dqe"], whole["dke"], whole["dqb"], whole["dkd"]
            dq = dqe * g["eq"] + dqb * g["eb"]
            dk = dke * g["ek"] + dkd * g["ed"]
            dkdk = dkd * g["kd"]
            db = dqe * qe_m.astype(F32) - dke * ke_m.astype(F32) + dqb * g["qb"] - dkdk
            dlogf = _dot_exact(g["tri_t"], db) + (whole["dlast"] + _row_sum(dkdk))
            dfg = dlogf / g["fg"] - dk
            sg = g["sg"]
            acc_ref[0:1, :] += _row_sum(dfg * (1.0 - sg))
            dz_ref[rows, 0:D] = (dq * g["q_grad"]).astype(dz_ref.dtype)
            dz_ref[rows, D:2 * D] = (dfg * (1.0 - lb) * sg * (1.0 - sg)).astype(dz_ref.dtype)
            dz_ref[rows, 2 * D:3 * D] = whole["dv"].astype(dz_ref.dtype)
            dz_ref[rows, 3 * D:4 * D] = whole["dzg"].astype(dz_ref.dtype)
            return carry

        lax.fori_loop(0, nc, chunk, 0, unroll=True)

    return pl.pallas_call(
        body, name=name, grid=(nt,),
        in_specs=[rows_rev] + _hg_specs(tt, rev) + [rows_rev, pl.BlockSpec((nc, HEADS, HEAD_DIM, HEAD_DIM), lambda i: (rev(i), 0, 0, 0)),
                                                  _vec_spec()],
        out_specs=[pl.BlockSpec((tt, 4 * D), lambda i: (rev(i), 0)), _vec_spec()],
        out_shape=[jax.ShapeDtypeStruct((t, 4 * D), MXU), jax.ShapeDtypeStruct((SUBLANES, D), F32)],
        scratch_shapes=[pltpu.VMEM((HEADS, HEAD_DIM, HEAD_DIM), F32)], compiler_params=_params())(dp, z, z, z, z, o, ss, vec)


def _mod_forward(name, c_all, mod_w, lower):
    depth, _, sw = mod_w.shape

    def body(c_ref, w_ref, lo_ref, cs_ref, mod_ref, lb_ref):
        cs = _silu(c_ref[...])
        mod_ref[...] = _dot(cs, w_ref[...])

        @pl.when(pl.program_id(0) == 0)
        def _():
            cs_ref[...] = cs
            lo = lo_ref[...]
            e = jnp.exp(lo - jnp.max(lo, axis=0, keepdims=True))
            sm = e / jnp.sum(e, axis=0, keepdims=True)
            lb_ref[0:1, :] = jnp.zeros((1, D), F32)
            for l in range(1, depth):
                lb_ref[l:l + 1, :] = lb_ref[l - 1:l, :] + sm[l:l + 1, :]

    return pl.pallas_call(
        body, name=name, grid=(depth,),
        in_specs=[pl.BlockSpec((N_DEV, D), lambda l: (0, 0)), pl.BlockSpec((None, D, sw), lambda l: (l, 0, 0)),
                  pl.BlockSpec((depth, D), lambda l: (0, 0))],
        out_specs=[pl.BlockSpec((N_DEV, D), lambda l: (0, 0)), pl.BlockSpec((None, N_DEV, sw), lambda l: (l, 0, 0)),
                   pl.BlockSpec((depth, D), lambda l: (0, 0))],
        out_shape=[jax.ShapeDtypeStruct((N_DEV, D), F32), jax.ShapeDtypeStruct((depth, N_DEV, sw), F32),
                   jax.ShapeDtypeStruct((depth, D), F32)], compiler_params=_params())(c_all, mod_w, lower)


def _mod_weight_grad(name, cs_t, dmod):
    depth, pad, sw = dmod.shape

    def body(c_ref, d_ref, g_ref):
        g_ref[...] = _dot(c_ref[...], d_ref[...])

    return pl.pallas_call(
        body, name=name, grid=(depth,),
        in_specs=[pl.BlockSpec((D, pad), lambda l: (0, 0)), pl.BlockSpec((None, pad, sw), lambda l: (l, 0, 0))],
        out_specs=pl.BlockSpec((None, D, sw), lambda l: (l, 0, 0)),
        out_shape=jax.ShapeDtypeStruct((depth, D, sw), F32), compiler_params=_params())(cs_t, dmod)


def _lower_bound_grad(name, lower, dlb):
    depth = lower.shape[0]

    def body(lo_ref, d_ref, out_ref):
        lo = lo_ref[...]
        e = jnp.exp(lo - jnp.max(lo, axis=0, keepdims=True))
        sm = e / jnp.sum(e, axis=0, keepdims=True)
        out_ref[...] = jnp.zeros_like(out_ref)
        dsm = [jnp.zeros((1, D), F32)]
        for l in range(1, depth):
            tail = d_ref[l:l + 1, :]
            for m in range(l + 1, depth):
                tail = tail + d_ref[m:m + 1, :]
            dsm.append(tail)
        inner = sm[1:2, :] * dsm[1]
        for l in range(2, depth):
            inner = inner + sm[l:l + 1, :] * dsm[l]
        for l in range(depth):
            out_ref[l:l + 1, :] = sm[l:l + 1, :] * (dsm[l] - inner)
        for j in range(SUBLANES - depth):
            row = d_ref[depth + j:depth + j + 1, :]
            tot = row[:, 0:HEAD_DIM]
            for hd in range(1, HEADS):
                tot = tot + row[:, hd * HEAD_DIM:(hd + 1) * HEAD_DIM]
            out_ref[depth + j:depth + j + 1, 0:HEAD_DIM] = tot

    return pl.pallas_call(body, name=name, out_shape=jax.ShapeDtypeStruct((SUBLANES, D), F32))(lower, dlb)


def _adamw(name, parts, w, m, v, layer=None, prev=None):
    p, r, c = parts.shape
    tr = _tile(r, max(SUBLANES, ADAMW_STEP_BYTES // (4 * c * (p + 7))))
    stacked = layer is not None
    n_prev = 4 if prev is not None else 0

    def body(*refs):
        parts_ref, w_ref, m_ref, v_ref = refs[:4]
        g_ref, d_ref, m_out, v_out = refs[4 + n_prev:]
        g = parts_ref[0].astype(F32)
        for q in range(1, p):
            g = g + parts_ref[q].astype(F32)
        m2 = ADAM_B1 * m_ref[...] + (1.0 - ADAM_B1) * g
        v2 = ADAM_B2 * v_ref[...] + (1.0 - ADAM_B2) * (g * g)
        m_hat = m2 / (1.0 - ADAM_B1 ** ADAM_STEP)
        v_hat = v2 / (1.0 - ADAM_B2 ** ADAM_STEP)
        g_ref[...] = g
        d_ref[...] = -ADAM_LR * (m_hat / (jnp.sqrt(v_hat) + ADAM_EPS) + ADAM_WD * w_ref[...])
        m_out[...] = m2
        v_out[...] = v2

    if stacked:
        spec = pl.BlockSpec((None, tr, c), lambda i: (layer, i, 0))
    else:
        spec = pl.BlockSpec((tr, c), lambda i: (i, 0))
    return pl.pallas_call(
        body, name=name, grid=(r // tr,),
        in_specs=[pl.BlockSpec((p, tr, c), lambda i: (0, i, 0)), spec, spec, spec] + [ANY] * n_prev, out_specs=[spec] * 4,
        out_shape=[jax.ShapeDtypeStruct(w.shape, F32)] * 4,
        input_output_aliases={4 + q: q for q in range(n_prev)}, compiler_params=_params(),
    )(parts, w, m, v, *(prev or []))


def _hbm(a):
    return pltpu.with_memory_space_constraint(a, pltpu.HBM)


def _vec(*rows):
    rows = [r.reshape(1, D).astype(F32) for r in rows]
    packed = jnp.concatenate(rows + [jnp.zeros((SUBLANES - len(rows), D), F32)], axis=0)
    return _hbm(packed)


def _pad_rows(a, rows=SUBLANES):
    a = a.reshape(-1, a.shape[-1])
    return jnp.concatenate([a, jnp.zeros((rows - a.shape[0], a.shape[1]), a.dtype)], axis=0) if a.shape[0] < rows else a


def kernel(x, c, mod_w, mod_b, norm_mix, norm_mlp, norm_final, rg_w_in, rg_conv_w, rg_conv_b, rg_w_a, rg_b_a, rg_w_x, rg_b_x, rg_lambda, rg_w_out, hg_w_in, hg_lower_bounds, hg_gnorm, hg_w_out, mlp_w1, mlp_w2, loss_target, m_mod_w, m_mod_b, m_norm_mix, m_norm_mlp, m_norm_final, m_rg_w_in, m_rg_conv_w, m_rg_conv_b, m_rg_w_a, m_rg_b_a, m_rg_w_x, m_rg_b_x, m_rg_lambda, m_rg_w_out, m_hg_w_in, m_hg_lower_bounds, m_hg_gnorm, m_hg_w_out, m_mlp_w1, m_mlp_w2, v_mod_w, v_mod_b, v_norm_mix, v_norm_mlp, v_norm_final, v_rg_w_in, v_rg_conv_w, v_rg_conv_b, v_rg_w_a, v_rg_b_a, v_rg_w_x, v_rg_b_x, v_rg_lambda, v_rg_w_out, v_hg_w_in, v_hg_lower_bounds, v_hg_gnorm, v_hg_w_out, v_mlp_w1, v_mlp_w2):
    me = 4 * lax.axis_index("x") + 2 * lax.axis_index("y") + lax.axis_index("c")
    x0 = x[0]
    target = loss_target[0]
    n_rg, n_hg = rg_w_in.shape[0], hg_w_in.shape[0]
    sw_mod = mod_w.shape[2]

    c_all, cw_all = _all_gather([_pad_rows(c), rg_conv_w.reshape(n_rg * CONV_WIDTH, -1)], "gather_cond")
    c_all = c_all[:, 0, :]
    conv_w = cw_all.transpose(1, 0, 2).reshape(n_rg, CONV_WIDTH, D)
    cs_all, mod_part, lb_all = _mod_forward("mod_forward", c_all, mod_w, hg_lower_bounds)
    (mod_gathered,) = _all_gather([mod_part.reshape(DEPTH * N_DEV, sw_mod)], "gather_mod")

    ids = iter(range(5 * DEPTH))
    shards = []
    for layer in range(DEPTH):
        j = layer // 2
        w_in, w_out = (rg_w_in[j], rg_w_out[j]) if layer % 2 == 0 else (hg_w_in[j], hg_w_out[j])
        shards.append([w_in.astype(MXU), w_out.astype(MXU), mlp_w1[layer].astype(MXU), mlp_w2[layer].astype(MXU)])
    shards, mod_gathered = lax.optimization_barrier((shards, mod_gathered))
    weights = []
    for layer in range(DEPTH):
        g_in, g_out = _all_gather_async(shards[layer][:2], f"gather_mixer_weights_{layer}", next(ids), ["cols", "rows"])
        g_w1, g_w2 = _all_gather_async(shards[layer][2:], f"gather_mlp_weights_{layer}", next(ids), ["cols", "rows"])
        weights.append(dict(w_in=g_in, w_out=g_out, w1=g_w1, w2=g_w2))

    mod_mine = lax.dynamic_index_in_dim(mod_gathered.reshape(N_DEV, DEPTH, N_DEV, sw_mod), me, axis=2, keepdims=False)
    mod = mod_mine.transpose(1, 0, 2).reshape(DEPTH, 6, D) + mod_b.reshape(DEPTH, 6, D)

    saved = []
    xl = x0
    for layer in range(DEPTH):
        j, wt = layer // 2, weights[layer]
        is_rg = layer % 2 == 0
        s = dict(x=xl)
        s["vec_mix"] = _vec(norm_mix[layer], mod[layer, 1], mod[layer, 0])
        s["vec_mlp"] = _vec(norm_mlp[layer], mod[layer, 4], mod[layer, 3])
        s["gate_mix"], s["gate_mlp"] = _vec(mod[layer, 2]), _vec(mod[layer, 5])
        s["h"] = _hbm(_norm_mod("norm_mod", xl, s["vec_mix"]))
        if is_rg:
            (s["z"],) = _mm_tokens("rg_in", s["h"], wt["w_in"], False, (F32,))
            s["cw"] = conv_w[j]
            s["vec"] = _vec(rg_conv_b[j], rg_b_a[j], rg_b_x[j], rg_lambda[j])
            s["wa"], s["wx"] = rg_w_a[j].astype(MXU), rg_w_x[j].astype(MXU)
            s["p"], s["hr"] = _rg_forward("rg_forward", s["z"], s["cw"], s["vec"], s["wa"], s["wx"])
        else:
            (s["z"],) = _mm_tokens("hg_in", s["h"], wt["w_in"], False, (F32,))
            s["vec"] = _vec(lb_all[layer], jnp.tile(hg_gnorm[j], HEADS))
            s["p"], s["o"], s["ss"] = _hg_forward("hg_forward", s["z"], s["vec"])
        s["y"], s["x1"] = _mm_tokens("mix_out", s["p"], wt["w_out"], False, (F32, F32), _ep_residual, extras=(xl, s["gate_mix"]))
        s["p"] = _hbm(s["p"])
        s["h2"] = _hbm(_norm_mod("norm_mod", s["x1"], s["vec_mlp"]))
        s["r"], s["ff"], xl = _mlp_forward("mlp_forward", s["h2"], wt["w1"], wt["w2"], s["x1"], s["gate_mlp"])
        saved.append(s)

    dx, head, dff = _loss_head("loss_head", xl, target, _vec(norm_final), saved[-1]["ff"], saved[-1]["gate_mlp"])
    g_mlp = head[GATE_ROW:GATE_ROW + 1]

    results = {}
    big = dict(rg_w_in=(rg_w_in, m_rg_w_in, v_rg_w_in), rg_w_out=(rg_w_out, m_rg_w_out, v_rg_w_out),
               hg_w_in=(hg_w_in, m_hg_w_in, v_hg_w_in), hg_w_out=(hg_w_out, m_hg_w_out, v_hg_w_out),
               mlp_w1=(mlp_w1, m_mlp_w1, v_mlp_w1), mlp_w2=(mlp_w2, m_mlp_w2, v_mlp_w2))

    def update_layer(layer, landed):
        mixer = "rg" if layer % 2 == 0 else "hg"
        l_in, l_out, l_w1, l_w2 = landed
        for nm, parts, idx in ((f"{mixer}_w_in", l_in, layer // 2), (f"{mixer}_w_out", l_out, layer // 2),
                               ("mlp_w1", l_w1, layer), ("mlp_w2", l_w2, layer)):
            w, m, v = big[nm]
            results[nm] = _adamw(f"adamw_{nm}", parts, w, m, v, layer=idx, prev=results.get(nm))

    landed = [None] * DEPTH
    small = [None] * DEPTH
    parity = lax.axis_index("c").astype(jnp.int32).reshape(1)

    def send_chip_sums(layer, grads, from_sibling, anchor):
        sums = _pair_sum("pair_sum", parity, grads, from_sibling)
        sums, anchor = lax.optimization_barrier((sums, anchor))
        landed[layer] = _chip_exchange_async(sums, f"exchange_grads_{layer}", next(ids))
        return anchor

    pending = None
    for layer in reversed(range(DEPTH)):
        j, wt, s = layer // 2, weights[layer], saved[layer]
        is_rg = layer % 2 == 0
        dff = _hbm(dff)
        da, dh2 = _mlp_backward("mlp_backward", dff, s["r"], wt["w1"], wt["w2"])
        if pending is not None:
            da = send_chip_sums(*pending, da)
        dw2 = _mm_grad("mlp_out_grad", s["r"], dff, False, prologue=_square).reshape(N_DEV, -1, D)
        dw1 = _mm_grad("mlp_in_grad", s["h2"], da, True)
        dx1, n_mlp, dyb = _norm_mod_grad("norm_mod_grad", dh2, s["x1"], s["vec_mlp"], dx, s["y"], s["gate_mix"])
        dyb = _hbm(dyb)
        (dp,) = _mm_tokens("mix_out_t", dyb, wt["w_out"], True, (F32,))
        dw_out = _hbm(_mm_grad("mix_out_grad", s["p"], dyb, False).reshape(N_DEV, -1, D))
        if is_rg:
            dz, dvec, dcw, dwa, dwx = _rg_backward("rg_backward", dp, s["z"], s["hr"], s["cw"], s["vec"], s["wa"], s["wx"])
            gate_grads, dz = lax.optimization_barrier(
                ([dwa.reshape(-1, HEAD_DIM).astype(MXU), dwx.reshape(-1, HEAD_DIM).astype(MXU)], dz))
            gate_parts = _all_gather_async(gate_grads, f"gather_gate_grads_{j}", next(ids), ["stack", "stack"])
            mixer_small = dict(dvec=dvec, dcw=dcw, gate_parts=gate_parts)
            dw_in = _mm_grad("rg_in_grad", s["h"], dz, True)
            (dh,) = _mm_tokens("rg_in_t", dz, wt["w_in"], True, (F32,))
        else:
            dz, dvec = _hg_backward("hg_backward", dp, s["z"], s["o"], s["ss"], s["vec"])
            mixer_small = dict(dvec=dvec)
            dw_in = _mm_grad("hg_in_grad", s["h"], dz, True)
            (dh,) = _mm_tokens("hg_in_t", dz, wt["w_in"], True, (F32,))
        grads = [dw_in, dw_out, dw1, dw2]
        pending = (layer, grads, _sibling_send_async(grads, f"pair_grads_{layer}", next(ids)))
        small[layer] = dict(g_mlp=g_mlp, n_mlp=n_mlp, g_mix=n_mlp[GATE_ROW:GATE_ROW + 1], **mixer_small)
        if layer:
            below = saved[layer - 1]
            dx, n_mix, dff = _norm_mod_grad("norm_mod_grad", dh, s["x"], s["vec_mix"], dx1, below["ff"], below["gate_mlp"])
            g_mlp = n_mix[GATE_ROW:GATE_ROW + 1]
        else:
            dx, n_mix = _norm_mod_grad("norm_mod_grad_first", dh, s["x"], s["vec_mix"], dx1)
        small[layer]["n_mix"] = n_mix
        if layer + 1 < DEPTH:
            stream = (dx, dff) if layer else (dx,)
            landed[layer + 1], stream = lax.optimization_barrier((landed[layer + 1], stream))
            if "gate_parts" in small[layer + 1]:
                small[layer + 1]["gate_parts"], stream = lax.optimization_barrier((small[layer + 1]["gate_parts"], stream))
            dx, dff = stream if layer else (stream[0], None)
            update_layer(layer + 1, landed[layer + 1])

    dlb_rows = [jnp.zeros((1, D), F32) if l % 2 == 0 else small[l]["dvec"][0:1] for l in range(DEPTH)]
    dgn_rows = [small[2 * j + 1]["dvec"][1:2] for j in range(n_hg)]
    lb_grad = _lower_bound_grad("lower_bound_grad", hg_lower_bounds, _pad_rows(jnp.concatenate(dlb_rows + dgn_rows, axis=0)))
    dmod = jnp.stack([jnp.concatenate([small[l]["n_mix"][2], small[l]["n_mix"][1], small[l]["g_mix"][0],
                                       small[l]["n_mlp"][2], small[l]["n_mlp"][1], small[l]["g_mlp"][0]]) for l in range(DEPTH)])
    groups = [
        dmod.reshape(DEPTH * 6, D),
        jnp.stack([small[l]["n_mix"][0] for l in range(DEPTH)]),
        jnp.stack([small[l]["n_mlp"][0] for l in range(DEPTH)]),
        head[0:1],
        jnp.stack([small[2 * j]["dvec"][0] for j in range(n_rg)]),
        jnp.stack([small[2 * j]["dvec"][1] for j in range(n_rg)]),
        jnp.stack([small[2 * j]["dvec"][2] for j in range(n_rg)]),
        jnp.stack([small[2 * j]["dvec"][3] for j in range(n_rg)]),
        lb_grad[0:DEPTH],
        jnp.concatenate([lb_grad[DEPTH + j:DEPTH + j + 1, 0:HEAD_DIM] for j in range(n_hg)]
                        + [jnp.zeros((1, D - n_hg * HEAD_DIM), F32)], axis=1),
        jnp.concatenate([small[2 * j]["dcw"][0:CONV_WIDTH] for j in range(n_rg)], axis=0),
        head[1:2],
    ]
    params = [mod_b, norm_mix, norm_mlp, norm_final, rg_conv_b, rg_b_a, rg_b_x, rg_lambda, hg_lower_bounds, hg_gnorm]
    moms = [m_mod_b, m_norm_mix, m_norm_mlp, m_norm_final, m_rg_conv_b, m_rg_b_a, m_rg_b_x, m_rg_lambda, m_hg_lower_bounds, m_hg_gnorm]
    vars_ = [v_mod_b, v_norm_mix, v_norm_mlp, v_norm_final, v_rg_conv_b, v_rg_b_a, v_rg_b_x, v_rg_lambda, v_hg_lower_bounds, v_hg_gnorm]
    offsets, rows_of, at = [], [], 0
    for g in groups:
        offsets.append(at)
        rows_of.append(g.shape[0])
        at += -(-g.shape[0] // SUBLANES) * SUBLANES
    packed = jnp.concatenate([_pad_rows(g, -(-g.shape[0] // SUBLANES) * SUBLANES) for g in groups], axis=0)
    (small_parts,) = _all_gather([packed], "gather_small_grads")
    last_layer, last_grads, from_sibling = pending
    from_sibling, small_parts = lax.optimization_barrier((from_sibling, small_parts))
    send_chip_sums(last_layer, last_grads, from_sibling, dx)

    def pack_like(arrs):
        out = []
        for g_rows, off, a in zip(rows_of, offsets, arrs):
            flat = a.reshape(-1)
            flat = jnp.concatenate([flat, jnp.zeros((g_rows * D - flat.shape[0],), F32)])
            out.append(_pad_rows(flat.reshape(g_rows, D), -(-g_rows // SUBLANES) * SUBLANES))
        rest = packed.shape[0] - sum(o.shape[0] for o in out)
        return jnp.concatenate(out + [jnp.zeros((rest, D), F32)], axis=0)

    small_out = _adamw("adamw_small", small_parts, pack_like(params), pack_like(moms), pack_like(vars_))

    def unpack(q, idx, like):
        rows = small_out[q][offsets[idx]:offsets[idx] + rows_of[idx]]
        return rows.reshape(-1)[:like.size].reshape(like.shape)

    loss = jnp.sum(small_out[0][offsets[11]])
    names =["mod_b", "norm_mix", "norm_mlp", "norm_final", "rg_conv_b", "rg_b_a", "rg_b_x", "rg_lambda", "hg_lower_bounds", "hg_gnorm"]
    for idx, (nm, like) in enumerate(zip(names, params)):
        results[nm] = [unpack(q, idx, like) for q in range(4)]

    cw_parts = lax.dynamic_slice_in_dim(small_parts[:, offsets[10]:offsets[10] + n_rg * CONV_WIDTH, :], me * (D // N_DEV), D // N_DEV, axis=2)
    shp = rg_conv_w.shape
    results["rg_conv_w"] = [o.reshape(shp) for o in _adamw(
        "adamw_conv", cw_parts, rg_conv_w.reshape(-1, shp[-1]), m_rg_conv_w.reshape(-1, shp[-1]), v_rg_conv_w.reshape(-1, shp[-1]))]
    shp = rg_w_a.shape
    stacked = (n_rg, HEADS * HEAD_DIM, HEAD_DIM)
    for nm, which, (w, m, v) in (("rg_w_a", 0, (rg_w_a, m_rg_w_a, v_rg_w_a)), ("rg_w_x", 1, (rg_w_x, m_rg_w_x, v_rg_w_x))):
        out = None
        for j in reversed(range(n_rg)):
            out = _adamw("adamw_gate", small[2 * j]["gate_parts"][which], w.reshape(stacked), m.reshape(stacked),
                         v.reshape(stacked), layer=j, prev=out)
        results[nm] = [o.reshape(shp) for o in out]

    dmod_all = small_parts[:, 0:DEPTH * 6, :].reshape(N_DEV, DEPTH, 6 * D)
    dmod_cols = lax.dynamic_slice_in_dim(dmod_all, me * sw_mod, sw_mod, axis=2).transpose(1, 0, 2)
    pad = HEAD_DIM - N_DEV
    dmod_pad = jnp.concatenate([dmod_cols, jnp.zeros((DEPTH, pad, sw_mod), F32)], axis=1).astype(MXU)
    cs_t = jnp.concatenate([cs_all.T, jnp.zeros((D, pad), F32)], axis=1).astype(MXU)
    g_mod_w = _mod_weight_grad("mod_weight_grad", cs_t, dmod_pad)
    results["mod_w"] = [o.reshape(mod_w.shape) for o in _adamw(
        "adamw_mod", g_mod_w.reshape(1, -1, sw_mod), mod_w.reshape(-1, sw_mod), m_mod_w.reshape(-1, sw_mod), v_mod_w.reshape(-1, sw_mod))]

    update_layer(0, landed[0])

    order = ["mod_w", "mod_b", "norm_mix", "norm_mlp", "norm_final", "rg_w_in", "rg_conv_w", "rg_conv_b", "rg_w_a", "rg_b_a", "rg_w_x",
             "rg_b_x", "rg_lambda", "rg_w_out", "hg_w_in", "hg_lower_bounds", "hg_gnorm", "hg_w_out", "mlp_w1", "mlp_w2"]
    return (loss, dx[None], *[results[n][0] for n in order], *[results[n][1] for n in order],
            *[results[n][2] for n in order], *[results[n][3] for n in order])
```

```python
import functools

import jax
import jax.numpy as jnp
from jax import lax
from jax.experimental import pallas as pl
from jax.experimental.pallas import tpu as pltpu
from jax.experimental.pallas import tpu_sc as plsc

F32 = jnp.float32
MXU = jnp.bfloat16

N_DEV = 8
D = 1024
DEPTH = 4
HEADS = 8
HEAD_DIM = 128
CONV_WIDTH = 4
LRU_C = 8.0
HG_CHUNK = 64
NORM_EPS = 1e-6
GNORM_EPS = 1e-5
ADAM_LR = 0.001
ADAM_B1 = 0.9
ADAM_B2 = 0.999
ADAM_EPS = 1e-08
ADAM_WD = 0.01
ADAM_STEP = 10
GELU_C = 0.7978845608028654
GELU_K = 0.044715
EXP_CLAMP = 80.0
SUBLANES = 8
VMEM_LIMIT = 48 * 1024 * 1024
ROW_TILE = 256
RG_ROWS = 256
MM_TILE = 1024
MM_TOKENS = 512
MLP_TOKENS = 256
ADAMW_STEP_BYTES = 8 * 1024 * 1024

MESH = pl.DeviceIdType.MESH
ANY = pl.BlockSpec(memory_space=pl.ANY)


def _params():
    return pltpu.CompilerParams(vmem_limit_bytes=VMEM_LIMIT)


def _tile(n, target):
    if n <= target:
        return n
    t = target // SUBLANES * SUBLANES
    while n % t:
        t -= SUBLANES
    return t


def _sigmoid(x):
    return 1.0 / (1.0 + jnp.exp(-x))


def _silu(x):
    return x * _sigmoid(x)


def _silu_and_grad(x):
    s = _sigmoid(x)
    return x * s, s * (1.0 + x * (1.0 - s))


def _gelu(y):
    return 0.5 * y * (1.0 + jnp.tanh(GELU_C * (y + GELU_K * y * y * y)))


def _gelu_and_grad(y):
    y2 = y * y
    th = jnp.tanh(GELU_C * (y + GELU_K * y2 * y))
    half = 0.5 * (1.0 + th)
    return y * half, half + 0.5 * y * (1.0 - th * th) * GELU_C * (1.0 + 3.0 * GELU_K * y2)


def _dot(a, b):
    return lax.dot_general(a.astype(MXU), b.astype(MXU), (((1,), (0,)), ((), ())), preferred_element_type=F32)


def _dot_nt(a, b):
    return lax.dot_general(a.astype(MXU), b.astype(MXU), (((1,), (1,)), ((), ())), preferred_element_type=F32)


def _dot_tn(a, b):
    return lax.dot_general(a.astype(MXU), b.astype(MXU), (((0,), (0,)), ((), ())), preferred_element_type=F32)


def _dot_exact(tri, x):
    t = tri.astype(MXU)
    hi = x.astype(MXU)
    r1 = x - hi.astype(F32)
    mid = r1.astype(MXU)
    lo = (r1 - mid.astype(F32)).astype(MXU)
    dn = (((1,), (0,)), ((), ()))
    return (lax.dot_general(t, hi, dn, preferred_element_type=F32) + lax.dot_general(t, mid, dn, preferred_element_type=F32)
            + lax.dot_general(t, lo, dn, preferred_element_type=F32))


def _row_sum(v):
    return jnp.sum(v, axis=0, keepdims=True)


def _handshake(partners):
    barrier = pltpu.get_barrier_semaphore()
    for p in partners:
        pl.semaphore_signal(barrier, inc=1, device_id=p, device_id_type=MESH)
    pl.semaphore_wait(barrier, len(partners))


def _gather_body(n, per_array_sems, handshake, layouts):
    def body(*refs):
        ins, outs = refs[:n], refs[n:2 * n]
        send_sems, recv_sems, local_sems = refs[2 * n:]
        x, y, c = lax.axis_index("x"), lax.axis_index("y"), lax.axis_index("c")
        me, sibling = (x, y, c), (x, y, 1 - c)
        chips = [(1 - x, y), (x, 1 - y), (1 - x, 1 - y)]
        if handshake:
            _handshake([sibling] + [(*chip, c) for chip in chips])

        def sem(sems, a, k):
            return sems.at[a, k] if per_array_sems else sems.at[k]

        def slot(a, p):
            block = 4 * p[0] + 2 * p[1] + p[2]
            r, c_ = ins[a].shape
            if layouts[a] == "rows":
                return outs[a].at[pl.ds(block * r, r), :]
            if layouts[a] == "cols":
                return outs[a].at[:, pl.ds(block * c_, c_)]
            return outs[a].at[block]

        def copy(a, k, block, to, src=None):
            return pltpu.make_async_remote_copy(
                src_ref=slot(a, block) if src is None else src, dst_ref=slot(a, block),
                send_sem=sem(send_sems, a, k), recv_sem=sem(recv_sems, a, k), device_id=to, device_id_type=MESH)

        mine = [pltpu.make_async_copy(ins[a], slot(a, me), local_sems.at[a if per_array_sems else 0]) for a in range(n)]
        for cp in mine:
            cp.start()
        first = []
        for a in range(n):
            first.append(copy(a, 0, me, sibling, src=ins[a]))
            first += [copy(a, 1 + j, me, (*chip, c), src=ins[a]) for j, chip in enumerate(chips)]
        for cp in first:
            cp.start()
        passed = []
        for j, chip in enumerate(chips):
            for a in range(n):
                copy(a, 1 + j, (*chip, c), me).wait_recv()
            for a in range(n):
                cp = copy(a, 4 + j, (*chip, c), sibling)
                cp.start()
                passed.append(cp)
        for a in range(n):
            copy(a, 0, sibling, me).wait_recv()
        for j, chip in enumerate(chips):
            for a in range(n):
                copy(a, 4 + j, (*chip, 1 - c), me).wait_recv()
        for cp in first + passed:
            cp.wait_send()
        for cp in mine:
            cp.wait()

    return body


def _sibling_send_body(n, per_array_sems, handshake):
    def body(*refs):
        ins, outs = refs[:n], refs[n:2 * n]
        send_sems, recv_sems, _ = refs[2 * n:]
        x, y, c = lax.axis_index("x"), lax.axis_index("y"), lax.axis_index("c")
        sibling = (x, y, 1 - c)
        if handshake:
            _handshake([sibling])

        def sem(sems, a):
            return sems.at[a, 0] if per_array_sems else sems.at[0]

        copies = [pltpu.make_async_remote_copy(
            src_ref=ins[a].at[2 * q + 1 - c], dst_ref=outs[a].at[q], send_sem=sem(send_sems, a), recv_sem=sem(recv_sems, a),
            device_id=sibling, device_id_type=MESH) for a in range(n) for q in range(4)]
        for cp in copies:
            cp.start()
        for cp in copies:
            cp.wait_recv()
        for cp in copies:
            cp.wait_send()

    return body


def _chip_exchange_body(n, per_array_sems, handshake):
    def body(*refs):
        ins, outs = refs[:n], refs[n:2 * n]
        send_sems, recv_sems, local_sems = refs[2 * n:]
        x, y, c = lax.axis_index("x"), lax.axis_index("y"), lax.axis_index("c")
        my_chip = 2 * x + y
        chips = [(1 - x, y), (x, 1 - y), (1 - x, 1 - y)]
        if handshake:
            _handshake([(*chip, c) for chip in chips])

        def sem(sems, a, k):
            return sems.at[a, k] if per_array_sems else sems.at[k]

        def copy(a, k, landing):
            px, py = chips[k]
            return pltpu.make_async_remote_copy(
                src_ref=ins[a].at[2 * px + py], dst_ref=outs[a].at[landing], send_sem=sem(send_sems, a, k),
                recv_sem=sem(recv_sems, a, k), device_id=(px, py, c), device_id_type=MESH)

        mine = [pltpu.make_async_copy(ins[a].at[my_chip], outs[a].at[my_chip], local_sems.at[a if per_array_sems else 0])
                for a in range(n)]
        for cp in mine:
            cp.start()
        sent = [copy(a, k, my_chip) for a in range(n) for k in range(3)]
        for cp in sent:
            cp.start()
        for a in range(n):
            for k, (px, py) in enumerate(chips):
                copy(a, k, 2 * px + py).wait_recv()
        for cp in sent:
            cp.wait_send()
        for cp in mine:
            cp.wait()

    return body


def _all_gather(arrs, name):
    n = len(arrs)
    return pl.pallas_call(
        _gather_body(n, True, False, ["stack"] * n), name=name, in_specs=[ANY] * n, out_specs=[ANY] * n,
        out_shape=[jax.ShapeDtypeStruct((N_DEV,) + a.shape, a.dtype) for a in arrs],
        scratch_shapes=[pltpu.SemaphoreType.DMA((n, 7)), pltpu.SemaphoreType.DMA((n, 7)), pltpu.SemaphoreType.DMA((n,))],
    )(*arrs)


def _on_sequencer(body, arrs, out_type, name, collective_id):
    return pl.kernel(
        body, name=name, out_type=out_type, mesh=plsc.ScalarSubcoreMesh(axis_name="sequencer", num_cores=1),
        scratch_types=[pltpu.SemaphoreType.DMA((RING_SEMS,)), pltpu.SemaphoreType.DMA((RING_SEMS,)),
                       pltpu.SemaphoreType.DMA((1,))],
        compiler_params=pltpu.CompilerParams(collective_id=collective_id))(*arrs)


RING_SEMS = 8


def _ring_gather_body(n, per_array_sems, handshake, layouts):
    def body(*refs):
        ins, outs = refs[:n], refs[n:2 * n]
        send_sems, recv_sems, local_sems = refs[2 * n:]
        x, y, c = lax.axis_index("x"), lax.axis_index("y"), lax.axis_index("c")
        me, sibling = (x, y, c), (x, y, 1 - c)
        over_x, over_y, diagonal = (1 - x, y, c), (x, 1 - y, c), (1 - x, 1 - y, c)
        if handshake:
            _handshake([sibling, over_x, over_y])

        def sem(sems, a, k):
            return sems.at[a, k] if per_array_sems else sems.at[k]

        def slot(a, p, half=None):
            block = 4 * p[0] + 2 * p[1] + p[2]
            r, c_ = ins[a].shape
            first, count = (0, r) if half is None else (half * (r // 2), r // 2)
            if layouts[a] == "rows":
                return outs[a].at[pl.ds(block * r + first, count), :]
            if layouts[a] == "cols":
                return outs[a].at[pl.ds(first, count), pl.ds(block * c_, c_)]
            return outs[a].at[block, pl.ds(first, count), :]

        def copy(a, k, block, to, half=None, src=None):
            return pltpu.make_async_remote_copy(
                src_ref=slot(a, block, half) if src is None else src, dst_ref=slot(a, block, half),
                send_sem=sem(send_sems, a, k), recv_sem=sem(recv_sems, a, k), device_id=to, device_id_type=MESH)

        def start(copies):
            for cp in copies:
                cp.start()
            return copies

        def arrived(k, block, half=None):
            for a in range(n):
                copy(a, k, block, me, half).wait_recv()

        arrays = range(n)
        mine = start([pltpu.make_async_copy(ins[a], slot(a, me), local_sems.at[a if per_array_sems else 0]) for a in arrays])
        sent = start([copy(a, k, me, to, src=ins[a]) for a in arrays for k, to in ((0, sibling), (1, over_x), (2, over_y))])
        arrived(1, over_x)
        sent += start([copy(a, 3, over_x, over_y, half=0) for a in arrays] + [copy(a, 5, over_x, sibling) for a in arrays])
        arrived(2, over_y)
        sent += start([copy(a, 4, over_y, over_x, half=1) for a in arrays] + [copy(a, 6, over_y, sibling) for a in arrays])
        arrived(3, diagonal, half=0)
        arrived(4, diagonal, half=1)
        sent += start([copy(a, 7, diagonal, sibling) for a in arrays])
        arrived(0, sibling)
        for k, chip in ((5, over_x), (6, over_y), (7, diagonal)):
            arrived(k, (chip[0], chip[1], 1 - c))
        for cp in sent:
            cp.wait_send()
        for cp in mine:
            cp.wait()

    return body


def _all_gather_async(arrs, name, collective_id, layouts):
    shape = dict(stack=lambda r, c: (N_DEV, r, c), rows=lambda r, c: (N_DEV * r, c), cols=lambda r, c: (r, N_DEV * c))
    out_type = [jax.ShapeDtypeStruct(shape[lay](*a.shape), a.dtype) for a, lay in zip(arrs, layouts)]
    return _on_sequencer(_ring_gather_body(len(arrs), False, True, layouts), arrs, out_type, name, collective_id)


def _sibling_send_async(arrs, name, collective_id):
    out_type = [jax.ShapeDtypeStruct((N_DEV // 2,) + a.shape[1:], a.dtype) for a in arrs]
    return _on_sequencer(_sibling_send_body(len(arrs), False, True), arrs, out_type, name, collective_id)


def _chip_exchange_async(arrs, name, collective_id):
    out_type = [jax.ShapeDtypeStruct(a.shape, a.dtype) for a in arrs]
    return _on_sequencer(_chip_exchange_body(len(arrs), False, True), arrs, out_type, name, collective_id)


def _pair_sum(name, parity, mine, theirs):
    n = len(mine)

    def body(par_ref, *refs):
        for a in range(n):
            refs[2 * n + a][...] = (refs[a][...].astype(F32) + refs[n + a][...].astype(F32)).astype(refs[2 * n + a].dtype)

    def block(a):
        return (None,) + a.shape[1:]

    grid_spec = pltpu.PrefetchScalarGridSpec(
        num_scalar_prefetch=1, grid=(N_DEV // 2,),
        in_specs=[pl.BlockSpec(block(a), lambda q, par: (2 * q + par[0], 0, 0)) for a in mine]
        + [pl.BlockSpec(block(a), lambda q, par: (q, 0, 0)) for a in theirs],
        out_specs=[pl.BlockSpec(block(a), lambda q, par: (q, 0, 0)) for a in theirs])
    return pl.pallas_call(body, name=name, grid_spec=grid_spec, out_shape=[jax.ShapeDtypeStruct(a.shape, a.dtype) for a in theirs],
                          compiler_params=_params())(parity, *mine, *theirs)


NN = (((1,), (0,)), ((), ()))
NT = (((1,), (1,)), ((), ()))
TN = (((0,), (0,)), ((), ()))


def _matmul(name, a, b, dims, grid, a_spec, b_spec, outs, epilogue, extras=(), prologue=None):
    n_in = 2 + len(extras)
    n_out = len(outs)

    def body(*refs):
        a_ref, b_ref = refs[0], refs[1]
        ex, out_refs = refs[2:n_in], refs[n_in:n_in + n_out]
        a_tile = a_ref[...] if prologue is None else prologue(a_ref[...])
        epilogue(lax.dot_general(a_tile, b_ref[...], dims, preferred_element_type=F32), ex, out_refs)

    return pl.pallas_call(
        body, name=name, grid=grid, in_specs=[a_spec, b_spec] + [s for _, s in extras], out_specs=[s for _, _, s in outs],
        out_shape=[jax.ShapeDtypeStruct(sh, dt) for sh, dt, _ in outs], compiler_params=_params(),
    )(a, b, *[e for e, _ in extras])


def _square(tile):
    return tile * tile


def _ep_store(acc, ex, outs):
    outs[0][...] = acc.astype(outs[0].dtype)


def _ep_residual(acc, ex, outs):
    outs[0][...] = acc
    outs[1][...] = ex[0][...] + ex[1][0:1, :] * acc


def _mm_tokens(name, a, w, transposed, out_dtypes, epilogue=_ep_store, extras=(), prologue=None, rows=MM_TOKENS):
    m, n = a.shape[0], w.shape[0 if transposed else 1]
    tm = _tile(m, rows)
    rows_spec = lambda width: pl.BlockSpec((tm, width), lambda i, j, kk: (i, 0))
    whole = lambda arr: pl.BlockSpec(arr.shape, lambda i, j, kk: (0, 0))
    return _matmul(name, a, w, NT if transposed else NN, (m // tm, 1, 1), rows_spec(a.shape[1]), whole(w),
                   [((m, n), dt, rows_spec(n)) for dt in out_dtypes], epilogue,
                   extras=[(e, rows_spec(n) if e.shape[0] == m else whole(e)) for e in extras], prologue=prologue)


def _resident(w):
    return pl.BlockSpec(w.shape, lambda i: (0, 0), pipeline_mode=pl.Buffered(1))


def _mlp_forward(name, h, w1, w2, x, gate):
    m, d = x.shape
    f = w1.shape[1]
    tm = _tile(m, MLP_TOKENS)
    rows = lambda width: pl.BlockSpec((tm, width), lambda i: (i, 0))

    def body(h_ref, w1_ref, w2_ref, x_ref, g_ref, r_ref, ff_ref, out_ref):
        r = jnp.maximum(lax.dot_general(h_ref[...], w1_ref[...], NN, preferred_element_type=F32), 0.0).astype(r_ref.dtype)
        r_ref[...] = r
        ff = lax.dot_general(r * r, w2_ref[...], NN, preferred_element_type=F32)
        ff_ref[...] = ff
        out_ref[...] = x_ref[...] + g_ref[0:1, :] * ff

    return pl.pallas_call(
        body, name=name, grid=(m // tm,), in_specs=[rows(d), _resident(w1), _resident(w2), rows(d), _vec_spec()],
        out_specs=[rows(f), rows(d), rows(d)],
        out_shape=[jax.ShapeDtypeStruct((m, f), MXU), jax.ShapeDtypeStruct((m, d), F32), jax.ShapeDtypeStruct((m, d), F32)],
        compiler_params=_params())(h, w1, w2, x, gate)


def _mlp_backward(name, dff, r, w1, w2):
    m, d = dff.shape
    f = r.shape[1]
    tm = _tile(m, MLP_TOKENS)
    rows = lambda width: pl.BlockSpec((tm, width), lambda i: (i, 0))

    def body(dff_ref, r_ref, w1_ref, w2_ref, da_ref, dh_ref):
        ds = lax.dot_general(dff_ref[...], w2_ref[...], NT, preferred_element_type=F32)
        da = (ds * (2.0 * r_ref[...].astype(F32))).astype(da_ref.dtype)
        da_ref[...] = da
        dh_ref[...] = lax.dot_general(da, w1_ref[...], NT, preferred_element_type=F32)

    return pl.pallas_call(
        body, name=name, grid=(m // tm,), in_specs=[rows(d), rows(f), _resident(w1), _resident(w2)],
        out_specs=[rows(f), rows(d)],
        out_shape=[jax.ShapeDtypeStruct((m, f), MXU), jax.ShapeDtypeStruct((m, d), F32)],
        compiler_params=_params())(dff, r, w1, w2)


def _mm_grad(name, a, b, shard_cols, prologue=None):
    t, m = a.shape
    n = b.shape[1]
    tm = _tile(m, MM_TILE)
    if shard_cols:
        tn = n // N_DEV
        out = ((N_DEV, m, tn), MXU, pl.BlockSpec((None, tm, tn), lambda i, j, kk: (j, i, 0)))
    else:
        tn = _tile(n, MM_TILE)
        out = ((m, n), MXU, pl.BlockSpec((tm, tn), lambda i, j, kk: (i, j)))
    return _matmul(name, a, b, TN, (m // tm, n // tn, 1), pl.BlockSpec((t, tm), lambda i, j, kk: (0, i)),
                   pl.BlockSpec((t, tn), lambda i, j, kk: (0, j)), [out], _ep_store, prologue=prologue)[0]


def _rows_spec(tt, width=D):
    return pl.BlockSpec((tt, width), lambda i: (i, 0))


def _vec_spec(rows=SUBLANES, width=D):
    return pl.BlockSpec((rows, width), lambda i: (0, 0))


def _norm_mod(name, x, vec):
    t = x.shape[0]
    tt = _tile(t, ROW_TILE)

    def body(x_ref, v_ref, h_ref):
        xv = x_ref[...]
        r = lax.rsqrt(jnp.mean(xv * xv, axis=-1, keepdims=True) + NORM_EPS)
        h = (xv * r) * v_ref[0:1, :]
        h_ref[...] = (h * (1.0 + v_ref[1:2, :]) + v_ref[2:3, :]).astype(h_ref.dtype)

    return pl.pallas_call(body, name=name, grid=(t // tt,), in_specs=[_rows_spec(tt), _vec_spec()], out_specs=_rows_spec(tt),
                          out_shape=jax.ShapeDtypeStruct((t, D), MXU))(x, vec)


GATE_ROW = 3


def _branch_grad(dx, y_ref, g_ref, dy_ref, acc_ref):
    dy_ref[...] = (dx * g_ref[0:1, :]).astype(dy_ref.dtype)
    acc_ref[GATE_ROW:GATE_ROW + 1, :] += _row_sum(dx * y_ref[...])


def _norm_mod_grad(name, dh, x, vec, dres, y=None, gate=None):
    t = x.shape[0]
    tt = _tile(t, ROW_TILE)
    nt = t // tt
    branch = y is not None

    def body(dh_ref, x_ref, v_ref, dres_ref, *rest):
        dx_ref, acc_ref = rest[2 * branch], rest[2 * branch + 1]
        i = pl.program_id(0)

        @pl.when(i == 0)
        def _():
            acc_ref[...] = jnp.zeros_like(acc_ref)

        xv, dhv = x_ref[...], dh_ref[...]
        r = lax.rsqrt(jnp.mean(xv * xv, axis=-1, keepdims=True) + NORM_EPS)
        xn = xv * r
        w = v_ref[0:1, :] * (1.0 + v_ref[1:2, :])
        acc_ref[0:1, :] += _row_sum(dhv * xn)
        acc_ref[2:3, :] += _row_sum(dhv)
        dxn = dhv * w
        dx = dres_ref[...] + r * (dxn - xn * jnp.mean(dxn * xn, axis=-1, keepdims=True))
        dx_ref[...] = dx
        if branch:
            _branch_grad(dx, rest[0], rest[1], rest[4], acc_ref)

        @pl.when(i == nt - 1)
        def _():
            dw = acc_ref[0:1, :]
            acc_ref[1:2, :] = dw * v_ref[0:1, :]
            acc_ref[0:1, :] = dw * (1.0 + v_ref[1:2, :])

    extra_in = [_rows_spec(tt), _vec_spec()] if branch else []
    extra_out = [_rows_spec(tt)] if branch else []
    extra_shape = [jax.ShapeDtypeStruct((t, D), MXU)] if branch else []
    return pl.pallas_call(
        body, name=name, grid=(nt,), in_specs=[_rows_spec(tt), _rows_spec(tt), _vec_spec(), _rows_spec(tt)] + extra_in,
        out_specs=[_rows_spec(tt), _vec_spec()] + extra_out,
        out_shape=[jax.ShapeDtypeStruct((t, D), F32), jax.ShapeDtypeStruct((SUBLANES, D), F32)] + extra_shape,
    )(dh, x, vec, dres, *([y, gate] if branch else []))


def _loss_head(name, x, target, vec, y, gate):
    t = x.shape[0]
    tt = _tile(t, ROW_TILE)

    def body(x_ref, t_ref, v_ref, y_ref, g_ref, dx_ref, acc_ref, dy_ref):
        @pl.when(pl.program_id(0) == 0)
        def _():
            acc_ref[...] = jnp.zeros_like(acc_ref)

        xv = x_ref[...]
        r = lax.rsqrt(jnp.mean(xv * xv, axis=-1, keepdims=True) + NORM_EPS)
        xn = xv * r
        gain = v_ref[0:1, :]
        err = xn * gain - t_ref[...]
        acc_ref[1:2, :] += _row_sum(err * err) * (0.5 / D)
        dout = err * (1.0 / D)
        acc_ref[0:1, :] += _row_sum(dout * xn)
        dxn = dout * gain
        dx = r * (dxn - xn * jnp.mean(dxn * xn, axis=-1, keepdims=True))
        dx_ref[...] = dx
        _branch_grad(dx, y_ref, g_ref, dy_ref, acc_ref)

    return pl.pallas_call(
        body, name=name, grid=(t // tt,), in_specs=[_rows_spec(tt), _rows_spec(tt), _vec_spec(), _rows_spec(tt), _vec_spec()],
        out_specs=[_rows_spec(tt), _vec_spec(), _rows_spec(tt)],
        out_shape=[jax.ShapeDtypeStruct((t, D), F32), jax.ShapeDtypeStruct((SUBLANES, D), F32),
                   jax.ShapeDtypeStruct((t, D), MXU)])(x, target, vec, y, gate)


def _shift_down(x, halo, k):
    y = pltpu.roll(x, k, 0)
    top = jnp.where(lax.broadcasted_iota(jnp.int32, halo.shape, 0) < k, pltpu.roll(halo, k, 0), y[0:SUBLANES, :])
    return jnp.concatenate([top, y[SUBLANES:, :]], axis=0)


def _shift_up(x, halo, k):
    n = x.shape[0]
    y = pltpu.roll(x, n - k, 0)
    bottom = jnp.where(lax.broadcasted_iota(jnp.int32, halo.shape, 0) >= SUBLANES - k, pltpu.roll(halo, SUBLANES - k, 0),
                       y[n - SUBLANES:, :])
    return jnp.concatenate([y[:n - SUBLANES, :], bottom], axis=0)


def _rg_gates(xb, halo, cw_ref, vec_ref, wa_ref, wx_ref, at_start):
    shifted = [xb] + [_shift_down(xb, halo, k) for k in range(1, CONV_WIDTH)]
    xc = vec_ref[0:1, :] + shifted[0] * cw_ref[CONV_WIDTH - 1:CONV_WIDTH, :]
    for k in range(1, CONV_WIDTH):
        xc = xc + shifted[k] * cw_ref[CONV_WIDTH - 1 - k:CONV_WIDTH - k, :]
    heads = [slice(h * HEAD_DIM, (h + 1) * HEAD_DIM) for h in range(HEADS)]
    pa = jnp.concatenate([_dot(xc[:, s], wa_ref[h]) for h, s in enumerate(heads)], axis=1) + vec_ref[1:2, :]
    px = jnp.concatenate([_dot(xc[:, s], wx_ref[h]) for h, s in enumerate(heads)], axis=1) + vec_ref[2:3, :]
    ra, ia = _sigmoid(pa), _sigmoid(px)
    nl = -vec_ref[3:4, :]
    sp = jnp.maximum(nl, 0.0) + jnp.log(1.0 + jnp.exp(-jnp.abs(nl)))
    log_a = (-LRU_C) * ra * sp
    a = jnp.exp(log_a)
    th = jnp.tanh(log_a)
    is_t0 = jnp.logical_and(lax.broadcasted_iota(jnp.int32, xb.shape, 0) == 0, at_start)
    mult = jnp.where(is_t0, 1.0, jnp.sqrt(-2.0 * th / (1.0 - th)))
    return dict(shifted=shifted, xc=xc, ra=ra, ia=ia, sp=sp, a=a, mult=mult, is_t0=is_t0, heads=heads)


def _rg_specs(tt, nt, order):
    blk = tt // SUBLANES
    return dict(
        x=pl.BlockSpec((tt, D), lambda i: (order(i), 0)), y=pl.BlockSpec((tt, D), lambda i: (order(i), 1)),
        halo=pl.BlockSpec((SUBLANES, D), lambda i: (jnp.maximum(order(i) * blk - 1, 0), 0)),
        cw=_vec_spec(CONV_WIDTH), vec=_vec_spec(), w=pl.BlockSpec((HEADS, HEAD_DIM, HEAD_DIM), lambda i: (0, 0, 0)))


def _rg_forward(name, z, cw, vec, wa, wx):
    t = z.shape[0]
    tt = _tile(t, RG_ROWS)
    nt = t // tt
    sp = _rg_specs(tt, nt, lambda i: i)

    def body(zx_ref, zy_ref, halo_ref, cw_ref, vec_ref, wa_ref, wx_ref, p_ref, h_ref, a_s, u_s, carry):
        i = pl.program_id(0)

        @pl.when(i == 0)
        def _():
            carry[...] = jnp.zeros_like(carry)

        halo = jnp.where(i > 0, halo_ref[...], 0.0)
        g = _rg_gates(zx_ref[...], halo, cw_ref, vec_ref, wa_ref, wx_ref, i == 0)
        a_s[...] = g["a"]
        u_s[...] = g["mult"] * (g["ia"] * g["xc"])

        def group(gi, h):
            rows = pl.ds(pl.multiple_of(gi * SUBLANES, SUBLANES), SUBLANES)
            a8, u8 = a_s[rows, :], u_s[rows, :]
            out = []
            for j in range(SUBLANES):
                h = a8[j:j + 1, :] * h + u8[j:j + 1, :]
                out.append(h)
            h_ref[rows, :] = jnp.concatenate(out, axis=0)
            return h

        carry[0:1, :] = lax.fori_loop(0, tt // SUBLANES, group, carry[0:1, :])
        p_ref[...] = (h_ref[...] * _gelu(zy_ref[...])).astype(p_ref.dtype)

    return pl.pallas_call(
        body, name=name, grid=(nt,), in_specs=[sp["x"], sp["y"], sp["halo"], sp["cw"], sp["vec"], sp["w"], sp["w"]],
        out_specs=[_rows_spec(tt), _rows_spec(tt)],
        out_shape=[jax.ShapeDtypeStruct((t, D), MXU), jax.ShapeDtypeStruct((t, D), F32)],
        scratch_shapes=[pltpu.VMEM((tt, D), F32), pltpu.VMEM((tt, D), F32), pltpu.VMEM((SUBLANES, D), F32)],
        compiler_params=_params())(z, z, z, cw, vec, wa, wx)


def _rg_backward(name, dp, z, h, cw, vec, wa, wx):
    t = z.shape[0]
    tt = _tile(t, RG_ROWS)
    nt = t // tt
    rev = lambda i: nt - 1 - i
    sp = _rg_specs(tt, nt, rev)
    rows_rev = pl.BlockSpec((tt, D), lambda i: (rev(i), 0))

    def body(dp_ref, zx_ref, zy_ref, halo_ref, h_ref, hhalo_ref, cw_ref, vec_ref, wa_ref, wx_ref,
             dz_ref, dvec_ref, dcw_ref, dwa_ref, dwx_ref, a_s, d_s, carry, nxt):
        i = pl.program_id(0)
        j = rev(i)

        @pl.when(i == 0)
        def _():
            carry[...] = jnp.zeros_like(carry)
            nxt[...] = jnp.zeros_like(nxt)
            dvec_ref[...] = jnp.zeros_like(dvec_ref)
            dcw_ref[...] = jnp.zeros_like(dcw_ref)
            dwa_ref[...] = jnp.zeros_like(dwa_ref)
            dwx_ref[...] = jnp.zeros_like(dwx_ref)

        halo = jnp.where(j > 0, halo_ref[...], 0.0)
        g = _rg_gates(zx_ref[...], halo, cw_ref, vec_ref, wa_ref, wx_ref, j == 0)
        xc, ra, ia, a, mult = g["xc"], g["ra"], g["ia"], g["a"], g["mult"]
        hv, zy, dpv = h_ref[...], zy_ref[...], dp_ref[...]
        gelu, gelu_grad = _gelu_and_grad(zy)
        dyb = dpv * hv * gelu_grad
        a_s[...] = a
        d_s[...] = dpv * gelu

        def group(gi, c):
            rows = pl.ds(pl.multiple_of((tt // SUBLANES - 1 - gi) * SUBLANES, SUBLANES), SUBLANES)
            a8, d8 = a_s[rows, :], d_s[rows, :]
            out = [None] * SUBLANES
            for r in reversed(range(SUBLANES)):
                dht = d8[r:r + 1, :] + c
                out[r] = dht
                c = a8[r:r + 1, :] * dht
            d_s[rows, :] = jnp.concatenate(out, axis=0)
            return c

        carry[0:1, :] = lax.fori_loop(0, tt // SUBLANES, group, carry[0:1, :])
        dht = d_s[...]
        hprev = _shift_down(hv, jnp.where(j > 0, hhalo_ref[...], 0.0), 1)
        ixc = ia * xc
        dlog_a = dht * hprev * a + jnp.where(g["is_t0"], 0.0, dht * ixc * (-(a * a) / mult))
        dia = dht * mult * xc
        dxc = dht * mult * ia
        dra = dlog_a * ((-LRU_C) * g["sp"])
        nl = -vec_ref[3:4, :]
        dvec_ref[3:4, :] += _row_sum(dlog_a * ((-LRU_C) * ra)) * (-_sigmoid(nl))
        dpa = dra * ra * (1.0 - ra)
        dpx = dia * ia * (1.0 - ia)
        dvec_ref[1:2, :] += _row_sum(dpa)
        dvec_ref[2:3, :] += _row_sum(dpx)
        back = []
        for hd, s in enumerate(g["heads"]):
            dwa_ref[hd] += _dot_tn(xc[:, s], dpa[:, s])
            dwx_ref[hd] += _dot_tn(xc[:, s], dpx[:, s])
            back.append(_dot_nt(dpa[:, s], wa_ref[hd]) + _dot_nt(dpx[:, s], wx_ref[hd]))
        dxc = dxc + jnp.concatenate(back, axis=1)
        dvec_ref[0:1, :] += _row_sum(dxc)
        dxb = dxc * cw_ref[CONV_WIDTH - 1:CONV_WIDTH, :]
        for k in range(CONV_WIDTH):
            row = CONV_WIDTH - 1 - k
            dcw_ref[row:row + 1, :] += _row_sum(dxc * g["shifted"][k])
            if k:
                dxb = dxb + _shift_up(dxc, nxt[...], k) * cw_ref[row:row + 1, :]
        nxt[...] = dxc[0:SUBLANES, :]
        dz_ref[:, 0:D] = dxb.astype(dz_ref.dtype)
        dz_ref[:, D:2 * D] = dyb.astype(dz_ref.dtype)

    hhalo = pl.BlockSpec((SUBLANES, D), lambda i: (jnp.maximum(rev(i) * (tt // SUBLANES) - 1, 0), 0))
    wacc = pl.BlockSpec((HEADS, HEAD_DIM, HEAD_DIM), lambda i: (0, 0, 0))
    return pl.pallas_call(
        body, name=name, grid=(nt,),
        in_specs=[rows_rev, sp["x"], sp["y"], sp["halo"], rows_rev, hhalo, sp["cw"], sp["vec"], sp["w"], sp["w"]],
        out_specs=[pl.BlockSpec((tt, 2 * D), lambda i: (rev(i), 0)), _vec_spec(), _vec_spec(), wacc, wacc],
        out_shape=[jax.ShapeDtypeStruct((t, 2 * D), MXU), jax.ShapeDtypeStruct((SUBLANES, D), F32),
                   jax.ShapeDtypeStruct((SUBLANES, D), F32), jax.ShapeDtypeStruct((HEADS, HEAD_DIM, HEAD_DIM), F32),
                   jax.ShapeDtypeStruct((HEADS, HEAD_DIM, HEAD_DIM), F32)],
        scratch_shapes=[pltpu.VMEM((tt, D), F32), pltpu.VMEM((tt, D), F32), pltpu.VMEM((SUBLANES, D), F32),
                        pltpu.VMEM((SUBLANES, D), F32)],
        compiler_params=_params())(dp, z, z, z, h, h, cw, vec, wa, wx)


def _hg_chunk(zq, zf, zi, lb):
    c = HG_CHUNK
    q, q_grad = _silu_and_grad(zq)
    sg = _sigmoid(zf)
    fg = lb + (1.0 - lb) * sg
    k = 1.0 - fg
    row, col = lax.broadcasted_iota(jnp.int32, (c, c), 0), lax.broadcasted_iota(jnp.int32, (c, c), 1)
    tri, tri_t = (row >= col).astype(F32), (row <= col).astype(F32)
    b = _dot_exact(tri, jnp.log(fg))
    mid, last = b[c // 2 - 1:c // 2, :], b[c - 1:c, :]
    eq = jnp.exp(jnp.minimum(b - mid, EXP_CLAMP))
    ek = jnp.exp(jnp.minimum(mid - b, EXP_CLAMP))
    eb = jnp.exp(b)
    ed = jnp.exp(last - b)
    return dict(q=q, q_grad=q_grad, sg=sg, fg=fg, k=k, v=zi, eq=eq, ek=ek, eb=eb, ed=ed, elast=jnp.exp(last), tri=tri, tri_t=tri_t,
                qe=q * eq, ke=k * ek, qb=q * eb, kd=k * ed)


def _hg_specs(tt, order):
    return [pl.BlockSpec((tt, D), lambda i, col=col: (order(i), col)) for col in range(4)]


def _hg_forward(name, z, vec):
    t = z.shape[0]
    tt = _tile(t, ROW_TILE)
    nt, nc = t // tt, tt // HG_CHUNK
    heads = [slice(h * HEAD_DIM, (h + 1) * HEAD_DIM) for h in range(HEADS)]

    def body(zq_ref, zf_ref, zi_ref, zg_ref, vec_ref, p_ref, o_ref, ss_ref, state):
        @pl.when(pl.program_id(0) == 0)
        def _():
            state[...] = jnp.zeros_like(state)

        lb, gain = vec_ref[0:1, :], vec_ref[1:2, :]

        def chunk(ci, carry):
            rows = pl.ds(pl.multiple_of(ci * HG_CHUNK, HG_CHUNK), HG_CHUNK)
            g = _hg_chunk(zq_ref[rows, :], zf_ref[rows, :], zi_ref[rows, :], lb)
            zg = zg_ref[rows, :]
            causal = g["tri"] > 0.0
            qe_m, ke_m, qb_m, kd_m, v_m = (g[n].astype(MXU) for n in ("qe", "ke", "qb", "kd", "v"))
            atts = [_dot_nt(qe_m[:, s], ke_m[:, s]) for s in heads]
            o_parts, p_parts = [], []
            for hd, s in enumerate(heads):
                st = state[hd]
                ss_ref[ci, hd] = st
                att = jnp.where(causal, atts[hd], 0.0)
                o_parts.append(_dot(att, v_m[:, s]) + _dot_nt(qb_m[:, s], st))
                state[hd] = st * g["elast"][:, s] + _dot_tn(v_m[:, s], kd_m[:, s])
            for o, s in zip(o_parts, heads):
                r = lax.rsqrt(jnp.mean(o * o, axis=-1, keepdims=True) + GNORM_EPS)
                p_parts.append((o * r) * gain[:, s])
            o_ref[rows, :] = jnp.concatenate(o_parts, axis=1)
            p_ref[rows, :] = (jnp.concatenate(p_parts, axis=1) * _silu(zg)).astype(p_ref.dtype)
            return carry

        lax.fori_loop(0, nc, chunk, 0, unroll=True)

    return pl.pallas_call(
        body, name=name, grid=(nt,), in_specs=_hg_specs(tt, lambda i: i) + [_vec_spec()],
        out_specs=[_rows_spec(tt), _rows_spec(tt), pl.BlockSpec((nc, HEADS, HEAD_DIM, HEAD_DIM), lambda i: (i, 0, 0, 0))],
        out_shape=[jax.ShapeDtypeStruct((t, D), MXU), jax.ShapeDtypeStruct((t, D), F32),
                   jax.ShapeDtypeStruct((t // HG_CHUNK, HEADS, HEAD_DIM, HEAD_DIM), F32)],
        scratch_shapes=[pltpu.VMEM((HEADS, HEAD_DIM, HEAD_DIM), F32)], compiler_params=_params())(z, z, z, z, vec)


def _hg_backward(name, dp, z, o, ss, vec):
    t = z.shape[0]
    tt = _tile(t, ROW_TILE)
    nt, nc = t // tt, tt // HG_CHUNK
    rev = lambda i: nt - 1 - i
    heads = [slice(h * HEAD_DIM, (h + 1) * HEAD_DIM) for h in range(HEADS)]
    rows_rev = pl.BlockSpec((tt, D), lambda i: (rev(i), 0))

    def body(dp_ref, zq_ref, zf_ref, zi_ref, zg_ref, o_ref, ss_ref, vec_ref, dz_ref, acc_ref, dstate):
        @pl.when(pl.program_id(0) == 0)
        def _():
            dstate[...] = jnp.zeros_like(dstate)
            acc_ref[...] = jnp.zeros_like(acc_ref)

        lb, gain = vec_ref[0:1, :], vec_ref[1:2, :]

        def chunk(cr, carry):
            ci = nc - 1 - cr
            rows = pl.ds(pl.multiple_of(ci * HG_CHUNK, HG_CHUNK), HG_CHUNK)
            zq, zg = zq_ref[rows, :], zg_ref[rows, :]
            g = _hg_chunk(zq, zf_ref[rows, :], zi_ref[rows, :], lb)
            ov, dpv = o_ref[rows, :], dp_ref[rows, :]
            causal = g["tri"] > 0.0
            silu_g, silu_g_grad = _silu_and_grad(zg)
            don = dpv * silu_g
            dgate = dpv * silu_g_grad
            qe_m, ke_m, qb_m, kd_m, v_m = (g[n].astype(MXU) for n in ("qe", "ke", "qb", "kd", "v"))
            parts = {n: [] for n in ("dzg", "dgain", "dv", "dqe", "dke", "dqb", "dkd", "dlast")}
            dos, atts, datts = [], [], []
            for hd, s in enumerate(heads):
                oh = ov[:, s]
                r = lax.rsqrt(jnp.mean(oh * oh, axis=-1, keepdims=True) + GNORM_EPS)
                on = oh * r
                parts["dzg"].append(dgate[:, s] * (on * gain[:, s]))
                parts["dgain"].append(_row_sum(don[:, s] * on))
                dtmp = don[:, s] * gain[:, s]
                dos.append((r * (dtmp - on * jnp.mean(dtmp * on, axis=-1, keepdims=True))).astype(MXU))
            for hd, s in enumerate(heads):
                atts.append(_dot_nt(qe_m[:, s], ke_m[:, s]))
                datts.append(_dot_nt(dos[hd], v_m[:, s]))
            for hd, s in enumerate(heads):
                do, att, datt = dos[hd], jnp.where(causal, atts[hd], 0.0).astype(MXU), jnp.where(causal, datts[hd], 0.0).astype(MXU)
                st, dst = ss_ref[ci, hd], dstate[hd]
                st_m, dst_m = st.astype(MXU), dst.astype(MXU)
                qe, ke, qb, kd, v = qe_m[:, s], ke_m[:, s], qb_m[:, s], kd_m[:, s], v_m[:, s]
                parts["dv"].append(_dot_tn(att, do) + _dot_nt(kd, dst_m))
                parts["dqe"].append(_dot(datt, ke))
                parts["dke"].append(_dot_tn(datt, qe))
                parts["dqb"].append(_dot(do, st_m))
                parts["dkd"].append(_dot(v, dst_m))
                parts["dlast"].append(g["elast"][:, s] * _row_sum(dst * st))
                dstate[hd] = dst * g["elast"][:, s] + _dot_tn(do, qb)
            whole = {n: jnp.concatenate(p, axis=1) for n, p in parts.items()}
            acc_ref[1:2, :] += whole["dgain"]
            dqe, dke, dqb, dkd = whole["dqe"], whole["dke"], whole["dqb"], whole["dkd"]
            dq = dqe * g["eq"] + dqb * g["eb"]
            dk = dke * g["ek"] + dkd * g["ed"]
            dkdk = dkd * g["kd"]
            db = dqe * qe_m.astype(F32) - dke * ke_m.astype(F32) + dqb * g["qb"] - dkdk
            dlogf = _dot_exact(g["tri_t"], db) + (whole["dlast"] + _row_sum(dkdk))
            dfg = dlogf / g["fg"] - dk
            sg = g["sg"]
            acc_ref[0:1, :] += _row_sum(dfg * (1.0 - sg))
            dz_ref[rows, 0:D] = (dq * g["q_grad"]).astype(dz_ref.dtype)
            dz_ref[rows, D:2 * D] = (dfg * (1.0 - lb) * sg * (1.0 - sg)).astype(dz_ref.dtype)
            dz_ref[rows, 2 * D:3 * D] = whole["dv"].astype(dz_ref.dtype)
            dz_ref[rows, 3 * D:4 * D] = whole["dzg"].astype(dz_ref.dtype)
            return carry

        lax.fori_loop(0, nc, chunk, 0, unroll=True)

    return pl.pallas_call(
        body, name=name, grid=(nt,),
        in_specs=[rows_rev] + _hg_specs(tt, rev) + [rows_rev, pl.BlockSpec((nc, HEADS, HEAD_DIM, HEAD_DIM), lambda i: (rev(i), 0, 0, 0)),
                                                  _vec_spec()],
        out_specs=[pl.BlockSpec((tt, 4 * D), lambda i: (rev(i), 0)), _vec_spec()],
        out_shape=[jax.ShapeDtypeStruct((t, 4 * D), MXU), jax.ShapeDtypeStruct((SUBLANES, D), F32)],
        scratch_shapes=[pltpu.VMEM((HEADS, HEAD_DIM, HEAD_DIM), F32)], compiler_params=_params())(dp, z, z, z, z, o, ss, vec)


def _mod_forward(name, c_all, mod_w, lower):
    depth, _, sw = mod_w.shape

    def body(c_ref, w_ref, lo_ref, cs_ref, mod_ref, lb_ref):
        cs = _silu(c_ref[...])
        mod_ref[...] = _dot(cs, w_ref[...])

        @pl.when(pl.program_id(0) == 0)
        def _():
            cs_ref[...] = cs
            lo = lo_ref[...]
            e = jnp.exp(lo - jnp.max(lo, axis=0, keepdims=True))
            sm = e / jnp.sum(e, axis=0, keepdims=True)
            lb_ref[0:1, :] = jnp.zeros((1, D), F32)
            for l in range(1, depth):
                lb_ref[l:l + 1, :] = lb_ref[l - 1:l, :] + sm[l:l + 1, :]

    return pl.pallas_call(
        body, name=name, grid=(depth,),
        in_specs=[pl.BlockSpec((N_DEV, D), lambda l: (0, 0)), pl.BlockSpec((None, D, sw), lambda l: (l, 0, 0)),
                  pl.BlockSpec((depth, D), lambda l: (0, 0))],
        out_specs=[pl.BlockSpec((N_DEV, D), lambda l: (0, 0)), pl.BlockSpec((None, N_DEV, sw), lambda l: (l, 0, 0)),
                   pl.BlockSpec((depth, D), lambda l: (0, 0))],
        out_shape=[jax.ShapeDtypeStruct((N_DEV, D), F32), jax.ShapeDtypeStruct((depth, N_DEV, sw), F32),
                   jax.ShapeDtypeStruct((depth, D), F32)], compiler_params=_params())(c_all, mod_w, lower)


def _mod_weight_grad(name, cs_t, dmod):
    depth, pad, sw = dmod.shape

    def body(c_ref, d_ref, g_ref):
        g_ref[...] = _dot(c_ref[...], d_ref[...])

    return pl.pallas_call(
        body, name=name, grid=(depth,),
        in_specs=[pl.BlockSpec((D, pad), lambda l: (0, 0)), pl.BlockSpec((None, pad, sw), lambda l: (l, 0, 0))],
        out_specs=pl.BlockSpec((None, D, sw), lambda l: (l, 0, 0)),
        out_shape=jax.ShapeDtypeStruct((depth, D, sw), F32), compiler_params=_params())(cs_t, dmod)


def _lower_bound_grad(name, lower, dlb):
    depth = lower.shape[0]

    def body(lo_ref, d_ref, out_ref):
        lo = lo_ref[...]
        e = jnp.exp(lo - jnp.max(lo, axis=0, keepdims=True))
        sm = e / jnp.sum(e, axis=0, keepdims=True)
        out_ref[...] = jnp.zeros_like(out_ref)
        dsm = [jnp.zeros((1, D), F32)]
        for l in range(1, depth):
            tail = d_ref[l:l + 1, :]
            for m in range(l + 1, depth):
                tail = tail + d_ref[m:m + 1, :]
            dsm.append(tail)
        inner = sm[1:2, :] * dsm[1]
        for l in range(2, depth):
            inner = inner + sm[l:l + 1, :] * dsm[l]
        for l in range(depth):
            out_ref[l:l + 1, :] = sm[l:l + 1, :] * (dsm[l] - inner)
        for j in range(SUBLANES - depth):
            row = d_ref[depth + j:depth + j + 1, :]
            tot = row[:, 0:HEAD_DIM]
            for hd in range(1, HEADS):
                tot = tot + row[:, hd * HEAD_DIM:(hd + 1) * HEAD_DIM]
            out_ref[depth + j:depth + j + 1, 0:HEAD_DIM] = tot

    return pl.pallas_call(body, name=name, out_shape=jax.ShapeDtypeStruct((SUBLANES, D), F32))(lower, dlb)


def _adamw(name, parts, w, m, v, layer=None, prev=None):
    p, r, c = parts.shape
    tr = _tile(r, max(SUBLANES, ADAMW_STEP_BYTES // (4 * c * (p + 7))))
    stacked = layer is not None
    n_prev = 4 if prev is not None else 0

    def body(*refs):
        parts_ref, w_ref, m_ref, v_ref = refs[:4]
        g_ref, d_ref, m_out, v_out = refs[4 + n_prev:]
        g = parts_ref[0].astype(F32)
        for q in range(1, p):
            g = g + parts_ref[q].astype(F32)
        m2 = ADAM_B1 * m_ref[...] + (1.0 - ADAM_B1) * g
        v2 = ADAM_B2 * v_ref[...] + (1.0 - ADAM_B2) * (g * g)
        m_hat = m2 / (1.0 - ADAM_B1 ** ADAM_STEP)
        v_hat = v2 / (1.0 - ADAM_B2 ** ADAM_STEP)
        g_ref[...] = g
        d_ref[...] = -ADAM_LR * (m_hat / (jnp.sqrt(v_hat) + ADAM_EPS) + ADAM_WD * w_ref[...])
        m_out[...] = m2
        v_out[...] = v2

    if stacked:
        spec = pl.BlockSpec((None, tr, c), lambda i: (layer, i, 0))
    else:
        spec = pl.BlockSpec((tr, c), lambda i: (i, 0))
    return pl.pallas_call(
        body, name=name, grid=(r // tr,),
        in_specs=[pl.BlockSpec((p, tr, c), lambda i: (0, i, 0)), spec, spec, spec] + [ANY] * n_prev, out_specs=[spec] * 4,
        out_shape=[jax.ShapeDtypeStruct(w.shape, F32)] * 4,
        input_output_aliases={4 + q: q for q in range(n_prev)}, compiler_params=_params(),
    )(parts, w, m, v, *(prev or []))


def _hbm(a):
    return pltpu.with_memory_space_constraint(a, pltpu.HBM)


def _vec(*rows):
    rows = [r.reshape(1, D).astype(F32) for r in rows]
    packed = jnp.concatenate(rows + [jnp.zeros((SUBLANES - len(rows), D), F32)], axis=0)
    return _hbm(packed)


def _pad_rows(a, rows=SUBLANES):
    a = a.reshape(-1, a.shape[-1])
    return jnp.concatenate([a, jnp.zeros((rows - a.shape[0], a.shape[1]), a.dtype)], axis=0) if a.shape[0] < rows else a


def kernel(x, c, mod_w, mod_b, norm_mix, norm_mlp, norm_final, rg_w_in, rg_conv_w, rg_conv_b, rg_w_a, rg_b_a, rg_w_x, rg_b_x, rg_lambda, rg_w_out, hg_w_in, hg_lower_bounds, hg_gnorm, hg_w_out, mlp_w1, mlp_w2, loss_target, m_mod_w, m_mod_b, m_norm_mix, m_norm_mlp, m_norm_final, m_rg_w_in, m_rg_conv_w, m_rg_conv_b, m_rg_w_a, m_rg_b_a, m_rg_w_x, m_rg_b_x, m_rg_lambda, m_rg_w_out, m_hg_w_in, m_hg_lower_bounds, m_hg_gnorm, m_hg_w_out, m_mlp_w1, m_mlp_w2, v_mod_w, v_mod_b, v_norm_mix, v_norm_mlp, v_norm_final, v_rg_w_in, v_rg_conv_w, v_rg_conv_b, v_rg_w_a, v_rg_b_a, v_rg_w_x, v_rg_b_x, v_rg_lambda, v_rg_w_out, v_hg_w_in, v_hg_lower_bounds, v_hg_gnorm, v_hg_w_out, v_mlp_w1, v_mlp_w2):
    me = 4 * lax.axis_index("x") + 2 * lax.axis_index("y") + lax.axis_index("c")
    x0 = x[0]
    target = loss_target[0]
    n_rg, n_hg = rg_w_in.shape[0], hg_w_in.shape[0]
    sw_mod = mod_w.shape[2]

    c_all, cw_all = _all_gather([_pad_rows(c), rg_conv_w.reshape(n_rg * CONV_WIDTH, -1)], "gather_cond")
    c_all = c_all[:, 0, :]
    conv_w = cw_all.transpose(1, 0, 2).reshape(n_rg, CONV_WIDTH, D)
    cs_all, mod_part, lb_all = _mod_forward("mod_forward", c_all, mod_w, hg_lower_bounds)
    (mod_gathered,) = _all_gather([mod_part.reshape(DEPTH * N_DEV, sw_mod)], "gather_mod")

    ids = iter(range(5 * DEPTH))
    shards = []
    for layer in range(DEPTH):
        j = layer // 2
        w_in, w_out = (rg_w_in[j], rg_w_out[j]) if layer % 2 == 0 else (hg_w_in[j], hg_w_out[j])
        shards.append([w_in.astype(MXU), w_out.astype(MXU), mlp_w1[layer].astype(MXU), mlp_w2[layer].astype(MXU)])
    shards, mod_gathered = lax.optimization_barrier((shards, mod_gathered))
    weights = []
    for layer in range(DEPTH):
        g_in, g_out = _all_gather_async(shards[layer][:2], f"gather_mixer_weights_{layer}", next(ids), ["cols", "rows"])
        g_w1, g_w2 = _all_gather_async(shards[layer][2:], f"gather_mlp_weights_{layer}", next(ids), ["cols", "rows"])
        weights.append(dict(w_in=g_in, w_out=g_out, w1=g_w1, w2=g_w2))

    mod_mine = lax.dynamic_index_in_dim(mod_gathered.reshape(N_DEV, DEPTH, N_DEV, sw_mod), me, axis=2, keepdims=False)
    mod = mod_mine.transpose(1, 0, 2).reshape(DEPTH, 6, D) + mod_b.reshape(DEPTH, 6, D)

    saved = []
    xl = x0
    for layer in range(DEPTH):
        j, wt = layer // 2, weights[layer]
        is_rg = layer % 2 == 0
        s = dict(x=xl)
        s["vec_mix"] = _vec(norm_mix[layer], mod[layer, 1], mod[layer, 0])
        s["vec_mlp"] = _vec(norm_mlp[layer], mod[layer, 4], mod[layer, 3])
        s["gate_mix"], s["gate_mlp"] = _vec(mod[layer, 2]), _vec(mod[layer, 5])
        s["h"] = _hbm(_norm_mod("norm_mod", xl, s["vec_mix"]))
        if is_rg:
            (s["z"],) = _mm_tokens("rg_in", s["h"], wt["w_in"], False, (F32,))
            s["cw"] = conv_w[j]
            s["vec"] = _vec(rg_conv_b[j], rg_b_a[j], rg_b_x[j], rg_lambda[j])
            s["wa"], s["wx"] = rg_w_a[j].astype(MXU), rg_w_x[j].astype(MXU)
            s["p"], s["hr"] = _rg_forward("rg_forward", s["z"], s["cw"], s["vec"], s["wa"], s["wx"])
        else:
            (s["z"],) = _mm_tokens("hg_in", s["h"], wt["w_in"], False, (F32,))
            s["vec"] = _vec(lb_all[layer], jnp.tile(hg_gnorm[j], HEADS))
            s["p"], s["o"], s["ss"] = _hg_forward("hg_forward", s["z"], s["vec"])
        s["y"], s["x1"] = _mm_tokens("mix_out", s["p"], wt["w_out"], False, (F32, F32), _ep_residual, extras=(xl, s["gate_mix"]))
        s["p"] = _hbm(s["p"])
        s["h2"] = _hbm(_norm_mod("norm_mod", s["x1"], s["vec_mlp"]))
        s["r"], s["ff"], xl = _mlp_forward("mlp_forward", s["h2"], wt["w1"], wt["w2"], s["x1"], s["gate_mlp"])
        saved.append(s)

    dx, head, dff = _loss_head("loss_head", xl, target, _vec(norm_final), saved[-1]["ff"], saved[-1]["gate_mlp"])
    g_mlp = head[GATE_ROW:GATE_ROW + 1]

    results = {}
    big = dict(rg_w_in=(rg_w_in, m_rg_w_in, v_rg_w_in), rg_w_out=(rg_w_out, m_rg_w_out, v_rg_w_out),
               hg_w_in=(hg_w_in, m_hg_w_in, v_hg_w_in), hg_w_out=(hg_w_out, m_hg_w_out, v_hg_w_out),
               mlp_w1=(mlp_w1, m_mlp_w1, v_mlp_w1), mlp_w2=(mlp_w2, m_mlp_w2, v_mlp_w2))

    def update_layer(layer, landed):
        mixer = "rg" if layer % 2 == 0 else "hg"
        l_in, l_out, l_w1, l_w2 = landed
        for nm, parts, idx in ((f"{mixer}_w_in", l_in, layer // 2), (f"{mixer}_w_out", l_out, layer // 2),
                               ("mlp_w1", l_w1, layer), ("mlp_w2", l_w2, layer)):
            w, m, v = big[nm]
            results[nm] = _adamw(f"adamw_{nm}", parts, w, m, v, layer=idx, prev=results.get(nm))

    landed = [None] * DEPTH
    small = [None] * DEPTH
    parity = lax.axis_index("c").astype(jnp.int32).reshape(1)

    def send_chip_sums(layer, grads, from_sibling, anchor):
        sums = _pair_sum("pair_sum", parity, grads, from_sibling)
        sums, anchor = lax.optimization_barrier((sums, anchor))
        landed[layer] = _chip_exchange_async(sums, f"exchange_grads_{layer}", next(ids))
        return anchor

    pending = None
    for layer in reversed(range(DEPTH)):
        j, wt, s = layer // 2, weights[layer], saved[layer]
        is_rg = layer % 2 == 0
        dff = _hbm(dff)
        da, dh2 = _mlp_backward("mlp_backward", dff, s["r"], wt["w1"], wt["w2"])
        if pending is not None:
            da = send_chip_sums(*pending, da)
        dw2 = _mm_grad("mlp_out_grad", s["r"], dff, False, prologue=_square).reshape(N_DEV, -1, D)
        dw1 = _mm_grad("mlp_in_grad", s["h2"], da, True)
        dx1, n_mlp, dyb = _norm_mod_grad("norm_mod_grad", dh2, s["x1"], s["vec_mlp"], dx, s["y"], s["gate_mix"])
        dyb = _hbm(dyb)
        (dp,) = _mm_tokens("mix_out_t", dyb, wt["w_out"], True, (F32,))
        dw_out = _hbm(_mm_grad("mix_out_grad", s["p"], dyb, False).reshape(N_DEV, -1, D))
        if is_rg:
            dz, dvec, dcw, dwa, dwx = _rg_backward("rg_backward", dp, s["z"], s["hr"], s["cw"], s["vec"], s["wa"], s["wx"])
            gate_grads, dz = lax.optimization_barrier(
                ([dwa.reshape(-1, HEAD_DIM).astype(MXU), dwx.reshape(-1, HEAD_DIM).astype(MXU)], dz))
            gate_parts = _all_gather_async(gate_grads, f"gather_gate_grads_{j}", next(ids), ["stack", "stack"])
            mixer_small = dict(dvec=dvec, dcw=dcw, gate_parts=gate_parts)
            dw_in = _mm_grad("rg_in_grad", s["h"], dz, True)
            (dh,) = _mm_tokens("rg_in_t", dz, wt["w_in"], True, (F32,))
        else:
            dz, dvec = _hg_backward("hg_backward", dp, s["z"], s["o"], s["ss"], s["vec"])
            mixer_small = dict(dvec=dvec)
            dw_in = _mm_grad("hg_in_grad", s["h"], dz, True)
            (dh,) = _mm_tokens("hg_in_t", dz, wt["w_in"], True, (F32,))
        grads = [dw_in, dw_out, dw1, dw2]
        pending = (layer, grads, _sibling_send_async(grads, f"pair_grads_{layer}", next(ids)))
        small[layer] = dict(g_mlp=g_mlp, n_mlp=n_mlp, g_mix=n_mlp[GATE_ROW:GATE_ROW + 1], **mixer_small)
        if layer:
            below = saved[layer - 1]
            dx, n_mix, dff = _norm_mod_grad("norm_mod_grad", dh, s["x"], s["vec_mix"], dx1, below["ff"], below["gate_mlp"])
            g_mlp = n_mix[GATE_ROW:GATE_ROW + 1]
        else:
            dx, n_mix = _norm_mod_grad("norm_mod_grad_first", dh, s["x"], s["vec_mix"], dx1)
        small[layer]["n_mix"] = n_mix
        if layer + 1 < DEPTH:
            stream = (dx, dff) if layer else (dx,)
            landed[layer + 1], stream = lax.optimization_barrier((landed[layer + 1], stream))
            if "gate_parts" in small[layer + 1]:
                small[layer + 1]["gate_parts"], stream = lax.optimization_barrier((small[layer + 1]["gate_parts"], stream))
            dx, dff = stream if layer else (stream[0], None)
            update_layer(layer + 1, landed[layer + 1])

    dlb_rows = [jnp.zeros((1, D), F32) if l % 2 == 0 else small[l]["dvec"][0:1] for l in range(DEPTH)]
    dgn_rows = [small[2 * j + 1]["dvec"][1:2] for j in range(n_hg)]
    lb_grad = _lower_bound_grad("lower_bound_grad", hg_lower_bounds, _pad_rows(jnp.concatenate(dlb_rows + dgn_rows, axis=0)))
    dmod = jnp.stack([jnp.concatenate([small[l]["n_mix"][2], small[l]["n_mix"][1], small[l]["g_mix"][0],
                                       small[l]["n_mlp"][2], small[l]["n_mlp"][1], small[l]["g_mlp"][0]]) for l in range(DEPTH)])
    groups = [
        dmod.reshape(DEPTH * 6, D),
        jnp.stack([small[l]["n_mix"][0] for l in range(DEPTH)]),
        jnp.stack([small[l]["n_mlp"][0] for l in range(DEPTH)]),
        head[0:1],
        jnp.stack([small[2 * j]["dvec"][0] for j in range(n_rg)]),
        jnp.stack([small[2 * j]["dvec"][1] for j in range(n_rg)]),
        jnp.stack([small[2 * j]["dvec"][2] for j in range(n_rg)]),
        jnp.stack([small[2 * j]["dvec"][3] for j in range(n_rg)]),
        lb_grad[0:DEPTH],
        jnp.concatenate([lb_grad[DEPTH + j:DEPTH + j + 1, 0:HEAD_DIM] for j in range(n_hg)]
                        + [jnp.zeros((1, D - n_hg * HEAD_DIM), F32)], axis=1),
        jnp.concatenate([small[2 * j]["dcw"][0:CONV_WIDTH] for j in range(n_rg)], axis=0),
        head[1:2],
    ]
    params = [mod_b, norm_mix, norm_mlp, norm_final, rg_conv_b, rg_b_a, rg_b_x, rg_lambda, hg_lower_bounds, hg_gnorm]
    moms = [m_mod_b, m_norm_mix, m_norm_mlp, m_norm_final, m_rg_conv_b, m_rg_b_a, m_rg_b_x, m_rg_lambda, m_hg_lower_bounds, m_hg_gnorm]
    vars_ = [v_mod_b, v_norm_mix, v_norm_mlp, v_norm_final, v_rg_conv_b, v_rg_b_a, v_rg_b_x, v_rg_lambda, v_hg_lower_bounds, v_hg_gnorm]
    offsets, rows_of, at = [], [], 0
    for g in groups:
        offsets.append(at)
        rows_of.append(g.shape[0])
        at += -(-g.shape[0] // SUBLANES) * SUBLANES
    packed = jnp.concatenate([_pad_rows(g, -(-g.shape[0] // SUBLANES) * SUBLANES) for g in groups], axis=0)
    (small_parts,) = _all_gather([packed], "gather_small_grads")
    last_layer, last_grads, from_sibling = pending
    from_sibling, small_parts = lax.optimization_barrier((from_sibling, small_parts))
    send_chip_sums(last_layer, last_grads, from_sibling, dx)

    def pack_like(arrs):
        out = []
        for g_rows, off, a in zip(rows_of, offsets, arrs):
            flat = a.reshape(-1)
            flat = jnp.concatenate([flat, jnp.zeros((g_rows * D - flat.shape[0],), F32)])
            out.append(_pad_rows(flat.reshape(g_rows, D), -(-g_rows // SUBLANES) * SUBLANES))
        rest = packed.shape[0] - sum(o.shape[0] for o in out)
        return jnp.concatenate(out + [jnp.zeros((rest, D), F32)], axis=0)

    small_out = _adamw("adamw_small", small_parts, pack_like(params), pack_like(moms), pack_like(vars_))

    def unpack(q, idx, like):
        rows = small_out[q][offsets[idx]:offsets[idx] + rows_of[idx]]
        return rows.reshape(-1)[:like.size].reshape(like.shape)

    loss = jnp.sum(small_out[0][offsets[11]])
    names =["mod_b", "norm_mix", "norm_mlp", "norm_final", "rg_conv_b", "rg_b_a", "rg_b_x", "rg_lambda", "hg_lower_bounds", "hg_gnorm"]
    for idx, (nm, like) in enumerate(zip(names, params)):
        results[nm] = [unpack(q, idx, like) for q in range(4)]

    cw_parts = lax.dynamic_slice_in_dim(small_parts[:, offsets[10]:offsets[10] + n_rg * CONV_WIDTH, :], me * (D // N_DEV), D // N_DEV, axis=2)
    shp = rg_conv_w.shape
    results["rg_conv_w"] = [o.reshape(shp) for o in _adamw(
        "adamw_conv", cw_parts, rg_conv_w.reshape(-1, shp[-1]), m_rg_conv_w.reshape(-1, shp[-1]), v_rg_conv_w.reshape(-1, shp[-1]))]
    shp = rg_w_a.shape
    stacked = (n_rg, HEADS * HEAD_DIM, HEAD_DIM)
    for nm, which, (w, m, v) in (("rg_w_a", 0, (rg_w_a, m_rg_w_a, v_rg_w_a)), ("rg_w_x", 1, (rg_w_x, m_rg_w_x, v_rg_w_x))):
        out = None
        for j in reversed(range(n_rg)):
            out = _adamw("adamw_gate", small[2 * j]["gate_parts"][which], w.reshape(stacked), m.reshape(stacked),
                         v.reshape(stacked), layer=j, prev=out)
        results[nm] = [o.reshape(shp) for o in out]

    dmod_all = small_parts[:, 0:DEPTH * 6, :].reshape(N_DEV, DEPTH, 6 * D)
    dmod_cols = lax.dynamic_slice_in_dim(dmod_all, me * sw_mod, sw_mod, axis=2).transpose(1, 0, 2)
    pad = HEAD_DIM - N_DEV
    dmod_pad = jnp.concatenate([dmod_cols, jnp.zeros((DEPTH, pad, sw_mod), F32)], axis=1).astype(MXU)
    cs_t = jnp.concatenate([cs_all.T, jnp.zeros((D, pad), F32)], axis=1).astype(MXU)
    g_mod_w = _mod_weight_grad("mod_weight_grad", cs_t, dmod_pad)
    results["mod_w"] = [o.reshape(mod_w.shape) for o in _adamw(
        "adamw_mod", g_mod_w.reshape(1, -1, sw_mod), mod_w.reshape(-1, sw_mod), m_mod_w.reshape(-1, sw_mod), v_mod_w.reshape(-1, sw_mod))]

    update_layer(0, landed[0])

    order = ["mod_w", "mod_b", "norm_mix", "norm_mlp", "norm_final", "rg_w_in", "rg_conv_w", "rg_conv_b", "rg_w_a", "rg_b_a", "rg_w_x",
             "rg_b_x", "rg_lambda", "rg_w_out", "hg_w_in", "hg_lower_bounds", "hg_gnorm", "hg_w_out", "mlp_w1", "mlp_w2"]
    return (loss, dx[None], *[results[n][0] for n in order], *[results[n][1] for n in order],
            *[results[n][2] for n in order], *[results[n][3] for n in order])
```

```python
import collections

import jax
import jax.numpy as jnp
from jax import lax
from jax.experimental import pallas as pl
from jax.experimental.pallas import tpu as pltpu
from jax.experimental.pallas import tpu_sc as plsc

F32 = jnp.float32
MXU = jnp.bfloat16

N_DEV = 8
D = 1024
DEPTH = 4
HEADS = 8
HEAD_DIM = 128
CONV_WIDTH = 4
LRU_C = 8.0
HG_CHUNK = 64
NORM_EPS = 1e-6
GNORM_EPS = 1e-5
ADAM_LR = 0.001
ADAM_B1 = 0.9
ADAM_B2 = 0.999
ADAM_EPS = 1e-08
ADAM_WD = 0.01
ADAM_STEP = 10
GELU_C = 0.7978845608028654
GELU_K = 0.044715
EXP_CLAMP = 80.0
SUBLANES = 8
VMEM_LIMIT = 48 * 1024 * 1024
ROW_TILE = 256
RG_ROWS = 256
MM_TILE = 1024
MM_TOKENS = 512
MLP_TOKENS = 256
ADAMW_STEP_BYTES = 8 * 1024 * 1024

MESH = pl.DeviceIdType.MESH
ANY = pl.BlockSpec(memory_space=pl.ANY)


def _params():
    return pltpu.CompilerParams(vmem_limit_bytes=VMEM_LIMIT)


def _tile(n, target):
    if n <= target:
        return n
    t = target // SUBLANES * SUBLANES
    while n % t:
        t -= SUBLANES
    return t


def _sigmoid(x):
    return 1.0 / (1.0 + jnp.exp(-x))


def _silu(x):
    return x * _sigmoid(x)


def _silu_and_grad(x):
    s = _sigmoid(x)
    return x * s, s * (1.0 + x * (1.0 - s))


def _gelu(y):
    return 0.5 * y * (1.0 + jnp.tanh(GELU_C * (y + GELU_K * y * y * y)))


def _gelu_and_grad(y):
    y2 = y * y
    th = jnp.tanh(GELU_C * (y + GELU_K * y2 * y))
    half = 0.5 * (1.0 + th)
    return y * half, half + 0.5 * y * (1.0 - th * th) * GELU_C * (1.0 + 3.0 * GELU_K * y2)


def _dot(a, b):
    return lax.dot_general(a.astype(MXU), b.astype(MXU), (((1,), (0,)), ((), ())), preferred_element_type=F32)


def _dot_nt(a, b):
    return lax.dot_general(a.astype(MXU), b.astype(MXU), (((1,), (1,)), ((), ())), preferred_element_type=F32)


def _dot_tn(a, b):
    return lax.dot_general(a.astype(MXU), b.astype(MXU), (((0,), (0,)), ((), ())), preferred_element_type=F32)


def _dot_exact(tri, x):
    t = tri.astype(MXU)
    hi = x.astype(MXU)
    r1 = x - hi.astype(F32)
    mid = r1.astype(MXU)
    lo = (r1 - mid.astype(F32)).astype(MXU)
    dn = (((1,), (0,)), ((), ()))
    return (lax.dot_general(t, hi, dn, preferred_element_type=F32) + lax.dot_general(t, mid, dn, preferred_element_type=F32)
            + lax.dot_general(t, lo, dn, preferred_element_type=F32))


def _row_sum(v):
    return jnp.sum(v, axis=0, keepdims=True)


def _handshake(partners):
    barrier = pltpu.get_barrier_semaphore()
    for p in partners:
        pl.semaphore_signal(barrier, inc=1, device_id=p, device_id_type=MESH)
    pl.semaphore_wait(barrier, len(partners))


def _gather_body(n, per_array_sems, handshake, layouts):
    def body(*refs):
        ins, outs = refs[:n], refs[n:2 * n]
        send_sems, recv_sems, local_sems = refs[2 * n:]
        x, y, c = lax.axis_index("x"), lax.axis_index("y"), lax.axis_index("c")
        me, sibling = (x, y, c), (x, y, 1 - c)
        chips = [(1 - x, y), (x, 1 - y), (1 - x, 1 - y)]
        if handshake:
            _handshake([sibling] + [(*chip, c) for chip in chips])

        def sem(sems, a, k):
            return sems.at[a, k] if per_array_sems else sems.at[k]

        def slot(a, p):
            block = 4 * p[0] + 2 * p[1] + p[2]
            r, c_ = ins[a].shape
            if layouts[a] == "rows":
                return outs[a].at[pl.ds(block * r, r), :]
            if layouts[a] == "cols":
                return outs[a].at[:, pl.ds(block * c_, c_)]
            return outs[a].at[block]

        def copy(a, k, block, to, src=None):
            return pltpu.make_async_remote_copy(
                src_ref=slot(a, block) if src is None else src, dst_ref=slot(a, block),
                send_sem=sem(send_sems, a, k), recv_sem=sem(recv_sems, a, k), device_id=to, device_id_type=MESH)

        mine = [pltpu.make_async_copy(ins[a], slot(a, me), local_sems.at[a if per_array_sems else 0]) for a in range(n)]
        for cp in mine:
            cp.start()
        first = []
        for a in range(n):
            first.append(copy(a, 0, me, sibling, src=ins[a]))
            first += [copy(a, 1 + j, me, (*chip, c), src=ins[a]) for j, chip in enumerate(chips)]
        for cp in first:
            cp.start()
        passed = []
        for j, chip in enumerate(chips):
            for a in range(n):
                copy(a, 1 + j, (*chip, c), me).wait_recv()
            for a in range(n):
                cp = copy(a, 4 + j, (*chip, c), sibling)
                cp.start()
                passed.append(cp)
        for a in range(n):
            copy(a, 0, sibling, me).wait_recv()
        for j, chip in enumerate(chips):
            for a in range(n):
                copy(a, 4 + j, (*chip, 1 - c), me).wait_recv()
        for cp in first + passed:
            cp.wait_send()
        for cp in mine:
            cp.wait()

    return body


def _sibling_send_body(n, per_array_sems, handshake):
    def body(*refs):
        ins, outs = refs[:n], refs[n:2 * n]
        send_sems, recv_sems, _ = refs[2 * n:]
        x, y, c = lax.axis_index("x"), lax.axis_index("y"), lax.axis_index("c")
        sibling = (x, y, 1 - c)
        if handshake:
            _handshake([sibling])

        def sem(sems, a):
            return sems.at[a, 0] if per_array_sems else sems.at[0]

        copies = [pltpu.make_async_remote_copy(
            src_ref=ins[a].at[2 * q + 1 - c], dst_ref=outs[a].at[q], send_sem=sem(send_sems, a), recv_sem=sem(recv_sems, a),
            device_id=sibling, device_id_type=MESH) for a in range(n) for q in range(4)]
        for cp in copies:
            cp.start()
        for cp in copies:
            cp.wait_recv()
        for cp in copies:
            cp.wait_send()

    return body


def _chip_exchange_body(n, per_array_sems, handshake):
    def body(*refs):
        ins, outs = refs[:n], refs[n:2 * n]
        send_sems, recv_sems, local_sems = refs[2 * n:]
        x, y, c = lax.axis_index("x"), lax.axis_index("y"), lax.axis_index("c")
        my_chip = 2 * x + y
        chips = [(1 - x, y), (x, 1 - y), (1 - x, 1 - y)]
        if handshake:
            _handshake([(*chip, c) for chip in chips])

        def sem(sems, a, k):
            return sems.at[a, k] if per_array_sems else sems.at[k]

        def copy(a, k, landing):
            px, py = chips[k]
            return pltpu.make_async_remote_copy(
                src_ref=ins[a].at[2 * px + py], dst_ref=outs[a].at[landing], send_sem=sem(send_sems, a, k),
                recv_sem=sem(recv_sems, a, k), device_id=(px, py, c), device_id_type=MESH)

        mine = [pltpu.make_async_copy(ins[a].at[my_chip], outs[a].at[my_chip], local_sems.at[a if per_array_sems else 0])
                for a in range(n)]
        for cp in mine:
            cp.start()
        sent = [copy(a, k, my_chip) for a in range(n) for k in range(3)]
        for cp in sent:
            cp.start()
        for a in range(n):
            for k, (px, py) in enumerate(chips):
                copy(a, k, 2 * px + py).wait_recv()
        for cp in sent:
            cp.wait_send()
        for cp in mine:
            cp.wait()

    return body


def _all_gather(arrs, name):
    n = len(arrs)
    return pl.pallas_call(
        _gather_body(n, True, False, ["stack"] * n), name=name, in_specs=[ANY] * n, out_specs=[ANY] * n,
        out_shape=[jax.ShapeDtypeStruct((N_DEV,) + a.shape, a.dtype) for a in arrs],
        scratch_shapes=[pltpu.SemaphoreType.DMA((n, 7)), pltpu.SemaphoreType.DMA((n, 7)), pltpu.SemaphoreType.DMA((n,))],
    )(*arrs)


def _on_sequencer(body, arrs, out_type, name, collective_id):
    return pl.kernel(
        body, name=name, out_type=out_type, mesh=plsc.ScalarSubcoreMesh(axis_name="sequencer", num_cores=1),
        scratch_types=[pltpu.SemaphoreType.DMA((RING_SEMS,)), pltpu.SemaphoreType.DMA((RING_SEMS,)),
                       pltpu.SemaphoreType.DMA((1,))],
        compiler_params=pltpu.CompilerParams(collective_id=collective_id))(*arrs)


RING_SEMS = 8


def _ring_gather_body(n, per_array_sems, handshake, layouts):
    def body(*refs):
        ins, outs = refs[:n], refs[n:2 * n]
        send_sems, recv_sems, local_sems = refs[2 * n:]
        x, y, c = lax.axis_index("x"), lax.axis_index("y"), lax.axis_index("c")
        me, sibling = (x, y, c), (x, y, 1 - c)
        over_x, over_y, diagonal = (1 - x, y, c), (x, 1 - y, c), (1 - x, 1 - y, c)
        if handshake:
            _handshake([sibling, over_x, over_y])

        def sem(sems, a, k):
            return sems.at[a, k] if per_array_sems else sems.at[k]

        def slot(a, p, half=None):
            block = 4 * p[0] + 2 * p[1] + p[2]
            r, c_ = ins[a].shape
            first, count = (0, r) if half is None else (half * (r // 2), r // 2)
            if layouts[a] == "rows":
                return outs[a].at[pl.ds(block * r + first, count), :]
            if layouts[a] == "cols":
                return outs[a].at[pl.ds(first, count), pl.ds(block * c_, c_)]
            return outs[a].at[block, pl.ds(first, count), :]

        def copy(a, k, block, to, half=None, src=None):
            return pltpu.make_async_remote_copy(
                src_ref=slot(a, block, half) if src is None else src, dst_ref=slot(a, block, half),
                send_sem=sem(send_sems, a, k), recv_sem=sem(recv_sems, a, k), device_id=to, device_id_type=MESH)

        def start(copies):
            for cp in copies:
                cp.start()
            return copies

        def arrived(k, block, half=None):
            for a in range(n):
                copy(a, k, block, me, half).wait_recv()

        arrays = range(n)
        mine = start([pltpu.make_async_copy(ins[a], slot(a, me), local_sems.at[a if per_array_sems else 0]) for a in arrays])
        sent = start([copy(a, k, me, to, src=ins[a]) for a in arrays for k, to in ((0, sibling), (1, over_x), (2, over_y))])
        arrived(1, over_x)
        sent += start([copy(a, 3, over_x, over_y, half=0) for a in arrays] + [copy(a, 5, over_x, sibling) for a in arrays])
        arrived(2, over_y)
        sent += start([copy(a, 4, over_y, over_x, half=1) for a in arrays] + [copy(a, 6, over_y, sibling) for a in arrays])
        arrived(3, diagonal, half=0)
        arrived(4, diagonal, half=1)
        sent += start([copy(a, 7, diagonal, sibling) for a in arrays])
        arrived(0, sibling)
        for k, chip in ((5, over_x), (6, over_y), (7, diagonal)):
            arrived(k, (chip[0], chip[1], 1 - c))
        for cp in sent:
            cp.wait_send()
        for cp in mine:
            cp.wait()

    return body


def _all_gather_async(arrs, name, collective_id, layouts):
    shape = dict(stack=lambda r, c: (N_DEV, r, c), rows=lambda r, c: (N_DEV * r, c), cols=lambda r, c: (r, N_DEV * c))
    out_type = [jax.ShapeDtypeStruct(shape[lay](*a.shape), a.dtype) for a, lay in zip(arrs, layouts)]
    return _on_sequencer(_ring_gather_body(len(arrs), False, True, layouts), arrs, out_type, name, collective_id)


def _sibling_send_async(arrs, name, collective_id):
    out_type = [jax.ShapeDtypeStruct((N_DEV // 2,) + a.shape[1:], a.dtype) for a in arrs]
    return _on_sequencer(_sibling_send_body(len(arrs), False, True), arrs, out_type, name, collective_id)


def _chip_exchange_async(arrs, name, collective_id):
    out_type = [jax.ShapeDtypeStruct(a.shape, a.dtype) for a in arrs]
    return _on_sequencer(_chip_exchange_body(len(arrs), False, True), arrs, out_type, name, collective_id)


def _pair_sum(name, parity, mine, theirs):
    n = len(mine)

    def body(par_ref, *refs):
        for a in range(n):
            refs[2 * n + a][...] = (refs[a][...].astype(F32) + refs[n + a][...].astype(F32)).astype(refs[2 * n + a].dtype)

    def block(a):
        return (None,) + a.shape[1:]

    grid_spec = pltpu.PrefetchScalarGridSpec(
        num_scalar_prefetch=1, grid=(N_DEV // 2,),
        in_specs=[pl.BlockSpec(block(a), lambda q, par: (2 * q + par[0], 0, 0)) for a in mine]
        + [pl.BlockSpec(block(a), lambda q, par: (q, 0, 0)) for a in theirs],
        out_specs=[pl.BlockSpec(block(a), lambda q, par: (q, 0, 0)) for a in theirs])
    return pl.pallas_call(body, name=name, grid_spec=grid_spec, out_shape=[jax.ShapeDtypeStruct(a.shape, a.dtype) for a in theirs],
                          compiler_params=_params())(parity, *mine, *theirs)


NN = (((1,), (0,)), ((), ()))
NT = (((1,), (1,)), ((), ()))
TN = (((0,), (0,)), ((), ()))


def _matmul(name, a, b, dims, grid, a_spec, b_spec, outs, epilogue, extras=(), prologue=None):
    n_in = 2 + len(extras)
    n_out = len(outs)

    def body(*refs):
        a_ref, b_ref = refs[0], refs[1]
        ex, out_refs = refs[2:n_in], refs[n_in:n_in + n_out]
        a_tile = a_ref[...] if prologue is None else prologue(a_ref[...])
        epilogue(lax.dot_general(a_tile, b_ref[...], dims, preferred_element_type=F32), ex, out_refs)

    return pl.pallas_call(
        body, name=name, grid=grid, in_specs=[a_spec, b_spec] + [s for _, s in extras], out_specs=[s for _, _, s in outs],
        out_shape=[jax.ShapeDtypeStruct(sh, dt) for sh, dt, _ in outs], compiler_params=_params(),
    )(a, b, *[e for e, _ in extras])


def _square(tile):
    return tile * tile


def _ep_store(acc, ex, outs):
    outs[0][...] = acc.astype(outs[0].dtype)


def _ep_residual(acc, ex, outs):
    outs[0][...] = acc.astype(outs[0].dtype)
    outs[1][...] = ex[0][...] + ex[1][0:1, :] * acc


def _mm_tokens(name, a, w, transposed, out_dtypes, epilogue=_ep_store, extras=(), prologue=None, rows=MM_TOKENS):
    m, n = a.shape[0], w.shape[0 if transposed else 1]
    tm = _tile(m, rows)
    rows_spec = lambda width: pl.BlockSpec((tm, width), lambda i, j, kk: (i, 0))
    whole = lambda arr: pl.BlockSpec(arr.shape, lambda i, j, kk: (0, 0))
    return _matmul(name, a, w, NT if transposed else NN, (m // tm, 1, 1), rows_spec(a.shape[1]), whole(w),
                   [((m, n), dt, rows_spec(n)) for dt in out_dtypes], epilogue,
                   extras=[(e.table, _group_spec(e, 3)) if isinstance(e, Group) else (e, rows_spec(n)) for e in extras],
                   prologue=prologue)


def _resident(w):
    return pl.BlockSpec(w.shape, lambda i: (0, 0), pipeline_mode=pl.Buffered(1))


def _mlp_forward(name, h, w1, w2, x, gate):
    m, d = x.shape
    f = w1.shape[1]
    tm = _tile(m, MLP_TOKENS)
    rows = lambda width: pl.BlockSpec((tm, width), lambda i: (i, 0))

    def body(h_ref, w1_ref, w2_ref, x_ref, g_ref, r_ref, ff_ref, out_ref):
        r = jnp.maximum(lax.dot_general(h_ref[...], w1_ref[...], NN, preferred_element_type=F32), 0.0).astype(r_ref.dtype)
        r_ref[...] = r
        ff = lax.dot_general(r * r, w2_ref[...], NN, preferred_element_type=F32)
        ff_ref[...] = ff.astype(ff_ref.dtype)
        out_ref[...] = x_ref[...] + g_ref[0:1, :] * ff

    return pl.pallas_call(
        body, name=name, grid=(m // tm,), in_specs=[rows(d), _resident(w1), _resident(w2), rows(d), _group_spec(gate)],
        out_specs=[rows(f), rows(d), rows(d)],
        out_shape=[jax.ShapeDtypeStruct((m, f), MXU), jax.ShapeDtypeStruct((m, d), MXU), jax.ShapeDtypeStruct((m, d), F32)],
        compiler_params=_params())(h, w1, w2, x, gate.table)


def _mlp_backward(name, dff, r, w1, w2):
    m, d = dff.shape
    f = r.shape[1]
    tm = _tile(m, MLP_TOKENS)
    rows = lambda width: pl.BlockSpec((tm, width), lambda i: (i, 0))

    def body(dff_ref, r_ref, w1_ref, w2_ref, da_ref, dh_ref):
        ds = lax.dot_general(dff_ref[...], w2_ref[...], NT, preferred_element_type=F32)
        da = (ds * (2.0 * r_ref[...].astype(F32))).astype(da_ref.dtype)
        da_ref[...] = da
        dh_ref[...] = lax.dot_general(da, w1_ref[...], NT, preferred_element_type=F32)

    return pl.pallas_call(
        body, name=name, grid=(m // tm,), in_specs=[rows(d), rows(f), _resident(w1), _resident(w2)],
        out_specs=[rows(f), rows(d)],
        out_shape=[jax.ShapeDtypeStruct((m, f), MXU), jax.ShapeDtypeStruct((m, d), F32)],
        compiler_params=_params())(dff, r, w1, w2)


def _mm_grad(name, a, b, shard_cols, prologue=None):
    t, m = a.shape
    n = b.shape[1]
    tm = _tile(m, MM_TILE)
    if shard_cols:
        tn = n // N_DEV
        out = ((N_DEV, m, tn), MXU, pl.BlockSpec((None, tm, tn), lambda i, j, kk: (j, i, 0)))
    else:
        tn = _tile(n, MM_TILE)
        out = ((m, n), MXU, pl.BlockSpec((tm, tn), lambda i, j, kk: (i, j)))
    return _matmul(name, a, b, TN, (m // tm, n // tn, 1), pl.BlockSpec((t, tm), lambda i, j, kk: (0, i)),
                   pl.BlockSpec((t, tn), lambda i, j, kk: (0, j)), [out], _ep_store, prologue=prologue)[0]


def _rows_spec(tt, width=D):
    return pl.BlockSpec((tt, width), lambda i: (i, 0))


def _vec_spec(rows=SUBLANES, width=D):
    return pl.BlockSpec((rows, width), lambda i: (0, 0))


Group = collections.namedtuple("Group", ["table", "index"])


def _group_spec(group, grid_rank=1):
    if grid_rank == 1:
        return pl.BlockSpec((SUBLANES, D), lambda i: (group.index, 0))
    return pl.BlockSpec((SUBLANES, D), lambda i, j, kk: (group.index, 0))


def _norm_mod(name, x, vec):
    t = x.shape[0]
    tt = _tile(t, ROW_TILE)

    def body(x_ref, v_ref, h_ref):
        xv = x_ref[...]
        r = lax.rsqrt(jnp.mean(xv * xv, axis=-1, keepdims=True) + NORM_EPS)
        h = (xv * r) * v_ref[0:1, :]
        h_ref[...] = (h * (1.0 + v_ref[1:2, :]) + v_ref[2:3, :]).astype(h_ref.dtype)

    return pl.pallas_call(body, name=name, grid=(t // tt,), in_specs=[_rows_spec(tt), _group_spec(vec)], out_specs=_rows_spec(tt),
                          out_shape=jax.ShapeDtypeStruct((t, D), MXU))(x, vec.table)


GATE_ROW = 3


def _branch_grad(dx, y_ref, g_ref, dy_ref, acc_ref):
    dy_ref[...] = (dx * g_ref[0:1, :]).astype(dy_ref.dtype)
    acc_ref[GATE_ROW:GATE_ROW + 1, :] += _row_sum(dx * y_ref[...])


def _norm_mod_grad(name, dh, x, vec, dres, y=None, gate=None):
    t = x.shape[0]
    tt = _tile(t, ROW_TILE)
    nt = t // tt
    branch = y is not None

    def body(dh_ref, x_ref, v_ref, dres_ref, *rest):
        dx_ref, acc_ref = rest[2 * branch], rest[2 * branch + 1]
        i = pl.program_id(0)

        @pl.when(i == 0)
        def _():
            acc_ref[...] = jnp.zeros_like(acc_ref)

        xv, dhv = x_ref[...], dh_ref[...]
        r = lax.rsqrt(jnp.mean(xv * xv, axis=-1, keepdims=True) + NORM_EPS)
        xn = xv * r
        w = v_ref[0:1, :] * (1.0 + v_ref[1:2, :])
        acc_ref[0:1, :] += _row_sum(dhv * xn)
        acc_ref[2:3, :] += _row_sum(dhv)
        dxn = dhv * w
        dx = dres_ref[...] + r * (dxn - xn * jnp.mean(dxn * xn, axis=-1, keepdims=True))
        dx_ref[...] = dx
        if branch:
            _branch_grad(dx, rest[0], rest[1], rest[4], acc_ref)

        @pl.when(i == nt - 1)
        def _():
            dw = acc_ref[0:1, :]
            acc_ref[1:2, :] = dw * v_ref[0:1, :]
            acc_ref[0:1, :] = dw * (1.0 + v_ref[1:2, :])

    extra_in = [_rows_spec(tt), _group_spec(gate)] if branch else []
    extra_out = [_rows_spec(tt)] if branch else []
    extra_shape = [jax.ShapeDtypeStruct((t, D), MXU)] if branch else []
    return pl.pallas_call(
        body, name=name, grid=(nt,), in_specs=[_rows_spec(tt), _rows_spec(tt), _group_spec(vec), _rows_spec(tt)] + extra_in,
        out_specs=[_rows_spec(tt), _vec_spec()] + extra_out,
        out_shape=[jax.ShapeDtypeStruct((t, D), F32), jax.ShapeDtypeStruct((SUBLANES, D), F32)] + extra_shape,
    )(dh, x, vec.table, dres, *([y, gate.table] if branch else []))


def _loss_head(name, x, target, vec, y, gate):
    t = x.shape[0]
    tt = _tile(t, ROW_TILE)

    def body(x_ref, t_ref, v_ref, y_ref, g_ref, dx_ref, acc_ref, dy_ref):
        @pl.when(pl.program_id(0) == 0)
        def _():
            acc_ref[...] = jnp.zeros_like(acc_ref)

        xv = x_ref[...]
        r = lax.rsqrt(jnp.mean(xv * xv, axis=-1, keepdims=True) + NORM_EPS)
        xn = xv * r
        gain = v_ref[0:1, :]
        err = xn * gain - t_ref[...]
        acc_ref[1:2, :] += _row_sum(err * err) * (0.5 / D)
        dout = err * (1.0 / D)
        acc_ref[0:1, :] += _row_sum(dout * xn)
        dxn = dout * gain
        dx = r * (dxn - xn * jnp.mean(dxn * xn, axis=-1, keepdims=True))
        dx_ref[...] = dx
        _branch_grad(dx, y_ref, g_ref, dy_ref, acc_ref)

    return pl.pallas_call(
        body, name=name, grid=(t // tt,),
        in_specs=[_rows_spec(tt), _rows_spec(tt), _group_spec(vec), _rows_spec(tt), _group_spec(gate)],
        out_specs=[_rows_spec(tt), _vec_spec(), _rows_spec(tt)],
        out_shape=[jax.ShapeDtypeStruct((t, D), F32), jax.ShapeDtypeStruct((SUBLANES, D), F32),
                   jax.ShapeDtypeStruct((t, D), MXU)])(x, target, vec.table, y, gate.table)


def _shift_down(x, halo, k):
    y = pltpu.roll(x, k, 0)
    top = jnp.where(lax.broadcasted_iota(jnp.int32, halo.shape, 0) < k, pltpu.roll(halo, k, 0), y[0:SUBLANES, :])
    return jnp.concatenate([top, y[SUBLANES:, :]], axis=0)


def _shift_up(x, halo, k):
    n = x.shape[0]
    y = pltpu.roll(x, n - k, 0)
    bottom = jnp.where(lax.broadcasted_iota(jnp.int32, halo.shape, 0) >= SUBLANES - k, pltpu.roll(halo, SUBLANES - k, 0),
                       y[n - SUBLANES:, :])
    return jnp.concatenate([y[:n - SUBLANES, :], bottom], axis=0)


def _rg_gates(xb, halo, cw_ref, vec_ref, wa_ref, wx_ref, at_start):
    shifted = [xb] + [_shift_down(xb, halo, k) for k in range(1, CONV_WIDTH)]
    xc = vec_ref[0:1, :] + shifted[0] * cw_ref[CONV_WIDTH - 1:CONV_WIDTH, :]
    for k in range(1, CONV_WIDTH):
        xc = xc + shifted[k] * cw_ref[CONV_WIDTH - 1 - k:CONV_WIDTH - k, :]
    heads = [slice(h * HEAD_DIM, (h + 1) * HEAD_DIM) for h in range(HEADS)]
    pa = jnp.concatenate([_dot(xc[:, s], wa_ref[h]) for h, s in enumerate(heads)], axis=1) + vec_ref[1:2, :]
    px = jnp.concatenate([_dot(xc[:, s], wx_ref[h]) for h, s in enumerate(heads)], axis=1) + vec_ref[2:3, :]
    ra, ia = _sigmoid(pa), _sigmoid(px)
    nl = -vec_ref[3:4, :]
    sp = jnp.maximum(nl, 0.0) + jnp.log(1.0 + jnp.exp(-jnp.abs(nl)))
    log_a = (-LRU_C) * ra * sp
    a = jnp.exp(log_a)
    th = jnp.tanh(log_a)
    is_t0 = jnp.logical_and(lax.broadcasted_iota(jnp.int32, xb.shape, 0) == 0, at_start)
    mult = jnp.where(is_t0, 1.0, jnp.sqrt(-2.0 * th / (1.0 - th)))
    return dict(shifted=shifted, xc=xc, ra=ra, ia=ia, sp=sp, a=a, mult=mult, is_t0=is_t0, heads=heads)


def _rg_specs(tt, nt, order, cw, vec):
    blk = tt // SUBLANES
    return dict(
        x=pl.BlockSpec((tt, D), lambda i: (order(i), 0)), y=pl.BlockSpec((tt, D), lambda i: (order(i), 1)),
        halo=pl.BlockSpec((SUBLANES, D), lambda i: (jnp.maximum(order(i) * blk - 1, 0), 0)),
        cw=_group_spec(cw), vec=_group_spec(vec), w=pl.BlockSpec((HEADS, HEAD_DIM, HEAD_DIM), lambda i: (0, 0, 0)))


def _rg_forward(name, z, cw, vec, wa, wx):
    t = z.shape[0]
    tt = _tile(t, RG_ROWS)
    nt = t // tt
    sp = _rg_specs(tt, nt, lambda i: i, cw, vec)

    def body(zx_ref, zy_ref, halo_ref, cw_ref, vec_ref, wa_ref, wx_ref, p_ref, h_ref, a_s, u_s, carry):
        i = pl.program_id(0)

        @pl.when(i == 0)
        def _():
            carry[...] = jnp.zeros_like(carry)

        halo = jnp.where(i > 0, halo_ref[...], 0.0)
        g = _rg_gates(zx_ref[...], halo, cw_ref, vec_ref, wa_ref, wx_ref, i == 0)
        a_s[...] = g["a"]
        u_s[...] = g["mult"] * (g["ia"] * g["xc"])

        def group(gi, h):
            rows = pl.ds(pl.multiple_of(gi * SUBLANES, SUBLANES), SUBLANES)
            a8, u8 = a_s[rows, :], u_s[rows, :]
            out = []
            for j in range(SUBLANES):
                h = a8[j:j + 1, :] * h + u8[j:j + 1, :]
                out.append(h)
            h_ref[rows, :] = jnp.concatenate(out, axis=0)
            return h

        carry[0:1, :] = lax.fori_loop(0, tt // SUBLANES, group, carry[0:1, :])
        p_ref[...] = (h_ref[...] * _gelu(zy_ref[...])).astype(p_ref.dtype)

    return pl.pallas_call(
        body, name=name, grid=(nt,), in_specs=[sp["x"], sp["y"], sp["halo"], sp["cw"], sp["vec"], sp["w"], sp["w"]],
        out_specs=[_rows_spec(tt), _rows_spec(tt)],
        out_shape=[jax.ShapeDtypeStruct((t, D), MXU), jax.ShapeDtypeStruct((t, D), F32)],
        scratch_shapes=[pltpu.VMEM((tt, D), F32), pltpu.VMEM((tt, D), F32), pltpu.VMEM((SUBLANES, D), F32)],
        compiler_params=_params())(z, z, z, cw.table, vec.table, wa, wx)


def _rg_backward(name, dp, z, h, cw, vec, wa, wx):
    t = z.shape[0]
    tt = _tile(t, RG_ROWS)
    nt = t // tt
    rev = lambda i: nt - 1 - i
    sp = _rg_specs(tt, nt, rev, cw, vec)
    rows_rev = pl.BlockSpec((tt, D), lambda i: (rev(i), 0))

    def body(dp_ref, zx_ref, zy_ref, halo_ref, h_ref, hhalo_ref, cw_ref, vec_ref, wa_ref, wx_ref,
             dz_ref, dvec_ref, dcw_ref, dwa_ref, dwx_ref, a_s, d_s, carry, nxt):
        i = pl.program_id(0)
        j = rev(i)

        @pl.when(i == 0)
        def _():
            carry[...] = jnp.zeros_like(carry)
            nxt[...] = jnp.zeros_like(nxt)
            dvec_ref[...] = jnp.zeros_like(dvec_ref)
            dcw_ref[...] = jnp.zeros_like(dcw_ref)
            dwa_ref[...] = jnp.zeros_like(dwa_ref)
            dwx_ref[...] = jnp.zeros_like(dwx_ref)

        halo = jnp.where(j > 0, halo_ref[...], 0.0)
        g = _rg_gates(zx_ref[...], halo, cw_ref, vec_ref, wa_ref, wx_ref, j == 0)
        xc, ra, ia, a, mult = g["xc"], g["ra"], g["ia"], g["a"], g["mult"]
        hv, zy, dpv = h_ref[...], zy_ref[...], dp_ref[...]
        gelu, gelu_grad = _gelu_and_grad(zy)
        dyb = dpv * hv * gelu_grad
        a_s[...] = a
        d_s[...] = dpv * gelu

        def group(gi, c):
            rows = pl.ds(pl.multiple_of((tt // SUBLANES - 1 - gi) * SUBLANES, SUBLANES), SUBLANES)
            a8, d8 = a_s[rows, :], d_s[rows, :]
            out = [None] * SUBLANES
            for r in reversed(range(SUBLANES)):
                dht = d8[r:r + 1, :] + c
                out[r] = dht
                c = a8[r:r + 1, :] * dht
            d_s[rows, :] = jnp.concatenate(out, axis=0)
            return c

        carry[0:1, :] = lax.fori_loop(0, tt // SUBLANES, group, carry[0:1, :])
        dht = d_s[...]
        hprev = _shift_down(hv, jnp.where(j > 0, hhalo_ref[...], 0.0), 1)
        ixc = ia * xc
        dlog_a = dht * hprev * a + jnp.where(g["is_t0"], 0.0, dht * ixc * (-(a * a) / mult))
        dia = dht * mult * xc
        dxc = dht * mult * ia
        dra = dlog_a * ((-LRU_C) * g["sp"])
        nl = -vec_ref[3:4, :]
        dvec_ref[3:4, :] += _row_sum(dlog_a * ((-LRU_C) * ra)) * (-_sigmoid(nl))
        dpa = dra * ra * (1.0 - ra)
        dpx = dia * ia * (1.0 - ia)
        dvec_ref[1:2, :] += _row_sum(dpa)
        dvec_ref[2:3, :] += _row_sum(dpx)
        back = []
        for hd, s in enumerate(g["heads"]):
            dwa_ref[hd] += _dot_tn(xc[:, s], dpa[:, s])
            dwx_ref[hd] += _dot_tn(xc[:, s], dpx[:, s])
            back.append(_dot_nt(dpa[:, s], wa_ref[hd]) + _dot_nt(dpx[:, s], wx_ref[hd]))
        dxc = dxc + jnp.concatenate(back, axis=1)
        dvec_ref[0:1, :] += _row_sum(dxc)
        dxb = dxc * cw_ref[CONV_WIDTH - 1:CONV_WIDTH, :]
        for k in range(CONV_WIDTH):
            row = CONV_WIDTH - 1 - k
            dcw_ref[row:row + 1, :] += _row_sum(dxc * g["shifted"][k])
            if k:
                dxb = dxb + _shift_up(dxc, nxt[...], k) * cw_ref[row:row + 1, :]
        nxt[...] = dxc[0:SUBLANES, :]
        dz_ref[:, 0:D] = dxb.astype(dz_ref.dtype)
        dz_ref[:, D:2 * D] = dyb.astype(dz_ref.dtype)

    hhalo = pl.BlockSpec((SUBLANES, D), lambda i: (jnp.maximum(rev(i) * (tt // SUBLANES) - 1, 0), 0))
    wacc = pl.BlockSpec((HEADS, HEAD_DIM, HEAD_DIM), lambda i: (0, 0, 0))
    return pl.pallas_call(
        body, name=name, grid=(nt,),
        in_specs=[rows_rev, sp["x"], sp["y"], sp["halo"], rows_rev, hhalo, sp["cw"], sp["vec"], sp["w"], sp["w"]],
        out_specs=[pl.BlockSpec((tt, 2 * D), lambda i: (rev(i), 0)), _vec_spec(), _vec_spec(), wacc, wacc],
        out_shape=[jax.ShapeDtypeStruct((t, 2 * D), MXU), jax.ShapeDtypeStruct((SUBLANES, D), F32),
                   jax.ShapeDtypeStruct((SUBLANES, D), F32), jax.ShapeDtypeStruct((HEADS, HEAD_DIM, HEAD_DIM), F32),
                   jax.ShapeDtypeStruct((HEADS, HEAD_DIM, HEAD_DIM), F32)],
        scratch_shapes=[pltpu.VMEM((tt, D), F32), pltpu.VMEM((tt, D), F32), pltpu.VMEM((SUBLANES, D), F32),
                        pltpu.VMEM((SUBLANES, D), F32)],
        compiler_params=_params())(dp, z, z, z, h, h, cw.table, vec.table, wa, wx)


def _hg_chunk(zq, zf, zi, lb):
    c = HG_CHUNK
    q, q_grad = _silu_and_grad(zq)
    sg = _sigmoid(zf)
    fg = lb + (1.0 - lb) * sg
    k = 1.0 - fg
    row, col = lax.broadcasted_iota(jnp.int32, (c, c), 0), lax.broadcasted_iota(jnp.int32, (c, c), 1)
    tri, tri_t = (row >= col).astype(F32), (row <= col).astype(F32)
    b = _dot_exact(tri, jnp.log(fg))
    mid, last = b[c // 2 - 1:c // 2, :], b[c - 1:c, :]
    eq = jnp.exp(jnp.minimum(b - mid, EXP_CLAMP))
    ek = jnp.exp(jnp.minimum(mid - b, EXP_CLAMP))
    eb = jnp.exp(b)
    ed = jnp.exp(last - b)
    return dict(q=q, q_grad=q_grad, sg=sg, fg=fg, k=k, v=zi, eq=eq, ek=ek, eb=eb, ed=ed, elast=jnp.exp(last), tri=tri, tri_t=tri_t,
                qe=q * eq, ke=k * ek, qb=q * eb, kd=k * ed)


def _hg_specs(tt, order):
    return [pl.BlockSpec((tt, D), lambda i, col=col: (order(i), col)) for col in range(4)]


def _hg_forward(name, z, vec):
    t = z.shape[0]
    tt = _tile(t, ROW_TILE)
    nt, nc = t // tt, tt // HG_CHUNK
    heads = [slice(h * HEAD_DIM, (h + 1) * HEAD_DIM) for h in range(HEADS)]

    def body(zq_ref, zf_ref, zi_ref, zg_ref, vec_ref, p_ref, o_ref, ss_ref, state):
        @pl.when(pl.program_id(0) == 0)
        def _():
            state[...] = jnp.zeros_like(state)

        lb, gain = vec_ref[0:1, :], vec_ref[1:2, :]

        def chunk(ci, carry):
            rows = pl.ds(pl.multiple_of(ci * HG_CHUNK, HG_CHUNK), HG_CHUNK)
            g = _hg_chunk(zq_ref[rows, :], zf_ref[rows, :], zi_ref[rows, :], lb)
            zg = zg_ref[rows, :]
            causal = g["tri"] > 0.0
            qe_m, ke_m, qb_m, kd_m, v_m = (g[n].astype(MXU) for n in ("qe", "ke", "qb", "kd", "v"))
            atts = [_dot_nt(qe_m[:, s], ke_m[:, s]) for s in heads]
            o_parts, p_parts = [], []
            for hd, s in enumerate(heads):
                st = state[hd]
                ss_ref[ci, hd] = st
                att = jnp.where(causal, atts[hd], 0.0)
                o_parts.append(_dot(att, v_m[:, s]) + _dot_nt(qb_m[:, s], st))
                state[hd] = st * g["elast"][:, s] + _dot_tn(v_m[:, s], kd_m[:, s])
            for o, s in zip(o_parts, heads):
                r = lax.rsqrt(jnp.mean(o * o, axis=-1, keepdims=True) + GNORM_EPS)
                p_parts.append((o * r) * gain[:, s])
            o_ref[rows, :] = jnp.concatenate(o_parts, axis=1)
            p_ref[rows, :] = (jnp.concatenate(p_parts, axis=1) * _silu(zg)).astype(p_ref.dtype)
            return carry

        lax.fori_loop(0, nc, chunk, 0, unroll=True)

    return pl.pallas_call(
        body, name=name, grid=(nt,), in_specs=_hg_specs(tt, lambda i: i) + [_group_spec(vec)],
        out_specs=[_rows_spec(tt), _rows_spec(tt), pl.BlockSpec((nc, HEADS, HEAD_DIM, HEAD_DIM), lambda i: (i, 0, 0, 0))],
        out_shape=[jax.ShapeDtypeStruct((t, D), MXU), jax.ShapeDtypeStruct((t, D), F32),
                   jax.ShapeDtypeStruct((t // HG_CHUNK, HEADS, HEAD_DIM, HEAD_DIM), F32)],
        scratch_shapes=[pltpu.VMEM((HEADS, HEAD_DIM, HEAD_DIM), F32)], compiler_params=_params())(z, z, z, z, vec.table)


def _hg_backward(name, dp, z, o, ss, vec):
    t = z.shape[0]
    tt = _tile(t, ROW_TILE)
    nt, nc = t // tt, tt // HG_CHUNK
    rev = lambda i: nt - 1 - i
    heads = [slice(h * HEAD_DIM, (h + 1) * HEAD_DIM) for h in range(HEADS)]
    rows_rev = pl.BlockSpec((tt, D), lambda i: (rev(i), 0))

    def body(dp_ref, zq_ref, zf_ref, zi_ref, zg_ref, o_ref, ss_ref, vec_ref, dz_ref, acc_ref, dstate):
        @pl.when(pl.program_id(0) == 0)
        def _():
            dstate[...] = jnp.zeros_like(dstate)
            acc_ref[...] = jnp.zeros_like(acc_ref)

        lb, gain = vec_ref[0:1, :], vec_ref[1:2, :]

        def chunk(cr, carry):
            ci = nc - 1 - cr
            rows = pl.ds(pl.multiple_of(ci * HG_CHUNK, HG_CHUNK), HG_CHUNK)
            zq, zg = zq_ref[rows, :], zg_ref[rows, :]
            g = _hg_chunk(zq, zf_ref[rows, :], zi_ref[rows, :], lb)
            ov, dpv = o_ref[rows, :], dp_ref[rows, :]
            causal = g["tri"] > 0.0
            silu_g, silu_g_grad = _silu_and_grad(zg)
            don = dpv * silu_g
            dgate = dpv * silu_g_grad
            qe_m, ke_m, qb_m, kd_m, v_m = (g[n].astype(MXU) for n in ("qe", "ke", "qb", "kd", "v"))
            parts = {n: [] for n in ("dzg", "dgain", "dv", "dqe", "dke", "dqb", "dkd", "dlast")}
            dos, atts, datts = [], [], []
            for hd, s in enumerate(heads):
                oh = ov[:, s]
                r = lax.rsqrt(jnp.mean(oh * oh, axis=-1, keepdims=True) + GNORM_EPS)
                on = oh * r
                parts["dzg"].append(dgate[:, s] * (on * gain[:, s]))
                parts["dgain"].append(_row_sum(don[:, s] * on))
                dtmp = don[:, s] * gain[:, s]
                dos.append((r * (dtmp - on * jnp.mean(dtmp * on, axis=-1, keepdims=True))).astype(MXU))
            for hd, s in enumerate(heads):
                atts.append(_dot_nt(qe_m[:, s], ke_m[:, s]))
                datts.append(_dot_nt(dos[hd], v_m[:, s]))
            for hd, s in enumerate(heads):
                do, att, datt = dos[hd], jnp.where(causal, atts[hd], 0.0).astype(MXU), jnp.where(causal, datts[hd], 0.0).astype(MXU)
                st, dst = ss_ref[ci, hd], dstate[hd]
                st_m, dst_m = st.astype(MXU), dst.astype(MXU)
                qe, ke, qb, kd, v = qe_m[:, s], ke_m[:, s], qb_m[:, s], kd_m[:, s], v_m[:, s]
                parts["dv"].append(_dot_tn(att, do) + _dot_nt(kd, dst_m))
                parts["dqe"].append(_dot(datt, ke))
                parts["dke"].append(_dot_tn(datt, qe))
                parts["dqb"].append(_dot(do, st_m))
                parts["dkd"].append(_dot(v, dst_m))
                parts["dlast"].append(g["elast"][:, s] * _row_sum(dst * st))
                dstate[hd] = dst * g["elast"][:, s] + _dot_tn(do, qb)
            whole = {n: jnp.concatenate(p, axis=1) for n, p in parts.items()}
            acc_ref[1:2, :] += whole["dgain"]
            dqe, dke, dqb, dkd = whole["dqe"], whole["dke"], whole["dqb"], whole["dkd"]
            dq = dqe * g["eq"] + dqb * g["eb"]
            dk = dke * g["ek"] + dkd * g["ed"]
            dkdk = dkd * g["kd"]
            db = dqe * qe_m.astype(F32) - dke * ke_m.astype(F32) + dqb * g["qb"] - dkdk
            dlogf = _dot_exact(g["tri_t"], db) + (whole["dlast"] + _row_sum(dkdk))
            dfg = dlogf / g["fg"] - dk
            sg = g["sg"]
            acc_ref[0:1, :] += _row_sum(dfg * (1.0 - sg))
            dz_ref[rows, 0:D] = (dq * g["q_grad"]).astype(dz_ref.dtype)
            dz_ref[rows, D:2 * D] = (dfg * (1.0 - lb) * sg * (1.0 - sg)).astype(dz_ref.dtype)
            dz_ref[rows, 2 * D:3 * D] = whole["dv"].astype(dz_ref.dtype)
            dz_ref[rows, 3 * D:4 * D] = whole["dzg"].astype(dz_ref.dtype)
            return carry

        lax.fori_loop(0, nc, chunk, 0, unroll=True)

    return pl.pallas_call(
        body, name=name, grid=(nt,),
        in_specs=[rows_rev] + _hg_specs(tt, rev) + [rows_rev, pl.BlockSpec((nc, HEADS, HEAD_DIM, HEAD_DIM), lambda i: (rev(i), 0, 0, 0)),
                                                  _group_spec(vec)],
        out_specs=[pl.BlockSpec((tt, 4 * D), lambda i: (rev(i), 0)), _vec_spec()],
        out_shape=[jax.ShapeDtypeStruct((t, 4 * D), MXU), jax.ShapeDtypeStruct((SUBLANES, D), F32)],
        scratch_shapes=[pltpu.VMEM((HEADS, HEAD_DIM, HEAD_DIM), F32)], compiler_params=_params())(dp, z, z, z, z, o, ss, vec.table)


def _mod_forward(name, c_all, mod_w, lower):
    depth, _, sw = mod_w.shape

    def body(c_ref, w_ref, lo_ref, cs_ref, mod_ref, lb_ref):
        cs = _silu(c_ref[...])
        mod_ref[...] = _dot(cs, w_ref[...])

        @pl.when(pl.program_id(0) == 0)
        def _():
            cs_ref[...] = cs
            lo = lo_ref[...]
            e = jnp.exp(lo - jnp.max(lo, axis=0, keepdims=True))
            sm = e / jnp.sum(e, axis=0, keepdims=True)
            lb_ref[0:1, :] = jnp.zeros((1, D), F32)
            for l in range(1, depth):
                lb_ref[l:l + 1, :] = lb_ref[l - 1:l, :] + sm[l:l + 1, :]

    return pl.pallas_call(
        body, name=name, grid=(depth,),
        in_specs=[pl.BlockSpec((N_DEV, D), lambda l: (0, 0)), pl.BlockSpec((None, D, sw), lambda l: (l, 0, 0)),
                  pl.BlockSpec((depth, D), lambda l: (0, 0))],
        out_specs=[pl.BlockSpec((N_DEV, D), lambda l: (0, 0)), pl.BlockSpec((None, N_DEV, sw), lambda l: (l, 0, 0)),
                   pl.BlockSpec((depth, D), lambda l: (0, 0))],
        out_shape=[jax.ShapeDtypeStruct((N_DEV, D), F32), jax.ShapeDtypeStruct((depth, N_DEV, sw), F32),
                   jax.ShapeDtypeStruct((depth, D), F32)], compiler_params=_params())(c_all, mod_w, lower)


def _mod_weight_grad(name, cs_t, dmod):
    depth, pad, sw = dmod.shape

    def body(c_ref, d_ref, g_ref):
        g_ref[...] = _dot(c_ref[...], d_ref[...])

    return pl.pallas_call(
        body, name=name, grid=(depth,),
        in_specs=[pl.BlockSpec((D, pad), lambda l: (0, 0)), pl.BlockSpec((None, pad, sw), lambda l: (l, 0, 0))],
        out_specs=pl.BlockSpec((None, D, sw), lambda l: (l, 0, 0)),
        out_shape=jax.ShapeDtypeStruct((depth, D, sw), F32), compiler_params=_params())(cs_t, dmod)


def _lower_bound_grad(name, lower, dlb):
    depth = lower.shape[0]

    def body(lo_ref, d_ref, out_ref):
        lo = lo_ref[...]
        e = jnp.exp(lo - jnp.max(lo, axis=0, keepdims=True))
        sm = e / jnp.sum(e, axis=0, keepdims=True)
        out_ref[...] = jnp.zeros_like(out_ref)
        dsm = [jnp.zeros((1, D), F32)]
        for l in range(1, depth):
            tail = d_ref[l:l + 1, :]
            for m in range(l + 1, depth):
                tail = tail + d_ref[m:m + 1, :]
            dsm.append(tail)
        inner = sm[1:2, :] * dsm[1]
        for l in range(2, depth):
            inner = inner + sm[l:l + 1, :] * dsm[l]
        for l in range(depth):
            out_ref[l:l + 1, :] = sm[l:l + 1, :] * (dsm[l] - inner)
        for j in range(SUBLANES - depth):
            row = d_ref[depth + j:depth + j + 1, :]
            tot = row[:, 0:HEAD_DIM]
            for hd in range(1, HEADS):
                tot = tot + row[:, hd * HEAD_DIM:(hd + 1) * HEAD_DIM]
            out_ref[depth + j:depth + j + 1, 0:HEAD_DIM] = tot

    return pl.pallas_call(body, name=name, out_shape=jax.ShapeDtypeStruct((SUBLANES, D), F32))(lower, dlb)


def _adamw(name, parts, w, m, v, layer=None, prev=None):
    p, r, c = parts.shape
    tr = _tile(r, max(SUBLANES, ADAMW_STEP_BYTES // (4 * c * (p + 7))))
    stacked = layer is not None
    n_prev = 4 if prev is not None else 0

    def body(*refs):
        parts_ref, w_ref, m_ref, v_ref = refs[:4]
        g_ref, d_ref, m_out, v_out = refs[4 + n_prev:]
        g = parts_ref[0].astype(F32)
        for q in range(1, p):
            g = g + parts_ref[q].astype(F32)
        m2 = ADAM_B1 * m_ref[...] + (1.0 - ADAM_B1) * g
        v2 = ADAM_B2 * v_ref[...] + (1.0 - ADAM_B2) * (g * g)
        m_hat = m2 / (1.0 - ADAM_B1 ** ADAM_STEP)
        v_hat = v2 / (1.0 - ADAM_B2 ** ADAM_STEP)
        g_ref[...] = g
        d_ref[...] = -ADAM_LR * (m_hat / (jnp.sqrt(v_hat) + ADAM_EPS) + ADAM_WD * w_ref[...])
        m_out[...] = m2
        v_out[...] = v2

    if stacked:
        spec = pl.BlockSpec((None, tr, c), lambda i: (layer, i, 0))
    else:
        spec = pl.BlockSpec((tr, c), lambda i: (i, 0))
    return pl.pallas_call(
        body, name=name, grid=(r // tr,),
        in_specs=[pl.BlockSpec((p, tr, c), lambda i: (0, i, 0)), spec, spec, spec] + [ANY] * n_prev, out_specs=[spec] * 4,
        out_shape=[jax.ShapeDtypeStruct(w.shape, F32)] * 4,
        input_output_aliases={4 + q: q for q in range(n_prev)}, compiler_params=_params(),
    )(parts, w, m, v, *(prev or []))


def _hbm(a):
    return pltpu.with_memory_space_constraint(a, pltpu.HBM)


def _pad_rows(a, rows=SUBLANES):
    a = a.reshape(-1, a.shape[-1])
    return jnp.concatenate([a, jnp.zeros((rows - a.shape[0], a.shape[1]), a.dtype)], axis=0) if a.shape[0] < rows else a


def kernel(x, c, mod_w, mod_b, norm_mix, norm_mlp, norm_final, rg_w_in, rg_conv_w, rg_conv_b, rg_w_a, rg_b_a, rg_w_x, rg_b_x, rg_lambda, rg_w_out, hg_w_in, hg_lower_bounds, hg_gnorm, hg_w_out, mlp_w1, mlp_w2, loss_target, m_mod_w, m_mod_b, m_norm_mix, m_norm_mlp, m_norm_final, m_rg_w_in, m_rg_conv_w, m_rg_conv_b, m_rg_w_a, m_rg_b_a, m_rg_w_x, m_rg_b_x, m_rg_lambda, m_rg_w_out, m_hg_w_in, m_hg_lower_bounds, m_hg_gnorm, m_hg_w_out, m_mlp_w1, m_mlp_w2, v_mod_w, v_mod_b, v_norm_mix, v_norm_mlp, v_norm_final, v_rg_w_in, v_rg_conv_w, v_rg_conv_b, v_rg_w_a, v_rg_b_a, v_rg_w_x, v_rg_b_x, v_rg_lambda, v_rg_w_out, v_hg_w_in, v_hg_lower_bounds, v_hg_gnorm, v_hg_w_out, v_mlp_w1, v_mlp_w2):
    me = 4 * lax.axis_index("x") + 2 * lax.axis_index("y") + lax.axis_index("c")
    x0 = x[0]
    target = loss_target[0]
    n_rg, n_hg = rg_w_in.shape[0], hg_w_in.shape[0]
    sw_mod = mod_w.shape[2]

    c_all, cw_all = _all_gather([_pad_rows(c), rg_conv_w.reshape(n_rg * CONV_WIDTH, -1)], "gather_cond")
    c_all = c_all[:, 0, :]
    conv_w = cw_all.transpose(1, 0, 2).reshape(n_rg, CONV_WIDTH, D)
    cs_all, mod_part, lb_all = _mod_forward("mod_forward", c_all, mod_w, hg_lower_bounds)
    (mod_gathered,) = _all_gather([mod_part.reshape(DEPTH * N_DEV, sw_mod)], "gather_mod")

    ids = iter(range(5 * DEPTH))
    shards = []
    for layer in range(DEPTH):
        j = layer // 2
        w_in, w_out = (rg_w_in[j], rg_w_out[j]) if layer % 2 == 0 else (hg_w_in[j], hg_w_out[j])
        shards.append([w_in.astype(MXU), w_out.astype(MXU), mlp_w1[layer].astype(MXU), mlp_w2[layer].astype(MXU)])
    shards, mod_gathered = lax.optimization_barrier((shards, mod_gathered))
    weights = []
    for layer in range(DEPTH):
        g_in, g_out = _all_gather_async(shards[layer][:2], f"gather_mixer_weights_{layer}", next(ids), ["cols", "rows"])
        g_w1, g_w2 = _all_gather_async(shards[layer][2:], f"gather_mlp_weights_{layer}", next(ids), ["cols", "rows"])
        weights.append(dict(w_in=g_in, w_out=g_out, w1=g_w1, w2=g_w2))

    mod_mine = lax.dynamic_index_in_dim(mod_gathered.reshape(N_DEV, DEPTH, N_DEV, sw_mod), me, axis=2, keepdims=False)
    mod = mod_mine.transpose(1, 0, 2).reshape(DEPTH, 6, D) + mod_b.reshape(DEPTH, 6, D)

    pieces, n_groups = [], 0
    placed = [dict() for _ in range(DEPTH)]

    def place(rows):
        nonlocal n_groups
        rows = [r.reshape(-1, D).astype(F32) for r in rows]
        pieces.extend(rows + [jnp.zeros((SUBLANES - sum(r.shape[0] for r in rows), D), F32)])
        n_groups += 1
        return n_groups - 1

    for layer in range(DEPTH):
        j, at = layer // 2, placed[layer]
        at["vec_mix"] = place([norm_mix[layer], mod[layer, 1], mod[layer, 0]])
        at["vec_mlp"] = place([norm_mlp[layer], mod[layer, 4], mod[layer, 3]])
        at["gate_mix"], at["gate_mlp"] = place([mod[layer, 2]]), place([mod[layer, 5]])
        if layer % 2 == 0:
            at["cw"] = place([conv_w[j]])
            at["vec"] = place([rg_conv_b[j], rg_b_a[j], rg_b_x[j], rg_lambda[j]])
        else:
            at["vec"] = place([lb_all[layer], jnp.tile(hg_gnorm[j], HEADS)])
    final_group = place([norm_final])
    table = _hbm(jnp.concatenate(pieces, axis=0))

    saved = []
    xl = x0
    for layer in range(DEPTH):
        j, wt = layer // 2, weights[layer]
        is_rg = layer % 2 == 0
        s = dict(x=xl, **{k: Group(table, g) for k, g in placed[layer].items()})
        s["h"] = _hbm(_norm_mod("norm_mod", xl, s["vec_mix"]))
        if is_rg:
            (s["z"],) = _mm_tokens("rg_in", s["h"], wt["w_in"], False, (F32,))
            s["wa"], s["wx"] = rg_w_a[j].astype(MXU), rg_w_x[j].astype(MXU)
            s["p"], s["hr"] = _rg_forward("rg_forward", s["z"], s["cw"], s["vec"], s["wa"], s["wx"])
        else:
            (s["z"],) = _mm_tokens("hg_in", s["h"], wt["w_in"], False, (F32,))
            s["p"], s["o"], s["ss"] = _hg_forward("hg_forward", s["z"], s["vec"])
        s["y"], s["x1"] = _mm_tokens("mix_out", s["p"], wt["w_out"], False, (MXU, F32), _ep_residual, extras=(xl, s["gate_mix"]))
        s["p"] = _hbm(s["p"])
        s["h2"] = _hbm(_norm_mod("norm_mod", s["x1"], s["vec_mlp"]))
        s["r"], s["ff"], xl = _mlp_forward("mlp_forward", s["h2"], wt["w1"], wt["w2"], s["x1"], s["gate_mlp"])
        saved.append(s)

    dx, head, dff = _loss_head("loss_head", xl, target, Group(table, final_group), saved[-1]["ff"], saved[-1]["gate_mlp"])
    g_mlp = head[GATE_ROW:GATE_ROW + 1]

    results = {}
    big = dict(rg_w_in=(rg_w_in, m_rg_w_in, v_rg_w_in), rg_w_out=(rg_w_out, m_rg_w_out, v_rg_w_out),
               hg_w_in=(hg_w_in, m_hg_w_in, v_hg_w_in), hg_w_out=(hg_w_out, m_hg_w_out, v_hg_w_out),
               mlp_w1=(mlp_w1, m_mlp_w1, v_mlp_w1), mlp_w2=(mlp_w2, m_mlp_w2, v_mlp_w2))

    def update_layer(layer, landed):
        mixer = "rg" if layer % 2 == 0 else "hg"
        l_in, l_out, l_w1, l_w2 = landed
        for nm, parts, idx in ((f"{mixer}_w_in", l_in, layer // 2), (f"{mixer}_w_out", l_out, layer // 2),
                               ("mlp_w1", l_w1, layer), ("mlp_w2", l_w2, layer)):
            w, m, v = big[nm]
            results[nm] = _adamw(f"adamw_{nm}", parts, w, m, v, layer=idx, prev=results.get(nm))

    landed = [None] * DEPTH
    small = [None] * DEPTH
    parity = lax.axis_index("c").astype(jnp.int32).reshape(1)

    def send_chip_sums(layer, grads, from_sibling, anchor):
        sums = _pair_sum("pair_sum", parity, grads, from_sibling)
        sums, anchor = lax.optimization_barrier((sums, anchor))
        landed[layer] = _chip_exchange_async(sums, f"exchange_grads_{layer}", next(ids))
        return anchor

    pending = None
    for layer in reversed(range(DEPTH)):
        j, wt, s = layer // 2, weights[layer], saved[layer]
        is_rg = layer % 2 == 0
        dff = _hbm(dff)
        da, dh2 = _mlp_backward("mlp_backward", dff, s["r"], wt["w1"], wt["w2"])
        if pending is not None:
            da = send_chip_sums(*pending, da)
        dw2 = _mm_grad("mlp_out_grad", s["r"], dff, False, prologue=_square).reshape(N_DEV, -1, D)
        dw1 = _mm_grad("mlp_in_grad", s["h2"], da, True)
        dx1, n_mlp, dyb = _norm_mod_grad("norm_mod_grad", dh2, s["x1"], s["vec_mlp"], dx, s["y"], s["gate_mix"])
        dyb = _hbm(dyb)
        (dp,) = _mm_tokens("mix_out_t", dyb, wt["w_out"], True, (F32,))
        dw_out = _hbm(_mm_grad("mix_out_grad", s["p"], dyb, False).reshape(N_DEV, -1, D))
        if is_rg:
            dz, dvec, dcw, dwa, dwx = _rg_backward("rg_backward", dp, s["z"], s["hr"], s["cw"], s["vec"], s["wa"], s["wx"])
            gate_grads, dz = lax.optimization_barrier(
                ([dwa.reshape(-1, HEAD_DIM).astype(MXU), dwx.reshape(-1, HEAD_DIM).astype(MXU)], dz))
            gate_parts = _all_gather_async(gate_grads, f"gather_gate_grads_{j}", next(ids), ["stack", "stack"])
            mixer_small = dict(dvec=dvec, dcw=dcw, gate_parts=gate_parts)
            dw_in = _mm_grad("rg_in_grad", s["h"], dz, True)
            (dh,) = _mm_tokens("rg_in_t", dz, wt["w_in"], True, (F32,))
        else:
            dz, dvec = _hg_backward("hg_backward", dp, s["z"], s["o"], s["ss"], s["vec"])
            mixer_small = dict(dvec=dvec)
            dw_in = _mm_grad("hg_in_grad", s["h"], dz, True)
            (dh,) = _mm_tokens("hg_in_t", dz, wt["w_in"], True, (F32,))
        grads = [dw_in, dw_out, dw1, dw2]
        pending = (layer, grads, _sibling_send_async(grads, f"pair_grads_{layer}", next(ids)))
        small[layer] = dict(g_mlp=g_mlp, n_mlp=n_mlp, g_mix=n_mlp[GATE_ROW:GATE_ROW + 1], **mixer_small)
        if layer:
            below = saved[layer - 1]
            dx, n_mix, dff = _norm_mod_grad("norm_mod_grad", dh, s["x"], s["vec_mix"], dx1, below["ff"], below["gate_mlp"])
            g_mlp = n_mix[GATE_ROW:GATE_ROW + 1]
        else:
            dx, n_mix = _norm_mod_grad("norm_mod_grad_first", dh, s["x"], s["vec_mix"], dx1)
        small[layer]["n_mix"] = n_mix
        if layer + 1 < DEPTH:
            stream = (dx, dff) if layer else (dx,)
            landed[layer + 1], stream = lax.optimization_barrier((landed[layer + 1], stream))
            if "gate_parts" in small[layer + 1]:
                small[layer + 1]["gate_parts"], stream = lax.optimization_barrier((small[layer + 1]["gate_parts"], stream))
            dx, dff = stream if layer else (stream[0], None)
            update_layer(layer + 1, landed[layer + 1])

    dlb_rows = [jnp.zeros((1, D), F32) if l % 2 == 0 else small[l]["dvec"][0:1] for l in range(DEPTH)]
    dgn_rows = [small[2 * j + 1]["dvec"][1:2] for j in range(n_hg)]
    lb_grad = _lower_bound_grad("lower_bound_grad", hg_lower_bounds, _pad_rows(jnp.concatenate(dlb_rows + dgn_rows, axis=0)))
    dmod = jnp.stack([jnp.concatenate([small[l]["n_mix"][2], small[l]["n_mix"][1], small[l]["g_mix"][0],
                                       small[l]["n_mlp"][2], small[l]["n_mlp"][1], small[l]["g_mlp"][0]]) for l in range(DEPTH)])
    groups = [
        dmod.reshape(DEPTH * 6, D),
        jnp.stack([small[l]["n_mix"][0] for l in range(DEPTH)]),
        jnp.stack([small[l]["n_mlp"][0] for l in range(DEPTH)]),
        head[0:1],
        jnp.stack([small[2 * j]["dvec"][0] for j in range(n_rg)]),
        jnp.stack([small[2 * j]["dvec"][1] for j in range(n_rg)]),
        jnp.stack([small[2 * j]["dvec"][2] for j in range(n_rg)]),
        jnp.stack([small[2 * j]["dvec"][3] for j in range(n_rg)]),
        lb_grad[0:DEPTH],
        jnp.concatenate([lb_grad[DEPTH + j:DEPTH + j + 1, 0:HEAD_DIM] for j in range(n_hg)]
                        + [jnp.zeros((1, D - n_hg * HEAD_DIM), F32)], axis=1),
        jnp.concatenate([small[2 * j]["dcw"][0:CONV_WIDTH] for j in range(n_rg)], axis=0),
        head[1:2],
    ]
    params = [mod_b, norm_mix, norm_mlp, norm_final, rg_conv_b, rg_b_a, rg_b_x, rg_lambda, hg_lower_bounds, hg_gnorm]
    moms = [m_mod_b, m_norm_mix, m_norm_mlp, m_norm_final, m_rg_conv_b, m_rg_b_a, m_rg_b_x, m_rg_lambda, m_hg_lower_bounds, m_hg_gnorm]
    vars_ = [v_mod_b, v_norm_mix, v_norm_mlp, v_norm_final, v_rg_conv_b, v_rg_b_a, v_rg_b_x, v_rg_lambda, v_hg_lower_bounds, v_hg_gnorm]
    offsets, rows_of, at = [], [], 0
    for g in groups:
        offsets.append(at)
        rows_of.append(g.shape[0])
        at += -(-g.shape[0] // SUBLANES) * SUBLANES
    packed = jnp.concatenate([_pad_rows(g, -(-g.shape[0] // SUBLANES) * SUBLANES) for g in groups], axis=0)
    (small_parts,) = _all_gather([packed], "gather_small_grads")
    last_layer, last_grads, from_sibling = pending
    from_sibling, small_parts = lax.optimization_barrier((from_sibling, small_parts))
    send_chip_sums(last_layer, last_grads, from_sibling, dx)

    def pack_like(arrs):
        out = []
        for g_rows, off, a in zip(rows_of, offsets, arrs):
            flat = a.reshape(-1)
            flat = jnp.concatenate([flat, jnp.zeros((g_rows * D - flat.shape[0],), F32)])
            out.append(_pad_rows(flat.reshape(g_rows, D), -(-g_rows // SUBLANES) * SUBLANES))
        rest = packed.shape[0] - sum(o.shape[0] for o in out)
        return jnp.concatenate(out + [jnp.zeros((rest, D), F32)], axis=0)

    small_out = _adamw("adamw_small", small_parts, pack_like(params), pack_like(moms), pack_like(vars_))

    def unpack(q, idx, like):
        rows = small_out[q][offsets[idx]:offsets[idx] + rows_of[idx]]
        return rows.reshape(-1)[:like.size].reshape(like.shape)

    loss = jnp.sum(small_out[0][offsets[11]])
    names =["mod_b", "norm_mix", "norm_mlp", "norm_final", "rg_conv_b", "rg_b_a", "rg_b_x", "rg_lambda", "hg_lower_bounds", "hg_gnorm"]
    for idx, (nm, like) in enumerate(zip(names, params)):
        results[nm] = [unpack(q, idx, like) for q in range(4)]

    cw_parts = lax.dynamic_slice_in_dim(small_parts[:, offsets[10]:offsets[10] + n_rg * CONV_WIDTH, :], me * (D // N_DEV), D // N_DEV, axis=2)
    shp = rg_conv_w.shape
    results["rg_conv_w"] = [o.reshape(shp) for o in _adamw(
        "adamw_conv", cw_parts, rg_conv_w.reshape(-1, shp[-1]), m_rg_conv_w.reshape(-1, shp[-1]), v_rg_conv_w.reshape(-1, shp[-1]))]
    shp = rg_w_a.shape
    stacked = (n_rg, HEADS * HEAD_DIM, HEAD_DIM)
    for nm, which, (w, m, v) in (("rg_w_a", 0, (rg_w_a, m_rg_w_a, v_rg_w_a)), ("rg_w_x", 1, (rg_w_x, m_rg_w_x, v_rg_w_x))):
        out = None
        for j in reversed(range(n_rg)):
            out = _adamw("adamw_gate", small[2 * j]["gate_parts"][which], w.reshape(stacked), m.reshape(stacked),
                         v.reshape(stacked), layer=j, prev=out)
        results[nm] = [o.reshape(shp) for o in out]

    dmod_all = small_parts[:, 0:DEPTH * 6, :].reshape(N_DEV, DEPTH, 6 * D)
    dmod_cols = lax.dynamic_slice_in_dim(dmod_all, me * sw_mod, sw_mod, axis=2).transpose(1, 0, 2)
    pad = HEAD_DIM - N_DEV
    dmod_pad = jnp.concatenate([dmod_cols, jnp.zeros((DEPTH, pad, sw_mod), F32)], axis=1).astype(MXU)
    cs_t = jnp.concatenate([cs_all.T, jnp.zeros((D, pad), F32)], axis=1).astype(MXU)
    g_mod_w = _mod_weight_grad("mod_weight_grad", cs_t, dmod_pad)
    results["mod_w"] = [o.reshape(mod_w.shape) for o in _adamw(
        "adamw_mod", g_mod_w.reshape(1, -1, sw_mod), mod_w.reshape(-1, sw_mod), m_mod_w.reshape(-1, sw_mod), v_mod_w.reshape(-1, sw_mod))]

    update_layer(0, landed[0])

    order = ["mod_w", "mod_b", "norm_mix", "norm_mlp", "norm_final", "rg_w_in", "rg_conv_w", "rg_conv_b", "rg_w_a", "rg_b_a", "rg_w_x",
             "rg_b_x", "rg_lambda", "rg_w_out", "hg_w_in", "hg_lower_bounds", "hg_gnorm", "hg_w_out", "mlp_w1", "mlp_w2"]
    return (loss, dx[None], *[results[n][0] for n in order], *[results[n][1] for n in order],
            *[results[n][2] for n in order], *[results[n][3] for n in order])
```

```python
import collections

import jax
import jax.numpy as jnp
from jax import lax
from jax.experimental import pallas as pl
from jax.experimental.pallas import tpu as pltpu
from jax.experimental.pallas import tpu_sc as plsc

F32 = jnp.float32
MXU = jnp.bfloat16

N_DEV = 8
D = 1024
DEPTH = 4
HEADS = 8
HEAD_DIM = 128
CONV_WIDTH = 4
LRU_C = 8.0
HG_CHUNK = 64
NORM_EPS = 1e-6
GNORM_EPS = 1e-5
ADAM_LR = 0.001
ADAM_B1 = 0.9
ADAM_B2 = 0.999
ADAM_EPS = 1e-08
ADAM_WD = 0.01
ADAM_STEP = 10
GELU_C = 0.7978845608028654
GELU_K = 0.044715
EXP_CLAMP = 80.0
SUBLANES = 8
VMEM_LIMIT = 48 * 1024 * 1024
ROW_TILE = 256
RG_ROWS = 256
MM_TILE = 1024
MM_TOKENS = 512
MLP_TOKENS = 256
ADAMW_STEP_BYTES = 8 * 1024 * 1024

MESH = pl.DeviceIdType.MESH
ANY = pl.BlockSpec(memory_space=pl.ANY)


def _params():
    return pltpu.CompilerParams(vmem_limit_bytes=VMEM_LIMIT)


def _tile(n, target):
    if n <= target:
        return n
    t = target // SUBLANES * SUBLANES
    while n % t:
        t -= SUBLANES
    return t


def _sigmoid(x):
    return 1.0 / (1.0 + jnp.exp(-x))


def _silu(x):
    return x * _sigmoid(x)


def _silu_and_grad(x):
    s = _sigmoid(x)
    return x * s, s * (1.0 + x * (1.0 - s))


def _gelu(y):
    return 0.5 * y * (1.0 + jnp.tanh(GELU_C * (y + GELU_K * y * y * y)))


def _gelu_and_grad(y):
    y2 = y * y
    th = jnp.tanh(GELU_C * (y + GELU_K * y2 * y))
    half = 0.5 * (1.0 + th)
    return y * half, half + 0.5 * y * (1.0 - th * th) * GELU_C * (1.0 + 3.0 * GELU_K * y2)


def _dot(a, b):
    return lax.dot_general(a.astype(MXU), b.astype(MXU), (((1,), (0,)), ((), ())), preferred_element_type=F32)


def _dot_nt(a, b):
    return lax.dot_general(a.astype(MXU), b.astype(MXU), (((1,), (1,)), ((), ())), preferred_element_type=F32)


def _dot_tn(a, b):
    return lax.dot_general(a.astype(MXU), b.astype(MXU), (((0,), (0,)), ((), ())), preferred_element_type=F32)


def _dot_exact(tri, x):
    t = tri.astype(MXU)
    hi = x.astype(MXU)
    r1 = x - hi.astype(F32)
    mid = r1.astype(MXU)
    lo = (r1 - mid.astype(F32)).astype(MXU)
    dn = (((1,), (0,)), ((), ()))
    return (lax.dot_general(t, hi, dn, preferred_element_type=F32) + lax.dot_general(t, mid, dn, preferred_element_type=F32)
            + lax.dot_general(t, lo, dn, preferred_element_type=F32))


def _row_sum(v):
    return jnp.sum(v, axis=0, keepdims=True)


def _handshake(partners):
    barrier = pltpu.get_barrier_semaphore()
    for p in partners:
        pl.semaphore_signal(barrier, inc=1, device_id=p, device_id_type=MESH)
    pl.semaphore_wait(barrier, len(partners))


def _gather_body(n, per_array_sems, handshake, layouts):
    def body(*refs):
        ins, outs = refs[:n], refs[n:2 * n]
        send_sems, recv_sems, local_sems = refs[2 * n:]
        x, y, c = lax.axis_index("x"), lax.axis_index("y"), lax.axis_index("c")
        me, sibling = (x, y, c), (x, y, 1 - c)
        chips = [(1 - x, y), (x, 1 - y), (1 - x, 1 - y)]
        if handshake:
            _handshake([sibling] + [(*chip, c) for chip in chips])

        def sem(sems, a, k):
            return sems.at[a, k] if per_array_sems else sems.at[k]

        def slot(a, p):
            block = 4 * p[0] + 2 * p[1] + p[2]
            r, c_ = ins[a].shape
            if layouts[a] == "rows":
                return outs[a].at[pl.ds(block * r, r), :]
            if layouts[a] == "cols":
                return outs[a].at[:, pl.ds(block * c_, c_)]
            return outs[a].at[block]

        def copy(a, k, block, to, src=None):
            return pltpu.make_async_remote_copy(
                src_ref=slot(a, block) if src is None else src, dst_ref=slot(a, block),
                send_sem=sem(send_sems, a, k), recv_sem=sem(recv_sems, a, k), device_id=to, device_id_type=MESH)

        mine = [pltpu.make_async_copy(ins[a], slot(a, me), local_sems.at[a if per_array_sems else 0]) for a in range(n)]
        for cp in mine:
            cp.start()
        first = []
        for a in range(n):
            first.append(copy(a, 0, me, sibling, src=ins[a]))
            first += [copy(a, 1 + j, me, (*chip, c), src=ins[a]) for j, chip in enumerate(chips)]
        for cp in first:
            cp.start()
        passed = []
        for j, chip in enumerate(chips):
            for a in range(n):
                copy(a, 1 + j, (*chip, c), me).wait_recv()
            for a in range(n):
                cp = copy(a, 4 + j, (*chip, c), sibling)
                cp.start()
                passed.append(cp)
        for a in range(n):
            copy(a, 0, sibling, me).wait_recv()
        for j, chip in enumerate(chips):
            for a in range(n):
                copy(a, 4 + j, (*chip, 1 - c), me).wait_recv()
        for cp in first + passed:
            cp.wait_send()
        for cp in mine:
            cp.wait()

    return body


def _sibling_send_body(n, per_array_sems, handshake):
    def body(*refs):
        ins, outs = refs[:n], refs[n:2 * n]
        send_sems, recv_sems, _ = refs[2 * n:]
        x, y, c = lax.axis_index("x"), lax.axis_index("y"), lax.axis_index("c")
        sibling = (x, y, 1 - c)
        if handshake:
            _handshake([sibling])

        def sem(sems, a):
            return sems.at[a, 0] if per_array_sems else sems.at[0]

        copies = [pltpu.make_async_remote_copy(
            src_ref=ins[a].at[2 * q + 1 - c], dst_ref=outs[a].at[q], send_sem=sem(send_sems, a), recv_sem=sem(recv_sems, a),
            device_id=sibling, device_id_type=MESH) for a in range(n) for q in range(4)]
        for cp in copies:
            cp.start()
        for cp in copies:
            cp.wait_recv()
        for cp in copies:
            cp.wait_send()

    return body


def _chip_exchange_body(n, per_array_sems, handshake):
    def body(*refs):
        ins, outs = refs[:n], refs[n:2 * n]
        send_sems, recv_sems, local_sems = refs[2 * n:]
        x, y, c = lax.axis_index("x"), lax.axis_index("y"), lax.axis_index("c")
        my_chip = 2 * x + y
        chips = [(1 - x, y), (x, 1 - y), (1 - x, 1 - y)]
        if handshake:
            _handshake([(*chip, c) for chip in chips])

        def sem(sems, a, k):
            return sems.at[a, k] if per_array_sems else sems.at[k]

        def copy(a, k, landing):
            px, py = chips[k]
            return pltpu.make_async_remote_copy(
                src_ref=ins[a].at[2 * px + py], dst_ref=outs[a].at[landing], send_sem=sem(send_sems, a, k),
                recv_sem=sem(recv_sems, a, k), device_id=(px, py, c), device_id_type=MESH)

        mine = [pltpu.make_async_copy(ins[a].at[my_chip], outs[a].at[my_chip], local_sems.at[a if per_array_sems else 0])
                for a in range(n)]
        for cp in mine:
            cp.start()
        sent = [copy(a, k, my_chip) for a in range(n) for k in range(3)]
        for cp in sent:
            cp.start()
        for a in range(n):
            for k, (px, py) in enumerate(chips):
                copy(a, k, 2 * px + py).wait_recv()
        for cp in sent:
            cp.wait_send()
        for cp in mine:
            cp.wait()

    return body


def _all_gather(arrs, name):
    n = len(arrs)
    return pl.pallas_call(
        _gather_body(n, True, False, ["stack"] * n), name=name, in_specs=[ANY] * n, out_specs=[ANY] * n,
        out_shape=[jax.ShapeDtypeStruct((N_DEV,) + a.shape, a.dtype) for a in arrs],
        scratch_shapes=[pltpu.SemaphoreType.DMA((n, 7)), pltpu.SemaphoreType.DMA((n, 7)), pltpu.SemaphoreType.DMA((n,))],
    )(*arrs)


def _on_sequencer(body, arrs, out_type, name, collective_id):
    return pl.kernel(
        body, name=name, out_type=out_type, mesh=plsc.ScalarSubcoreMesh(axis_name="sequencer", num_cores=1),
        scratch_types=[pltpu.SemaphoreType.DMA((RING_SEMS,)), pltpu.SemaphoreType.DMA((RING_SEMS,)),
                       pltpu.SemaphoreType.DMA((1,))],
        compiler_params=pltpu.CompilerParams(collective_id=collective_id))(*arrs)


RING_SEMS = 8


def _ring_gather_body(n, per_array_sems, handshake, layouts):
    def body(*refs):
        ins, outs = refs[:n], refs[n:2 * n]
        send_sems, recv_sems, local_sems = refs[2 * n:]
        x, y, c = lax.axis_index("x"), lax.axis_index("y"), lax.axis_index("c")
        me, sibling = (x, y, c), (x, y, 1 - c)
        over_x, over_y, diagonal = (1 - x, y, c), (x, 1 - y, c), (1 - x, 1 - y, c)
        if handshake:
            _handshake([sibling, over_x, over_y])

        def sem(sems, a, k):
            return sems.at[a, k] if per_array_sems else sems.at[k]

        def slot(a, p, half=None):
            block = 4 * p[0] + 2 * p[1] + p[2]
            r, c_ = ins[a].shape
            first, count = (0, r) if half is None else (half * (r // 2), r // 2)
            if layouts[a] == "rows":
                return outs[a].at[pl.ds(block * r + first, count), :]
            if layouts[a] == "cols":
                return outs[a].at[pl.ds(first, count), pl.ds(block * c_, c_)]
            return outs[a].at[block, pl.ds(first, count), :]

        def copy(a, k, block, to, half=None, src=None):
            return pltpu.make_async_remote_copy(
                src_ref=slot(a, block, half) if src is None else src, dst_ref=slot(a, block, half),
                send_sem=sem(send_sems, a, k), recv_sem=sem(recv_sems, a, k), device_id=to, device_id_type=MESH)

        def start(copies):
            for cp in copies:
                cp.start()
            return copies

        def arrived(k, block, half=None):
            for a in range(n):
                copy(a, k, block, me, half).wait_recv()

        arrays = range(n)
        mine = start([pltpu.make_async_copy(ins[a], slot(a, me), local_sems.at[a if per_array_sems else 0]) for a in arrays])
        sent = start([copy(a, k, me, to, src=ins[a]) for a in arrays for k, to in ((0, sibling), (1, over_x), (2, over_y))])
        arrived(1, over_x)
        sent += start([copy(a, 3, over_x, over_y, half=0) for a in arrays] + [copy(a, 5, over_x, sibling) for a in arrays])
        arrived(2, over_y)
        sent += start([copy(a, 4, over_y, over_x, half=1) for a in arrays] + [copy(a, 6, over_y, sibling) for a in arrays])
        arrived(3, diagonal, half=0)
        arrived(4, diagonal, half=1)
        sent += start([copy(a, 7, diagonal, sibling) for a in arrays])
        arrived(0, sibling)
        for k, chip in ((5, over_x), (6, over_y), (7, diagonal)):
            arrived(k, (chip[0], chip[1], 1 - c))
        for cp in sent:
            cp.wait_send()
        for cp in mine:
            cp.wait()

    return body


def _all_gather_async(arrs, name, collective_id, layouts):
    shape = dict(stack=lambda r, c: (N_DEV, r, c), rows=lambda r, c: (N_DEV * r, c), cols=lambda r, c: (r, N_DEV * c))
    out_type = [jax.ShapeDtypeStruct(shape[lay](*a.shape), a.dtype) for a, lay in zip(arrs, layouts)]
    return _on_sequencer(_ring_gather_body(len(arrs), False, True, layouts), arrs, out_type, name, collective_id)


def _sibling_send_async(arrs, name, collective_id):
    out_type = [jax.ShapeDtypeStruct((N_DEV // 2,) + a.shape[1:], a.dtype) for a in arrs]
    return _on_sequencer(_sibling_send_body(len(arrs), False, True), arrs, out_type, name, collective_id)


def _chip_exchange_async(arrs, name, collective_id):
    out_type = [jax.ShapeDtypeStruct(a.shape, a.dtype) for a in arrs]
    return _on_sequencer(_chip_exchange_body(len(arrs), False, True), arrs, out_type, name, collective_id)


def _pair_sum(name, parity, mine, theirs):
    n = len(mine)

    def body(par_ref, *refs):
        for a in range(n):
            refs[2 * n + a][...] = (refs[a][...].astype(F32) + refs[n + a][...].astype(F32)).astype(refs[2 * n + a].dtype)

    def block(a):
        return (None,) + a.shape[1:]

    grid_spec = pltpu.PrefetchScalarGridSpec(
        num_scalar_prefetch=1, grid=(N_DEV // 2,),
        in_specs=[pl.BlockSpec(block(a), lambda q, par: (2 * q + par[0], 0, 0)) for a in mine]
        + [pl.BlockSpec(block(a), lambda q, par: (q, 0, 0)) for a in theirs],
        out_specs=[pl.BlockSpec(block(a), lambda q, par: (q, 0, 0)) for a in theirs])
    return pl.pallas_call(body, name=name, grid_spec=grid_spec, out_shape=[jax.ShapeDtypeStruct(a.shape, a.dtype) for a in theirs],
                          compiler_params=_params())(parity, *mine, *theirs)


NN = (((1,), (0,)), ((), ()))
NT = (((1,), (1,)), ((), ()))
TN = (((0,), (0,)), ((), ()))


def _matmul(name, a, b, dims, grid, a_spec, b_spec, outs, epilogue, extras=(), prologue=None):
    n_in = 2 + len(extras)
    n_out = len(outs)

    def body(*refs):
        a_ref, b_ref = refs[0], refs[1]
        ex, out_refs = refs[2:n_in], refs[n_in:n_in + n_out]
        a_tile = a_ref[...] if prologue is None else prologue(a_ref[...])
        epilogue(lax.dot_general(a_tile, b_ref[...], dims, preferred_element_type=F32), ex, out_refs)

    return pl.pallas_call(
        body, name=name, grid=grid, in_specs=[a_spec, b_spec] + [s for _, s in extras], out_specs=[s for _, _, s in outs],
        out_shape=[jax.ShapeDtypeStruct(sh, dt) for sh, dt, _ in outs], compiler_params=_params(),
    )(a, b, *[e for e, _ in extras])


def _square(tile):
    return tile * tile


def _ep_store(acc, ex, outs):
    outs[0][...] = acc.astype(outs[0].dtype)


def _ep_residual(acc, ex, outs):
    outs[0][...] = acc.astype(outs[0].dtype)
    outs[1][...] = ex[0][...] + ex[1][0:1, :] * acc


def _mm_tokens(name, a, w, transposed, out_dtypes, epilogue=_ep_store, extras=(), prologue=None, rows=MM_TOKENS):
    m, n = a.shape[0], w.shape[0 if transposed else 1]
    tm = _tile(m, rows)
    rows_spec = lambda width: pl.BlockSpec((tm, width), lambda i, j, kk: (i, 0))
    whole = lambda arr: pl.BlockSpec(arr.shape, lambda i, j, kk: (0, 0))
    return _matmul(name, a, w, NT if transposed else NN, (m // tm, 1, 1), rows_spec(a.shape[1]), whole(w),
                   [((m, n), dt, rows_spec(n)) for dt in out_dtypes], epilogue,
                   extras=[(e.table, _group_spec(e, 3)) if isinstance(e, Group) else (e, rows_spec(n)) for e in extras],
                   prologue=prologue)


def _resident(w):
    return pl.BlockSpec(w.shape, lambda i: (0, 0), pipeline_mode=pl.Buffered(1))


def _mlp_forward(name, h, w1, w2, x, gate):
    m, d = x.shape
    f = w1.shape[1]
    tm = _tile(m, MLP_TOKENS)
    rows = lambda width: pl.BlockSpec((tm, width), lambda i: (i, 0))

    def body(h_ref, w1_ref, w2_ref, x_ref, g_ref, r_ref, ff_ref, out_ref):
        r = jnp.maximum(lax.dot_general(h_ref[...], w1_ref[...], NN, preferred_element_type=F32), 0.0).astype(r_ref.dtype)
        r_ref[...] = r
        ff = lax.dot_general(r * r, w2_ref[...], NN, preferred_element_type=F32)
        ff_ref[...] = ff.astype(ff_ref.dtype)
        out_ref[...] = x_ref[...] + g_ref[0:1, :] * ff

    return pl.pallas_call(
        body, name=name, grid=(m // tm,), in_specs=[rows(d), _resident(w1), _resident(w2), rows(d), _group_spec(gate)],
        out_specs=[rows(f), rows(d), rows(d)],
        out_shape=[jax.ShapeDtypeStruct((m, f), MXU), jax.ShapeDtypeStruct((m, d), MXU), jax.ShapeDtypeStruct((m, d), F32)],
        compiler_params=_params())(h, w1, w2, x, gate.table)


def _mlp_backward(name, dff, r, w1, w2, x, vec, dres, y, gate):
    m, d = dff.shape
    f = r.shape[1]
    tm = _tile(m, MLP_TOKENS)
    nt = m // tm
    rows = lambda width: pl.BlockSpec((tm, width), lambda i: (i, 0))

    def body(dff_ref, r_ref, w1_ref, w2_ref, x_ref, v_ref, dres_ref, y_ref, g_ref, da_ref, dx_ref, acc_ref, dy_ref):
        ds = lax.dot_general(dff_ref[...], w2_ref[...], NT, preferred_element_type=F32)
        da = (ds * (2.0 * r_ref[...].astype(F32))).astype(da_ref.dtype)
        da_ref[...] = da
        dh = lax.dot_general(da, w1_ref[...], NT, preferred_element_type=F32)
        _norm_grad_step(nt, dh, x_ref, v_ref, dres_ref, dx_ref, acc_ref, (y_ref, g_ref, dy_ref))

    return pl.pallas_call(
        body, name=name, grid=(nt,),
        in_specs=[rows(d), rows(f), _resident(w1), _resident(w2), rows(d), _group_spec(vec), rows(d), rows(d), _group_spec(gate)],
        out_specs=[rows(f), rows(d), _vec_spec(), rows(d)],
        out_shape=[jax.ShapeDtypeStruct((m, f), MXU), jax.ShapeDtypeStruct((m, d), F32),
                   jax.ShapeDtypeStruct((SUBLANES, d), F32), jax.ShapeDtypeStruct((m, d), MXU)],
        compiler_params=_params())(dff, r, w1, w2, x, vec.table, dres, y, gate.table)


def _mm_norm_grad(name, a, w, x, vec, dres, y=None, gate=None):
    m = a.shape[0]
    tm = _tile(m, MM_TOKENS)
    nt = m // tm
    branch = y is not None
    rows = lambda width: pl.BlockSpec((tm, width), lambda i: (i, 0))

    def body(a_ref, w_ref, x_ref, v_ref, dres_ref, *rest):
        dh = lax.dot_general(a_ref[...], w_ref[...], NT, preferred_element_type=F32)
        dx_ref, acc_ref = rest[2 * branch], rest[2 * branch + 1]
        _norm_grad_step(nt, dh, x_ref, v_ref, dres_ref, dx_ref, acc_ref, (rest[0], rest[1], rest[4]) if branch else None)

    return pl.pallas_call(
        body, name=name, grid=(nt,),
        in_specs=[rows(a.shape[1]), _resident(w), rows(D), _group_spec(vec), rows(D)] + ([rows(D), _group_spec(gate)] if branch else []),
        out_specs=[rows(D), _vec_spec()] + ([rows(D)] if branch else []),
        out_shape=[jax.ShapeDtypeStruct((m, D), F32), jax.ShapeDtypeStruct((SUBLANES, D), F32)]
        + ([jax.ShapeDtypeStruct((m, D), MXU)] if branch else []),
        compiler_params=_params())(a, w, x, vec.table, dres, *([y, gate.table] if branch else []))


def _mm_grad(name, a, b, shard_cols, prologue=None):
    t, m = a.shape
    n = b.shape[1]
    tm = _tile(m, MM_TILE)
    if shard_cols:
        tn = n // N_DEV
        out = ((N_DEV, m, tn), MXU, pl.BlockSpec((None, tm, tn), lambda i, j, kk: (j, i, 0)))
    else:
        tn = _tile(n, MM_TILE)
        out = ((m, n), MXU, pl.BlockSpec((tm, tn), lambda i, j, kk: (i, j)))
    return _matmul(name, a, b, TN, (m // tm, n // tn, 1), pl.BlockSpec((t, tm), lambda i, j, kk: (0, i)),
                   pl.BlockSpec((t, tn), lambda i, j, kk: (0, j)), [out], _ep_store, prologue=prologue)[0]


def _rows_spec(tt, width=D):
    return pl.BlockSpec((tt, width), lambda i: (i, 0))


def _vec_spec(rows=SUBLANES, width=D):
    return pl.BlockSpec((rows, width), lambda i: (0, 0))


Group = collections.namedtuple("Group", ["table", "index"])


def _group_spec(group, grid_rank=1):
    if grid_rank == 1:
        return pl.BlockSpec((SUBLANES, D), lambda i: (group.index, 0))
    return pl.BlockSpec((SUBLANES, D), lambda i, j, kk: (group.index, 0))


def _norm_mod(name, x, vec):
    t = x.shape[0]
    tt = _tile(t, ROW_TILE)

    def body(x_ref, v_ref, h_ref):
        xv = x_ref[...]
        r = lax.rsqrt(jnp.mean(xv * xv, axis=-1, keepdims=True) + NORM_EPS)
        h = (xv * r) * v_ref[0:1, :]
        h_ref[...] = (h * (1.0 + v_ref[1:2, :]) + v_ref[2:3, :]).astype(h_ref.dtype)

    return pl.pallas_call(body, name=name, grid=(t // tt,), in_specs=[_rows_spec(tt), _group_spec(vec)], out_specs=_rows_spec(tt),
                          out_shape=jax.ShapeDtypeStruct((t, D), MXU))(x, vec.table)


GATE_ROW = 3


def _branch_grad(dx, y_ref, g_ref, dy_ref, acc_ref):
    dy_ref[...] = (dx * g_ref[0:1, :]).astype(dy_ref.dtype)
    acc_ref[GATE_ROW:GATE_ROW + 1, :] += _row_sum(dx * y_ref[...])


def _norm_grad_step(nt, dh, x_ref, v_ref, dres_ref, dx_ref, acc_ref, branch):
    i = pl.program_id(0)

    @pl.when(i == 0)
    def _():
        acc_ref[...] = jnp.zeros_like(acc_ref)

    xv = x_ref[...]
    r = lax.rsqrt(jnp.mean(xv * xv, axis=-1, keepdims=True) + NORM_EPS)
    xn = xv * r
    w = v_ref[0:1, :] * (1.0 + v_ref[1:2, :])
    acc_ref[0:1, :] += _row_sum(dh * xn)
    acc_ref[2:3, :] += _row_sum(dh)
    dxn = dh * w
    dx = dres_ref[...] + r * (dxn - xn * jnp.mean(dxn * xn, axis=-1, keepdims=True))
    dx_ref[...] = dx
    if branch is not None:
        _branch_grad(dx, branch[0], branch[1], branch[2], acc_ref)

    @pl.when(i == nt - 1)
    def _():
        dw = acc_ref[0:1, :]
        acc_ref[1:2, :] = dw * v_ref[0:1, :]
        acc_ref[0:1, :] = dw * (1.0 + v_ref[1:2, :])


def _loss_head(name, x, target, vec, y, gate):
    t = x.shape[0]
    tt = _tile(t, ROW_TILE)

    def body(x_ref, t_ref, v_ref, y_ref, g_ref, dx_ref, acc_ref, dy_ref):
        @pl.when(pl.program_id(0) == 0)
        def _():
            acc_ref[...] = jnp.zeros_like(acc_ref)

        xv = x_ref[...]
        r = lax.rsqrt(jnp.mean(xv * xv, axis=-1, keepdims=True) + NORM_EPS)
        xn = xv * r
        gain = v_ref[0:1, :]
        err = xn * gain - t_ref[...]
        acc_ref[1:2, :] += _row_sum(err * err) * (0.5 / D)
        dout = err * (1.0 / D)
        acc_ref[0:1, :] += _row_sum(dout * xn)
        dxn = dout * gain
        dx = r * (dxn - xn * jnp.mean(dxn * xn, axis=-1, keepdims=True))
        dx_ref[...] = dx
        _branch_grad(dx, y_ref, g_ref, dy_ref, acc_ref)

    return pl.pallas_call(
        body, name=name, grid=(t // tt,),
        in_specs=[_rows_spec(tt), _rows_spec(tt), _group_spec(vec), _rows_spec(tt), _group_spec(gate)],
        out_specs=[_rows_spec(tt), _vec_spec(), _rows_spec(tt)],
        out_shape=[jax.ShapeDtypeStruct((t, D), F32), jax.ShapeDtypeStruct((SUBLANES, D), F32),
                   jax.ShapeDtypeStruct((t, D), MXU)])(x, target, vec.table, y, gate.table)


def _shift_down(x, halo, k):
    y = pltpu.roll(x, k, 0)
    top = jnp.where(lax.broadcasted_iota(jnp.int32, halo.shape, 0) < k, pltpu.roll(halo, k, 0), y[0:SUBLANES, :])
    return jnp.concatenate([top, y[SUBLANES:, :]], axis=0)


def _shift_up(x, halo, k):
    n = x.shape[0]
    y = pltpu.roll(x, n - k, 0)
    bottom = jnp.where(lax.broadcasted_iota(jnp.int32, halo.shape, 0) >= SUBLANES - k, pltpu.roll(halo, SUBLANES - k, 0),
                       y[n - SUBLANES:, :])
    return jnp.concatenate([y[:n - SUBLANES, :], bottom], axis=0)


def _rg_gates(xb, halo, cw_ref, vec_ref, wa_ref, wx_ref, at_start):
    shifted = [xb] + [_shift_down(xb, halo, k) for k in range(1, CONV_WIDTH)]
    xc = vec_ref[0:1, :] + shifted[0] * cw_ref[CONV_WIDTH - 1:CONV_WIDTH, :]
    for k in range(1, CONV_WIDTH):
        xc = xc + shifted[k] * cw_ref[CONV_WIDTH - 1 - k:CONV_WIDTH - k, :]
    heads = [slice(h * HEAD_DIM, (h + 1) * HEAD_DIM) for h in range(HEADS)]
    pa = jnp.concatenate([_dot(xc[:, s], wa_ref[h]) for h, s in enumerate(heads)], axis=1) + vec_ref[1:2, :]
    px = jnp.concatenate([_dot(xc[:, s], wx_ref[h]) for h, s in enumerate(heads)], axis=1) + vec_ref[2:3, :]
    ra, ia = _sigmoid(pa), _sigmoid(px)
    nl = -vec_ref[3:4, :]
    sp = jnp.maximum(nl, 0.0) + jnp.log(1.0 + jnp.exp(-jnp.abs(nl)))
    log_a = (-LRU_C) * ra * sp
    a = jnp.exp(log_a)
    th = jnp.tanh(log_a)
    is_t0 = jnp.logical_and(lax.broadcasted_iota(jnp.int32, xb.shape, 0) == 0, at_start)
    mult = jnp.where(is_t0, 1.0, jnp.sqrt(-2.0 * th / (1.0 - th)))
    return dict(shifted=shifted, xc=xc, ra=ra, ia=ia, sp=sp, a=a, mult=mult, is_t0=is_t0, heads=heads)


def _rg_specs(tt, nt, order, cw, vec):
    blk = tt // SUBLANES
    return dict(
        x=pl.BlockSpec((tt, D), lambda i: (order(i), 0)), y=pl.BlockSpec((tt, D), lambda i: (order(i), 1)),
        halo=pl.BlockSpec((SUBLANES, D), lambda i: (jnp.maximum(order(i) * blk - 1, 0), 0)),
        cw=_group_spec(cw), vec=_group_spec(vec), w=pl.BlockSpec((HEADS, HEAD_DIM, HEAD_DIM), lambda i: (0, 0, 0)))


def _rg_forward(name, z, cw, vec, wa, wx):
    t = z.shape[0]
    tt = _tile(t, RG_ROWS)
    nt = t // tt
    sp = _rg_specs(tt, nt, lambda i: i, cw, vec)

    def body(zx_ref, zy_ref, halo_ref, cw_ref, vec_ref, wa_ref, wx_ref, p_ref, h_ref, a_s, u_s, carry):
        i = pl.program_id(0)

        @pl.when(i == 0)
        def _():
            carry[...] = jnp.zeros_like(carry)

        halo = jnp.where(i > 0, halo_ref[...], 0.0)
        g = _rg_gates(zx_ref[...], halo, cw_ref, vec_ref, wa_ref, wx_ref, i == 0)
        a_s[...] = g["a"]
        u_s[...] = g["mult"] * (g["ia"] * g["xc"])

        def group(gi, h):
            rows = pl.ds(pl.multiple_of(gi * SUBLANES, SUBLANES), SUBLANES)
            a8, u8 = a_s[rows, :], u_s[rows, :]
            out = []
            for j in range(SUBLANES):
                h = a8[j:j + 1, :] * h + u8[j:j + 1, :]
                out.append(h)
            h_ref[rows, :] = jnp.concatenate(out, axis=0)
            return h

        carry[0:1, :] = lax.fori_loop(0, tt // SUBLANES, group, carry[0:1, :])
        p_ref[...] = (h_ref[...] * _gelu(zy_ref[...])).astype(p_ref.dtype)

    return pl.pallas_call(
        body, name=name, grid=(nt,), in_specs=[sp["x"], sp["y"], sp["halo"], sp["cw"], sp["vec"], sp["w"], sp["w"]],
        out_specs=[_rows_spec(tt), _rows_spec(tt)],
        out_shape=[jax.ShapeDtypeStruct((t, D), MXU), jax.ShapeDtypeStruct((t, D), F32)],
        scratch_shapes=[pltpu.VMEM((tt, D), F32), pltpu.VMEM((tt, D), F32), pltpu.VMEM((SUBLANES, D), F32)],
        compiler_params=_params())(z, z, z, cw.table, vec.table, wa, wx)


def _rg_backward(name, dp, z, h, cw, vec, wa, wx):
    t = z.shape[0]
    tt = _tile(t, RG_ROWS)
    nt = t // tt
    rev = lambda i: nt - 1 - i
    sp = _rg_specs(tt, nt, rev, cw, vec)
    rows_rev = pl.BlockSpec((tt, D), lambda i: (rev(i), 0))

    def body(dp_ref, zx_ref, zy_ref, halo_ref, h_ref, hhalo_ref, cw_ref, vec_ref, wa_ref, wx_ref,
             dz_ref, dvec_ref, dcw_ref, dwa_ref, dwx_ref, a_s, d_s, carry, nxt):
        i = pl.program_id(0)
        j = rev(i)

        @pl.when(i == 0)
        def _():
            carry[...] = jnp.zeros_like(carry)
            nxt[...] = jnp.zeros_like(nxt)
            dvec_ref[...] = jnp.zeros_like(dvec_ref)
            dcw_ref[...] = jnp.zeros_like(dcw_ref)
            dwa_ref[...] = jnp.zeros_like(dwa_ref)
            dwx_ref[...] = jnp.zeros_like(dwx_ref)

        halo = jnp.where(j > 0, halo_ref[...], 0.0)
        g = _rg_gates(zx_ref[...], halo, cw_ref, vec_ref, wa_ref, wx_ref, j == 0)
        xc, ra, ia, a, mult = g["xc"], g["ra"], g["ia"], g["a"], g["mult"]
        hv, zy, dpv = h_ref[...], zy_ref[...], dp_ref[...]
        gelu, gelu_grad = _gelu_and_grad(zy)
        dyb = dpv * hv * gelu_grad
        a_s[...] = a
        d_s[...] = dpv * gelu

        def group(gi, c):
            rows = pl.ds(pl.multiple_of((tt // SUBLANES - 1 - gi) * SUBLANES, SUBLANES), SUBLANES)
            a8, d8 = a_s[rows, :], d_s[rows, :]
            out = [None] * SUBLANES
            for r in reversed(range(SUBLANES)):
                dht = d8[r:r + 1, :] + c
                out[r] = dht
                c = a8[r:r + 1, :] * dht
            d_s[rows, :] = jnp.concatenate(out, axis=0)
            return c

        carry[0:1, :] = lax.fori_loop(0, tt // SUBLANES, group, carry[0:1, :])
        dht = d_s[...]
        hprev = _shift_down(hv, jnp.where(j > 0, hhalo_ref[...], 0.0), 1)
        ixc = ia * xc
        dlog_a = dht * hprev * a + jnp.where(g["is_t0"], 0.0, dht * ixc * (-(a * a) / mult))
        dia = dht * mult * xc
        dxc = dht * mult * ia
        dra = dlog_a * ((-LRU_C) * g["sp"])
        nl = -vec_ref[3:4, :]
        dvec_ref[3:4, :] += _row_sum(dlog_a * ((-LRU_C) * ra)) * (-_sigmoid(nl))
        dpa = dra * ra * (1.0 - ra)
        dpx = dia * ia * (1.0 - ia)
        dvec_ref[1:2, :] += _row_sum(dpa)
        dvec_ref[2:3, :] += _row_sum(dpx)
        xc_m, dpa_m, dpx_m = xc.astype(MXU), dpa.astype(MXU), dpx.astype(MXU)
        back = [_dot_nt(dpa_m[:, s], wa_ref[hd]) + _dot_nt(dpx_m[:, s], wx_ref[hd]) for hd, s in enumerate(g["heads"])]
        dwa = [_dot_tn(xc_m[:, s], dpa_m[:, s]) for s in g["heads"]]
        dwx = [_dot_tn(xc_m[:, s], dpx_m[:, s]) for s in g["heads"]]
        for hd in range(HEADS):
            dwa_ref[hd] += dwa[hd]
            dwx_ref[hd] += dwx[hd]
        dxc = dxc + jnp.concatenate(back, axis=1)
        dvec_ref[0:1, :] += _row_sum(dxc)
        dxb = dxc * cw_ref[CONV_WIDTH - 1:CONV_WIDTH, :]
        for k in range(CONV_WIDTH):
            row = CONV_WIDTH - 1 - k
            dcw_ref[row:row + 1, :] += _row_sum(dxc * g["shifted"][k])
            if k:
                dxb = dxb + _shift_up(dxc, nxt[...], k) * cw_ref[row:row + 1, :]
        nxt[...] = dxc[0:SUBLANES, :]
        dz_ref[:, 0:D] = dxb.astype(dz_ref.dtype)
        dz_ref[:, D:2 * D] = dyb.astype(dz_ref.dtype)

    hhalo = pl.BlockSpec((SUBLANES, D), lambda i: (jnp.maximum(rev(i) * (tt // SUBLANES) - 1, 0), 0))
    wacc = pl.BlockSpec((HEADS, HEAD_DIM, HEAD_DIM), lambda i: (0, 0, 0))
    return pl.pallas_call(
        body, name=name, grid=(nt,),
        in_specs=[rows_rev, sp["x"], sp["y"], sp["halo"], rows_rev, hhalo, sp["cw"], sp["vec"], sp["w"], sp["w"]],
        out_specs=[pl.BlockSpec((tt, 2 * D), lambda i: (rev(i), 0)), _vec_spec(), _vec_spec(), wacc, wacc],
        out_shape=[jax.ShapeDtypeStruct((t, 2 * D), MXU), jax.ShapeDtypeStruct((SUBLANES, D), F32),
                   jax.ShapeDtypeStruct((SUBLANES, D), F32), jax.ShapeDtypeStruct((HEADS, HEAD_DIM, HEAD_DIM), F32),
                   jax.ShapeDtypeStruct((HEADS, HEAD_DIM, HEAD_DIM), F32)],
        scratch_shapes=[pltpu.VMEM((tt, D), F32), pltpu.VMEM((tt, D), F32), pltpu.VMEM((SUBLANES, D), F32),
                        pltpu.VMEM((SUBLANES, D), F32)],
        compiler_params=_params())(dp, z, z, z, h, h, cw.table, vec.table, wa, wx)


def _hg_chunk(zq, zf, zi, lb):
    c = HG_CHUNK
    q, q_grad = _silu_and_grad(zq)
    sg = _sigmoid(zf)
    fg = lb + (1.0 - lb) * sg
    k = 1.0 - fg
    row, col = lax.broadcasted_iota(jnp.int32, (c, c), 0), lax.broadcasted_iota(jnp.int32, (c, c), 1)
    tri, tri_t = (row >= col).astype(F32), (row <= col).astype(F32)
    b = _dot_exact(tri, jnp.log(fg))
    mid, last = b[c // 2 - 1:c // 2, :], b[c - 1:c, :]
    eq = jnp.exp(jnp.minimum(b - mid, EXP_CLAMP))
    ek = jnp.exp(jnp.minimum(mid - b, EXP_CLAMP))
    eb = jnp.exp(b)
    ed = jnp.exp(last - b)
    return dict(q=q, q_grad=q_grad, sg=sg, fg=fg, k=k, v=zi, eq=eq, ek=ek, eb=eb, ed=ed, elast=jnp.exp(last), tri=tri, tri_t=tri_t,
                qe=q * eq, ke=k * ek, qb=q * eb, kd=k * ed)


def _hg_specs(tt, order):
    return [pl.BlockSpec((tt, D), lambda i, col=col: (order(i), col)) for col in range(4)]


def _hg_forward(name, z, vec):
    t = z.shape[0]
    tt = _tile(t, ROW_TILE)
    nt, nc = t // tt, tt // HG_CHUNK
    heads = [slice(h * HEAD_DIM, (h + 1) * HEAD_DIM) for h in range(HEADS)]

    def body(zq_ref, zf_ref, zi_ref, zg_ref, vec_ref, p_ref, o_ref, ss_ref, state):
        @pl.when(pl.program_id(0) == 0)
        def _():
            state[...] = jnp.zeros_like(state)

        lb, gain = vec_ref[0:1, :], vec_ref[1:2, :]

        def chunk(ci, carry):
            rows = pl.ds(pl.multiple_of(ci * HG_CHUNK, HG_CHUNK), HG_CHUNK)
            g = _hg_chunk(zq_ref[rows, :], zf_ref[rows, :], zi_ref[rows, :], lb)
            zg = zg_ref[rows, :]
            causal = g["tri"] > 0.0
            qe_m, ke_m, qb_m, kd_m, v_m = (g[n].astype(MXU) for n in ("qe", "ke", "qb", "kd", "v"))
            atts = [_dot_nt(qe_m[:, s], ke_m[:, s]) for s in heads]
            o_parts, p_parts = [], []
            for hd, s in enumerate(heads):
                st = state[hd]
                ss_ref[ci, hd] = st
                att = jnp.where(causal, atts[hd], 0.0)
                o_parts.append(_dot(att, v_m[:, s]) + _dot_nt(qb_m[:, s], st))
                state[hd] = st * g["elast"][:, s] + _dot_tn(v_m[:, s], kd_m[:, s])
            for o, s in zip(o_parts, heads):
                r = lax.rsqrt(jnp.mean(o * o, axis=-1, keepdims=True) + GNORM_EPS)
                p_parts.append((o * r) * gain[:, s])
            o_ref[rows, :] = jnp.concatenate(o_parts, axis=1)
            p_ref[rows, :] = (jnp.concatenate(p_parts, axis=1) * _silu(zg)).astype(p_ref.dtype)
            return carry

        lax.fori_loop(0, nc, chunk, 0, unroll=True)

    return pl.pallas_call(
        body, name=name, grid=(nt,), in_specs=_hg_specs(tt, lambda i: i) + [_group_spec(vec)],
        out_specs=[_rows_spec(tt), _rows_spec(tt), pl.BlockSpec((nc, HEADS, HEAD_DIM, HEAD_DIM), lambda i: (i, 0, 0, 0))],
        out_shape=[jax.ShapeDtypeStruct((t, D), MXU), jax.ShapeDtypeStruct((t, D), F32),
                   jax.ShapeDtypeStruct((t // HG_CHUNK, HEADS, HEAD_DIM, HEAD_DIM), F32)],
        scratch_shapes=[pltpu.VMEM((HEADS, HEAD_DIM, HEAD_DIM), F32)], compiler_params=_params())(z, z, z, z, vec.table)


def _hg_backward(name, dp, z, o, ss, vec):
    t = z.shape[0]
    tt = _tile(t, ROW_TILE)
    nt, nc = t // tt, tt // HG_CHUNK
    rev = lambda i: nt - 1 - i
    heads = [slice(h * HEAD_DIM, (h + 1) * HEAD_DIM) for h in range(HEADS)]
    rows_rev = pl.BlockSpec((tt, D), lambda i: (rev(i), 0))

    def body(dp_ref, zq_ref, zf_ref, zi_ref, zg_ref, o_ref, ss_ref, vec_ref, dz_ref, acc_ref, dstate):
        @pl.when(pl.program_id(0) == 0)
        def _():
            dstate[...] = jnp.zeros_like(dstate)
            acc_ref[...] = jnp.zeros_like(acc_ref)

        lb, gain = vec_ref[0:1, :], vec_ref[1:2, :]

        def chunk(cr, carry):
            ci = nc - 1 - cr
            rows = pl.ds(pl.multiple_of(ci * HG_CHUNK, HG_CHUNK), HG_CHUNK)
            zq, zg = zq_ref[rows, :], zg_ref[rows, :]
            g = _hg_chunk(zq, zf_ref[rows, :], zi_ref[rows, :], lb)
            ov, dpv = o_ref[rows, :], dp_ref[rows, :]
            causal = g["tri"] > 0.0
            silu_g, silu_g_grad = _silu_and_grad(zg)
            don = dpv * silu_g
            dgate = dpv * silu_g_grad
            qe_m, ke_m, qb_m, kd_m, v_m = (g[n].astype(MXU) for n in ("qe", "ke", "qb", "kd", "v"))
            parts = {n: [] for n in ("dzg", "dgain", "dv", "dqe", "dke", "dqb", "dkd", "dlast")}
            dos, atts, datts = [], [], []
            for hd, s in enumerate(heads):
                oh = ov[:, s]
                r = lax.rsqrt(jnp.mean(oh * oh, axis=-1, keepdims=True) + GNORM_EPS)
                on = oh * r
                parts["dzg"].append(dgate[:, s] * (on * gain[:, s]))
                parts["dgain"].append(_row_sum(don[:, s] * on))
                dtmp = don[:, s] * gain[:, s]
                dos.append((r * (dtmp - on * jnp.mean(dtmp * on, axis=-1, keepdims=True))).astype(MXU))
            for hd, s in enumerate(heads):
                atts.append(_dot_nt(qe_m[:, s], ke_m[:, s]))
                datts.append(_dot_nt(dos[hd], v_m[:, s]))
            for hd, s in enumerate(heads):
                do, att, datt = dos[hd], jnp.where(causal, atts[hd], 0.0).astype(MXU), jnp.where(causal, datts[hd], 0.0).astype(MXU)
                st, dst = ss_ref[ci, hd], dstate[hd]
                st_m, dst_m = st.astype(MXU), dst.astype(MXU)
                qe, ke, qb, kd, v = qe_m[:, s], ke_m[:, s], qb_m[:, s], kd_m[:, s], v_m[:, s]
                parts["dv"].append(_dot_tn(att, do) + _dot_nt(kd, dst_m))
                parts["dqe"].append(_dot(datt, ke))
                parts["dke"].append(_dot_tn(datt, qe))
                parts["dqb"].append(_dot(do, st_m))
                parts["dkd"].append(_dot(v, dst_m))
                parts["dlast"].append(g["elast"][:, s] * _row_sum(dst * st))
                dstate[hd] = dst * g["elast"][:, s] + _dot_tn(do, qb)
            whole = {n: jnp.concatenate(p, axis=1) for n, p in parts.items()}
            acc_ref[1:2, :] += whole["dgain"]
            dqe, dke, dqb, dkd = whole["dqe"], whole["dke"], whole["dqb"], whole["dkd"]
            dq = dqe * g["eq"] + dqb * g["eb"]
            dk = dke * g["ek"] + dkd * g["ed"]
            dkdk = dkd * g["kd"]
            db = dqe * qe_m.astype(F32) - dke * ke_m.astype(F32) + dqb * g["qb"] - dkdk
            dlogf = _dot_exact(g["tri_t"], db) + (whole["dlast"] + _row_sum(dkdk))
            dfg = dlogf / g["fg"] - dk
            sg = g["sg"]
            acc_ref[0:1, :] += _row_sum(dfg * (1.0 - sg))
            dz_ref[rows, 0:D] = (dq * g["q_grad"]).astype(dz_ref.dtype)
            dz_ref[rows, D:2 * D] = (dfg * (1.0 - lb) * sg * (1.0 - sg)).astype(dz_ref.dtype)
            dz_ref[rows, 2 * D:3 * D] = whole["dv"].astype(dz_ref.dtype)
            dz_ref[rows, 3 * D:4 * D] = whole["dzg"].astype(dz_ref.dtype)
            return carry

        lax.fori_loop(0, nc, chunk, 0, unroll=True)

    return pl.pallas_call(
        body, name=name, grid=(nt,),
        in_specs=[rows_rev] + _hg_specs(tt, rev) + [rows_rev, pl.BlockSpec((nc, HEADS, HEAD_DIM, HEAD_DIM), lambda i: (rev(i), 0, 0, 0)),
                                                  _group_spec(vec)],
        out_specs=[pl.BlockSpec((tt, 4 * D), lambda i: (rev(i), 0)), _vec_spec()],
        out_shape=[jax.ShapeDtypeStruct((t, 4 * D), MXU), jax.ShapeDtypeStruct((SUBLANES, D), F32)],
        scratch_shapes=[pltpu.VMEM((HEADS, HEAD_DIM, HEAD_DIM), F32)], compiler_params=_params())(dp, z, z, z, z, o, ss, vec.table)


def _mod_forward(name, c_all, mod_w, lower):
    depth, _, sw = mod_w.shape

    def body(c_ref, w_ref, lo_ref, cs_ref, mod_ref, lb_ref):
        cs = _silu(c_ref[...])
        mod_ref[...] = _dot(cs, w_ref[...])

        @pl.when(pl.program_id(0) == 0)
        def _():
            cs_ref[...] = cs
            lo = lo_ref[...]
            e = jnp.exp(lo - jnp.max(lo, axis=0, keepdims=True))
            sm = e / jnp.sum(e, axis=0, keepdims=True)
            lb_ref[0:1, :] = jnp.zeros((1, D), F32)
            for l in range(1, depth):
                lb_ref[l:l + 1, :] = lb_ref[l - 1:l, :] + sm[l:l + 1, :]

    return pl.pallas_call(
        body, name=name, grid=(depth,),
        in_specs=[pl.BlockSpec((N_DEV, D), lambda l: (0, 0)), pl.BlockSpec((None, D, sw), lambda l: (l, 0, 0)),
                  pl.BlockSpec((depth, D), lambda l: (0, 0))],
        out_specs=[pl.BlockSpec((N_DEV, D), lambda l: (0, 0)), pl.BlockSpec((None, N_DEV, sw), lambda l: (l, 0, 0)),
                   pl.BlockSpec((depth, D), lambda l: (0, 0))],
        out_shape=[jax.ShapeDtypeStruct((N_DEV, D), F32), jax.ShapeDtypeStruct((depth, N_DEV, sw), F32),
                   jax.ShapeDtypeStruct((depth, D), F32)], compiler_params=_params())(c_all, mod_w, lower)


def _mod_weight_grad(name, cs_t, dmod):
    depth, pad, sw = dmod.shape

    def body(c_ref, d_ref, g_ref):
        g_ref[...] = _dot(c_ref[...], d_ref[...])

    return pl.pallas_call(
        body, name=name, grid=(depth,),
        in_specs=[pl.BlockSpec((D, pad), lambda l: (0, 0)), pl.BlockSpec((None, pad, sw), lambda l: (l, 0, 0))],
        out_specs=pl.BlockSpec((None, D, sw), lambda l: (l, 0, 0)),
        out_shape=jax.ShapeDtypeStruct((depth, D, sw), F32), compiler_params=_params())(cs_t, dmod)


def _lower_bound_grad(name, lower, dlb):
    depth = lower.shape[0]

    def body(lo_ref, d_ref, out_ref):
        lo = lo_ref[...]
        e = jnp.exp(lo - jnp.max(lo, axis=0, keepdims=True))
        sm = e / jnp.sum(e, axis=0, keepdims=True)
        out_ref[...] = jnp.zeros_like(out_ref)
        dsm = [jnp.zeros((1, D), F32)]
        for l in range(1, depth):
            tail = d_ref[l:l + 1, :]
            for m in range(l + 1, depth):
                tail = tail + d_ref[m:m + 1, :]
            dsm.append(tail)
        inner = sm[1:2, :] * dsm[1]
        for l in range(2, depth):
            inner = inner + sm[l:l + 1, :] * dsm[l]
        for l in range(depth):
            out_ref[l:l + 1, :] = sm[l:l + 1, :] * (dsm[l] - inner)
        for j in range(SUBLANES - depth):
            row = d_ref[depth + j:depth + j + 1, :]
            tot = row[:, 0:HEAD_DIM]
            for hd in range(1, HEADS):
                tot = tot + row[:, hd * HEAD_DIM:(hd + 1) * HEAD_DIM]
            out_ref[depth + j:depth + j + 1, 0:HEAD_DIM] = tot

    return pl.pallas_call(body, name=name, out_shape=jax.ShapeDtypeStruct((SUBLANES, D), F32))(lower, dlb)


def _adamw(name, parts, w, m, v, layer=None, prev=None):
    p, r, c = parts.shape
    tr = _tile(r, max(SUBLANES, ADAMW_STEP_BYTES // (4 * c * (p + 7))))
    stacked = layer is not None
    n_prev = 4 if prev is not None else 0

    def body(*refs):
        parts_ref, w_ref, m_ref, v_ref = refs[:4]
        g_ref, d_ref, m_out, v_out = refs[4 + n_prev:]
        g = parts_ref[0].astype(F32)
        for q in range(1, p):
            g = g + parts_ref[q].astype(F32)
        m2 = ADAM_B1 * m_ref[...] + (1.0 - ADAM_B1) * g
        v2 = ADAM_B2 * v_ref[...] + (1.0 - ADAM_B2) * (g * g)
        m_hat = m2 / (1.0 - ADAM_B1 ** ADAM_STEP)
        v_hat = v2 / (1.0 - ADAM_B2 ** ADAM_STEP)
        g_ref[...] = g
        d_ref[...] = -ADAM_LR * (m_hat / (jnp.sqrt(v_hat) + ADAM_EPS) + ADAM_WD * w_ref[...])
        m_out[...] = m2
        v_out[...] = v2

    if stacked:
        spec = pl.BlockSpec((None, tr, c), lambda i: (layer, i, 0))
    else:
        spec = pl.BlockSpec((tr, c), lambda i: (i, 0))
    return pl.pallas_call(
        body, name=name, grid=(r // tr,),
        in_specs=[pl.BlockSpec((p, tr, c), lambda i: (0, i, 0)), spec, spec, spec] + [ANY] * n_prev, out_specs=[spec] * 4,
        out_shape=[jax.ShapeDtypeStruct(w.shape, F32)] * 4,
        input_output_aliases={4 + q: q for q in range(n_prev)}, compiler_params=_params(),
    )(parts, w, m, v, *(prev or []))


def _hbm(a):
    return pltpu.with_memory_space_constraint(a, pltpu.HBM)


def _pad_rows(a, rows=SUBLANES):
    a = a.reshape(-1, a.shape[-1])
    return jnp.concatenate([a, jnp.zeros((rows - a.shape[0], a.shape[1]), a.dtype)], axis=0) if a.shape[0] < rows else a


def kernel(x, c, mod_w, mod_b, norm_mix, norm_mlp, norm_final, rg_w_in, rg_conv_w, rg_conv_b, rg_w_a, rg_b_a, rg_w_x, rg_b_x, rg_lambda, rg_w_out, hg_w_in, hg_lower_bounds, hg_gnorm, hg_w_out, mlp_w1, mlp_w2, loss_target, m_mod_w, m_mod_b, m_norm_mix, m_norm_mlp, m_norm_final, m_rg_w_in, m_rg_conv_w, m_rg_conv_b, m_rg_w_a, m_rg_b_a, m_rg_w_x, m_rg_b_x, m_rg_lambda, m_rg_w_out, m_hg_w_in, m_hg_lower_bounds, m_hg_gnorm, m_hg_w_out, m_mlp_w1, m_mlp_w2, v_mod_w, v_mod_b, v_norm_mix, v_norm_mlp, v_norm_final, v_rg_w_in, v_rg_conv_w, v_rg_conv_b, v_rg_w_a, v_rg_b_a, v_rg_w_x, v_rg_b_x, v_rg_lambda, v_rg_w_out, v_hg_w_in, v_hg_lower_bounds, v_hg_gnorm, v_hg_w_out, v_mlp_w1, v_mlp_w2):
    me = 4 * lax.axis_index("x") + 2 * lax.axis_index("y") + lax.axis_index("c")
    x0 = x[0]
    target = loss_target[0]
    n_rg, n_hg = rg_w_in.shape[0], hg_w_in.shape[0]
    sw_mod = mod_w.shape[2]

    c_all, cw_all = _all_gather([_pad_rows(c), rg_conv_w.reshape(n_rg * CONV_WIDTH, -1)], "gather_cond")
    c_all = c_all[:, 0, :]
    conv_w = cw_all.transpose(1, 0, 2).reshape(n_rg, CONV_WIDTH, D)
    cs_all, mod_part, lb_all = _mod_forward("mod_forward", c_all, mod_w, hg_lower_bounds)
    (mod_gathered,) = _all_gather([mod_part.reshape(DEPTH * N_DEV, sw_mod)], "gather_mod")

    ids = iter(range(5 * DEPTH))
    shards = []
    for layer in range(DEPTH):
        j = layer // 2
        w_in, w_out = (rg_w_in[j], rg_w_out[j]) if layer % 2 == 0 else (hg_w_in[j], hg_w_out[j])
        shards.append([w_in.astype(MXU), w_out.astype(MXU), mlp_w1[layer].astype(MXU), mlp_w2[layer].astype(MXU)])
    shards, mod_gathered = lax.optimization_barrier((shards, mod_gathered))
    weights = []
    for layer in range(DEPTH):
        g_in, g_out = _all_gather_async(shards[layer][:2], f"gather_mixer_weights_{layer}", next(ids), ["cols", "rows"])
        g_w1, g_w2 = _all_gather_async(shards[layer][2:], f"gather_mlp_weights_{layer}", next(ids), ["cols", "rows"])
        weights.append(dict(w_in=g_in, w_out=g_out, w1=g_w1, w2=g_w2))

    mod_mine = lax.dynamic_index_in_dim(mod_gathered.reshape(N_DEV, DEPTH, N_DEV, sw_mod), me, axis=2, keepdims=False)
    mod = mod_mine.transpose(1, 0, 2).reshape(DEPTH, 6, D) + mod_b.reshape(DEPTH, 6, D)

    pieces, n_groups = [], 0
    placed = [dict() for _ in range(DEPTH)]

    def place(rows):
        nonlocal n_groups
        rows = [r.reshape(-1, D).astype(F32) for r in rows]
        pieces.extend(rows + [jnp.zeros((SUBLANES - sum(r.shape[0] for r in rows), D), F32)])
        n_groups += 1
        return n_groups - 1

    for layer in range(DEPTH):
        j, at = layer // 2, placed[layer]
        at["vec_mix"] = place([norm_mix[layer], mod[layer, 1], mod[layer, 0]])
        at["vec_mlp"] = place([norm_mlp[layer], mod[layer, 4], mod[layer, 3]])
        at["gate_mix"], at["gate_mlp"] = place([mod[layer, 2]]), place([mod[layer, 5]])
        if layer % 2 == 0:
            at["cw"] = place([conv_w[j]])
            at["vec"] = place([rg_conv_b[j], rg_b_a[j], rg_b_x[j], rg_lambda[j]])
        else:
            at["vec"] = place([lb_all[layer], jnp.tile(hg_gnorm[j], HEADS)])
    final_group = place([norm_final])
    table = _hbm(jnp.concatenate(pieces, axis=0))

    saved = []
    xl = x0
    for layer in range(DEPTH):
        j, wt = layer // 2, weights[layer]
        is_rg = layer % 2 == 0
        s = dict(x=xl, **{k: Group(table, g) for k, g in placed[layer].items()})
        s["h"] = _hbm(_norm_mod("norm_mod", xl, s["vec_mix"]))
        if is_rg:
            (s["z"],) = _mm_tokens("rg_in", s["h"], wt["w_in"], False, (F32,))
            s["wa"], s["wx"] = rg_w_a[j].astype(MXU), rg_w_x[j].astype(MXU)
            s["p"], s["hr"] = _rg_forward("rg_forward", s["z"], s["cw"], s["vec"], s["wa"], s["wx"])
        else:
            (s["z"],) = _mm_tokens("hg_in", s["h"], wt["w_in"], False, (F32,))
            s["p"], s["o"], s["ss"] = _hg_forward("hg_forward", s["z"], s["vec"])
        s["y"], s["x1"] = _mm_tokens("mix_out", s["p"], wt["w_out"], False, (MXU, F32), _ep_residual, extras=(xl, s["gate_mix"]))
        s["p"] = _hbm(s["p"])
        s["h2"] = _hbm(_norm_mod("norm_mod", s["x1"], s["vec_mlp"]))
        s["r"], s["ff"], xl = _mlp_forward("mlp_forward", s["h2"], wt["w1"], wt["w2"], s["x1"], s["gate_mlp"])
        saved.append(s)

    dx, head, dff = _loss_head("loss_head", xl, target, Group(table, final_group), saved[-1]["ff"], saved[-1]["gate_mlp"])
    g_mlp = head[GATE_ROW:GATE_ROW + 1]

    results = {}
    big = dict(rg_w_in=(rg_w_in, m_rg_w_in, v_rg_w_in), rg_w_out=(rg_w_out, m_rg_w_out, v_rg_w_out),
               hg_w_in=(hg_w_in, m_hg_w_in, v_hg_w_in), hg_w_out=(hg_w_out, m_hg_w_out, v_hg_w_out),
               mlp_w1=(mlp_w1, m_mlp_w1, v_mlp_w1), mlp_w2=(mlp_w2, m_mlp_w2, v_mlp_w2))

    def update_layer(layer, landed):
        mixer = "rg" if layer % 2 == 0 else "hg"
        l_in, l_out, l_w1, l_w2 = landed
        for nm, parts, idx in ((f"{mixer}_w_in", l_in, layer // 2), (f"{mixer}_w_out", l_out, layer // 2),
                               ("mlp_w1", l_w1, layer), ("mlp_w2", l_w2, layer)):
            w, m, v = big[nm]
            results[nm] = _adamw(f"adamw_{nm}", parts, w, m, v, layer=idx, prev=results.get(nm))

    landed = [None] * DEPTH
    small = [None] * DEPTH
    parity = lax.axis_index("c").astype(jnp.int32).reshape(1)

    def send_chip_sums(layer, grads, from_sibling, anchor):
        sums = _pair_sum("pair_sum", parity, grads, from_sibling)
        sums, anchor = lax.optimization_barrier((sums, anchor))
        landed[layer] = _chip_exchange_async(sums, f"exchange_grads_{layer}", next(ids))
        return anchor

    pending = None
    for layer in reversed(range(DEPTH)):
        j, wt, s = layer // 2, weights[layer], saved[layer]
        is_rg = layer % 2 == 0
        dff = _hbm(dff)
        da, dx1, n_mlp, dyb = _mlp_backward("mlp_backward", dff, s["r"], wt["w1"], wt["w2"], s["x1"], s["vec_mlp"], dx,
                                            s["y"], s["gate_mix"])
        if pending is not None:
            da = send_chip_sums(*pending, da)
        dw2 = _mm_grad("mlp_out_grad", s["r"], dff, False, prologue=_square).reshape(N_DEV, -1, D)
        dw1 = _mm_grad("mlp_in_grad", s["h2"], da, True)
        dyb = _hbm(dyb)
        (dp,) = _mm_tokens("mix_out_t", dyb, wt["w_out"], True, (F32,))
        dw_out = _hbm(_mm_grad("mix_out_grad", s["p"], dyb, False).reshape(N_DEV, -1, D))
        if is_rg:
            dz, dvec, dcw, dwa, dwx = _rg_backward("rg_backward", dp, s["z"], s["hr"], s["cw"], s["vec"], s["wa"], s["wx"])
            gate_grads, dz = lax.optimization_barrier(
                ([dwa.reshape(-1, HEAD_DIM).astype(MXU), dwx.reshape(-1, HEAD_DIM).astype(MXU)], dz))
            gate_parts = _all_gather_async(gate_grads, f"gather_gate_grads_{j}", next(ids), ["stack", "stack"])
            mixer_small = dict(dvec=dvec, dcw=dcw, gate_parts=gate_parts)
            dw_in = _mm_grad("rg_in_grad", s["h"], dz, True)
        else:
            dz, dvec = _hg_backward("hg_backward", dp, s["z"], s["o"], s["ss"], s["vec"])
            mixer_small = dict(dvec=dvec)
            dw_in = _mm_grad("hg_in_grad", s["h"], dz, True)
        grads = [dw_in, dw_out, dw1, dw2]
        pending = (layer, grads, _sibling_send_async(grads, f"pair_grads_{layer}", next(ids)))
        small[layer] = dict(g_mlp=g_mlp, n_mlp=n_mlp, g_mix=n_mlp[GATE_ROW:GATE_ROW + 1], **mixer_small)
        if layer:
            below = saved[layer - 1]
            dx, n_mix, dff = _mm_norm_grad("mixer_in_t", dz, wt["w_in"], s["x"], s["vec_mix"], dx1, below["ff"], below["gate_mlp"])
            g_mlp = n_mix[GATE_ROW:GATE_ROW + 1]
        else:
            dx, n_mix = _mm_norm_grad("mixer_in_t_first", dz, wt["w_in"], s["x"], s["vec_mix"], dx1)
        small[layer]["n_mix"] = n_mix
        if layer + 1 < DEPTH:
            stream = (dx, dff) if layer else (dx,)
            landed[layer + 1], stream = lax.optimization_barrier((landed[layer + 1], stream))
            if "gate_parts" in small[layer + 1]:
                small[layer + 1]["gate_parts"], stream = lax.optimization_barrier((small[layer + 1]["gate_parts"], stream))
            dx, dff = stream if layer else (stream[0], None)
            update_layer(layer + 1, landed[layer + 1])

    dlb_rows = [jnp.zeros((1, D), F32) if l % 2 == 0 else small[l]["dvec"][0:1] for l in range(DEPTH)]
    dgn_rows = [small[2 * j + 1]["dvec"][1:2] for j in range(n_hg)]
    lb_grad = _lower_bound_grad("lower_bound_grad", hg_lower_bounds, _pad_rows(jnp.concatenate(dlb_rows + dgn_rows, axis=0)))
    dmod = jnp.stack([jnp.concatenate([small[l]["n_mix"][2], small[l]["n_mix"][1], small[l]["g_mix"][0],
                                       small[l]["n_mlp"][2], small[l]["n_mlp"][1], small[l]["g_mlp"][0]]) for l in range(DEPTH)])
    groups = [
        dmod.reshape(DEPTH * 6, D),
        jnp.stack([small[l]["n_mix"][0] for l in range(DEPTH)]),
        jnp.stack([small[l]["n_mlp"][0] for l in range(DEPTH)]),
        head[0:1],
        jnp.stack([small[2 * j]["dvec"][0] for j in range(n_rg)]),
        jnp.stack([small[2 * j]["dvec"][1] for j in range(n_rg)]),
        jnp.stack([small[2 * j]["dvec"][2] for j in range(n_rg)]),
        jnp.stack([small[2 * j]["dvec"][3] for j in range(n_rg)]),
        lb_grad[0:DEPTH],
        jnp.concatenate([lb_grad[DEPTH + j:DEPTH + j + 1, 0:HEAD_DIM] for j in range(n_hg)]
                        + [jnp.zeros((1, D - n_hg * HEAD_DIM), F32)], axis=1),
        jnp.concatenate([small[2 * j]["dcw"][0:CONV_WIDTH] for j in range(n_rg)], axis=0),
        head[1:2],
    ]
    params = [mod_b, norm_mix, norm_mlp, norm_final, rg_conv_b, rg_b_a, rg_b_x, rg_lambda, hg_lower_bounds, hg_gnorm]
    moms = [m_mod_b, m_norm_mix, m_norm_mlp, m_norm_final, m_rg_conv_b, m_rg_b_a, m_rg_b_x, m_rg_lambda, m_hg_lower_bounds, m_hg_gnorm]
    vars_ = [v_mod_b, v_norm_mix, v_norm_mlp, v_norm_final, v_rg_conv_b, v_rg_b_a, v_rg_b_x, v_rg_lambda, v_hg_lower_bounds, v_hg_gnorm]
    offsets, rows_of, at = [], [], 0
    for g in groups:
        offsets.append(at)
        rows_of.append(g.shape[0])
        at += -(-g.shape[0] // SUBLANES) * SUBLANES
    packed = jnp.concatenate([_pad_rows(g, -(-g.shape[0] // SUBLANES) * SUBLANES) for g in groups], axis=0)
    (small_parts,) = _all_gather([packed], "gather_small_grads")
    last_layer, last_grads, from_sibling = pending
    from_sibling, small_parts = lax.optimization_barrier((from_sibling, small_parts))
    send_chip_sums(last_layer, last_grads, from_sibling, dx)

    def pack_like(arrs):
        out = []
        for g_rows, off, a in zip(rows_of, offsets, arrs):
            flat = a.reshape(-1)
            flat = jnp.concatenate([flat, jnp.zeros((g_rows * D - flat.shape[0],), F32)])
            out.append(_pad_rows(flat.reshape(g_rows, D), -(-g_rows // SUBLANES) * SUBLANES))
        rest = packed.shape[0] - sum(o.shape[0] for o in out)
        return jnp.concatenate(out + [jnp.zeros((rest, D), F32)], axis=0)

    small_out = _adamw("adamw_small", small_parts, pack_like(params), pack_like(moms), pack_like(vars_))

    def unpack(q, idx, like):
        rows = small_out[q][offsets[idx]:offsets[idx] + rows_of[idx]]
        return rows.reshape(-1)[:like.size].reshape(like.shape)

    loss = jnp.sum(small_out[0][offsets[11]])
    names =["mod_b", "norm_mix", "norm_mlp", "norm_final", "rg_conv_b", "rg_b_a", "rg_b_x", "rg_lambda", "hg_lower_bounds", "hg_gnorm"]
    for idx, (nm, like) in enumerate(zip(names, params)):
        results[nm] = [unpack(q, idx, like) for q in range(4)]

    cw_parts = lax.dynamic_slice_in_dim(small_parts[:, offsets[10]:offsets[10] + n_rg * CONV_WIDTH, :], me * (D // N_DEV), D // N_DEV, axis=2)
    shp = rg_conv_w.shape
    results["rg_conv_w"] = [o.reshape(shp) for o in _adamw(
        "adamw_conv", cw_parts, rg_conv_w.reshape(-1, shp[-1]), m_rg_conv_w.reshape(-1, shp[-1]), v_rg_conv_w.reshape(-1, shp[-1]))]
    shp = rg_w_a.shape
    stacked = (n_rg, HEADS * HEAD_DIM, HEAD_DIM)
    for nm, which, (w, m, v) in (("rg_w_a", 0, (rg_w_a, m_rg_w_a, v_rg_w_a)), ("rg_w_x", 1, (rg_w_x, m_rg_w_x, v_rg_w_x))):
        out = None
        for j in reversed(range(n_rg)):
            out = _adamw("adamw_gate", small[2 * j]["gate_parts"][which], w.reshape(stacked), m.reshape(stacked),
                         v.reshape(stacked), layer=j, prev=out)
        results[nm] = [o.reshape(shp) for o in out]

    dmod_all = small_parts[:, 0:DEPTH * 6, :].reshape(N_DEV, DEPTH, 6 * D)
    dmod_cols = lax.dynamic_slice_in_dim(dmod_all, me * sw_mod, sw_mod, axis=2).transpose(1, 0, 2)
    pad = HEAD_DIM - N_DEV
    dmod_pad = jnp.concatenate([dmod_cols, jnp.zeros((DEPTH, pad, sw_mod), F32)], axis=1).astype(MXU)
    cs_t = jnp.concatenate([cs_all.T, jnp.zeros((D, pad), F32)], axis=1).astype(MXU)
    g_mod_w = _mod_weight_grad("mod_weight_grad", cs_t, dmod_pad)
    results["mod_w"] = [o.reshape(mod_w.shape) for o in _adamw(
        "adamw_mod", g_mod_w.reshape(1, -1, sw_mod), mod_w.reshape(-1, sw_mod), m_mod_w.reshape(-1, sw_mod), v_mod_w.reshape(-1, sw_mod))]

    update_layer(0, landed[0])

    order = ["mod_w", "mod_b", "norm_mix", "norm_mlp", "norm_final", "rg_w_in", "rg_conv_w", "rg_conv_b", "rg_w_a", "rg_b_a", "rg_w_x",
             "rg_b_x", "rg_lambda", "rg_w_out", "hg_w_in", "hg_lower_bounds", "hg_gnorm", "hg_w_out", "mlp_w1", "mlp_w2"]
    return (loss, dx[None], *[results[n][0] for n in order], *[results[n][1] for n in order],
            *[results[n][2] for n in order], *[results[n][3] for n in order])
```

```python
import collections

import jax
import jax.numpy as jnp
from jax import lax
from jax.experimental import pallas as pl
from jax.experimental.pallas import tpu as pltpu
from jax.experimental.pallas import tpu_sc as plsc

F32 = jnp.float32
MXU = jnp.bfloat16

N_DEV = 8
D = 1024
DEPTH = 4
HEADS = 8
HEAD_DIM = 128
CONV_WIDTH = 4
LRU_C = 8.0
HG_CHUNK = 64
NORM_EPS = 1e-6
GNORM_EPS = 1e-5
ADAM_LR = 0.001
ADAM_B1 = 0.9
ADAM_B2 = 0.999
ADAM_EPS = 1e-08
ADAM_WD = 0.01
ADAM_STEP = 10
GELU_C = 0.7978845608028654
GELU_K = 0.044715
EXP_CLAMP = 80.0
SUBLANES = 8
VMEM_LIMIT = 48 * 1024 * 1024
ROW_TILE = 256
RG_ROWS = 256
MM_TILE = 1024
MM_TOKENS = 512
MLP_TOKENS = 256
ADAMW_STEP_BYTES = 8 * 1024 * 1024

MESH = pl.DeviceIdType.MESH
ANY = pl.BlockSpec(memory_space=pl.ANY)


def _params():
    return pltpu.CompilerParams(vmem_limit_bytes=VMEM_LIMIT)


def _tile(n, target):
    if n <= target:
        return n
    t = target // SUBLANES * SUBLANES
    while n % t:
        t -= SUBLANES
    return t


def _sigmoid(x):
    return 1.0 / (1.0 + jnp.exp(-x))


def _silu(x):
    return x * _sigmoid(x)


def _silu_and_grad(x):
    s = _sigmoid(x)
    return x * s, s * (1.0 + x * (1.0 - s))


def _gelu(y):
    return 0.5 * y * (1.0 + jnp.tanh(GELU_C * (y + GELU_K * y * y * y)))


def _gelu_and_grad(y):
    y2 = y * y
    th = jnp.tanh(GELU_C * (y + GELU_K * y2 * y))
    half = 0.5 * (1.0 + th)
    return y * half, half + 0.5 * y * (1.0 - th * th) * GELU_C * (1.0 + 3.0 * GELU_K * y2)


def _dot(a, b):
    return lax.dot_general(a.astype(MXU), b.astype(MXU), (((1,), (0,)), ((), ())), preferred_element_type=F32)


def _dot_nt(a, b):
    return lax.dot_general(a.astype(MXU), b.astype(MXU), (((1,), (1,)), ((), ())), preferred_element_type=F32)


def _dot_tn(a, b):
    return lax.dot_general(a.astype(MXU), b.astype(MXU), (((0,), (0,)), ((), ())), preferred_element_type=F32)


def _dot_exact(tri, x):
    t = tri.astype(MXU)
    hi = x.astype(MXU)
    r1 = x - hi.astype(F32)
    mid = r1.astype(MXU)
    lo = (r1 - mid.astype(F32)).astype(MXU)
    dn = (((1,), (0,)), ((), ()))
    return (lax.dot_general(t, hi, dn, preferred_element_type=F32) + lax.dot_general(t, mid, dn, preferred_element_type=F32)
            + lax.dot_general(t, lo, dn, preferred_element_type=F32))


def _row_sum(v):
    return jnp.sum(v, axis=0, keepdims=True)


def _handshake(partners):
    barrier = pltpu.get_barrier_semaphore()
    for p in partners:
        pl.semaphore_signal(barrier, inc=1, device_id=p, device_id_type=MESH)
    pl.semaphore_wait(barrier, len(partners))


def _gather_body(n, per_array_sems, handshake, layouts):
    def body(*refs):
        ins, outs = refs[:n], refs[n:2 * n]
        send_sems, recv_sems, local_sems = refs[2 * n:]
        x, y, c = lax.axis_index("x"), lax.axis_index("y"), lax.axis_index("c")
        me, sibling = (x, y, c), (x, y, 1 - c)
        chips = [(1 - x, y), (x, 1 - y), (1 - x, 1 - y)]
        if handshake:
            _handshake([sibling] + [(*chip, c) for chip in chips])

        def sem(sems, a, k):
            return sems.at[a, k] if per_array_sems else sems.at[k]

        def slot(a, p):
            block = 4 * p[0] + 2 * p[1] + p[2]
            r, c_ = ins[a].shape
            if layouts[a] == "rows":
                return outs[a].at[pl.ds(block * r, r), :]
            if layouts[a] == "cols":
                return outs[a].at[:, pl.ds(block * c_, c_)]
            return outs[a].at[block]

        def copy(a, k, block, to, src=None):
            return pltpu.make_async_remote_copy(
                src_ref=slot(a, block) if src is None else src, dst_ref=slot(a, block),
                send_sem=sem(send_sems, a, k), recv_sem=sem(recv_sems, a, k), device_id=to, device_id_type=MESH)

        mine = [pltpu.make_async_copy(ins[a], slot(a, me), local_sems.at[a if per_array_sems else 0]) for a in range(n)]
        for cp in mine:
            cp.start()
        first = []
        for a in range(n):
            first.append(copy(a, 0, me, sibling, src=ins[a]))
            first += [copy(a, 1 + j, me, (*chip, c), src=ins[a]) for j, chip in enumerate(chips)]
        for cp in first:
            cp.start()
        passed = []
        for j, chip in enumerate(chips):
            for a in range(n):
                copy(a, 1 + j, (*chip, c), me).wait_recv()
            for a in range(n):
                cp = copy(a, 4 + j, (*chip, c), sibling)
                cp.start()
                passed.append(cp)
        for a in range(n):
            copy(a, 0, sibling, me).wait_recv()
        for j, chip in enumerate(chips):
            for a in range(n):
                copy(a, 4 + j, (*chip, 1 - c), me).wait_recv()
        for cp in first + passed:
            cp.wait_send()
        for cp in mine:
            cp.wait()

    return body


def _sibling_send_body(n, per_array_sems, handshake):
    def body(*refs):
        ins, outs = refs[:n], refs[n:2 * n]
        send_sems, recv_sems, _ = refs[2 * n:]
        x, y, c = lax.axis_index("x"), lax.axis_index("y"), lax.axis_index("c")
        sibling = (x, y, 1 - c)
        if handshake:
            _handshake([sibling])

        def sem(sems, a):
            return sems.at[a, 0] if per_array_sems else sems.at[0]

        copies = [pltpu.make_async_remote_copy(
            src_ref=ins[a].at[2 * q + 1 - c], dst_ref=outs[a].at[q], send_sem=sem(send_sems, a), recv_sem=sem(recv_sems, a),
            device_id=sibling, device_id_type=MESH) for a in range(n) for q in range(4)]
        for cp in copies:
            cp.start()
        for cp in copies:
            cp.wait_recv()
        for cp in copies:
            cp.wait_send()

    return body


def _chip_exchange_body(n, per_array_sems, handshake):
    def body(*refs):
        ins, outs = refs[:n], refs[n:2 * n]
        send_sems, recv_sems, local_sems = refs[2 * n:]
        x, y, c = lax.axis_index("x"), lax.axis_index("y"), lax.axis_index("c")
        my_chip = 2 * x + y
        chips = [(1 - x, y), (x, 1 - y), (1 - x, 1 - y)]
        if handshake:
            _handshake([(*chip, c) for chip in chips])

        def sem(sems, a, k):
            return sems.at[a, k] if per_array_sems else sems.at[k]

        def copy(a, k, landing):
            px, py = chips[k]
            return pltpu.make_async_remote_copy(
                src_ref=ins[a].at[2 * px + py], dst_ref=outs[a].at[landing], send_sem=sem(send_sems, a, k),
                recv_sem=sem(recv_sems, a, k), device_id=(px, py, c), device_id_type=MESH)

        mine = [pltpu.make_async_copy(ins[a].at[my_chip], outs[a].at[my_chip], local_sems.at[a if per_array_sems else 0])
                for a in range(n)]
        for cp in mine:
            cp.start()
        sent = [copy(a, k, my_chip) for a in range(n) for k in range(3)]
        for cp in sent:
            cp.start()
        for a in range(n):
            for k, (px, py) in enumerate(chips):
                copy(a, k, 2 * px + py).wait_recv()
        for cp in sent:
            cp.wait_send()
        for cp in mine:
            cp.wait()

    return body


def _all_gather(arrs, name):
    n = len(arrs)
    return pl.pallas_call(
        _gather_body(n, True, False, ["stack"] * n), name=name, in_specs=[ANY] * n, out_specs=[ANY] * n,
        out_shape=[jax.ShapeDtypeStruct((N_DEV,) + a.shape, a.dtype) for a in arrs],
        scratch_shapes=[pltpu.SemaphoreType.DMA((n, 7)), pltpu.SemaphoreType.DMA((n, 7)), pltpu.SemaphoreType.DMA((n,))],
    )(*arrs)


def _on_sequencer(body, arrs, out_type, name, collective_id):
    return pl.kernel(
        body, name=name, out_type=out_type, mesh=plsc.ScalarSubcoreMesh(axis_name="sequencer", num_cores=1),
        scratch_types=[pltpu.SemaphoreType.DMA((RING_SEMS,)), pltpu.SemaphoreType.DMA((RING_SEMS,)),
                       pltpu.SemaphoreType.DMA((1,))],
        compiler_params=pltpu.CompilerParams(collective_id=collective_id))(*arrs)


RING_SEMS = 8


def _ring_gather_body(n, per_array_sems, handshake, layouts):
    def body(*refs):
        ins, outs = refs[:n], refs[n:2 * n]
        send_sems, recv_sems, local_sems = refs[2 * n:]
        x, y, c = lax.axis_index("x"), lax.axis_index("y"), lax.axis_index("c")
        me, sibling = (x, y, c), (x, y, 1 - c)
        over_x, over_y, diagonal = (1 - x, y, c), (x, 1 - y, c), (1 - x, 1 - y, c)
        if handshake:
            _handshake([sibling, over_x, over_y])

        def sem(sems, a, k):
            return sems.at[a, k] if per_array_sems else sems.at[k]

        def slot(a, p, half=None):
            block = 4 * p[0] + 2 * p[1] + p[2]
            r, c_ = ins[a].shape
            first, count = (0, r) if half is None else (half * (r // 2), r // 2)
            if layouts[a] == "rows":
                return outs[a].at[pl.ds(block * r + first, count), :]
            if layouts[a] == "cols":
                return outs[a].at[pl.ds(first, count), pl.ds(block * c_, c_)]
            return outs[a].at[block, pl.ds(first, count), :]

        def copy(a, k, block, to, half=None, src=None):
            return pltpu.make_async_remote_copy(
                src_ref=slot(a, block, half) if src is None else src, dst_ref=slot(a, block, half),
                send_sem=sem(send_sems, a, k), recv_sem=sem(recv_sems, a, k), device_id=to, device_id_type=MESH)

        def start(copies):
            for cp in copies:
                cp.start()
            return copies

        def arrived(k, block, half=None):
            for a in range(n):
                copy(a, k, block, me, half).wait_recv()

        arrays = range(n)
        mine = start([pltpu.make_async_copy(ins[a], slot(a, me), local_sems.at[a if per_array_sems else 0]) for a in arrays])
        sent = start([copy(a, k, me, to, src=ins[a]) for a in arrays for k, to in ((0, sibling), (1, over_x), (2, over_y))])
        arrived(1, over_x)
        sent += start([copy(a, 3, over_x, over_y, half=0) for a in arrays] + [copy(a, 5, over_x, sibling) for a in arrays])
        arrived(2, over_y)
        sent += start([copy(a, 4, over_y, over_x, half=1) for a in arrays] + [copy(a, 6, over_y, sibling) for a in arrays])
        arrived(3, diagonal, half=0)
        arrived(4, diagonal, half=1)
        sent += start([copy(a, 7, diagonal, sibling) for a in arrays])
        arrived(0, sibling)
        for k, chip in ((5, over_x), (6, over_y), (7, diagonal)):
            arrived(k, (chip[0], chip[1], 1 - c))
        for cp in sent:
            cp.wait_send()
        for cp in mine:
            cp.wait()

    return body


def _all_gather_async(arrs, name, collective_id, layouts):
    shape = dict(stack=lambda r, c: (N_DEV, r, c), rows=lambda r, c: (N_DEV * r, c), cols=lambda r, c: (r, N_DEV * c))
    out_type = [jax.ShapeDtypeStruct(shape[lay](*a.shape), a.dtype) for a, lay in zip(arrs, layouts)]
    return _on_sequencer(_ring_gather_body(len(arrs), False, True, layouts), arrs, out_type, name, collective_id)


def _sibling_send_async(arrs, name, collective_id):
    out_type = [jax.ShapeDtypeStruct((N_DEV // 2,) + a.shape[1:], a.dtype) for a in arrs]
    return _on_sequencer(_sibling_send_body(len(arrs), False, True), arrs, out_type, name, collective_id)


def _chip_exchange_async(arrs, name, collective_id):
    out_type = [jax.ShapeDtypeStruct(a.shape, a.dtype) for a in arrs]
    return _on_sequencer(_chip_exchange_body(len(arrs), False, True), arrs, out_type, name, collective_id)


def _pair_sum(name, parity, mine, theirs):
    n = len(mine)

    def body(par_ref, *refs):
        for a in range(n):
            refs[2 * n + a][...] = (refs[a][...].astype(F32) + refs[n + a][...].astype(F32)).astype(refs[2 * n + a].dtype)

    def block(a):
        return (None,) + a.shape[1:]

    grid_spec = pltpu.PrefetchScalarGridSpec(
        num_scalar_prefetch=1, grid=(N_DEV // 2,),
        in_specs=[pl.BlockSpec(block(a), lambda q, par: (2 * q + par[0], 0, 0)) for a in mine]
        + [pl.BlockSpec(block(a), lambda q, par: (q, 0, 0)) for a in theirs],
        out_specs=[pl.BlockSpec(block(a), lambda q, par: (q, 0, 0)) for a in theirs])
    return pl.pallas_call(body, name=name, grid_spec=grid_spec, out_shape=[jax.ShapeDtypeStruct(a.shape, a.dtype) for a in theirs],
                          compiler_params=_params())(parity, *mine, *theirs)


NN = (((1,), (0,)), ((), ()))
NT = (((1,), (1,)), ((), ()))
TN = (((0,), (0,)), ((), ()))


def _matmul(name, a, b, dims, grid, a_spec, b_spec, outs, epilogue, extras=(), prologue=None):
    n_in = 2 + len(extras)
    n_out = len(outs)

    def body(*refs):
        a_ref, b_ref = refs[0], refs[1]
        ex, out_refs = refs[2:n_in], refs[n_in:n_in + n_out]
        a_tile = a_ref[...] if prologue is None else prologue(a_ref[...])
        epilogue(lax.dot_general(a_tile, b_ref[...], dims, preferred_element_type=F32), ex, out_refs)

    return pl.pallas_call(
        body, name=name, grid=grid, in_specs=[a_spec, b_spec] + [s for _, s in extras], out_specs=[s for _, _, s in outs],
        out_shape=[jax.ShapeDtypeStruct(sh, dt) for sh, dt, _ in outs], compiler_params=_params(),
    )(a, b, *[e for e, _ in extras])


def _square(tile):
    return tile * tile


def _ep_store(acc, ex, outs):
    outs[0][...] = acc.astype(outs[0].dtype)


def _ep_residual(acc, ex, outs):
    outs[0][...] = acc.astype(outs[0].dtype)
    outs[1][...] = ex[0][...] + ex[1][0:1, :] * acc


def _mm_tokens(name, a, w, transposed, out_dtypes, epilogue=_ep_store, extras=(), prologue=None, rows=MM_TOKENS):
    m, n = a.shape[0], w.shape[0 if transposed else 1]
    tm = _tile(m, rows)
    rows_spec = lambda width: pl.BlockSpec((tm, width), lambda i, j, kk: (i, 0))
    whole = lambda arr: pl.BlockSpec(arr.shape, lambda i, j, kk: (0, 0))
    return _matmul(name, a, w, NT if transposed else NN, (m // tm, 1, 1), rows_spec(a.shape[1]), whole(w),
                   [((m, n), dt, rows_spec(n)) for dt in out_dtypes], epilogue,
                   extras=[(e.table, _group_spec(e, 3)) if isinstance(e, Group) else (e, rows_spec(n)) for e in extras],
                   prologue=prologue)


def _resident(w):
    return pl.BlockSpec(w.shape, lambda i: (0, 0), pipeline_mode=pl.Buffered(1))


def _mlp_forward(name, h, w1, w2, x, gate):
    m, d = x.shape
    f = w1.shape[1]
    tm = _tile(m, MLP_TOKENS)
    rows = lambda width: pl.BlockSpec((tm, width), lambda i: (i, 0))

    def body(h_ref, w1_ref, w2_ref, x_ref, g_ref, r_ref, ff_ref, out_ref):
        r = jnp.maximum(lax.dot_general(h_ref[...], w1_ref[...], NN, preferred_element_type=F32), 0.0).astype(r_ref.dtype)
        r_ref[...] = r
        ff = lax.dot_general(r * r, w2_ref[...], NN, preferred_element_type=F32)
        ff_ref[...] = ff.astype(ff_ref.dtype)
        out_ref[...] = x_ref[...] + g_ref[0:1, :] * ff

    return pl.pallas_call(
        body, name=name, grid=(m // tm,), in_specs=[rows(d), _resident(w1), _resident(w2), rows(d), _group_spec(gate)],
        out_specs=[rows(f), rows(d), rows(d)],
        out_shape=[jax.ShapeDtypeStruct((m, f), MXU), jax.ShapeDtypeStruct((m, d), MXU), jax.ShapeDtypeStruct((m, d), F32)],
        compiler_params=_params())(h, w1, w2, x, gate.table)


def _mlp_backward(name, dff, r, w1, w2, x, vec, dres, y, gate):
    m, d = dff.shape
    f = r.shape[1]
    tm = _tile(m, MLP_TOKENS)
    nt = m // tm
    rows = lambda width: pl.BlockSpec((tm, width), lambda i: (i, 0))

    def body(dff_ref, r_ref, w1_ref, w2_ref, x_ref, v_ref, dres_ref, y_ref, g_ref, da_ref, dx_ref, acc_ref, dy_ref):
        ds = lax.dot_general(dff_ref[...], w2_ref[...], NT, preferred_element_type=F32)
        da = (ds * (2.0 * r_ref[...].astype(F32))).astype(da_ref.dtype)
        da_ref[...] = da
        dh = lax.dot_general(da, w1_ref[...], NT, preferred_element_type=F32)
        _norm_grad_step(nt, dh, x_ref, v_ref, dres_ref, dx_ref, acc_ref, (y_ref, g_ref, dy_ref))

    return pl.pallas_call(
        body, name=name, grid=(nt,),
        in_specs=[rows(d), rows(f), _resident(w1), _resident(w2), rows(d), _group_spec(vec), rows(d), rows(d), _group_spec(gate)],
        out_specs=[rows(f), rows(d), _vec_spec(), rows(d)],
        out_shape=[jax.ShapeDtypeStruct((m, f), MXU), jax.ShapeDtypeStruct((m, d), F32),
                   jax.ShapeDtypeStruct((SUBLANES, d), F32), jax.ShapeDtypeStruct((m, d), MXU)],
        compiler_params=_params())(dff, r, w1, w2, x, vec.table, dres, y, gate.table)


def _mm_norm_grad(name, a, w, x, vec, dres, y=None, gate=None):
    m = a.shape[0]
    tm = _tile(m, MM_TOKENS)
    nt = m // tm
    branch = y is not None
    rows = lambda width: pl.BlockSpec((tm, width), lambda i: (i, 0))

    def body(a_ref, w_ref, x_ref, v_ref, dres_ref, *rest):
        dh = lax.dot_general(a_ref[...], w_ref[...], NT, preferred_element_type=F32)
        dx_ref, acc_ref = rest[2 * branch], rest[2 * branch + 1]
        _norm_grad_step(nt, dh, x_ref, v_ref, dres_ref, dx_ref, acc_ref, (rest[0], rest[1], rest[4]) if branch else None)

    return pl.pallas_call(
        body, name=name, grid=(nt,),
        in_specs=[rows(a.shape[1]), _resident(w), rows(D), _group_spec(vec), rows(D)] + ([rows(D), _group_spec(gate)] if branch else []),
        out_specs=[rows(D), _vec_spec()] + ([rows(D)] if branch else []),
        out_shape=[jax.ShapeDtypeStruct((m, D), F32), jax.ShapeDtypeStruct((SUBLANES, D), F32)]
        + ([jax.ShapeDtypeStruct((m, D), MXU)] if branch else []),
        compiler_params=_params())(a, w, x, vec.table, dres, *([y, gate.table] if branch else []))


def _mm_grad(name, a, b, shard_cols, prologue=None):
    t, m = a.shape
    n = b.shape[1]
    tm = _tile(m, MM_TILE)
    if shard_cols:
        tn = n // N_DEV
        out = ((N_DEV, m, tn), MXU, pl.BlockSpec((None, tm, tn), lambda i, j, kk: (j, i, 0)))
    else:
        tn = _tile(n, MM_TILE)
        out = ((m, n), MXU, pl.BlockSpec((tm, tn), lambda i, j, kk: (i, j)))
    return _matmul(name, a, b, TN, (m // tm, n // tn, 1), pl.BlockSpec((t, tm), lambda i, j, kk: (0, i)),
                   pl.BlockSpec((t, tn), lambda i, j, kk: (0, j)), [out], _ep_store, prologue=prologue)[0]


def _rows_spec(tt, width=D):
    return pl.BlockSpec((tt, width), lambda i: (i, 0))


def _vec_spec(rows=SUBLANES, width=D):
    return pl.BlockSpec((rows, width), lambda i: (0, 0))


Group = collections.namedtuple("Group", ["table", "index"])


def _group_spec(group, grid_rank=1):
    if grid_rank == 1:
        return pl.BlockSpec((SUBLANES, D), lambda i: (group.index, 0))
    return pl.BlockSpec((SUBLANES, D), lambda i, j, kk: (group.index, 0))


def _norm_mod(name, x, vec):
    t = x.shape[0]
    tt = _tile(t, ROW_TILE)

    def body(x_ref, v_ref, h_ref):
        xv = x_ref[...]
        r = lax.rsqrt(jnp.mean(xv * xv, axis=-1, keepdims=True) + NORM_EPS)
        h = (xv * r) * v_ref[0:1, :]
        h_ref[...] = (h * (1.0 + v_ref[1:2, :]) + v_ref[2:3, :]).astype(h_ref.dtype)

    return pl.pallas_call(body, name=name, grid=(t // tt,), in_specs=[_rows_spec(tt), _group_spec(vec)], out_specs=_rows_spec(tt),
                          out_shape=jax.ShapeDtypeStruct((t, D), MXU))(x, vec.table)


GATE_ROW = 3


def _branch_grad(dx, y_ref, g_ref, dy_ref, acc_ref):
    dy_ref[...] = (dx * g_ref[0:1, :]).astype(dy_ref.dtype)
    acc_ref[GATE_ROW:GATE_ROW + 1, :] += _row_sum(dx * y_ref[...])


def _norm_grad_step(nt, dh, x_ref, v_ref, dres_ref, dx_ref, acc_ref, branch):
    i = pl.program_id(0)

    @pl.when(i == 0)
    def _():
        acc_ref[...] = jnp.zeros_like(acc_ref)

    xv = x_ref[...]
    r = lax.rsqrt(jnp.mean(xv * xv, axis=-1, keepdims=True) + NORM_EPS)
    xn = xv * r
    w = v_ref[0:1, :] * (1.0 + v_ref[1:2, :])
    acc_ref[0:1, :] += _row_sum(dh * xn)
    acc_ref[2:3, :] += _row_sum(dh)
    dxn = dh * w
    dx = dres_ref[...] + r * (dxn - xn * jnp.mean(dxn * xn, axis=-1, keepdims=True))
    dx_ref[...] = dx
    if branch is not None:
        _branch_grad(dx, branch[0], branch[1], branch[2], acc_ref)

    @pl.when(i == nt - 1)
    def _():
        dw = acc_ref[0:1, :]
        acc_ref[1:2, :] = dw * v_ref[0:1, :]
        acc_ref[0:1, :] = dw * (1.0 + v_ref[1:2, :])


def _loss_head(name, x, target, vec, y, gate):
    t = x.shape[0]
    tt = _tile(t, ROW_TILE)

    def body(x_ref, t_ref, v_ref, y_ref, g_ref, dx_ref, acc_ref, dy_ref):
        @pl.when(pl.program_id(0) == 0)
        def _():
            acc_ref[...] = jnp.zeros_like(acc_ref)

        xv = x_ref[...]
        r = lax.rsqrt(jnp.mean(xv * xv, axis=-1, keepdims=True) + NORM_EPS)
        xn = xv * r
        gain = v_ref[0:1, :]
        err = xn * gain - t_ref[...]
        acc_ref[1:2, :] += _row_sum(err * err) * (0.5 / D)
        dout = err * (1.0 / D)
        acc_ref[0:1, :] += _row_sum(dout * xn)
        dxn = dout * gain
        dx = r * (dxn - xn * jnp.mean(dxn * xn, axis=-1, keepdims=True))
        dx_ref[...] = dx
        _branch_grad(dx, y_ref, g_ref, dy_ref, acc_ref)

    return pl.pallas_call(
        body, name=name, grid=(t // tt,),
        in_specs=[_rows_spec(tt), _rows_spec(tt), _group_spec(vec), _rows_spec(tt), _group_spec(gate)],
        out_specs=[_rows_spec(tt), _vec_spec(), _rows_spec(tt)],
        out_shape=[jax.ShapeDtypeStruct((t, D), F32), jax.ShapeDtypeStruct((SUBLANES, D), F32),
                   jax.ShapeDtypeStruct((t, D), MXU)])(x, target, vec.table, y, gate.table)


def _shift_down(x, halo, k):
    y = pltpu.roll(x, k, 0)
    top = jnp.where(lax.broadcasted_iota(jnp.int32, halo.shape, 0) < k, pltpu.roll(halo, k, 0), y[0:SUBLANES, :])
    return jnp.concatenate([top, y[SUBLANES:, :]], axis=0)


def _shift_up(x, halo, k):
    n = x.shape[0]
    y = pltpu.roll(x, n - k, 0)
    bottom = jnp.where(lax.broadcasted_iota(jnp.int32, halo.shape, 0) >= SUBLANES - k, pltpu.roll(halo, SUBLANES - k, 0),
                       y[n - SUBLANES:, :])
    return jnp.concatenate([y[:n - SUBLANES, :], bottom], axis=0)


def _rg_gates(xb, halo, cw_ref, vec_ref, wa_ref, wx_ref, at_start):
    shifted = [xb] + [_shift_down(xb, halo, k) for k in range(1, CONV_WIDTH)]
    xc = vec_ref[0:1, :] + shifted[0] * cw_ref[CONV_WIDTH - 1:CONV_WIDTH, :]
    for k in range(1, CONV_WIDTH):
        xc = xc + shifted[k] * cw_ref[CONV_WIDTH - 1 - k:CONV_WIDTH - k, :]
    heads = [slice(h * HEAD_DIM, (h + 1) * HEAD_DIM) for h in range(HEADS)]
    pa = jnp.concatenate([_dot(xc[:, s], wa_ref[h]) for h, s in enumerate(heads)], axis=1) + vec_ref[1:2, :]
    px = jnp.concatenate([_dot(xc[:, s], wx_ref[h]) for h, s in enumerate(heads)], axis=1) + vec_ref[2:3, :]
    ra, ia = _sigmoid(pa), _sigmoid(px)
    nl = -vec_ref[3:4, :]
    sp = jnp.maximum(nl, 0.0) + jnp.log(1.0 + jnp.exp(-jnp.abs(nl)))
    log_a = (-LRU_C) * ra * sp
    a = jnp.exp(log_a)
    th = jnp.tanh(log_a)
    is_t0 = jnp.logical_and(lax.broadcasted_iota(jnp.int32, xb.shape, 0) == 0, at_start)
    mult = jnp.where(is_t0, 1.0, jnp.sqrt(-2.0 * th / (1.0 - th)))
    return dict(shifted=shifted, xc=xc, ra=ra, ia=ia, sp=sp, a=a, mult=mult, is_t0=is_t0, heads=heads)


def _rg_specs(tt, nt, order, cw, vec):
    blk = tt // SUBLANES
    return dict(
        x=pl.BlockSpec((tt, D), lambda i: (order(i), 0)), y=pl.BlockSpec((tt, D), lambda i: (order(i), 1)),
        halo=pl.BlockSpec((SUBLANES, D), lambda i: (jnp.maximum(order(i) * blk - 1, 0), 0)),
        cw=_group_spec(cw), vec=_group_spec(vec), w=pl.BlockSpec((HEADS, HEAD_DIM, HEAD_DIM), lambda i: (0, 0, 0)))


def _rg_forward(name, z, cw, vec, wa, wx):
    t = z.shape[0]
    tt = _tile(t, RG_ROWS)
    nt = t // tt
    sp = _rg_specs(tt, nt, lambda i: i, cw, vec)

    def body(zx_ref, zy_ref, halo_ref, cw_ref, vec_ref, wa_ref, wx_ref, p_ref, h_ref, a_s, u_s, carry):
        i = pl.program_id(0)

        @pl.when(i == 0)
        def _():
            carry[...] = jnp.zeros_like(carry)

        halo = jnp.where(i > 0, halo_ref[...], 0.0)
        g = _rg_gates(zx_ref[...], halo, cw_ref, vec_ref, wa_ref, wx_ref, i == 0)
        a_s[...] = g["a"]
        u_s[...] = g["mult"] * (g["ia"] * g["xc"])

        def group(gi, h):
            rows = pl.ds(pl.multiple_of(gi * SUBLANES, SUBLANES), SUBLANES)
            a8, u8 = a_s[rows, :], u_s[rows, :]
            out = []
            for j in range(SUBLANES):
                h = a8[j:j + 1, :] * h + u8[j:j + 1, :]
                out.append(h)
            h_ref[rows, :] = jnp.concatenate(out, axis=0)
            return h

        carry[0:1, :] = lax.fori_loop(0, tt // SUBLANES, group, carry[0:1, :])
        p_ref[...] = (h_ref[...] * _gelu(zy_ref[...])).astype(p_ref.dtype)

    return pl.pallas_call(
        body, name=name, grid=(nt,), in_specs=[sp["x"], sp["y"], sp["halo"], sp["cw"], sp["vec"], sp["w"], sp["w"]],
        out_specs=[_rows_spec(tt), _rows_spec(tt)],
        out_shape=[jax.ShapeDtypeStruct((t, D), MXU), jax.ShapeDtypeStruct((t, D), F32)],
        scratch_shapes=[pltpu.VMEM((tt, D), F32), pltpu.VMEM((tt, D), F32), pltpu.VMEM((SUBLANES, D), F32)],
        compiler_params=_params())(z, z, z, cw.table, vec.table, wa, wx)


def _rg_backward(name, dp, z, h, cw, vec, wa, wx):
    t = z.shape[0]
    tt = _tile(t, RG_ROWS)
    nt = t // tt
    rev = lambda i: nt - 1 - i
    sp = _rg_specs(tt, nt, rev, cw, vec)
    rows_rev = pl.BlockSpec((tt, D), lambda i: (rev(i), 0))

    def body(dp_ref, zx_ref, zy_ref, halo_ref, h_ref, hhalo_ref, cw_ref, vec_ref, wa_ref, wx_ref,
             dz_ref, dvec_ref, dcw_ref, dwa_ref, dwx_ref, a_s, d_s, carry, nxt):
        i = pl.program_id(0)
        j = rev(i)

        @pl.when(i == 0)
        def _():
            carry[...] = jnp.zeros_like(carry)
            nxt[...] = jnp.zeros_like(nxt)
            dvec_ref[...] = jnp.zeros_like(dvec_ref)
            dcw_ref[...] = jnp.zeros_like(dcw_ref)
            dwa_ref[...] = jnp.zeros_like(dwa_ref)
            dwx_ref[...] = jnp.zeros_like(dwx_ref)

        halo = jnp.where(j > 0, halo_ref[...], 0.0)
        g = _rg_gates(zx_ref[...], halo, cw_ref, vec_ref, wa_ref, wx_ref, j == 0)
        xc, ra, ia, a, mult = g["xc"], g["ra"], g["ia"], g["a"], g["mult"]
        hv, zy, dpv = h_ref[...], zy_ref[...], dp_ref[...]
        gelu, gelu_grad = _gelu_and_grad(zy)
        dyb = dpv * hv * gelu_grad
        a_s[...] = a
        d_s[...] = dpv * gelu

        def group(gi, c):
            rows = pl.ds(pl.multiple_of((tt // SUBLANES - 1 - gi) * SUBLANES, SUBLANES), SUBLANES)
            a8, d8 = a_s[rows, :], d_s[rows, :]
            out = [None] * SUBLANES
            for r in reversed(range(SUBLANES)):
                dht = d8[r:r + 1, :] + c
                out[r] = dht
                c = a8[r:r + 1, :] * dht
            d_s[rows, :] = jnp.concatenate(out, axis=0)
            return c

        carry[0:1, :] = lax.fori_loop(0, tt // SUBLANES, group, carry[0:1, :])
        dht = d_s[...]
        hprev = _shift_down(hv, jnp.where(j > 0, hhalo_ref[...], 0.0), 1)
        ixc = ia * xc
        dlog_a = dht * hprev * a + jnp.where(g["is_t0"], 0.0, dht * ixc * (-(a * a) / mult))
        dia = dht * mult * xc
        dxc = dht * mult * ia
        dra = dlog_a * ((-LRU_C) * g["sp"])
        nl = -vec_ref[3:4, :]
        dvec_ref[3:4, :] += _row_sum(dlog_a * ((-LRU_C) * ra)) * (-_sigmoid(nl))
        dpa = dra * ra * (1.0 - ra)
        dpx = dia * ia * (1.0 - ia)
        dvec_ref[1:2, :] += _row_sum(dpa)
        dvec_ref[2:3, :] += _row_sum(dpx)
        xc_m, dpa_m, dpx_m = xc.astype(MXU), dpa.astype(MXU), dpx.astype(MXU)
        back = [_dot_nt(dpa_m[:, s], wa_ref[hd]) + _dot_nt(dpx_m[:, s], wx_ref[hd]) for hd, s in enumerate(g["heads"])]
        dwa = [_dot_tn(xc_m[:, s], dpa_m[:, s]) for s in g["heads"]]
        dwx = [_dot_tn(xc_m[:, s], dpx_m[:, s]) for s in g["heads"]]
        for hd in range(HEADS):
            dwa_ref[hd] += dwa[hd]
            dwx_ref[hd] += dwx[hd]
        dxc = dxc + jnp.concatenate(back, axis=1)
        dvec_ref[0:1, :] += _row_sum(dxc)
        dxb = dxc * cw_ref[CONV_WIDTH - 1:CONV_WIDTH, :]
        for k in range(CONV_WIDTH):
            row = CONV_WIDTH - 1 - k
            dcw_ref[row:row + 1, :] += _row_sum(dxc * g["shifted"][k])
            if k:
                dxb = dxb + _shift_up(dxc, nxt[...], k) * cw_ref[row:row + 1, :]
        nxt[...] = dxc[0:SUBLANES, :]
        dz_ref[:, 0:D] = dxb.astype(dz_ref.dtype)
        dz_ref[:, D:2 * D] = dyb.astype(dz_ref.dtype)

    hhalo = pl.BlockSpec((SUBLANES, D), lambda i: (jnp.maximum(rev(i) * (tt // SUBLANES) - 1, 0), 0))
    wacc = pl.BlockSpec((HEADS, HEAD_DIM, HEAD_DIM), lambda i: (0, 0, 0))
    return pl.pallas_call(
        body, name=name, grid=(nt,),
        in_specs=[rows_rev, sp["x"], sp["y"], sp["halo"], rows_rev, hhalo, sp["cw"], sp["vec"], sp["w"], sp["w"]],
        out_specs=[pl.BlockSpec((tt, 2 * D), lambda i: (rev(i), 0)), _vec_spec(), _vec_spec(), wacc, wacc],
        out_shape=[jax.ShapeDtypeStruct((t, 2 * D), MXU), jax.ShapeDtypeStruct((SUBLANES, D), F32),
                   jax.ShapeDtypeStruct((SUBLANES, D), F32), jax.ShapeDtypeStruct((HEADS, HEAD_DIM, HEAD_DIM), F32),
                   jax.ShapeDtypeStruct((HEADS, HEAD_DIM, HEAD_DIM), F32)],
        scratch_shapes=[pltpu.VMEM((tt, D), F32), pltpu.VMEM((tt, D), F32), pltpu.VMEM((SUBLANES, D), F32),
                        pltpu.VMEM((SUBLANES, D), F32)],
        compiler_params=_params())(dp, z, z, z, h, h, cw.table, vec.table, wa, wx)


def _hg_chunk(zq, zf, zi, lb):
    c = HG_CHUNK
    q, q_grad = _silu_and_grad(zq)
    sg = _sigmoid(zf)
    fg = lb + (1.0 - lb) * sg
    k = 1.0 - fg
    row, col = lax.broadcasted_iota(jnp.int32, (c, c), 0), lax.broadcasted_iota(jnp.int32, (c, c), 1)
    tri, tri_t = (row >= col).astype(F32), (row <= col).astype(F32)
    b = _dot_exact(tri, jnp.log(fg))
    mid, last = b[c // 2 - 1:c // 2, :], b[c - 1:c, :]
    eq = jnp.exp(jnp.minimum(b - mid, EXP_CLAMP))
    ek = jnp.exp(jnp.minimum(mid - b, EXP_CLAMP))
    eb = jnp.exp(b)
    ed = jnp.exp(last - b)
    return dict(q=q, q_grad=q_grad, sg=sg, fg=fg, k=k, v=zi, eq=eq, ek=ek, eb=eb, ed=ed, elast=jnp.exp(last), tri=tri, tri_t=tri_t,
                qe=q * eq, ke=k * ek, qb=q * eb, kd=k * ed)


def _hg_specs(tt, order):
    return [pl.BlockSpec((tt, D), lambda i, col=col: (order(i), col)) for col in range(4)]


def _hg_forward(name, z, vec):
    t = z.shape[0]
    tt = _tile(t, ROW_TILE)
    nt, nc = t // tt, tt // HG_CHUNK
    heads = [slice(h * HEAD_DIM, (h + 1) * HEAD_DIM) for h in range(HEADS)]

    def body(zq_ref, zf_ref, zi_ref, zg_ref, vec_ref, p_ref, o_ref, ss_ref, state):
        @pl.when(pl.program_id(0) == 0)
        def _():
            state[...] = jnp.zeros_like(state)

        lb, gain = vec_ref[0:1, :], vec_ref[1:2, :]

        def chunk(ci, carry):
            rows = pl.ds(pl.multiple_of(ci * HG_CHUNK, HG_CHUNK), HG_CHUNK)
            g = _hg_chunk(zq_ref[rows, :], zf_ref[rows, :], zi_ref[rows, :], lb)
            zg = zg_ref[rows, :]
            causal = g["tri"] > 0.0
            qe_m, ke_m, qb_m, kd_m, v_m = (g[n].astype(MXU) for n in ("qe", "ke", "qb", "kd", "v"))
            atts = [_dot_nt(qe_m[:, s], ke_m[:, s]) for s in heads]
            o_parts, p_parts = [], []
            for hd, s in enumerate(heads):
                st = state[hd]
                ss_ref[ci, hd] = st
                att = jnp.where(causal, atts[hd], 0.0)
                o_parts.append(_dot(att, v_m[:, s]) + _dot_nt(qb_m[:, s], st))
                state[hd] = st * g["elast"][:, s] + _dot_tn(v_m[:, s], kd_m[:, s])
            for o, s in zip(o_parts, heads):
                r = lax.rsqrt(jnp.mean(o * o, axis=-1, keepdims=True) + GNORM_EPS)
                p_parts.append((o * r) * gain[:, s])
            o_ref[rows, :] = jnp.concatenate(o_parts, axis=1)
            p_ref[rows, :] = (jnp.concatenate(p_parts, axis=1) * _silu(zg)).astype(p_ref.dtype)
            return carry

        lax.fori_loop(0, nc, chunk, 0, unroll=True)

    return pl.pallas_call(
        body, name=name, grid=(nt,), in_specs=_hg_specs(tt, lambda i: i) + [_group_spec(vec)],
        out_specs=[_rows_spec(tt), _rows_spec(tt), pl.BlockSpec((nc, HEADS, HEAD_DIM, HEAD_DIM), lambda i: (i, 0, 0, 0))],
        out_shape=[jax.ShapeDtypeStruct((t, D), MXU), jax.ShapeDtypeStruct((t, D), F32),
                   jax.ShapeDtypeStruct((t // HG_CHUNK, HEADS, HEAD_DIM, HEAD_DIM), F32)],
        scratch_shapes=[pltpu.VMEM((HEADS, HEAD_DIM, HEAD_DIM), F32)], compiler_params=_params())(z, z, z, z, vec.table)


def _hg_backward(name, dp, z, o, ss, vec):
    t = z.shape[0]
    tt = _tile(t, ROW_TILE)
    nt, nc = t // tt, tt // HG_CHUNK
    rev = lambda i: nt - 1 - i
    heads = [slice(h * HEAD_DIM, (h + 1) * HEAD_DIM) for h in range(HEADS)]
    rows_rev = pl.BlockSpec((tt, D), lambda i: (rev(i), 0))

    def body(dp_ref, zq_ref, zf_ref, zi_ref, zg_ref, o_ref, ss_ref, vec_ref, dz_ref, acc_ref, dstate):
        @pl.when(pl.program_id(0) == 0)
        def _():
            dstate[...] = jnp.zeros_like(dstate)
            acc_ref[...] = jnp.zeros_like(acc_ref)

        lb, gain = vec_ref[0:1, :], vec_ref[1:2, :]

        def chunk(cr, carry):
            ci = nc - 1 - cr
            rows = pl.ds(pl.multiple_of(ci * HG_CHUNK, HG_CHUNK), HG_CHUNK)
            zq, zg = zq_ref[rows, :], zg_ref[rows, :]
            g = _hg_chunk(zq, zf_ref[rows, :], zi_ref[rows, :], lb)
            ov, dpv = o_ref[rows, :], dp_ref[rows, :]
            causal = g["tri"] > 0.0
            silu_g, silu_g_grad = _silu_and_grad(zg)
            don = dpv * silu_g
            dgate = dpv * silu_g_grad
            qe_m, ke_m, qb_m, kd_m, v_m = (g[n].astype(MXU) for n in ("qe", "ke", "qb", "kd", "v"))
            parts = {n: [] for n in ("dzg", "dgain", "dv", "dqe", "dke", "dqb", "dkd", "dlast")}
            dos, atts, datts = [], [], []
            for hd, s in enumerate(heads):
                oh = ov[:, s]
                r = lax.rsqrt(jnp.mean(oh * oh, axis=-1, keepdims=True) + GNORM_EPS)
                on = oh * r
                parts["dzg"].append(dgate[:, s] * (on * gain[:, s]))
                parts["dgain"].append(_row_sum(don[:, s] * on))
                dtmp = don[:, s] * gain[:, s]
                dos.append((r * (dtmp - on * jnp.mean(dtmp * on, axis=-1, keepdims=True))).astype(MXU))
            for hd, s in enumerate(heads):
                atts.append(_dot_nt(qe_m[:, s], ke_m[:, s]))
                datts.append(_dot_nt(dos[hd], v_m[:, s]))
            for hd, s in enumerate(heads):
                do, att, datt = dos[hd], jnp.where(causal, atts[hd], 0.0).astype(MXU), jnp.where(causal, datts[hd], 0.0).astype(MXU)
                st, dst = ss_ref[ci, hd], dstate[hd]
                st_m, dst_m = st.astype(MXU), dst.astype(MXU)
                qe, ke, qb, kd, v = qe_m[:, s], ke_m[:, s], qb_m[:, s], kd_m[:, s], v_m[:, s]
                parts["dv"].append(_dot_tn(att, do) + _dot_nt(kd, dst_m))
                parts["dqe"].append(_dot(datt, ke))
                parts["dke"].append(_dot_tn(datt, qe))
                parts["dqb"].append(_dot(do, st_m))
                parts["dkd"].append(_dot(v, dst_m))
                parts["dlast"].append(g["elast"][:, s] * _row_sum(dst * st))
                dstate[hd] = dst * g["elast"][:, s] + _dot_tn(do, qb)
            whole = {n: jnp.concatenate(p, axis=1) for n, p in parts.items()}
            acc_ref[1:2, :] += whole["dgain"]
            dqe, dke, dqb, dkd = whole["dqe"], whole["dke"], whole["dqb"], whole["dkd"]
            dq = dqe * g["eq"] + dqb * g["eb"]
            dk = dke * g["ek"] + dkd * g["ed"]
            dkdk = dkd * g["kd"]
            db = dqe * qe_m.astype(F32) - dke * ke_m.astype(F32) + dqb * g["qb"] - dkdk
            dlogf = _dot_exact(g["tri_t"], db) + (whole["dlast"] + _row_sum(dkdk))
            dfg = dlogf / g["fg"] - dk
            sg = g["sg"]
            acc_ref[0:1, :] += _row_sum(dfg * (1.0 - sg))
            dz_ref[rows, 0:D] = (dq * g["q_grad"]).astype(dz_ref.dtype)
            dz_ref[rows, D:2 * D] = (dfg * (1.0 - lb) * sg * (1.0 - sg)).astype(dz_ref.dtype)
            dz_ref[rows, 2 * D:3 * D] = whole["dv"].astype(dz_ref.dtype)
            dz_ref[rows, 3 * D:4 * D] = whole["dzg"].astype(dz_ref.dtype)
            return carry

        lax.fori_loop(0, nc, chunk, 0, unroll=True)

    return pl.pallas_call(
        body, name=name, grid=(nt,),
        in_specs=[rows_rev] + _hg_specs(tt, rev) + [rows_rev, pl.BlockSpec((nc, HEADS, HEAD_DIM, HEAD_DIM), lambda i: (rev(i), 0, 0, 0)),
                                                  _group_spec(vec)],
        out_specs=[pl.BlockSpec((tt, 4 * D), lambda i: (rev(i), 0)), _vec_spec()],
        out_shape=[jax.ShapeDtypeStruct((t, 4 * D), MXU), jax.ShapeDtypeStruct((SUBLANES, D), F32)],
        scratch_shapes=[pltpu.VMEM((HEADS, HEAD_DIM, HEAD_DIM), F32)], compiler_params=_params())(dp, z, z, z, z, o, ss, vec.table)


def _mod_forward(name, c_all, mod_w, lower):
    depth, _, sw = mod_w.shape

    def body(c_ref, w_ref, lo_ref, cs_ref, mod_ref, lb_ref):
        cs = _silu(c_ref[...])
        mod_ref[...] = _dot(cs, w_ref[...])

        @pl.when(pl.program_id(0) == 0)
        def _():
            cs_ref[...] = cs
            lo = lo_ref[...]
            e = jnp.exp(lo - jnp.max(lo, axis=0, keepdims=True))
            sm = e / jnp.sum(e, axis=0, keepdims=True)
            lb_ref[0:1, :] = jnp.zeros((1, D), F32)
            for l in range(1, depth):
                lb_ref[l:l + 1, :] = lb_ref[l - 1:l, :] + sm[l:l + 1, :]

    return pl.pallas_call(
        body, name=name, grid=(depth,),
        in_specs=[pl.BlockSpec((N_DEV, D), lambda l: (0, 0)), pl.BlockSpec((None, D, sw), lambda l: (l, 0, 0)),
                  pl.BlockSpec((depth, D), lambda l: (0, 0))],
        out_specs=[pl.BlockSpec((N_DEV, D), lambda l: (0, 0)), pl.BlockSpec((None, N_DEV, sw), lambda l: (l, 0, 0)),
                   pl.BlockSpec((depth, D), lambda l: (0, 0))],
        out_shape=[jax.ShapeDtypeStruct((N_DEV, D), F32), jax.ShapeDtypeStruct((depth, N_DEV, sw), F32),
                   jax.ShapeDtypeStruct((depth, D), F32)], compiler_params=_params())(c_all, mod_w, lower)


def _mod_weight_grad(name, cs_t, dmod):
    depth, pad, sw = dmod.shape

    def body(c_ref, d_ref, g_ref):
        g_ref[...] = _dot(c_ref[...], d_ref[...])

    return pl.pallas_call(
        body, name=name, grid=(depth,),
        in_specs=[pl.BlockSpec((D, pad), lambda l: (0, 0)), pl.BlockSpec((None, pad, sw), lambda l: (l, 0, 0))],
        out_specs=pl.BlockSpec((None, D, sw), lambda l: (l, 0, 0)),
        out_shape=jax.ShapeDtypeStruct((depth, D, sw), F32), compiler_params=_params())(cs_t, dmod)


def _lower_bound_grad(name, lower, dlb):
    depth = lower.shape[0]

    def body(lo_ref, d_ref, out_ref):
        lo = lo_ref[...]
        e = jnp.exp(lo - jnp.max(lo, axis=0, keepdims=True))
        sm = e / jnp.sum(e, axis=0, keepdims=True)
        out_ref[...] = jnp.zeros_like(out_ref)
        dsm = [jnp.zeros((1, D), F32)]
        for l in range(1, depth):
            tail = d_ref[l:l + 1, :]
            for m in range(l + 1, depth):
                tail = tail + d_ref[m:m + 1, :]
            dsm.append(tail)
        inner = sm[1:2, :] * dsm[1]
        for l in range(2, depth):
            inner = inner + sm[l:l + 1, :] * dsm[l]
        for l in range(depth):
            out_ref[l:l + 1, :] = sm[l:l + 1, :] * (dsm[l] - inner)
        for j in range(SUBLANES - depth):
            row = d_ref[depth + j:depth + j + 1, :]
            tot = row[:, 0:HEAD_DIM]
            for hd in range(1, HEADS):
                tot = tot + row[:, hd * HEAD_DIM:(hd + 1) * HEAD_DIM]
            out_ref[depth + j:depth + j + 1, 0:HEAD_DIM] = tot

    return pl.pallas_call(body, name=name, out_shape=jax.ShapeDtypeStruct((SUBLANES, D), F32))(lower, dlb)


def _adamw(name, parts, w, m, v, layer=None, prev=None):
    p, r, c = parts.shape
    tr = _tile(r, max(SUBLANES, ADAMW_STEP_BYTES // (4 * c * (p + 7))))
    stacked = layer is not None
    n_prev = 4 if prev is not None else 0

    def body(*refs):
        parts_ref, w_ref, m_ref, v_ref = refs[:4]
        g_ref, d_ref, m_out, v_out = refs[4 + n_prev:]
        g = parts_ref[0].astype(F32)
        for q in range(1, p):
            g = g + parts_ref[q].astype(F32)
        m2 = ADAM_B1 * m_ref[...] + (1.0 - ADAM_B1) * g
        v2 = ADAM_B2 * v_ref[...] + (1.0 - ADAM_B2) * (g * g)
        m_hat = m2 / (1.0 - ADAM_B1 ** ADAM_STEP)
        v_hat = v2 / (1.0 - ADAM_B2 ** ADAM_STEP)
        g_ref[...] = g
        d_ref[...] = -ADAM_LR * (m_hat / (jnp.sqrt(v_hat) + ADAM_EPS) + ADAM_WD * w_ref[...])
        m_out[...] = m2
        v_out[...] = v2

    if stacked:
        spec = pl.BlockSpec((None, tr, c), lambda i: (layer, i, 0))
    else:
        spec = pl.BlockSpec((tr, c), lambda i: (i, 0))
    return pl.pallas_call(
        body, name=name, grid=(r // tr,),
        in_specs=[pl.BlockSpec((p, tr, c), lambda i: (0, i, 0)), spec, spec, spec] + [ANY] * n_prev, out_specs=[spec] * 4,
        out_shape=[jax.ShapeDtypeStruct(w.shape, F32)] * 4,
        input_output_aliases={4 + q: q for q in range(n_prev)}, compiler_params=_params(),
    )(parts, w, m, v, *(prev or []))


def _hbm(a):
    return pltpu.with_memory_space_constraint(a, pltpu.HBM)


def _pad_rows(a, rows=SUBLANES):
    a = a.reshape(-1, a.shape[-1])
    return jnp.concatenate([a, jnp.zeros((rows - a.shape[0], a.shape[1]), a.dtype)], axis=0) if a.shape[0] < rows else a


def kernel(x, c, mod_w, mod_b, norm_mix, norm_mlp, norm_final, rg_w_in, rg_conv_w, rg_conv_b, rg_w_a, rg_b_a, rg_w_x, rg_b_x, rg_lambda, rg_w_out, hg_w_in, hg_lower_bounds, hg_gnorm, hg_w_out, mlp_w1, mlp_w2, loss_target, m_mod_w, m_mod_b, m_norm_mix, m_norm_mlp, m_norm_final, m_rg_w_in, m_rg_conv_w, m_rg_conv_b, m_rg_w_a, m_rg_b_a, m_rg_w_x, m_rg_b_x, m_rg_lambda, m_rg_w_out, m_hg_w_in, m_hg_lower_bounds, m_hg_gnorm, m_hg_w_out, m_mlp_w1, m_mlp_w2, v_mod_w, v_mod_b, v_norm_mix, v_norm_mlp, v_norm_final, v_rg_w_in, v_rg_conv_w, v_rg_conv_b, v_rg_w_a, v_rg_b_a, v_rg_w_x, v_rg_b_x, v_rg_lambda, v_rg_w_out, v_hg_w_in, v_hg_lower_bounds, v_hg_gnorm, v_hg_w_out, v_mlp_w1, v_mlp_w2):
    me = 4 * lax.axis_index("x") + 2 * lax.axis_index("y") + lax.axis_index("c")
    x0 = x[0]
    target = loss_target[0]
    n_rg, n_hg = rg_w_in.shape[0], hg_w_in.shape[0]
    sw_mod = mod_w.shape[2]

    c_all, cw_all = _all_gather([_pad_rows(c), rg_conv_w.reshape(n_rg * CONV_WIDTH, -1)], "gather_cond")
    c_all = c_all[:, 0, :]
    conv_w = cw_all.transpose(1, 0, 2).reshape(n_rg, CONV_WIDTH, D)
    cs_all, mod_part, lb_all = _mod_forward("mod_forward", c_all, mod_w, hg_lower_bounds)
    (mod_gathered,) = _all_gather([mod_part.reshape(DEPTH * N_DEV, sw_mod)], "gather_mod")

    ids = iter(range(5 * DEPTH))
    shards = []
    for layer in range(DEPTH):
        j = layer // 2
        w_in, w_out = (rg_w_in[j], rg_w_out[j]) if layer % 2 == 0 else (hg_w_in[j], hg_w_out[j])
        shards.append([w_in.astype(MXU), w_out.astype(MXU), mlp_w1[layer].astype(MXU), mlp_w2[layer].astype(MXU)])
    shards, mod_gathered = lax.optimization_barrier((shards, mod_gathered))
    weights = []
    for layer in range(DEPTH):
        g_in, g_out = _all_gather_async(shards[layer][:2], f"gather_mixer_weights_{layer}", next(ids), ["cols", "rows"])
        g_w1, g_w2 = _all_gather_async(shards[layer][2:], f"gather_mlp_weights_{layer}", next(ids), ["cols", "rows"])
        weights.append(dict(w_in=_hbm(g_in), w_out=_hbm(g_out), w1=_hbm(g_w1), w2=_hbm(g_w2)))

    mod_mine = lax.dynamic_index_in_dim(mod_gathered.reshape(N_DEV, DEPTH, N_DEV, sw_mod), me, axis=2, keepdims=False)
    mod = mod_mine.transpose(1, 0, 2).reshape(DEPTH, 6, D) + mod_b.reshape(DEPTH, 6, D)

    pieces, n_groups = [], 0
    placed = [dict() for _ in range(DEPTH)]

    def place(rows):
        nonlocal n_groups
        rows = [r.reshape(-1, D).astype(F32) for r in rows]
        pieces.extend(rows + [jnp.zeros((SUBLANES - sum(r.shape[0] for r in rows), D), F32)])
        n_groups += 1
        return n_groups - 1

    for layer in range(DEPTH):
        j, at = layer // 2, placed[layer]
        at["vec_mix"] = place([norm_mix[layer], mod[layer, 1], mod[layer, 0]])
        at["vec_mlp"] = place([norm_mlp[layer], mod[layer, 4], mod[layer, 3]])
        at["gate_mix"], at["gate_mlp"] = place([mod[layer, 2]]), place([mod[layer, 5]])
        if layer % 2 == 0:
            at["cw"] = place([conv_w[j]])
            at["vec"] = place([rg_conv_b[j], rg_b_a[j], rg_b_x[j], rg_lambda[j]])
        else:
            at["vec"] = place([lb_all[layer], jnp.tile(hg_gnorm[j], HEADS)])
    final_group = place([norm_final])
    table = _hbm(jnp.concatenate(pieces, axis=0))

    saved = []
    xl = x0
    for layer in range(DEPTH):
        j, wt = layer // 2, weights[layer]
        is_rg = layer % 2 == 0
        s = dict(x=xl, **{k: Group(table, g) for k, g in placed[layer].items()})
        s["h"] = _hbm(_norm_mod("norm_mod", xl, s["vec_mix"]))
        if is_rg:
            (s["z"],) = _mm_tokens("rg_in", s["h"], wt["w_in"], False, (F32,))
            s["wa"], s["wx"] = rg_w_a[j].astype(MXU), rg_w_x[j].astype(MXU)
            s["p"], s["hr"] = _rg_forward("rg_forward", s["z"], s["cw"], s["vec"], s["wa"], s["wx"])
        else:
            (s["z"],) = _mm_tokens("hg_in", s["h"], wt["w_in"], False, (F32,))
            s["p"], s["o"], s["ss"] = _hg_forward("hg_forward", s["z"], s["vec"])
        s["y"], s["x1"] = _mm_tokens("mix_out", s["p"], wt["w_out"], False, (MXU, F32), _ep_residual, extras=(xl, s["gate_mix"]))
        s["p"] = _hbm(s["p"])
        s["h2"] = _hbm(_norm_mod("norm_mod", s["x1"], s["vec_mlp"]))
        s["r"], s["ff"], xl = _mlp_forward("mlp_forward", s["h2"], wt["w1"], wt["w2"], s["x1"], s["gate_mlp"])
        saved.append(s)

    dx, head, dff = _loss_head("loss_head", xl, target, Group(table, final_group), saved[-1]["ff"], saved[-1]["gate_mlp"])
    g_mlp = head[GATE_ROW:GATE_ROW + 1]

    results = {}
    big = dict(rg_w_in=(rg_w_in, m_rg_w_in, v_rg_w_in), rg_w_out=(rg_w_out, m_rg_w_out, v_rg_w_out),
               hg_w_in=(hg_w_in, m_hg_w_in, v_hg_w_in), hg_w_out=(hg_w_out, m_hg_w_out, v_hg_w_out),
               mlp_w1=(mlp_w1, m_mlp_w1, v_mlp_w1), mlp_w2=(mlp_w2, m_mlp_w2, v_mlp_w2))

    def update_layer(layer, landed):
        mixer = "rg" if layer % 2 == 0 else "hg"
        l_in, l_out, l_w1, l_w2 = landed
        for nm, parts, idx in ((f"{mixer}_w_in", l_in, layer // 2), (f"{mixer}_w_out", l_out, layer // 2),
                               ("mlp_w1", l_w1, layer), ("mlp_w2", l_w2, layer)):
            w, m, v = big[nm]
            results[nm] = _adamw(f"adamw_{nm}", parts, w, m, v, layer=idx, prev=results.get(nm))

    landed = [None] * DEPTH
    small = [None] * DEPTH
    parity = lax.axis_index("c").astype(jnp.int32).reshape(1)

    def send_chip_sums(layer, grads, from_sibling, anchor):
        sums = _pair_sum("pair_sum", parity, grads, from_sibling)
        sums, anchor = lax.optimization_barrier((sums, anchor))
        landed[layer] = [_hbm(a) for a in _chip_exchange_async(sums, f"exchange_grads_{layer}", next(ids))]
        return anchor

    pending = None
    for layer in reversed(range(DEPTH)):
        j, wt, s = layer // 2, weights[layer], saved[layer]
        is_rg = layer % 2 == 0
        dff = _hbm(dff)
        da, dx1, n_mlp, dyb = _mlp_backward("mlp_backward", dff, s["r"], wt["w1"], wt["w2"], s["x1"], s["vec_mlp"], dx,
                                            s["y"], s["gate_mix"])
        if pending is not None:
            da = send_chip_sums(*pending, da)
        dw2 = _mm_grad("mlp_out_grad", s["r"], dff, False, prologue=_square).reshape(N_DEV, -1, D)
        dw1 = _mm_grad("mlp_in_grad", s["h2"], da, True)
        dyb = _hbm(dyb)
        (dp,) = _mm_tokens("mix_out_t", dyb, wt["w_out"], True, (F32,))
        dw_out = _hbm(_mm_grad("mix_out_grad", s["p"], dyb, False).reshape(N_DEV, -1, D))
        if is_rg:
            dz, dvec, dcw, dwa, dwx = _rg_backward("rg_backward", dp, s["z"], s["hr"], s["cw"], s["vec"], s["wa"], s["wx"])
            gate_grads, dz = lax.optimization_barrier(
                ([dwa.reshape(-1, HEAD_DIM).astype(MXU), dwx.reshape(-1, HEAD_DIM).astype(MXU)], dz))
            gate_parts = _all_gather_async(gate_grads, f"gather_gate_grads_{j}", next(ids), ["stack", "stack"])
            mixer_small = dict(dvec=dvec, dcw=dcw, gate_parts=gate_parts)
            dw_in = _mm_grad("rg_in_grad", s["h"], dz, True)
        else:
            dz, dvec = _hg_backward("hg_backward", dp, s["z"], s["o"], s["ss"], s["vec"])
            mixer_small = dict(dvec=dvec)
            dw_in = _mm_grad("hg_in_grad", s["h"], dz, True)
        grads = [dw_in, dw_out, dw1, dw2]
        pending = (layer, grads, _sibling_send_async(grads, f"pair_grads_{layer}", next(ids)))
        small[layer] = dict(g_mlp=g_mlp, n_mlp=n_mlp, g_mix=n_mlp[GATE_ROW:GATE_ROW + 1], **mixer_small)
        if layer:
            below = saved[layer - 1]
            dx, n_mix, dff = _mm_norm_grad("mixer_in_t", dz, wt["w_in"], s["x"], s["vec_mix"], dx1, below["ff"], below["gate_mlp"])
            g_mlp = n_mix[GATE_ROW:GATE_ROW + 1]
        else:
            dx, n_mix = _mm_norm_grad("mixer_in_t_first", dz, wt["w_in"], s["x"], s["vec_mix"], dx1)
        small[layer]["n_mix"] = n_mix
        if layer + 1 < DEPTH:
            stream = (dx, dff) if layer else (dx,)
            landed[layer + 1], stream = lax.optimization_barrier((landed[layer + 1], stream))
            if "gate_parts" in small[layer + 1]:
                small[layer + 1]["gate_parts"], stream = lax.optimization_barrier((small[layer + 1]["gate_parts"], stream))
            dx, dff = stream if layer else (stream[0], None)
            update_layer(layer + 1, landed[layer + 1])

    dlb_rows = [jnp.zeros((1, D), F32) if l % 2 == 0 else small[l]["dvec"][0:1] for l in range(DEPTH)]
    dgn_rows = [small[2 * j + 1]["dvec"][1:2] for j in range(n_hg)]
    lb_grad = _lower_bound_grad("lower_bound_grad", hg_lower_bounds, _pad_rows(jnp.concatenate(dlb_rows + dgn_rows, axis=0)))
    dmod = jnp.stack([jnp.concatenate([small[l]["n_mix"][2], small[l]["n_mix"][1], small[l]["g_mix"][0],
                                       small[l]["n_mlp"][2], small[l]["n_mlp"][1], small[l]["g_mlp"][0]]) for l in range(DEPTH)])
    groups = [
        dmod.reshape(DEPTH * 6, D),
        jnp.stack([small[l]["n_mix"][0] for l in range(DEPTH)]),
        jnp.stack([small[l]["n_mlp"][0] for l in range(DEPTH)]),
        head[0:1],
        jnp.stack([small[2 * j]["dvec"][0] for j in range(n_rg)]),
        jnp.stack([small[2 * j]["dvec"][1] for j in range(n_rg)]),
        jnp.stack([small[2 * j]["dvec"][2] for j in range(n_rg)]),
        jnp.stack([small[2 * j]["dvec"][3] for j in range(n_rg)]),
        lb_grad[0:DEPTH],
        jnp.concatenate([lb_grad[DEPTH + j:DEPTH + j + 1, 0:HEAD_DIM] for j in range(n_hg)]
                        + [jnp.zeros((1, D - n_hg * HEAD_DIM), F32)], axis=1),
        jnp.concatenate([small[2 * j]["dcw"][0:CONV_WIDTH] for j in range(n_rg)], axis=0),
        head[1:2],
    ]
    params = [mod_b, norm_mix, norm_mlp, norm_final, rg_conv_b, rg_b_a, rg_b_x, rg_lambda, hg_lower_bounds, hg_gnorm]
    moms = [m_mod_b, m_norm_mix, m_norm_mlp, m_norm_final, m_rg_conv_b, m_rg_b_a, m_rg_b_x, m_rg_lambda, m_hg_lower_bounds, m_hg_gnorm]
    vars_ = [v_mod_b, v_norm_mix, v_norm_mlp, v_norm_final, v_rg_conv_b, v_rg_b_a, v_rg_b_x, v_rg_lambda, v_hg_lower_bounds, v_hg_gnorm]
    offsets, rows_of, at = [], [], 0
    for g in groups:
        offsets.append(at)
        rows_of.append(g.shape[0])
        at += -(-g.shape[0] // SUBLANES) * SUBLANES
    packed = jnp.concatenate([_pad_rows(g, -(-g.shape[0] // SUBLANES) * SUBLANES) for g in groups], axis=0)
    (small_parts,) = _all_gather_async([packed], "gather_small_grads", next(ids), ["stack"])
    send_chip_sums(*pending, dx)

    def pack_like(arrs):
        out = []
        for g_rows, off, a in zip(rows_of, offsets, arrs):
            flat = a.reshape(-1)
            flat = jnp.concatenate([flat, jnp.zeros((g_rows * D - flat.shape[0],), F32)])
            out.append(_pad_rows(flat.reshape(g_rows, D), -(-g_rows // SUBLANES) * SUBLANES))
        rest = packed.shape[0] - sum(o.shape[0] for o in out)
        return jnp.concatenate(out + [jnp.zeros((rest, D), F32)], axis=0)

    small_out = _adamw("adamw_small", small_parts, pack_like(params), pack_like(moms), pack_like(vars_))

    def unpack(q, idx, like):
        rows = small_out[q][offsets[idx]:offsets[idx] + rows_of[idx]]
        return rows.reshape(-1)[:like.size].reshape(like.shape)

    loss = jnp.sum(small_out[0][offsets[11]])
    names =["mod_b", "norm_mix", "norm_mlp", "norm_final", "rg_conv_b", "rg_b_a", "rg_b_x", "rg_lambda", "hg_lower_bounds", "hg_gnorm"]
    for idx, (nm, like) in enumerate(zip(names, params)):
        results[nm] = [unpack(q, idx, like) for q in range(4)]

    cw_parts = lax.dynamic_slice_in_dim(small_parts[:, offsets[10]:offsets[10] + n_rg * CONV_WIDTH, :], me * (D // N_DEV), D // N_DEV, axis=2)
    shp = rg_conv_w.shape
    results["rg_conv_w"] = [o.reshape(shp) for o in _adamw(
        "adamw_conv", cw_parts, rg_conv_w.reshape(-1, shp[-1]), m_rg_conv_w.reshape(-1, shp[-1]), v_rg_conv_w.reshape(-1, shp[-1]))]
    shp = rg_w_a.shape
    stacked = (n_rg, HEADS * HEAD_DIM, HEAD_DIM)
    for nm, which, (w, m, v) in (("rg_w_a", 0, (rg_w_a, m_rg_w_a, v_rg_w_a)), ("rg_w_x", 1, (rg_w_x, m_rg_w_x, v_rg_w_x))):
        out = None
        for j in reversed(range(n_rg)):
            out = _adamw("adamw_gate", small[2 * j]["gate_parts"][which], w.reshape(stacked), m.reshape(stacked),
                         v.reshape(stacked), layer=j, prev=out)
        results[nm] = [o.reshape(shp) for o in out]

    dmod_all = small_parts[:, 0:DEPTH * 6, :].reshape(N_DEV, DEPTH, 6 * D)
    dmod_cols = lax.dynamic_slice_in_dim(dmod_all, me * sw_mod, sw_mod, axis=2).transpose(1, 0, 2)
    pad = HEAD_DIM - N_DEV
    dmod_pad = jnp.concatenate([dmod_cols, jnp.zeros((DEPTH, pad, sw_mod), F32)], axis=1).astype(MXU)
    cs_t = jnp.concatenate([cs_all.T, jnp.zeros((D, pad), F32)], axis=1).astype(MXU)
    g_mod_w = _mod_weight_grad("mod_weight_grad", cs_t, dmod_pad)
    results["mod_w"] = [o.reshape(mod_w.shape) for o in _adamw(
        "adamw_mod", g_mod_w.reshape(1, -1, sw_mod), mod_w.reshape(-1, sw_mod), m_mod_w.reshape(-1, sw_mod), v_mod_w.reshape(-1, sw_mod))]

    update_layer(0, landed[0])

    order = ["mod_w", "mod_b", "norm_mix", "norm_mlp", "norm_final", "rg_w_in", "rg_conv_w", "rg_conv_b", "rg_w_a", "rg_b_a", "rg_w_x",
             "rg_b_x", "rg_lambda", "rg_w_out", "hg_w_in", "hg_lower_bounds", "hg_gnorm", "hg_w_out", "mlp_w1", "mlp_w2"]
    return (loss, dx[None], *[results[n][0] for n in order], *[results[n][1] for n in order],
            *[results[n][2] for n in order], *[results[n][3] for n in order])
```

```python
import collections

import jax
import jax.numpy as jnp
from jax import lax
from jax.experimental import pallas as pl
from jax.experimental.pallas import tpu as pltpu
from jax.experimental.pallas import tpu_sc as plsc

F32 = jnp.float32
MXU = jnp.bfloat16

N_DEV = 8
D = 1024
DEPTH = 4
HEADS = 8
HEAD_DIM = 128
CONV_WIDTH = 4
LRU_C = 8.0
HG_CHUNK = 64
NORM_EPS = 1e-6
GNORM_EPS = 1e-5
ADAM_LR = 0.001
ADAM_B1 = 0.9
ADAM_B2 = 0.999
ADAM_EPS = 1e-08
ADAM_WD = 0.01
ADAM_STEP = 10
GELU_C = 0.7978845608028654
GELU_K = 0.044715
EXP_CLAMP = 80.0
SUBLANES = 8
VMEM_LIMIT = 48 * 1024 * 1024
ROW_TILE = 256
RG_ROWS = 256
MM_TILE = 1024
MM_TOKENS = 512
MLP_TOKENS = 256
ADAMW_STEP_BYTES = 8 * 1024 * 1024

MESH = pl.DeviceIdType.MESH
ANY = pl.BlockSpec(memory_space=pl.ANY)


def _params():
    return pltpu.CompilerParams(vmem_limit_bytes=VMEM_LIMIT)


def _tile(n, target):
    if n <= target:
        return n
    t = target // SUBLANES * SUBLANES
    while n % t:
        t -= SUBLANES
    return t


def _sigmoid(x):
    return 1.0 / (1.0 + jnp.exp(-x))


def _silu(x):
    return x * _sigmoid(x)


def _silu_and_grad(x):
    s = _sigmoid(x)
    return x * s, s * (1.0 + x * (1.0 - s))


def _gelu(y):
    return 0.5 * y * (1.0 + jnp.tanh(GELU_C * (y + GELU_K * y * y * y)))


def _gelu_and_grad(y):
    y2 = y * y
    th = jnp.tanh(GELU_C * (y + GELU_K * y2 * y))
    half = 0.5 * (1.0 + th)
    return y * half, half + 0.5 * y * (1.0 - th * th) * GELU_C * (1.0 + 3.0 * GELU_K * y2)


def _dot(a, b):
    return lax.dot_general(a.astype(MXU), b.astype(MXU), (((1,), (0,)), ((), ())), preferred_element_type=F32)


def _dot_nt(a, b):
    return lax.dot_general(a.astype(MXU), b.astype(MXU), (((1,), (1,)), ((), ())), preferred_element_type=F32)


def _dot_tn(a, b):
    return lax.dot_general(a.astype(MXU), b.astype(MXU), (((0,), (0,)), ((), ())), preferred_element_type=F32)


def _dot_exact(tri, x):
    t = tri.astype(MXU)
    hi = x.astype(MXU)
    r1 = x - hi.astype(F32)
    mid = r1.astype(MXU)
    lo = (r1 - mid.astype(F32)).astype(MXU)
    dn = (((1,), (0,)), ((), ()))
    return (lax.dot_general(t, hi, dn, preferred_element_type=F32) + lax.dot_general(t, mid, dn, preferred_element_type=F32)
            + lax.dot_general(t, lo, dn, preferred_element_type=F32))


def _row_sum(v):
    return jnp.sum(v, axis=0, keepdims=True)


def _handshake(partners):
    barrier = pltpu.get_barrier_semaphore()
    for p in partners:
        pl.semaphore_signal(barrier, inc=1, device_id=p, device_id_type=MESH)
    pl.semaphore_wait(barrier, len(partners))


def _gather_body(n, per_array_sems, handshake, layouts):
    def body(*refs):
        ins, outs = refs[:n], refs[n:2 * n]
        send_sems, recv_sems, local_sems = refs[2 * n:]
        x, y, c = lax.axis_index("x"), lax.axis_index("y"), lax.axis_index("c")
        me, sibling = (x, y, c), (x, y, 1 - c)
        chips = [(1 - x, y), (x, 1 - y), (1 - x, 1 - y)]
        if handshake:
            _handshake([sibling] + [(*chip, c) for chip in chips])

        def sem(sems, a, k):
            return sems.at[a, k] if per_array_sems else sems.at[k]

        def slot(a, p):
            block = 4 * p[0] + 2 * p[1] + p[2]
            r, c_ = ins[a].shape
            if layouts[a] == "rows":
                return outs[a].at[pl.ds(block * r, r), :]
            if layouts[a] == "cols":
                return outs[a].at[:, pl.ds(block * c_, c_)]
            return outs[a].at[block]

        def copy(a, k, block, to, src=None):
            return pltpu.make_async_remote_copy(
                src_ref=slot(a, block) if src is None else src, dst_ref=slot(a, block),
                send_sem=sem(send_sems, a, k), recv_sem=sem(recv_sems, a, k), device_id=to, device_id_type=MESH)

        mine = [pltpu.make_async_copy(ins[a], slot(a, me), local_sems.at[a if per_array_sems else 0]) for a in range(n)]
        for cp in mine:
            cp.start()
        first = []
        for a in range(n):
            first.append(copy(a, 0, me, sibling, src=ins[a]))
            first += [copy(a, 1 + j, me, (*chip, c), src=ins[a]) for j, chip in enumerate(chips)]
        for cp in first:
            cp.start()
        passed = []
        for j, chip in enumerate(chips):
            for a in range(n):
                copy(a, 1 + j, (*chip, c), me).wait_recv()
            for a in range(n):
                cp = copy(a, 4 + j, (*chip, c), sibling)
                cp.start()
                passed.append(cp)
        for a in range(n):
            copy(a, 0, sibling, me).wait_recv()
        for j, chip in enumerate(chips):
            for a in range(n):
                copy(a, 4 + j, (*chip, 1 - c), me).wait_recv()
        for cp in first + passed:
            cp.wait_send()
        for cp in mine:
            cp.wait()

    return body


def _sibling_send_body(n, per_array_sems, handshake):
    def body(*refs):
        ins, outs = refs[:n], refs[n:2 * n]
        send_sems, recv_sems, _ = refs[2 * n:]
        x, y, c = lax.axis_index("x"), lax.axis_index("y"), lax.axis_index("c")
        sibling = (x, y, 1 - c)
        if handshake:
            _handshake([sibling])

        def sem(sems, a):
            return sems.at[a, 0] if per_array_sems else sems.at[0]

        copies = [pltpu.make_async_remote_copy(
            src_ref=ins[a].at[2 * q + 1 - c], dst_ref=outs[a].at[q], send_sem=sem(send_sems, a), recv_sem=sem(recv_sems, a),
            device_id=sibling, device_id_type=MESH) for a in range(n) for q in range(4)]
        for cp in copies:
            cp.start()
        for cp in copies:
            cp.wait_recv()
        for cp in copies:
            cp.wait_send()

    return body


def _chip_exchange_body(n, per_array_sems, handshake):
    def body(*refs):
        ins, outs = refs[:n], refs[n:2 * n]
        send_sems, recv_sems, local_sems = refs[2 * n:]
        x, y, c = lax.axis_index("x"), lax.axis_index("y"), lax.axis_index("c")
        my_chip = 2 * x + y
        chips = [(1 - x, y), (x, 1 - y), (1 - x, 1 - y)]
        if handshake:
            _handshake([(*chip, c) for chip in chips])

        def sem(sems, a, k):
            return sems.at[a, k] if per_array_sems else sems.at[k]

        def copy(a, k, landing):
            px, py = chips[k]
            return pltpu.make_async_remote_copy(
                src_ref=ins[a].at[2 * px + py], dst_ref=outs[a].at[landing], send_sem=sem(send_sems, a, k),
                recv_sem=sem(recv_sems, a, k), device_id=(px, py, c), device_id_type=MESH)

        mine = [pltpu.make_async_copy(ins[a].at[my_chip], outs[a].at[my_chip], local_sems.at[a if per_array_sems else 0])
                for a in range(n)]
        for cp in mine:
            cp.start()
        sent = [copy(a, k, my_chip) for a in range(n) for k in range(3)]
        for cp in sent:
            cp.start()
        for a in range(n):
            for k, (px, py) in enumerate(chips):
                copy(a, k, 2 * px + py).wait_recv()
        for cp in sent:
            cp.wait_send()
        for cp in mine:
            cp.wait()

    return body


def _all_gather(arrs, name):
    n = len(arrs)
    return pl.pallas_call(
        _gather_body(n, True, False, ["stack"] * n), name=name, in_specs=[ANY] * n, out_specs=[ANY] * n,
        out_shape=[jax.ShapeDtypeStruct((N_DEV,) + a.shape, a.dtype) for a in arrs],
        scratch_shapes=[pltpu.SemaphoreType.DMA((n, 7)), pltpu.SemaphoreType.DMA((n, 7)), pltpu.SemaphoreType.DMA((n,))],
    )(*arrs)


def _on_sequencer(body, arrs, out_type, name, collective_id):
    return pl.kernel(
        body, name=name, out_type=out_type, mesh=plsc.ScalarSubcoreMesh(axis_name="sequencer", num_cores=1),
        scratch_types=[pltpu.SemaphoreType.DMA((RING_SEMS,)), pltpu.SemaphoreType.DMA((RING_SEMS,)),
                       pltpu.SemaphoreType.DMA((1,))],
        compiler_params=pltpu.CompilerParams(collective_id=collective_id))(*arrs)


RING_SEMS = 8


def _ring_gather_body(n, per_array_sems, handshake, layouts):
    def body(*refs):
        ins, outs = refs[:n], refs[n:2 * n]
        send_sems, recv_sems, local_sems = refs[2 * n:]
        x, y, c = lax.axis_index("x"), lax.axis_index("y"), lax.axis_index("c")
        me, sibling = (x, y, c), (x, y, 1 - c)
        over_x, over_y, diagonal = (1 - x, y, c), (x, 1 - y, c), (1 - x, 1 - y, c)
        if handshake:
            _handshake([sibling, over_x, over_y])

        def sem(sems, a, k):
            return sems.at[a, k] if per_array_sems else sems.at[k]

        def slot(a, p, half=None):
            block = 4 * p[0] + 2 * p[1] + p[2]
            r, c_ = ins[a].shape
            first, count = (0, r) if half is None else (half * (r // 2), r // 2)
            if layouts[a] == "rows":
                return outs[a].at[pl.ds(block * r + first, count), :]
            if layouts[a] == "cols":
                return outs[a].at[pl.ds(first, count), pl.ds(block * c_, c_)]
            return outs[a].at[block, pl.ds(first, count), :]

        def copy(a, k, block, to, half=None, src=None):
            return pltpu.make_async_remote_copy(
                src_ref=slot(a, block, half) if src is None else src, dst_ref=slot(a, block, half),
                send_sem=sem(send_sems, a, k), recv_sem=sem(recv_sems, a, k), device_id=to, device_id_type=MESH)

        def start(copies):
            for cp in copies:
                cp.start()
            return copies

        def arrived(k, block, half=None):
            for a in range(n):
                copy(a, k, block, me, half).wait_recv()

        arrays = range(n)
        mine = start([pltpu.make_async_copy(ins[a], slot(a, me), local_sems.at[a if per_array_sems else 0]) for a in arrays])
        sent = start([copy(a, k, me, to, src=ins[a]) for a in arrays for k, to in ((0, sibling), (1, over_x), (2, over_y))])
        arrived(1, over_x)
        sent += start([copy(a, 3, over_x, over_y, half=0) for a in arrays] + [copy(a, 5, over_x, sibling) for a in arrays])
        arrived(2, over_y)
        sent += start([copy(a, 4, over_y, over_x, half=1) for a in arrays] + [copy(a, 6, over_y, sibling) for a in arrays])
        arrived(3, diagonal, half=0)
        arrived(4, diagonal, half=1)
        sent += start([copy(a, 7, diagonal, sibling) for a in arrays])
        arrived(0, sibling)
        for k, chip in ((5, over_x), (6, over_y), (7, diagonal)):
            arrived(k, (chip[0], chip[1], 1 - c))
        for cp in sent:
            cp.wait_send()
        for cp in mine:
            cp.wait()

    return body


def _all_gather_async(arrs, name, collective_id, layouts):
    shape = dict(stack=lambda r, c: (N_DEV, r, c), rows=lambda r, c: (N_DEV * r, c), cols=lambda r, c: (r, N_DEV * c))
    out_type = [jax.ShapeDtypeStruct(shape[lay](*a.shape), a.dtype) for a, lay in zip(arrs, layouts)]
    return _on_sequencer(_ring_gather_body(len(arrs), False, True, layouts), arrs, out_type, name, collective_id)


def _sibling_send_async(arrs, name, collective_id):
    out_type = [jax.ShapeDtypeStruct((N_DEV // 2,) + a.shape[1:], a.dtype) for a in arrs]
    return _on_sequencer(_sibling_send_body(len(arrs), False, True), arrs, out_type, name, collective_id)


def _chip_exchange_async(arrs, name, collective_id):
    out_type = [jax.ShapeDtypeStruct(a.shape, a.dtype) for a in arrs]
    return _on_sequencer(_chip_exchange_body(len(arrs), False, True), arrs, out_type, name, collective_id)


def _pair_sum(name, parity, mine, theirs):
    n = len(mine)

    def body(par_ref, *refs):
        for a in range(n):
            refs[2 * n + a][...] = (refs[a][...].astype(F32) + refs[n + a][...].astype(F32)).astype(refs[2 * n + a].dtype)

    def block(a):
        return (None,) + a.shape[1:]

    grid_spec = pltpu.PrefetchScalarGridSpec(
        num_scalar_prefetch=1, grid=(N_DEV // 2,),
        in_specs=[pl.BlockSpec(block(a), lambda q, par: (2 * q + par[0], 0, 0)) for a in mine]
        + [pl.BlockSpec(block(a), lambda q, par: (q, 0, 0)) for a in theirs],
        out_specs=[pl.BlockSpec(block(a), lambda q, par: (q, 0, 0)) for a in theirs])
    return pl.pallas_call(body, name=name, grid_spec=grid_spec, out_shape=[jax.ShapeDtypeStruct(a.shape, a.dtype) for a in theirs],
                          compiler_params=_params())(parity, *mine, *theirs)


NN = (((1,), (0,)), ((), ()))
NT = (((1,), (1,)), ((), ()))
TN = (((0,), (0,)), ((), ()))


def _matmul(name, a, b, dims, grid, a_spec, b_spec, outs, epilogue, extras=(), prologue=None):
    n_in = 2 + len(extras)
    n_out = len(outs)

    def body(*refs):
        a_ref, b_ref = refs[0], refs[1]
        ex, out_refs = refs[2:n_in], refs[n_in:n_in + n_out]
        a_tile = a_ref[...] if prologue is None else prologue(a_ref[...])
        epilogue(lax.dot_general(a_tile, b_ref[...], dims, preferred_element_type=F32), ex, out_refs)

    return pl.pallas_call(
        body, name=name, grid=grid, in_specs=[a_spec, b_spec] + [s for _, s in extras], out_specs=[s for _, _, s in outs],
        out_shape=[jax.ShapeDtypeStruct(sh, dt) for sh, dt, _ in outs], compiler_params=_params(),
    )(a, b, *[e for e, _ in extras])


def _square(tile):
    return tile * tile


def _ep_store(acc, ex, outs):
    outs[0][...] = acc.astype(outs[0].dtype)


def _ep_residual(acc, ex, outs):
    outs[0][...] = acc.astype(outs[0].dtype)
    outs[1][...] = ex[0][...] + ex[1][0:1, :] * acc


def _mm_tokens(name, a, w, transposed, out_dtypes, epilogue=_ep_store, extras=(), prologue=None, rows=MM_TOKENS):
    m, n = a.shape[0], w.shape[0 if transposed else 1]
    tm = _tile(m, rows)
    rows_spec = lambda width: pl.BlockSpec((tm, width), lambda i, j, kk: (i, 0))
    whole = lambda arr: pl.BlockSpec(arr.shape, lambda i, j, kk: (0, 0))
    return _matmul(name, a, w, NT if transposed else NN, (m // tm, 1, 1), rows_spec(a.shape[1]), whole(w),
                   [((m, n), dt, rows_spec(n)) for dt in out_dtypes], epilogue,
                   extras=[(e.table, _group_spec(e, 3)) if isinstance(e, Group) else (e, rows_spec(n)) for e in extras],
                   prologue=prologue)


def _resident(w):
    return pl.BlockSpec(w.shape, lambda i: (0, 0), pipeline_mode=pl.Buffered(1))


def _mlp_forward(name, h, w1, w2, x, gate):
    m, d = x.shape
    f = w1.shape[1]
    tm = _tile(m, MLP_TOKENS)
    rows = lambda width: pl.BlockSpec((tm, width), lambda i: (i, 0))

    def body(h_ref, w1_ref, w2_ref, x_ref, g_ref, r_ref, ff_ref, out_ref):
        r = jnp.maximum(lax.dot_general(h_ref[...], w1_ref[...], NN, preferred_element_type=F32), 0.0).astype(r_ref.dtype)
        r_ref[...] = r
        ff = lax.dot_general(r * r, w2_ref[...], NN, preferred_element_type=F32)
        ff_ref[...] = ff.astype(ff_ref.dtype)
        out_ref[...] = x_ref[...] + g_ref[0:1, :] * ff

    return pl.pallas_call(
        body, name=name, grid=(m // tm,), in_specs=[rows(d), _resident(w1), _resident(w2), rows(d), _group_spec(gate)],
        out_specs=[rows(f), rows(d), rows(d)],
        out_shape=[jax.ShapeDtypeStruct((m, f), MXU), jax.ShapeDtypeStruct((m, d), MXU), jax.ShapeDtypeStruct((m, d), F32)],
        compiler_params=_params())(h, w1, w2, x, gate.table)


def _mlp_backward(name, dff, r, w1, w2, x, vec, dres, y, gate):
    m, d = dff.shape
    f = r.shape[1]
    tm = _tile(m, MLP_TOKENS)
    nt = m // tm
    rows = lambda width: pl.BlockSpec((tm, width), lambda i: (i, 0))

    def body(dff_ref, r_ref, w1_ref, w2_ref, x_ref, v_ref, dres_ref, y_ref, g_ref, da_ref, dx_ref, acc_ref, dy_ref):
        ds = lax.dot_general(dff_ref[...], w2_ref[...], NT, preferred_element_type=F32)
        da = (ds * (2.0 * r_ref[...].astype(F32))).astype(da_ref.dtype)
        da_ref[...] = da
        dh = lax.dot_general(da, w1_ref[...], NT, preferred_element_type=F32)
        _norm_grad_step(nt, dh, x_ref, v_ref, dres_ref, dx_ref, acc_ref, (y_ref, g_ref, dy_ref))

    return pl.pallas_call(
        body, name=name, grid=(nt,),
        in_specs=[rows(d), rows(f), _resident(w1), _resident(w2), rows(d), _group_spec(vec), rows(d), rows(d), _group_spec(gate)],
        out_specs=[rows(f), rows(d), _vec_spec(), rows(d)],
        out_shape=[jax.ShapeDtypeStruct((m, f), MXU), jax.ShapeDtypeStruct((m, d), F32),
                   jax.ShapeDtypeStruct((SUBLANES, d), F32), jax.ShapeDtypeStruct((m, d), MXU)],
        compiler_params=_params())(dff, r, w1, w2, x, vec.table, dres, y, gate.table)


def _mm_norm_grad(name, a, w, x, vec, dres, y=None, gate=None):
    m = a.shape[0]
    tm = _tile(m, MM_TOKENS)
    nt = m // tm
    branch = y is not None
    rows = lambda width: pl.BlockSpec((tm, width), lambda i: (i, 0))

    def body(a_ref, w_ref, x_ref, v_ref, dres_ref, *rest):
        dh = lax.dot_general(a_ref[...], w_ref[...], NT, preferred_element_type=F32)
        dx_ref, acc_ref = rest[2 * branch], rest[2 * branch + 1]
        _norm_grad_step(nt, dh, x_ref, v_ref, dres_ref, dx_ref, acc_ref, (rest[0], rest[1], rest[4]) if branch else None)

    return pl.pallas_call(
        body, name=name, grid=(nt,),
        in_specs=[rows(a.shape[1]), _resident(w), rows(D), _group_spec(vec), rows(D)] + ([rows(D), _group_spec(gate)] if branch else []),
        out_specs=[rows(D), _vec_spec()] + ([rows(D)] if branch else []),
        out_shape=[jax.ShapeDtypeStruct((m, D), F32), jax.ShapeDtypeStruct((SUBLANES, D), F32)]
        + ([jax.ShapeDtypeStruct((m, D), MXU)] if branch else []),
        compiler_params=_params())(a, w, x, vec.table, dres, *([y, gate.table] if branch else []))


def _mm_grad(name, a, b, shard_cols, prologue=None):
    t, m = a.shape
    n = b.shape[1]
    tm = _tile(m, MM_TILE)
    if shard_cols:
        tn = n // N_DEV
        out = ((N_DEV, m, tn), MXU, pl.BlockSpec((None, tm, tn), lambda i, j, kk: (j, i, 0)))
    else:
        tn = _tile(n, MM_TILE)
        out = ((m, n), MXU, pl.BlockSpec((tm, tn), lambda i, j, kk: (i, j)))
    return _matmul(name, a, b, TN, (m // tm, n // tn, 1), pl.BlockSpec((t, tm), lambda i, j, kk: (0, i)),
                   pl.BlockSpec((t, tn), lambda i, j, kk: (0, j)), [out], _ep_store, prologue=prologue)[0]


def _rows_spec(tt, width=D):
    return pl.BlockSpec((tt, width), lambda i: (i, 0))


def _vec_spec(rows=SUBLANES, width=D):
    return pl.BlockSpec((rows, width), lambda i: (0, 0))


Group = collections.namedtuple("Group", ["table", "index"])


def _group_spec(group, grid_rank=1):
    if grid_rank == 1:
        return pl.BlockSpec((SUBLANES, D), lambda i: (group.index, 0))
    return pl.BlockSpec((SUBLANES, D), lambda i, j, kk: (group.index, 0))


def _norm_mod(name, x, vec):
    t = x.shape[0]
    tt = _tile(t, ROW_TILE)

    def body(x_ref, v_ref, h_ref):
        xv = x_ref[...]
        r = lax.rsqrt(jnp.mean(xv * xv, axis=-1, keepdims=True) + NORM_EPS)
        h = (xv * r) * v_ref[0:1, :]
        h_ref[...] = (h * (1.0 + v_ref[1:2, :]) + v_ref[2:3, :]).astype(h_ref.dtype)

    return pl.pallas_call(body, name=name, grid=(t // tt,), in_specs=[_rows_spec(tt), _group_spec(vec)], out_specs=_rows_spec(tt),
                          out_shape=jax.ShapeDtypeStruct((t, D), MXU))(x, vec.table)


GATE_ROW = 3


def _branch_grad(dx, y_ref, g_ref, dy_ref, acc_ref):
    dy_ref[...] = (dx * g_ref[0:1, :]).astype(dy_ref.dtype)
    acc_ref[GATE_ROW:GATE_ROW + 1, :] += _row_sum(dx * y_ref[...])


def _norm_grad_step(nt, dh, x_ref, v_ref, dres_ref, dx_ref, acc_ref, branch):
    i = pl.program_id(0)

    @pl.when(i == 0)
    def _():
        acc_ref[...] = jnp.zeros_like(acc_ref)

    xv = x_ref[...]
    r = lax.rsqrt(jnp.mean(xv * xv, axis=-1, keepdims=True) + NORM_EPS)
    xn = xv * r
    w = v_ref[0:1, :] * (1.0 + v_ref[1:2, :])
    acc_ref[0:1, :] += _row_sum(dh * xn)
    acc_ref[2:3, :] += _row_sum(dh)
    dxn = dh * w
    dx = dres_ref[...] + r * (dxn - xn * jnp.mean(dxn * xn, axis=-1, keepdims=True))
    dx_ref[...] = dx
    if branch is not None:
        _branch_grad(dx, branch[0], branch[1], branch[2], acc_ref)

    @pl.when(i == nt - 1)
    def _():
        dw = acc_ref[0:1, :]
        acc_ref[1:2, :] = dw * v_ref[0:1, :]
        acc_ref[0:1, :] = dw * (1.0 + v_ref[1:2, :])


def _loss_head(name, x, target, vec, y, gate):
    t = x.shape[0]
    tt = _tile(t, ROW_TILE)

    def body(x_ref, t_ref, v_ref, y_ref, g_ref, dx_ref, acc_ref, dy_ref):
        @pl.when(pl.program_id(0) == 0)
        def _():
            acc_ref[...] = jnp.zeros_like(acc_ref)

        xv = x_ref[...]
        r = lax.rsqrt(jnp.mean(xv * xv, axis=-1, keepdims=True) + NORM_EPS)
        xn = xv * r
        gain = v_ref[0:1, :]
        err = xn * gain - t_ref[...]
        acc_ref[1:2, :] += _row_sum(err * err) * (0.5 / D)
        dout = err * (1.0 / D)
        acc_ref[0:1, :] += _row_sum(dout * xn)
        dxn = dout * gain
        dx = r * (dxn - xn * jnp.mean(dxn * xn, axis=-1, keepdims=True))
        dx_ref[...] = dx
        _branch_grad(dx, y_ref, g_ref, dy_ref, acc_ref)

    return pl.pallas_call(
        body, name=name, grid=(t // tt,),
        in_specs=[_rows_spec(tt), _rows_spec(tt), _group_spec(vec), _rows_spec(tt), _group_spec(gate)],
        out_specs=[_rows_spec(tt), _vec_spec(), _rows_spec(tt)],
        out_shape=[jax.ShapeDtypeStruct((t, D), F32), jax.ShapeDtypeStruct((SUBLANES, D), F32),
                   jax.ShapeDtypeStruct((t, D), MXU)])(x, target, vec.table, y, gate.table)


def _shift_down(x, halo, k):
    y = pltpu.roll(x, k, 0)
    top = jnp.where(lax.broadcasted_iota(jnp.int32, halo.shape, 0) < k, pltpu.roll(halo, k, 0), y[0:SUBLANES, :])
    return jnp.concatenate([top, y[SUBLANES:, :]], axis=0)


def _shift_up(x, halo, k):
    n = x.shape[0]
    y = pltpu.roll(x, n - k, 0)
    bottom = jnp.where(lax.broadcasted_iota(jnp.int32, halo.shape, 0) >= SUBLANES - k, pltpu.roll(halo, SUBLANES - k, 0),
                       y[n - SUBLANES:, :])
    return jnp.concatenate([y[:n - SUBLANES, :], bottom], axis=0)


def _rg_gates(xb, halo, cw_ref, vec_ref, wa_ref, wx_ref, at_start):
    shifted = [xb] + [_shift_down(xb, halo, k) for k in range(1, CONV_WIDTH)]
    xc = vec_ref[0:1, :] + shifted[0] * cw_ref[CONV_WIDTH - 1:CONV_WIDTH, :]
    for k in range(1, CONV_WIDTH):
        xc = xc + shifted[k] * cw_ref[CONV_WIDTH - 1 - k:CONV_WIDTH - k, :]
    heads = [slice(h * HEAD_DIM, (h + 1) * HEAD_DIM) for h in range(HEADS)]
    pa = jnp.concatenate([_dot(xc[:, s], wa_ref[h]) for h, s in enumerate(heads)], axis=1) + vec_ref[1:2, :]
    px = jnp.concatenate([_dot(xc[:, s], wx_ref[h]) for h, s in enumerate(heads)], axis=1) + vec_ref[2:3, :]
    ra, ia = _sigmoid(pa), _sigmoid(px)
    nl = -vec_ref[3:4, :]
    sp = jnp.maximum(nl, 0.0) + jnp.log(1.0 + jnp.exp(-jnp.abs(nl)))
    log_a = (-LRU_C) * ra * sp
    a = jnp.exp(log_a)
    th = jnp.tanh(log_a)
    is_t0 = jnp.logical_and(lax.broadcasted_iota(jnp.int32, xb.shape, 0) == 0, at_start)
    mult = jnp.where(is_t0, 1.0, jnp.sqrt(-2.0 * th / (1.0 - th)))
    return dict(shifted=shifted, xc=xc, ra=ra, ia=ia, sp=sp, a=a, mult=mult, is_t0=is_t0, heads=heads)


def _rg_specs(tt, nt, order, cw, vec):
    blk = tt // SUBLANES
    return dict(
        x=pl.BlockSpec((tt, D), lambda i: (order(i), 0)), y=pl.BlockSpec((tt, D), lambda i: (order(i), 1)),
        halo=pl.BlockSpec((SUBLANES, D), lambda i: (jnp.maximum(order(i) * blk - 1, 0), 0)),
        cw=_group_spec(cw), vec=_group_spec(vec), w=pl.BlockSpec((HEADS, HEAD_DIM, HEAD_DIM), lambda i: (0, 0, 0)))


def _rg_forward(name, z, cw, vec, wa, wx):
    t = z.shape[0]
    tt = _tile(t, RG_ROWS)
    nt = t // tt
    sp = _rg_specs(tt, nt, lambda i: i, cw, vec)

    def body(zx_ref, zy_ref, halo_ref, cw_ref, vec_ref, wa_ref, wx_ref, p_ref, h_ref, a_s, u_s, carry):
        i = pl.program_id(0)

        @pl.when(i == 0)
        def _():
            carry[...] = jnp.zeros_like(carry)

        halo = jnp.where(i > 0, halo_ref[...], 0.0)
        g = _rg_gates(zx_ref[...], halo, cw_ref, vec_ref, wa_ref, wx_ref, i == 0)
        a_s[...] = g["a"]
        u_s[...] = g["mult"] * (g["ia"] * g["xc"])

        def group(gi, h):
            rows = pl.ds(pl.multiple_of(gi * SUBLANES, SUBLANES), SUBLANES)
            a8, u8 = a_s[rows, :], u_s[rows, :]
            out = []
            for j in range(SUBLANES):
                h = a8[j:j + 1, :] * h + u8[j:j + 1, :]
                out.append(h)
            h_ref[rows, :] = jnp.concatenate(out, axis=0)
            return h

        carry[0:1, :] = lax.fori_loop(0, tt // SUBLANES, group, carry[0:1, :])
        p_ref[...] = (h_ref[...] * _gelu(zy_ref[...])).astype(p_ref.dtype)

    return pl.pallas_call(
        body, name=name, grid=(nt,), in_specs=[sp["x"], sp["y"], sp["halo"], sp["cw"], sp["vec"], sp["w"], sp["w"]],
        out_specs=[_rows_spec(tt), _rows_spec(tt)],
        out_shape=[jax.ShapeDtypeStruct((t, D), MXU), jax.ShapeDtypeStruct((t, D), F32)],
        scratch_shapes=[pltpu.VMEM((tt, D), F32), pltpu.VMEM((tt, D), F32), pltpu.VMEM((SUBLANES, D), F32)],
        compiler_params=_params())(z, z, z, cw.table, vec.table, wa, wx)


def _rg_backward(name, dp, z, h, cw, vec, wa, wx):
    t = z.shape[0]
    tt = _tile(t, RG_ROWS)
    nt = t // tt
    rev = lambda i: nt - 1 - i
    sp = _rg_specs(tt, nt, rev, cw, vec)
    rows_rev = pl.BlockSpec((tt, D), lambda i: (rev(i), 0))

    def body(dp_ref, zx_ref, zy_ref, halo_ref, h_ref, hhalo_ref, cw_ref, vec_ref, wa_ref, wx_ref,
             dz_ref, dvec_ref, dcw_ref, dwa_ref, dwx_ref, a_s, d_s, carry, nxt):
        i = pl.program_id(0)
        j = rev(i)

        @pl.when(i == 0)
        def _():
            carry[...] = jnp.zeros_like(carry)
            nxt[...] = jnp.zeros_like(nxt)
            dvec_ref[...] = jnp.zeros_like(dvec_ref)
            dcw_ref[...] = jnp.zeros_like(dcw_ref)
            dwa_ref[...] = jnp.zeros_like(dwa_ref)
            dwx_ref[...] = jnp.zeros_like(dwx_ref)

        halo = jnp.where(j > 0, halo_ref[...], 0.0)
        g = _rg_gates(zx_ref[...], halo, cw_ref, vec_ref, wa_ref, wx_ref, j == 0)
        xc, ra, ia, a, mult = g["xc"], g["ra"], g["ia"], g["a"], g["mult"]
        hv, zy, dpv = h_ref[...], zy_ref[...], dp_ref[...]
        gelu, gelu_grad = _gelu_and_grad(zy)
        dyb = dpv * hv * gelu_grad
        a_s[...] = a
        d_s[...] = dpv * gelu

        def group(gi, c):
            rows = pl.ds(pl.multiple_of((tt // SUBLANES - 1 - gi) * SUBLANES, SUBLANES), SUBLANES)
            a8, d8 = a_s[rows, :], d_s[rows, :]
            out = [None] * SUBLANES
            for r in reversed(range(SUBLANES)):
                dht = d8[r:r + 1, :] + c
                out[r] = dht
                c = a8[r:r + 1, :] * dht
            d_s[rows, :] = jnp.concatenate(out, axis=0)
            return c

        carry[0:1, :] = lax.fori_loop(0, tt // SUBLANES, group, carry[0:1, :])
        dht = d_s[...]
        hprev = _shift_down(hv, jnp.where(j > 0, hhalo_ref[...], 0.0), 1)
        ixc = ia * xc
        dlog_a = dht * hprev * a + jnp.where(g["is_t0"], 0.0, dht * ixc * (-(a * a) / mult))
        dia = dht * mult * xc
        dxc = dht * mult * ia
        dra = dlog_a * ((-LRU_C) * g["sp"])
        nl = -vec_ref[3:4, :]
        dvec_ref[3:4, :] += _row_sum(dlog_a * ((-LRU_C) * ra)) * (-_sigmoid(nl))
        dpa = dra * ra * (1.0 - ra)
        dpx = dia * ia * (1.0 - ia)
        dvec_ref[1:2, :] += _row_sum(dpa)
        dvec_ref[2:3, :] += _row_sum(dpx)
        xc_m, dpa_m, dpx_m = xc.astype(MXU), dpa.astype(MXU), dpx.astype(MXU)
        back = [_dot_nt(dpa_m[:, s], wa_ref[hd]) + _dot_nt(dpx_m[:, s], wx_ref[hd]) for hd, s in enumerate(g["heads"])]
        dwa = [_dot_tn(xc_m[:, s], dpa_m[:, s]) for s in g["heads"]]
        dwx = [_dot_tn(xc_m[:, s], dpx_m[:, s]) for s in g["heads"]]
        for hd in range(HEADS):
            dwa_ref[hd] += dwa[hd]
            dwx_ref[hd] += dwx[hd]
        dxc = dxc + jnp.concatenate(back, axis=1)
        dvec_ref[0:1, :] += _row_sum(dxc)
        dxb = dxc * cw_ref[CONV_WIDTH - 1:CONV_WIDTH, :]
        for k in range(CONV_WIDTH):
            row = CONV_WIDTH - 1 - k
            dcw_ref[row:row + 1, :] += _row_sum(dxc * g["shifted"][k])
            if k:
                dxb = dxb + _shift_up(dxc, nxt[...], k) * cw_ref[row:row + 1, :]
        nxt[...] = dxc[0:SUBLANES, :]
        dz_ref[:, 0:D] = dxb.astype(dz_ref.dtype)
        dz_ref[:, D:2 * D] = dyb.astype(dz_ref.dtype)

    hhalo = pl.BlockSpec((SUBLANES, D), lambda i: (jnp.maximum(rev(i) * (tt // SUBLANES) - 1, 0), 0))
    wacc = pl.BlockSpec((HEADS, HEAD_DIM, HEAD_DIM), lambda i: (0, 0, 0))
    return pl.pallas_call(
        body, name=name, grid=(nt,),
        in_specs=[rows_rev, sp["x"], sp["y"], sp["halo"], rows_rev, hhalo, sp["cw"], sp["vec"], sp["w"], sp["w"]],
        out_specs=[pl.BlockSpec((tt, 2 * D), lambda i: (rev(i), 0)), _vec_spec(), _vec_spec(), wacc, wacc],
        out_shape=[jax.ShapeDtypeStruct((t, 2 * D), MXU), jax.ShapeDtypeStruct((SUBLANES, D), F32),
                   jax.ShapeDtypeStruct((SUBLANES, D), F32), jax.ShapeDtypeStruct((HEADS, HEAD_DIM, HEAD_DIM), F32),
                   jax.ShapeDtypeStruct((HEADS, HEAD_DIM, HEAD_DIM), F32)],
        scratch_shapes=[pltpu.VMEM((tt, D), F32), pltpu.VMEM((tt, D), F32), pltpu.VMEM((SUBLANES, D), F32),
                        pltpu.VMEM((SUBLANES, D), F32)],
        compiler_params=_params())(dp, z, z, z, h, h, cw.table, vec.table, wa, wx)


def _hg_chunk(zq, zf, zi, lb):
    c = HG_CHUNK
    q, q_grad = _silu_and_grad(zq)
    sg = _sigmoid(zf)
    fg = lb + (1.0 - lb) * sg
    k = 1.0 - fg
    row, col = lax.broadcasted_iota(jnp.int32, (c, c), 0), lax.broadcasted_iota(jnp.int32, (c, c), 1)
    tri, tri_t = (row >= col).astype(F32), (row <= col).astype(F32)
    b = _dot_exact(tri, jnp.log(fg))
    mid, last = b[c // 2 - 1:c // 2, :], b[c - 1:c, :]
    eq = jnp.exp(jnp.minimum(b - mid, EXP_CLAMP))
    ek = jnp.exp(jnp.minimum(mid - b, EXP_CLAMP))
    eb = jnp.exp(b)
    ed = jnp.exp(last - b)
    return dict(q=q, q_grad=q_grad, sg=sg, fg=fg, k=k, v=zi, eq=eq, ek=ek, eb=eb, ed=ed, elast=jnp.exp(last), tri=tri, tri_t=tri_t,
                qe=q * eq, ke=k * ek, qb=q * eb, kd=k * ed)


def _hg_specs(tt, order):
    return [pl.BlockSpec((tt, D), lambda i, col=col: (order(i), col)) for col in range(4)]


def _hg_forward(name, z, vec):
    t = z.shape[0]
    tt = _tile(t, ROW_TILE)
    nt, nc = t // tt, tt // HG_CHUNK
    heads = [slice(h * HEAD_DIM, (h + 1) * HEAD_DIM) for h in range(HEADS)]

    def body(zq_ref, zf_ref, zi_ref, zg_ref, vec_ref, p_ref, o_ref, ss_ref, state):
        @pl.when(pl.program_id(0) == 0)
        def _():
            state[...] = jnp.zeros_like(state)

        lb, gain = vec_ref[0:1, :], vec_ref[1:2, :]

        def chunk(ci, carry):
            rows = pl.ds(pl.multiple_of(ci * HG_CHUNK, HG_CHUNK), HG_CHUNK)
            g = _hg_chunk(zq_ref[rows, :], zf_ref[rows, :], zi_ref[rows, :], lb)
            zg = zg_ref[rows, :]
            causal = g["tri"] > 0.0
            qe_m, ke_m, qb_m, kd_m, v_m = (g[n].astype(MXU) for n in ("qe", "ke", "qb", "kd", "v"))
            atts = [_dot_nt(qe_m[:, s], ke_m[:, s]) for s in heads]
            o_parts, p_parts = [], []
            for hd, s in enumerate(heads):
                st = state[hd]
                ss_ref[ci, hd] = st
                att = jnp.where(causal, atts[hd], 0.0)
                o_parts.append(_dot(att, v_m[:, s]) + _dot_nt(qb_m[:, s], st))
                state[hd] = st * g["elast"][:, s] + _dot_tn(v_m[:, s], kd_m[:, s])
            for o, s in zip(o_parts, heads):
                r = lax.rsqrt(jnp.mean(o * o, axis=-1, keepdims=True) + GNORM_EPS)
                p_parts.append((o * r) * gain[:, s])
            o_ref[rows, :] = jnp.concatenate(o_parts, axis=1)
            p_ref[rows, :] = (jnp.concatenate(p_parts, axis=1) * _silu(zg)).astype(p_ref.dtype)
            return carry

        lax.fori_loop(0, nc, chunk, 0, unroll=True)

    return pl.pallas_call(
        body, name=name, grid=(nt,), in_specs=_hg_specs(tt, lambda i: i) + [_group_spec(vec)],
        out_specs=[_rows_spec(tt), _rows_spec(tt), pl.BlockSpec((nc, HEADS, HEAD_DIM, HEAD_DIM), lambda i: (i, 0, 0, 0))],
        out_shape=[jax.ShapeDtypeStruct((t, D), MXU), jax.ShapeDtypeStruct((t, D), F32),
                   jax.ShapeDtypeStruct((t // HG_CHUNK, HEADS, HEAD_DIM, HEAD_DIM), F32)],
        scratch_shapes=[pltpu.VMEM((HEADS, HEAD_DIM, HEAD_DIM), F32)], compiler_params=_params())(z, z, z, z, vec.table)


def _hg_backward(name, dp, z, o, ss, vec):
    t = z.shape[0]
    tt = _tile(t, ROW_TILE)
    nt, nc = t // tt, tt // HG_CHUNK
    rev = lambda i: nt - 1 - i
    heads = [slice(h * HEAD_DIM, (h + 1) * HEAD_DIM) for h in range(HEADS)]
    rows_rev = pl.BlockSpec((tt, D), lambda i: (rev(i), 0))

    def body(dp_ref, zq_ref, zf_ref, zi_ref, zg_ref, o_ref, ss_ref, vec_ref, dz_ref, acc_ref, dstate):
        @pl.when(pl.program_id(0) == 0)
        def _():
            dstate[...] = jnp.zeros_like(dstate)
            acc_ref[...] = jnp.zeros_like(acc_ref)

        lb, gain = vec_ref[0:1, :], vec_ref[1:2, :]

        def chunk(cr, carry):
            ci = nc - 1 - cr
            rows = pl.ds(pl.multiple_of(ci * HG_CHUNK, HG_CHUNK), HG_CHUNK)
            zq, zg = zq_ref[rows, :], zg_ref[rows, :]
            g = _hg_chunk(zq, zf_ref[rows, :], zi_ref[rows, :], lb)
            ov, dpv = o_ref[rows, :], dp_ref[rows, :]
            causal = g["tri"] > 0.0
            silu_g, silu_g_grad = _silu_and_grad(zg)
            don = dpv * silu_g
            dgate = dpv * silu_g_grad
            qe_m, ke_m, qb_m, kd_m, v_m = (g[n].astype(MXU) for n in ("qe", "ke", "qb", "kd", "v"))
            parts = {n: [] for n in ("dzg", "dgain", "dv", "dqe", "dke", "dqb", "dkd", "dlast")}
            dos, atts, datts = [], [], []
            for hd, s in enumerate(heads):
                oh = ov[:, s]
                r = lax.rsqrt(jnp.mean(oh * oh, axis=-1, keepdims=True) + GNORM_EPS)
                on = oh * r
                parts["dzg"].append(dgate[:, s] * (on * gain[:, s]))
                parts["dgain"].append(_row_sum(don[:, s] * on))
                dtmp = don[:, s] * gain[:, s]
                dos.append((r * (dtmp - on * jnp.mean(dtmp * on, axis=-1, keepdims=True))).astype(MXU))
            for hd, s in enumerate(heads):
                atts.append(_dot_nt(qe_m[:, s], ke_m[:, s]))
                datts.append(_dot_nt(dos[hd], v_m[:, s]))
            for hd, s in enumerate(heads):
                do, att, datt = dos[hd], jnp.where(causal, atts[hd], 0.0).astype(MXU), jnp.where(causal, datts[hd], 0.0).astype(MXU)
                st, dst = ss_ref[ci, hd], dstate[hd]
                st_m, dst_m = st.astype(MXU), dst.astype(MXU)
                qe, ke, qb, kd, v = qe_m[:, s], ke_m[:, s], qb_m[:, s], kd_m[:, s], v_m[:, s]
                parts["dv"].append(_dot_tn(att, do) + _dot_nt(kd, dst_m))
                parts["dqe"].append(_dot(datt, ke))
                parts["dke"].append(_dot_tn(datt, qe))
                parts["dqb"].append(_dot(do, st_m))
                parts["dkd"].append(_dot(v, dst_m))
                parts["dlast"].append(g["elast"][:, s] * _row_sum(dst * st))
                dstate[hd] = dst * g["elast"][:, s] + _dot_tn(do, qb)
            whole = {n: jnp.concatenate(p, axis=1) for n, p in parts.items()}
            acc_ref[1:2, :] += whole["dgain"]
            dqe, dke, dqb, dkd = whole["dqe"], whole["dke"], whole["dqb"], whole["dkd"]
            dq = dqe * g["eq"] + dqb * g["eb"]
            dk = dke * g["ek"] + dkd * g["ed"]
            dkdk = dkd * g["kd"]
            db = dqe * qe_m.astype(F32) - dke * ke_m.astype(F32) + dqb * g["qb"] - dkdk
            dlogf = _dot_exact(g["tri_t"], db) + (whole["dlast"] + _row_sum(dkdk))
            dfg = dlogf / g["fg"] - dk
            sg = g["sg"]
            acc_ref[0:1, :] += _row_sum(dfg * (1.0 - sg))
            dz_ref[rows, 0:D] = (dq * g["q_grad"]).astype(dz_ref.dtype)
            dz_ref[rows, D:2 * D] = (dfg * (1.0 - lb) * sg * (1.0 - sg)).astype(dz_ref.dtype)
            dz_ref[rows, 2 * D:3 * D] = whole["dv"].astype(dz_ref.dtype)
            dz_ref[rows, 3 * D:4 * D] = whole["dzg"].astype(dz_ref.dtype)
            return carry

        lax.fori_loop(0, nc, chunk, 0, unroll=True)

    return pl.pallas_call(
        body, name=name, grid=(nt,),
        in_specs=[rows_rev] + _hg_specs(tt, rev) + [rows_rev, pl.BlockSpec((nc, HEADS, HEAD_DIM, HEAD_DIM), lambda i: (rev(i), 0, 0, 0)),
                                                  _group_spec(vec)],
        out_specs=[pl.BlockSpec((tt, 4 * D), lambda i: (rev(i), 0)), _vec_spec()],
        out_shape=[jax.ShapeDtypeStruct((t, 4 * D), MXU), jax.ShapeDtypeStruct((SUBLANES, D), F32)],
        scratch_shapes=[pltpu.VMEM((HEADS, HEAD_DIM, HEAD_DIM), F32)], compiler_params=_params())(dp, z, z, z, z, o, ss, vec.table)


def _mod_forward(name, c_all, mod_w, lower):
    depth, _, sw = mod_w.shape

    def body(c_ref, w_ref, lo_ref, cs_ref, mod_ref, lb_ref):
        cs = _silu(c_ref[...])
        mod_ref[...] = _dot(cs, w_ref[...])

        @pl.when(pl.program_id(0) == 0)
        def _():
            cs_ref[...] = cs
            lo = lo_ref[...]
            e = jnp.exp(lo - jnp.max(lo, axis=0, keepdims=True))
            sm = e / jnp.sum(e, axis=0, keepdims=True)
            lb_ref[0:1, :] = jnp.zeros((1, D), F32)
            for l in range(1, depth):
                lb_ref[l:l + 1, :] = lb_ref[l - 1:l, :] + sm[l:l + 1, :]

    return pl.pallas_call(
        body, name=name, grid=(depth,),
        in_specs=[pl.BlockSpec((N_DEV, D), lambda l: (0, 0)), pl.BlockSpec((None, D, sw), lambda l: (l, 0, 0)),
                  pl.BlockSpec((depth, D), lambda l: (0, 0))],
        out_specs=[pl.BlockSpec((N_DEV, D), lambda l: (0, 0)), pl.BlockSpec((None, N_DEV, sw), lambda l: (l, 0, 0)),
                   pl.BlockSpec((depth, D), lambda l: (0, 0))],
        out_shape=[jax.ShapeDtypeStruct((N_DEV, D), F32), jax.ShapeDtypeStruct((depth, N_DEV, sw), F32),
                   jax.ShapeDtypeStruct((depth, D), F32)], compiler_params=_params())(c_all, mod_w, lower)


def _mod_weight_grad(name, cs_t, dmod):
    depth, pad, sw = dmod.shape

    def body(c_ref, d_ref, g_ref):
        g_ref[...] = _dot(c_ref[...], d_ref[...])

    return pl.pallas_call(
        body, name=name, grid=(depth,),
        in_specs=[pl.BlockSpec((D, pad), lambda l: (0, 0)), pl.BlockSpec((None, pad, sw), lambda l: (l, 0, 0))],
        out_specs=pl.BlockSpec((None, D, sw), lambda l: (l, 0, 0)),
        out_shape=jax.ShapeDtypeStruct((depth, D, sw), F32), compiler_params=_params())(cs_t, dmod)


def _lower_bound_grad(name, lower, dlb):
    depth = lower.shape[0]

    def body(lo_ref, d_ref, out_ref):
        lo = lo_ref[...]
        e = jnp.exp(lo - jnp.max(lo, axis=0, keepdims=True))
        sm = e / jnp.sum(e, axis=0, keepdims=True)
        out_ref[...] = jnp.zeros_like(out_ref)
        dsm = [jnp.zeros((1, D), F32)]
        for l in range(1, depth):
            tail = d_ref[l:l + 1, :]
            for m in range(l + 1, depth):
                tail = tail + d_ref[m:m + 1, :]
            dsm.append(tail)
        inner = sm[1:2, :] * dsm[1]
        for l in range(2, depth):
            inner = inner + sm[l:l + 1, :] * dsm[l]
        for l in range(depth):
            out_ref[l:l + 1, :] = sm[l:l + 1, :] * (dsm[l] - inner)
        for j in range(SUBLANES - depth):
            row = d_ref[depth + j:depth + j + 1, :]
            tot = row[:, 0:HEAD_DIM]
            for hd in range(1, HEADS):
                tot = tot + row[:, hd * HEAD_DIM:(hd + 1) * HEAD_DIM]
            out_ref[depth + j:depth + j + 1, 0:HEAD_DIM] = tot

    return pl.pallas_call(body, name=name, out_shape=jax.ShapeDtypeStruct((SUBLANES, D), F32))(lower, dlb)


def _adamw(name, parts, w, m, v, layer=None, prev=None):
    p, r, c = parts.shape
    tr = _tile(r, max(SUBLANES, ADAMW_STEP_BYTES // (4 * c * (p + 7))))
    stacked = layer is not None
    n_prev = 4 if prev is not None else 0

    def body(*refs):
        parts_ref, w_ref, m_ref, v_ref = refs[:4]
        g_ref, d_ref, m_out, v_out = refs[4 + n_prev:]
        g = parts_ref[0].astype(F32)
        for q in range(1, p):
            g = g + parts_ref[q].astype(F32)
        m2 = ADAM_B1 * m_ref[...] + (1.0 - ADAM_B1) * g
        v2 = ADAM_B2 * v_ref[...] + (1.0 - ADAM_B2) * (g * g)
        m_hat = m2 / (1.0 - ADAM_B1 ** ADAM_STEP)
        v_hat = v2 / (1.0 - ADAM_B2 ** ADAM_STEP)
        g_ref[...] = g
        d_ref[...] = -ADAM_LR * (m_hat / (jnp.sqrt(v_hat) + ADAM_EPS) + ADAM_WD * w_ref[...])
        m_out[...] = m2
        v_out[...] = v2

    if stacked:
        spec = pl.BlockSpec((None, tr, c), lambda i: (layer, i, 0))
    else:
        spec = pl.BlockSpec((tr, c), lambda i: (i, 0))
    return pl.pallas_call(
        body, name=name, grid=(r // tr,),
        in_specs=[pl.BlockSpec((p, tr, c), lambda i: (0, i, 0)), spec, spec, spec] + [ANY] * n_prev, out_specs=[spec] * 4,
        out_shape=[jax.ShapeDtypeStruct(w.shape, F32)] * 4,
        input_output_aliases={4 + q: q for q in range(n_prev)}, compiler_params=_params(),
    )(_hbm(parts), _hbm(w), _hbm(m), _hbm(v), *(prev or []))


def _hbm(a):
    return pltpu.with_memory_space_constraint(a, pltpu.HBM)


def _pad_rows(a, rows=SUBLANES):
    a = a.reshape(-1, a.shape[-1])
    return jnp.concatenate([a, jnp.zeros((rows - a.shape[0], a.shape[1]), a.dtype)], axis=0) if a.shape[0] < rows else a


def kernel(x, c, mod_w, mod_b, norm_mix, norm_mlp, norm_final, rg_w_in, rg_conv_w, rg_conv_b, rg_w_a, rg_b_a, rg_w_x, rg_b_x, rg_lambda, rg_w_out, hg_w_in, hg_lower_bounds, hg_gnorm, hg_w_out, mlp_w1, mlp_w2, loss_target, m_mod_w, m_mod_b, m_norm_mix, m_norm_mlp, m_norm_final, m_rg_w_in, m_rg_conv_w, m_rg_conv_b, m_rg_w_a, m_rg_b_a, m_rg_w_x, m_rg_b_x, m_rg_lambda, m_rg_w_out, m_hg_w_in, m_hg_lower_bounds, m_hg_gnorm, m_hg_w_out, m_mlp_w1, m_mlp_w2, v_mod_w, v_mod_b, v_norm_mix, v_norm_mlp, v_norm_final, v_rg_w_in, v_rg_conv_w, v_rg_conv_b, v_rg_w_a, v_rg_b_a, v_rg_w_x, v_rg_b_x, v_rg_lambda, v_rg_w_out, v_hg_w_in, v_hg_lower_bounds, v_hg_gnorm, v_hg_w_out, v_mlp_w1, v_mlp_w2):
    me = 4 * lax.axis_index("x") + 2 * lax.axis_index("y") + lax.axis_index("c")
    x0 = x[0]
    target = loss_target[0]
    n_rg, n_hg = rg_w_in.shape[0], hg_w_in.shape[0]
    sw_mod = mod_w.shape[2]

    c_all, cw_all = _all_gather([_pad_rows(c), rg_conv_w.reshape(n_rg * CONV_WIDTH, -1)], "gather_cond")
    c_all = c_all[:, 0, :]
    conv_w = cw_all.transpose(1, 0, 2).reshape(n_rg, CONV_WIDTH, D)
    cs_all, mod_part, lb_all = _mod_forward("mod_forward", c_all, mod_w, hg_lower_bounds)
    (mod_gathered,) = _all_gather([mod_part.reshape(DEPTH * N_DEV, sw_mod)], "gather_mod")

    ids = iter(range(5 * DEPTH))
    shards = []
    for layer in range(DEPTH):
        j = layer // 2
        w_in, w_out = (rg_w_in[j], rg_w_out[j]) if layer % 2 == 0 else (hg_w_in[j], hg_w_out[j])
        shards.append([w_in.astype(MXU), w_out.astype(MXU), mlp_w1[layer].astype(MXU), mlp_w2[layer].astype(MXU)])
    shards, mod_gathered = lax.optimization_barrier((shards, mod_gathered))
    weights = []
    for layer in range(DEPTH):
        g_in, g_out = _all_gather_async(shards[layer][:2], f"gather_mixer_weights_{layer}", next(ids), ["cols", "rows"])
        g_w1, g_w2 = _all_gather_async(shards[layer][2:], f"gather_mlp_weights_{layer}", next(ids), ["cols", "rows"])
        weights.append(dict(w_in=_hbm(g_in), w_out=_hbm(g_out), w1=_hbm(g_w1), w2=_hbm(g_w2)))

    mod_mine = lax.dynamic_index_in_dim(mod_gathered.reshape(N_DEV, DEPTH, N_DEV, sw_mod), me, axis=2, keepdims=False)
    mod = mod_mine.transpose(1, 0, 2).reshape(DEPTH, 6, D) + mod_b.reshape(DEPTH, 6, D)

    pieces, n_groups = [], 0
    placed = [dict() for _ in range(DEPTH)]

    def place(rows):
        nonlocal n_groups
        rows = [r.reshape(-1, D).astype(F32) for r in rows]
        pieces.extend(rows + [jnp.zeros((SUBLANES - sum(r.shape[0] for r in rows), D), F32)])
        n_groups += 1
        return n_groups - 1

    for layer in range(DEPTH):
        j, at = layer // 2, placed[layer]
        at["vec_mix"] = place([norm_mix[layer], mod[layer, 1], mod[layer, 0]])
        at["vec_mlp"] = place([norm_mlp[layer], mod[layer, 4], mod[layer, 3]])
        at["gate_mix"], at["gate_mlp"] = place([mod[layer, 2]]), place([mod[layer, 5]])
        if layer % 2 == 0:
            at["cw"] = place([conv_w[j]])
            at["vec"] = place([rg_conv_b[j], rg_b_a[j], rg_b_x[j], rg_lambda[j]])
        else:
            at["vec"] = place([lb_all[layer], jnp.tile(hg_gnorm[j], HEADS)])
    final_group = place([norm_final])
    table = _hbm(jnp.concatenate(pieces, axis=0))

    saved = []
    xl = x0
    for layer in range(DEPTH):
        j, wt = layer // 2, weights[layer]
        is_rg = layer % 2 == 0
        s = dict(x=xl, **{k: Group(table, g) for k, g in placed[layer].items()})
        s["h"] = _hbm(_norm_mod("norm_mod", xl, s["vec_mix"]))
        if is_rg:
            (s["z"],) = _mm_tokens("rg_in", s["h"], wt["w_in"], False, (F32,))
            s["wa"], s["wx"] = _hbm(rg_w_a[j].astype(MXU)), _hbm(rg_w_x[j].astype(MXU))
            s["p"], s["hr"] = _rg_forward("rg_forward", s["z"], s["cw"], s["vec"], s["wa"], s["wx"])
        else:
            (s["z"],) = _mm_tokens("hg_in", s["h"], wt["w_in"], False, (F32,))
            s["p"], s["o"], s["ss"] = _hg_forward("hg_forward", s["z"], s["vec"])
        s["y"], s["x1"] = _mm_tokens("mix_out", s["p"], wt["w_out"], False, (MXU, F32), _ep_residual, extras=(xl, s["gate_mix"]))
        s["p"] = _hbm(s["p"])
        s["h2"] = _hbm(_norm_mod("norm_mod", s["x1"], s["vec_mlp"]))
        s["r"], s["ff"], xl = _mlp_forward("mlp_forward", s["h2"], wt["w1"], wt["w2"], s["x1"], s["gate_mlp"])
        saved.append(s)

    dx, head, dff = _loss_head("loss_head", xl, target, Group(table, final_group), saved[-1]["ff"], saved[-1]["gate_mlp"])
    g_mlp = head[GATE_ROW:GATE_ROW + 1]

    results = {}
    big = dict(rg_w_in=(rg_w_in, m_rg_w_in, v_rg_w_in), rg_w_out=(rg_w_out, m_rg_w_out, v_rg_w_out),
               hg_w_in=(hg_w_in, m_hg_w_in, v_hg_w_in), hg_w_out=(hg_w_out, m_hg_w_out, v_hg_w_out),
               mlp_w1=(mlp_w1, m_mlp_w1, v_mlp_w1), mlp_w2=(mlp_w2, m_mlp_w2, v_mlp_w2))

    def update_layer(layer, landed):
        mixer = "rg" if layer % 2 == 0 else "hg"
        l_in, l_out, l_w1, l_w2 = landed
        for nm, parts, idx in ((f"{mixer}_w_in", l_in, layer // 2), (f"{mixer}_w_out", l_out, layer // 2),
                               ("mlp_w1", l_w1, layer), ("mlp_w2", l_w2, layer)):
            w, m, v = big[nm]
            results[nm] = _adamw(f"adamw_{nm}", parts, w, m, v, layer=idx, prev=results.get(nm))

    landed = [None] * DEPTH
    small = [None] * DEPTH
    parity = lax.axis_index("c").astype(jnp.int32).reshape(1)

    def send_chip_sums(layer, grads, from_sibling, anchor):
        sums = _pair_sum("pair_sum", parity, grads, from_sibling)
        sums, anchor = lax.optimization_barrier((sums, anchor))
        sums = [_hbm(a) for a in sums]
        landed[layer] = [_hbm(a) for a in _chip_exchange_async(sums, f"exchange_grads_{layer}", next(ids))]
        return anchor

    pending = None
    for layer in reversed(range(DEPTH)):
        j, wt, s = layer // 2, weights[layer], saved[layer]
        is_rg = layer % 2 == 0
        dff = _hbm(dff)
        da, dx1, n_mlp, dyb = _mlp_backward("mlp_backward", dff, s["r"], wt["w1"], wt["w2"], s["x1"], s["vec_mlp"], dx,
                                            s["y"], s["gate_mix"])
        if pending is not None:
            da = send_chip_sums(*pending, da)
        dw2 = _mm_grad("mlp_out_grad", s["r"], dff, False, prologue=_square).reshape(N_DEV, -1, D)
        dw1 = _mm_grad("mlp_in_grad", s["h2"], da, True)
        dyb = _hbm(dyb)
        (dp,) = _mm_tokens("mix_out_t", dyb, wt["w_out"], True, (F32,))
        dw_out = _hbm(_mm_grad("mix_out_grad", s["p"], dyb, False).reshape(N_DEV, -1, D))
        if is_rg:
            dz, dvec, dcw, dwa, dwx = _rg_backward("rg_backward", dp, s["z"], s["hr"], s["cw"], s["vec"], s["wa"], s["wx"])
            gate_grads, dz = lax.optimization_barrier(
                ([dwa.reshape(-1, HEAD_DIM).astype(MXU), dwx.reshape(-1, HEAD_DIM).astype(MXU)], dz))
            gate_parts = _all_gather_async(gate_grads, f"gather_gate_grads_{j}", next(ids), ["stack", "stack"])
            mixer_small = dict(dvec=dvec, dcw=dcw, gate_parts=gate_parts)
            dw_in = _mm_grad("rg_in_grad", s["h"], dz, True)
        else:
            dz, dvec = _hg_backward("hg_backward", dp, s["z"], s["o"], s["ss"], s["vec"])
            mixer_small = dict(dvec=dvec)
            dw_in = _mm_grad("hg_in_grad", s["h"], dz, True)
        grads = [dw_in, dw_out, dw1, dw2]
        pending = (layer, grads, _sibling_send_async(grads, f"pair_grads_{layer}", next(ids)))
        small[layer] = dict(g_mlp=g_mlp, n_mlp=n_mlp, g_mix=n_mlp[GATE_ROW:GATE_ROW + 1], **mixer_small)
        if layer:
            below = saved[layer - 1]
            dx, n_mix, dff = _mm_norm_grad("mixer_in_t", dz, wt["w_in"], s["x"], s["vec_mix"], dx1, below["ff"], below["gate_mlp"])
            g_mlp = n_mix[GATE_ROW:GATE_ROW + 1]
        else:
            dx, n_mix = _mm_norm_grad("mixer_in_t_first", dz, wt["w_in"], s["x"], s["vec_mix"], dx1)
        small[layer]["n_mix"] = n_mix
        if layer + 1 < DEPTH:
            stream = (dx, dff) if layer else (dx,)
            landed[layer + 1], stream = lax.optimization_barrier((landed[layer + 1], stream))
            if "gate_parts" in small[layer + 1]:
                small[layer + 1]["gate_parts"], stream = lax.optimization_barrier((small[layer + 1]["gate_parts"], stream))
            dx, dff = stream if layer else (stream[0], None)
            update_layer(layer + 1, landed[layer + 1])

    dlb_rows = [jnp.zeros((1, D), F32) if l % 2 == 0 else small[l]["dvec"][0:1] for l in range(DEPTH)]
    dgn_rows = [small[2 * j + 1]["dvec"][1:2] for j in range(n_hg)]
    lb_grad = _lower_bound_grad("lower_bound_grad", hg_lower_bounds, _pad_rows(jnp.concatenate(dlb_rows + dgn_rows, axis=0)))
    dmod = jnp.stack([jnp.concatenate([small[l]["n_mix"][2], small[l]["n_mix"][1], small[l]["g_mix"][0],
                                       small[l]["n_mlp"][2], small[l]["n_mlp"][1], small[l]["g_mlp"][0]]) for l in range(DEPTH)])
    groups = [
        dmod.reshape(DEPTH * 6, D),
        jnp.stack([small[l]["n_mix"][0] for l in range(DEPTH)]),
        jnp.stack([small[l]["n_mlp"][0] for l in range(DEPTH)]),
        head[0:1],
        jnp.stack([small[2 * j]["dvec"][0] for j in range(n_rg)]),
        jnp.stack([small[2 * j]["dvec"][1] for j in range(n_rg)]),
        jnp.stack([small[2 * j]["dvec"][2] for j in range(n_rg)]),
        jnp.stack([small[2 * j]["dvec"][3] for j in range(n_rg)]),
        lb_grad[0:DEPTH],
        jnp.concatenate([lb_grad[DEPTH + j:DEPTH + j + 1, 0:HEAD_DIM] for j in range(n_hg)]
                        + [jnp.zeros((1, D - n_hg * HEAD_DIM), F32)], axis=1),
        jnp.concatenate([small[2 * j]["dcw"][0:CONV_WIDTH] for j in range(n_rg)], axis=0),
        head[1:2],
    ]
    params = [mod_b, norm_mix, norm_mlp, norm_final, rg_conv_b, rg_b_a, rg_b_x, rg_lambda, hg_lower_bounds, hg_gnorm]
    moms = [m_mod_b, m_norm_mix, m_norm_mlp, m_norm_final, m_rg_conv_b, m_rg_b_a, m_rg_b_x, m_rg_lambda, m_hg_lower_bounds, m_hg_gnorm]
    vars_ = [v_mod_b, v_norm_mix, v_norm_mlp, v_norm_final, v_rg_conv_b, v_rg_b_a, v_rg_b_x, v_rg_lambda, v_hg_lower_bounds, v_hg_gnorm]
    offsets, rows_of, at = [], [], 0
    for g in groups:
        offsets.append(at)
        rows_of.append(g.shape[0])
        at += -(-g.shape[0] // SUBLANES) * SUBLANES
    packed = jnp.concatenate([_pad_rows(g, -(-g.shape[0] // SUBLANES) * SUBLANES) for g in groups], axis=0)
    (small_parts,) = _all_gather_async([packed], "gather_small_grads", next(ids), ["stack"])
    send_chip_sums(*pending, dx)

    def pack_like(arrs):
        out = []
        for g_rows, off, a in zip(rows_of, offsets, arrs):
            flat = a.reshape(-1)
            flat = jnp.concatenate([flat, jnp.zeros((g_rows * D - flat.shape[0],), F32)])
            out.append(_pad_rows(flat.reshape(g_rows, D), -(-g_rows // SUBLANES) * SUBLANES))
        rest = packed.shape[0] - sum(o.shape[0] for o in out)
        return jnp.concatenate(out + [jnp.zeros((rest, D), F32)], axis=0)

    small_out = _adamw("adamw_small", small_parts, pack_like(params), pack_like(moms), pack_like(vars_))

    def unpack(q, idx, like):
        rows = small_out[q][offsets[idx]:offsets[idx] + rows_of[idx]]
        return rows.reshape(-1)[:like.size].reshape(like.shape)

    loss = jnp.sum(small_out[0][offsets[11]])
    names =["mod_b", "norm_mix", "norm_mlp", "norm_final", "rg_conv_b", "rg_b_a", "rg_b_x", "rg_lambda", "hg_lower_bounds", "hg_gnorm"]
    for idx, (nm, like) in enumerate(zip(names, params)):
        results[nm] = [unpack(q, idx, like) for q in range(4)]

    cw_parts = lax.dynamic_slice_in_dim(small_parts[:, offsets[10]:offsets[10] + n_rg * CONV_WIDTH, :], me * (D // N_DEV), D // N_DEV, axis=2)
    shp = rg_conv_w.shape
    results["rg_conv_w"] = [o.reshape(shp) for o in _adamw(
        "adamw_conv", cw_parts, rg_conv_w.reshape(-1, shp[-1]), m_rg_conv_w.reshape(-1, shp[-1]), v_rg_conv_w.reshape(-1, shp[-1]))]
    shp = rg_w_a.shape
    stacked = (n_rg, HEADS * HEAD_DIM, HEAD_DIM)
    for nm, which, (w, m, v) in (("rg_w_a", 0, (rg_w_a, m_rg_w_a, v_rg_w_a)), ("rg_w_x", 1, (rg_w_x, m_rg_w_x, v_rg_w_x))):
        out = None
        for j in reversed(range(n_rg)):
            out = _adamw("adamw_gate", small[2 * j]["gate_parts"][which], w.reshape(stacked), m.reshape(stacked),
                         v.reshape(stacked), layer=j, prev=out)
        results[nm] = [o.reshape(shp) for o in out]

    dmod_all = small_parts[:, 0:DEPTH * 6, :].reshape(N_DEV, DEPTH, 6 * D)
    dmod_cols = lax.dynamic_slice_in_dim(dmod_all, me * sw_mod, sw_mod, axis=2).transpose(1, 0, 2)
    pad = HEAD_DIM - N_DEV
    dmod_pad = jnp.concatenate([dmod_cols, jnp.zeros((DEPTH, pad, sw_mod), F32)], axis=1).astype(MXU)
    cs_t = jnp.concatenate([cs_all.T, jnp.zeros((D, pad), F32)], axis=1).astype(MXU)
    g_mod_w = _mod_weight_grad("mod_weight_grad", cs_t, dmod_pad)
    results["mod_w"] = [o.reshape(mod_w.shape) for o in _adamw(
        "adamw_mod", g_mod_w.reshape(1, -1, sw_mod), mod_w.reshape(-1, sw_mod), m_mod_w.reshape(-1, sw_mod), v_mod_w.reshape(-1, sw_mod))]

    update_layer(0, landed[0])

    order = ["mod_w", "mod_b", "norm_mix", "norm_mlp", "norm_final", "rg_w_in", "rg_conv_w", "rg_conv_b", "rg_w_a", "rg_b_a", "rg_w_x",
             "rg_b_x", "rg_lambda", "rg_w_out", "hg_w_in", "hg_lower_bounds", "hg_gnorm", "hg_w_out", "mlp_w1", "mlp_w2"]
    return (loss, dx[None], *[results[n][0] for n in order], *[results[n][1] for n in order],
            *[results[n][2] for n in order], *[results[n][3] for n in order])
```

```python
import collections

import jax
import jax.numpy as jnp
from jax import lax
from jax.experimental import pallas as pl
from jax.experimental.pallas import tpu as pltpu
from jax.experimental.pallas import tpu_sc as plsc

F32 = jnp.float32
MXU = jnp.bfloat16

N_DEV = 8
D = 1024
DEPTH = 4
HEADS = 8
HEAD_DIM = 128
CONV_WIDTH = 4
LRU_C = 8.0
HG_CHUNK = 64
NORM_EPS = 1e-6
GNORM_EPS = 1e-5
ADAM_LR = 0.001
ADAM_B1 = 0.9
ADAM_B2 = 0.999
ADAM_EPS = 1e-08
ADAM_WD = 0.01
ADAM_STEP = 10
GELU_C = 0.7978845608028654
GELU_K = 0.044715
EXP_CLAMP = 80.0
SUBLANES = 8
VMEM_LIMIT = 48 * 1024 * 1024
ROW_TILE = 256
RG_ROWS = 256
MM_TILE = 1024
MM_TOKENS = 512
MLP_TOKENS = 256
ADAMW_STEP_BYTES = 8 * 1024 * 1024

MESH = pl.DeviceIdType.MESH
ANY = pl.BlockSpec(memory_space=pl.ANY)


def _params():
    return pltpu.CompilerParams(vmem_limit_bytes=VMEM_LIMIT)


def _tile(n, target):
    if n <= target:
        return n
    t = target // SUBLANES * SUBLANES
    while n % t:
        t -= SUBLANES
    return t


def _sigmoid(x):
    return 1.0 / (1.0 + jnp.exp(-x))


def _silu(x):
    return x * _sigmoid(x)


def _silu_and_grad(x):
    s = _sigmoid(x)
    return x * s, s * (1.0 + x * (1.0 - s))


def _gelu(y):
    return 0.5 * y * (1.0 + jnp.tanh(GELU_C * (y + GELU_K * y * y * y)))


def _gelu_and_grad(y):
    y2 = y * y
    th = jnp.tanh(GELU_C * (y + GELU_K * y2 * y))
    half = 0.5 * (1.0 + th)
    return y * half, half + 0.5 * y * (1.0 - th * th) * GELU_C * (1.0 + 3.0 * GELU_K * y2)


def _dot(a, b):
    return lax.dot_general(a.astype(MXU), b.astype(MXU), (((1,), (0,)), ((), ())), preferred_element_type=F32)


def _dot_nt(a, b):
    return lax.dot_general(a.astype(MXU), b.astype(MXU), (((1,), (1,)), ((), ())), preferred_element_type=F32)


def _dot_tn(a, b):
    return lax.dot_general(a.astype(MXU), b.astype(MXU), (((0,), (0,)), ((), ())), preferred_element_type=F32)


def _dot_exact(tri, x):
    t = tri.astype(MXU)
    hi = x.astype(MXU)
    r1 = x - hi.astype(F32)
    mid = r1.astype(MXU)
    lo = (r1 - mid.astype(F32)).astype(MXU)
    dn = (((1,), (0,)), ((), ()))
    return (lax.dot_general(t, hi, dn, preferred_element_type=F32) + lax.dot_general(t, mid, dn, preferred_element_type=F32)
            + lax.dot_general(t, lo, dn, preferred_element_type=F32))


def _row_sum(v):
    return jnp.sum(v, axis=0, keepdims=True)


def _handshake(partners):
    barrier = pltpu.get_barrier_semaphore()
    for p in partners:
        pl.semaphore_signal(barrier, inc=1, device_id=p, device_id_type=MESH)
    pl.semaphore_wait(barrier, len(partners))


def _gather_body(n, per_array_sems, handshake, layouts):
    def body(*refs):
        ins, outs = refs[:n], refs[n:2 * n]
        send_sems, recv_sems, local_sems = refs[2 * n:]
        x, y, c = lax.axis_index("x"), lax.axis_index("y"), lax.axis_index("c")
        me, sibling = (x, y, c), (x, y, 1 - c)
        chips = [(1 - x, y), (x, 1 - y), (1 - x, 1 - y)]
        if handshake:
            _handshake([sibling] + [(*chip, c) for chip in chips])

        def sem(sems, a, k):
            return sems.at[a, k] if per_array_sems else sems.at[k]

        def slot(a, p):
            block = 4 * p[0] + 2 * p[1] + p[2]
            r, c_ = ins[a].shape
            if layouts[a] == "rows":
                return outs[a].at[pl.ds(block * r, r), :]
            if layouts[a] == "cols":
                return outs[a].at[:, pl.ds(block * c_, c_)]
            return outs[a].at[block]

        def copy(a, k, block, to, src=None):
            return pltpu.make_async_remote_copy(
                src_ref=slot(a, block) if src is None else src, dst_ref=slot(a, block),
                send_sem=sem(send_sems, a, k), recv_sem=sem(recv_sems, a, k), device_id=to, device_id_type=MESH)

        mine = [pltpu.make_async_copy(ins[a], slot(a, me), local_sems.at[a if per_array_sems else 0]) for a in range(n)]
        for cp in mine:
            cp.start()
        first = []
        for a in range(n):
            first.append(copy(a, 0, me, sibling, src=ins[a]))
            first += [copy(a, 1 + j, me, (*chip, c), src=ins[a]) for j, chip in enumerate(chips)]
        for cp in first:
            cp.start()
        passed = []
        for j, chip in enumerate(chips):
            for a in range(n):
                copy(a, 1 + j, (*chip, c), me).wait_recv()
            for a in range(n):
                cp = copy(a, 4 + j, (*chip, c), sibling)
                cp.start()
                passed.append(cp)
        for a in range(n):
            copy(a, 0, sibling, me).wait_recv()
        for j, chip in enumerate(chips):
            for a in range(n):
                copy(a, 4 + j, (*chip, 1 - c), me).wait_recv()
        for cp in first + passed:
            cp.wait_send()
        for cp in mine:
            cp.wait()

    return body


def _sibling_send_body(n, per_array_sems, handshake):
    def body(*refs):
        ins, outs = refs[:n], refs[n:2 * n]
        send_sems, recv_sems, _ = refs[2 * n:]
        x, y, c = lax.axis_index("x"), lax.axis_index("y"), lax.axis_index("c")
        sibling = (x, y, 1 - c)
        if handshake:
            _handshake([sibling])

        def sem(sems, a):
            return sems.at[a, 0] if per_array_sems else sems.at[0]

        copies = [pltpu.make_async_remote_copy(
            src_ref=ins[a].at[2 * q + 1 - c], dst_ref=outs[a].at[q], send_sem=sem(send_sems, a), recv_sem=sem(recv_sems, a),
            device_id=sibling, device_id_type=MESH) for a in range(n) for q in range(4)]
        for cp in copies:
            cp.start()
        for cp in copies:
            cp.wait_recv()
        for cp in copies:
            cp.wait_send()

    return body


def _chip_exchange_body(n, per_array_sems, handshake):
    def body(*refs):
        ins, outs = refs[:n], refs[n:2 * n]
        send_sems, recv_sems, local_sems = refs[2 * n:]
        x, y, c = lax.axis_index("x"), lax.axis_index("y"), lax.axis_index("c")
        my_chip = 2 * x + y
        chips = [(1 - x, y), (x, 1 - y), (1 - x, 1 - y)]
        if handshake:
            _handshake([(*chip, c) for chip in chips])

        def sem(sems, a, k):
            return sems.at[a, k] if per_array_sems else sems.at[k]

        def copy(a, k, landing):
            px, py = chips[k]
            return pltpu.make_async_remote_copy(
                src_ref=ins[a].at[2 * px + py], dst_ref=outs[a].at[landing], send_sem=sem(send_sems, a, k),
                recv_sem=sem(recv_sems, a, k), device_id=(px, py, c), device_id_type=MESH)

        mine = [pltpu.make_async_copy(ins[a].at[my_chip], outs[a].at[my_chip], local_sems.at[a if per_array_sems else 0])
                for a in range(n)]
        for cp in mine:
            cp.start()
        sent = [copy(a, k, my_chip) for a in range(n) for k in range(3)]
        for cp in sent:
            cp.start()
        for a in range(n):
            for k, (px, py) in enumerate(chips):
                copy(a, k, 2 * px + py).wait_recv()
        for cp in sent:
            cp.wait_send()
        for cp in mine:
            cp.wait()

    return body


def _all_gather(arrs, name):
    n = len(arrs)
    return pl.pallas_call(
        _gather_body(n, True, False, ["stack"] * n), name=name, in_specs=[ANY] * n, out_specs=[ANY] * n,
        out_shape=[jax.ShapeDtypeStruct((N_DEV,) + a.shape, a.dtype) for a in arrs],
        scratch_shapes=[pltpu.SemaphoreType.DMA((n, 7)), pltpu.SemaphoreType.DMA((n, 7)), pltpu.SemaphoreType.DMA((n,))],
    )(*arrs)


def _on_sequencer(body, arrs, out_type, name, collective_id):
    return pl.kernel(
        body, name=name, out_type=out_type, mesh=plsc.ScalarSubcoreMesh(axis_name="sequencer", num_cores=1),
        scratch_types=[pltpu.SemaphoreType.DMA((RING_SEMS,)), pltpu.SemaphoreType.DMA((RING_SEMS,)),
                       pltpu.SemaphoreType.DMA((1,))],
        compiler_params=pltpu.CompilerParams(collective_id=collective_id))(*arrs)


RING_SEMS = 8


def _ring_gather_body(n, per_array_sems, handshake, layouts):
    def body(*refs):
        ins, outs = refs[:n], refs[n:2 * n]
        send_sems, recv_sems, local_sems = refs[2 * n:]
        x, y, c = lax.axis_index("x"), lax.axis_index("y"), lax.axis_index("c")
        me, sibling = (x, y, c), (x, y, 1 - c)
        over_x, over_y, diagonal = (1 - x, y, c), (x, 1 - y, c), (1 - x, 1 - y, c)
        if handshake:
            _handshake([sibling, over_x, over_y])

        def sem(sems, a, k):
            return sems.at[a, k] if per_array_sems else sems.at[k]

        def slot(a, p, half=None):
            block = 4 * p[0] + 2 * p[1] + p[2]
            r, c_ = ins[a].shape
            first, count = (0, r) if half is None else (half * (r // 2), r // 2)
            if layouts[a] == "rows":
                return outs[a].at[pl.ds(block * r + first, count), :]
            if layouts[a] == "cols":
                return outs[a].at[pl.ds(first, count), pl.ds(block * c_, c_)]
            return outs[a].at[block, pl.ds(first, count), :]

        def copy(a, k, block, to, half=None, src=None):
            return pltpu.make_async_remote_copy(
                src_ref=slot(a, block, half) if src is None else src, dst_ref=slot(a, block, half),
                send_sem=sem(send_sems, a, k), recv_sem=sem(recv_sems, a, k), device_id=to, device_id_type=MESH)

        def start(copies):
            for cp in copies:
                cp.start()
            return copies

        def arrived(k, block, half=None):
            for a in range(n):
                copy(a, k, block, me, half).wait_recv()

        arrays = range(n)
        mine = start([pltpu.make_async_copy(ins[a], slot(a, me), local_sems.at[a if per_array_sems else 0]) for a in arrays])
        sent = start([copy(a, k, me, to, src=ins[a]) for a in arrays for k, to in ((0, sibling), (1, over_x), (2, over_y))])
        arrived(1, over_x)
        sent += start([copy(a, 3, over_x, over_y, half=0) for a in arrays] + [copy(a, 5, over_x, sibling) for a in arrays])
        arrived(2, over_y)
        sent += start([copy(a, 4, over_y, over_x, half=1) for a in arrays] + [copy(a, 6, over_y, sibling) for a in arrays])
        arrived(3, diagonal, half=0)
        arrived(4, diagonal, half=1)
        sent += start([copy(a, 7, diagonal, sibling) for a in arrays])
        arrived(0, sibling)
        for k, chip in ((5, over_x), (6, over_y), (7, diagonal)):
            arrived(k, (chip[0], chip[1], 1 - c))
        for cp in sent:
            cp.wait_send()
        for cp in mine:
            cp.wait()

    return body


def _all_gather_async(arrs, name, collective_id, layouts):
    shape = dict(stack=lambda r, c: (N_DEV, r, c), rows=lambda r, c: (N_DEV * r, c), cols=lambda r, c: (r, N_DEV * c))
    out_type = [jax.ShapeDtypeStruct(shape[lay](*a.shape), a.dtype) for a, lay in zip(arrs, layouts)]
    return _on_sequencer(_ring_gather_body(len(arrs), False, True, layouts), arrs, out_type, name, collective_id)


def _sibling_send_async(arrs, name, collective_id):
    out_type = [jax.ShapeDtypeStruct((N_DEV // 2,) + a.shape[1:], a.dtype) for a in arrs]
    return _on_sequencer(_sibling_send_body(len(arrs), False, True), arrs, out_type, name, collective_id)


def _chip_exchange_async(arrs, name, collective_id):
    out_type = [jax.ShapeDtypeStruct(a.shape, a.dtype) for a in arrs]
    return _on_sequencer(_chip_exchange_body(len(arrs), False, True), arrs, out_type, name, collective_id)


def _pair_sum(name, parity, mine, theirs):
    n = len(mine)

    def body(par_ref, *refs):
        for a in range(n):
            refs[2 * n + a][...] = (refs[a][...].astype(F32) + refs[n + a][...].astype(F32)).astype(refs[2 * n + a].dtype)

    def block(a):
        return (None,) + a.shape[1:]

    grid_spec = pltpu.PrefetchScalarGridSpec(
        num_scalar_prefetch=1, grid=(N_DEV // 2,),
        in_specs=[pl.BlockSpec(block(a), lambda q, par: (2 * q + par[0], 0, 0)) for a in mine]
        + [pl.BlockSpec(block(a), lambda q, par: (q, 0, 0)) for a in theirs],
        out_specs=[pl.BlockSpec(block(a), lambda q, par: (q, 0, 0)) for a in theirs])
    return pl.pallas_call(body, name=name, grid_spec=grid_spec, out_shape=[jax.ShapeDtypeStruct(a.shape, a.dtype) for a in theirs],
                          compiler_params=_params())(parity, *mine, *theirs)


NN = (((1,), (0,)), ((), ()))
NT = (((1,), (1,)), ((), ()))
TN = (((0,), (0,)), ((), ()))


def _matmul(name, a, b, dims, grid, a_spec, b_spec, outs, epilogue, extras=(), prologue=None):
    n_in = 2 + len(extras)
    n_out = len(outs)

    def body(*refs):
        a_ref, b_ref = refs[0], refs[1]
        ex, out_refs = refs[2:n_in], refs[n_in:n_in + n_out]
        a_tile = a_ref[...] if prologue is None else prologue(a_ref[...])
        epilogue(lax.dot_general(a_tile, b_ref[...], dims, preferred_element_type=F32), ex, out_refs)

    return pl.pallas_call(
        body, name=name, grid=grid, in_specs=[a_spec, b_spec] + [s for _, s in extras], out_specs=[s for _, _, s in outs],
        out_shape=[jax.ShapeDtypeStruct(sh, dt) for sh, dt, _ in outs], compiler_params=_params(),
    )(a, b, *[e for e, _ in extras])


def _square(tile):
    return tile * tile


def _ep_store(acc, ex, outs):
    outs[0][...] = acc.astype(outs[0].dtype)


def _norm_mod_rows(xv, v_ref):
    r = lax.rsqrt(jnp.mean(xv * xv, axis=-1, keepdims=True) + NORM_EPS)
    h = (xv * r) * v_ref[0:1, :]
    return h * (1.0 + v_ref[1:2, :]) + v_ref[2:3, :]


def _ep_residual_norm(acc, ex, outs):
    outs[0][...] = acc.astype(outs[0].dtype)
    xv = ex[0][...] + ex[1][0:1, :] * acc
    outs[1][...] = xv
    outs[2][...] = _norm_mod_rows(xv, ex[2]).astype(outs[2].dtype)


def _mm_tokens(name, a, w, transposed, out_dtypes, epilogue=_ep_store, extras=(), prologue=None, rows=MM_TOKENS):
    m, n = a.shape[0], w.shape[0 if transposed else 1]
    tm = _tile(m, rows)
    rows_spec = lambda width: pl.BlockSpec((tm, width), lambda i, j, kk: (i, 0))
    whole = lambda arr: pl.BlockSpec(arr.shape, lambda i, j, kk: (0, 0))
    return _matmul(name, a, w, NT if transposed else NN, (m // tm, 1, 1), rows_spec(a.shape[1]), whole(w),
                   [((m, n), dt, rows_spec(n)) for dt in out_dtypes], epilogue,
                   extras=[(e.table, _group_spec(e, 3)) if isinstance(e, Group) else (e, rows_spec(n)) for e in extras],
                   prologue=prologue)


def _resident(w):
    return pl.BlockSpec(w.shape, lambda i: (0, 0), pipeline_mode=pl.Buffered(1))


def _mlp_forward(name, h, w1, w2, x, gate, next_norm=None):
    m, d = x.shape
    f = w1.shape[1]
    tm = _tile(m, MLP_TOKENS)
    rows = lambda width: pl.BlockSpec((tm, width), lambda i: (i, 0))
    chained = next_norm is not None

    def body(h_ref, w1_ref, w2_ref, x_ref, g_ref, *rest):
        r_ref, ff_ref, out_ref = rest[chained:chained + 3]
        r = jnp.maximum(lax.dot_general(h_ref[...], w1_ref[...], NN, preferred_element_type=F32), 0.0).astype(r_ref.dtype)
        r_ref[...] = r
        ff = lax.dot_general(r * r, w2_ref[...], NN, preferred_element_type=F32)
        ff_ref[...] = ff.astype(ff_ref.dtype)
        xv = x_ref[...] + g_ref[0:1, :] * ff
        out_ref[...] = xv
        if chained:
            rest[4][...] = _norm_mod_rows(xv, rest[0]).astype(rest[4].dtype)

    return pl.pallas_call(
        body, name=name, grid=(m // tm,),
        in_specs=[rows(d), _resident(w1), _resident(w2), rows(d), _group_spec(gate)] + ([_group_spec(next_norm)] if chained else []),
        out_specs=[rows(f), rows(d), rows(d)] + ([rows(d)] if chained else []),
        out_shape=[jax.ShapeDtypeStruct((m, f), MXU), jax.ShapeDtypeStruct((m, d), MXU), jax.ShapeDtypeStruct((m, d), F32)]
        + ([jax.ShapeDtypeStruct((m, d), MXU)] if chained else []),
        compiler_params=_params())(h, w1, w2, x, gate.table, *([next_norm.table] if chained else []))


def _mlp_backward(name, dff, r, w1, w2, x, vec, dres, y, gate):
    m, d = dff.shape
    f = r.shape[1]
    tm = _tile(m, MLP_TOKENS)
    nt = m // tm
    rows = lambda width: pl.BlockSpec((tm, width), lambda i: (i, 0))

    def body(dff_ref, r_ref, w1_ref, w2_ref, x_ref, v_ref, dres_ref, y_ref, g_ref, da_ref, dx_ref, acc_ref, dy_ref):
        ds = lax.dot_general(dff_ref[...], w2_ref[...], NT, preferred_element_type=F32)
        da = (ds * (2.0 * r_ref[...].astype(F32))).astype(da_ref.dtype)
        da_ref[...] = da
        dh = lax.dot_general(da, w1_ref[...], NT, preferred_element_type=F32)
        _norm_grad_step(nt, dh, x_ref, v_ref, dres_ref, dx_ref, acc_ref, (y_ref, g_ref, dy_ref))

    return pl.pallas_call(
        body, name=name, grid=(nt,),
        in_specs=[rows(d), rows(f), _resident(w1), _resident(w2), rows(d), _group_spec(vec), rows(d), rows(d), _group_spec(gate)],
        out_specs=[rows(f), rows(d), _vec_spec(), rows(d)],
        out_shape=[jax.ShapeDtypeStruct((m, f), MXU), jax.ShapeDtypeStruct((m, d), F32),
                   jax.ShapeDtypeStruct((SUBLANES, d), F32), jax.ShapeDtypeStruct((m, d), MXU)],
        compiler_params=_params())(dff, r, w1, w2, x, vec.table, dres, y, gate.table)


def _mm_norm_grad(name, a, w, x, vec, dres, y=None, gate=None):
    m = a.shape[0]
    tm = _tile(m, MM_TOKENS)
    nt = m // tm
    branch = y is not None
    rows = lambda width: pl.BlockSpec((tm, width), lambda i: (i, 0))

    def body(a_ref, w_ref, x_ref, v_ref, dres_ref, *rest):
        dh = lax.dot_general(a_ref[...], w_ref[...], NT, preferred_element_type=F32)
        dx_ref, acc_ref = rest[2 * branch], rest[2 * branch + 1]
        _norm_grad_step(nt, dh, x_ref, v_ref, dres_ref, dx_ref, acc_ref, (rest[0], rest[1], rest[4]) if branch else None)

    return pl.pallas_call(
        body, name=name, grid=(nt,),
        in_specs=[rows(a.shape[1]), _resident(w), rows(D), _group_spec(vec), rows(D)] + ([rows(D), _group_spec(gate)] if branch else []),
        out_specs=[rows(D), _vec_spec()] + ([rows(D)] if branch else []),
        out_shape=[jax.ShapeDtypeStruct((m, D), F32), jax.ShapeDtypeStruct((SUBLANES, D), F32)]
        + ([jax.ShapeDtypeStruct((m, D), MXU)] if branch else []),
        compiler_params=_params())(a, w, x, vec.table, dres, *([y, gate.table] if branch else []))


def _mm_grad(name, a, b, shard_cols, prologue=None):
    t, m = a.shape
    n = b.shape[1]
    tm = _tile(m, MM_TILE)
    if shard_cols:
        tn = n // N_DEV
        out = ((N_DEV, m, tn), MXU, pl.BlockSpec((None, tm, tn), lambda i, j, kk: (j, i, 0)))
    else:
        tn = _tile(n, MM_TILE)
        out = ((m, n), MXU, pl.BlockSpec((tm, tn), lambda i, j, kk: (i, j)))
    return _matmul(name, a, b, TN, (m // tm, n // tn, 1), pl.BlockSpec((t, tm), lambda i, j, kk: (0, i)),
                   pl.BlockSpec((t, tn), lambda i, j, kk: (0, j)), [out], _ep_store, prologue=prologue)[0]


def _rows_spec(tt, width=D):
    return pl.BlockSpec((tt, width), lambda i: (i, 0))


def _vec_spec(rows=SUBLANES, width=D):
    return pl.BlockSpec((rows, width), lambda i: (0, 0))


Group = collections.namedtuple("Group", ["table", "index"])


def _group_spec(group, grid_rank=1):
    if grid_rank == 1:
        return pl.BlockSpec((SUBLANES, D), lambda i: (group.index, 0))
    return pl.BlockSpec((SUBLANES, D), lambda i, j, kk: (group.index, 0))


def _norm_mod(name, x, vec):
    t = x.shape[0]
    tt = _tile(t, ROW_TILE)

    def body(x_ref, v_ref, h_ref):
        xv = x_ref[...]
        r = lax.rsqrt(jnp.mean(xv * xv, axis=-1, keepdims=True) + NORM_EPS)
        h = (xv * r) * v_ref[0:1, :]
        h_ref[...] = (h * (1.0 + v_ref[1:2, :]) + v_ref[2:3, :]).astype(h_ref.dtype)

    return pl.pallas_call(body, name=name, grid=(t // tt,), in_specs=[_rows_spec(tt), _group_spec(vec)], out_specs=_rows_spec(tt),
                          out_shape=jax.ShapeDtypeStruct((t, D), MXU))(x, vec.table)


GATE_ROW = 3


def _branch_grad(dx, y_ref, g_ref, dy_ref, acc_ref):
    dy_ref[...] = (dx * g_ref[0:1, :]).astype(dy_ref.dtype)
    acc_ref[GATE_ROW:GATE_ROW + 1, :] += _row_sum(dx * y_ref[...])


def _norm_grad_step(nt, dh, x_ref, v_ref, dres_ref, dx_ref, acc_ref, branch):
    i = pl.program_id(0)

    @pl.when(i == 0)
    def _():
        acc_ref[...] = jnp.zeros_like(acc_ref)

    xv = x_ref[...]
    r = lax.rsqrt(jnp.mean(xv * xv, axis=-1, keepdims=True) + NORM_EPS)
    xn = xv * r
    w = v_ref[0:1, :] * (1.0 + v_ref[1:2, :])
    acc_ref[0:1, :] += _row_sum(dh * xn)
    acc_ref[2:3, :] += _row_sum(dh)
    dxn = dh * w
    dx = dres_ref[...] + r * (dxn - xn * jnp.mean(dxn * xn, axis=-1, keepdims=True))
    dx_ref[...] = dx
    if branch is not None:
        _branch_grad(dx, branch[0], branch[1], branch[2], acc_ref)

    @pl.when(i == nt - 1)
    def _():
        dw = acc_ref[0:1, :]
        acc_ref[1:2, :] = dw * v_ref[0:1, :]
        acc_ref[0:1, :] = dw * (1.0 + v_ref[1:2, :])


def _loss_head(name, x, target, vec, y, gate):
    t = x.shape[0]
    tt = _tile(t, ROW_TILE)

    def body(x_ref, t_ref, v_ref, y_ref, g_ref, dx_ref, acc_ref, dy_ref):
        @pl.when(pl.program_id(0) == 0)
        def _():
            acc_ref[...] = jnp.zeros_like(acc_ref)

        xv = x_ref[...]
        r = lax.rsqrt(jnp.mean(xv * xv, axis=-1, keepdims=True) + NORM_EPS)
        xn = xv * r
        gain = v_ref[0:1, :]
        err = xn * gain - t_ref[...]
        acc_ref[1:2, :] += _row_sum(err * err) * (0.5 / D)
        dout = err * (1.0 / D)
        acc_ref[0:1, :] += _row_sum(dout * xn)
        dxn = dout * gain
        dx = r * (dxn - xn * jnp.mean(dxn * xn, axis=-1, keepdims=True))
        dx_ref[...] = dx
        _branch_grad(dx, y_ref, g_ref, dy_ref, acc_ref)

    return pl.pallas_call(
        body, name=name, grid=(t // tt,),
        in_specs=[_rows_spec(tt), _rows_spec(tt), _group_spec(vec), _rows_spec(tt), _group_spec(gate)],
        out_specs=[_rows_spec(tt), _vec_spec(), _rows_spec(tt)],
        out_shape=[jax.ShapeDtypeStruct((t, D), F32), jax.ShapeDtypeStruct((SUBLANES, D), F32),
                   jax.ShapeDtypeStruct((t, D), MXU)])(x, target, vec.table, y, gate.table)


def _shift_down(x, halo, k):
    y = pltpu.roll(x, k, 0)
    top = jnp.where(lax.broadcasted_iota(jnp.int32, halo.shape, 0) < k, pltpu.roll(halo, k, 0), y[0:SUBLANES, :])
    return jnp.concatenate([top, y[SUBLANES:, :]], axis=0)


def _shift_up(x, halo, k):
    n = x.shape[0]
    y = pltpu.roll(x, n - k, 0)
    bottom = jnp.where(lax.broadcasted_iota(jnp.int32, halo.shape, 0) >= SUBLANES - k, pltpu.roll(halo, SUBLANES - k, 0),
                       y[n - SUBLANES:, :])
    return jnp.concatenate([y[:n - SUBLANES, :], bottom], axis=0)


def _rg_gates(xb, halo, cw_ref, vec_ref, wa_ref, wx_ref, at_start):
    shifted = [xb] + [_shift_down(xb, halo, k) for k in range(1, CONV_WIDTH)]
    xc = vec_ref[0:1, :] + shifted[0] * cw_ref[CONV_WIDTH - 1:CONV_WIDTH, :]
    for k in range(1, CONV_WIDTH):
        xc = xc + shifted[k] * cw_ref[CONV_WIDTH - 1 - k:CONV_WIDTH - k, :]
    heads = [slice(h * HEAD_DIM, (h + 1) * HEAD_DIM) for h in range(HEADS)]
    pa = jnp.concatenate([_dot(xc[:, s], wa_ref[h]) for h, s in enumerate(heads)], axis=1) + vec_ref[1:2, :]
    px = jnp.concatenate([_dot(xc[:, s], wx_ref[h]) for h, s in enumerate(heads)], axis=1) + vec_ref[2:3, :]
    ra, ia = _sigmoid(pa), _sigmoid(px)
    nl = -vec_ref[3:4, :]
    sp = jnp.maximum(nl, 0.0) + jnp.log(1.0 + jnp.exp(-jnp.abs(nl)))
    log_a = (-LRU_C) * ra * sp
    a = jnp.exp(log_a)
    th = jnp.tanh(log_a)
    is_t0 = jnp.logical_and(lax.broadcasted_iota(jnp.int32, xb.shape, 0) == 0, at_start)
    mult = jnp.where(is_t0, 1.0, jnp.sqrt(-2.0 * th / (1.0 - th)))
    return dict(shifted=shifted, xc=xc, ra=ra, ia=ia, sp=sp, a=a, mult=mult, is_t0=is_t0, heads=heads)


def _rg_specs(tt, nt, order, cw, vec):
    blk = tt // SUBLANES
    return dict(
        x=pl.BlockSpec((tt, D), lambda i: (order(i), 0)), y=pl.BlockSpec((tt, D), lambda i: (order(i), 1)),
        halo=pl.BlockSpec((SUBLANES, D), lambda i: (jnp.maximum(order(i) * blk - 1, 0), 0)),
        cw=_group_spec(cw), vec=_group_spec(vec), w=pl.BlockSpec((HEADS, HEAD_DIM, HEAD_DIM), lambda i: (0, 0, 0)))


def _rg_forward(name, z, cw, vec, wa, wx):
    t = z.shape[0]
    tt = _tile(t, RG_ROWS)
    nt = t // tt
    sp = _rg_specs(tt, nt, lambda i: i, cw, vec)

    def body(zx_ref, zy_ref, halo_ref, cw_ref, vec_ref, wa_ref, wx_ref, p_ref, h_ref, a_s, u_s, carry):
        i = pl.program_id(0)

        @pl.when(i == 0)
        def _():
            carry[...] = jnp.zeros_like(carry)

        halo = jnp.where(i > 0, halo_ref[...], 0.0)
        g = _rg_gates(zx_ref[...], halo, cw_ref, vec_ref, wa_ref, wx_ref, i == 0)
        a_s[...] = g["a"]
        u_s[...] = g["mult"] * (g["ia"] * g["xc"])

        def group(gi, h):
            rows = pl.ds(pl.multiple_of(gi * SUBLANES, SUBLANES), SUBLANES)
            a8, u8 = a_s[rows, :], u_s[rows, :]
            out = []
            for j in range(SUBLANES):
                h = a8[j:j + 1, :] * h + u8[j:j + 1, :]
                out.append(h)
            h_ref[rows, :] = jnp.concatenate(out, axis=0)
            return h

        carry[0:1, :] = lax.fori_loop(0, tt // SUBLANES, group, carry[0:1, :])
        p_ref[...] = (h_ref[...] * _gelu(zy_ref[...])).astype(p_ref.dtype)

    return pl.pallas_call(
        body, name=name, grid=(nt,), in_specs=[sp["x"], sp["y"], sp["halo"], sp["cw"], sp["vec"], sp["w"], sp["w"]],
        out_specs=[_rows_spec(tt), _rows_spec(tt)],
        out_shape=[jax.ShapeDtypeStruct((t, D), MXU), jax.ShapeDtypeStruct((t, D), F32)],
        scratch_shapes=[pltpu.VMEM((tt, D), F32), pltpu.VMEM((tt, D), F32), pltpu.VMEM((SUBLANES, D), F32)],
        compiler_params=_params())(z, z, z, cw.table, vec.table, wa, wx)


def _rg_backward(name, dp, z, h, cw, vec, wa, wx):
    t = z.shape[0]
    tt = _tile(t, RG_ROWS)
    nt = t // tt
    rev = lambda i: nt - 1 - i
    sp = _rg_specs(tt, nt, rev, cw, vec)
    rows_rev = pl.BlockSpec((tt, D), lambda i: (rev(i), 0))

    def body(dp_ref, zx_ref, zy_ref, halo_ref, h_ref, hhalo_ref, cw_ref, vec_ref, wa_ref, wx_ref,
             dz_ref, dvec_ref, dcw_ref, dwa_ref, dwx_ref, a_s, d_s, carry, nxt):
        i = pl.program_id(0)
        j = rev(i)

        @pl.when(i == 0)
        def _():
            carry[...] = jnp.zeros_like(carry)
            nxt[...] = jnp.zeros_like(nxt)
            dvec_ref[...] = jnp.zeros_like(dvec_ref)
            dcw_ref[...] = jnp.zeros_like(dcw_ref)
            dwa_ref[...] = jnp.zeros_like(dwa_ref)
            dwx_ref[...] = jnp.zeros_like(dwx_ref)

        halo = jnp.where(j > 0, halo_ref[...], 0.0)
        g = _rg_gates(zx_ref[...], halo, cw_ref, vec_ref, wa_ref, wx_ref, j == 0)
        xc, ra, ia, a, mult = g["xc"], g["ra"], g["ia"], g["a"], g["mult"]
        hv, zy, dpv = h_ref[...], zy_ref[...], dp_ref[...]
        gelu, gelu_grad = _gelu_and_grad(zy)
        dyb = dpv * hv * gelu_grad
        a_s[...] = a
        d_s[...] = dpv * gelu

        def group(gi, c):
            rows = pl.ds(pl.multiple_of((tt // SUBLANES - 1 - gi) * SUBLANES, SUBLANES), SUBLANES)
            a8, d8 = a_s[rows, :], d_s[rows, :]
            out = [None] * SUBLANES
            for r in reversed(range(SUBLANES)):
                dht = d8[r:r + 1, :] + c
                out[r] = dht
                c = a8[r:r + 1, :] * dht
            d_s[rows, :] = jnp.concatenate(out, axis=0)
            return c

        carry[0:1, :] = lax.fori_loop(0, tt // SUBLANES, group, carry[0:1, :])
        dht = d_s[...]
        hprev = _shift_down(hv, jnp.where(j > 0, hhalo_ref[...], 0.0), 1)
        ixc = ia * xc
        dlog_a = dht * hprev * a + jnp.where(g["is_t0"], 0.0, dht * ixc * (-(a * a) / mult))
        dia = dht * mult * xc
        dxc = dht * mult * ia
        dra = dlog_a * ((-LRU_C) * g["sp"])
        nl = -vec_ref[3:4, :]
        dvec_ref[3:4, :] += _row_sum(dlog_a * ((-LRU_C) * ra)) * (-_sigmoid(nl))
        dpa = dra * ra * (1.0 - ra)
        dpx = dia * ia * (1.0 - ia)
        dvec_ref[1:2, :] += _row_sum(dpa)
        dvec_ref[2:3, :] += _row_sum(dpx)
        xc_m, dpa_m, dpx_m = xc.astype(MXU), dpa.astype(MXU), dpx.astype(MXU)
        back = [_dot_nt(dpa_m[:, s], wa_ref[hd]) + _dot_nt(dpx_m[:, s], wx_ref[hd]) for hd, s in enumerate(g["heads"])]
        dwa = [_dot_tn(xc_m[:, s], dpa_m[:, s]) for s in g["heads"]]
        dwx = [_dot_tn(xc_m[:, s], dpx_m[:, s]) for s in g["heads"]]
        for hd in range(HEADS):
            dwa_ref[hd] += dwa[hd]
            dwx_ref[hd] += dwx[hd]
        dxc = dxc + jnp.concatenate(back, axis=1)
        dvec_ref[0:1, :] += _row_sum(dxc)
        dxb = dxc * cw_ref[CONV_WIDTH - 1:CONV_WIDTH, :]
        for k in range(CONV_WIDTH):
            row = CONV_WIDTH - 1 - k
            dcw_ref[row:row + 1, :] += _row_sum(dxc * g["shifted"][k])
            if k:
                dxb = dxb + _shift_up(dxc, nxt[...], k) * cw_ref[row:row + 1, :]
        nxt[...] = dxc[0:SUBLANES, :]
        dz_ref[:, 0:D] = dxb.astype(dz_ref.dtype)
        dz_ref[:, D:2 * D] = dyb.astype(dz_ref.dtype)

    hhalo = pl.BlockSpec((SUBLANES, D), lambda i: (jnp.maximum(rev(i) * (tt // SUBLANES) - 1, 0), 0))
    wacc = pl.BlockSpec((HEADS, HEAD_DIM, HEAD_DIM), lambda i: (0, 0, 0))
    return pl.pallas_call(
        body, name=name, grid=(nt,),
        in_specs=[rows_rev, sp["x"], sp["y"], sp["halo"], rows_rev, hhalo, sp["cw"], sp["vec"], sp["w"], sp["w"]],
        out_specs=[pl.BlockSpec((tt, 2 * D), lambda i: (rev(i), 0)), _vec_spec(), _vec_spec(), wacc, wacc],
        out_shape=[jax.ShapeDtypeStruct((t, 2 * D), MXU), jax.ShapeDtypeStruct((SUBLANES, D), F32),
                   jax.ShapeDtypeStruct((SUBLANES, D), F32), jax.ShapeDtypeStruct((HEADS, HEAD_DIM, HEAD_DIM), F32),
                   jax.ShapeDtypeStruct((HEADS, HEAD_DIM, HEAD_DIM), F32)],
        scratch_shapes=[pltpu.VMEM((tt, D), F32), pltpu.VMEM((tt, D), F32), pltpu.VMEM((SUBLANES, D), F32),
                        pltpu.VMEM((SUBLANES, D), F32)],
        compiler_params=_params())(dp, z, z, z, h, h, cw.table, vec.table, wa, wx)


def _hg_chunk(zq, zf, zi, lb):
    c = HG_CHUNK
    q, q_grad = _silu_and_grad(zq)
    sg = _sigmoid(zf)
    fg = lb + (1.0 - lb) * sg
    k = 1.0 - fg
    row, col = lax.broadcasted_iota(jnp.int32, (c, c), 0), lax.broadcasted_iota(jnp.int32, (c, c), 1)
    tri, tri_t = (row >= col).astype(F32), (row <= col).astype(F32)
    b = _dot_exact(tri, jnp.log(fg))
    mid, last = b[c // 2 - 1:c // 2, :], b[c - 1:c, :]
    eq = jnp.exp(jnp.minimum(b - mid, EXP_CLAMP))
    ek = jnp.exp(jnp.minimum(mid - b, EXP_CLAMP))
    eb = jnp.exp(b)
    ed = jnp.exp(last - b)
    return dict(q=q, q_grad=q_grad, sg=sg, fg=fg, k=k, v=zi, eq=eq, ek=ek, eb=eb, ed=ed, elast=jnp.exp(last), tri=tri, tri_t=tri_t,
                qe=q * eq, ke=k * ek, qb=q * eb, kd=k * ed)


def _hg_specs(tt, order):
    return [pl.BlockSpec((tt, D), lambda i, col=col: (order(i), col)) for col in range(4)]


def _hg_forward(name, z, vec):
    t = z.shape[0]
    tt = _tile(t, ROW_TILE)
    nt, nc = t // tt, tt // HG_CHUNK
    heads = [slice(h * HEAD_DIM, (h + 1) * HEAD_DIM) for h in range(HEADS)]

    def body(zq_ref, zf_ref, zi_ref, zg_ref, vec_ref, p_ref, o_ref, ss_ref, state):
        @pl.when(pl.program_id(0) == 0)
        def _():
            state[...] = jnp.zeros_like(state)

        lb, gain = vec_ref[0:1, :], vec_ref[1:2, :]

        def chunk(ci, carry):
            rows = pl.ds(pl.multiple_of(ci * HG_CHUNK, HG_CHUNK), HG_CHUNK)
            g = _hg_chunk(zq_ref[rows, :], zf_ref[rows, :], zi_ref[rows, :], lb)
            zg = zg_ref[rows, :]
            causal = g["tri"] > 0.0
            qe_m, ke_m, qb_m, kd_m, v_m = (g[n].astype(MXU) for n in ("qe", "ke", "qb", "kd", "v"))
            atts = [_dot_nt(qe_m[:, s], ke_m[:, s]) for s in heads]
            o_parts, p_parts = [], []
            for hd, s in enumerate(heads):
                st = state[hd]
                ss_ref[ci, hd] = st
                att = jnp.where(causal, atts[hd], 0.0)
                o_parts.append(_dot(att, v_m[:, s]) + _dot_nt(qb_m[:, s], st))
                state[hd] = st * g["elast"][:, s] + _dot_tn(v_m[:, s], kd_m[:, s])
            for o, s in zip(o_parts, heads):
                r = lax.rsqrt(jnp.mean(o * o, axis=-1, keepdims=True) + GNORM_EPS)
                p_parts.append((o * r) * gain[:, s])
            o_ref[rows, :] = jnp.concatenate(o_parts, axis=1)
            p_ref[rows, :] = (jnp.concatenate(p_parts, axis=1) * _silu(zg)).astype(p_ref.dtype)
            return carry

        lax.fori_loop(0, nc, chunk, 0, unroll=True)

    return pl.pallas_call(
        body, name=name, grid=(nt,), in_specs=_hg_specs(tt, lambda i: i) + [_group_spec(vec)],
        out_specs=[_rows_spec(tt), _rows_spec(tt), pl.BlockSpec((nc, HEADS, HEAD_DIM, HEAD_DIM), lambda i: (i, 0, 0, 0))],
        out_shape=[jax.ShapeDtypeStruct((t, D), MXU), jax.ShapeDtypeStruct((t, D), F32),
                   jax.ShapeDtypeStruct((t // HG_CHUNK, HEADS, HEAD_DIM, HEAD_DIM), F32)],
        scratch_shapes=[pltpu.VMEM((HEADS, HEAD_DIM, HEAD_DIM), F32)], compiler_params=_params())(z, z, z, z, vec.table)


def _hg_backward(name, dp, z, o, ss, vec):
    t = z.shape[0]
    tt = _tile(t, ROW_TILE)
    nt, nc = t // tt, tt // HG_CHUNK
    rev = lambda i: nt - 1 - i
    heads = [slice(h * HEAD_DIM, (h + 1) * HEAD_DIM) for h in range(HEADS)]
    rows_rev = pl.BlockSpec((tt, D), lambda i: (rev(i), 0))

    def body(dp_ref, zq_ref, zf_ref, zi_ref, zg_ref, o_ref, ss_ref, vec_ref, dz_ref, acc_ref, dstate):
        @pl.when(pl.program_id(0) == 0)
        def _():
            dstate[...] = jnp.zeros_like(dstate)
            acc_ref[...] = jnp.zeros_like(acc_ref)

        lb, gain = vec_ref[0:1, :], vec_ref[1:2, :]

        def chunk(cr, carry):
            ci = nc - 1 - cr
            rows = pl.ds(pl.multiple_of(ci * HG_CHUNK, HG_CHUNK), HG_CHUNK)
            zq, zg = zq_ref[rows, :], zg_ref[rows, :]
            g = _hg_chunk(zq, zf_ref[rows, :], zi_ref[rows, :], lb)
            ov, dpv = o_ref[rows, :], dp_ref[rows, :]
            causal = g["tri"] > 0.0
            silu_g, silu_g_grad = _silu_and_grad(zg)
            don = dpv * silu_g
            dgate = dpv * silu_g_grad
            qe_m, ke_m, qb_m, kd_m, v_m = (g[n].astype(MXU) for n in ("qe", "ke", "qb", "kd", "v"))
            parts = {n: [] for n in ("dzg", "dgain", "dv", "dqe", "dke", "dqb", "dkd", "dlast")}
            dos, atts, datts = [], [], []
            for hd, s in enumerate(heads):
                oh = ov[:, s]
                r = lax.rsqrt(jnp.mean(oh * oh, axis=-1, keepdims=True) + GNORM_EPS)
                on = oh * r
                parts["dzg"].append(dgate[:, s] * (on * gain[:, s]))
                parts["dgain"].append(_row_sum(don[:, s] * on))
                dtmp = don[:, s] * gain[:, s]
                dos.append((r * (dtmp - on * jnp.mean(dtmp * on, axis=-1, keepdims=True))).astype(MXU))
            for hd, s in enumerate(heads):
                atts.append(_dot_nt(qe_m[:, s], ke_m[:, s]))
                datts.append(_dot_nt(dos[hd], v_m[:, s]))
            for hd, s in enumerate(heads):
                do, att, datt = dos[hd], jnp.where(causal, atts[hd], 0.0).astype(MXU), jnp.where(causal, datts[hd], 0.0).astype(MXU)
                st, dst = ss_ref[ci, hd], dstate[hd]
                st_m, dst_m = st.astype(MXU), dst.astype(MXU)
                qe, ke, qb, kd, v = qe_m[:, s], ke_m[:, s], qb_m[:, s], kd_m[:, s], v_m[:, s]
                parts["dv"].append(_dot_tn(att, do) + _dot_nt(kd, dst_m))
                parts["dqe"].append(_dot(datt, ke))
                parts["dke"].append(_dot_tn(datt, qe))
                parts["dqb"].append(_dot(do, st_m))
                parts["dkd"].append(_dot(v, dst_m))
                parts["dlast"].append(g["elast"][:, s] * _row_sum(dst * st))
                dstate[hd] = dst * g["elast"][:, s] + _dot_tn(do, qb)
            whole = {n: jnp.concatenate(p, axis=1) for n, p in parts.items()}
            acc_ref[1:2, :] += whole["dgain"]
            dqe, dke, dqb, dkd = whole["dqe"], whole["dke"], whole["dqb"], whole["dkd"]
            dq = dqe * g["eq"] + dqb * g["eb"]
            dk = dke * g["ek"] + dkd * g["ed"]
            dkdk = dkd * g["kd"]
            db = dqe * qe_m.astype(F32) - dke * ke_m.astype(F32) + dqb * g["qb"] - dkdk
            dlogf = _dot_exact(g["tri_t"], db) + (whole["dlast"] + _row_sum(dkdk))
            dfg = dlogf / g["fg"] - dk
            sg = g["sg"]
            acc_ref[0:1, :] += _row_sum(dfg * (1.0 - sg))
            dz_ref[rows, 0:D] = (dq * g["q_grad"]).astype(dz_ref.dtype)
            dz_ref[rows, D:2 * D] = (dfg * (1.0 - lb) * sg * (1.0 - sg)).astype(dz_ref.dtype)
            dz_ref[rows, 2 * D:3 * D] = whole["dv"].astype(dz_ref.dtype)
            dz_ref[rows, 3 * D:4 * D] = whole["dzg"].astype(dz_ref.dtype)
            return carry

        lax.fori_loop(0, nc, chunk, 0, unroll=True)

    return pl.pallas_call(
        body, name=name, grid=(nt,),
        in_specs=[rows_rev] + _hg_specs(tt, rev) + [rows_rev, pl.BlockSpec((nc, HEADS, HEAD_DIM, HEAD_DIM), lambda i: (rev(i), 0, 0, 0)),
                                                  _group_spec(vec)],
        out_specs=[pl.BlockSpec((tt, 4 * D), lambda i: (rev(i), 0)), _vec_spec()],
        out_shape=[jax.ShapeDtypeStruct((t, 4 * D), MXU), jax.ShapeDtypeStruct((SUBLANES, D), F32)],
        scratch_shapes=[pltpu.VMEM((HEADS, HEAD_DIM, HEAD_DIM), F32)], compiler_params=_params())(dp, z, z, z, z, o, ss, vec.table)


def _mod_forward(name, c_all, mod_w, lower):
    depth, _, sw = mod_w.shape

    def body(c_ref, w_ref, lo_ref, cs_ref, mod_ref, lb_ref):
        cs = _silu(c_ref[...])
        mod_ref[...] = _dot(cs, w_ref[...])

        @pl.when(pl.program_id(0) == 0)
        def _():
            cs_ref[...] = cs
            lo = lo_ref[...]
            e = jnp.exp(lo - jnp.max(lo, axis=0, keepdims=True))
            sm = e / jnp.sum(e, axis=0, keepdims=True)
            lb_ref[0:1, :] = jnp.zeros((1, D), F32)
            for l in range(1, depth):
                lb_ref[l:l + 1, :] = lb_ref[l - 1:l, :] + sm[l:l + 1, :]

    return pl.pallas_call(
        body, name=name, grid=(depth,),
        in_specs=[pl.BlockSpec((N_DEV, D), lambda l: (0, 0)), pl.BlockSpec((None, D, sw), lambda l: (l, 0, 0)),
                  pl.BlockSpec((depth, D), lambda l: (0, 0))],
        out_specs=[pl.BlockSpec((N_DEV, D), lambda l: (0, 0)), pl.BlockSpec((None, N_DEV, sw), lambda l: (l, 0, 0)),
                   pl.BlockSpec((depth, D), lambda l: (0, 0))],
        out_shape=[jax.ShapeDtypeStruct((N_DEV, D), F32), jax.ShapeDtypeStruct((depth, N_DEV, sw), F32),
                   jax.ShapeDtypeStruct((depth, D), F32)], compiler_params=_params())(c_all, mod_w, lower)


def _mod_weight_grad(name, cs_t, dmod):
    depth, pad, sw = dmod.shape

    def body(c_ref, d_ref, g_ref):
        g_ref[...] = _dot(c_ref[...], d_ref[...])

    return pl.pallas_call(
        body, name=name, grid=(depth,),
        in_specs=[pl.BlockSpec((D, pad), lambda l: (0, 0)), pl.BlockSpec((None, pad, sw), lambda l: (l, 0, 0))],
        out_specs=pl.BlockSpec((None, D, sw), lambda l: (l, 0, 0)),
        out_shape=jax.ShapeDtypeStruct((depth, D, sw), F32), compiler_params=_params())(cs_t, dmod)


def _lower_bound_grad(name, lower, dlb):
    depth = lower.shape[0]

    def body(lo_ref, d_ref, out_ref):
        lo = lo_ref[...]
        e = jnp.exp(lo - jnp.max(lo, axis=0, keepdims=True))
        sm = e / jnp.sum(e, axis=0, keepdims=True)
        out_ref[...] = jnp.zeros_like(out_ref)
        dsm = [jnp.zeros((1, D), F32)]
        for l in range(1, depth):
            tail = d_ref[l:l + 1, :]
            for m in range(l + 1, depth):
                tail = tail + d_ref[m:m + 1, :]
            dsm.append(tail)
        inner = sm[1:2, :] * dsm[1]
        for l in range(2, depth):
            inner = inner + sm[l:l + 1, :] * dsm[l]
        for l in range(depth):
            out_ref[l:l + 1, :] = sm[l:l + 1, :] * (dsm[l] - inner)
        for j in range(SUBLANES - depth):
            row = d_ref[depth + j:depth + j + 1, :]
            tot = row[:, 0:HEAD_DIM]
            for hd in range(1, HEADS):
                tot = tot + row[:, hd * HEAD_DIM:(hd + 1) * HEAD_DIM]
            out_ref[depth + j:depth + j + 1, 0:HEAD_DIM] = tot

    return pl.pallas_call(body, name=name, out_shape=jax.ShapeDtypeStruct((SUBLANES, D), F32))(lower, dlb)


def _adamw(name, parts, w, m, v, layer=None, prev=None):
    p, r, c = parts.shape
    tr = _tile(r, max(SUBLANES, ADAMW_STEP_BYTES // (4 * c * (p + 7))))
    stacked = layer is not None
    n_prev = 4 if prev is not None else 0

    def body(*refs):
        parts_ref, w_ref, m_ref, v_ref = refs[:4]
        g_ref, d_ref, m_out, v_out = refs[4 + n_prev:]
        g = parts_ref[0].astype(F32)
        for q in range(1, p):
            g = g + parts_ref[q].astype(F32)
        m2 = ADAM_B1 * m_ref[...] + (1.0 - ADAM_B1) * g
        v2 = ADAM_B2 * v_ref[...] + (1.0 - ADAM_B2) * (g * g)
        m_hat = m2 / (1.0 - ADAM_B1 ** ADAM_STEP)
        v_hat = v2 / (1.0 - ADAM_B2 ** ADAM_STEP)
        g_ref[...] = g
        d_ref[...] = -ADAM_LR * (m_hat / (jnp.sqrt(v_hat) + ADAM_EPS) + ADAM_WD * w_ref[...])
        m_out[...] = m2
        v_out[...] = v2

    if stacked:
        spec = pl.BlockSpec((None, tr, c), lambda i: (layer, i, 0))
    else:
        spec = pl.BlockSpec((tr, c), lambda i: (i, 0))
    return pl.pallas_call(
        body, name=name, grid=(r // tr,),
        in_specs=[pl.BlockSpec((p, tr, c), lambda i: (0, i, 0)), spec, spec, spec] + [ANY] * n_prev, out_specs=[spec] * 4,
        out_shape=[jax.ShapeDtypeStruct(w.shape, F32)] * 4,
        input_output_aliases={4 + q: q for q in range(n_prev)}, compiler_params=_params(),
    )(_hbm(parts), _hbm(w), _hbm(m), _hbm(v), *(prev or []))


def _hbm(a):
    return pltpu.with_memory_space_constraint(a, pltpu.HBM)


def _pad_rows(a, rows=SUBLANES):
    a = a.reshape(-1, a.shape[-1])
    return jnp.concatenate([a, jnp.zeros((rows - a.shape[0], a.shape[1]), a.dtype)], axis=0) if a.shape[0] < rows else a


def kernel(x, c, mod_w, mod_b, norm_mix, norm_mlp, norm_final, rg_w_in, rg_conv_w, rg_conv_b, rg_w_a, rg_b_a, rg_w_x, rg_b_x, rg_lambda, rg_w_out, hg_w_in, hg_lower_bounds, hg_gnorm, hg_w_out, mlp_w1, mlp_w2, loss_target, m_mod_w, m_mod_b, m_norm_mix, m_norm_mlp, m_norm_final, m_rg_w_in, m_rg_conv_w, m_rg_conv_b, m_rg_w_a, m_rg_b_a, m_rg_w_x, m_rg_b_x, m_rg_lambda, m_rg_w_out, m_hg_w_in, m_hg_lower_bounds, m_hg_gnorm, m_hg_w_out, m_mlp_w1, m_mlp_w2, v_mod_w, v_mod_b, v_norm_mix, v_norm_mlp, v_norm_final, v_rg_w_in, v_rg_conv_w, v_rg_conv_b, v_rg_w_a, v_rg_b_a, v_rg_w_x, v_rg_b_x, v_rg_lambda, v_rg_w_out, v_hg_w_in, v_hg_lower_bounds, v_hg_gnorm, v_hg_w_out, v_mlp_w1, v_mlp_w2):
    me = 4 * lax.axis_index("x") + 2 * lax.axis_index("y") + lax.axis_index("c")
    x0 = x[0]
    target = loss_target[0]
    n_rg, n_hg = rg_w_in.shape[0], hg_w_in.shape[0]
    sw_mod = mod_w.shape[2]

    c_all, cw_all = _all_gather([_pad_rows(c), rg_conv_w.reshape(n_rg * CONV_WIDTH, -1)], "gather_cond")
    c_all = c_all[:, 0, :]
    conv_w = cw_all.transpose(1, 0, 2).reshape(n_rg, CONV_WIDTH, D)
    cs_all, mod_part, lb_all = _mod_forward("mod_forward", c_all, mod_w, hg_lower_bounds)
    (mod_gathered,) = _all_gather([mod_part.reshape(DEPTH * N_DEV, sw_mod)], "gather_mod")

    ids = iter(range(5 * DEPTH))
    shards = []
    for layer in range(DEPTH):
        j = layer // 2
        w_in, w_out = (rg_w_in[j], rg_w_out[j]) if layer % 2 == 0 else (hg_w_in[j], hg_w_out[j])
        shards.append([w_in.astype(MXU), w_out.astype(MXU), mlp_w1[layer].astype(MXU), mlp_w2[layer].astype(MXU)])
    shards, mod_gathered = lax.optimization_barrier((shards, mod_gathered))
    weights = []
    for layer in range(DEPTH):
        g_in, g_out = _all_gather_async(shards[layer][:2], f"gather_mixer_weights_{layer}", next(ids), ["cols", "rows"])
        g_w1, g_w2 = _all_gather_async(shards[layer][2:], f"gather_mlp_weights_{layer}", next(ids), ["cols", "rows"])
        weights.append(dict(w_in=_hbm(g_in), w_out=_hbm(g_out), w1=_hbm(g_w1), w2=_hbm(g_w2)))

    mod_mine = lax.dynamic_index_in_dim(mod_gathered.reshape(N_DEV, DEPTH, N_DEV, sw_mod), me, axis=2, keepdims=False)
    mod = mod_mine.transpose(1, 0, 2).reshape(DEPTH, 6, D) + mod_b.reshape(DEPTH, 6, D)

    pieces, n_groups = [], 0
    placed = [dict() for _ in range(DEPTH)]

    def place(rows):
        nonlocal n_groups
        rows = [r.reshape(-1, D).astype(F32) for r in rows]
        pieces.extend(rows + [jnp.zeros((SUBLANES - sum(r.shape[0] for r in rows), D), F32)])
        n_groups += 1
        return n_groups - 1

    for layer in range(DEPTH):
        j, at = layer // 2, placed[layer]
        at["vec_mix"] = place([norm_mix[layer], mod[layer, 1], mod[layer, 0]])
        at["vec_mlp"] = place([norm_mlp[layer], mod[layer, 4], mod[layer, 3]])
        at["gate_mix"], at["gate_mlp"] = place([mod[layer, 2]]), place([mod[layer, 5]])
        if layer % 2 == 0:
            at["cw"] = place([conv_w[j]])
            at["vec"] = place([rg_conv_b[j], rg_b_a[j], rg_b_x[j], rg_lambda[j]])
        else:
            at["vec"] = place([lb_all[layer], jnp.tile(hg_gnorm[j], HEADS)])
    final_group = place([norm_final])
    table = _hbm(jnp.concatenate(pieces, axis=0))

    saved = []
    xl = x0
    for layer in range(DEPTH):
        j, wt = layer // 2, weights[layer]
        is_rg = layer % 2 == 0
        s = dict(x=xl, **{k: Group(table, g) for k, g in placed[layer].items()})
        s["h"] = h_next if layer else _hbm(_norm_mod("norm_mod", xl, s["vec_mix"]))
        if is_rg:
            (s["z"],) = _mm_tokens("rg_in", s["h"], wt["w_in"], False, (F32,))
            s["wa"], s["wx"] = _hbm(rg_w_a[j].astype(MXU)), _hbm(rg_w_x[j].astype(MXU))
            s["p"], s["hr"] = _rg_forward("rg_forward", s["z"], s["cw"], s["vec"], s["wa"], s["wx"])
        else:
            (s["z"],) = _mm_tokens("hg_in", s["h"], wt["w_in"], False, (F32,))
            s["p"], s["o"], s["ss"] = _hg_forward("hg_forward", s["z"], s["vec"])
        s["y"], s["x1"], h2 = _mm_tokens("mix_out", s["p"], wt["w_out"], False, (MXU, F32, MXU), _ep_residual_norm,
                                         extras=(xl, s["gate_mix"], s["vec_mlp"]))
        s["p"], s["h2"] = _hbm(s["p"]), _hbm(h2)
        next_norm = Group(table, placed[layer + 1]["vec_mix"]) if layer + 1 < DEPTH else None
        s["r"], s["ff"], xl, *chained = _mlp_forward("mlp_forward", s["h2"], wt["w1"], wt["w2"], s["x1"], s["gate_mlp"], next_norm)
        h_next = _hbm(chained[0]) if chained else None
        saved.append(s)

    dx, head, dff = _loss_head("loss_head", xl, target, Group(table, final_group), saved[-1]["ff"], saved[-1]["gate_mlp"])
    g_mlp = head[GATE_ROW:GATE_ROW + 1]

    results = {}
    big = dict(rg_w_in=(rg_w_in, m_rg_w_in, v_rg_w_in), rg_w_out=(rg_w_out, m_rg_w_out, v_rg_w_out),
               hg_w_in=(hg_w_in, m_hg_w_in, v_hg_w_in), hg_w_out=(hg_w_out, m_hg_w_out, v_hg_w_out),
               mlp_w1=(mlp_w1, m_mlp_w1, v_mlp_w1), mlp_w2=(mlp_w2, m_mlp_w2, v_mlp_w2))

    def update_layer(layer, landed):
        mixer = "rg" if layer % 2 == 0 else "hg"
        l_in, l_out, l_w1, l_w2 = landed
        for nm, parts, idx in ((f"{mixer}_w_in", l_in, layer // 2), (f"{mixer}_w_out", l_out, layer // 2),
                               ("mlp_w1", l_w1, layer), ("mlp_w2", l_w2, layer)):
            w, m, v = big[nm]
            results[nm] = _adamw(f"adamw_{nm}", parts, w, m, v, layer=idx, prev=results.get(nm))

    landed = [None] * DEPTH
    small = [None] * DEPTH
    parity = lax.axis_index("c").astype(jnp.int32).reshape(1)

    def send_chip_sums(layer, grads, from_sibling, anchor):
        sums = _pair_sum("pair_sum", parity, grads, from_sibling)
        sums, anchor = lax.optimization_barrier((sums, anchor))
        sums = [_hbm(a) for a in sums]
        landed[layer] = [_hbm(a) for a in _chip_exchange_async(sums, f"exchange_grads_{layer}", next(ids))]
        return anchor

    pending = None
    for layer in reversed(range(DEPTH)):
        j, wt, s = layer // 2, weights[layer], saved[layer]
        is_rg = layer % 2 == 0
        dff = _hbm(dff)
        da, dx1, n_mlp, dyb = _mlp_backward("mlp_backward", dff, s["r"], wt["w1"], wt["w2"], s["x1"], s["vec_mlp"], dx,
                                            s["y"], s["gate_mix"])
        if pending is not None:
            da = send_chip_sums(*pending, da)
        dw2 = _mm_grad("mlp_out_grad", s["r"], dff, False, prologue=_square).reshape(N_DEV, -1, D)
        dw1 = _mm_grad("mlp_in_grad", s["h2"], da, True)
        dyb = _hbm(dyb)
        (dp,) = _mm_tokens("mix_out_t", dyb, wt["w_out"], True, (F32,))
        dw_out = _hbm(_mm_grad("mix_out_grad", s["p"], dyb, False).reshape(N_DEV, -1, D))
        if is_rg:
            dz, dvec, dcw, dwa, dwx = _rg_backward("rg_backward", dp, s["z"], s["hr"], s["cw"], s["vec"], s["wa"], s["wx"])
            gate_grads, dz = lax.optimization_barrier(
                ([dwa.reshape(-1, HEAD_DIM).astype(MXU), dwx.reshape(-1, HEAD_DIM).astype(MXU)], dz))
            gate_parts = _all_gather_async(gate_grads, f"gather_gate_grads_{j}", next(ids), ["stack", "stack"])
            mixer_small = dict(dvec=dvec, dcw=dcw, gate_parts=gate_parts)
            dw_in = _mm_grad("rg_in_grad", s["h"], dz, True)
        else:
            dz, dvec = _hg_backward("hg_backward", dp, s["z"], s["o"], s["ss"], s["vec"])
            mixer_small = dict(dvec=dvec)
            dw_in = _mm_grad("hg_in_grad", s["h"], dz, True)
        grads = [dw_in, dw_out, dw1, dw2]
        pending = (layer, grads, _sibling_send_async(grads, f"pair_grads_{layer}", next(ids)))
        small[layer] = dict(g_mlp=g_mlp, n_mlp=n_mlp, g_mix=n_mlp[GATE_ROW:GATE_ROW + 1], **mixer_small)
        if layer:
            below = saved[layer - 1]
            dx, n_mix, dff = _mm_norm_grad("mixer_in_t", dz, wt["w_in"], s["x"], s["vec_mix"], dx1, below["ff"], below["gate_mlp"])
            g_mlp = n_mix[GATE_ROW:GATE_ROW + 1]
        else:
            dx, n_mix = _mm_norm_grad("mixer_in_t_first", dz, wt["w_in"], s["x"], s["vec_mix"], dx1)
        small[layer]["n_mix"] = n_mix
        if layer + 1 < DEPTH:
            stream = (dx, dff) if layer else (dx,)
            landed[layer + 1], stream = lax.optimization_barrier((landed[layer + 1], stream))
            if "gate_parts" in small[layer + 1]:
                small[layer + 1]["gate_parts"], stream = lax.optimization_barrier((small[layer + 1]["gate_parts"], stream))
            dx, dff = stream if layer else (stream[0], None)
            update_layer(layer + 1, landed[layer + 1])

    dlb_rows = [jnp.zeros((1, D), F32) if l % 2 == 0 else small[l]["dvec"][0:1] for l in range(DEPTH)]
    dgn_rows = [small[2 * j + 1]["dvec"][1:2] for j in range(n_hg)]
    lb_grad = _lower_bound_grad("lower_bound_grad", hg_lower_bounds, _pad_rows(jnp.concatenate(dlb_rows + dgn_rows, axis=0)))
    dmod = jnp.stack([jnp.concatenate([small[l]["n_mix"][2], small[l]["n_mix"][1], small[l]["g_mix"][0],
                                       small[l]["n_mlp"][2], small[l]["n_mlp"][1], small[l]["g_mlp"][0]]) for l in range(DEPTH)])
    groups = [
        dmod.reshape(DEPTH * 6, D),
        jnp.stack([small[l]["n_mix"][0] for l in range(DEPTH)]),
        jnp.stack([small[l]["n_mlp"][0] for l in range(DEPTH)]),
        head[0:1],
        jnp.stack([small[2 * j]["dvec"][0] for j in range(n_rg)]),
        jnp.stack([small[2 * j]["dvec"][1] for j in range(n_rg)]),
        jnp.stack([small[2 * j]["dvec"][2] for j in range(n_rg)]),
        jnp.stack([small[2 * j]["dvec"][3] for j in range(n_rg)]),
        lb_grad[0:DEPTH],
        jnp.concatenate([lb_grad[DEPTH + j:DEPTH + j + 1, 0:HEAD_DIM] for j in range(n_hg)]
                        + [jnp.zeros((1, D - n_hg * HEAD_DIM), F32)], axis=1),
        jnp.concatenate([small[2 * j]["dcw"][0:CONV_WIDTH] for j in range(n_rg)], axis=0),
        head[1:2],
    ]
    params = [mod_b, norm_mix, norm_mlp, norm_final, rg_conv_b, rg_b_a, rg_b_x, rg_lambda, hg_lower_bounds, hg_gnorm]
    moms = [m_mod_b, m_norm_mix, m_norm_mlp, m_norm_final, m_rg_conv_b, m_rg_b_a, m_rg_b_x, m_rg_lambda, m_hg_lower_bounds, m_hg_gnorm]
    vars_ = [v_mod_b, v_norm_mix, v_norm_mlp, v_norm_final, v_rg_conv_b, v_rg_b_a, v_rg_b_x, v_rg_lambda, v_hg_lower_bounds, v_hg_gnorm]
    offsets, rows_of, at = [], [], 0
    for g in groups:
        offsets.append(at)
        rows_of.append(g.shape[0])
        at += -(-g.shape[0] // SUBLANES) * SUBLANES
    packed = jnp.concatenate([_pad_rows(g, -(-g.shape[0] // SUBLANES) * SUBLANES) for g in groups], axis=0)
    (small_parts,) = _all_gather_async([packed], "gather_small_grads", next(ids), ["stack"])
    send_chip_sums(*pending, dx)

    def pack_like(arrs):
        out = []
        for g_rows, off, a in zip(rows_of, offsets, arrs):
            flat = a.reshape(-1)
            flat = jnp.concatenate([flat, jnp.zeros((g_rows * D - flat.shape[0],), F32)])
            out.append(_pad_rows(flat.reshape(g_rows, D), -(-g_rows // SUBLANES) * SUBLANES))
        rest = packed.shape[0] - sum(o.shape[0] for o in out)
        return jnp.concatenate(out + [jnp.zeros((rest, D), F32)], axis=0)

    small_out = _adamw("adamw_small", small_parts, pack_like(params), pack_like(moms), pack_like(vars_))

    def unpack(q, idx, like):
        rows = small_out[q][offsets[idx]:offsets[idx] + rows_of[idx]]
        return rows.reshape(-1)[:like.size].reshape(like.shape)

    loss = jnp.sum(small_out[0][offsets[11]])
    names =["mod_b", "norm_mix", "norm_mlp", "norm_final", "rg_conv_b", "rg_b_a", "rg_b_x", "rg_lambda", "hg_lower_bounds", "hg_gnorm"]
    for idx, (nm, like) in enumerate(zip(names, params)):
        results[nm] = [unpack(q, idx, like) for q in range(4)]

    cw_parts = lax.dynamic_slice_in_dim(small_parts[:, offsets[10]:offsets[10] + n_rg * CONV_WIDTH, :], me * (D // N_DEV), D // N_DEV, axis=2)
    shp = rg_conv_w.shape
    results["rg_conv_w"] = [o.reshape(shp) for o in _adamw(
        "adamw_conv", cw_parts, rg_conv_w.reshape(-1, shp[-1]), m_rg_conv_w.reshape(-1, shp[-1]), v_rg_conv_w.reshape(-1, shp[-1]))]
    shp = rg_w_a.shape
    stacked = (n_rg, HEADS * HEAD_DIM, HEAD_DIM)
    for nm, which, (w, m, v) in (("rg_w_a", 0, (rg_w_a, m_rg_w_a, v_rg_w_a)), ("rg_w_x", 1, (rg_w_x, m_rg_w_x, v_rg_w_x))):
        out = None
        for j in reversed(range(n_rg)):
            out = _adamw("adamw_gate", small[2 * j]["gate_parts"][which], w.reshape(stacked), m.reshape(stacked),
                         v.reshape(stacked), layer=j, prev=out)
        results[nm] = [o.reshape(shp) for o in out]

    dmod_all = small_parts[:, 0:DEPTH * 6, :].reshape(N_DEV, DEPTH, 6 * D)
    dmod_cols = lax.dynamic_slice_in_dim(dmod_all, me * sw_mod, sw_mod, axis=2).transpose(1, 0, 2)
    pad = HEAD_DIM - N_DEV
    dmod_pad = jnp.concatenate([dmod_cols, jnp.zeros((DEPTH, pad, sw_mod), F32)], axis=1).astype(MXU)
    cs_t = jnp.concatenate([cs_all.T, jnp.zeros((D, pad), F32)], axis=1).astype(MXU)
    g_mod_w = _mod_weight_grad("mod_weight_grad", cs_t, dmod_pad)
    results["mod_w"] = [o.reshape(mod_w.shape) for o in _adamw(
        "adamw_mod", g_mod_w.reshape(1, -1, sw_mod), mod_w.reshape(-1, sw_mod), m_mod_w.reshape(-1, sw_mod), v_mod_w.reshape(-1, sw_mod))]

    update_layer(0, landed[0])

    order = ["mod_w", "mod_b", "norm_mix", "norm_mlp", "norm_final", "rg_w_in", "rg_conv_w", "rg_conv_b", "rg_w_a", "rg_b_a", "rg_w_x",
             "rg_b_x", "rg_lambda", "rg_w_out", "hg_w_in", "hg_lower_bounds", "hg_gnorm", "hg_w_out", "mlp_w1", "mlp_w2"]
    return (loss, dx[None], *[results[n][0] for n in order], *[results[n][1] for n in order],
            *[results[n][2] for n in order], *[results[n][3] for n in order])
```

```python
import collections

import jax
import jax.numpy as jnp
from jax import lax
from jax.experimental import pallas as pl
from jax.experimental.pallas import tpu as pltpu
from jax.experimental.pallas import tpu_sc as plsc

F32 = jnp.float32
MXU = jnp.bfloat16

N_DEV = 8
D = 1024
DEPTH = 4
HEADS = 8
HEAD_DIM = 128
CONV_WIDTH = 4
LRU_C = 8.0
HG_CHUNK = 64
NORM_EPS = 1e-6
GNORM_EPS = 1e-5
ADAM_LR = 0.001
ADAM_B1 = 0.9
ADAM_B2 = 0.999
ADAM_EPS = 1e-08
ADAM_WD = 0.01
ADAM_STEP = 10
GELU_C = 0.7978845608028654
GELU_K = 0.044715
EXP_CLAMP = 80.0
SUBLANES = 8
VMEM_LIMIT = 48 * 1024 * 1024
ROW_TILE = 256
RG_ROWS = 256
MM_TILE = 1024
MM_TOKENS = 512
MLP_TOKENS = 256
ADAMW_STEP_BYTES = 8 * 1024 * 1024

MESH = pl.DeviceIdType.MESH
ANY = pl.BlockSpec(memory_space=pl.ANY)


def _params():
    return pltpu.CompilerParams(vmem_limit_bytes=VMEM_LIMIT)


def _tile(n, target):
    if n <= target:
        return n
    t = target // SUBLANES * SUBLANES
    while n % t:
        t -= SUBLANES
    return t


def _sigmoid(x):
    return 1.0 / (1.0 + jnp.exp(-x))


def _silu(x):
    return x * _sigmoid(x)


def _silu_and_grad(x):
    s = _sigmoid(x)
    return x * s, s * (1.0 + x * (1.0 - s))


def _gelu(y):
    return 0.5 * y * (1.0 + jnp.tanh(GELU_C * (y + GELU_K * y * y * y)))


def _gelu_and_grad(y):
    y2 = y * y
    th = jnp.tanh(GELU_C * (y + GELU_K * y2 * y))
    half = 0.5 * (1.0 + th)
    return y * half, half + 0.5 * y * (1.0 - th * th) * GELU_C * (1.0 + 3.0 * GELU_K * y2)


def _dot(a, b):
    return lax.dot_general(a.astype(MXU), b.astype(MXU), (((1,), (0,)), ((), ())), preferred_element_type=F32)


def _dot_nt(a, b):
    return lax.dot_general(a.astype(MXU), b.astype(MXU), (((1,), (1,)), ((), ())), preferred_element_type=F32)


def _dot_tn(a, b):
    return lax.dot_general(a.astype(MXU), b.astype(MXU), (((0,), (0,)), ((), ())), preferred_element_type=F32)


def _dot_exact(tri, x):
    t = tri.astype(MXU)
    hi = x.astype(MXU)
    r1 = x - hi.astype(F32)
    mid = r1.astype(MXU)
    lo = (r1 - mid.astype(F32)).astype(MXU)
    dn = (((1,), (0,)), ((), ()))
    return (lax.dot_general(t, hi, dn, preferred_element_type=F32) + lax.dot_general(t, mid, dn, preferred_element_type=F32)
            + lax.dot_general(t, lo, dn, preferred_element_type=F32))


def _row_sum(v):
    return jnp.sum(v, axis=0, keepdims=True)


def _handshake(partners):
    barrier = pltpu.get_barrier_semaphore()
    for p in partners:
        pl.semaphore_signal(barrier, inc=1, device_id=p, device_id_type=MESH)
    pl.semaphore_wait(barrier, len(partners))


def _gather_body(n, per_array_sems, handshake, layouts):
    def body(*refs):
        ins, outs = refs[:n], refs[n:2 * n]
        send_sems, recv_sems, local_sems = refs[2 * n:]
        x, y, c = lax.axis_index("x"), lax.axis_index("y"), lax.axis_index("c")
        me, sibling = (x, y, c), (x, y, 1 - c)
        chips = [(1 - x, y), (x, 1 - y), (1 - x, 1 - y)]
        if handshake:
            _handshake([sibling] + [(*chip, c) for chip in chips])

        def sem(sems, a, k):
            return sems.at[a, k] if per_array_sems else sems.at[k]

        def slot(a, p):
            block = 4 * p[0] + 2 * p[1] + p[2]
            r, c_ = ins[a].shape
            if layouts[a] == "rows":
                return outs[a].at[pl.ds(block * r, r), :]
            if layouts[a] == "cols":
                return outs[a].at[:, pl.ds(block * c_, c_)]
            return outs[a].at[block]

        def copy(a, k, block, to, src=None):
            return pltpu.make_async_remote_copy(
                src_ref=slot(a, block) if src is None else src, dst_ref=slot(a, block),
                send_sem=sem(send_sems, a, k), recv_sem=sem(recv_sems, a, k), device_id=to, device_id_type=MESH)

        mine = [pltpu.make_async_copy(ins[a], slot(a, me), local_sems.at[a if per_array_sems else 0]) for a in range(n)]
        for cp in mine:
            cp.start()
        first = []
        for a in range(n):
            first.append(copy(a, 0, me, sibling, src=ins[a]))
            first += [copy(a, 1 + j, me, (*chip, c), src=ins[a]) for j, chip in enumerate(chips)]
        for cp in first:
            cp.start()
        passed = []
        for j, chip in enumerate(chips):
            for a in range(n):
                copy(a, 1 + j, (*chip, c), me).wait_recv()
            for a in range(n):
                cp = copy(a, 4 + j, (*chip, c), sibling)
                cp.start()
                passed.append(cp)
        for a in range(n):
            copy(a, 0, sibling, me).wait_recv()
        for j, chip in enumerate(chips):
            for a in range(n):
                copy(a, 4 + j, (*chip, 1 - c), me).wait_recv()
        for cp in first + passed:
            cp.wait_send()
        for cp in mine:
            cp.wait()

    return body


def _sibling_send_body(n, per_array_sems, handshake):
    def body(*refs):
        ins, outs = refs[:n], refs[n:2 * n]
        send_sems, recv_sems, _ = refs[2 * n:]
        x, y, c = lax.axis_index("x"), lax.axis_index("y"), lax.axis_index("c")
        sibling = (x, y, 1 - c)
        if handshake:
            _handshake([sibling])

        def sem(sems, a):
            return sems.at[a, 0] if per_array_sems else sems.at[0]

        copies = [pltpu.make_async_remote_copy(
            src_ref=ins[a].at[2 * q + 1 - c], dst_ref=outs[a].at[q], send_sem=sem(send_sems, a), recv_sem=sem(recv_sems, a),
            device_id=sibling, device_id_type=MESH) for a in range(n) for q in range(4)]
        for cp in copies:
            cp.start()
        for cp in copies:
            cp.wait_recv()
        for cp in copies:
            cp.wait_send()

    return body


def _chip_exchange_body(n, per_array_sems, handshake):
    def body(*refs):
        ins, outs = refs[:n], refs[n:2 * n]
        send_sems, recv_sems, local_sems = refs[2 * n:]
        x, y, c = lax.axis_index("x"), lax.axis_index("y"), lax.axis_index("c")
        my_chip = 2 * x + y
        chips = [(1 - x, y), (x, 1 - y), (1 - x, 1 - y)]
        if handshake:
            _handshake([(*chip, c) for chip in chips])

        def sem(sems, a, k):
            return sems.at[a, k] if per_array_sems else sems.at[k]

        def copy(a, k, landing):
            px, py = chips[k]
            return pltpu.make_async_remote_copy(
                src_ref=ins[a].at[2 * px + py], dst_ref=outs[a].at[landing], send_sem=sem(send_sems, a, k),
                recv_sem=sem(recv_sems, a, k), device_id=(px, py, c), device_id_type=MESH)

        mine = [pltpu.make_async_copy(ins[a].at[my_chip], outs[a].at[my_chip], local_sems.at[a if per_array_sems else 0])
                for a in range(n)]
        for cp in mine:
            cp.start()
        sent = [copy(a, k, my_chip) for a in range(n) for k in range(3)]
        for cp in sent:
            cp.start()
        for a in range(n):
            for k, (px, py) in enumerate(chips):
                copy(a, k, 2 * px + py).wait_recv()
        for cp in sent:
            cp.wait_send()
        for cp in mine:
            cp.wait()

    return body


def _all_gather(arrs, name):
    n = len(arrs)
    return pl.pallas_call(
        _gather_body(n, True, False, ["stack"] * n), name=name, in_specs=[ANY] * n, out_specs=[ANY] * n,
        out_shape=[jax.ShapeDtypeStruct((N_DEV,) + a.shape, a.dtype) for a in arrs],
        scratch_shapes=[pltpu.SemaphoreType.DMA((n, 7)), pltpu.SemaphoreType.DMA((n, 7)), pltpu.SemaphoreType.DMA((n,))],
    )(*arrs)


def _on_sequencer(body, arrs, out_type, name, collective_id):
    return pl.kernel(
        body, name=name, out_type=out_type, mesh=plsc.ScalarSubcoreMesh(axis_name="sequencer", num_cores=1),
        scratch_types=[pltpu.SemaphoreType.DMA((RING_SEMS,)), pltpu.SemaphoreType.DMA((RING_SEMS,)),
                       pltpu.SemaphoreType.DMA((1,))],
        compiler_params=pltpu.CompilerParams(collective_id=collective_id))(*arrs)


RING_SEMS = 8


def _ring_gather_body(n, per_array_sems, handshake, layouts):
    def body(*refs):
        ins, outs = refs[:n], refs[n:2 * n]
        send_sems, recv_sems, local_sems = refs[2 * n:]
        x, y, c = lax.axis_index("x"), lax.axis_index("y"), lax.axis_index("c")
        me, sibling = (x, y, c), (x, y, 1 - c)
        over_x, over_y, diagonal = (1 - x, y, c), (x, 1 - y, c), (1 - x, 1 - y, c)
        if handshake:
            _handshake([sibling, over_x, over_y])

        def sem(sems, a, k):
            return sems.at[a, k] if per_array_sems else sems.at[k]

        def slot(a, p, half=None):
            block = 4 * p[0] + 2 * p[1] + p[2]
            r, c_ = ins[a].shape
            first, count = (0, r) if half is None else (half * (r // 2), r // 2)
            if layouts[a] == "rows":
                return outs[a].at[pl.ds(block * r + first, count), :]
            if layouts[a] == "cols":
                return outs[a].at[pl.ds(first, count), pl.ds(block * c_, c_)]
            return outs[a].at[block, pl.ds(first, count), :]

        def copy(a, k, block, to, half=None, src=None):
            return pltpu.make_async_remote_copy(
                src_ref=slot(a, block, half) if src is None else src, dst_ref=slot(a, block, half),
                send_sem=sem(send_sems, a, k), recv_sem=sem(recv_sems, a, k), device_id=to, device_id_type=MESH)

        def start(copies):
            for cp in copies:
                cp.start()
            return copies

        def arrived(k, block, half=None):
            for a in range(n):
                copy(a, k, block, me, half).wait_recv()

        arrays = range(n)
        mine = start([pltpu.make_async_copy(ins[a], slot(a, me), local_sems.at[a if per_array_sems else 0]) for a in arrays])
        sent = start([copy(a, k, me, to, src=ins[a]) for a in arrays for k, to in ((0, sibling), (1, over_x), (2, over_y))])
        arrived(1, over_x)
        sent += start([copy(a, 3, over_x, over_y, half=0) for a in arrays] + [copy(a, 5, over_x, sibling) for a in arrays])
        arrived(2, over_y)
        sent += start([copy(a, 4, over_y, over_x, half=1) for a in arrays] + [copy(a, 6, over_y, sibling) for a in arrays])
        arrived(3, diagonal, half=0)
        arrived(4, diagonal, half=1)
        sent += start([copy(a, 7, diagonal, sibling) for a in arrays])
        arrived(0, sibling)
        for k, chip in ((5, over_x), (6, over_y), (7, diagonal)):
            arrived(k, (chip[0], chip[1], 1 - c))
        for cp in sent:
            cp.wait_send()
        for cp in mine:
            cp.wait()

    return body


def _all_gather_async(arrs, name, collective_id, layouts):
    shape = dict(stack=lambda r, c: (N_DEV, r, c), rows=lambda r, c: (N_DEV * r, c), cols=lambda r, c: (r, N_DEV * c))
    out_type = [jax.ShapeDtypeStruct(shape[lay](*a.shape), a.dtype) for a, lay in zip(arrs, layouts)]
    return _on_sequencer(_ring_gather_body(len(arrs), False, True, layouts), arrs, out_type, name, collective_id)


def _sibling_send_async(arrs, name, collective_id):
    out_type = [jax.ShapeDtypeStruct((N_DEV // 2,) + a.shape[1:], a.dtype) for a in arrs]
    return _on_sequencer(_sibling_send_body(len(arrs), False, True), arrs, out_type, name, collective_id)


def _chip_exchange_async(arrs, name, collective_id):
    out_type = [jax.ShapeDtypeStruct(a.shape, a.dtype) for a in arrs]
    return _on_sequencer(_chip_exchange_body(len(arrs), False, True), arrs, out_type, name, collective_id)


def _pair_sum(name, parity, mine, theirs):
    n = len(mine)

    def body(par_ref, *refs):
        for a in range(n):
            refs[2 * n + a][...] = (refs[a][...].astype(F32) + refs[n + a][...].astype(F32)).astype(refs[2 * n + a].dtype)

    def block(a):
        return (None,) + a.shape[1:]

    grid_spec = pltpu.PrefetchScalarGridSpec(
        num_scalar_prefetch=1, grid=(N_DEV // 2,),
        in_specs=[pl.BlockSpec(block(a), lambda q, par: (2 * q + par[0], 0, 0)) for a in mine]
        + [pl.BlockSpec(block(a), lambda q, par: (q, 0, 0)) for a in theirs],
        out_specs=[pl.BlockSpec(block(a), lambda q, par: (q, 0, 0)) for a in theirs])
    return pl.pallas_call(body, name=name, grid_spec=grid_spec, out_shape=[jax.ShapeDtypeStruct(a.shape, a.dtype) for a in theirs],
                          compiler_params=_params())(parity, *mine, *theirs)


NN = (((1,), (0,)), ((), ()))
NT = (((1,), (1,)), ((), ()))
TN = (((0,), (0,)), ((), ()))


def _matmul(name, a, b, dims, grid, a_spec, b_spec, outs, epilogue, extras=(), prologue=None):
    n_in = 2 + len(extras)
    n_out = len(outs)

    def body(*refs):
        a_ref, b_ref = refs[0], refs[1]
        ex, out_refs = refs[2:n_in], refs[n_in:n_in + n_out]
        a_tile = a_ref[...] if prologue is None else prologue(a_ref[...])
        epilogue(lax.dot_general(a_tile, b_ref[...], dims, preferred_element_type=F32), ex, out_refs)

    return pl.pallas_call(
        body, name=name, grid=grid, in_specs=[a_spec, b_spec] + [s for _, s in extras], out_specs=[s for _, _, s in outs],
        out_shape=[jax.ShapeDtypeStruct(sh, dt) for sh, dt, _ in outs], compiler_params=_params(),
    )(a, b, *[e for e, _ in extras])


def _square(tile):
    return tile * tile


def _ep_store(acc, ex, outs):
    outs[0][...] = acc.astype(outs[0].dtype)


def _norm_mod_rows(xv, v_ref):
    r = lax.rsqrt(jnp.mean(xv * xv, axis=-1, keepdims=True) + NORM_EPS)
    h = (xv * r) * v_ref[0:1, :]
    return h * (1.0 + v_ref[1:2, :]) + v_ref[2:3, :]


def _ep_residual_norm(acc, ex, outs):
    outs[0][...] = acc.astype(outs[0].dtype)
    xv = ex[0][...] + ex[1][0:1, :] * acc
    outs[1][...] = xv
    outs[2][...] = _norm_mod_rows(xv, ex[2]).astype(outs[2].dtype)


def _mm_tokens(name, a, w, transposed, out_dtypes, epilogue=_ep_store, extras=(), prologue=None, rows=MM_TOKENS):
    m, n = a.shape[0], w.shape[0 if transposed else 1]
    tm = _tile(m, rows)
    rows_spec = lambda width: pl.BlockSpec((tm, width), lambda i, j, kk: (i, 0))
    whole = lambda arr: pl.BlockSpec(arr.shape, lambda i, j, kk: (0, 0))
    return _matmul(name, a, w, NT if transposed else NN, (m // tm, 1, 1), rows_spec(a.shape[1]), whole(w),
                   [((m, n), dt, rows_spec(n)) for dt in out_dtypes], epilogue,
                   extras=[(e.table, _group_spec(e, 3)) if isinstance(e, Group) else (e, rows_spec(n)) for e in extras],
                   prologue=prologue)


def _resident(w):
    return pl.BlockSpec(w.shape, lambda i: (0, 0), pipeline_mode=pl.Buffered(1))


def _mlp_forward(name, h, w1, w2, x, gate, next_norm=None):
    m, d = x.shape
    f = w1.shape[1]
    tm = _tile(m, MLP_TOKENS)
    rows = lambda width: pl.BlockSpec((tm, width), lambda i: (i, 0))
    chained = next_norm is not None

    def body(h_ref, w1_ref, w2_ref, x_ref, g_ref, *rest):
        r_ref, ff_ref, out_ref = rest[chained:chained + 3]
        r = jnp.maximum(lax.dot_general(h_ref[...], w1_ref[...], NN, preferred_element_type=F32), 0.0).astype(r_ref.dtype)
        r_ref[...] = r
        ff = lax.dot_general(r * r, w2_ref[...], NN, preferred_element_type=F32)
        ff_ref[...] = ff.astype(ff_ref.dtype)
        xv = x_ref[...] + g_ref[0:1, :] * ff
        out_ref[...] = xv
        if chained:
            rest[4][...] = _norm_mod_rows(xv, rest[0]).astype(rest[4].dtype)

    return pl.pallas_call(
        body, name=name, grid=(m // tm,),
        in_specs=[rows(d), _resident(w1), _resident(w2), rows(d), _group_spec(gate)] + ([_group_spec(next_norm)] if chained else []),
        out_specs=[rows(f), rows(d), rows(d)] + ([rows(d)] if chained else []),
        out_shape=[jax.ShapeDtypeStruct((m, f), MXU), jax.ShapeDtypeStruct((m, d), MXU), jax.ShapeDtypeStruct((m, d), F32)]
        + ([jax.ShapeDtypeStruct((m, d), MXU)] if chained else []),
        compiler_params=_params())(h, w1, w2, x, gate.table, *([next_norm.table] if chained else []))


def _mlp_backward(name, dff, r, w1, w2, x, vec, dres, y, gate):
    m, d = dff.shape
    f = r.shape[1]
    tm = _tile(m, MLP_TOKENS)
    nt = m // tm
    rows = lambda width: pl.BlockSpec((tm, width), lambda i: (i, 0))

    def body(dff_ref, r_ref, w1_ref, w2_ref, x_ref, v_ref, dres_ref, y_ref, g_ref, da_ref, dx_ref, acc_ref, dy_ref):
        ds = lax.dot_general(dff_ref[...], w2_ref[...], NT, preferred_element_type=F32)
        da = (ds * (2.0 * r_ref[...].astype(F32))).astype(da_ref.dtype)
        da_ref[...] = da
        dh = lax.dot_general(da, w1_ref[...], NT, preferred_element_type=F32)
        _norm_grad_step(nt, dh, x_ref, v_ref, dres_ref, dx_ref, acc_ref, (y_ref, g_ref, dy_ref))

    return pl.pallas_call(
        body, name=name, grid=(nt,),
        in_specs=[rows(d), rows(f), _resident(w1), _resident(w2), rows(d), _group_spec(vec), rows(d), rows(d), _group_spec(gate)],
        out_specs=[rows(f), rows(d), _vec_spec(), rows(d)],
        out_shape=[jax.ShapeDtypeStruct((m, f), MXU), jax.ShapeDtypeStruct((m, d), F32),
                   jax.ShapeDtypeStruct((SUBLANES, d), F32), jax.ShapeDtypeStruct((m, d), MXU)],
        compiler_params=_params())(dff, r, w1, w2, x, vec.table, dres, y, gate.table)


def _mm_norm_grad(name, a, w, x, vec, dres, y=None, gate=None):
    m = a.shape[0]
    tm = _tile(m, MM_TOKENS)
    nt = m // tm
    branch = y is not None
    rows = lambda width: pl.BlockSpec((tm, width), lambda i: (i, 0))

    def body(a_ref, w_ref, x_ref, v_ref, dres_ref, *rest):
        dh = lax.dot_general(a_ref[...], w_ref[...], NT, preferred_element_type=F32)
        dx_ref, acc_ref = rest[2 * branch], rest[2 * branch + 1]
        _norm_grad_step(nt, dh, x_ref, v_ref, dres_ref, dx_ref, acc_ref, (rest[0], rest[1], rest[4]) if branch else None)

    return pl.pallas_call(
        body, name=name, grid=(nt,),
        in_specs=[rows(a.shape[1]), _resident(w), rows(D), _group_spec(vec), rows(D)] + ([rows(D), _group_spec(gate)] if branch else []),
        out_specs=[rows(D), _vec_spec()] + ([rows(D)] if branch else []),
        out_shape=[jax.ShapeDtypeStruct((m, D), F32), jax.ShapeDtypeStruct((SUBLANES, D), F32)]
        + ([jax.ShapeDtypeStruct((m, D), MXU)] if branch else []),
        compiler_params=_params())(a, w, x, vec.table, dres, *([y, gate.table] if branch else []))


def _mm_grad(name, a, b, shard_cols, prologue=None):
    t, m = a.shape
    n = b.shape[1]
    tm = _tile(m, MM_TILE)
    if shard_cols:
        tn = n // N_DEV
        out = ((N_DEV, m, tn), MXU, pl.BlockSpec((None, tm, tn), lambda i, j, kk: (j, i, 0)))
    else:
        tn = _tile(n, MM_TILE)
        out = ((m, n), MXU, pl.BlockSpec((tm, tn), lambda i, j, kk: (i, j)))
    return _matmul(name, a, b, TN, (m // tm, n // tn, 1), pl.BlockSpec((t, tm), lambda i, j, kk: (0, i)),
                   pl.BlockSpec((t, tn), lambda i, j, kk: (0, j)), [out], _ep_store, prologue=prologue)[0]


def _rows_spec(tt, width=D):
    return pl.BlockSpec((tt, width), lambda i: (i, 0))


def _vec_spec(rows=SUBLANES, width=D):
    return pl.BlockSpec((rows, width), lambda i: (0, 0))


Group = collections.namedtuple("Group", ["table", "index"])


def _group_spec(group, grid_rank=1):
    if grid_rank == 1:
        return pl.BlockSpec((SUBLANES, D), lambda i: (group.index, 0))
    return pl.BlockSpec((SUBLANES, D), lambda i, j, kk: (group.index, 0))


def _norm_mod(name, x, vec):
    t = x.shape[0]
    tt = _tile(t, ROW_TILE)

    def body(x_ref, v_ref, h_ref):
        xv = x_ref[...]
        r = lax.rsqrt(jnp.mean(xv * xv, axis=-1, keepdims=True) + NORM_EPS)
        h = (xv * r) * v_ref[0:1, :]
        h_ref[...] = (h * (1.0 + v_ref[1:2, :]) + v_ref[2:3, :]).astype(h_ref.dtype)

    return pl.pallas_call(body, name=name, grid=(t // tt,), in_specs=[_rows_spec(tt), _group_spec(vec)], out_specs=_rows_spec(tt),
                          out_shape=jax.ShapeDtypeStruct((t, D), MXU))(x, vec.table)


GATE_ROW = 3


def _branch_grad(dx, y_ref, g_ref, dy_ref, acc_ref):
    dy_ref[...] = (dx * g_ref[0:1, :]).astype(dy_ref.dtype)
    acc_ref[GATE_ROW:GATE_ROW + 1, :] += _row_sum(dx * y_ref[...])


def _norm_grad_step(nt, dh, x_ref, v_ref, dres_ref, dx_ref, acc_ref, branch):
    i = pl.program_id(0)

    @pl.when(i == 0)
    def _():
        acc_ref[...] = jnp.zeros_like(acc_ref)

    xv = x_ref[...]
    r = lax.rsqrt(jnp.mean(xv * xv, axis=-1, keepdims=True) + NORM_EPS)
    xn = xv * r
    w = v_ref[0:1, :] * (1.0 + v_ref[1:2, :])
    acc_ref[0:1, :] += _row_sum(dh * xn)
    acc_ref[2:3, :] += _row_sum(dh)
    dxn = dh * w
    dx = dres_ref[...] + r * (dxn - xn * jnp.mean(dxn * xn, axis=-1, keepdims=True))
    dx_ref[...] = dx
    if branch is not None:
        _branch_grad(dx, branch[0], branch[1], branch[2], acc_ref)

    @pl.when(i == nt - 1)
    def _():
        dw = acc_ref[0:1, :]
        acc_ref[1:2, :] = dw * v_ref[0:1, :]
        acc_ref[0:1, :] = dw * (1.0 + v_ref[1:2, :])


def _loss_head(name, x, target, vec, y, gate):
    t = x.shape[0]
    tt = _tile(t, ROW_TILE)

    def body(x_ref, t_ref, v_ref, y_ref, g_ref, dx_ref, acc_ref, dy_ref):
        @pl.when(pl.program_id(0) == 0)
        def _():
            acc_ref[...] = jnp.zeros_like(acc_ref)

        xv = x_ref[...]
        r = lax.rsqrt(jnp.mean(xv * xv, axis=-1, keepdims=True) + NORM_EPS)
        xn = xv * r
        gain = v_ref[0:1, :]
        err = xn * gain - t_ref[...]
        acc_ref[1:2, :] += _row_sum(err * err) * (0.5 / D)
        dout = err * (1.0 / D)
        acc_ref[0:1, :] += _row_sum(dout * xn)
        dxn = dout * gain
        dx = r * (dxn - xn * jnp.mean(dxn * xn, axis=-1, keepdims=True))
        dx_ref[...] = dx
        _branch_grad(dx, y_ref, g_ref, dy_ref, acc_ref)

    return pl.pallas_call(
        body, name=name, grid=(t // tt,),
        in_specs=[_rows_spec(tt), _rows_spec(tt), _group_spec(vec), _rows_spec(tt), _group_spec(gate)],
        out_specs=[_rows_spec(tt), _vec_spec(), _rows_spec(tt)],
        out_shape=[jax.ShapeDtypeStruct((t, D), F32), jax.ShapeDtypeStruct((SUBLANES, D), F32),
                   jax.ShapeDtypeStruct((t, D), MXU)])(x, target, vec.table, y, gate.table)


def _shift_down(x, halo, k):
    y = pltpu.roll(x, k, 0)
    top = jnp.where(lax.broadcasted_iota(jnp.int32, halo.shape, 0) < k, pltpu.roll(halo, k, 0), y[0:SUBLANES, :])
    return jnp.concatenate([top, y[SUBLANES:, :]], axis=0)


def _shift_up(x, halo, k):
    n = x.shape[0]
    y = pltpu.roll(x, n - k, 0)
    bottom = jnp.where(lax.broadcasted_iota(jnp.int32, halo.shape, 0) >= SUBLANES - k, pltpu.roll(halo, SUBLANES - k, 0),
                       y[n - SUBLANES:, :])
    return jnp.concatenate([y[:n - SUBLANES, :], bottom], axis=0)


def _rg_gates(xb, halo, cw_ref, vec_ref, wa_ref, wx_ref, at_start):
    shifted = [xb] + [_shift_down(xb, halo, k) for k in range(1, CONV_WIDTH)]
    xc = vec_ref[0:1, :] + shifted[0] * cw_ref[CONV_WIDTH - 1:CONV_WIDTH, :]
    for k in range(1, CONV_WIDTH):
        xc = xc + shifted[k] * cw_ref[CONV_WIDTH - 1 - k:CONV_WIDTH - k, :]
    heads = [slice(h * HEAD_DIM, (h + 1) * HEAD_DIM) for h in range(HEADS)]
    pa = jnp.concatenate([_dot(xc[:, s], wa_ref[h]) for h, s in enumerate(heads)], axis=1) + vec_ref[1:2, :]
    px = jnp.concatenate([_dot(xc[:, s], wx_ref[h]) for h, s in enumerate(heads)], axis=1) + vec_ref[2:3, :]
    ra, ia = _sigmoid(pa), _sigmoid(px)
    nl = -vec_ref[3:4, :]
    sp = jnp.maximum(nl, 0.0) + jnp.log(1.0 + jnp.exp(-jnp.abs(nl)))
    log_a = (-LRU_C) * ra * sp
    a = jnp.exp(log_a)
    th = jnp.tanh(log_a)
    is_t0 = jnp.logical_and(lax.broadcasted_iota(jnp.int32, xb.shape, 0) == 0, at_start)
    mult = jnp.where(is_t0, 1.0, jnp.sqrt(-2.0 * th / (1.0 - th)))
    return dict(shifted=shifted, xc=xc, ra=ra, ia=ia, sp=sp, a=a, mult=mult, is_t0=is_t0, heads=heads)


def _rg_specs(tt, nt, order, cw, vec):
    blk = tt // SUBLANES
    return dict(
        x=pl.BlockSpec((tt, D), lambda i: (order(i), 0)), y=pl.BlockSpec((tt, D), lambda i: (order(i), 1)),
        halo=pl.BlockSpec((SUBLANES, D), lambda i: (jnp.maximum(order(i) * blk - 1, 0), 0)),
        cw=_group_spec(cw), vec=_group_spec(vec), w=pl.BlockSpec((HEADS, HEAD_DIM, HEAD_DIM), lambda i: (0, 0, 0)))


def _rg_forward(name, z, cw, vec, wa, wx):
    t = z.shape[0]
    tt = _tile(t, RG_ROWS)
    nt = t // tt
    sp = _rg_specs(tt, nt, lambda i: i, cw, vec)

    def body(zx_ref, zy_ref, halo_ref, cw_ref, vec_ref, wa_ref, wx_ref, p_ref, h_ref, a_s, u_s, carry):
        i = pl.program_id(0)

        @pl.when(i == 0)
        def _():
            carry[...] = jnp.zeros_like(carry)

        halo = jnp.where(i > 0, halo_ref[...], 0.0)
        g = _rg_gates(zx_ref[...], halo, cw_ref, vec_ref, wa_ref, wx_ref, i == 0)
        a_s[...] = g["a"]
        u_s[...] = g["mult"] * (g["ia"] * g["xc"])

        def group(gi, h):
            rows = pl.ds(pl.multiple_of(gi * SUBLANES, SUBLANES), SUBLANES)
            a8, u8 = a_s[rows, :], u_s[rows, :]
            out = []
            for j in range(SUBLANES):
                h = a8[j:j + 1, :] * h + u8[j:j + 1, :]
                out.append(h)
            h_ref[rows, :] = jnp.concatenate(out, axis=0)
            return h

        carry[0:1, :] = lax.fori_loop(0, tt // SUBLANES, group, carry[0:1, :])
        p_ref[...] = (h_ref[...] * _gelu(zy_ref[...])).astype(p_ref.dtype)

    return pl.pallas_call(
        body, name=name, grid=(nt,), in_specs=[sp["x"], sp["y"], sp["halo"], sp["cw"], sp["vec"], sp["w"], sp["w"]],
        out_specs=[_rows_spec(tt), _rows_spec(tt)],
        out_shape=[jax.ShapeDtypeStruct((t, D), MXU), jax.ShapeDtypeStruct((t, D), F32)],
        scratch_shapes=[pltpu.VMEM((tt, D), F32), pltpu.VMEM((tt, D), F32), pltpu.VMEM((SUBLANES, D), F32)],
        compiler_params=_params())(z, z, z, cw.table, vec.table, wa, wx)


def _rg_backward(name, dp, z, h, cw, vec, wa, wx):
    t = z.shape[0]
    tt = _tile(t, RG_ROWS)
    nt = t // tt
    rev = lambda i: nt - 1 - i
    sp = _rg_specs(tt, nt, rev, cw, vec)
    rows_rev = pl.BlockSpec((tt, D), lambda i: (rev(i), 0))

    def body(dp_ref, zx_ref, zy_ref, halo_ref, h_ref, hhalo_ref, cw_ref, vec_ref, wa_ref, wx_ref,
             dz_ref, dvec_ref, dcw_ref, dwa_ref, dwx_ref, a_s, d_s, carry, nxt):
        i = pl.program_id(0)
        j = rev(i)

        @pl.when(i == 0)
        def _():
            carry[...] = jnp.zeros_like(carry)
            nxt[...] = jnp.zeros_like(nxt)
            dvec_ref[...] = jnp.zeros_like(dvec_ref)
            dcw_ref[...] = jnp.zeros_like(dcw_ref)
            dwa_ref[...] = jnp.zeros_like(dwa_ref)
            dwx_ref[...] = jnp.zeros_like(dwx_ref)

        halo = jnp.where(j > 0, halo_ref[...], 0.0)
        g = _rg_gates(zx_ref[...], halo, cw_ref, vec_ref, wa_ref, wx_ref, j == 0)
        xc, ra, ia, a, mult = g["xc"], g["ra"], g["ia"], g["a"], g["mult"]
        hv, zy, dpv = h_ref[...], zy_ref[...], dp_ref[...]
        gelu, gelu_grad = _gelu_and_grad(zy)
        dyb = dpv * hv * gelu_grad
        a_s[...] = a
        d_s[...] = dpv * gelu

        def group(gi, c):
            rows = pl.ds(pl.multiple_of((tt // SUBLANES - 1 - gi) * SUBLANES, SUBLANES), SUBLANES)
            a8, d8 = a_s[rows, :], d_s[rows, :]
            out = [None] * SUBLANES
            for r in reversed(range(SUBLANES)):
                dht = d8[r:r + 1, :] + c
                out[r] = dht
                c = a8[r:r + 1, :] * dht
            d_s[rows, :] = jnp.concatenate(out, axis=0)
            return c

        carry[0:1, :] = lax.fori_loop(0, tt // SUBLANES, group, carry[0:1, :])
        dht = d_s[...]
        hprev = _shift_down(hv, jnp.where(j > 0, hhalo_ref[...], 0.0), 1)
        ixc = ia * xc
        dlog_a = dht * hprev * a + jnp.where(g["is_t0"], 0.0, dht * ixc * (-(a * a) / mult))
        dia = dht * mult * xc
        dxc = dht * mult * ia
        dra = dlog_a * ((-LRU_C) * g["sp"])
        nl = -vec_ref[3:4, :]
        dvec_ref[3:4, :] += _row_sum(dlog_a * ((-LRU_C) * ra)) * (-_sigmoid(nl))
        dpa = dra * ra * (1.0 - ra)
        dpx = dia * ia * (1.0 - ia)
        dvec_ref[1:2, :] += _row_sum(dpa)
        dvec_ref[2:3, :] += _row_sum(dpx)
        xc_m, dpa_m, dpx_m = xc.astype(MXU), dpa.astype(MXU), dpx.astype(MXU)
        back = [_dot_nt(dpa_m[:, s], wa_ref[hd]) + _dot_nt(dpx_m[:, s], wx_ref[hd]) for hd, s in enumerate(g["heads"])]
        dwa = [_dot_tn(xc_m[:, s], dpa_m[:, s]) for s in g["heads"]]
        dwx = [_dot_tn(xc_m[:, s], dpx_m[:, s]) for s in g["heads"]]
        for hd in range(HEADS):
            dwa_ref[hd] += dwa[hd]
            dwx_ref[hd] += dwx[hd]
        dxc = dxc + jnp.concatenate(back, axis=1)
        dvec_ref[0:1, :] += _row_sum(dxc)
        dxb = dxc * cw_ref[CONV_WIDTH - 1:CONV_WIDTH, :]
        for k in range(CONV_WIDTH):
            row = CONV_WIDTH - 1 - k
            dcw_ref[row:row + 1, :] += _row_sum(dxc * g["shifted"][k])
            if k:
                dxb = dxb + _shift_up(dxc, nxt[...], k) * cw_ref[row:row + 1, :]
        nxt[...] = dxc[0:SUBLANES, :]
        dz_ref[:, 0:D] = dxb.astype(dz_ref.dtype)
        dz_ref[:, D:2 * D] = dyb.astype(dz_ref.dtype)

    hhalo = pl.BlockSpec((SUBLANES, D), lambda i: (jnp.maximum(rev(i) * (tt // SUBLANES) - 1, 0), 0))
    wacc = pl.BlockSpec((HEADS, HEAD_DIM, HEAD_DIM), lambda i: (0, 0, 0))
    return pl.pallas_call(
        body, name=name, grid=(nt,),
        in_specs=[rows_rev, sp["x"], sp["y"], sp["halo"], rows_rev, hhalo, sp["cw"], sp["vec"], sp["w"], sp["w"]],
        out_specs=[pl.BlockSpec((tt, 2 * D), lambda i: (rev(i), 0)), _vec_spec(), _vec_spec(), wacc, wacc],
        out_shape=[jax.ShapeDtypeStruct((t, 2 * D), MXU), jax.ShapeDtypeStruct((SUBLANES, D), F32),
                   jax.ShapeDtypeStruct((SUBLANES, D), F32), jax.ShapeDtypeStruct((HEADS, HEAD_DIM, HEAD_DIM), F32),
                   jax.ShapeDtypeStruct((HEADS, HEAD_DIM, HEAD_DIM), F32)],
        scratch_shapes=[pltpu.VMEM((tt, D), F32), pltpu.VMEM((tt, D), F32), pltpu.VMEM((SUBLANES, D), F32),
                        pltpu.VMEM((SUBLANES, D), F32)],
        compiler_params=_params())(dp, z, z, z, h, h, cw.table, vec.table, wa, wx)


def _hg_chunk(zq, zf, zi, lb):
    c = HG_CHUNK
    zq, zf, zi = zq.astype(F32), zf.astype(F32), zi.astype(F32)
    q, q_grad = _silu_and_grad(zq)
    sg = _sigmoid(zf)
    fg = lb + (1.0 - lb) * sg
    k = 1.0 - fg
    row, col = lax.broadcasted_iota(jnp.int32, (c, c), 0), lax.broadcasted_iota(jnp.int32, (c, c), 1)
    tri, tri_t = (row >= col).astype(F32), (row <= col).astype(F32)
    b = _dot_exact(tri, jnp.log(fg))
    mid, last = b[c // 2 - 1:c // 2, :], b[c - 1:c, :]
    eq = jnp.exp(jnp.minimum(b - mid, EXP_CLAMP))
    ek = jnp.exp(jnp.minimum(mid - b, EXP_CLAMP))
    eb = jnp.exp(b)
    ed = jnp.exp(last - b)
    return dict(q=q, q_grad=q_grad, sg=sg, fg=fg, k=k, v=zi, eq=eq, ek=ek, eb=eb, ed=ed, elast=jnp.exp(last), tri=tri, tri_t=tri_t,
                qe=q * eq, ke=k * ek, qb=q * eb, kd=k * ed)


def _hg_specs(tt, order):
    return [pl.BlockSpec((tt, D), lambda i, col=col: (order(i), col)) for col in range(4)]


def _hg_forward(name, z, vec):
    t = z.shape[0]
    tt = _tile(t, ROW_TILE)
    nt, nc = t // tt, tt // HG_CHUNK
    heads = [slice(h * HEAD_DIM, (h + 1) * HEAD_DIM) for h in range(HEADS)]

    def body(zq_ref, zf_ref, zi_ref, zg_ref, vec_ref, p_ref, o_ref, ss_ref, state):
        @pl.when(pl.program_id(0) == 0)
        def _():
            state[...] = jnp.zeros_like(state)

        lb, gain = vec_ref[0:1, :], vec_ref[1:2, :]

        def chunk(ci, carry):
            rows = pl.ds(pl.multiple_of(ci * HG_CHUNK, HG_CHUNK), HG_CHUNK)
            g = _hg_chunk(zq_ref[rows, :], zf_ref[rows, :], zi_ref[rows, :], lb)
            zg = zg_ref[rows, :].astype(F32)
            causal = g["tri"] > 0.0
            qe_m, ke_m, qb_m, kd_m, v_m = (g[n].astype(MXU) for n in ("qe", "ke", "qb", "kd", "v"))
            atts = [_dot_nt(qe_m[:, s], ke_m[:, s]) for s in heads]
            o_parts, p_parts = [], []
            for hd, s in enumerate(heads):
                st = state[hd]
                ss_ref[ci, hd] = st
                att = jnp.where(causal, atts[hd], 0.0)
                o_parts.append(_dot(att, v_m[:, s]) + _dot_nt(qb_m[:, s], st))
                state[hd] = st * g["elast"][:, s] + _dot_tn(v_m[:, s], kd_m[:, s])
            for o, s in zip(o_parts, heads):
                r = lax.rsqrt(jnp.mean(o * o, axis=-1, keepdims=True) + GNORM_EPS)
                p_parts.append((o * r) * gain[:, s])
            o_ref[rows, :] = jnp.concatenate(o_parts, axis=1)
            p_ref[rows, :] = (jnp.concatenate(p_parts, axis=1) * _silu(zg)).astype(p_ref.dtype)
            return carry

        lax.fori_loop(0, nc, chunk, 0, unroll=True)

    return pl.pallas_call(
        body, name=name, grid=(nt,), in_specs=_hg_specs(tt, lambda i: i) + [_group_spec(vec)],
        out_specs=[_rows_spec(tt), _rows_spec(tt), pl.BlockSpec((nc, HEADS, HEAD_DIM, HEAD_DIM), lambda i: (i, 0, 0, 0))],
        out_shape=[jax.ShapeDtypeStruct((t, D), MXU), jax.ShapeDtypeStruct((t, D), F32),
                   jax.ShapeDtypeStruct((t // HG_CHUNK, HEADS, HEAD_DIM, HEAD_DIM), F32)],
        scratch_shapes=[pltpu.VMEM((HEADS, HEAD_DIM, HEAD_DIM), F32)], compiler_params=_params())(z, z, z, z, vec.table)


def _hg_backward(name, dp, z, o, ss, vec):
    t = z.shape[0]
    tt = _tile(t, ROW_TILE)
    nt, nc = t // tt, tt // HG_CHUNK
    rev = lambda i: nt - 1 - i
    heads = [slice(h * HEAD_DIM, (h + 1) * HEAD_DIM) for h in range(HEADS)]
    rows_rev = pl.BlockSpec((tt, D), lambda i: (rev(i), 0))

    def body(dp_ref, zq_ref, zf_ref, zi_ref, zg_ref, o_ref, ss_ref, vec_ref, dz_ref, acc_ref, dstate):
        @pl.when(pl.program_id(0) == 0)
        def _():
            dstate[...] = jnp.zeros_like(dstate)
            acc_ref[...] = jnp.zeros_like(acc_ref)

        lb, gain = vec_ref[0:1, :], vec_ref[1:2, :]

        def chunk(cr, carry):
            ci = nc - 1 - cr
            rows = pl.ds(pl.multiple_of(ci * HG_CHUNK, HG_CHUNK), HG_CHUNK)
            zq, zg = zq_ref[rows, :], zg_ref[rows, :].astype(F32)
            g = _hg_chunk(zq, zf_ref[rows, :], zi_ref[rows, :], lb)
            ov, dpv = o_ref[rows, :], dp_ref[rows, :]
            causal = g["tri"] > 0.0
            silu_g, silu_g_grad = _silu_and_grad(zg)
            don = dpv * silu_g
            dgate = dpv * silu_g_grad
            qe_m, ke_m, qb_m, kd_m, v_m = (g[n].astype(MXU) for n in ("qe", "ke", "qb", "kd", "v"))
            parts = {n: [] for n in ("dzg", "dgain", "dv", "dqe", "dke", "dqb", "dkd", "dlast")}
            dos, atts, datts = [], [], []
            for hd, s in enumerate(heads):
                oh = ov[:, s]
                r = lax.rsqrt(jnp.mean(oh * oh, axis=-1, keepdims=True) + GNORM_EPS)
                on = oh * r
                parts["dzg"].append(dgate[:, s] * (on * gain[:, s]))
                parts["dgain"].append(_row_sum(don[:, s] * on))
                dtmp = don[:, s] * gain[:, s]
                dos.append((r * (dtmp - on * jnp.mean(dtmp * on, axis=-1, keepdims=True))).astype(MXU))
            for hd, s in enumerate(heads):
                atts.append(_dot_nt(qe_m[:, s], ke_m[:, s]))
                datts.append(_dot_nt(dos[hd], v_m[:, s]))
            for hd, s in enumerate(heads):
                do, att, datt = dos[hd], jnp.where(causal, atts[hd], 0.0).astype(MXU), jnp.where(causal, datts[hd], 0.0).astype(MXU)
                st, dst = ss_ref[ci, hd], dstate[hd]
                st_m, dst_m = st.astype(MXU), dst.astype(MXU)
                qe, ke, qb, kd, v = qe_m[:, s], ke_m[:, s], qb_m[:, s], kd_m[:, s], v_m[:, s]
                parts["dv"].append(_dot_tn(att, do) + _dot_nt(kd, dst_m))
                parts["dqe"].append(_dot(datt, ke))
                parts["dke"].append(_dot_tn(datt, qe))
                parts["dqb"].append(_dot(do, st_m))
                parts["dkd"].append(_dot(v, dst_m))
                parts["dlast"].append(g["elast"][:, s] * _row_sum(dst * st))
                dstate[hd] = dst * g["elast"][:, s] + _dot_tn(do, qb)
            whole = {n: jnp.concatenate(p, axis=1) for n, p in parts.items()}
            acc_ref[1:2, :] += whole["dgain"]
            dqe, dke, dqb, dkd = whole["dqe"], whole["dke"], whole["dqb"], whole["dkd"]
            dq = dqe * g["eq"] + dqb * g["eb"]
            dk = dke * g["ek"] + dkd * g["ed"]
            dkdk = dkd * g["kd"]
            db = dqe * qe_m.astype(F32) - dke * ke_m.astype(F32) + dqb * g["qb"] - dkdk
            dlogf = _dot_exact(g["tri_t"], db) + (whole["dlast"] + _row_sum(dkdk))
            dfg = dlogf / g["fg"] - dk
            sg = g["sg"]
            acc_ref[0:1, :] += _row_sum(dfg * (1.0 - sg))
            dz_ref[rows, 0:D] = (dq * g["q_grad"]).astype(dz_ref.dtype)
            dz_ref[rows, D:2 * D] = (dfg * (1.0 - lb) * sg * (1.0 - sg)).astype(dz_ref.dtype)
            dz_ref[rows, 2 * D:3 * D] = whole["dv"].astype(dz_ref.dtype)
            dz_ref[rows, 3 * D:4 * D] = whole["dzg"].astype(dz_ref.dtype)
            return carry

        lax.fori_loop(0, nc, chunk, 0, unroll=True)

    return pl.pallas_call(
        body, name=name, grid=(nt,),
        in_specs=[rows_rev] + _hg_specs(tt, rev) + [rows_rev, pl.BlockSpec((nc, HEADS, HEAD_DIM, HEAD_DIM), lambda i: (rev(i), 0, 0, 0)),
                                                  _group_spec(vec)],
        out_specs=[pl.BlockSpec((tt, 4 * D), lambda i: (rev(i), 0)), _vec_spec()],
        out_shape=[jax.ShapeDtypeStruct((t, 4 * D), MXU), jax.ShapeDtypeStruct((SUBLANES, D), F32)],
        scratch_shapes=[pltpu.VMEM((HEADS, HEAD_DIM, HEAD_DIM), F32)], compiler_params=_params())(dp, z, z, z, z, o, ss, vec.table)


def _mod_forward(name, c_all, mod_w, lower):
    depth, _, sw = mod_w.shape

    def body(c_ref, w_ref, lo_ref, cs_ref, mod_ref, lb_ref):
        cs = _silu(c_ref[...])
        mod_ref[...] = _dot(cs, w_ref[...])

        @pl.when(pl.program_id(0) == 0)
        def _():
            cs_ref[...] = cs
            lo = lo_ref[...]
            e = jnp.exp(lo - jnp.max(lo, axis=0, keepdims=True))
            sm = e / jnp.sum(e, axis=0, keepdims=True)
            lb_ref[0:1, :] = jnp.zeros((1, D), F32)
            for l in range(1, depth):
                lb_ref[l:l + 1, :] = lb_ref[l - 1:l, :] + sm[l:l + 1, :]

    return pl.pallas_call(
        body, name=name, grid=(depth,),
        in_specs=[pl.BlockSpec((N_DEV, D), lambda l: (0, 0)), pl.BlockSpec((None, D, sw), lambda l: (l, 0, 0)),
                  pl.BlockSpec((depth, D), lambda l: (0, 0))],
        out_specs=[pl.BlockSpec((N_DEV, D), lambda l: (0, 0)), pl.BlockSpec((None, N_DEV, sw), lambda l: (l, 0, 0)),
                   pl.BlockSpec((depth, D), lambda l: (0, 0))],
        out_shape=[jax.ShapeDtypeStruct((N_DEV, D), F32), jax.ShapeDtypeStruct((depth, N_DEV, sw), F32),
                   jax.ShapeDtypeStruct((depth, D), F32)], compiler_params=_params())(c_all, mod_w, lower)


def _mod_weight_grad(name, cs_t, dmod):
    depth, pad, sw = dmod.shape

    def body(c_ref, d_ref, g_ref):
        g_ref[...] = _dot(c_ref[...], d_ref[...])

    return pl.pallas_call(
        body, name=name, grid=(depth,),
        in_specs=[pl.BlockSpec((D, pad), lambda l: (0, 0)), pl.BlockSpec((None, pad, sw), lambda l: (l, 0, 0))],
        out_specs=pl.BlockSpec((None, D, sw), lambda l: (l, 0, 0)),
        out_shape=jax.ShapeDtypeStruct((depth, D, sw), F32), compiler_params=_params())(cs_t, dmod)


def _lower_bound_grad(name, lower, dlb):
    depth = lower.shape[0]

    def body(lo_ref, d_ref, out_ref):
        lo = lo_ref[...]
        e = jnp.exp(lo - jnp.max(lo, axis=0, keepdims=True))
        sm = e / jnp.sum(e, axis=0, keepdims=True)
        out_ref[...] = jnp.zeros_like(out_ref)
        dsm = [jnp.zeros((1, D), F32)]
        for l in range(1, depth):
            tail = d_ref[l:l + 1, :]
            for m in range(l + 1, depth):
                tail = tail + d_ref[m:m + 1, :]
            dsm.append(tail)
        inner = sm[1:2, :] * dsm[1]
        for l in range(2, depth):
            inner = inner + sm[l:l + 1, :] * dsm[l]
        for l in range(depth):
            out_ref[l:l + 1, :] = sm[l:l + 1, :] * (dsm[l] - inner)
        for j in range(SUBLANES - depth):
            row = d_ref[depth + j:depth + j + 1, :]
            tot = row[:, 0:HEAD_DIM]
            for hd in range(1, HEADS):
                tot = tot + row[:, hd * HEAD_DIM:(hd + 1) * HEAD_DIM]
            out_ref[depth + j:depth + j + 1, 0:HEAD_DIM] = tot

    return pl.pallas_call(body, name=name, out_shape=jax.ShapeDtypeStruct((SUBLANES, D), F32))(lower, dlb)


def _adamw(name, parts, w, m, v, layer=None, prev=None):
    p, r, c = parts.shape
    tr = _tile(r, max(SUBLANES, ADAMW_STEP_BYTES // (4 * c * (p + 7))))
    stacked = layer is not None
    n_prev = 4 if prev is not None else 0

    def body(*refs):
        parts_ref, w_ref, m_ref, v_ref = refs[:4]
        g_ref, d_ref, m_out, v_out = refs[4 + n_prev:]
        g = parts_ref[0].astype(F32)
        for q in range(1, p):
            g = g + parts_ref[q].astype(F32)
        m2 = ADAM_B1 * m_ref[...] + (1.0 - ADAM_B1) * g
        v2 = ADAM_B2 * v_ref[...] + (1.0 - ADAM_B2) * (g * g)
        m_hat = m2 / (1.0 - ADAM_B1 ** ADAM_STEP)
        v_hat = v2 / (1.0 - ADAM_B2 ** ADAM_STEP)
        g_ref[...] = g
        d_ref[...] = -ADAM_LR * (m_hat / (jnp.sqrt(v_hat) + ADAM_EPS) + ADAM_WD * w_ref[...])
        m_out[...] = m2
        v_out[...] = v2

    if stacked:
        spec = pl.BlockSpec((None, tr, c), lambda i: (layer, i, 0))
    else:
        spec = pl.BlockSpec((tr, c), lambda i: (i, 0))
    return pl.pallas_call(
        body, name=name, grid=(r // tr,),
        in_specs=[pl.BlockSpec((p, tr, c), lambda i: (0, i, 0)), spec, spec, spec] + [ANY] * n_prev, out_specs=[spec] * 4,
        out_shape=[jax.ShapeDtypeStruct(w.shape, F32)] * 4,
        input_output_aliases={4 + q: q for q in range(n_prev)}, compiler_params=_params(),
    )(_hbm(parts), _hbm(w), _hbm(m), _hbm(v), *(prev or []))


def _hbm(a):
    return pltpu.with_memory_space_constraint(a, pltpu.HBM)


def _pad_rows(a, rows=SUBLANES):
    a = a.reshape(-1, a.shape[-1])
    return jnp.concatenate([a, jnp.zeros((rows - a.shape[0], a.shape[1]), a.dtype)], axis=0) if a.shape[0] < rows else a


def kernel(x, c, mod_w, mod_b, norm_mix, norm_mlp, norm_final, rg_w_in, rg_conv_w, rg_conv_b, rg_w_a, rg_b_a, rg_w_x, rg_b_x, rg_lambda, rg_w_out, hg_w_in, hg_lower_bounds, hg_gnorm, hg_w_out, mlp_w1, mlp_w2, loss_target, m_mod_w, m_mod_b, m_norm_mix, m_norm_mlp, m_norm_final, m_rg_w_in, m_rg_conv_w, m_rg_conv_b, m_rg_w_a, m_rg_b_a, m_rg_w_x, m_rg_b_x, m_rg_lambda, m_rg_w_out, m_hg_w_in, m_hg_lower_bounds, m_hg_gnorm, m_hg_w_out, m_mlp_w1, m_mlp_w2, v_mod_w, v_mod_b, v_norm_mix, v_norm_mlp, v_norm_final, v_rg_w_in, v_rg_conv_w, v_rg_conv_b, v_rg_w_a, v_rg_b_a, v_rg_w_x, v_rg_b_x, v_rg_lambda, v_rg_w_out, v_hg_w_in, v_hg_lower_bounds, v_hg_gnorm, v_hg_w_out, v_mlp_w1, v_mlp_w2):
    me = 4 * lax.axis_index("x") + 2 * lax.axis_index("y") + lax.axis_index("c")
    x0 = x[0]
    target = loss_target[0]
    n_rg, n_hg = rg_w_in.shape[0], hg_w_in.shape[0]
    sw_mod = mod_w.shape[2]

    c_all, cw_all = _all_gather([_pad_rows(c), rg_conv_w.reshape(n_rg * CONV_WIDTH, -1)], "gather_cond")
    c_all = c_all[:, 0, :]
    conv_w = cw_all.transpose(1, 0, 2).reshape(n_rg, CONV_WIDTH, D)
    cs_all, mod_part, lb_all = _mod_forward("mod_forward", c_all, mod_w, hg_lower_bounds)
    (mod_gathered,) = _all_gather([mod_part.reshape(DEPTH * N_DEV, sw_mod)], "gather_mod")

    ids = iter(range(5 * DEPTH))
    shards = []
    for layer in range(DEPTH):
        j = layer // 2
        w_in, w_out = (rg_w_in[j], rg_w_out[j]) if layer % 2 == 0 else (hg_w_in[j], hg_w_out[j])
        shards.append([w_in.astype(MXU), w_out.astype(MXU), mlp_w1[layer].astype(MXU), mlp_w2[layer].astype(MXU)])
    shards, mod_gathered = lax.optimization_barrier((shards, mod_gathered))
    weights = []
    for layer in range(DEPTH):
        g_in, g_out = _all_gather_async(shards[layer][:2], f"gather_mixer_weights_{layer}", next(ids), ["cols", "rows"])
        g_w1, g_w2 = _all_gather_async(shards[layer][2:], f"gather_mlp_weights_{layer}", next(ids), ["cols", "rows"])
        weights.append(dict(w_in=_hbm(g_in), w_out=_hbm(g_out), w1=_hbm(g_w1), w2=_hbm(g_w2)))

    mod_mine = lax.dynamic_index_in_dim(mod_gathered.reshape(N_DEV, DEPTH, N_DEV, sw_mod), me, axis=2, keepdims=False)
    mod = mod_mine.transpose(1, 0, 2).reshape(DEPTH, 6, D) + mod_b.reshape(DEPTH, 6, D)

    pieces, n_groups = [], 0
    placed = [dict() for _ in range(DEPTH)]

    def place(rows):
        nonlocal n_groups
        rows = [r.reshape(-1, D).astype(F32) for r in rows]
        pieces.extend(rows + [jnp.zeros((SUBLANES - sum(r.shape[0] for r in rows), D), F32)])
        n_groups += 1
        return n_groups - 1

    for layer in range(DEPTH):
        j, at = layer // 2, placed[layer]
        at["vec_mix"] = place([norm_mix[layer], mod[layer, 1], mod[layer, 0]])
        at["vec_mlp"] = place([norm_mlp[layer], mod[layer, 4], mod[layer, 3]])
        at["gate_mix"], at["gate_mlp"] = place([mod[layer, 2]]), place([mod[layer, 5]])
        if layer % 2 == 0:
            at["cw"] = place([conv_w[j]])
            at["vec"] = place([rg_conv_b[j], rg_b_a[j], rg_b_x[j], rg_lambda[j]])
        else:
            at["vec"] = place([lb_all[layer], jnp.tile(hg_gnorm[j], HEADS)])
    final_group = place([norm_final])
    table = _hbm(jnp.concatenate(pieces, axis=0))

    saved = []
    xl = x0
    for layer in range(DEPTH):
        j, wt = layer // 2, weights[layer]
        is_rg = layer % 2 == 0
        s = dict(x=xl, **{k: Group(table, g) for k, g in placed[layer].items()})
        s["h"] = h_next if layer else _hbm(_norm_mod("norm_mod", xl, s["vec_mix"]))
        if is_rg:
            (s["z"],) = _mm_tokens("rg_in", s["h"], wt["w_in"], False, (F32,))
            s["wa"], s["wx"] = _hbm(rg_w_a[j].astype(MXU)), _hbm(rg_w_x[j].astype(MXU))
            s["p"], s["hr"] = _rg_forward("rg_forward", s["z"], s["cw"], s["vec"], s["wa"], s["wx"])
        else:
            (s["z"],) = _mm_tokens("hg_in", s["h"], wt["w_in"], False, (MXU,))
            s["p"], s["o"], s["ss"] = _hg_forward("hg_forward", s["z"], s["vec"])
        s["y"], s["x1"], h2 = _mm_tokens("mix_out", s["p"], wt["w_out"], False, (MXU, F32, MXU), _ep_residual_norm,
                                         extras=(xl, s["gate_mix"], s["vec_mlp"]))
        s["p"], s["h2"] = _hbm(s["p"]), _hbm(h2)
        next_norm = Group(table, placed[layer + 1]["vec_mix"]) if layer + 1 < DEPTH else None
        s["r"], s["ff"], xl, *chained = _mlp_forward("mlp_forward", s["h2"], wt["w1"], wt["w2"], s["x1"], s["gate_mlp"], next_norm)
        h_next = _hbm(chained[0]) if chained else None
        saved.append(s)

    dx, head, dff = _loss_head("loss_head", xl, target, Group(table, final_group), saved[-1]["ff"], saved[-1]["gate_mlp"])
    g_mlp = head[GATE_ROW:GATE_ROW + 1]

    results = {}
    big = dict(rg_w_in=(rg_w_in, m_rg_w_in, v_rg_w_in), rg_w_out=(rg_w_out, m_rg_w_out, v_rg_w_out),
               hg_w_in=(hg_w_in, m_hg_w_in, v_hg_w_in), hg_w_out=(hg_w_out, m_hg_w_out, v_hg_w_out),
               mlp_w1=(mlp_w1, m_mlp_w1, v_mlp_w1), mlp_w2=(mlp_w2, m_mlp_w2, v_mlp_w2))

    def update_layer(layer, landed):
        mixer = "rg" if layer % 2 == 0 else "hg"
        l_in, l_out, l_w1, l_w2 = landed
        for nm, parts, idx in ((f"{mixer}_w_in", l_in, layer // 2), (f"{mixer}_w_out", l_out, layer // 2),
                               ("mlp_w1", l_w1, layer), ("mlp_w2", l_w2, layer)):
            w, m, v = big[nm]
            results[nm] = _adamw(f"adamw_{nm}", parts, w, m, v, layer=idx, prev=results.get(nm))

    landed = [None] * DEPTH
    small = [None] * DEPTH
    parity = lax.axis_index("c").astype(jnp.int32).reshape(1)

    def send_chip_sums(layer, grads, from_sibling, anchor):
        sums = _pair_sum("pair_sum", parity, grads, from_sibling)
        sums, anchor = lax.optimization_barrier((sums, anchor))
        sums = [_hbm(a) for a in sums]
        landed[layer] = [_hbm(a) for a in _chip_exchange_async(sums, f"exchange_grads_{layer}", next(ids))]
        return anchor

    pending = None
    for layer in reversed(range(DEPTH)):
        j, wt, s = layer // 2, weights[layer], saved[layer]
        is_rg = layer % 2 == 0
        dff = _hbm(dff)
        da, dx1, n_mlp, dyb = _mlp_backward("mlp_backward", dff, s["r"], wt["w1"], wt["w2"], s["x1"], s["vec_mlp"], dx,
                                            s["y"], s["gate_mix"])
        if pending is not None:
            da = send_chip_sums(*pending, da)
        dw2 = _mm_grad("mlp_out_grad", s["r"], dff, False, prologue=_square).reshape(N_DEV, -1, D)
        dw1 = _mm_grad("mlp_in_grad", s["h2"], da, True)
        dyb = _hbm(dyb)
        (dp,) = _mm_tokens("mix_out_t", dyb, wt["w_out"], True, (F32,))
        dw_out = _hbm(_mm_grad("mix_out_grad", s["p"], dyb, False).reshape(N_DEV, -1, D))
        if is_rg:
            dz, dvec, dcw, dwa, dwx = _rg_backward("rg_backward", dp, s["z"], s["hr"], s["cw"], s["vec"], s["wa"], s["wx"])
            gate_grads, dz = lax.optimization_barrier(
                ([dwa.reshape(-1, HEAD_DIM).astype(MXU), dwx.reshape(-1, HEAD_DIM).astype(MXU)], dz))
            gate_parts = _all_gather_async(gate_grads, f"gather_gate_grads_{j}", next(ids), ["stack", "stack"])
            mixer_small = dict(dvec=dvec, dcw=dcw, gate_parts=gate_parts)
            dw_in = _mm_grad("rg_in_grad", s["h"], dz, True)
        else:
            dz, dvec = _hg_backward("hg_backward", dp, s["z"], s["o"], s["ss"], s["vec"])
            mixer_small = dict(dvec=dvec)
            dw_in = _mm_grad("hg_in_grad", s["h"], dz, True)
        grads = [dw_in, dw_out, dw1, dw2]
        pending = (layer, grads, _sibling_send_async(grads, f"pair_grads_{layer}", next(ids)))
        small[layer] = dict(g_mlp=g_mlp, n_mlp=n_mlp, g_mix=n_mlp[GATE_ROW:GATE_ROW + 1], **mixer_small)
        if layer:
            below = saved[layer - 1]
            dx, n_mix, dff = _mm_norm_grad("mixer_in_t", dz, wt["w_in"], s["x"], s["vec_mix"], dx1, below["ff"], below["gate_mlp"])
            g_mlp = n_mix[GATE_ROW:GATE_ROW + 1]
        else:
            dx, n_mix = _mm_norm_grad("mixer_in_t_first", dz, wt["w_in"], s["x"], s["vec_mix"], dx1)
        small[layer]["n_mix"] = n_mix
        if layer + 1 < DEPTH:
            stream = (dx, dff) if layer else (dx,)
            landed[layer + 1], stream = lax.optimization_barrier((landed[layer + 1], stream))
            if "gate_parts" in small[layer + 1]:
                small[layer + 1]["gate_parts"], stream = lax.optimization_barrier((small[layer + 1]["gate_parts"], stream))
            dx, dff = stream if layer else (stream[0], None)
            update_layer(layer + 1, landed[layer + 1])

    dlb_rows = [jnp.zeros((1, D), F32) if l % 2 == 0 else small[l]["dvec"][0:1] for l in range(DEPTH)]
    dgn_rows = [small[2 * j + 1]["dvec"][1:2] for j in range(n_hg)]
    lb_grad = _lower_bound_grad("lower_bound_grad", hg_lower_bounds, _pad_rows(jnp.concatenate(dlb_rows + dgn_rows, axis=0)))
    dmod = jnp.stack([jnp.concatenate([small[l]["n_mix"][2], small[l]["n_mix"][1], small[l]["g_mix"][0],
                                       small[l]["n_mlp"][2], small[l]["n_mlp"][1], small[l]["g_mlp"][0]]) for l in range(DEPTH)])
    groups = [
        dmod.reshape(DEPTH * 6, D),
        jnp.stack([small[l]["n_mix"][0] for l in range(DEPTH)]),
        jnp.stack([small[l]["n_mlp"][0] for l in range(DEPTH)]),
        head[0:1],
        jnp.stack([small[2 * j]["dvec"][0] for j in range(n_rg)]),
        jnp.stack([small[2 * j]["dvec"][1] for j in range(n_rg)]),
        jnp.stack([small[2 * j]["dvec"][2] for j in range(n_rg)]),
        jnp.stack([small[2 * j]["dvec"][3] for j in range(n_rg)]),
        lb_grad[0:DEPTH],
        jnp.concatenate([lb_grad[DEPTH + j:DEPTH + j + 1, 0:HEAD_DIM] for j in range(n_hg)]
                        + [jnp.zeros((1, D - n_hg * HEAD_DIM), F32)], axis=1),
        jnp.concatenate([small[2 * j]["dcw"][0:CONV_WIDTH] for j in range(n_rg)], axis=0),
        head[1:2],
    ]
    params = [mod_b, norm_mix, norm_mlp, norm_final, rg_conv_b, rg_b_a, rg_b_x, rg_lambda, hg_lower_bounds, hg_gnorm]
    moms = [m_mod_b, m_norm_mix, m_norm_mlp, m_norm_final, m_rg_conv_b, m_rg_b_a, m_rg_b_x, m_rg_lambda, m_hg_lower_bounds, m_hg_gnorm]
    vars_ = [v_mod_b, v_norm_mix, v_norm_mlp, v_norm_final, v_rg_conv_b, v_rg_b_a, v_rg_b_x, v_rg_lambda, v_hg_lower_bounds, v_hg_gnorm]
    offsets, rows_of, at = [], [], 0
    for g in groups:
        offsets.append(at)
        rows_of.append(g.shape[0])
        at += -(-g.shape[0] // SUBLANES) * SUBLANES
    packed = jnp.concatenate([_pad_rows(g, -(-g.shape[0] // SUBLANES) * SUBLANES) for g in groups], axis=0)
    (small_parts,) = _all_gather_async([packed], "gather_small_grads", next(ids), ["stack"])
    send_chip_sums(*pending, dx)

    def pack_like(arrs):
        out = []
        for g_rows, off, a in zip(rows_of, offsets, arrs):
            flat = a.reshape(-1)
            flat = jnp.concatenate([flat, jnp.zeros((g_rows * D - flat.shape[0],), F32)])
            out.append(_pad_rows(flat.reshape(g_rows, D), -(-g_rows // SUBLANES) * SUBLANES))
        rest = packed.shape[0] - sum(o.shape[0] for o in out)
        return jnp.concatenate(out + [jnp.zeros((rest, D), F32)], axis=0)

    small_out = _adamw("adamw_small", small_parts, pack_like(params), pack_like(moms), pack_like(vars_))

    def unpack(q, idx, like):
        rows = small_out[q][offsets[idx]:offsets[idx] + rows_of[idx]]
        return rows.reshape(-1)[:like.size].reshape(like.shape)

    loss = jnp.sum(small_out[0][offsets[11]])
    names =["mod_b", "norm_mix", "norm_mlp", "norm_final", "rg_conv_b", "rg_b_a", "rg_b_x", "rg_lambda", "hg_lower_bounds", "hg_gnorm"]
    for idx, (nm, like) in enumerate(zip(names, params)):
        results[nm] = [unpack(q, idx, like) for q in range(4)]

    cw_parts = lax.dynamic_slice_in_dim(small_parts[:, offsets[10]:offsets[10] + n_rg * CONV_WIDTH, :], me * (D // N_DEV), D // N_DEV, axis=2)
    shp = rg_conv_w.shape
    results["rg_conv_w"] = [o.reshape(shp) for o in _adamw(
        "adamw_conv", cw_parts, rg_conv_w.reshape(-1, shp[-1]), m_rg_conv_w.reshape(-1, shp[-1]), v_rg_conv_w.reshape(-1, shp[-1]))]
    shp = rg_w_a.shape
    stacked = (n_rg, HEADS * HEAD_DIM, HEAD_DIM)
    for nm, which, (w, m, v) in (("rg_w_a", 0, (rg_w_a, m_rg_w_a, v_rg_w_a)), ("rg_w_x", 1, (rg_w_x, m_rg_w_x, v_rg_w_x))):
        out = None
        for j in reversed(range(n_rg)):
            out = _adamw("adamw_gate", small[2 * j]["gate_parts"][which], w.reshape(stacked), m.reshape(stacked),
                         v.reshape(stacked), layer=j, prev=out)
        results[nm] = [o.reshape(shp) for o in out]

    dmod_all = small_parts[:, 0:DEPTH * 6, :].reshape(N_DEV, DEPTH, 6 * D)
    dmod_cols = lax.dynamic_slice_in_dim(dmod_all, me * sw_mod, sw_mod, axis=2).transpose(1, 0, 2)
    pad = HEAD_DIM - N_DEV
    dmod_pad = jnp.concatenate([dmod_cols, jnp.zeros((DEPTH, pad, sw_mod), F32)], axis=1).astype(MXU)
    cs_t = jnp.concatenate([cs_all.T, jnp.zeros((D, pad), F32)], axis=1).astype(MXU)
    g_mod_w = _mod_weight_grad("mod_weight_grad", cs_t, dmod_pad)
    results["mod_w"] = [o.reshape(mod_w.shape) for o in _adamw(
        "adamw_mod", g_mod_w.reshape(1, -1, sw_mod), mod_w.reshape(-1, sw_mod), m_mod_w.reshape(-1, sw_mod), v_mod_w.reshape(-1, sw_mod))]

    update_layer(0, landed[0])

    order = ["mod_w", "mod_b", "norm_mix", "norm_mlp", "norm_final", "rg_w_in", "rg_conv_w", "rg_conv_b", "rg_w_a", "rg_b_a", "rg_w_x",
             "rg_b_x", "rg_lambda", "rg_w_out", "hg_w_in", "hg_lower_bounds", "hg_gnorm", "hg_w_out", "mlp_w1", "mlp_w2"]
    return (loss, dx[None], *[results[n][0] for n in order], *[results[n][1] for n in order],
            *[results[n][2] for n in order], *[results[n][3] for n in order])
```

```python
import collections

import jax
import jax.numpy as jnp
from jax import lax
from jax.experimental import pallas as pl
from jax.experimental.pallas import tpu as pltpu
from jax.experimental.pallas import tpu_sc as plsc

F32 = jnp.float32
MXU = jnp.bfloat16

N_DEV = 8
D = 1024
DEPTH = 4
HEADS = 8
HEAD_DIM = 128
CONV_WIDTH = 4
LRU_C = 8.0
HG_CHUNK = 64
NORM_EPS = 1e-6
GNORM_EPS = 1e-5
ADAM_LR = 0.001
ADAM_B1 = 0.9
ADAM_B2 = 0.999
ADAM_EPS = 1e-08
ADAM_WD = 0.01
ADAM_STEP = 10
GELU_C = 0.7978845608028654
GELU_K = 0.044715
EXP_CLAMP = 80.0
SUBLANES = 8
VMEM_LIMIT = 48 * 1024 * 1024
ROW_TILE = 256
RG_ROWS = 256
MM_TILE = 1024
MM_TOKENS = 512
MLP_TOKENS = 256
ADAMW_STEP_BYTES = 8 * 1024 * 1024

MESH = pl.DeviceIdType.MESH
ANY = pl.BlockSpec(memory_space=pl.ANY)


def _params():
    return pltpu.CompilerParams(vmem_limit_bytes=VMEM_LIMIT)


def _tile(n, target):
    if n <= target:
        return n
    t = target // SUBLANES * SUBLANES
    while n % t:
        t -= SUBLANES
    return t


def _sigmoid(x):
    return 1.0 / (1.0 + jnp.exp(-x))


def _silu(x):
    return x * _sigmoid(x)


def _silu_and_grad(x):
    s = _sigmoid(x)
    return x * s, s * (1.0 + x * (1.0 - s))


def _gelu(y):
    return 0.5 * y * (1.0 + jnp.tanh(GELU_C * (y + GELU_K * y * y * y)))


def _gelu_and_grad(y):
    y2 = y * y
    th = jnp.tanh(GELU_C * (y + GELU_K * y2 * y))
    half = 0.5 * (1.0 + th)
    return y * half, half + 0.5 * y * (1.0 - th * th) * GELU_C * (1.0 + 3.0 * GELU_K * y2)


def _dot(a, b):
    return lax.dot_general(a.astype(MXU), b.astype(MXU), (((1,), (0,)), ((), ())), preferred_element_type=F32)


def _dot_nt(a, b):
    return lax.dot_general(a.astype(MXU), b.astype(MXU), (((1,), (1,)), ((), ())), preferred_element_type=F32)


def _dot_tn(a, b):
    return lax.dot_general(a.astype(MXU), b.astype(MXU), (((0,), (0,)), ((), ())), preferred_element_type=F32)


def _dot_exact(tri, x):
    t = tri.astype(MXU)
    hi = x.astype(MXU)
    r1 = x - hi.astype(F32)
    mid = r1.astype(MXU)
    lo = (r1 - mid.astype(F32)).astype(MXU)
    dn = (((1,), (0,)), ((), ()))
    return (lax.dot_general(t, hi, dn, preferred_element_type=F32) + lax.dot_general(t, mid, dn, preferred_element_type=F32)
            + lax.dot_general(t, lo, dn, preferred_element_type=F32))


def _row_sum(v):
    return jnp.sum(v, axis=0, keepdims=True)


def _handshake(partners):
    barrier = pltpu.get_barrier_semaphore()
    for p in partners:
        pl.semaphore_signal(barrier, inc=1, device_id=p, device_id_type=MESH)
    pl.semaphore_wait(barrier, len(partners))


def _gather_body(n, per_array_sems, handshake, layouts):
    def body(*refs):
        ins, outs = refs[:n], refs[n:2 * n]
        send_sems, recv_sems, local_sems = refs[2 * n:]
        x, y, c = lax.axis_index("x"), lax.axis_index("y"), lax.axis_index("c")
        me, sibling = (x, y, c), (x, y, 1 - c)
        chips = [(1 - x, y), (x, 1 - y), (1 - x, 1 - y)]
        if handshake:
            _handshake([sibling] + [(*chip, c) for chip in chips])

        def sem(sems, a, k):
            return sems.at[a, k] if per_array_sems else sems.at[k]

        def slot(a, p):
            block = 4 * p[0] + 2 * p[1] + p[2]
            r, c_ = ins[a].shape
            if layouts[a] == "rows":
                return outs[a].at[pl.ds(block * r, r), :]
            if layouts[a] == "cols":
                return outs[a].at[:, pl.ds(block * c_, c_)]
            return outs[a].at[block]

        def copy(a, k, block, to, src=None):
            return pltpu.make_async_remote_copy(
                src_ref=slot(a, block) if src is None else src, dst_ref=slot(a, block),
                send_sem=sem(send_sems, a, k), recv_sem=sem(recv_sems, a, k), device_id=to, device_id_type=MESH)

        mine = [pltpu.make_async_copy(ins[a], slot(a, me), local_sems.at[a if per_array_sems else 0]) for a in range(n)]
        for cp in mine:
            cp.start()
        first = []
        for a in range(n):
            first.append(copy(a, 0, me, sibling, src=ins[a]))
            first += [copy(a, 1 + j, me, (*chip, c), src=ins[a]) for j, chip in enumerate(chips)]
        for cp in first:
            cp.start()
        passed = []
        for j, chip in enumerate(chips):
            for a in range(n):
                copy(a, 1 + j, (*chip, c), me).wait_recv()
            for a in range(n):
                cp = copy(a, 4 + j, (*chip, c), sibling)
                cp.start()
                passed.append(cp)
        for a in range(n):
            copy(a, 0, sibling, me).wait_recv()
        for j, chip in enumerate(chips):
            for a in range(n):
                copy(a, 4 + j, (*chip, 1 - c), me).wait_recv()
        for cp in first + passed:
            cp.wait_send()
        for cp in mine:
            cp.wait()

    return body


def _sibling_send_body(n, per_array_sems, handshake):
    def body(*refs):
        ins, outs = refs[:n], refs[n:2 * n]
        send_sems, recv_sems, _ = refs[2 * n:]
        x, y, c = lax.axis_index("x"), lax.axis_index("y"), lax.axis_index("c")
        sibling = (x, y, 1 - c)
        if handshake:
            _handshake([sibling])

        def sem(sems, a):
            return sems.at[a, 0] if per_array_sems else sems.at[0]

        copies = [pltpu.make_async_remote_copy(
            src_ref=ins[a].at[2 * q + 1 - c], dst_ref=outs[a].at[q], send_sem=sem(send_sems, a), recv_sem=sem(recv_sems, a),
            device_id=sibling, device_id_type=MESH) for a in range(n) for q in range(4)]
        for cp in copies:
            cp.start()
        for cp in copies:
            cp.wait_recv()
        for cp in copies:
            cp.wait_send()

    return body


def _chip_exchange_body(n, per_array_sems, handshake):
    def body(*refs):
        ins, outs = refs[:n], refs[n:2 * n]
        send_sems, recv_sems, local_sems = refs[2 * n:]
        x, y, c = lax.axis_index("x"), lax.axis_index("y"), lax.axis_index("c")
        my_chip = 2 * x + y
        chips = [(1 - x, y), (x, 1 - y), (1 - x, 1 - y)]
        if handshake:
            _handshake([(*chip, c) for chip in chips])

        def sem(sems, a, k):
            return sems.at[a, k] if per_array_sems else sems.at[k]

        def copy(a, k, landing):
            px, py = chips[k]
            return pltpu.make_async_remote_copy(
                src_ref=ins[a].at[2 * px + py], dst_ref=outs[a].at[landing], send_sem=sem(send_sems, a, k),
                recv_sem=sem(recv_sems, a, k), device_id=(px, py, c), device_id_type=MESH)

        mine = [pltpu.make_async_copy(ins[a].at[my_chip], outs[a].at[my_chip], local_sems.at[a if per_array_sems else 0])
                for a in range(n)]
        for cp in mine:
            cp.start()
        sent = [copy(a, k, my_chip) for a in range(n) for k in range(3)]
        for cp in sent:
            cp.start()
        for a in range(n):
            for k, (px, py) in enumerate(chips):
                copy(a, k, 2 * px + py).wait_recv()
        for cp in sent:
            cp.wait_send()
        for cp in mine:
            cp.wait()

    return body


def _all_gather(arrs, name):
    n = len(arrs)
    return pl.pallas_call(
        _gather_body(n, True, False, ["stack"] * n), name=name, in_specs=[ANY] * n, out_specs=[ANY] * n,
        out_shape=[jax.ShapeDtypeStruct((N_DEV,) + a.shape, a.dtype) for a in arrs],
        scratch_shapes=[pltpu.SemaphoreType.DMA((n, 7)), pltpu.SemaphoreType.DMA((n, 7)), pltpu.SemaphoreType.DMA((n,))],
    )(*arrs)


def _on_sequencer(body, arrs, out_type, name, collective_id):
    return pl.kernel(
        body, name=name, out_type=out_type, mesh=plsc.ScalarSubcoreMesh(axis_name="sequencer", num_cores=1),
        scratch_types=[pltpu.SemaphoreType.DMA((RING_SEMS,)), pltpu.SemaphoreType.DMA((RING_SEMS,)),
                       pltpu.SemaphoreType.DMA((1,))],
        compiler_params=pltpu.CompilerParams(collective_id=collective_id))(*arrs)


RING_SEMS = 8


def _ring_gather_body(n, per_array_sems, handshake, layouts):
    def body(*refs):
        ins, outs = refs[:n], refs[n:2 * n]
        send_sems, recv_sems, local_sems = refs[2 * n:]
        x, y, c = lax.axis_index("x"), lax.axis_index("y"), lax.axis_index("c")
        me, sibling = (x, y, c), (x, y, 1 - c)
        over_x, over_y, diagonal = (1 - x, y, c), (x, 1 - y, c), (1 - x, 1 - y, c)
        if handshake:
            _handshake([sibling, over_x, over_y])

        def sem(sems, a, k):
            return sems.at[a, k] if per_array_sems else sems.at[k]

        def slot(a, p, half=None):
            block = 4 * p[0] + 2 * p[1] + p[2]
            r, c_ = ins[a].shape
            first, count = (0, r) if half is None else (half * (r // 2), r // 2)
            if layouts[a] == "rows":
                return outs[a].at[pl.ds(block * r + first, count), :]
            if layouts[a] == "cols":
                return outs[a].at[pl.ds(first, count), pl.ds(block * c_, c_)]
            return outs[a].at[block, pl.ds(first, count), :]

        def copy(a, k, block, to, half=None, src=None):
            return pltpu.make_async_remote_copy(
                src_ref=slot(a, block, half) if src is None else src, dst_ref=slot(a, block, half),
                send_sem=sem(send_sems, a, k), recv_sem=sem(recv_sems, a, k), device_id=to, device_id_type=MESH)

        def start(copies):
            for cp in copies:
                cp.start()
            return copies

        def arrived(k, block, half=None):
            for a in range(n):
                copy(a, k, block, me, half).wait_recv()

        arrays = range(n)
        mine = start([pltpu.make_async_copy(ins[a], slot(a, me), local_sems.at[a if per_array_sems else 0]) for a in arrays])
        sent = start([copy(a, k, me, to, src=ins[a]) for a in arrays for k, to in ((0, sibling), (1, over_x), (2, over_y))])
        arrived(1, over_x)
        sent += start([copy(a, 3, over_x, over_y, half=0) for a in arrays] + [copy(a, 5, over_x, sibling) for a in arrays])
        arrived(2, over_y)
        sent += start([copy(a, 4, over_y, over_x, half=1) for a in arrays] + [copy(a, 6, over_y, sibling) for a in arrays])
        arrived(3, diagonal, half=0)
        arrived(4, diagonal, half=1)
        sent += start([copy(a, 7, diagonal, sibling) for a in arrays])
        arrived(0, sibling)
        for k, chip in ((5, over_x), (6, over_y), (7, diagonal)):
            arrived(k, (chip[0], chip[1], 1 - c))
        for cp in sent:
            cp.wait_send()
        for cp in mine:
            cp.wait()

    return body


def _all_gather_async(arrs, name, collective_id, layouts):
    shape = dict(stack=lambda r, c: (N_DEV, r, c), rows=lambda r, c: (N_DEV * r, c), cols=lambda r, c: (r, N_DEV * c))
    out_type = [jax.ShapeDtypeStruct(shape[lay](*a.shape), a.dtype) for a, lay in zip(arrs, layouts)]
    return _on_sequencer(_ring_gather_body(len(arrs), False, True, layouts), arrs, out_type, name, collective_id)


def _sibling_send_async(arrs, name, collective_id):
    out_type = [jax.ShapeDtypeStruct((N_DEV // 2,) + a.shape[1:], a.dtype) for a in arrs]
    return _on_sequencer(_sibling_send_body(len(arrs), False, True), arrs, out_type, name, collective_id)


def _chip_exchange_async(arrs, name, collective_id):
    out_type = [jax.ShapeDtypeStruct(a.shape, a.dtype) for a in arrs]
    return _on_sequencer(_chip_exchange_body(len(arrs), False, True), arrs, out_type, name, collective_id)


def _pair_sum(name, parity, mine, theirs):
    n = len(mine)

    def body(par_ref, *refs):
        for a in range(n):
            refs[2 * n + a][...] = (refs[a][...].astype(F32) + refs[n + a][...].astype(F32)).astype(refs[2 * n + a].dtype)

    def block(a):
        return (None,) + a.shape[1:]

    grid_spec = pltpu.PrefetchScalarGridSpec(
        num_scalar_prefetch=1, grid=(N_DEV // 2,),
        in_specs=[pl.BlockSpec(block(a), lambda q, par: (2 * q + par[0], 0, 0)) for a in mine]
        + [pl.BlockSpec(block(a), lambda q, par: (q, 0, 0)) for a in theirs],
        out_specs=[pl.BlockSpec(block(a), lambda q, par: (q, 0, 0)) for a in theirs])
    return pl.pallas_call(body, name=name, grid_spec=grid_spec, out_shape=[jax.ShapeDtypeStruct(a.shape, a.dtype) for a in theirs],
                          compiler_params=_params())(parity, *mine, *theirs)


NN = (((1,), (0,)), ((), ()))
NT = (((1,), (1,)), ((), ()))
TN = (((0,), (0,)), ((), ()))


def _matmul(name, a, b, dims, grid, a_spec, b_spec, outs, epilogue, extras=(), prologue=None):
    n_in = 2 + len(extras)
    n_out = len(outs)

    def body(*refs):
        a_ref, b_ref = refs[0], refs[1]
        ex, out_refs = refs[2:n_in], refs[n_in:n_in + n_out]
        a_tile = a_ref[...] if prologue is None else prologue(a_ref[...])
        epilogue(lax.dot_general(a_tile, b_ref[...], dims, preferred_element_type=F32), ex, out_refs)

    return pl.pallas_call(
        body, name=name, grid=grid, in_specs=[a_spec, b_spec] + [s for _, s in extras], out_specs=[s for _, _, s in outs],
        out_shape=[jax.ShapeDtypeStruct(sh, dt) for sh, dt, _ in outs], compiler_params=_params(),
    )(a, b, *[e for e, _ in extras])


def _square(tile):
    return tile * tile


def _ep_store(acc, ex, outs):
    outs[0][...] = acc.astype(outs[0].dtype)


def _norm_mod_rows(xv, v_ref):
    r = lax.rsqrt(jnp.mean(xv * xv, axis=-1, keepdims=True) + NORM_EPS)
    h = (xv * r) * v_ref[0:1, :]
    return h * (1.0 + v_ref[1:2, :]) + v_ref[2:3, :]


def _ep_residual_norm(acc, ex, outs):
    outs[0][...] = acc.astype(outs[0].dtype)
    xv = ex[0][...] + ex[1][0:1, :] * acc
    outs[1][...] = xv
    outs[2][...] = _norm_mod_rows(xv, ex[2]).astype(outs[2].dtype)


def _mm_tokens(name, a, w, transposed, out_dtypes, epilogue=_ep_store, extras=(), prologue=None, rows=MM_TOKENS):
    m, n = a.shape[0], w.shape[0 if transposed else 1]
    tm = _tile(m, rows)
    rows_spec = lambda width: pl.BlockSpec((tm, width), lambda i, j, kk: (i, 0))
    whole = lambda arr: pl.BlockSpec(arr.shape, lambda i, j, kk: (0, 0))
    return _matmul(name, a, w, NT if transposed else NN, (m // tm, 1, 1), rows_spec(a.shape[1]), whole(w),
                   [((m, n), dt, rows_spec(n)) for dt in out_dtypes], epilogue,
                   extras=[(e.table, _group_spec(e, 3)) if isinstance(e, Group) else (e, rows_spec(n)) for e in extras],
                   prologue=prologue)


def _resident(w):
    return pl.BlockSpec(w.shape, lambda i: (0, 0), pipeline_mode=pl.Buffered(1))


def _mlp_forward(name, h, w1, w2, x, gate, next_norm=None):
    m, d = x.shape
    f = w1.shape[1]
    tm = _tile(m, MLP_TOKENS)
    rows = lambda width: pl.BlockSpec((tm, width), lambda i: (i, 0))
    chained = next_norm is not None

    def body(h_ref, w1_ref, w2_ref, x_ref, g_ref, *rest):
        r_ref, ff_ref, out_ref = rest[chained:chained + 3]
        r = jnp.maximum(lax.dot_general(h_ref[...], w1_ref[...], NN, preferred_element_type=F32), 0.0).astype(r_ref.dtype)
        r_ref[...] = r
        ff = lax.dot_general(r * r, w2_ref[...], NN, preferred_element_type=F32)
        ff_ref[...] = ff.astype(ff_ref.dtype)
        xv = x_ref[...] + g_ref[0:1, :] * ff
        out_ref[...] = xv
        if chained:
            rest[4][...] = _norm_mod_rows(xv, rest[0]).astype(rest[4].dtype)

    return pl.pallas_call(
        body, name=name, grid=(m // tm,),
        in_specs=[rows(d), _resident(w1), _resident(w2), rows(d), _group_spec(gate)] + ([_group_spec(next_norm)] if chained else []),
        out_specs=[rows(f), rows(d), rows(d)] + ([rows(d)] if chained else []),
        out_shape=[jax.ShapeDtypeStruct((m, f), MXU), jax.ShapeDtypeStruct((m, d), MXU), jax.ShapeDtypeStruct((m, d), F32)]
        + ([jax.ShapeDtypeStruct((m, d), MXU)] if chained else []),
        compiler_params=_params())(h, w1, w2, x, gate.table, *([next_norm.table] if chained else []))


def _mlp_backward(name, dff, r, w1, w2, x, vec, dres, y, gate):
    m, d = dff.shape
    f = r.shape[1]
    tm = _tile(m, MLP_TOKENS)
    nt = m // tm
    rows = lambda width: pl.BlockSpec((tm, width), lambda i: (i, 0))

    def body(dff_ref, r_ref, w1_ref, w2_ref, x_ref, v_ref, dres_ref, y_ref, g_ref, da_ref, dx_ref, acc_ref, dy_ref):
        ds = lax.dot_general(dff_ref[...], w2_ref[...], NT, preferred_element_type=F32)
        da = (ds * (2.0 * r_ref[...].astype(F32))).astype(da_ref.dtype)
        da_ref[...] = da
        dh = lax.dot_general(da, w1_ref[...], NT, preferred_element_type=F32)
        _norm_grad_step(nt, dh, x_ref, v_ref, dres_ref, dx_ref, acc_ref, (y_ref, g_ref, dy_ref))

    return pl.pallas_call(
        body, name=name, grid=(nt,),
        in_specs=[rows(d), rows(f), _resident(w1), _resident(w2), rows(d), _group_spec(vec), rows(d), rows(d), _group_spec(gate)],
        out_specs=[rows(f), rows(d), _vec_spec(), rows(d)],
        out_shape=[jax.ShapeDtypeStruct((m, f), MXU), jax.ShapeDtypeStruct((m, d), F32),
                   jax.ShapeDtypeStruct((SUBLANES, d), F32), jax.ShapeDtypeStruct((m, d), MXU)],
        compiler_params=_params())(dff, r, w1, w2, x, vec.table, dres, y, gate.table)


def _mm_norm_grad(name, a, w, x, vec, dres, y=None, gate=None):
    m = a.shape[0]
    tm = _tile(m, MM_TOKENS)
    nt = m // tm
    branch = y is not None
    rows = lambda width: pl.BlockSpec((tm, width), lambda i: (i, 0))

    def body(a_ref, w_ref, x_ref, v_ref, dres_ref, *rest):
        dh = lax.dot_general(a_ref[...], w_ref[...], NT, preferred_element_type=F32)
        dx_ref, acc_ref = rest[2 * branch], rest[2 * branch + 1]
        _norm_grad_step(nt, dh, x_ref, v_ref, dres_ref, dx_ref, acc_ref, (rest[0], rest[1], rest[4]) if branch else None)

    return pl.pallas_call(
        body, name=name, grid=(nt,),
        in_specs=[rows(a.shape[1]), _resident(w), rows(D), _group_spec(vec), rows(D)] + ([rows(D), _group_spec(gate)] if branch else []),
        out_specs=[rows(D), _vec_spec()] + ([rows(D)] if branch else []),
        out_shape=[jax.ShapeDtypeStruct((m, D), F32), jax.ShapeDtypeStruct((SUBLANES, D), F32)]
        + ([jax.ShapeDtypeStruct((m, D), MXU)] if branch else []),
        compiler_params=_params())(a, w, x, vec.table, dres, *([y, gate.table] if branch else []))


def _mm_grad(name, a, b, shard_cols, prologue=None):
    t, m = a.shape
    n = b.shape[1]
    tm = _tile(m, MM_TILE)
    if shard_cols:
        tn = n // N_DEV
        out = ((N_DEV, m, tn), MXU, pl.BlockSpec((None, tm, tn), lambda i, j, kk: (j, i, 0)))
    else:
        tn = _tile(n, MM_TILE)
        out = ((m, n), MXU, pl.BlockSpec((tm, tn), lambda i, j, kk: (i, j)))
    return _matmul(name, a, b, TN, (m // tm, n // tn, 1), pl.BlockSpec((t, tm), lambda i, j, kk: (0, i)),
                   pl.BlockSpec((t, tn), lambda i, j, kk: (0, j)), [out], _ep_store, prologue=prologue)[0]


def _rows_spec(tt, width=D):
    return pl.BlockSpec((tt, width), lambda i: (i, 0))


def _vec_spec(rows=SUBLANES, width=D):
    return pl.BlockSpec((rows, width), lambda i: (0, 0))


Group = collections.namedtuple("Group", ["table", "index"])


def _group_spec(group, grid_rank=1):
    if grid_rank == 1:
        return pl.BlockSpec((SUBLANES, D), lambda i: (group.index, 0))
    return pl.BlockSpec((SUBLANES, D), lambda i, j, kk: (group.index, 0))


def _norm_mod(name, x, vec):
    t = x.shape[0]
    tt = _tile(t, ROW_TILE)

    def body(x_ref, v_ref, h_ref):
        xv = x_ref[...]
        r = lax.rsqrt(jnp.mean(xv * xv, axis=-1, keepdims=True) + NORM_EPS)
        h = (xv * r) * v_ref[0:1, :]
        h_ref[...] = (h * (1.0 + v_ref[1:2, :]) + v_ref[2:3, :]).astype(h_ref.dtype)

    return pl.pallas_call(body, name=name, grid=(t // tt,), in_specs=[_rows_spec(tt), _group_spec(vec)], out_specs=_rows_spec(tt),
                          out_shape=jax.ShapeDtypeStruct((t, D), MXU))(x, vec.table)


GATE_ROW = 3


def _branch_grad(dx, y_ref, g_ref, dy_ref, acc_ref):
    dy_ref[...] = (dx * g_ref[0:1, :]).astype(dy_ref.dtype)
    acc_ref[GATE_ROW:GATE_ROW + 1, :] += _row_sum(dx * y_ref[...])


def _norm_grad_step(nt, dh, x_ref, v_ref, dres_ref, dx_ref, acc_ref, branch):
    i = pl.program_id(0)

    @pl.when(i == 0)
    def _():
        acc_ref[...] = jnp.zeros_like(acc_ref)

    xv = x_ref[...]
    r = lax.rsqrt(jnp.mean(xv * xv, axis=-1, keepdims=True) + NORM_EPS)
    xn = xv * r
    w = v_ref[0:1, :] * (1.0 + v_ref[1:2, :])
    acc_ref[0:1, :] += _row_sum(dh * xn)
    acc_ref[2:3, :] += _row_sum(dh)
    dxn = dh * w
    dx = dres_ref[...] + r * (dxn - xn * jnp.mean(dxn * xn, axis=-1, keepdims=True))
    dx_ref[...] = dx
    if branch is not None:
        _branch_grad(dx, branch[0], branch[1], branch[2], acc_ref)

    @pl.when(i == nt - 1)
    def _():
        dw = acc_ref[0:1, :]
        acc_ref[1:2, :] = dw * v_ref[0:1, :]
        acc_ref[0:1, :] = dw * (1.0 + v_ref[1:2, :])


def _loss_head(name, x, target, vec, y, gate):
    t = x.shape[0]
    tt = _tile(t, ROW_TILE)

    def body(x_ref, t_ref, v_ref, y_ref, g_ref, dx_ref, acc_ref, dy_ref):
        @pl.when(pl.program_id(0) == 0)
        def _():
            acc_ref[...] = jnp.zeros_like(acc_ref)

        xv = x_ref[...]
        r = lax.rsqrt(jnp.mean(xv * xv, axis=-1, keepdims=True) + NORM_EPS)
        xn = xv * r
        gain = v_ref[0:1, :]
        err = xn * gain - t_ref[...]
        acc_ref[1:2, :] += _row_sum(err * err) * (0.5 / D)
        dout = err * (1.0 / D)
        acc_ref[0:1, :] += _row_sum(dout * xn)
        dxn = dout * gain
        dx = r * (dxn - xn * jnp.mean(dxn * xn, axis=-1, keepdims=True))
        dx_ref[...] = dx
        _branch_grad(dx, y_ref, g_ref, dy_ref, acc_ref)

    return pl.pallas_call(
        body, name=name, grid=(t // tt,),
        in_specs=[_rows_spec(tt), _rows_spec(tt), _group_spec(vec), _rows_spec(tt), _group_spec(gate)],
        out_specs=[_rows_spec(tt), _vec_spec(), _rows_spec(tt)],
        out_shape=[jax.ShapeDtypeStruct((t, D), F32), jax.ShapeDtypeStruct((SUBLANES, D), F32),
                   jax.ShapeDtypeStruct((t, D), MXU)])(x, target, vec.table, y, gate.table)


def _shift_down(x, halo, k):
    y = pltpu.roll(x, k, 0)
    top = jnp.where(lax.broadcasted_iota(jnp.int32, halo.shape, 0) < k, pltpu.roll(halo, k, 0), y[0:SUBLANES, :])
    return jnp.concatenate([top, y[SUBLANES:, :]], axis=0)


def _shift_up(x, halo, k):
    n = x.shape[0]
    y = pltpu.roll(x, n - k, 0)
    bottom = jnp.where(lax.broadcasted_iota(jnp.int32, halo.shape, 0) >= SUBLANES - k, pltpu.roll(halo, SUBLANES - k, 0),
                       y[n - SUBLANES:, :])
    return jnp.concatenate([y[:n - SUBLANES, :], bottom], axis=0)


def _scan_rows(a, u, reverse):
    row = lax.broadcasted_iota(jnp.int32, a.shape, 0)
    for d in (1, 2, 4):
        edge = row >= SUBLANES - d if reverse else row < d
        shift = SUBLANES - d if reverse else d
        a_far = jnp.where(edge, 1.0, pltpu.roll(a, shift, 0))
        u_far = jnp.where(edge, 0.0, pltpu.roll(u, shift, 0))
        u = u + a * u_far
        a = a * a_far
    return a, u


def _rg_gates(xb, halo, cw_ref, vec_ref, wa_ref, wx_ref, at_start):
    shifted = [xb] + [_shift_down(xb, halo, k) for k in range(1, CONV_WIDTH)]
    xc = vec_ref[0:1, :] + shifted[0] * cw_ref[CONV_WIDTH - 1:CONV_WIDTH, :]
    for k in range(1, CONV_WIDTH):
        xc = xc + shifted[k] * cw_ref[CONV_WIDTH - 1 - k:CONV_WIDTH - k, :]
    heads = [slice(h * HEAD_DIM, (h + 1) * HEAD_DIM) for h in range(HEADS)]
    pa = jnp.concatenate([_dot(xc[:, s], wa_ref[h]) for h, s in enumerate(heads)], axis=1) + vec_ref[1:2, :]
    px = jnp.concatenate([_dot(xc[:, s], wx_ref[h]) for h, s in enumerate(heads)], axis=1) + vec_ref[2:3, :]
    ra, ia = _sigmoid(pa), _sigmoid(px)
    nl = -vec_ref[3:4, :]
    sp = jnp.maximum(nl, 0.0) + jnp.log(1.0 + jnp.exp(-jnp.abs(nl)))
    log_a = (-LRU_C) * ra * sp
    a = jnp.exp(log_a)
    th = jnp.tanh(log_a)
    is_t0 = jnp.logical_and(lax.broadcasted_iota(jnp.int32, xb.shape, 0) == 0, at_start)
    mult = jnp.where(is_t0, 1.0, jnp.sqrt(-2.0 * th / (1.0 - th)))
    return dict(shifted=shifted, xc=xc, ra=ra, ia=ia, sp=sp, a=a, mult=mult, is_t0=is_t0, heads=heads)


def _rg_specs(tt, nt, order, cw, vec):
    blk = tt // SUBLANES
    return dict(
        x=pl.BlockSpec((tt, D), lambda i: (order(i), 0)), y=pl.BlockSpec((tt, D), lambda i: (order(i), 1)),
        halo=pl.BlockSpec((SUBLANES, D), lambda i: (jnp.maximum(order(i) * blk - 1, 0), 0)),
        cw=_group_spec(cw), vec=_group_spec(vec), w=pl.BlockSpec((HEADS, HEAD_DIM, HEAD_DIM), lambda i: (0, 0, 0)))


def _rg_forward(name, z, cw, vec, wa, wx):
    t = z.shape[0]
    tt = _tile(t, RG_ROWS)
    nt = t // tt
    sp = _rg_specs(tt, nt, lambda i: i, cw, vec)

    def body(zx_ref, zy_ref, halo_ref, cw_ref, vec_ref, wa_ref, wx_ref, p_ref, h_ref, a_s, u_s, carry):
        i = pl.program_id(0)

        @pl.when(i == 0)
        def _():
            carry[...] = jnp.zeros_like(carry)

        halo = jnp.where(i > 0, halo_ref[...], 0.0)
        g = _rg_gates(zx_ref[...], halo, cw_ref, vec_ref, wa_ref, wx_ref, i == 0)
        a_s[...] = g["a"]
        u_s[...] = g["mult"] * (g["ia"] * g["xc"])

        def group(gi, h):
            rows = pl.ds(pl.multiple_of(gi * SUBLANES, SUBLANES), SUBLANES)
            decay, inner = _scan_rows(a_s[rows, :], u_s[rows, :], reverse=False)
            out = inner + decay * h
            h_ref[rows, :] = out
            return out[SUBLANES - 1:SUBLANES, :]

        carry[0:1, :] = lax.fori_loop(0, tt // SUBLANES, group, carry[0:1, :])
        p_ref[...] = (h_ref[...] * _gelu(zy_ref[...])).astype(p_ref.dtype)

    return pl.pallas_call(
        body, name=name, grid=(nt,), in_specs=[sp["x"], sp["y"], sp["halo"], sp["cw"], sp["vec"], sp["w"], sp["w"]],
        out_specs=[_rows_spec(tt), _rows_spec(tt)],
        out_shape=[jax.ShapeDtypeStruct((t, D), MXU), jax.ShapeDtypeStruct((t, D), F32)],
        scratch_shapes=[pltpu.VMEM((tt, D), F32), pltpu.VMEM((tt, D), F32), pltpu.VMEM((SUBLANES, D), F32)],
        compiler_params=_params())(z, z, z, cw.table, vec.table, wa, wx)


def _rg_backward(name, dp, z, h, cw, vec, wa, wx):
    t = z.shape[0]
    tt = _tile(t, RG_ROWS)
    nt = t // tt
    rev = lambda i: nt - 1 - i
    sp = _rg_specs(tt, nt, rev, cw, vec)
    rows_rev = pl.BlockSpec((tt, D), lambda i: (rev(i), 0))

    def body(dp_ref, zx_ref, zy_ref, halo_ref, h_ref, hhalo_ref, cw_ref, vec_ref, wa_ref, wx_ref,
             dz_ref, dvec_ref, dcw_ref, dwa_ref, dwx_ref, a_s, d_s, carry, nxt):
        i = pl.program_id(0)
        j = rev(i)

        @pl.when(i == 0)
        def _():
            carry[...] = jnp.zeros_like(carry)
            nxt[...] = jnp.zeros_like(nxt)
            dvec_ref[...] = jnp.zeros_like(dvec_ref)
            dcw_ref[...] = jnp.zeros_like(dcw_ref)
            dwa_ref[...] = jnp.zeros_like(dwa_ref)
            dwx_ref[...] = jnp.zeros_like(dwx_ref)

        halo = jnp.where(j > 0, halo_ref[...], 0.0)
        g = _rg_gates(zx_ref[...], halo, cw_ref, vec_ref, wa_ref, wx_ref, j == 0)
        xc, ra, ia, a, mult = g["xc"], g["ra"], g["ia"], g["a"], g["mult"]
        hv, zy, dpv = h_ref[...], zy_ref[...], dp_ref[...]
        gelu, gelu_grad = _gelu_and_grad(zy)
        dyb = dpv * hv * gelu_grad
        a_s[...] = a
        d_s[...] = dpv * gelu

        def group(gi, c):
            rows = pl.ds(pl.multiple_of((tt // SUBLANES - 1 - gi) * SUBLANES, SUBLANES), SUBLANES)
            a8 = a_s[rows, :]
            last = lax.broadcasted_iota(jnp.int32, a8.shape, 0) == SUBLANES - 1
            above = jnp.where(last, 1.0, pltpu.roll(a8, SUBLANES - 1, 0))
            decay, inner = _scan_rows(above, d_s[rows, :], reverse=True)
            out = inner + decay * c
            d_s[rows, :] = out
            return a8[0:1, :] * out[0:1, :]

        carry[0:1, :] = lax.fori_loop(0, tt // SUBLANES, group, carry[0:1, :])
        dht = d_s[...]
        hprev = _shift_down(hv, jnp.where(j > 0, hhalo_ref[...], 0.0), 1)
        ixc = ia * xc
        dlog_a = dht * hprev * a + jnp.where(g["is_t0"], 0.0, dht * ixc * (-(a * a) / mult))
        dia = dht * mult * xc
        dxc = dht * mult * ia
        dra = dlog_a * ((-LRU_C) * g["sp"])
        nl = -vec_ref[3:4, :]
        dvec_ref[3:4, :] += _row_sum(dlog_a * ((-LRU_C) * ra)) * (-_sigmoid(nl))
        dpa = dra * ra * (1.0 - ra)
        dpx = dia * ia * (1.0 - ia)
        dvec_ref[1:2, :] += _row_sum(dpa)
        dvec_ref[2:3, :] += _row_sum(dpx)
        xc_m, dpa_m, dpx_m = xc.astype(MXU), dpa.astype(MXU), dpx.astype(MXU)
        back = [_dot_nt(dpa_m[:, s], wa_ref[hd]) + _dot_nt(dpx_m[:, s], wx_ref[hd]) for hd, s in enumerate(g["heads"])]
        dwa = [_dot_tn(xc_m[:, s], dpa_m[:, s]) for s in g["heads"]]
        dwx = [_dot_tn(xc_m[:, s], dpx_m[:, s]) for s in g["heads"]]
        for hd in range(HEADS):
            dwa_ref[hd] += dwa[hd]
            dwx_ref[hd] += dwx[hd]
        dxc = dxc + jnp.concatenate(back, axis=1)
        dvec_ref[0:1, :] += _row_sum(dxc)
        dxb = dxc * cw_ref[CONV_WIDTH - 1:CONV_WIDTH, :]
        for k in range(CONV_WIDTH):
            row = CONV_WIDTH - 1 - k
            dcw_ref[row:row + 1, :] += _row_sum(dxc * g["shifted"][k])
            if k:
                dxb = dxb + _shift_up(dxc, nxt[...], k) * cw_ref[row:row + 1, :]
        nxt[...] = dxc[0:SUBLANES, :]
        dz_ref[:, 0:D] = dxb.astype(dz_ref.dtype)
        dz_ref[:, D:2 * D] = dyb.astype(dz_ref.dtype)

    hhalo = pl.BlockSpec((SUBLANES, D), lambda i: (jnp.maximum(rev(i) * (tt // SUBLANES) - 1, 0), 0))
    wacc = pl.BlockSpec((HEADS, HEAD_DIM, HEAD_DIM), lambda i: (0, 0, 0))
    return pl.pallas_call(
        body, name=name, grid=(nt,),
        in_specs=[rows_rev, sp["x"], sp["y"], sp["halo"], rows_rev, hhalo, sp["cw"], sp["vec"], sp["w"], sp["w"]],
        out_specs=[pl.BlockSpec((tt, 2 * D), lambda i: (rev(i), 0)), _vec_spec(), _vec_spec(), wacc, wacc],
        out_shape=[jax.ShapeDtypeStruct((t, 2 * D), MXU), jax.ShapeDtypeStruct((SUBLANES, D), F32),
                   jax.ShapeDtypeStruct((SUBLANES, D), F32), jax.ShapeDtypeStruct((HEADS, HEAD_DIM, HEAD_DIM), F32),
                   jax.ShapeDtypeStruct((HEADS, HEAD_DIM, HEAD_DIM), F32)],
        scratch_shapes=[pltpu.VMEM((tt, D), F32), pltpu.VMEM((tt, D), F32), pltpu.VMEM((SUBLANES, D), F32),
                        pltpu.VMEM((SUBLANES, D), F32)],
        compiler_params=_params())(dp, z, z, z, h, h, cw.table, vec.table, wa, wx)


def _hg_chunk(zq, zf, zi, lb):
    c = HG_CHUNK
    q, q_grad = _silu_and_grad(zq)
    sg = _sigmoid(zf)
    fg = lb + (1.0 - lb) * sg
    k = 1.0 - fg
    row, col = lax.broadcasted_iota(jnp.int32, (c, c), 0), lax.broadcasted_iota(jnp.int32, (c, c), 1)
    tri, tri_t = (row >= col).astype(F32), (row <= col).astype(F32)
    b = _dot_exact(tri, jnp.log(fg))
    mid, last = b[c // 2 - 1:c // 2, :], b[c - 1:c, :]
    eq = jnp.exp(jnp.minimum(b - mid, EXP_CLAMP))
    ek = jnp.exp(jnp.minimum(mid - b, EXP_CLAMP))
    eb = jnp.exp(b)
    ed = jnp.exp(last - b)
    return dict(q=q, q_grad=q_grad, sg=sg, fg=fg, k=k, v=zi, eq=eq, ek=ek, eb=eb, ed=ed, elast=jnp.exp(last), tri=tri, tri_t=tri_t,
                qe=q * eq, ke=k * ek, qb=q * eb, kd=k * ed)


def _hg_specs(tt, order):
    return [pl.BlockSpec((tt, D), lambda i, col=col: (order(i), col)) for col in range(4)]


def _hg_forward(name, z, vec):
    t = z.shape[0]
    tt = _tile(t, ROW_TILE)
    nt, nc = t // tt, tt // HG_CHUNK
    heads = [slice(h * HEAD_DIM, (h + 1) * HEAD_DIM) for h in range(HEADS)]

    def body(zq_ref, zf_ref, zi_ref, zg_ref, vec_ref, p_ref, o_ref, ss_ref, state):
        @pl.when(pl.program_id(0) == 0)
        def _():
            state[...] = jnp.zeros_like(state)

        lb, gain = vec_ref[0:1, :], vec_ref[1:2, :]

        def chunk(ci, carry):
            rows = pl.ds(pl.multiple_of(ci * HG_CHUNK, HG_CHUNK), HG_CHUNK)
            g = _hg_chunk(zq_ref[rows, :], zf_ref[rows, :], zi_ref[rows, :], lb)
            zg = zg_ref[rows, :]
            causal = g["tri"] > 0.0
            qe_m, ke_m, qb_m, kd_m, v_m = (g[n].astype(MXU) for n in ("qe", "ke", "qb", "kd", "v"))
            atts = [_dot_nt(qe_m[:, s], ke_m[:, s]) for s in heads]
            o_parts, p_parts = [], []
            for hd, s in enumerate(heads):
                st = state[hd]
                ss_ref[ci, hd] = st
                att = jnp.where(causal, atts[hd], 0.0)
                o_parts.append(_dot(att, v_m[:, s]) + _dot_nt(qb_m[:, s], st))
                state[hd] = st * g["elast"][:, s] + _dot_tn(v_m[:, s], kd_m[:, s])
            for o, s in zip(o_parts, heads):
                r = lax.rsqrt(jnp.mean(o * o, axis=-1, keepdims=True) + GNORM_EPS)
                p_parts.append((o * r) * gain[:, s])
            o_ref[rows, :] = jnp.concatenate(o_parts, axis=1)
            p_ref[rows, :] = (jnp.concatenate(p_parts, axis=1) * _silu(zg)).astype(p_ref.dtype)
            return carry

        lax.fori_loop(0, nc, chunk, 0, unroll=True)

    return pl.pallas_call(
        body, name=name, grid=(nt,), in_specs=_hg_specs(tt, lambda i: i) + [_group_spec(vec)],
        out_specs=[_rows_spec(tt), _rows_spec(tt), pl.BlockSpec((nc, HEADS, HEAD_DIM, HEAD_DIM), lambda i: (i, 0, 0, 0))],
        out_shape=[jax.ShapeDtypeStruct((t, D), MXU), jax.ShapeDtypeStruct((t, D), F32),
                   jax.ShapeDtypeStruct((t // HG_CHUNK, HEADS, HEAD_DIM, HEAD_DIM), F32)],
        scratch_shapes=[pltpu.VMEM((HEADS, HEAD_DIM, HEAD_DIM), F32)], compiler_params=_params())(z, z, z, z, vec.table)


def _hg_backward(name, dp, z, o, ss, vec):
    t = z.shape[0]
    tt = _tile(t, ROW_TILE)
    nt, nc = t // tt, tt // HG_CHUNK
    rev = lambda i: nt - 1 - i
    heads = [slice(h * HEAD_DIM, (h + 1) * HEAD_DIM) for h in range(HEADS)]
    rows_rev = pl.BlockSpec((tt, D), lambda i: (rev(i), 0))

    def body(dp_ref, zq_ref, zf_ref, zi_ref, zg_ref, o_ref, ss_ref, vec_ref, dz_ref, acc_ref, dstate):
        @pl.when(pl.program_id(0) == 0)
        def _():
            dstate[...] = jnp.zeros_like(dstate)
            acc_ref[...] = jnp.zeros_like(acc_ref)

        lb, gain = vec_ref[0:1, :], vec_ref[1:2, :]

        def chunk(cr, carry):
            ci = nc - 1 - cr
            rows = pl.ds(pl.multiple_of(ci * HG_CHUNK, HG_CHUNK), HG_CHUNK)
            zq, zg = zq_ref[rows, :], zg_ref[rows, :]
            g = _hg_chunk(zq, zf_ref[rows, :], zi_ref[rows, :], lb)
            ov, dpv = o_ref[rows, :], dp_ref[rows, :]
            causal = g["tri"] > 0.0
            silu_g, silu_g_grad = _silu_and_grad(zg)
            don = dpv * silu_g
            dgate = dpv * silu_g_grad
            qe_m, ke_m, qb_m, kd_m, v_m = (g[n].astype(MXU) for n in ("qe", "ke", "qb", "kd", "v"))
            parts = {n: [] for n in ("dzg", "dgain", "dv", "dqe", "dke", "dqb", "dkd", "dlast")}
            dos, atts, datts = [], [], []
            for hd, s in enumerate(heads):
                oh = ov[:, s]
                r = lax.rsqrt(jnp.mean(oh * oh, axis=-1, keepdims=True) + GNORM_EPS)
                on = oh * r
                parts["dzg"].append(dgate[:, s] * (on * gain[:, s]))
                parts["dgain"].append(_row_sum(don[:, s] * on))
                dtmp = don[:, s] * gain[:, s]
                dos.append((r * (dtmp - on * jnp.mean(dtmp * on, axis=-1, keepdims=True))).astype(MXU))
            for hd, s in enumerate(heads):
                atts.append(_dot_nt(qe_m[:, s], ke_m[:, s]))
                datts.append(_dot_nt(dos[hd], v_m[:, s]))
            for hd, s in enumerate(heads):
                do, att, datt = dos[hd], jnp.where(causal, atts[hd], 0.0).astype(MXU), jnp.where(causal, datts[hd], 0.0).astype(MXU)
                st, dst = ss_ref[ci, hd], dstate[hd]
                st_m, dst_m = st.astype(MXU), dst.astype(MXU)
                qe, ke, qb, kd, v = qe_m[:, s], ke_m[:, s], qb_m[:, s], kd_m[:, s], v_m[:, s]
                parts["dv"].append(_dot_tn(att, do) + _dot_nt(kd, dst_m))
                parts["dqe"].append(_dot(datt, ke))
                parts["dke"].append(_dot_tn(datt, qe))
                parts["dqb"].append(_dot(do, st_m))
                parts["dkd"].append(_dot(v, dst_m))
                parts["dlast"].append(g["elast"][:, s] * _row_sum(dst * st))
                dstate[hd] = dst * g["elast"][:, s] + _dot_tn(do, qb)
            whole = {n: jnp.concatenate(p, axis=1) for n, p in parts.items()}
            acc_ref[1:2, :] += whole["dgain"]
            dqe, dke, dqb, dkd = whole["dqe"], whole["dke"], whole["dqb"], whole["dkd"]
            dq = dqe * g["eq"] + dqb * g["eb"]
            dk = dke * g["ek"] + dkd * g["ed"]
            dkdk = dkd * g["kd"]
            db = dqe * qe_m.astype(F32) - dke * ke_m.astype(F32) + dqb * g["qb"] - dkdk
            dlogf = _dot_exact(g["tri_t"], db) + (whole["dlast"] + _row_sum(dkdk))
            dfg = dlogf / g["fg"] - dk
            sg = g["sg"]
            acc_ref[0:1, :] += _row_sum(dfg * (1.0 - sg))
            dz_ref[rows, 0:D] = (dq * g["q_grad"]).astype(dz_ref.dtype)
            dz_ref[rows, D:2 * D] = (dfg * (1.0 - lb) * sg * (1.0 - sg)).astype(dz_ref.dtype)
            dz_ref[rows, 2 * D:3 * D] = whole["dv"].astype(dz_ref.dtype)
            dz_ref[rows, 3 * D:4 * D] = whole["dzg"].astype(dz_ref.dtype)
            return carry

        lax.fori_loop(0, nc, chunk, 0, unroll=True)

    return pl.pallas_call(
        body, name=name, grid=(nt,),
        in_specs=[rows_rev] + _hg_specs(tt, rev) + [rows_rev, pl.BlockSpec((nc, HEADS, HEAD_DIM, HEAD_DIM), lambda i: (rev(i), 0, 0, 0)),
                                                  _group_spec(vec)],
        out_specs=[pl.BlockSpec((tt, 4 * D), lambda i: (rev(i), 0)), _vec_spec()],
        out_shape=[jax.ShapeDtypeStruct((t, 4 * D), MXU), jax.ShapeDtypeStruct((SUBLANES, D), F32)],
        scratch_shapes=[pltpu.VMEM((HEADS, HEAD_DIM, HEAD_DIM), F32)], compiler_params=_params())(dp, z, z, z, z, o, ss, vec.table)


def _mod_forward(name, c_all, mod_w, lower):
    depth, _, sw = mod_w.shape

    def body(c_ref, w_ref, lo_ref, cs_ref, mod_ref, lb_ref):
        cs = _silu(c_ref[...])
        mod_ref[...] = _dot(cs, w_ref[...])

        @pl.when(pl.program_id(0) == 0)
        def _():
            cs_ref[...] = cs
            lo = lo_ref[...]
            e = jnp.exp(lo - jnp.max(lo, axis=0, keepdims=True))
            sm = e / jnp.sum(e, axis=0, keepdims=True)
            lb_ref[0:1, :] = jnp.zeros((1, D), F32)
            for l in range(1, depth):
                lb_ref[l:l + 1, :] = lb_ref[l - 1:l, :] + sm[l:l + 1, :]

    return pl.pallas_call(
        body, name=name, grid=(depth,),
        in_specs=[pl.BlockSpec((N_DEV, D), lambda l: (0, 0)), pl.BlockSpec((None, D, sw), lambda l: (l, 0, 0)),
                  pl.BlockSpec((depth, D), lambda l: (0, 0))],
        out_specs=[pl.BlockSpec((N_DEV, D), lambda l: (0, 0)), pl.BlockSpec((None, N_DEV, sw), lambda l: (l, 0, 0)),
                   pl.BlockSpec((depth, D), lambda l: (0, 0))],
        out_shape=[jax.ShapeDtypeStruct((N_DEV, D), F32), jax.ShapeDtypeStruct((depth, N_DEV, sw), F32),
                   jax.ShapeDtypeStruct((depth, D), F32)], compiler_params=_params())(c_all, mod_w, lower)


def _mod_weight_grad(name, cs_t, dmod):
    depth, pad, sw = dmod.shape

    def body(c_ref, d_ref, g_ref):
        g_ref[...] = _dot(c_ref[...], d_ref[...])

    return pl.pallas_call(
        body, name=name, grid=(depth,),
        in_specs=[pl.BlockSpec((D, pad), lambda l: (0, 0)), pl.BlockSpec((None, pad, sw), lambda l: (l, 0, 0))],
        out_specs=pl.BlockSpec((None, D, sw), lambda l: (l, 0, 0)),
        out_shape=jax.ShapeDtypeStruct((depth, D, sw), F32), compiler_params=_params())(cs_t, dmod)


def _lower_bound_grad(name, lower, dlb):
    depth = lower.shape[0]

    def body(lo_ref, d_ref, out_ref):
        lo = lo_ref[...]
        e = jnp.exp(lo - jnp.max(lo, axis=0, keepdims=True))
        sm = e / jnp.sum(e, axis=0, keepdims=True)
        out_ref[...] = jnp.zeros_like(out_ref)
        dsm = [jnp.zeros((1, D), F32)]
        for l in range(1, depth):
            tail = d_ref[l:l + 1, :]
            for m in range(l + 1, depth):
                tail = tail + d_ref[m:m + 1, :]
            dsm.append(tail)
        inner = sm[1:2, :] * dsm[1]
        for l in range(2, depth):
            inner = inner + sm[l:l + 1, :] * dsm[l]
        for l in range(depth):
            out_ref[l:l + 1, :] = sm[l:l + 1, :] * (dsm[l] - inner)
        for j in range(SUBLANES - depth):
            row = d_ref[depth + j:depth + j + 1, :]
            tot = row[:, 0:HEAD_DIM]
            for hd in range(1, HEADS):
                tot = tot + row[:, hd * HEAD_DIM:(hd + 1) * HEAD_DIM]
            out_ref[depth + j:depth + j + 1, 0:HEAD_DIM] = tot

    return pl.pallas_call(body, name=name, out_shape=jax.ShapeDtypeStruct((SUBLANES, D), F32))(lower, dlb)


def _adamw(name, parts, w, m, v, layer=None, prev=None):
    p, r, c = parts.shape
    tr = _tile(r, max(SUBLANES, ADAMW_STEP_BYTES // (4 * c * (p + 7))))
    stacked = layer is not None
    n_prev = 4 if prev is not None else 0

    def body(*refs):
        parts_ref, w_ref, m_ref, v_ref = refs[:4]
        g_ref, d_ref, m_out, v_out = refs[4 + n_prev:]
        g = parts_ref[0].astype(F32)
        for q in range(1, p):
            g = g + parts_ref[q].astype(F32)
        m2 = ADAM_B1 * m_ref[...] + (1.0 - ADAM_B1) * g
        v2 = ADAM_B2 * v_ref[...] + (1.0 - ADAM_B2) * (g * g)
        m_hat = m2 / (1.0 - ADAM_B1 ** ADAM_STEP)
        v_hat = v2 / (1.0 - ADAM_B2 ** ADAM_STEP)
        g_ref[...] = g
        d_ref[...] = -ADAM_LR * (m_hat / (jnp.sqrt(v_hat) + ADAM_EPS) + ADAM_WD * w_ref[...])
        m_out[...] = m2
        v_out[...] = v2

    if stacked:
        spec = pl.BlockSpec((None, tr, c), lambda i: (layer, i, 0))
    else:
        spec = pl.BlockSpec((tr, c), lambda i: (i, 0))
    return pl.pallas_call(
        body, name=name, grid=(r // tr,),
        in_specs=[pl.BlockSpec((p, tr, c), lambda i: (0, i, 0)), spec, spec, spec] + [ANY] * n_prev, out_specs=[spec] * 4,
        out_shape=[jax.ShapeDtypeStruct(w.shape, F32)] * 4,
        input_output_aliases={4 + q: q for q in range(n_prev)}, compiler_params=_params(),
    )(_hbm(parts), _hbm(w), _hbm(m), _hbm(v), *(prev or []))


def _hbm(a):
    return pltpu.with_memory_space_constraint(a, pltpu.HBM)


def _pad_rows(a, rows=SUBLANES):
    a = a.reshape(-1, a.shape[-1])
    return jnp.concatenate([a, jnp.zeros((rows - a.shape[0], a.shape[1]), a.dtype)], axis=0) if a.shape[0] < rows else a


def kernel(x, c, mod_w, mod_b, norm_mix, norm_mlp, norm_final, rg_w_in, rg_conv_w, rg_conv_b, rg_w_a, rg_b_a, rg_w_x, rg_b_x, rg_lambda, rg_w_out, hg_w_in, hg_lower_bounds, hg_gnorm, hg_w_out, mlp_w1, mlp_w2, loss_target, m_mod_w, m_mod_b, m_norm_mix, m_norm_mlp, m_norm_final, m_rg_w_in, m_rg_conv_w, m_rg_conv_b, m_rg_w_a, m_rg_b_a, m_rg_w_x, m_rg_b_x, m_rg_lambda, m_rg_w_out, m_hg_w_in, m_hg_lower_bounds, m_hg_gnorm, m_hg_w_out, m_mlp_w1, m_mlp_w2, v_mod_w, v_mod_b, v_norm_mix, v_norm_mlp, v_norm_final, v_rg_w_in, v_rg_conv_w, v_rg_conv_b, v_rg_w_a, v_rg_b_a, v_rg_w_x, v_rg_b_x, v_rg_lambda, v_rg_w_out, v_hg_w_in, v_hg_lower_bounds, v_hg_gnorm, v_hg_w_out, v_mlp_w1, v_mlp_w2):
    me = 4 * lax.axis_index("x") + 2 * lax.axis_index("y") + lax.axis_index("c")
    x0 = x[0]
    target = loss_target[0]
    n_rg, n_hg = rg_w_in.shape[0], hg_w_in.shape[0]
    sw_mod = mod_w.shape[2]

    c_all, cw_all = _all_gather([_pad_rows(c), rg_conv_w.reshape(n_rg * CONV_WIDTH, -1)], "gather_cond")
    c_all = c_all[:, 0, :]
    conv_w = cw_all.transpose(1, 0, 2).reshape(n_rg, CONV_WIDTH, D)
    cs_all, mod_part, lb_all = _mod_forward("mod_forward", c_all, mod_w, hg_lower_bounds)
    (mod_gathered,) = _all_gather([mod_part.reshape(DEPTH * N_DEV, sw_mod)], "gather_mod")

    ids = iter(range(5 * DEPTH))
    shards = []
    for layer in range(DEPTH):
        j = layer // 2
        w_in, w_out = (rg_w_in[j], rg_w_out[j]) if layer % 2 == 0 else (hg_w_in[j], hg_w_out[j])
        shards.append([w_in.astype(MXU), w_out.astype(MXU), mlp_w1[layer].astype(MXU), mlp_w2[layer].astype(MXU)])
    shards, mod_gathered = lax.optimization_barrier((shards, mod_gathered))
    weights = []
    for layer in range(DEPTH):
        g_in, g_out = _all_gather_async(shards[layer][:2], f"gather_mixer_weights_{layer}", next(ids), ["cols", "rows"])
        g_w1, g_w2 = _all_gather_async(shards[layer][2:], f"gather_mlp_weights_{layer}", next(ids), ["cols", "rows"])
        weights.append(dict(w_in=_hbm(g_in), w_out=_hbm(g_out), w1=_hbm(g_w1), w2=_hbm(g_w2)))

    mod_mine = lax.dynamic_index_in_dim(mod_gathered.reshape(N_DEV, DEPTH, N_DEV, sw_mod), me, axis=2, keepdims=False)
    mod = mod_mine.transpose(1, 0, 2).reshape(DEPTH, 6, D) + mod_b.reshape(DEPTH, 6, D)

    pieces, n_groups = [], 0
    placed = [dict() for _ in range(DEPTH)]

    def place(rows):
        nonlocal n_groups
        rows = [r.reshape(-1, D).astype(F32) for r in rows]
        pieces.extend(rows + [jnp.zeros((SUBLANES - sum(r.shape[0] for r in rows), D), F32)])
        n_groups += 1
        return n_groups - 1

    for layer in range(DEPTH):
        j, at = layer // 2, placed[layer]
        at["vec_mix"] = place([norm_mix[layer], mod[layer, 1], mod[layer, 0]])
        at["vec_mlp"] = place([norm_mlp[layer], mod[layer, 4], mod[layer, 3]])
        at["gate_mix"], at["gate_mlp"] = place([mod[layer, 2]]), place([mod[layer, 5]])
        if layer % 2 == 0:
            at["cw"] = place([conv_w[j]])
            at["vec"] = place([rg_conv_b[j], rg_b_a[j], rg_b_x[j], rg_lambda[j]])
        else:
            at["vec"] = place([lb_all[layer], jnp.tile(hg_gnorm[j], HEADS)])
    final_group = place([norm_final])
    table = _hbm(jnp.concatenate(pieces, axis=0))

    saved = []
    xl = x0
    for layer in range(DEPTH):
        j, wt = layer // 2, weights[layer]
        is_rg = layer % 2 == 0
        s = dict(x=xl, **{k: Group(table, g) for k, g in placed[layer].items()})
        s["h"] = h_next if layer else _hbm(_norm_mod("norm_mod", xl, s["vec_mix"]))
        if is_rg:
            (s["z"],) = _mm_tokens("rg_in", s["h"], wt["w_in"], False, (F32,))
            s["wa"], s["wx"] = _hbm(rg_w_a[j].astype(MXU)), _hbm(rg_w_x[j].astype(MXU))
            s["p"], s["hr"] = _rg_forward("rg_forward", s["z"], s["cw"], s["vec"], s["wa"], s["wx"])
        else:
            (s["z"],) = _mm_tokens("hg_in", s["h"], wt["w_in"], False, (F32,))
            s["p"], s["o"], s["ss"] = _hg_forward("hg_forward", s["z"], s["vec"])
        s["y"], s["x1"], h2 = _mm_tokens("mix_out", s["p"], wt["w_out"], False, (MXU, F32, MXU), _ep_residual_norm,
                                         extras=(xl, s["gate_mix"], s["vec_mlp"]))
        s["p"], s["h2"] = _hbm(s["p"]), _hbm(h2)
        next_norm = Group(table, placed[layer + 1]["vec_mix"]) if layer + 1 < DEPTH else None
        s["r"], s["ff"], xl, *chained = _mlp_forward("mlp_forward", s["h2"], wt["w1"], wt["w2"], s["x1"], s["gate_mlp"], next_norm)
        h_next = _hbm(chained[0]) if chained else None
        saved.append(s)

    dx, head, dff = _loss_head("loss_head", xl, target, Group(table, final_group), saved[-1]["ff"], saved[-1]["gate_mlp"])
    g_mlp = head[GATE_ROW:GATE_ROW + 1]

    results = {}
    big = dict(rg_w_in=(rg_w_in, m_rg_w_in, v_rg_w_in), rg_w_out=(rg_w_out, m_rg_w_out, v_rg_w_out),
               hg_w_in=(hg_w_in, m_hg_w_in, v_hg_w_in), hg_w_out=(hg_w_out, m_hg_w_out, v_hg_w_out),
               mlp_w1=(mlp_w1, m_mlp_w1, v_mlp_w1), mlp_w2=(mlp_w2, m_mlp_w2, v_mlp_w2))

    def update_layer(layer, landed):
        mixer = "rg" if layer % 2 == 0 else "hg"
        l_in, l_out, l_w1, l_w2 = landed
        for nm, parts, idx in ((f"{mixer}_w_in", l_in, layer // 2), (f"{mixer}_w_out", l_out, layer // 2),
                               ("mlp_w1", l_w1, layer), ("mlp_w2", l_w2, layer)):
            w, m, v = big[nm]
            results[nm] = _adamw(f"adamw_{nm}", parts, w, m, v, layer=idx, prev=results.get(nm))

    landed = [None] * DEPTH
    small = [None] * DEPTH
    parity = lax.axis_index("c").astype(jnp.int32).reshape(1)

    def send_chip_sums(layer, grads, from_sibling, anchor):
        sums = _pair_sum("pair_sum", parity, grads, from_sibling)
        sums, anchor = lax.optimization_barrier((sums, anchor))
        sums = [_hbm(a) for a in sums]
        landed[layer] = [_hbm(a) for a in _chip_exchange_async(sums, f"exchange_grads_{layer}", next(ids))]
        return anchor

    pending = None
    for layer in reversed(range(DEPTH)):
        j, wt, s = layer // 2, weights[layer], saved[layer]
        is_rg = layer % 2 == 0
        dff = _hbm(dff)
        da, dx1, n_mlp, dyb = _mlp_backward("mlp_backward", dff, s["r"], wt["w1"], wt["w2"], s["x1"], s["vec_mlp"], dx,
                                            s["y"], s["gate_mix"])
        if pending is not None:
            da = send_chip_sums(*pending, da)
        dw2 = _mm_grad("mlp_out_grad", s["r"], dff, False, prologue=_square).reshape(N_DEV, -1, D)
        dw1 = _mm_grad("mlp_in_grad", s["h2"], da, True)
        dyb = _hbm(dyb)
        (dp,) = _mm_tokens("mix_out_t", dyb, wt["w_out"], True, (F32,))
        dw_out = _hbm(_mm_grad("mix_out_grad", s["p"], dyb, False).reshape(N_DEV, -1, D))
        if is_rg:
            dz, dvec, dcw, dwa, dwx = _rg_backward("rg_backward", dp, s["z"], s["hr"], s["cw"], s["vec"], s["wa"], s["wx"])
            gate_grads, dz = lax.optimization_barrier(
                ([dwa.reshape(-1, HEAD_DIM).astype(MXU), dwx.reshape(-1, HEAD_DIM).astype(MXU)], dz))
            gate_parts = _all_gather_async(gate_grads, f"gather_gate_grads_{j}", next(ids), ["stack", "stack"])
            mixer_small = dict(dvec=dvec, dcw=dcw, gate_parts=gate_parts)
            dw_in = _mm_grad("rg_in_grad", s["h"], dz, True)
        else:
            dz, dvec = _hg_backward("hg_backward", dp, s["z"], s["o"], s["ss"], s["vec"])
            mixer_small = dict(dvec=dvec)
            dw_in = _mm_grad("hg_in_grad", s["h"], dz, True)
        grads = [dw_in, dw_out, dw1, dw2]
        pending = (layer, grads, _sibling_send_async(grads, f"pair_grads_{layer}", next(ids)))
        small[layer] = dict(g_mlp=g_mlp, n_mlp=n_mlp, g_mix=n_mlp[GATE_ROW:GATE_ROW + 1], **mixer_small)
        if layer:
            below = saved[layer - 1]
            dx, n_mix, dff = _mm_norm_grad("mixer_in_t", dz, wt["w_in"], s["x"], s["vec_mix"], dx1, below["ff"], below["gate_mlp"])
            g_mlp = n_mix[GATE_ROW:GATE_ROW + 1]
        else:
            dx, n_mix = _mm_norm_grad("mixer_in_t_first", dz, wt["w_in"], s["x"], s["vec_mix"], dx1)
        small[layer]["n_mix"] = n_mix
        if layer + 1 < DEPTH:
            stream = (dx, dff) if layer else (dx,)
            landed[layer + 1], stream = lax.optimization_barrier((landed[layer + 1], stream))
            if "gate_parts" in small[layer + 1]:
                small[layer + 1]["gate_parts"], stream = lax.optimization_barrier((small[layer + 1]["gate_parts"], stream))
            dx, dff = stream if layer else (stream[0], None)
            update_layer(layer + 1, landed[layer + 1])

    dlb_rows = [jnp.zeros((1, D), F32) if l % 2 == 0 else small[l]["dvec"][0:1] for l in range(DEPTH)]
    dgn_rows = [small[2 * j + 1]["dvec"][1:2] for j in range(n_hg)]
    lb_grad = _lower_bound_grad("lower_bound_grad", hg_lower_bounds, _pad_rows(jnp.concatenate(dlb_rows + dgn_rows, axis=0)))
    dmod = jnp.stack([jnp.concatenate([small[l]["n_mix"][2], small[l]["n_mix"][1], small[l]["g_mix"][0],
                                       small[l]["n_mlp"][2], small[l]["n_mlp"][1], small[l]["g_mlp"][0]]) for l in range(DEPTH)])
    groups = [
        dmod.reshape(DEPTH * 6, D),
        jnp.stack([small[l]["n_mix"][0] for l in range(DEPTH)]),
        jnp.stack([small[l]["n_mlp"][0] for l in range(DEPTH)]),
        head[0:1],
        jnp.stack([small[2 * j]["dvec"][0] for j in range(n_rg)]),
        jnp.stack([small[2 * j]["dvec"][1] for j in range(n_rg)]),
        jnp.stack([small[2 * j]["dvec"][2] for j in range(n_rg)]),
        jnp.stack([small[2 * j]["dvec"][3] for j in range(n_rg)]),
        lb_grad[0:DEPTH],
        jnp.concatenate([lb_grad[DEPTH + j:DEPTH + j + 1, 0:HEAD_DIM] for j in range(n_hg)]
                        + [jnp.zeros((1, D - n_hg * HEAD_DIM), F32)], axis=1),
        jnp.concatenate([small[2 * j]["dcw"][0:CONV_WIDTH] for j in range(n_rg)], axis=0),
        head[1:2],
    ]
    params = [mod_b, norm_mix, norm_mlp, norm_final, rg_conv_b, rg_b_a, rg_b_x, rg_lambda, hg_lower_bounds, hg_gnorm]
    moms = [m_mod_b, m_norm_mix, m_norm_mlp, m_norm_final, m_rg_conv_b, m_rg_b_a, m_rg_b_x, m_rg_lambda, m_hg_lower_bounds, m_hg_gnorm]
    vars_ = [v_mod_b, v_norm_mix, v_norm_mlp, v_norm_final, v_rg_conv_b, v_rg_b_a, v_rg_b_x, v_rg_lambda, v_hg_lower_bounds, v_hg_gnorm]
    offsets, rows_of, at = [], [], 0
    for g in groups:
        offsets.append(at)
        rows_of.append(g.shape[0])
        at += -(-g.shape[0] // SUBLANES) * SUBLANES
    packed = jnp.concatenate([_pad_rows(g, -(-g.shape[0] // SUBLANES) * SUBLANES) for g in groups], axis=0)
    (small_parts,) = _all_gather_async([packed], "gather_small_grads", next(ids), ["stack"])
    send_chip_sums(*pending, dx)

    def pack_like(arrs):
        out = []
        for g_rows, off, a in zip(rows_of, offsets, arrs):
            flat = a.reshape(-1)
            flat = jnp.concatenate([flat, jnp.zeros((g_rows * D - flat.shape[0],), F32)])
            out.append(_pad_rows(flat.reshape(g_rows, D), -(-g_rows // SUBLANES) * SUBLANES))
        rest = packed.shape[0] - sum(o.shape[0] for o in out)
        return jnp.concatenate(out + [jnp.zeros((rest, D), F32)], axis=0)

    small_out = _adamw("adamw_small", small_parts, pack_like(params), pack_like(moms), pack_like(vars_))

    def unpack(q, idx, like):
        rows = small_out[q][offsets[idx]:offsets[idx] + rows_of[idx]]
        return rows.reshape(-1)[:like.size].reshape(like.shape)

    loss = jnp.sum(small_out[0][offsets[11]])
    names =["mod_b", "norm_mix", "norm_mlp", "norm_final", "rg_conv_b", "rg_b_a", "rg_b_x", "rg_lambda", "hg_lower_bounds", "hg_gnorm"]
    for idx, (nm, like) in enumerate(zip(names, params)):
        results[nm] = [unpack(q, idx, like) for q in range(4)]

    cw_parts = lax.dynamic_slice_in_dim(small_parts[:, offsets[10]:offsets[10] + n_rg * CONV_WIDTH, :], me * (D // N_DEV), D // N_DEV, axis=2)
    shp = rg_conv_w.shape
    results["rg_conv_w"] = [o.reshape(shp) for o in _adamw(
        "adamw_conv", cw_parts, rg_conv_w.reshape(-1, shp[-1]), m_rg_conv_w.reshape(-1, shp[-1]), v_rg_conv_w.reshape(-1, shp[-1]))]
    shp = rg_w_a.shape
    stacked = (n_rg, HEADS * HEAD_DIM, HEAD_DIM)
    for nm, which, (w, m, v) in (("rg_w_a", 0, (rg_w_a, m_rg_w_a, v_rg_w_a)), ("rg_w_x", 1, (rg_w_x, m_rg_w_x, v_rg_w_x))):
        out = None
        for j in reversed(range(n_rg)):
            out = _adamw("adamw_gate", small[2 * j]["gate_parts"][which], w.reshape(stacked), m.reshape(stacked),
                         v.reshape(stacked), layer=j, prev=out)
        results[nm] = [o.reshape(shp) for o in out]

    dmod_all = small_parts[:, 0:DEPTH * 6, :].reshape(N_DEV, DEPTH, 6 * D)
    dmod_cols = lax.dynamic_slice_in_dim(dmod_all, me * sw_mod, sw_mod, axis=2).transpose(1, 0, 2)
    pad = HEAD_DIM - N_DEV
    dmod_pad = jnp.concatenate([dmod_cols, jnp.zeros((DEPTH, pad, sw_mod), F32)], axis=1).astype(MXU)
    cs_t = jnp.concatenate([cs_all.T, jnp.zeros((D, pad), F32)], axis=1).astype(MXU)
    g_mod_w = _mod_weight_grad("mod_weight_grad", cs_t, dmod_pad)
    results["mod_w"] = [o.reshape(mod_w.shape) for o in _adamw(
        "adamw_mod", g_mod_w.reshape(1, -1, sw_mod), mod_w.reshape(-1, sw_mod), m_mod_w.reshape(-1, sw_mod), v_mod_w.reshape(-1, sw_mod))]

    update_layer(0, landed[0])

    order = ["mod_w", "mod_b", "norm_mix", "norm_mlp", "norm_final", "rg_w_in", "rg_conv_w", "rg_conv_b", "rg_w_a", "rg_b_a", "rg_w_x",
             "rg_b_x", "rg_lambda", "rg_w_out", "hg_w_in", "hg_lower_bounds", "hg_gnorm", "hg_w_out", "mlp_w1", "mlp_w2"]
    return (loss, dx[None], *[results[n][0] for n in order], *[results[n][1] for n in order],
            *[results[n][2] for n in order], *[results[n][3] for n in order])
```
